```python
import math
import jax, jax.numpy as jnp
from jax import lax
import numpy as np

D_MODEL = 2048
BATCH = 32
SEQ = 256
DEPTH = 2
DEC_BATCH = 8
DEC_SEQ = 4096
PAST_LEN = 256

GRID_W = 64
N_EVEN = (DEPTH + 1) // 2
N_ODD = DEPTH // 2
MIX_WIDTH = D_MODEL // 2
HEAD_DIM = 128
NA_HEADS = MIX_WIDTH // HEAD_DIM
NA_KH = 8
NA_KW = 16
HY_WIDTH = MIX_WIDTH
HY_ORDER = 2
HY_POS_EMB = 33
HY_HIDDEN = 64
S5_WIDTH = MIX_WIDTH
S5_GROUP = 16
S5_GROUPS = S5_WIDTH // S5_GROUP
S5_STATE = 64
DIFF_HEADS = MIX_WIDTH // HEAD_DIM
DIFF_QK_DIM = HEAD_DIM // 2
DIFF_V_DIM = HEAD_DIM
ROPE_BASE = 10000.0
PEER_HEADS = 8
PEER_N_KEYS = 128
PEER_N_EXPERTS = PEER_N_KEYS * PEER_N_KEYS
PEER_QUERY_DIM = 256
PEER_TOPK = 16
Q_BLOCK = 128
TOKEN_BLOCK = 128
NORM_EPS = 1e-6
NEG_INF = -1e30
EVEN_IN = 3 * MIX_WIDTH + (HY_ORDER + 1) * HY_WIDTH
ODD_IN = S5_WIDTH + DIFF_HEADS * (4 * DIFF_QK_DIM + DIFF_V_DIM)

kernel_name = "hybrid_diffusion_prefix_step"


def rmsnorm(x, g):
    xf = x.astype(jnp.float32)
    y = xf * lax.rsqrt(jnp.mean(xf * xf, axis=-1, keepdims=True) + NORM_EPS)
    return (y * g.astype(jnp.float32)).astype(x.dtype)


def adaln(cond, w, b):
    m = jax.nn.silu(cond) @ w + b
    return jnp.split(m[:, None, :], 6, axis=-1)


def modulate(h, shift, scale):
    return h * (1.0 + scale) + shift


def softmax_f32(s):
    return jax.nn.softmax(s.astype(jnp.float32), axis=-1)


def sweep_query_blocks(fn, q):
    b, l = q.shape[:2]
    qb = jnp.moveaxis(q.reshape((b, l // Q_BLOCK, Q_BLOCK) + q.shape[2:]), 1, 0)
    out = jnp.moveaxis(lax.map(fn, qb), 0, 1)
    return out.reshape((b, l) + out.shape[3:])


def grid_rope(l):
    t = jnp.arange(l)
    row = (t // GRID_W).astype(jnp.float32)
    col = (t % GRID_W).astype(jnp.float32)
    nf = DIFF_QK_DIM // 4
    inv = ROPE_BASE ** (-jnp.arange(nf, dtype=jnp.float32) / nf)
    ang = jnp.stack([row[:, None] * inv, col[:, None] * inv], axis=1)
    ang = jnp.stack([ang, ang], axis=2).reshape(l, DIFF_QK_DIM)
    return jnp.cos(ang), jnp.sin(ang)


def apply_rope(x, cos, sin):
    xr = x.reshape(x.shape[:-1] + (2, 2, DIFF_QK_DIM // 4))
    rot = jnp.stack([-xr[..., 1, :], xr[..., 0, :]], axis=-2).reshape(x.shape)
    return (x * cos + rot * sin).astype(x.dtype)


def context_attention(q, k, v):
    scale = q.shape[-1] ** -0.5

    def blk(qb):
        p = softmax_f32(jnp.einsum('bqhd,bkhd->bhqk', qb, k) * scale).astype(v.dtype)
        return jnp.einsum('bhqk,bkhd->bqhd', p, v)
    return sweep_query_blocks(blk, q)


def neighborhood_attention(q, k, v, ctx_k, ctx_v, rpb):
    b, l, h, d = q.shape
    rows = l // GRID_W
    kh = min(NA_KH, rows)
    n_lat = kh * GRID_W
    scale = d ** -0.5
    qg = jnp.moveaxis(q.reshape(b, rows, GRID_W, h, d), 1, 0)
    kg = k.reshape(b, rows, GRID_W, h, d)
    vg = v.reshape(b, rows, GRID_W, h, d)
    r_idx = jnp.arange(rows)
    row_start = jnp.clip(r_idx - kh // 2, 0, rows - kh)
    cols = jnp.arange(GRID_W)
    col_start = jnp.clip(cols - NA_KW // 2, 0, GRID_W - NA_KW)
    col_mask = (cols[None, :] >= col_start[:, None]) & (cols[None, :] < col_start[:, None] + NA_KW)
    mask = jnp.broadcast_to(col_mask[:, None, :], (GRID_W, kh, GRID_W)).reshape(GRID_W, n_lat)
    col_bias_idx = jnp.clip(cols[None, :] - cols[:, None] + NA_KW - 1, 0, 2 * NA_KW - 2)

    def row_fn(args):
        q_r, r0, r = args
        k_blk = lax.dynamic_slice_in_dim(kg, r0, kh, axis=1).reshape(b, n_lat, h, d)
        v_blk = lax.dynamic_slice_in_dim(vg, r0, kh, axis=1).reshape(b, n_lat, h, d)
        row_off = r0 + jnp.arange(kh) - r + NA_KH - 1
        bias = rpb[:, row_off][:, :, col_bias_idx]
        bias = jnp.transpose(bias, (0, 2, 1, 3)).reshape(h, GRID_W, n_lat).astype(jnp.float32)
        s_lat = jnp.einsum('bqhd,bkhd->bhqk', q_r, k_blk).astype(jnp.float32) * scale + bias
        s_lat = jnp.where(mask, s_lat, NEG_INF)
        s_ctx = jnp.einsum('bqhd,bkhd->bhqk', q_r, ctx_k).astype(jnp.float32) * scale
        p = jax.nn.softmax(jnp.concatenate([s_lat, s_ctx], axis=-1), axis=-1).astype(v.dtype)
        return (jnp.einsum('bhqk,bkhd->bqhd', p[..., :n_lat], v_blk)
                + jnp.einsum('bhqk,bkhd->bqhd', p[..., n_lat:], ctx_v))
    out = lax.map(row_fn, (qg, row_start, r_idx))
    return jnp.moveaxis(out, 0, 1).reshape(b, l, h, d)


def hyena_filters(l, w1, b1, f1, w2, b2, f2, w3, decay):
    t = jnp.linspace(0.0, 1.0, l, dtype=jnp.float32)[:, None]
    bands = (HY_POS_EMB - 1) // 2
    w_ang = 2.0 * math.pi * jnp.arange(l, dtype=jnp.float32)[:, None] / l
    freqs = jnp.linspace(1e-4, bands - 1, bands, dtype=jnp.float32)[None, :]
    z = jnp.concatenate([t, jnp.cos(freqs * w_ang), -jnp.sin(freqs * w_ang)], axis=-1)
    h = jnp.sin(f1 * (z @ w1 + b1))
    h = jnp.sin(f2 * (h @ w2 + b2))
    h = (h @ w3).reshape(l, 2, HY_ORDER, HY_WIDTH).astype(jnp.float32)
    h = h * jnp.exp(-t.reshape(l, 1, 1, 1) * jnp.abs(decay.astype(jnp.float32)))
    h_f, h_b = h[:, 0], h[:, 1]
    zero = jnp.zeros((1, HY_ORDER, HY_WIDTH), jnp.float32)
    return jnp.concatenate([h_f, zero, h_b[1:][::-1]], axis=0)


def hyena(u, conv_w, conv_b, filt, bias):
    b, l, _ = u.shape
    up = jnp.pad(u, ((0, 0), (1, 1), (0, 0)))
    u = up[:, :-2] * conv_w[0] + up[:, 1:-1] * conv_w[1] + up[:, 2:] * conv_w[2] + conv_b
    v, x1, x2 = jnp.split(u, 3, axis=-1)
    kf = jnp.fft.rfft(hyena_filters(l, *filt), axis=0)
    z = v
    for o, gate in enumerate((x1, x2)):
        zf = z.astype(jnp.float32)
        y = jnp.fft.irfft(jnp.fft.rfft(zf, n=2 * l, axis=1) * kf[None, :, o], n=2 * l, axis=1)[:, :l]
        z = gate * (y + zf * bias[o].astype(jnp.float32)).astype(u.dtype)
    return z


def s5_combine(e1, e2):
    a1r, a1i, b1r, b1i = e1
    a2r, a2i, b2r, b2i = e2
    return (a2r * a1r - a2i * a1i, a2r * a1i + a2i * a1r,
            a2r * b1r - a2i * b1i + b2r, a2r * b1i + a2i * b1r + b2i)


def s5_scan(ug, h0_re, h0_im, lam_re, lam_im, log_dt, b_re, b_im, c_re, c_im, reverse):
    f32 = jnp.float32
    lam_re, lam_im = lam_re.astype(f32), lam_im.astype(f32)
    b_re, b_im, c_re, c_im = b_re.astype(f32), b_im.astype(f32), c_re.astype(f32), c_im.astype(f32)
    dt = jnp.exp(log_dt.astype(f32))[:, None]
    mag = jnp.exp(lam_re * dt)
    ar, ai = mag * jnp.cos(lam_im * dt), mag * jnp.sin(lam_im * dt)
    den = lam_re * lam_re + lam_im * lam_im
    cr = ((ar - 1.0) * lam_re + ai * lam_im) / den
    ci = (ai * lam_re - (ar - 1.0) * lam_im) / den
    bbr = cr[..., None] * b_re - ci[..., None] * b_im
    bbi = cr[..., None] * b_im + ci[..., None] * b_re
    if reverse:
        ug = jnp.flip(ug, axis=1)

    def one(args):
        ub, hr0, hi0 = args
        bur = jnp.einsum('gnp,lgp->lgn', bbr, ub)
        bui = jnp.einsum('gnp,lgp->lgn', bbi, ub)
        bur = bur.at[0].add(ar * hr0 - ai * hi0)
        bui = bui.at[0].add(ar * hi0 + ai * hr0)
        a_r = jnp.broadcast_to(ar, bur.shape)
        a_i = jnp.broadcast_to(ai, bur.shape)
        _, _, hr, hi = lax.associative_scan(s5_combine, (a_r, a_i, bur, bui), axis=0)
        y = jnp.einsum('gpn,lgn->lgp', c_re, hr) - jnp.einsum('gpn,lgn->lgp', c_im, hi)
        return y, hr[-1], hi[-1]
    y, fr, fi = lax.map(one, (ug, h0_re.astype(f32), h0_im.astype(f32)))
    if reverse:
        y = jnp.flip(y, axis=1)
    return y, fr, fi


def s5_mixer(u, h0_re, h0_im, lam_re, lam_im, log_dt, b_re, b_im, c_re, c_im, d_skip, glu_w, glu_b):
    b, l, _ = u.shape
    ug = u.astype(jnp.float32).reshape(b, l, S5_GROUPS, S5_GROUP)
    y = d_skip.astype(jnp.float32) * ug
    fin_re, fin_im = [], []
    for d in range(2):
        yd, fr, fi = s5_scan(ug, h0_re[:, d], h0_im[:, d], lam_re[d], lam_im[d], log_dt[d],
                             b_re[d], b_im[d], c_re[d], c_im[d], reverse=(d == 1))
        y = y + yd
        fin_re.append(fr)
        fin_im.append(fi)
    y = jax.nn.gelu(y.reshape(b, l, S5_WIDTH)).astype(u.dtype)
    out = y * jax.nn.sigmoid(y @ glu_w + glu_b)
    return out, jnp.stack(fin_re, axis=1), jnp.stack(fin_im, axis=1)


def diff_attention(q, k, v, lam):
    scale = q.shape[-1] ** -0.5

    def blk(qb):
        p = softmax_f32(jnp.einsum('bqhsd,bkhsd->bshqk', qb, k) * scale)
        w = (p[:, 0] - lam * p[:, 1]).astype(v.dtype)
        return jnp.einsum('bhqk,bkhd->bqhd', w, v)
    return sweep_query_blocks(blk, q)


def even_mixer(h, ev, ctx_kv):
    (w_in, w_out, q_g, k_g, rpb, conv_w, conv_b, w1, b1, f1, w2, b2, f2, w3, decay, bias) = ev
    b, l, _ = h.shape
    p = h @ w_in
    qa, ka, va, hb = jnp.split(p, [MIX_WIDTH, 2 * MIX_WIDTH, 3 * MIX_WIDTH], axis=-1)
    shp = (b, l, NA_HEADS, HEAD_DIM)
    qa = rmsnorm(qa.reshape(shp), q_g)
    ka = rmsnorm(ka.reshape(shp), k_g)
    va = va.reshape(shp)
    if ctx_kv is None:
        oa = context_attention(qa, ka, va)
    else:
        oa = neighborhood_attention(qa, ka, va, ctx_kv[0], ctx_kv[1], rpb)
    ob = hyena(hb, conv_w, conv_b, (w1, b1, f1, w2, b2, f2, w3, decay), bias)
    out = jnp.concatenate([oa.reshape(b, l, MIX_WIDTH), ob], axis=-1) @ w_out
    return out, ka, va


def odd_mixer(h, od, lam_init, h0_re, h0_im, ctx_kv, rope):
    (w_in, w_out, lam_re, lam_im, log_dt, b_re, b_im, c_re, c_im, d_skip, glu_w, glu_b,
     q_g, k_g, lq1, lk1, lq2, lk2, subln_g) = od
    b, l, _ = h.shape
    qk = DIFF_HEADS * 2 * DIFF_QK_DIM
    p = h @ w_in
    u, qd, kd, vd = jnp.split(p, [S5_WIDTH, S5_WIDTH + qk, S5_WIDTH + 2 * qk], axis=-1)
    oc, fr, fi = s5_mixer(u, h0_re, h0_im, lam_re, lam_im, log_dt, b_re, b_im, c_re, c_im,
                          d_skip, glu_w, glu_b)
    shp = (b, l, DIFF_HEADS, 2, DIFF_QK_DIM)
    q = rmsnorm(qd.reshape(shp), q_g)
    k = rmsnorm(kd.reshape(shp), k_g)
    v = vd.reshape(b, l, DIFF_HEADS, DIFF_V_DIM)
    lam = (jnp.exp(jnp.sum(lq1 * lk1).astype(jnp.float32))
           - jnp.exp(jnp.sum(lq2 * lk2).astype(jnp.float32)) + lam_init)
    if ctx_kv is None:
        keys, vals = k, v
    else:
        cos, sin = rope
        cos = cos[None, :, None, None, :]
        sin = sin[None, :, None, None, :]
        q = apply_rope(q, cos, sin)
        keys = jnp.concatenate([apply_rope(k, cos, sin).astype(ctx_kv[0].dtype), ctx_kv[0]], axis=1)
        vals = jnp.concatenate([v.astype(ctx_kv[1].dtype), ctx_kv[1]], axis=1)
    o = diff_attention(q, keys, vals, lam)
    o = rmsnorm(o, subln_g) * (1.0 - lam_init)
    out = jnp.concatenate([oc, o.reshape(b, l, DIFF_HEADS * DIFF_V_DIM).astype(oc.dtype)], axis=-1) @ w_out
    return out, k, v, fr, fi


def peer(x, w_q, sub_keys, u_tab, v_tab):
    b, l, d = x.shape
    xt = x.reshape(-1, TOKEN_BLOCK, d)

    def blk(xb):
        q = (xb @ w_q).reshape(TOKEN_BLOCK, PEER_HEADS, 2, PEER_QUERY_DIM // 2)
        s = jnp.einsum('thsk,hsnk->thsn', q, sub_keys).astype(jnp.float32)
        sv, si = lax.top_k(s, PEER_TOPK)
        cand = (sv[:, :, 0, :, None] + sv[:, :, 1, None, :]).reshape(TOKEN_BLOCK, PEER_HEADS, -1)
        cidx = (si[:, :, 0, :, None] * PEER_N_KEYS + si[:, :, 1, None, :]).reshape(TOKEN_BLOCK, PEER_HEADS, -1)
        fv, fi = lax.top_k(cand, PEER_TOPK)
        eidx = jnp.take_along_axis(cidx, fi, axis=-1)
        g = jax.nn.softmax(fv, axis=-1).astype(xb.dtype)
        a = jax.nn.gelu(jnp.einsum('thkd,td->thk', u_tab[eidx], xb))
        return jnp.einsum('thk,thkd->td', g * a, v_tab[eidx])
    return lax.map(blk, xt).reshape(b, l, d)


def setup_inputs(seed: int = 0) -> dict:
    key = jax.random.key(seed)
    keys = iter(jax.random.split(key, 96))
    f32 = jnp.float32

    def nrm(shape, scale):
        return jax.random.normal(next(keys), shape, f32) * scale

    def gain(shape):
        return 1.0 + nrm(shape, 0.02)
    D = D_MODEL
    G, N, P = S5_GROUPS, S5_STATE, S5_GROUP
    C = HY_WIDTH
    dk, dv = DIFF_QK_DIM, DIFF_V_DIM
    decay_base = jnp.abs(jnp.linspace(math.log(1e-2) / 1.5, math.log(1e-2) / 0.3, C, dtype=f32))
    return {
        "x_prompt": nrm((BATCH, SEQ, D), 1.0),
        "x_sample": nrm((DEC_BATCH, DEC_SEQ, D), 1.0),
        "c": nrm((DEC_BATCH, D), 1.0),
        "cache_na_k": nrm((DEC_BATCH, N_EVEN, PAST_LEN, NA_HEADS, HEAD_DIM), 1.0),
        "cache_na_v": nrm((DEC_BATCH, N_EVEN, PAST_LEN, NA_HEADS, HEAD_DIM), 1.0),
        "cache_diff_k": nrm((DEC_BATCH, N_ODD, PAST_LEN, DIFF_HEADS, 2, dk), 1.0),
        "cache_diff_v": nrm((DEC_BATCH, N_ODD, PAST_LEN, DIFF_HEADS, dv), 1.0),
        "state_s5_re": nrm((DEC_BATCH, N_ODD, 2, G, N), 0.5),
        "state_s5_im": nrm((DEC_BATCH, N_ODD, 2, G, N), 0.5),
        "c_ctx": nrm((D,), 1.0),
        "mod_w": nrm((DEPTH, D, 6 * D), 0.5 * D ** -0.5),
        "mod_b": nrm((DEPTH, 6 * D), 0.02),
        "norm_mix_g": gain((DEPTH, D)),
        "norm_ffn_g": gain((DEPTH, D)),
        "ev_w_in": nrm((N_EVEN, D, EVEN_IN), D ** -0.5),
        "ev_w_out": nrm((N_EVEN, D, D), D ** -0.5),
        "na_q_g": gain((N_EVEN, HEAD_DIM)),
        "na_k_g": gain((N_EVEN, HEAD_DIM)),
        "na_rpb": nrm((N_EVEN, NA_HEADS, 2 * NA_KH - 1, 2 * NA_KW - 1), 0.1),
        "hy_conv_w": nrm((N_EVEN, 3, 3 * C), 3 ** -0.5),
        "hy_conv_b": nrm((N_EVEN, 3 * C), 0.02),
        "hy_w1": nrm((N_EVEN, HY_POS_EMB, HY_HIDDEN), HY_POS_EMB ** -0.5),
        "hy_b1": nrm((N_EVEN, HY_HIDDEN), 0.1),
        "hy_f1": 1.0 + nrm((N_EVEN, HY_HIDDEN), 0.05),
        "hy_w2": nrm((N_EVEN, HY_HIDDEN, HY_HIDDEN), HY_HIDDEN ** -0.5),
        "hy_b2": nrm((N_EVEN, HY_HIDDEN), 0.1),
        "hy_f2": 1.0 + nrm((N_EVEN, HY_HIDDEN), 0.05),
        "hy_w3": nrm((N_EVEN, HY_HIDDEN, 2 * HY_ORDER * C), 0.05 * HY_HIDDEN ** -0.5),
        "hy_decay": decay_base + nrm((N_EVEN, 2, HY_ORDER, C), 0.05),
        "hy_bias": nrm((N_EVEN, HY_ORDER, C), 0.5),
        "od_w_in": nrm((N_ODD, D, ODD_IN), D ** -0.5),
        "od_w_out": nrm((N_ODD, D, D), D ** -0.5),
        "s5_lam_re": -0.5 + nrm((N_ODD, 2, G, N), 0.01),
        "s5_lam_im": math.pi * jnp.arange(N, dtype=f32) + nrm((N_ODD, 2, G, N), 0.01),
        "s5_log_dt": jax.random.uniform(next(keys), (N_ODD, 2, G), f32, math.log(1e-3), math.log(1e-1)),
        "s5_b_re": nrm((N_ODD, 2, G, N, P), (2 * P) ** -0.5),
        "s5_b_im": nrm((N_ODD, 2, G, N, P), (2 * P) ** -0.5),
        "s5_c_re": nrm((N_ODD, 2, G, P, N), (2 * N) ** -0.5),
        "s5_c_im": nrm((N_ODD, 2, G, P, N), (2 * N) ** -0.5),
        "s5_d": nrm((N_ODD, G, P), 0.5),
        "s5_glu_w": nrm((N_ODD, S5_WIDTH, S5_WIDTH), S5_WIDTH ** -0.5),
        "s5_glu_b": nrm((N_ODD, S5_WIDTH), 0.02),
        "diff_q_g": gain((N_ODD, dk)),
        "diff_k_g": gain((N_ODD, dk)),
        "diff_lq1": nrm((N_ODD, dk), 0.1),
        "diff_lk1": nrm((N_ODD, dk), 0.1),
        "diff_lq2": nrm((N_ODD, dk), 0.1),
        "diff_lk2": nrm((N_ODD, dk), 0.1),
        "diff_subln_g": gain((N_ODD, dv)),
        "peer_w_q": nrm((DEPTH, D, PEER_HEADS * PEER_QUERY_DIM), D ** -0.5),
        "peer_keys": nrm((DEPTH, PEER_HEADS, 2, PEER_N_KEYS, PEER_QUERY_DIM // 2), (PEER_QUERY_DIM // 2) ** -0.5),
        "peer_u": nrm((DEPTH, PEER_N_EXPERTS, D), D ** -0.5),
        "peer_v": nrm((DEPTH, PEER_N_EXPERTS, D), (PEER_HEADS * PEER_TOPK) ** -0.5),
    }


def reference(x_prompt, x_sample, c, cache_na_k, cache_na_v, cache_diff_k, cache_diff_v,
              state_s5_re, state_s5_im, c_ctx, mod_w, mod_b, norm_mix_g, norm_ffn_g,
              ev_w_in, ev_w_out, na_q_g, na_k_g, na_rpb, hy_conv_w, hy_conv_b,
              hy_w1, hy_b1, hy_f1, hy_w2, hy_b2, hy_f2, hy_w3, hy_decay, hy_bias,
              od_w_in, od_w_out, s5_lam_re, s5_lam_im, s5_log_dt, s5_b_re, s5_b_im,
              s5_c_re, s5_c_im, s5_d, s5_glu_w, s5_glu_b, diff_q_g, diff_k_g,
              diff_lq1, diff_lk1, diff_lq2, diff_lk2, diff_subln_g,
              peer_w_q, peer_keys, peer_u, peer_v):
    rope = grid_rope(x_sample.shape[1])
    xp, xs = x_prompt, x_sample
    new_na_k, new_na_v, new_dk, new_dv, new_sr, new_si = [], [], [], [], [], []
    for i in range(DEPTH):
        mp = adaln(c_ctx[None, :], mod_w[i], mod_b[i])
        ms = adaln(c, mod_w[i], mod_b[i])
        hp = modulate(rmsnorm(xp, norm_mix_g[i]), mp[0], mp[1])
        hs = modulate(rmsnorm(xs, norm_mix_g[i]), ms[0], ms[1])
        j = i // 2
        if i % 2 == 0:
            ev = (ev_w_in[j], ev_w_out[j], na_q_g[j], na_k_g[j], na_rpb[j], hy_conv_w[j], hy_conv_b[j],
                  hy_w1[j], hy_b1[j], hy_f1[j], hy_w2[j], hy_b2[j], hy_f2[j], hy_w3[j], hy_decay[j], hy_bias[j])
            op, kp, vp = even_mixer(hp, ev, None)
            os_, _, _ = even_mixer(hs, ev, (cache_na_k[:, j], cache_na_v[:, j]))
            new_na_k.append(kp)
            new_na_v.append(vp)
        else:
            lam_init = 0.8 - 0.6 * math.exp(-0.3 * i)
            od = (od_w_in[j], od_w_out[j], s5_lam_re[j], s5_lam_im[j], s5_log_dt[j], s5_b_re[j], s5_b_im[j],
                  s5_c_re[j], s5_c_im[j], s5_d[j], s5_glu_w[j], s5_glu_b[j], diff_q_g[j], diff_k_g[j],
                  diff_lq1[j], diff_lk1[j], diff_lq2[j], diff_lk2[j], diff_subln_g[j])
            zeros = jnp.zeros((xp.shape[0], 2, S5_GROUPS, S5_STATE), jnp.float32)
            op, kp, vp, sr, si = odd_mixer(hp, od, lam_init, zeros, zeros, None, None)
            os_, _, _, _, _ = odd_mixer(hs, od, lam_init, state_s5_re[:, j], state_s5_im[:, j],
                                        (cache_diff_k[:, j], cache_diff_v[:, j]), rope)
            new_dk.append(kp)
            new_dv.append(vp)
            new_sr.append(sr)
            new_si.append(si)
        xp = xp + mp[2] * op
        xs = xs + ms[2] * os_
        xp = xp + mp[5] * peer(modulate(rmsnorm(xp, norm_ffn_g[i]), mp[3], mp[4]),
                               peer_w_q[i], peer_keys[i], peer_u[i], peer_v[i])
        xs = xs + ms[5] * peer(modulate(rmsnorm(xs, norm_ffn_g[i]), ms[3], ms[4]),
                               peer_w_q[i], peer_keys[i], peer_u[i], peer_v[i])
    return (xp, xs, jnp.stack(new_na_k, axis=1), jnp.stack(new_na_v, axis=1),
            jnp.stack(new_dk, axis=1), jnp.stack(new_dv, axis=1),
            jnp.stack(new_sr, axis=1), jnp.stack(new_si, axis=1))
```

```python
import functools
import math

import jax
import jax.numpy as jnp
from jax import lax
from jax.experimental import pallas as pl
from jax.experimental.pallas import tpu as pltpu

D_MODEL = 2048
DEPTH = 2
GRID_W = 64
MIX_WIDTH = D_MODEL // 2
HEAD_DIM = 128
NA_HEADS = MIX_WIDTH // HEAD_DIM
NA_KH = 8
NA_KW = 16
HY_WIDTH = MIX_WIDTH
HY_ORDER = 2
HY_POS_EMB = 33
S5_WIDTH = MIX_WIDTH
S5_GROUP = 16
S5_GROUPS = S5_WIDTH // S5_GROUP
S5_STATE = 64
DIFF_HEADS = MIX_WIDTH // HEAD_DIM
DIFF_QK_DIM = HEAD_DIM // 2
DIFF_V_DIM = HEAD_DIM
ROPE_BASE = 10000.0
PEER_HEADS = 8
PEER_N_KEYS = 128
PEER_QUERY_DIM = 256
PEER_TOPK = 16
Q_BLOCK = 128
TOKEN_BLOCK = 128
NORM_EPS = 1e-6
NEG_INF = -1e30

VMEM_LIMIT_BYTES = 48 * 1024 * 1024


def _mm_kernel(x_ref, w_ref, o_ref):
    o_ref[...] = jnp.dot(x_ref[...].astype(jnp.bfloat16), w_ref[...].astype(jnp.bfloat16),
                         preferred_element_type=jnp.float32)


def mm(x, w, tm=1024, tn=512):
    m, k = x.shape
    _, n = w.shape
    tm = min(tm, m)
    tn = min(tn, n)
    assert m % tm == 0 and n % tn == 0
    return pl.pallas_call(
        _mm_kernel,
        grid=(m // tm, n // tn),
        in_specs=[pl.BlockSpec((tm, k), lambda i, j: (i, 0)),
                  pl.BlockSpec((k, tn), lambda i, j: (0, j))],
        out_specs=pl.BlockSpec((tm, tn), lambda i, j: (i, j)),
        out_shape=jax.ShapeDtypeStruct((m, n), jnp.float32),
        compiler_params=pltpu.CompilerParams(
            dimension_semantics=("parallel", "arbitrary"),
            vmem_limit_bytes=VMEM_LIMIT_BYTES),
    )(x, w)


def mm3(x, w):
    b, l, d = x.shape
    return mm(x.reshape(b * l, d), w).reshape(b, l, w.shape[1])


def rmsnorm(x, g):
    xf = x.astype(jnp.float32)
    y = xf * lax.rsqrt(jnp.mean(xf * xf, axis=-1, keepdims=True) + NORM_EPS)
    return (y * g.astype(jnp.float32)).astype(x.dtype)


def adaln(cond, w, b):
    m = jax.nn.silu(cond) @ w + b
    return jnp.split(m[:, None, :], 6, axis=-1)


def modulate(h, shift, scale):
    return h * (1.0 + scale) + shift


def softmax_f32(s):
    return jax.nn.softmax(s.astype(jnp.float32), axis=-1)


def sweep_query_blocks(fn, q):
    b, l = q.shape[:2]
    qb = jnp.moveaxis(q.reshape((b, l // Q_BLOCK, Q_BLOCK) + q.shape[2:]), 1, 0)
    out = jnp.moveaxis(lax.map(fn, qb), 0, 1)
    return out.reshape((b, l) + out.shape[3:])


def grid_rope(l):
    t = jnp.arange(l)
    row = (t // GRID_W).astype(jnp.float32)
    col = (t % GRID_W).astype(jnp.float32)
    nf = DIFF_QK_DIM // 4
    inv = ROPE_BASE ** (-jnp.arange(nf, dtype=jnp.float32) / nf)
    ang = jnp.stack([row[:, None] * inv, col[:, None] * inv], axis=1)
    ang = jnp.stack([ang, ang], axis=2).reshape(l, DIFF_QK_DIM)
    return jnp.cos(ang), jnp.sin(ang)


def apply_rope(x, cos, sin):
    xr = x.reshape(x.shape[:-1] + (2, 2, DIFF_QK_DIM // 4))
    rot = jnp.stack([-xr[..., 1, :], xr[..., 0, :]], axis=-2).reshape(x.shape)
    return (x * cos + rot * sin).astype(x.dtype)


def context_attention(q, k, v):
    scale = q.shape[-1] ** -0.5

    def blk(qb):
        p = softmax_f32(jnp.einsum('bqhd,bkhd->bhqk', qb, k) * scale).astype(v.dtype)
        return jnp.einsum('bhqk,bkhd->bqhd', p, v)
    return sweep_query_blocks(blk, q)


def neighborhood_attention(q, k, v, ctx_k, ctx_v, rpb):
    b, l, h, d = q.shape
    rows = l // GRID_W
    kh = min(NA_KH, rows)
    n_lat = kh * GRID_W
    scale = d ** -0.5
    qg = jnp.moveaxis(q.reshape(b, rows, GRID_W, h, d), 1, 0)
    kg = k.reshape(b, rows, GRID_W, h, d)
    vg = v.reshape(b, rows, GRID_W, h, d)
    r_idx = jnp.arange(rows)
    row_start = jnp.clip(r_idx - kh // 2, 0, rows - kh)
    cols = jnp.arange(GRID_W)
    col_start = jnp.clip(cols - NA_KW // 2, 0, GRID_W - NA_KW)
    col_mask = (cols[None, :] >= col_start[:, None]) & (cols[None, :] < col_start[:, None] + NA_KW)
    mask = jnp.broadcast_to(col_mask[:, None, :], (GRID_W, kh, GRID_W)).reshape(GRID_W, n_lat)
    col_bias_idx = jnp.clip(cols[None, :] - cols[:, None] + NA_KW - 1, 0, 2 * NA_KW - 2)

    def row_fn(args):
        q_r, r0, r = args
        k_blk = lax.dynamic_slice_in_dim(kg, r0, kh, axis=1).reshape(b, n_lat, h, d)
        v_blk = lax.dynamic_slice_in_dim(vg, r0, kh, axis=1).reshape(b, n_lat, h, d)
        row_off = r0 + jnp.arange(kh) - r + NA_KH - 1
        bias = rpb[:, row_off][:, :, col_bias_idx]
        bias = jnp.transpose(bias, (0, 2, 1, 3)).reshape(h, GRID_W, n_lat).astype(jnp.float32)
        s_lat = jnp.einsum('bqhd,bkhd->bhqk', q_r, k_blk).astype(jnp.float32) * scale + bias
        s_lat = jnp.where(mask, s_lat, NEG_INF)
        s_ctx = jnp.einsum('bqhd,bkhd->bhqk', q_r, ctx_k).astype(jnp.float32) * scale
        p = jax.nn.softmax(jnp.concatenate([s_lat, s_ctx], axis=-1), axis=-1).astype(v.dtype)
        return (jnp.einsum('bhqk,bkhd->bqhd', p[..., :n_lat], v_blk)
                + jnp.einsum('bhqk,bkhd->bqhd', p[..., n_lat:], ctx_v))
    out = lax.map(row_fn, (qg, row_start, r_idx))
    return jnp.moveaxis(out, 0, 1).reshape(b, l, h, d)


def hyena_filters(l, w1, b1, f1, w2, b2, f2, w3, decay):
    t = jnp.linspace(0.0, 1.0, l, dtype=jnp.float32)[:, None]
    bands = (HY_POS_EMB - 1) // 2
    w_ang = 2.0 * math.pi * jnp.arange(l, dtype=jnp.float32)[:, None] / l
    freqs = jnp.linspace(1e-4, bands - 1, bands, dtype=jnp.float32)[None, :]
    z = jnp.concatenate([t, jnp.cos(freqs * w_ang), -jnp.sin(freqs * w_ang)], axis=-1)
    h = jnp.sin(f1 * (z @ w1 + b1))
    h = jnp.sin(f2 * (h @ w2 + b2))
    h = (h @ w3).reshape(l, 2, HY_ORDER, HY_WIDTH).astype(jnp.float32)
    h = h * jnp.exp(-t.reshape(l, 1, 1, 1) * jnp.abs(decay.astype(jnp.float32)))
    h_f, h_b = h[:, 0], h[:, 1]
    zero = jnp.zeros((1, HY_ORDER, HY_WIDTH), jnp.float32)
    return jnp.concatenate([h_f, zero, h_b[1:][::-1]], axis=0)


def hyena(u, conv_w, conv_b, filt, bias):
    b, l, _ = u.shape
    up = jnp.pad(u, ((0, 0), (1, 1), (0, 0)))
    u = up[:, :-2] * conv_w[0] + up[:, 1:-1] * conv_w[1] + up[:, 2:] * conv_w[2] + conv_b
    v, x1, x2 = jnp.split(u, 3, axis=-1)
    kf = jnp.fft.rfft(hyena_filters(l, *filt), axis=0)
    z = v
    for o, gate in enumerate((x1, x2)):
        zf = z.astype(jnp.float32)
        y = jnp.fft.irfft(jnp.fft.rfft(zf, n=2 * l, axis=1) * kf[None, :, o], n=2 * l, axis=1)[:, :l]
        z = gate * (y + zf * bias[o].astype(jnp.float32)).astype(u.dtype)
    return z


def s5_combine(e1, e2):
    a1r, a1i, b1r, b1i = e1
    a2r, a2i, b2r, b2i = e2
    return (a2r * a1r - a2i * a1i, a2r * a1i + a2i * a1r,
            a2r * b1r - a2i * b1i + b2r, a2r * b1i + a2i * b1r + b2i)


def s5_scan(ug, h0_re, h0_im, lam_re, lam_im, log_dt, b_re, b_im, c_re, c_im, reverse):
    f32 = jnp.float32
    lam_re, lam_im = lam_re.astype(f32), lam_im.astype(f32)
    b_re, b_im, c_re, c_im = b_re.astype(f32), b_im.astype(f32), c_re.astype(f32), c_im.astype(f32)
    dt = jnp.exp(log_dt.astype(f32))[:, None]
    mag = jnp.exp(lam_re * dt)
    ar, ai = mag * jnp.cos(lam_im * dt), mag * jnp.sin(lam_im * dt)
    den = lam_re * lam_re + lam_im * lam_im
    cr = ((ar - 1.0) * lam_re + ai * lam_im) / den
    ci = (ai * lam_re - (ar - 1.0) * lam_im) / den
    bbr = cr[..., None] * b_re - ci[..., None] * b_im
    bbi = cr[..., None] * b_im + ci[..., None] * b_re
    if reverse:
        ug = jnp.flip(ug, axis=1)

    def one(args):
        ub, hr0, hi0 = args
        bur = jnp.einsum('gnp,lgp->lgn', bbr, ub)
        bui = jnp.einsum('gnp,lgp->lgn', bbi, ub)
        bur = bur.at[0].add(ar * hr0 - ai * hi0)
        bui = bui.at[0].add(ar * hi0 + ai * hr0)
        a_r = jnp.broadcast_to(ar, bur.shape)
        a_i = jnp.broadcast_to(ai, bur.shape)
        _, _, hr, hi = lax.associative_scan(s5_combine, (a_r, a_i, bur, bui), axis=0)
        y = jnp.einsum('gpn,lgn->lgp', c_re, hr) - jnp.einsum('gpn,lgn->lgp', c_im, hi)
        return y, hr[-1], hi[-1]
    y, fr, fi = lax.map(one, (ug, h0_re.astype(f32), h0_im.astype(f32)))
    if reverse:
        y = jnp.flip(y, axis=1)
    return y, fr, fi


def s5_mixer(u, h0_re, h0_im, lam_re, lam_im, log_dt, b_re, b_im, c_re, c_im, d_skip, glu_w, glu_b):
    b, l, _ = u.shape
    ug = u.astype(jnp.float32).reshape(b, l, S5_GROUPS, S5_GROUP)
    y = d_skip.astype(jnp.float32) * ug
    fin_re, fin_im = [], []
    for d in range(2):
        yd, fr, fi = s5_scan(ug, h0_re[:, d], h0_im[:, d], lam_re[d], lam_im[d], log_dt[d],
                             b_re[d], b_im[d], c_re[d], c_im[d], reverse=(d == 1))
        y = y + yd
        fin_re.append(fr)
        fin_im.append(fi)
    y = jax.nn.gelu(y.reshape(b, l, S5_WIDTH)).astype(u.dtype)
    out = y * jax.nn.sigmoid(mm3(y, glu_w) + glu_b)
    return out, jnp.stack(fin_re, axis=1), jnp.stack(fin_im, axis=1)


def diff_attention(q, k, v, lam):
    scale = q.shape[-1] ** -0.5

    def blk(qb):
        p = softmax_f32(jnp.einsum('bqhsd,bkhsd->bshqk', qb, k) * scale)
        w = (p[:, 0] - lam * p[:, 1]).astype(v.dtype)
        return jnp.einsum('bhqk,bkhd->bqhd', w, v)
    return sweep_query_blocks(blk, q)


def even_mixer(h, ev, ctx_kv):
    (w_in, w_out, q_g, k_g, rpb, conv_w, conv_b, w1, b1, f1, w2, b2, f2, w3, decay, bias) = ev
    b, l, _ = h.shape
    p = mm3(h, w_in)
    qa, ka, va, hb = jnp.split(p, [MIX_WIDTH, 2 * MIX_WIDTH, 3 * MIX_WIDTH], axis=-1)
    shp = (b, l, NA_HEADS, HEAD_DIM)
    qa = rmsnorm(qa.reshape(shp), q_g)
    ka = rmsnorm(ka.reshape(shp), k_g)
    va = va.reshape(shp)
    if ctx_kv is None:
        oa = context_attention(qa, ka, va)
    else:
        oa = neighborhood_attention(qa, ka, va, ctx_kv[0], ctx_kv[1], rpb)
    ob = hyena(hb, conv_w, conv_b, (w1, b1, f1, w2, b2, f2, w3, decay), bias)
    out = mm3(jnp.concatenate([oa.reshape(b, l, MIX_WIDTH), ob], axis=-1), w_out)
    return out, ka, va


def odd_mixer(h, od, lam_init, h0_re, h0_im, ctx_kv, rope):
    (w_in, w_out, lam_re, lam_im, log_dt, b_re, b_im, c_re, c_im, d_skip, glu_w, glu_b,
     q_g, k_g, lq1, lk1, lq2, lk2, subln_g) = od
    b, l, _ = h.shape
    qk = DIFF_HEADS * 2 * DIFF_QK_DIM
    p = mm3(h, w_in)
    u, qd, kd, vd = jnp.split(p, [S5_WIDTH, S5_WIDTH + qk, S5_WIDTH + 2 * qk], axis=-1)
    oc, fr, fi = s5_mixer(u, h0_re, h0_im, lam_re, lam_im, log_dt, b_re, b_im, c_re, c_im,
                          d_skip, glu_w, glu_b)
    shp = (b, l, DIFF_HEADS, 2, DIFF_QK_DIM)
    q = rmsnorm(qd.reshape(shp), q_g)
    k = rmsnorm(kd.reshape(shp), k_g)
    v = vd.reshape(b, l, DIFF_HEADS, DIFF_V_DIM)
    lam = (jnp.exp(jnp.sum(lq1 * lk1).astype(jnp.float32))
           - jnp.exp(jnp.sum(lq2 * lk2).astype(jnp.float32)) + lam_init)
    if ctx_kv is None:
        keys, vals = k, v
    else:
        cos, sin = rope
        cos = cos[None, :, None, None, :]
        sin = sin[None, :, None, None, :]
        q = apply_rope(q, cos, sin)
        keys = jnp.concatenate([apply_rope(k, cos, sin).astype(ctx_kv[0].dtype), ctx_kv[0]], axis=1)
        vals = jnp.concatenate([v.astype(ctx_kv[1].dtype), ctx_kv[1]], axis=1)
    o = diff_attention(q, keys, vals, lam)
    o = rmsnorm(o, subln_g) * (1.0 - lam_init)
    out = mm3(jnp.concatenate([oc, o.reshape(b, l, DIFF_HEADS * DIFF_V_DIM).astype(oc.dtype)], axis=-1), w_out)
    return out, k, v, fr, fi


def peer(x, w_q, sub_keys, u_tab, v_tab):
    b, l, d = x.shape
    xt = x.reshape(-1, TOKEN_BLOCK, d)
    qt = mm(x.reshape(-1, d), w_q).reshape(-1, TOKEN_BLOCK, PEER_HEADS * PEER_QUERY_DIM)

    def blk(args):
        xb, qb = args
        q = qb.reshape(TOKEN_BLOCK, PEER_HEADS, 2, PEER_QUERY_DIM // 2)
        s = jnp.einsum('thsk,hsnk->thsn', q, sub_keys).astype(jnp.float32)
        sv, si = lax.top_k(s, PEER_TOPK)
        cand = (sv[:, :, 0, :, None] + sv[:, :, 1, None, :]).reshape(TOKEN_BLOCK, PEER_HEADS, -1)
        cidx = (si[:, :, 0, :, None] * PEER_N_KEYS + si[:, :, 1, None, :]).reshape(TOKEN_BLOCK, PEER_HEADS, -1)
        fv, fi = lax.top_k(cand, PEER_TOPK)
        eidx = jnp.take_along_axis(cidx, fi, axis=-1)
        g = jax.nn.softmax(fv, axis=-1).astype(xb.dtype)
        a = jax.nn.gelu(jnp.einsum('thkd,td->thk', u_tab[eidx], xb))
        return jnp.einsum('thk,thkd->td', g * a, v_tab[eidx])
    return lax.map(blk, (xt, qt)).reshape(b, l, d)


def kernel(x_prompt, x_sample, c, cache_na_k, cache_na_v, cache_diff_k, cache_diff_v, state_s5_re, state_s5_im, c_ctx, mod_w, mod_b, norm_mix_g, norm_ffn_g, ev_w_in, ev_w_out, na_q_g, na_k_g, na_rpb, hy_conv_w, hy_conv_b, hy_w1, hy_b1, hy_f1, hy_w2, hy_b2, hy_f2, hy_w3, hy_decay, hy_bias, od_w_in, od_w_out, s5_lam_re, s5_lam_im, s5_log_dt, s5_b_re, s5_b_im, s5_c_re, s5_c_im, s5_d, s5_glu_w, s5_glu_b, diff_q_g, diff_k_g, diff_lq1, diff_lk1, diff_lq2, diff_lk2, diff_subln_g, peer_w_q, peer_keys, peer_u, peer_v):
    rope = grid_rope(x_sample.shape[1])
    xp, xs = x_prompt, x_sample
    new_na_k, new_na_v, new_dk, new_dv, new_sr, new_si = [], [], [], [], [], []
    for i in range(DEPTH):
        mp = adaln(c_ctx[None, :], mod_w[i], mod_b[i])
        ms = adaln(c, mod_w[i], mod_b[i])
        hp = modulate(rmsnorm(xp, norm_mix_g[i]), mp[0], mp[1])
        hs = modulate(rmsnorm(xs, norm_mix_g[i]), ms[0], ms[1])
        j = i // 2
        if i % 2 == 0:
            ev = (ev_w_in[j], ev_w_out[j], na_q_g[j], na_k_g[j], na_rpb[j], hy_conv_w[j], hy_conv_b[j],
                  hy_w1[j], hy_b1[j], hy_f1[j], hy_w2[j], hy_b2[j], hy_f2[j], hy_w3[j], hy_decay[j], hy_bias[j])
            op, kp, vp = even_mixer(hp, ev, None)
            os_, _, _ = even_mixer(hs, ev, (cache_na_k[:, j], cache_na_v[:, j]))
            new_na_k.append(kp)
            new_na_v.append(vp)
        else:
            lam_init = 0.8 - 0.6 * math.exp(-0.3 * i)
            od = (od_w_in[j], od_w_out[j], s5_lam_re[j], s5_lam_im[j], s5_log_dt[j], s5_b_re[j], s5_b_im[j],
                  s5_c_re[j], s5_c_im[j], s5_d[j], s5_glu_w[j], s5_glu_b[j], diff_q_g[j], diff_k_g[j],
                  diff_lq1[j], diff_lk1[j], diff_lq2[j], diff_lk2[j], diff_subln_g[j])
            zeros = jnp.zeros((xp.shape[0], 2, S5_GROUPS, S5_STATE), jnp.float32)
            op, kp, vp, sr, si = odd_mixer(hp, od, lam_init, zeros, zeros, None, None)
            os_, _, _, _, _ = odd_mixer(hs, od, lam_init, state_s5_re[:, j], state_s5_im[:, j],
                                        (cache_diff_k[:, j], cache_diff_v[:, j]), rope)
            new_dk.append(kp)
            new_dv.append(vp)
            new_sr.append(sr)
            new_si.append(si)
        xp = xp + mp[2] * op
        xs = xs + ms[2] * os_
        xp = xp + mp[5] * peer(modulate(rmsnorm(xp, norm_ffn_g[i]), mp[3], mp[4]),
                               peer_w_q[i], peer_keys[i], peer_u[i], peer_v[i])
        xs = xs + ms[5] * peer(modulate(rmsnorm(xs, norm_ffn_g[i]), ms[3], ms[4]),
                               peer_w_q[i], peer_keys[i], peer_u[i], peer_v[i])
    return (xp, xs, jnp.stack(new_na_k, axis=1), jnp.stack(new_na_v, axis=1),
            jnp.stack(new_dk, axis=1), jnp.stack(new_dv, axis=1),
            jnp.stack(new_sr, axis=1), jnp.stack(new_si, axis=1))
```

```python
import functools
import math

import jax
import jax.numpy as jnp
from jax import lax
from jax.experimental import pallas as pl
from jax.experimental.pallas import tpu as pltpu

D_MODEL = 2048
DEPTH = 2
GRID_W = 64
MIX_WIDTH = D_MODEL // 2
HEAD_DIM = 128
NA_HEADS = MIX_WIDTH // HEAD_DIM
NA_KH = 8
NA_KW = 16
HY_WIDTH = MIX_WIDTH
HY_ORDER = 2
HY_POS_EMB = 33
S5_WIDTH = MIX_WIDTH
S5_GROUP = 16
S5_GROUPS = S5_WIDTH // S5_GROUP
S5_STATE = 64
DIFF_HEADS = MIX_WIDTH // HEAD_DIM
DIFF_QK_DIM = HEAD_DIM // 2
DIFF_V_DIM = HEAD_DIM
ROPE_BASE = 10000.0
PEER_HEADS = 8
PEER_N_KEYS = 128
PEER_QUERY_DIM = 256
PEER_TOPK = 16
Q_BLOCK = 128
TOKEN_BLOCK = 128
NORM_EPS = 1e-6
NEG_INF = -1e30

VMEM_LIMIT_BYTES = 56 * 1024 * 1024


def _mm_kernel(x_ref, w_ref, o_ref):
    o_ref[...] = jnp.dot(x_ref[...].astype(jnp.bfloat16), w_ref[...].astype(jnp.bfloat16),
                         preferred_element_type=jnp.float32)


def mm(x, w, tm=1024, tn=512):
    m, k = x.shape
    _, n = w.shape
    tm = min(tm, m)
    tn = min(tn, n)
    assert m % tm == 0 and n % tn == 0
    return pl.pallas_call(
        _mm_kernel,
        grid=(m // tm, n // tn),
        in_specs=[pl.BlockSpec((tm, k), lambda i, j: (i, 0)),
                  pl.BlockSpec((k, tn), lambda i, j: (0, j))],
        out_specs=pl.BlockSpec((tm, tn), lambda i, j: (i, j)),
        out_shape=jax.ShapeDtypeStruct((m, n), jnp.float32),
        compiler_params=pltpu.CompilerParams(
            dimension_semantics=("parallel", "arbitrary"),
            vmem_limit_bytes=VMEM_LIMIT_BYTES),
    )(x, w)


def mm3(x, w):
    b, l, d = x.shape
    return mm(x.reshape(b * l, d), w).reshape(b, l, w.shape[1])


def rmsnorm(x, g):
    xf = x.astype(jnp.float32)
    y = xf * lax.rsqrt(jnp.mean(xf * xf, axis=-1, keepdims=True) + NORM_EPS)
    return (y * g.astype(jnp.float32)).astype(x.dtype)


def adaln(cond, w, b):
    m = jax.nn.silu(cond) @ w + b
    return jnp.split(m[:, None, :], 6, axis=-1)


def modulate(h, shift, scale):
    return h * (1.0 + scale) + shift


def softmax_f32(s):
    return jax.nn.softmax(s.astype(jnp.float32), axis=-1)


def sweep_query_blocks(fn, q):
    b, l = q.shape[:2]
    qb = jnp.moveaxis(q.reshape((b, l // Q_BLOCK, Q_BLOCK) + q.shape[2:]), 1, 0)
    out = jnp.moveaxis(lax.map(fn, qb), 0, 1)
    return out.reshape((b, l) + out.shape[3:])


def grid_rope(l):
    t = jnp.arange(l)
    row = (t // GRID_W).astype(jnp.float32)
    col = (t % GRID_W).astype(jnp.float32)
    nf = DIFF_QK_DIM // 4
    inv = ROPE_BASE ** (-jnp.arange(nf, dtype=jnp.float32) / nf)
    ang = jnp.stack([row[:, None] * inv, col[:, None] * inv], axis=1)
    ang = jnp.stack([ang, ang], axis=2).reshape(l, DIFF_QK_DIM)
    return jnp.cos(ang), jnp.sin(ang)


def apply_rope(x, cos, sin):
    xr = x.reshape(x.shape[:-1] + (2, 2, DIFF_QK_DIM // 4))
    rot = jnp.stack([-xr[..., 1, :], xr[..., 0, :]], axis=-2).reshape(x.shape)
    return (x * cos + rot * sin).astype(x.dtype)


def context_attention(q, k, v):
    scale = q.shape[-1] ** -0.5

    def blk(qb):
        p = softmax_f32(jnp.einsum('bqhd,bkhd->bhqk', qb, k) * scale).astype(v.dtype)
        return jnp.einsum('bhqk,bkhd->bqhd', p, v)
    return sweep_query_blocks(blk, q)


def neighborhood_attention(q, k, v, ctx_k, ctx_v, rpb):
    b, l, h, d = q.shape
    rows = l // GRID_W
    kh = min(NA_KH, rows)
    n_lat = kh * GRID_W
    scale = d ** -0.5
    qg = jnp.moveaxis(q.reshape(b, rows, GRID_W, h, d), 1, 0)
    kg = k.reshape(b, rows, GRID_W, h, d)
    vg = v.reshape(b, rows, GRID_W, h, d)
    r_idx = jnp.arange(rows)
    row_start = jnp.clip(r_idx - kh // 2, 0, rows - kh)
    cols = jnp.arange(GRID_W)
    col_start = jnp.clip(cols - NA_KW // 2, 0, GRID_W - NA_KW)
    col_mask = (cols[None, :] >= col_start[:, None]) & (cols[None, :] < col_start[:, None] + NA_KW)
    mask = jnp.broadcast_to(col_mask[:, None, :], (GRID_W, kh, GRID_W)).reshape(GRID_W, n_lat)
    col_bias_idx = jnp.clip(cols[None, :] - cols[:, None] + NA_KW - 1, 0, 2 * NA_KW - 2)

    def row_fn(args):
        q_r, r0, r = args
        k_blk = lax.dynamic_slice_in_dim(kg, r0, kh, axis=1).reshape(b, n_lat, h, d)
        v_blk = lax.dynamic_slice_in_dim(vg, r0, kh, axis=1).reshape(b, n_lat, h, d)
        row_off = r0 + jnp.arange(kh) - r + NA_KH - 1
        bias = rpb[:, row_off][:, :, col_bias_idx]
        bias = jnp.transpose(bias, (0, 2, 1, 3)).reshape(h, GRID_W, n_lat).astype(jnp.float32)
        s_lat = jnp.einsum('bqhd,bkhd->bhqk', q_r, k_blk).astype(jnp.float32) * scale + bias
        s_lat = jnp.where(mask, s_lat, NEG_INF)
        s_ctx = jnp.einsum('bqhd,bkhd->bhqk', q_r, ctx_k).astype(jnp.float32) * scale
        p = jax.nn.softmax(jnp.concatenate([s_lat, s_ctx], axis=-1), axis=-1).astype(v.dtype)
        return (jnp.einsum('bhqk,bkhd->bqhd', p[..., :n_lat], v_blk)
                + jnp.einsum('bhqk,bkhd->bqhd', p[..., n_lat:], ctx_v))
    out = lax.map(row_fn, (qg, row_start, r_idx))
    return jnp.moveaxis(out, 0, 1).reshape(b, l, h, d)


def hyena_filters(l, w1, b1, f1, w2, b2, f2, w3, decay):
    t = jnp.linspace(0.0, 1.0, l, dtype=jnp.float32)[:, None]
    bands = (HY_POS_EMB - 1) // 2
    w_ang = 2.0 * math.pi * jnp.arange(l, dtype=jnp.float32)[:, None] / l
    freqs = jnp.linspace(1e-4, bands - 1, bands, dtype=jnp.float32)[None, :]
    z = jnp.concatenate([t, jnp.cos(freqs * w_ang), -jnp.sin(freqs * w_ang)], axis=-1)
    h = jnp.sin(f1 * (z @ w1 + b1))
    h = jnp.sin(f2 * (h @ w2 + b2))
    h = (h @ w3).reshape(l, 2, HY_ORDER, HY_WIDTH).astype(jnp.float32)
    h = h * jnp.exp(-t.reshape(l, 1, 1, 1) * jnp.abs(decay.astype(jnp.float32)))
    h_f, h_b = h[:, 0], h[:, 1]
    zero = jnp.zeros((1, HY_ORDER, HY_WIDTH), jnp.float32)
    return jnp.concatenate([h_f, zero, h_b[1:][::-1]], axis=0)


def hyena(u, conv_w, conv_b, filt, bias):
    b, l, _ = u.shape
    up = jnp.pad(u, ((0, 0), (1, 1), (0, 0)))
    u = up[:, :-2] * conv_w[0] + up[:, 1:-1] * conv_w[1] + up[:, 2:] * conv_w[2] + conv_b
    v, x1, x2 = jnp.split(u, 3, axis=-1)
    kf = jnp.fft.rfft(hyena_filters(l, *filt), axis=0)
    z = v
    for o, gate in enumerate((x1, x2)):
        zf = z.astype(jnp.float32)
        y = jnp.fft.irfft(jnp.fft.rfft(zf, n=2 * l, axis=1) * kf[None, :, o], n=2 * l, axis=1)[:, :l]
        z = gate * (y + zf * bias[o].astype(jnp.float32)).astype(u.dtype)
    return z


def s5_combine(e1, e2):
    a1r, a1i, b1r, b1i = e1
    a2r, a2i, b2r, b2i = e2
    return (a2r * a1r - a2i * a1i, a2r * a1i + a2i * a1r,
            a2r * b1r - a2i * b1i + b2r, a2r * b1i + a2i * b1r + b2i)


def s5_scan(ug, h0_re, h0_im, lam_re, lam_im, log_dt, b_re, b_im, c_re, c_im, reverse):
    f32 = jnp.float32
    lam_re, lam_im = lam_re.astype(f32), lam_im.astype(f32)
    b_re, b_im, c_re, c_im = b_re.astype(f32), b_im.astype(f32), c_re.astype(f32), c_im.astype(f32)
    dt = jnp.exp(log_dt.astype(f32))[:, None]
    mag = jnp.exp(lam_re * dt)
    ar, ai = mag * jnp.cos(lam_im * dt), mag * jnp.sin(lam_im * dt)
    den = lam_re * lam_re + lam_im * lam_im
    cr = ((ar - 1.0) * lam_re + ai * lam_im) / den
    ci = (ai * lam_re - (ar - 1.0) * lam_im) / den
    bbr = cr[..., None] * b_re - ci[..., None] * b_im
    bbi = cr[..., None] * b_im + ci[..., None] * b_re
    if reverse:
        ug = jnp.flip(ug, axis=1)

    def one(args):
        ub, hr0, hi0 = args
        bur = jnp.einsum('gnp,lgp->lgn', bbr, ub)
        bui = jnp.einsum('gnp,lgp->lgn', bbi, ub)
        bur = bur.at[0].add(ar * hr0 - ai * hi0)
        bui = bui.at[0].add(ar * hi0 + ai * hr0)
        a_r = jnp.broadcast_to(ar, bur.shape)
        a_i = jnp.broadcast_to(ai, bur.shape)
        _, _, hr, hi = lax.associative_scan(s5_combine, (a_r, a_i, bur, bui), axis=0)
        y = jnp.einsum('gpn,lgn->lgp', c_re, hr) - jnp.einsum('gpn,lgn->lgp', c_im, hi)
        return y, hr[-1], hi[-1]
    y, fr, fi = lax.map(one, (ug, h0_re.astype(f32), h0_im.astype(f32)))
    if reverse:
        y = jnp.flip(y, axis=1)
    return y, fr, fi


def s5_mixer(u, h0_re, h0_im, lam_re, lam_im, log_dt, b_re, b_im, c_re, c_im, d_skip, glu_w, glu_b):
    b, l, _ = u.shape
    ug = u.astype(jnp.float32).reshape(b, l, S5_GROUPS, S5_GROUP)
    y = d_skip.astype(jnp.float32) * ug
    fin_re, fin_im = [], []
    for d in range(2):
        yd, fr, fi = s5_scan(ug, h0_re[:, d], h0_im[:, d], lam_re[d], lam_im[d], log_dt[d],
                             b_re[d], b_im[d], c_re[d], c_im[d], reverse=(d == 1))
        y = y + yd
        fin_re.append(fr)
        fin_im.append(fi)
    y = jax.nn.gelu(y.reshape(b, l, S5_WIDTH)).astype(u.dtype)
    out = y * jax.nn.sigmoid(mm3(y, glu_w) + glu_b)
    return out, jnp.stack(fin_re, axis=1), jnp.stack(fin_im, axis=1)


def diff_attention(q, k, v, lam):
    scale = q.shape[-1] ** -0.5

    def blk(qb):
        p = softmax_f32(jnp.einsum('bqhsd,bkhsd->bshqk', qb, k) * scale)
        w = (p[:, 0] - lam * p[:, 1]).astype(v.dtype)
        return jnp.einsum('bhqk,bkhd->bqhd', w, v)
    return sweep_query_blocks(blk, q)


def even_mixer(h, ev, ctx_kv):
    (w_in, w_out, q_g, k_g, rpb, conv_w, conv_b, w1, b1, f1, w2, b2, f2, w3, decay, bias) = ev
    b, l, _ = h.shape
    p = mm3(h, w_in)
    qa, ka, va, hb = jnp.split(p, [MIX_WIDTH, 2 * MIX_WIDTH, 3 * MIX_WIDTH], axis=-1)
    shp = (b, l, NA_HEADS, HEAD_DIM)
    qa = rmsnorm(qa.reshape(shp), q_g)
    ka = rmsnorm(ka.reshape(shp), k_g)
    va = va.reshape(shp)
    if ctx_kv is None:
        oa = context_attention(qa, ka, va)
    else:
        oa = neighborhood_attention(qa, ka, va, ctx_kv[0], ctx_kv[1], rpb)
    ob = hyena(hb, conv_w, conv_b, (w1, b1, f1, w2, b2, f2, w3, decay), bias)
    out = mm3(jnp.concatenate([oa.reshape(b, l, MIX_WIDTH), ob], axis=-1), w_out)
    return out, ka, va


def odd_mixer(h, od, lam_init, h0_re, h0_im, ctx_kv, rope):
    (w_in, w_out, lam_re, lam_im, log_dt, b_re, b_im, c_re, c_im, d_skip, glu_w, glu_b,
     q_g, k_g, lq1, lk1, lq2, lk2, subln_g) = od
    b, l, _ = h.shape
    qk = DIFF_HEADS * 2 * DIFF_QK_DIM
    p = mm3(h, w_in)
    u, qd, kd, vd = jnp.split(p, [S5_WIDTH, S5_WIDTH + qk, S5_WIDTH + 2 * qk], axis=-1)
    oc, fr, fi = s5_mixer(u, h0_re, h0_im, lam_re, lam_im, log_dt, b_re, b_im, c_re, c_im,
                          d_skip, glu_w, glu_b)
    shp = (b, l, DIFF_HEADS, 2, DIFF_QK_DIM)
    q = rmsnorm(qd.reshape(shp), q_g)
    k = rmsnorm(kd.reshape(shp), k_g)
    v = vd.reshape(b, l, DIFF_HEADS, DIFF_V_DIM)
    lam = (jnp.exp(jnp.sum(lq1 * lk1).astype(jnp.float32))
           - jnp.exp(jnp.sum(lq2 * lk2).astype(jnp.float32)) + lam_init)
    if ctx_kv is None:
        keys, vals = k, v
    else:
        cos, sin = rope
        cos = cos[None, :, None, None, :]
        sin = sin[None, :, None, None, :]
        q = apply_rope(q, cos, sin)
        keys = jnp.concatenate([apply_rope(k, cos, sin).astype(ctx_kv[0].dtype), ctx_kv[0]], axis=1)
        vals = jnp.concatenate([v.astype(ctx_kv[1].dtype), ctx_kv[1]], axis=1)
    o = diff_attention(q, keys, vals, lam)
    o = rmsnorm(o, subln_g) * (1.0 - lam_init)
    out = mm3(jnp.concatenate([oc, o.reshape(b, l, DIFF_HEADS * DIFF_V_DIM).astype(oc.dtype)], axis=-1), w_out)
    return out, k, v, fr, fi


LANES = 128
SUBLANES = 8


def _top16_rows(cur, iota):
    n = cur.shape[0]
    tops = []
    for _ in range(PEER_TOPK):
        m = jnp.max(cur, axis=0, keepdims=True)
        tops.append(m)
        first = jnp.min(jnp.where(cur == m, iota, n), axis=0, keepdims=True)
        cur = jnp.where(iota == first, -jnp.inf, cur)
    return tops


def _router_kernel(q_ref, keys_ref, s0_ref, s1_ref, e0_ref, e1_ref, tau_ref, st_ref):
    tt = q_ref.shape[0]
    half = PEER_QUERY_DIM // 2
    for h in range(PEER_HEADS):
        for s in range(2):
            qs = q_ref[:, (2 * h + s) * half:(2 * h + s + 1) * half].astype(jnp.bfloat16)
            st = lax.dot_general(keys_ref[h, s], qs, (((1,), (1,)), ((), ())),
                                 preferred_element_type=jnp.float32)
            st_ref[2 * h + s] = st

    iota128 = lax.broadcasted_iota(jnp.int32, (PEER_N_KEYS, LANES), 0)
    iota256 = lax.broadcasted_iota(jnp.int32, (PEER_TOPK * PEER_TOPK, LANES), 0)
    n_chunks = tt // LANES

    def body(i, carry):
        h = i // n_chunks
        lane0 = pl.multiple_of((i % n_chunks) * LANES, LANES)
        s0 = st_ref[2 * h, :, pl.ds(lane0, LANES)]
        s1 = st_ref[2 * h + 1, :, pl.ds(lane0, LANES)]
        a = _top16_rows(s0, iota128)
        b = _top16_rows(s1, iota128)
        bmat = jnp.concatenate(b, axis=0)
        cand = jnp.concatenate([a[k] + bmat for k in range(PEER_TOPK)], axis=0)
        f = _top16_rows(cand, iota256)
        z = jnp.ones_like(f[0])
        for k in range(1, PEER_TOPK):
            z = z + jnp.exp(f[k] - f[0])
        s0_ref[h, :, pl.ds(lane0, LANES)] = s0
        s1_ref[h, :, pl.ds(lane0, LANES)] = s1
        e0_ref[h, :, pl.ds(lane0, LANES)] = jnp.exp(s0 - a[0]) / z
        e1_ref[h, :, pl.ds(lane0, LANES)] = jnp.exp(s1 - b[0])
        tau_ref[h, :, pl.ds(lane0, LANES)] = f[PEER_TOPK - 1]
        return carry

    lax.fori_loop(0, PEER_HEADS * n_chunks, body, 0)


def peer_router(q, keys_bf16, tt):
    t = q.shape[0]
    f32 = jnp.float32
    big = jax.ShapeDtypeStruct((PEER_HEADS, PEER_N_KEYS, t), f32)
    blk = pl.BlockSpec((PEER_HEADS, PEER_N_KEYS, tt), lambda i: (0, 0, i))
    return pl.pallas_call(
        _router_kernel,
        grid=(t // tt,),
        in_specs=[pl.BlockSpec((tt, q.shape[1]), lambda i: (i, 0)),
                  pl.BlockSpec(keys_bf16.shape, lambda i: (0, 0, 0, 0))],
        out_specs=[blk, blk, blk, blk, pl.BlockSpec((PEER_HEADS, 1, tt), lambda i: (0, 0, i))],
        out_shape=[big, big, big, big, jax.ShapeDtypeStruct((PEER_HEADS, 1, t), f32)],
        scratch_shapes=[pltpu.VMEM((2 * PEER_HEADS, PEER_N_KEYS, tt), f32)],
        compiler_params=pltpu.CompilerParams(dimension_semantics=("parallel",),
                                             vmem_limit_bytes=VMEM_LIMIT_BYTES),
        name="peer_router",
    )(q, keys_bf16)


def _gelu_tanh(x):
    return 0.5 * x * (1.0 + jnp.tanh(math.sqrt(2.0 / math.pi) * (x + 0.044715 * (x * x * x))))


def _dense_kernel(x_ref, u_ref, vt_ref, s0_ref, s1_ref, e0_ref, e1_ref, tau_ref, res_ref, gate_ref,
                  o_ref, at_ref, wt_ref, acc_ref):
    e = pl.program_id(1)
    n_e = pl.num_programs(1)
    eb = u_ref.shape[0]
    tt = x_ref.shape[0]
    n_i1 = eb // PEER_N_KEYS
    n_chunks = tt // LANES

    @pl.when(e == 0)
    def _():
        acc_ref[...] = jnp.zeros_like(acc_ref)

    at_ref[...] = lax.dot_general(u_ref[...], x_ref[...], (((1,), (1,)), ((), ())),
                                  preferred_element_type=jnp.float32)

    grp0 = pl.multiple_of((e * n_i1 // SUBLANES) * SUBLANES, SUBLANES)
    upper_half = (e * n_i1) % SUBLANES != 0

    def body(c, carry):
        lane0 = pl.multiple_of(c * LANES, LANES)
        s0g = [s0_ref[h, pl.ds(grp0, SUBLANES), pl.ds(lane0, LANES)] for h in range(PEER_HEADS)]
        e0g = [e0_ref[h, pl.ds(grp0, SUBLANES), pl.ds(lane0, LANES)] for h in range(PEER_HEADS)]
        for i1 in range(n_i1):
            g = jnp.zeros((PEER_N_KEYS, LANES), jnp.float32)
            for h in range(PEER_HEADS):
                s0row = jnp.where(upper_half, s0g[h][n_i1 + i1:n_i1 + i1 + 1], s0g[h][i1:i1 + 1])
                e0row = jnp.where(upper_half, e0g[h][n_i1 + i1:n_i1 + i1 + 1], e0g[h][i1:i1 + 1])
                taurow = tau_ref[h, :, pl.ds(lane0, LANES)]
                s1 = s1_ref[h, :, pl.ds(lane0, LANES)]
                e1 = e1_ref[h, :, pl.ds(lane0, LANES)]
                g = g + jnp.where(s0row + s1 >= taurow, e0row * e1, 0.0)
            a = at_ref[i1 * PEER_N_KEYS:(i1 + 1) * PEER_N_KEYS, pl.ds(lane0, LANES)]
            wt_ref[i1 * PEER_N_KEYS:(i1 + 1) * PEER_N_KEYS, pl.ds(lane0, LANES)] = (
                _gelu_tanh(a) * g).astype(jnp.bfloat16)
        return carry

    lax.fori_loop(0, n_chunks, body, 0)

    acc_ref[...] += jnp.dot(vt_ref[...], wt_ref[...], preferred_element_type=jnp.float32)

    @pl.when(e == n_e - 1)
    def _():
        o_ref[...] = res_ref[...] + gate_ref[0] * acc_ref[...].T


def peer_dense(x_bf16, u_bf16, vt_bf16, s0, s1, e0, e1, tau, resid, gate, tokens_per_gate, tt, eb):
    t, d = x_bf16.shape
    n_exp = u_bf16.shape[0]
    assert t % tt == 0 and n_exp % eb == 0 and tokens_per_gate % tt == 0
    assert 2 * (eb // PEER_N_KEYS) == SUBLANES
    tiles_per_gate = tokens_per_gate // tt
    rblk = pl.BlockSpec((PEER_HEADS, PEER_N_KEYS, tt), lambda i, j: (0, 0, i))
    return pl.pallas_call(
        _dense_kernel,
        grid=(t // tt, n_exp // eb),
        in_specs=[pl.BlockSpec((tt, d), lambda i, j: (i, 0)),
                  pl.BlockSpec((eb, d), lambda i, j: (j, 0)),
                  pl.BlockSpec((d, eb), lambda i, j: (0, j)),
                  rblk, rblk, rblk, rblk,
                  pl.BlockSpec((PEER_HEADS, 1, tt), lambda i, j: (0, 0, i)),
                  pl.BlockSpec((tt, d), lambda i, j: (i, 0)),
                  pl.BlockSpec((1, 1, d), lambda i, j: (i // tiles_per_gate, 0, 0))],
        out_specs=pl.BlockSpec((tt, d), lambda i, j: (i, 0)),
        out_shape=jax.ShapeDtypeStruct((t, d), jnp.float32),
        scratch_shapes=[pltpu.VMEM((eb, tt), jnp.float32),
                        pltpu.VMEM((eb, tt), jnp.bfloat16),
                        pltpu.VMEM((d, tt), jnp.float32)],
        compiler_params=pltpu.CompilerParams(dimension_semantics=("parallel", "arbitrary"),
                                             vmem_limit_bytes=VMEM_LIMIT_BYTES),
        name="peer_dense",
    )(x_bf16, u_bf16, vt_bf16, s0, s1, e0, e1, tau, resid, gate)


PEER_TOKEN_TILE = 512
PEER_EXPERT_TILE = 512


def peer_layer(x, norm_g, mods, w_q, keys, u_bf16, vt_bf16):
    b, l, d = x.shape
    h = modulate(rmsnorm(x, norm_g), mods[3], mods[4]).reshape(b * l, d)
    q = mm(h, w_q)
    gate = mods[5]
    tokens_per_gate = (b * l) // gate.shape[0]
    s0, s1, e0, e1, tau = peer_router(q, keys.astype(jnp.bfloat16), PEER_TOKEN_TILE)
    out = peer_dense(h.astype(jnp.bfloat16), u_bf16, vt_bf16, s0, s1, e0, e1, tau,
                     x.reshape(b * l, d), gate, tokens_per_gate, PEER_TOKEN_TILE, PEER_EXPERT_TILE)
    return out.reshape(b, l, d)


def kernel(x_prompt, x_sample, c, cache_na_k, cache_na_v, cache_diff_k, cache_diff_v, state_s5_re, state_s5_im, c_ctx, mod_w, mod_b, norm_mix_g, norm_ffn_g, ev_w_in, ev_w_out, na_q_g, na_k_g, na_rpb, hy_conv_w, hy_conv_b, hy_w1, hy_b1, hy_f1, hy_w2, hy_b2, hy_f2, hy_w3, hy_decay, hy_bias, od_w_in, od_w_out, s5_lam_re, s5_lam_im, s5_log_dt, s5_b_re, s5_b_im, s5_c_re, s5_c_im, s5_d, s5_glu_w, s5_glu_b, diff_q_g, diff_k_g, diff_lq1, diff_lk1, diff_lq2, diff_lk2, diff_subln_g, peer_w_q, peer_keys, peer_u, peer_v):
    rope = grid_rope(x_sample.shape[1])
    xp, xs = x_prompt, x_sample
    new_na_k, new_na_v, new_dk, new_dv, new_sr, new_si = [], [], [], [], [], []
    for i in range(DEPTH):
        mp = adaln(c_ctx[None, :], mod_w[i], mod_b[i])
        ms = adaln(c, mod_w[i], mod_b[i])
        hp = modulate(rmsnorm(xp, norm_mix_g[i]), mp[0], mp[1])
        hs = modulate(rmsnorm(xs, norm_mix_g[i]), ms[0], ms[1])
        j = i // 2
        if i % 2 == 0:
            ev = (ev_w_in[j], ev_w_out[j], na_q_g[j], na_k_g[j], na_rpb[j], hy_conv_w[j], hy_conv_b[j],
                  hy_w1[j], hy_b1[j], hy_f1[j], hy_w2[j], hy_b2[j], hy_f2[j], hy_w3[j], hy_decay[j], hy_bias[j])
            op, kp, vp = even_mixer(hp, ev, None)
            os_, _, _ = even_mixer(hs, ev, (cache_na_k[:, j], cache_na_v[:, j]))
            new_na_k.append(kp)
            new_na_v.append(vp)
        else:
            lam_init = 0.8 - 0.6 * math.exp(-0.3 * i)
            od = (od_w_in[j], od_w_out[j], s5_lam_re[j], s5_lam_im[j], s5_log_dt[j], s5_b_re[j], s5_b_im[j],
                  s5_c_re[j], s5_c_im[j], s5_d[j], s5_glu_w[j], s5_glu_b[j], diff_q_g[j], diff_k_g[j],
                  diff_lq1[j], diff_lk1[j], diff_lq2[j], diff_lk2[j], diff_subln_g[j])
            zeros = jnp.zeros((xp.shape[0], 2, S5_GROUPS, S5_STATE), jnp.float32)
            op, kp, vp, sr, si = odd_mixer(hp, od, lam_init, zeros, zeros, None, None)
            os_, _, _, _, _ = odd_mixer(hs, od, lam_init, state_s5_re[:, j], state_s5_im[:, j],
                                        (cache_diff_k[:, j], cache_diff_v[:, j]), rope)
            new_dk.append(kp)
            new_dv.append(vp)
            new_sr.append(sr)
            new_si.append(si)
        xp = xp + mp[2] * op
        xs = xs + ms[2] * os_
        u_bf16 = peer_u[i].astype(jnp.bfloat16)
        vt_bf16 = peer_v[i].T.astype(jnp.bfloat16)
        xp = peer_layer(xp, norm_ffn_g[i], mp, peer_w_q[i], peer_keys[i], u_bf16, vt_bf16)
        xs = peer_layer(xs, norm_ffn_g[i], ms, peer_w_q[i], peer_keys[i], u_bf16, vt_bf16)
    return (xp, xs, jnp.stack(new_na_k, axis=1), jnp.stack(new_na_v, axis=1),
            jnp.stack(new_dk, axis=1), jnp.stack(new_dv, axis=1),
            jnp.stack(new_sr, axis=1), jnp.stack(new_si, axis=1))
```

```python
import functools
import math

import jax
import jax.numpy as jnp
from jax import lax
from jax.experimental import pallas as pl
from jax.experimental.pallas import tpu as pltpu

D_MODEL = 2048
DEPTH = 2
GRID_W = 64
MIX_WIDTH = D_MODEL // 2
HEAD_DIM = 128
NA_HEADS = MIX_WIDTH // HEAD_DIM
NA_KH = 8
NA_KW = 16
HY_WIDTH = MIX_WIDTH
HY_ORDER = 2
HY_POS_EMB = 33
S5_WIDTH = MIX_WIDTH
S5_GROUP = 16
S5_GROUPS = S5_WIDTH // S5_GROUP
S5_STATE = 64
DIFF_HEADS = MIX_WIDTH // HEAD_DIM
DIFF_QK_DIM = HEAD_DIM // 2
DIFF_V_DIM = HEAD_DIM
ROPE_BASE = 10000.0
PEER_HEADS = 8
PEER_N_KEYS = 128
PEER_QUERY_DIM = 256
PEER_TOPK = 16
Q_BLOCK = 128
TOKEN_BLOCK = 128
NORM_EPS = 1e-6
NEG_INF = -1e30

VMEM_LIMIT_BYTES = 56 * 1024 * 1024


def _mm_kernel(x_ref, w_ref, o_ref):
    o_ref[...] = jnp.dot(x_ref[...].astype(jnp.bfloat16), w_ref[...].astype(jnp.bfloat16),
                         preferred_element_type=jnp.float32)


def mm(x, w, tm=1024, tn=512):
    m, k = x.shape
    _, n = w.shape
    tm = min(tm, m)
    tn = min(tn, n)
    assert m % tm == 0 and n % tn == 0
    return pl.pallas_call(
        _mm_kernel,
        grid=(m // tm, n // tn),
        in_specs=[pl.BlockSpec((tm, k), lambda i, j: (i, 0)),
                  pl.BlockSpec((k, tn), lambda i, j: (0, j))],
        out_specs=pl.BlockSpec((tm, tn), lambda i, j: (i, j)),
        out_shape=jax.ShapeDtypeStruct((m, n), jnp.float32),
        compiler_params=pltpu.CompilerParams(
            dimension_semantics=("parallel", "arbitrary"),
            vmem_limit_bytes=VMEM_LIMIT_BYTES),
    )(x, w)


def mm3(x, w):
    b, l, d = x.shape
    return mm(x.reshape(b * l, d), w).reshape(b, l, w.shape[1])


def rmsnorm(x, g):
    xf = x.astype(jnp.float32)
    y = xf * lax.rsqrt(jnp.mean(xf * xf, axis=-1, keepdims=True) + NORM_EPS)
    return (y * g.astype(jnp.float32)).astype(x.dtype)


def adaln(cond, w, b):
    m = jax.nn.silu(cond) @ w + b
    return jnp.split(m[:, None, :], 6, axis=-1)


def modulate(h, shift, scale):
    return h * (1.0 + scale) + shift


def softmax_f32(s):
    return jax.nn.softmax(s.astype(jnp.float32), axis=-1)


def sweep_query_blocks(fn, q):
    b, l = q.shape[:2]
    qb = jnp.moveaxis(q.reshape((b, l // Q_BLOCK, Q_BLOCK) + q.shape[2:]), 1, 0)
    out = jnp.moveaxis(lax.map(fn, qb), 0, 1)
    return out.reshape((b, l) + out.shape[3:])


def grid_rope(l):
    t = jnp.arange(l)
    row = (t // GRID_W).astype(jnp.float32)
    col = (t % GRID_W).astype(jnp.float32)
    nf = DIFF_QK_DIM // 4
    inv = ROPE_BASE ** (-jnp.arange(nf, dtype=jnp.float32) / nf)
    ang = jnp.stack([row[:, None] * inv, col[:, None] * inv], axis=1)
    ang = jnp.stack([ang, ang], axis=2).reshape(l, DIFF_QK_DIM)
    return jnp.cos(ang), jnp.sin(ang)


def apply_rope(x, cos, sin):
    xr = x.reshape(x.shape[:-1] + (2, 2, DIFF_QK_DIM // 4))
    rot = jnp.stack([-xr[..., 1, :], xr[..., 0, :]], axis=-2).reshape(x.shape)
    return (x * cos + rot * sin).astype(x.dtype)


def context_attention(q, k, v):
    scale = q.shape[-1] ** -0.5

    def blk(qb):
        p = softmax_f32(jnp.einsum('bqhd,bkhd->bhqk', qb, k) * scale).astype(v.dtype)
        return jnp.einsum('bhqk,bkhd->bqhd', p, v)
    return sweep_query_blocks(blk, q)


def neighborhood_attention(q, k, v, ctx_k, ctx_v, rpb):
    b, l, h, d = q.shape
    rows = l // GRID_W
    kh = min(NA_KH, rows)
    n_lat = kh * GRID_W
    scale = d ** -0.5
    qg = jnp.moveaxis(q.reshape(b, rows, GRID_W, h, d), 1, 0)
    kg = k.reshape(b, rows, GRID_W, h, d)
    vg = v.reshape(b, rows, GRID_W, h, d)
    r_idx = jnp.arange(rows)
    row_start = jnp.clip(r_idx - kh // 2, 0, rows - kh)
    cols = jnp.arange(GRID_W)
    col_start = jnp.clip(cols - NA_KW // 2, 0, GRID_W - NA_KW)
    col_mask = (cols[None, :] >= col_start[:, None]) & (cols[None, :] < col_start[:, None] + NA_KW)
    mask = jnp.broadcast_to(col_mask[:, None, :], (GRID_W, kh, GRID_W)).reshape(GRID_W, n_lat)
    col_bias_idx = jnp.clip(cols[None, :] - cols[:, None] + NA_KW - 1, 0, 2 * NA_KW - 2)

    def row_fn(args):
        q_r, r0, r = args
        k_blk = lax.dynamic_slice_in_dim(kg, r0, kh, axis=1).reshape(b, n_lat, h, d)
        v_blk = lax.dynamic_slice_in_dim(vg, r0, kh, axis=1).reshape(b, n_lat, h, d)
        row_off = r0 + jnp.arange(kh) - r + NA_KH - 1
        bias = rpb[:, row_off][:, :, col_bias_idx]
        bias = jnp.transpose(bias, (0, 2, 1, 3)).reshape(h, GRID_W, n_lat).astype(jnp.float32)
        s_lat = jnp.einsum('bqhd,bkhd->bhqk', q_r, k_blk).astype(jnp.float32) * scale + bias
        s_lat = jnp.where(mask, s_lat, NEG_INF)
        s_ctx = jnp.einsum('bqhd,bkhd->bhqk', q_r, ctx_k).astype(jnp.float32) * scale
        p = jax.nn.softmax(jnp.concatenate([s_lat, s_ctx], axis=-1), axis=-1).astype(v.dtype)
        return (jnp.einsum('bhqk,bkhd->bqhd', p[..., :n_lat], v_blk)
                + jnp.einsum('bhqk,bkhd->bqhd', p[..., n_lat:], ctx_v))
    out = lax.map(row_fn, (qg, row_start, r_idx))
    return jnp.moveaxis(out, 0, 1).reshape(b, l, h, d)


def hyena_filters(l, w1, b1, f1, w2, b2, f2, w3, decay):
    t = jnp.linspace(0.0, 1.0, l, dtype=jnp.float32)[:, None]
    bands = (HY_POS_EMB - 1) // 2
    w_ang = 2.0 * math.pi * jnp.arange(l, dtype=jnp.float32)[:, None] / l
    freqs = jnp.linspace(1e-4, bands - 1, bands, dtype=jnp.float32)[None, :]
    z = jnp.concatenate([t, jnp.cos(freqs * w_ang), -jnp.sin(freqs * w_ang)], axis=-1)
    h = jnp.sin(f1 * (z @ w1 + b1))
    h = jnp.sin(f2 * (h @ w2 + b2))
    h = (h @ w3).reshape(l, 2, HY_ORDER, HY_WIDTH).astype(jnp.float32)
    h = h * jnp.exp(-t.reshape(l, 1, 1, 1) * jnp.abs(decay.astype(jnp.float32)))
    h_f, h_b = h[:, 0], h[:, 1]
    zero = jnp.zeros((1, HY_ORDER, HY_WIDTH), jnp.float32)
    return jnp.concatenate([h_f, zero, h_b[1:][::-1]], axis=0)


def hyena(u, conv_w, conv_b, filt, bias):
    b, l, _ = u.shape
    up = jnp.pad(u, ((0, 0), (1, 1), (0, 0)))
    u = up[:, :-2] * conv_w[0] + up[:, 1:-1] * conv_w[1] + up[:, 2:] * conv_w[2] + conv_b
    v, x1, x2 = jnp.split(u, 3, axis=-1)
    kf = jnp.fft.rfft(hyena_filters(l, *filt), axis=0)
    z = v
    for o, gate in enumerate((x1, x2)):
        zf = z.astype(jnp.float32)
        y = jnp.fft.irfft(jnp.fft.rfft(zf, n=2 * l, axis=1) * kf[None, :, o], n=2 * l, axis=1)[:, :l]
        z = gate * (y + zf * bias[o].astype(jnp.float32)).astype(u.dtype)
    return z


S5_TILE_GROUPS = 8
S5_TILES = S5_GROUPS // S5_TILE_GROUPS
S5_TILE_IN = S5_TILE_GROUPS * S5_GROUP
S5_TILE_STATE = S5_TILE_GROUPS * S5_STATE
S5_ROWS_PER_STEP = 2048


def s5_prepare(lam_re, lam_im, log_dt, b_re, b_im, c_re, c_im):
    f32 = jnp.float32
    lam_re, lam_im = lam_re.astype(f32), lam_im.astype(f32)
    dt = jnp.exp(log_dt.astype(f32))[..., None]
    mag = jnp.exp(lam_re * dt)
    ar, ai = mag * jnp.cos(lam_im * dt), mag * jnp.sin(lam_im * dt)
    den = lam_re * lam_re + lam_im * lam_im
    cr = ((ar - 1.0) * lam_re + ai * lam_im) / den
    ci = (ai * lam_re - (ar - 1.0) * lam_im) / den
    bbr = cr[..., None] * b_re - ci[..., None] * b_im
    bbi = cr[..., None] * b_im + ci[..., None] * b_re
    eye = jnp.eye(S5_TILE_GROUPS, dtype=f32)

    def tile_in(bb):
        x = bb.reshape(2, S5_TILES, S5_TILE_GROUPS, S5_STATE, S5_GROUP)
        x = jnp.einsum('dtgnp,gh->dtgphn', x, eye)
        return x.reshape(2, S5_TILES, S5_TILE_IN, S5_TILE_STATE)

    def tile_out(cc):
        x = cc.astype(f32).reshape(2, S5_TILES, S5_TILE_GROUPS, S5_GROUP, S5_STATE)
        x = jnp.einsum('dtgpn,gh->dtgnhp', x, eye)
        return x.reshape(2, S5_TILES, S5_TILE_STATE, S5_TILE_IN)

    win = jnp.concatenate([tile_in(bbr), tile_in(bbi)], axis=-1).astype(jnp.bfloat16)
    wout = jnp.concatenate([tile_out(c_re), -tile_out(c_im)], axis=-2).astype(jnp.bfloat16)
    a = jnp.stack([ar.reshape(2, S5_TILES, S5_TILE_STATE), ai.reshape(2, S5_TILES, S5_TILE_STATE)], axis=2)
    return win, wout, a


def _s5_scan_kernel(u_ref, win_ref, wout_ref, a_ref, h0_ref, y_ref, fin_ref, bu_ref, st_ref, *, batch):
    d = pl.program_id(0)
    c = pl.program_id(2)
    n_c = pl.num_programs(2)
    ns = S5_TILE_STATE
    steps = u_ref.shape[0] // batch

    @pl.when(c == 0)
    def _():
        st_ref[...] = h0_ref[0, 0]

    bu_ref[...] = jnp.dot(u_ref[...].astype(jnp.bfloat16), win_ref[0, 0], preferred_element_type=jnp.float32)
    ar = jnp.broadcast_to(a_ref[0, 0, 0:1, :], (batch, ns))
    ai = jnp.broadcast_to(a_ref[0, 0, 1:2, :], (batch, ns))

    def step(t, carry):
        hr, hi = carry
        te = jnp.where(d == 0, t, steps - 1 - t)
        r0 = pl.multiple_of(te * batch, batch)
        nr = ar * hr - ai * hi + bu_ref[pl.ds(r0, batch), 0:ns]
        ni = ar * hi + ai * hr + bu_ref[pl.ds(r0, batch), ns:2 * ns]
        bu_ref[pl.ds(r0, batch), 0:ns] = nr
        bu_ref[pl.ds(r0, batch), ns:2 * ns] = ni
        return nr, ni

    hr, hi = lax.fori_loop(0, steps, step, (st_ref[:, 0:ns], st_ref[:, ns:2 * ns]), unroll=4)
    st_ref[:, 0:ns] = hr
    st_ref[:, ns:2 * ns] = hi
    y_ref[0] = jnp.dot(bu_ref[...].astype(jnp.bfloat16), wout_ref[0, 0], preferred_element_type=jnp.float32)

    @pl.when(c == n_c - 1)
    def _():
        fin_ref[0, 0] = st_ref[...]


def s5_scan(u_tm, win, wout, a, h0, batch):
    rows = u_tm.shape[0]
    r = min(S5_ROWS_PER_STEP, rows)
    assert rows % r == 0 and r % batch == 0
    n_c = rows // r

    def chunk(d, c):
        return c + d * (n_c - 1 - 2 * c)

    return pl.pallas_call(
        functools.partial(_s5_scan_kernel, batch=batch),
        grid=(2, S5_TILES, n_c),
        in_specs=[pl.BlockSpec((r, S5_TILE_IN), lambda d, j, c: (chunk(d, c), j)),
                  pl.BlockSpec((1, 1, S5_TILE_IN, 2 * S5_TILE_STATE), lambda d, j, c: (d, j, 0, 0)),
                  pl.BlockSpec((1, 1, 2 * S5_TILE_STATE, S5_TILE_IN), lambda d, j, c: (d, j, 0, 0)),
                  pl.BlockSpec((1, 1, 2, S5_TILE_STATE), lambda d, j, c: (d, j, 0, 0)),
                  pl.BlockSpec((1, 1, batch, 2 * S5_TILE_STATE), lambda d, j, c: (d, j, 0, 0))],
        out_specs=[pl.BlockSpec((1, r, S5_TILE_IN), lambda d, j, c: (d, chunk(d, c), j)),
                   pl.BlockSpec((1, 1, batch, 2 * S5_TILE_STATE), lambda d, j, c: (d, j, 0, 0))],
        out_shape=[jax.ShapeDtypeStruct((2, rows, S5_WIDTH), jnp.float32),
                   jax.ShapeDtypeStruct((2, S5_TILES, batch, 2 * S5_TILE_STATE), jnp.float32)],
        scratch_shapes=[pltpu.VMEM((r, 2 * S5_TILE_STATE), jnp.float32),
                        pltpu.VMEM((batch, 2 * S5_TILE_STATE), jnp.float32)],
        compiler_params=pltpu.CompilerParams(dimension_semantics=("parallel", "parallel", "arbitrary"),
                                             vmem_limit_bytes=VMEM_LIMIT_BYTES),
        name="s5_scan",
    )(u_tm, win, wout, a, h0)


def _gelu_tanh(x):
    return 0.5 * x * (1.0 + jnp.tanh(math.sqrt(2.0 / math.pi) * (x + 0.044715 * (x * x * x))))


def _s5_glu_kernel(u_ref, y_ref, d_ref, w_ref, b_ref, o_ref):
    y = d_ref[...] * u_ref[...] + y_ref[0] + y_ref[1]
    y = _gelu_tanh(y)
    z = jnp.dot(y.astype(jnp.bfloat16), w_ref[...], preferred_element_type=jnp.float32) + b_ref[...]
    o_ref[...] = y * (1.0 / (1.0 + jnp.exp(-z)))


def s5_glu(u_tm, y, d_skip, glu_w_bf16, glu_b, tr=1024):
    rows, w = u_tm.shape
    tr = min(tr, rows)
    assert rows % tr == 0
    return pl.pallas_call(
        _s5_glu_kernel,
        grid=(rows // tr,),
        in_specs=[pl.BlockSpec((tr, w), lambda i: (i, 0)),
                  pl.BlockSpec((2, tr, w), lambda i: (0, i, 0)),
                  pl.BlockSpec((1, w), lambda i: (0, 0)),
                  pl.BlockSpec((w, w), lambda i: (0, 0)),
                  pl.BlockSpec((1, w), lambda i: (0, 0))],
        out_specs=pl.BlockSpec((tr, w), lambda i: (i, 0)),
        out_shape=jax.ShapeDtypeStruct((rows, w), jnp.float32),
        compiler_params=pltpu.CompilerParams(dimension_semantics=("parallel",),
                                             vmem_limit_bytes=VMEM_LIMIT_BYTES),
        name="s5_glu",
    )(u_tm, y, d_skip.reshape(1, w), glu_w_bf16, glu_b.reshape(1, w))


def s5_mixer(u, h0_re, h0_im, prep, d_skip, glu_w, glu_b):
    win, wout, a = prep
    b, l, w = u.shape
    u_tm = jnp.swapaxes(u, 0, 1).reshape(l * b, w)
    if h0_re is None:
        h0 = jnp.zeros((2, S5_TILES, b, 2 * S5_TILE_STATE), jnp.float32)
    else:
        def tiles(h):
            return jnp.transpose(h.astype(jnp.float32).reshape(b, 2, S5_TILES, S5_TILE_STATE), (1, 2, 0, 3))
        h0 = jnp.concatenate([tiles(h0_re), tiles(h0_im)], axis=-1)
    y, fin = s5_scan(u_tm, win, wout, a, h0, b)
    out_tm = s5_glu(u_tm, y, d_skip, glu_w.astype(jnp.bfloat16), glu_b)
    out = jnp.swapaxes(out_tm.reshape(l, b, w), 0, 1)

    def untile(f):
        return jnp.transpose(f, (2, 0, 1, 3)).reshape(b, 2, S5_GROUPS, S5_STATE)
    return out, untile(fin[..., :S5_TILE_STATE]), untile(fin[..., S5_TILE_STATE:])


def diff_attention(q, k, v, lam):
    scale = q.shape[-1] ** -0.5

    def blk(qb):
        p = softmax_f32(jnp.einsum('bqhsd,bkhsd->bshqk', qb, k) * scale)
        w = (p[:, 0] - lam * p[:, 1]).astype(v.dtype)
        return jnp.einsum('bhqk,bkhd->bqhd', w, v)
    return sweep_query_blocks(blk, q)


def even_mixer(h, ev, ctx_kv):
    (w_in, w_out, q_g, k_g, rpb, conv_w, conv_b, w1, b1, f1, w2, b2, f2, w3, decay, bias) = ev
    b, l, _ = h.shape
    p = mm3(h, w_in)
    qa, ka, va, hb = jnp.split(p, [MIX_WIDTH, 2 * MIX_WIDTH, 3 * MIX_WIDTH], axis=-1)
    shp = (b, l, NA_HEADS, HEAD_DIM)
    qa = rmsnorm(qa.reshape(shp), q_g)
    ka = rmsnorm(ka.reshape(shp), k_g)
    va = va.reshape(shp)
    if ctx_kv is None:
        oa = context_attention(qa, ka, va)
    else:
        oa = neighborhood_attention(qa, ka, va, ctx_kv[0], ctx_kv[1], rpb)
    ob = hyena(hb, conv_w, conv_b, (w1, b1, f1, w2, b2, f2, w3, decay), bias)
    out = mm3(jnp.concatenate([oa.reshape(b, l, MIX_WIDTH), ob], axis=-1), w_out)
    return out, ka, va


def odd_mixer(h, od, lam_init, h0_re, h0_im, ctx_kv, rope):
    (w_in, w_out, s5_prep, d_skip, glu_w, glu_b,
     q_g, k_g, lq1, lk1, lq2, lk2, subln_g) = od
    b, l, _ = h.shape
    qk = DIFF_HEADS * 2 * DIFF_QK_DIM
    p = mm3(h, w_in)
    u, qd, kd, vd = jnp.split(p, [S5_WIDTH, S5_WIDTH + qk, S5_WIDTH + 2 * qk], axis=-1)
    oc, fr, fi = s5_mixer(u, h0_re, h0_im, s5_prep, d_skip, glu_w, glu_b)
    shp = (b, l, DIFF_HEADS, 2, DIFF_QK_DIM)
    q = rmsnorm(qd.reshape(shp), q_g)
    k = rmsnorm(kd.reshape(shp), k_g)
    v = vd.reshape(b, l, DIFF_HEADS, DIFF_V_DIM)
    lam = (jnp.exp(jnp.sum(lq1 * lk1).astype(jnp.float32))
           - jnp.exp(jnp.sum(lq2 * lk2).astype(jnp.float32)) + lam_init)
    if ctx_kv is None:
        keys, vals = k, v
    else:
        cos, sin = rope
        cos = cos[None, :, None, None, :]
        sin = sin[None, :, None, None, :]
        q = apply_rope(q, cos, sin)
        keys = jnp.concatenate([apply_rope(k, cos, sin).astype(ctx_kv[0].dtype), ctx_kv[0]], axis=1)
        vals = jnp.concatenate([v.astype(ctx_kv[1].dtype), ctx_kv[1]], axis=1)
    o = diff_attention(q, keys, vals, lam)
    o = rmsnorm(o, subln_g) * (1.0 - lam_init)
    out = mm3(jnp.concatenate([oc, o.reshape(b, l, DIFF_HEADS * DIFF_V_DIM).astype(oc.dtype)], axis=-1), w_out)
    return out, k, v, fr, fi


LANES = 128
SUBLANES = 8


def _top16_rows(cur, iota):
    n = cur.shape[0]
    tops = []
    for _ in range(PEER_TOPK):
        m = jnp.max(cur, axis=0, keepdims=True)
        tops.append(m)
        first = jnp.min(jnp.where(cur == m, iota, n), axis=0, keepdims=True)
        cur = jnp.where(iota == first, -jnp.inf, cur)
    return tops


def _router_kernel(q_ref, keys_ref, s0_ref, s1_ref, e0_ref, e1_ref, tau_ref, st_ref):
    tt = q_ref.shape[0]
    half = PEER_QUERY_DIM // 2
    for h in range(PEER_HEADS):
        for s in range(2):
            qs = q_ref[:, (2 * h + s) * half:(2 * h + s + 1) * half].astype(jnp.bfloat16)
            st = lax.dot_general(keys_ref[h, s], qs, (((1,), (1,)), ((), ())),
                                 preferred_element_type=jnp.float32)
            st_ref[2 * h + s] = st

    iota128 = lax.broadcasted_iota(jnp.int32, (PEER_N_KEYS, LANES), 0)
    iota256 = lax.broadcasted_iota(jnp.int32, (PEER_TOPK * PEER_TOPK, LANES), 0)
    n_chunks = tt // LANES

    def body(i, carry):
        h = i // n_chunks
        lane0 = pl.multiple_of((i % n_chunks) * LANES, LANES)
        s0 = st_ref[2 * h, :, pl.ds(lane0, LANES)]
        s1 = st_ref[2 * h + 1, :, pl.ds(lane0, LANES)]
        a = _top16_rows(s0, iota128)
        b = _top16_rows(s1, iota128)
        bmat = jnp.concatenate(b, axis=0)
        cand = jnp.concatenate([a[k] + bmat for k in range(PEER_TOPK)], axis=0)
        f = _top16_rows(cand, iota256)
        z = jnp.ones_like(f[0])
        for k in range(1, PEER_TOPK):
            z = z + jnp.exp(f[k] - f[0])
        s0_ref[h, :, pl.ds(lane0, LANES)] = s0
        s1_ref[h, :, pl.ds(lane0, LANES)] = s1
        e0_ref[h, :, pl.ds(lane0, LANES)] = jnp.exp(s0 - a[0]) / z
        e1_ref[h, :, pl.ds(lane0, LANES)] = jnp.exp(s1 - b[0])
        tau_ref[h, :, pl.ds(lane0, LANES)] = f[PEER_TOPK - 1]
        return carry

    lax.fori_loop(0, PEER_HEADS * n_chunks, body, 0)


def peer_router(q, keys_bf16, tt):
    t = q.shape[0]
    f32 = jnp.float32
    big = jax.ShapeDtypeStruct((PEER_HEADS, PEER_N_KEYS, t), f32)
    blk = pl.BlockSpec((PEER_HEADS, PEER_N_KEYS, tt), lambda i: (0, 0, i))
    return pl.pallas_call(
        _router_kernel,
        grid=(t // tt,),
        in_specs=[pl.BlockSpec((tt, q.shape[1]), lambda i: (i, 0)),
                  pl.BlockSpec(keys_bf16.shape, lambda i: (0, 0, 0, 0))],
        out_specs=[blk, blk, blk, blk, pl.BlockSpec((PEER_HEADS, 1, tt), lambda i: (0, 0, i))],
        out_shape=[big, big, big, big, jax.ShapeDtypeStruct((PEER_HEADS, 1, t), f32)],
        scratch_shapes=[pltpu.VMEM((2 * PEER_HEADS, PEER_N_KEYS, tt), f32)],
        compiler_params=pltpu.CompilerParams(dimension_semantics=("parallel",),
                                             vmem_limit_bytes=VMEM_LIMIT_BYTES),
        name="peer_router",
    )(q, keys_bf16)


def _dense_kernel(x_ref, u_ref, vt_ref, s0_ref, s1_ref, e0_ref, e1_ref, tau_ref, res_ref, gate_ref,
                  o_ref, at_ref, wt_ref, acc_ref):
    e = pl.program_id(1)
    n_e = pl.num_programs(1)
    eb = u_ref.shape[0]
    tt = x_ref.shape[0]
    n_i1 = eb // PEER_N_KEYS
    n_chunks = tt // LANES

    @pl.when(e == 0)
    def _():
        acc_ref[...] = jnp.zeros_like(acc_ref)

    at_ref[...] = lax.dot_general(u_ref[...], x_ref[...], (((1,), (1,)), ((), ())),
                                  preferred_element_type=jnp.float32)

    grp0 = pl.multiple_of((e * n_i1 // SUBLANES) * SUBLANES, SUBLANES)
    upper_half = (e * n_i1) % SUBLANES != 0

    def body(c, carry):
        lane0 = pl.multiple_of(c * LANES, LANES)
        s0g = [s0_ref[h, pl.ds(grp0, SUBLANES), pl.ds(lane0, LANES)] for h in range(PEER_HEADS)]
        e0g = [e0_ref[h, pl.ds(grp0, SUBLANES), pl.ds(lane0, LANES)] for h in range(PEER_HEADS)]
        for i1 in range(n_i1):
            g = jnp.zeros((PEER_N_KEYS, LANES), jnp.float32)
            for h in range(PEER_HEADS):
                s0row = jnp.where(upper_half, s0g[h][n_i1 + i1:n_i1 + i1 + 1], s0g[h][i1:i1 + 1])
                e0row = jnp.where(upper_half, e0g[h][n_i1 + i1:n_i1 + i1 + 1], e0g[h][i1:i1 + 1])
                taurow = tau_ref[h, :, pl.ds(lane0, LANES)]
                s1 = s1_ref[h, :, pl.ds(lane0, LANES)]
                e1 = e1_ref[h, :, pl.ds(lane0, LANES)]
                g = g + jnp.where(s0row + s1 >= taurow, e0row * e1, 0.0)
            a = at_ref[i1 * PEER_N_KEYS:(i1 + 1) * PEER_N_KEYS, pl.ds(lane0, LANES)]
            wt_ref[i1 * PEER_N_KEYS:(i1 + 1) * PEER_N_KEYS, pl.ds(lane0, LANES)] = (
                _gelu_tanh(a) * g).astype(jnp.bfloat16)
        return carry

    lax.fori_loop(0, n_chunks, body, 0)

    acc_ref[...] += jnp.dot(vt_ref[...], wt_ref[...], preferred_element_type=jnp.float32)

    @pl.when(e == n_e - 1)
    def _():
        o_ref[...] = res_ref[...] + gate_ref[0] * acc_ref[...].T


def peer_dense(x_bf16, u_bf16, vt_bf16, s0, s1, e0, e1, tau, resid, gate, tokens_per_gate, tt, eb):
    t, d = x_bf16.shape
    n_exp = u_bf16.shape[0]
    assert t % tt == 0 and n_exp % eb == 0 and tokens_per_gate % tt == 0
    assert 2 * (eb // PEER_N_KEYS) == SUBLANES
    tiles_per_gate = tokens_per_gate // tt
    rblk = pl.BlockSpec((PEER_HEADS, PEER_N_KEYS, tt), lambda i, j: (0, 0, i))
    return pl.pallas_call(
        _dense_kernel,
        grid=(t // tt, n_exp // eb),
        in_specs=[pl.BlockSpec((tt, d), lambda i, j: (i, 0)),
                  pl.BlockSpec((eb, d), lambda i, j: (j, 0)),
                  pl.BlockSpec((d, eb), lambda i, j: (0, j)),
                  rblk, rblk, rblk, rblk,
                  pl.BlockSpec((PEER_HEADS, 1, tt), lambda i, j: (0, 0, i)),
                  pl.BlockSpec((tt, d), lambda i, j: (i, 0)),
                  pl.BlockSpec((1, 1, d), lambda i, j: (i // tiles_per_gate, 0, 0))],
        out_specs=pl.BlockSpec((tt, d), lambda i, j: (i, 0)),
        out_shape=jax.ShapeDtypeStruct((t, d), jnp.float32),
        scratch_shapes=[pltpu.VMEM((eb, tt), jnp.float32),
                        pltpu.VMEM((eb, tt), jnp.bfloat16),
                        pltpu.VMEM((d, tt), jnp.float32)],
        compiler_params=pltpu.CompilerParams(dimension_semantics=("parallel", "arbitrary"),
                                             vmem_limit_bytes=VMEM_LIMIT_BYTES),
        name="peer_dense",
    )(x_bf16, u_bf16, vt_bf16, s0, s1, e0, e1, tau, resid, gate)


PEER_TOKEN_TILE = 512
PEER_EXPERT_TILE = 512


def peer_layer(x, norm_g, mods, w_q, keys, u_bf16, vt_bf16):
    b, l, d = x.shape
    h = modulate(rmsnorm(x, norm_g), mods[3], mods[4]).reshape(b * l, d)
    q = mm(h, w_q)
    gate = mods[5]
    tokens_per_gate = (b * l) // gate.shape[0]
    s0, s1, e0, e1, tau = peer_router(q, keys.astype(jnp.bfloat16), PEER_TOKEN_TILE)
    out = peer_dense(h.astype(jnp.bfloat16), u_bf16, vt_bf16, s0, s1, e0, e1, tau,
                     x.reshape(b * l, d), gate, tokens_per_gate, PEER_TOKEN_TILE, PEER_EXPERT_TILE)
    return out.reshape(b, l, d)


def kernel(x_prompt, x_sample, c, cache_na_k, cache_na_v, cache_diff_k, cache_diff_v, state_s5_re, state_s5_im, c_ctx, mod_w, mod_b, norm_mix_g, norm_ffn_g, ev_w_in, ev_w_out, na_q_g, na_k_g, na_rpb, hy_conv_w, hy_conv_b, hy_w1, hy_b1, hy_f1, hy_w2, hy_b2, hy_f2, hy_w3, hy_decay, hy_bias, od_w_in, od_w_out, s5_lam_re, s5_lam_im, s5_log_dt, s5_b_re, s5_b_im, s5_c_re, s5_c_im, s5_d, s5_glu_w, s5_glu_b, diff_q_g, diff_k_g, diff_lq1, diff_lk1, diff_lq2, diff_lk2, diff_subln_g, peer_w_q, peer_keys, peer_u, peer_v):
    rope = grid_rope(x_sample.shape[1])
    xp, xs = x_prompt, x_sample
    new_na_k, new_na_v, new_dk, new_dv, new_sr, new_si = [], [], [], [], [], []
    for i in range(DEPTH):
        mp = adaln(c_ctx[None, :], mod_w[i], mod_b[i])
        ms = adaln(c, mod_w[i], mod_b[i])
        hp = modulate(rmsnorm(xp, norm_mix_g[i]), mp[0], mp[1])
        hs = modulate(rmsnorm(xs, norm_mix_g[i]), ms[0], ms[1])
        j = i // 2
        if i % 2 == 0:
            ev = (ev_w_in[j], ev_w_out[j], na_q_g[j], na_k_g[j], na_rpb[j], hy_conv_w[j], hy_conv_b[j],
                  hy_w1[j], hy_b1[j], hy_f1[j], hy_w2[j], hy_b2[j], hy_f2[j], hy_w3[j], hy_decay[j], hy_bias[j])
            op, kp, vp = even_mixer(hp, ev, None)
            os_, _, _ = even_mixer(hs, ev, (cache_na_k[:, j], cache_na_v[:, j]))
            new_na_k.append(kp)
            new_na_v.append(vp)
        else:
            lam_init = 0.8 - 0.6 * math.exp(-0.3 * i)
            s5_prep = s5_prepare(s5_lam_re[j], s5_lam_im[j], s5_log_dt[j], s5_b_re[j], s5_b_im[j],
                                 s5_c_re[j], s5_c_im[j])
            od = (od_w_in[j], od_w_out[j], s5_prep, s5_d[j], s5_glu_w[j], s5_glu_b[j], diff_q_g[j], diff_k_g[j],
                  diff_lq1[j], diff_lk1[j], diff_lq2[j], diff_lk2[j], diff_subln_g[j])
            op, kp, vp, sr, si = odd_mixer(hp, od, lam_init, None, None, None, None)
            os_, _, _, _, _ = odd_mixer(hs, od, lam_init, state_s5_re[:, j], state_s5_im[:, j],
                                        (cache_diff_k[:, j], cache_diff_v[:, j]), rope)
            new_dk.append(kp)
            new_dv.append(vp)
            new_sr.append(sr)
            new_si.append(si)
        xp = xp + mp[2] * op
        xs = xs + ms[2] * os_
        u_bf16 = peer_u[i].astype(jnp.bfloat16)
        vt_bf16 = peer_v[i].T.astype(jnp.bfloat16)
        xp = peer_layer(xp, norm_ffn_g[i], mp, peer_w_q[i], peer_keys[i], u_bf16, vt_bf16)
        xs = peer_layer(xs, norm_ffn_g[i], ms, peer_w_q[i], peer_keys[i], u_bf16, vt_bf16)
    return (xp, xs, jnp.stack(new_na_k, axis=1), jnp.stack(new_na_v, axis=1),
            jnp.stack(new_dk, axis=1), jnp.stack(new_dv, axis=1),
            jnp.stack(new_sr, axis=1), jnp.stack(new_si, axis=1))
```

```python
import functools
import math

import jax
import jax.numpy as jnp
from jax import lax
from jax.experimental import pallas as pl
from jax.experimental.pallas import tpu as pltpu

D_MODEL = 2048
DEPTH = 2
GRID_W = 64
MIX_WIDTH = D_MODEL // 2
HEAD_DIM = 128
NA_HEADS = MIX_WIDTH // HEAD_DIM
NA_KH = 8
NA_KW = 16
HY_WIDTH = MIX_WIDTH
HY_ORDER = 2
HY_POS_EMB = 33
S5_WIDTH = MIX_WIDTH
S5_GROUP = 16
S5_GROUPS = S5_WIDTH // S5_GROUP
S5_STATE = 64
DIFF_HEADS = MIX_WIDTH // HEAD_DIM
DIFF_QK_DIM = HEAD_DIM // 2
DIFF_V_DIM = HEAD_DIM
ROPE_BASE = 10000.0
PEER_HEADS = 8
PEER_N_KEYS = 128
PEER_QUERY_DIM = 256
PEER_TOPK = 16
NORM_EPS = 1e-6
NEG_INF = -1e30

LANES = 128
SUBLANES = 8
VMEM_LIMIT_BYTES = 56 * 1024 * 1024


def _mm_kernel(x_ref, w_ref, o_ref):
    o_ref[...] = jnp.dot(x_ref[...].astype(jnp.bfloat16), w_ref[...].astype(jnp.bfloat16),
                         preferred_element_type=jnp.float32)


def mm(x, w, tm=1024, tn=512):
    m, k = x.shape
    _, n = w.shape
    tm = min(tm, m)
    tn = min(tn, n)
    assert m % tm == 0 and n % tn == 0
    return pl.pallas_call(
        _mm_kernel,
        grid=(m // tm, n // tn),
        in_specs=[pl.BlockSpec((tm, k), lambda i, j: (i, 0)),
                  pl.BlockSpec((k, tn), lambda i, j: (0, j))],
        out_specs=pl.BlockSpec((tm, tn), lambda i, j: (i, j)),
        out_shape=jax.ShapeDtypeStruct((m, n), jnp.float32),
        compiler_params=pltpu.CompilerParams(
            dimension_semantics=("parallel", "arbitrary"),
            vmem_limit_bytes=VMEM_LIMIT_BYTES),
    )(x, w)


def rmsnorm(x, g):
    xf = x.astype(jnp.float32)
    y = xf * lax.rsqrt(jnp.mean(xf * xf, axis=-1, keepdims=True) + NORM_EPS)
    return (y * g.astype(jnp.float32)).astype(x.dtype)


def adaln(cond, w, b):
    m = jax.nn.silu(cond) @ w + b
    return jnp.split(m[:, None, :], 6, axis=-1)


def modulate(h, shift, scale):
    return h * (1.0 + scale) + shift


def hyena_filters(l, w1, b1, f1, w2, b2, f2, w3, decay):
    t = jnp.linspace(0.0, 1.0, l, dtype=jnp.float32)[:, None]
    bands = (HY_POS_EMB - 1) // 2
    w_ang = 2.0 * math.pi * jnp.arange(l, dtype=jnp.float32)[:, None] / l
    freqs = jnp.linspace(1e-4, bands - 1, bands, dtype=jnp.float32)[None, :]
    z = jnp.concatenate([t, jnp.cos(freqs * w_ang), -jnp.sin(freqs * w_ang)], axis=-1)
    h = jnp.sin(f1 * (z @ w1 + b1))
    h = jnp.sin(f2 * (h @ w2 + b2))
    h = (h @ w3).reshape(l, 2, HY_ORDER, HY_WIDTH).astype(jnp.float32)
    h = h * jnp.exp(-t.reshape(l, 1, 1, 1) * jnp.abs(decay.astype(jnp.float32)))
    h_f, h_b = h[:, 0], h[:, 1]
    zero = jnp.zeros((1, HY_ORDER, HY_WIDTH), jnp.float32)
    return jnp.concatenate([h_f, zero, h_b[1:][::-1]], axis=0)


def hyena(u, conv_w, conv_b, filt, bias):
    b, l, _ = u.shape
    up = jnp.pad(u, ((0, 0), (1, 1), (0, 0)))
    u = up[:, :-2] * conv_w[0] + up[:, 1:-1] * conv_w[1] + up[:, 2:] * conv_w[2] + conv_b
    v, x1, x2 = jnp.split(u, 3, axis=-1)
    kf = jnp.fft.rfft(hyena_filters(l, *filt), axis=0)
    z = v
    for o, gate in enumerate((x1, x2)):
        zf = z.astype(jnp.float32)
        y = jnp.fft.irfft(jnp.fft.rfft(zf, n=2 * l, axis=1) * kf[None, :, o], n=2 * l, axis=1)[:, :l]
        z = gate * (y + zf * bias[o].astype(jnp.float32)).astype(u.dtype)
    return z


S5_TILE_GROUPS = 8
S5_TILES = S5_GROUPS // S5_TILE_GROUPS
S5_TILE_IN = S5_TILE_GROUPS * S5_GROUP
S5_TILE_STATE = S5_TILE_GROUPS * S5_STATE
S5_ROWS_PER_STEP = 2048


def s5_prepare(lam_re, lam_im, log_dt, b_re, b_im, c_re, c_im):
    f32 = jnp.float32
    lam_re, lam_im = lam_re.astype(f32), lam_im.astype(f32)
    dt = jnp.exp(log_dt.astype(f32))[..., None]
    mag = jnp.exp(lam_re * dt)
    ar, ai = mag * jnp.cos(lam_im * dt), mag * jnp.sin(lam_im * dt)
    den = lam_re * lam_re + lam_im * lam_im
    cr = ((ar - 1.0) * lam_re + ai * lam_im) / den
    ci = (ai * lam_re - (ar - 1.0) * lam_im) / den
    bbr = cr[..., None] * b_re - ci[..., None] * b_im
    bbi = cr[..., None] * b_im + ci[..., None] * b_re
    eye = jnp.eye(S5_TILE_GROUPS, dtype=f32)

    def tile_in(bb):
        x = bb.reshape(2, S5_TILES, S5_TILE_GROUPS, S5_STATE, S5_GROUP)
        x = jnp.einsum('dtgnp,gh->dtgphn', x, eye)
        return x.reshape(2, S5_TILES, S5_TILE_IN, S5_TILE_STATE)

    def tile_out(cc):
        x = cc.astype(f32).reshape(2, S5_TILES, S5_TILE_GROUPS, S5_GROUP, S5_STATE)
        x = jnp.einsum('dtgpn,gh->dtgnhp', x, eye)
        return x.reshape(2, S5_TILES, S5_TILE_STATE, S5_TILE_IN)

    win = jnp.concatenate([tile_in(bbr), tile_in(bbi)], axis=-1).astype(jnp.bfloat16)
    wout = jnp.concatenate([tile_out(c_re), -tile_out(c_im)], axis=-2).astype(jnp.bfloat16)
    a = jnp.stack([ar.reshape(2, S5_TILES, S5_TILE_STATE), ai.reshape(2, S5_TILES, S5_TILE_STATE)], axis=2)
    return win, wout, a


def _s5_scan_kernel(u_ref, win_ref, wout_ref, a_ref, h0_ref, y_ref, fin_ref, bu_ref, st_ref, *, batch):
    d = pl.program_id(0)
    c = pl.program_id(2)
    n_c = pl.num_programs(2)
    ns = S5_TILE_STATE
    steps = u_ref.shape[0] // batch

    @pl.when(c == 0)
    def _():
        st_ref[...] = h0_ref[0, 0]

    bu_ref[...] = jnp.dot(u_ref[...].astype(jnp.bfloat16), win_ref[0, 0], preferred_element_type=jnp.float32)
    ar = jnp.broadcast_to(a_ref[0, 0, 0:1, :], (batch, ns))
    ai = jnp.broadcast_to(a_ref[0, 0, 1:2, :], (batch, ns))

    def step(t, carry):
        hr, hi = carry
        te = jnp.where(d == 0, t, steps - 1 - t)
        r0 = pl.multiple_of(te * batch, batch)
        nr = ar * hr - ai * hi + bu_ref[pl.ds(r0, batch), 0:ns]
        ni = ar * hi + ai * hr + bu_ref[pl.ds(r0, batch), ns:2 * ns]
        bu_ref[pl.ds(r0, batch), 0:ns] = nr
        bu_ref[pl.ds(r0, batch), ns:2 * ns] = ni
        return nr, ni

    hr, hi = lax.fori_loop(0, steps, step, (st_ref[:, 0:ns], st_ref[:, ns:2 * ns]), unroll=4)
    st_ref[:, 0:ns] = hr
    st_ref[:, ns:2 * ns] = hi
    y_ref[0] = jnp.dot(bu_ref[...].astype(jnp.bfloat16), wout_ref[0, 0], preferred_element_type=jnp.float32)

    @pl.when(c == n_c - 1)
    def _():
        fin_ref[0, 0] = st_ref[...]


def s5_scan(u_tm, win, wout, a, h0, batch):
    rows = u_tm.shape[0]
    r = min(S5_ROWS_PER_STEP, rows)
    assert rows % r == 0 and r % batch == 0
    n_c = rows // r

    def chunk(d, c):
        return c + d * (n_c - 1 - 2 * c)

    return pl.pallas_call(
        functools.partial(_s5_scan_kernel, batch=batch),
        grid=(2, S5_TILES, n_c),
        in_specs=[pl.BlockSpec((r, S5_TILE_IN), lambda d, j, c: (chunk(d, c), j)),
                  pl.BlockSpec((1, 1, S5_TILE_IN, 2 * S5_TILE_STATE), lambda d, j, c: (d, j, 0, 0)),
                  pl.BlockSpec((1, 1, 2 * S5_TILE_STATE, S5_TILE_IN), lambda d, j, c: (d, j, 0, 0)),
                  pl.BlockSpec((1, 1, 2, S5_TILE_STATE), lambda d, j, c: (d, j, 0, 0)),
                  pl.BlockSpec((1, 1, batch, 2 * S5_TILE_STATE), lambda d, j, c: (d, j, 0, 0))],
        out_specs=[pl.BlockSpec((1, r, S5_TILE_IN), lambda d, j, c: (d, chunk(d, c), j)),
                   pl.BlockSpec((1, 1, batch, 2 * S5_TILE_STATE), lambda d, j, c: (d, j, 0, 0))],
        out_shape=[jax.ShapeDtypeStruct((2, rows, S5_WIDTH), jnp.float32),
                   jax.ShapeDtypeStruct((2, S5_TILES, batch, 2 * S5_TILE_STATE), jnp.float32)],
        scratch_shapes=[pltpu.VMEM((r, 2 * S5_TILE_STATE), jnp.float32),
                        pltpu.VMEM((batch, 2 * S5_TILE_STATE), jnp.float32)],
        compiler_params=pltpu.CompilerParams(dimension_semantics=("parallel", "parallel", "arbitrary"),
                                             vmem_limit_bytes=VMEM_LIMIT_BYTES),
        name="s5_scan",
    )(u_tm, win, wout, a, h0)


def _gelu_tanh(x):
    return 0.5 * x * (1.0 + jnp.tanh(math.sqrt(2.0 / math.pi) * (x + 0.044715 * (x * x * x))))


def _s5_glu_kernel(u_ref, y_ref, d_ref, w_ref, b_ref, o_ref):
    y = d_ref[...] * u_ref[...] + y_ref[0] + y_ref[1]
    y = _gelu_tanh(y)
    z = jnp.dot(y.astype(jnp.bfloat16), w_ref[...], preferred_element_type=jnp.float32) + b_ref[...]
    o_ref[...] = y * (1.0 / (1.0 + jnp.exp(-z)))


def s5_glu(u_tm, y, d_skip, glu_w_bf16, glu_b, tr=1024):
    rows, w = u_tm.shape
    tr = min(tr, rows)
    assert rows % tr == 0
    return pl.pallas_call(
        _s5_glu_kernel,
        grid=(rows // tr,),
        in_specs=[pl.BlockSpec((tr, w), lambda i: (i, 0)),
                  pl.BlockSpec((2, tr, w), lambda i: (0, i, 0)),
                  pl.BlockSpec((1, w), lambda i: (0, 0)),
                  pl.BlockSpec((w, w), lambda i: (0, 0)),
                  pl.BlockSpec((1, w), lambda i: (0, 0))],
        out_specs=pl.BlockSpec((tr, w), lambda i: (i, 0)),
        out_shape=jax.ShapeDtypeStruct((rows, w), jnp.float32),
        compiler_params=pltpu.CompilerParams(dimension_semantics=("parallel",),
                                             vmem_limit_bytes=VMEM_LIMIT_BYTES),
        name="s5_glu",
    )(u_tm, y, d_skip.reshape(1, w), glu_w_bf16, glu_b.reshape(1, w))


def s5_mixer(u, h0_re, h0_im, prep, d_skip, glu_w, glu_b):
    win, wout, a = prep
    b, l, w = u.shape
    u_tm = jnp.swapaxes(u, 0, 1).reshape(l * b, w)
    if h0_re is None:
        h0 = jnp.zeros((2, S5_TILES, b, 2 * S5_TILE_STATE), jnp.float32)
    else:
        def tiles(h):
            return jnp.transpose(h.astype(jnp.float32).reshape(b, 2, S5_TILES, S5_TILE_STATE), (1, 2, 0, 3))
        h0 = jnp.concatenate([tiles(h0_re), tiles(h0_im)], axis=-1)
    y, fin = s5_scan(u_tm, win, wout, a, h0, b)
    out_tm = s5_glu(u_tm, y, d_skip, glu_w.astype(jnp.bfloat16), glu_b)
    out = jnp.swapaxes(out_tm.reshape(l, b, w), 0, 1)

    def untile(f):
        return jnp.transpose(f, (2, 0, 1, 3)).reshape(b, 2, S5_GROUPS, S5_STATE)
    return out, untile(fin[..., :S5_TILE_STATE]), untile(fin[..., S5_TILE_STATE:])


def _half_mean_matrix():
    i = lax.broadcasted_iota(jnp.int32, (LANES, LANES), 0) // DIFF_QK_DIM
    j = lax.broadcasted_iota(jnp.int32, (LANES, LANES), 1) // DIFF_QK_DIM
    return jnp.where(i == j, 1.0 / DIFF_QK_DIM, 0.0).astype(jnp.bfloat16)


def _rms_groups(x, gain, avg):
    xx = x * x
    hi = xx.astype(jnp.bfloat16)
    lo = (xx - hi.astype(jnp.float32)).astype(jnp.bfloat16)
    ms = (jnp.dot(hi, avg, preferred_element_type=jnp.float32)
          + jnp.dot(lo, avg, preferred_element_type=jnp.float32))
    return x * lax.rsqrt(ms + NORM_EPS) * gain


def _rope_lanes(y, cos, sin_signed, first_half):
    rot = jnp.where(first_half, pltpu.roll(y, LANES - 16, 1), pltpu.roll(y, 16, 1))
    return y * cos + rot * sin_signed


def _diff_prep_kernel(q_ref, k_ref, qg_ref, kg_ref, cos_ref, sin_ref, qo_ref, ko_ref, kn_ref, *, use_rope):
    avg = _half_mean_matrix()
    lane = lax.broadcasted_iota(jnp.int32, (1, LANES), 1)
    first_half = (lane % 32) < 16
    scale = DIFF_QK_DIM ** -0.5
    for h in range(DIFF_HEADS):
        cols = slice(h * LANES, (h + 1) * LANES)
        qn = _rms_groups(q_ref[:, cols], qg_ref[...], avg)
        kn = _rms_groups(k_ref[:, cols], kg_ref[...], avg)
        kn_ref[:, cols] = kn
        if use_rope:
            qn = _rope_lanes(qn, cos_ref[...], sin_ref[...], first_half)
            kn = _rope_lanes(kn, cos_ref[...], sin_ref[...], first_half)
        qo_ref[:, cols] = (qn * scale).astype(jnp.bfloat16)
        ko_ref[:, cols] = kn.astype(jnp.bfloat16)


def diff_prep(p, q_g, k_g, cos, sin_signed, seq_len, use_rope, tr=512):
    rows = p.shape[0]
    tr = min(tr, seq_len)
    assert seq_len % tr == 0
    w = DIFF_HEADS * LANES
    per_seq = seq_len // tr
    g2 = lambda g: jnp.tile(g.astype(jnp.float32), 2).reshape(1, LANES)
    return pl.pallas_call(
        functools.partial(_diff_prep_kernel, use_rope=use_rope),
        grid=(rows // tr,),
        in_specs=[pl.BlockSpec((tr, w), lambda i: (i, 1)),
                  pl.BlockSpec((tr, w), lambda i: (i, 2)),
                  pl.BlockSpec((1, LANES), lambda i: (0, 0)),
                  pl.BlockSpec((1, LANES), lambda i: (0, 0)),
                  pl.BlockSpec((tr, LANES), lambda i: (i % per_seq, 0)),
                  pl.BlockSpec((tr, LANES), lambda i: (i % per_seq, 0))],
        out_specs=[pl.BlockSpec((tr, w), lambda i: (i, 0))] * 3,
        out_shape=[jax.ShapeDtypeStruct((rows, w), jnp.bfloat16), jax.ShapeDtypeStruct((rows, w), jnp.bfloat16),
                   jax.ShapeDtypeStruct((rows, w), jnp.float32)],
        compiler_params=pltpu.CompilerParams(dimension_semantics=("parallel",),
                                             vmem_limit_bytes=VMEM_LIMIT_BYTES),
        name="diff_prep",
    )(p, p, g2(q_g), g2(k_g), cos, sin_signed)


def rope_tables(l):
    t = jnp.arange(l)
    row = (t // GRID_W).astype(jnp.float32)
    col = (t % GRID_W).astype(jnp.float32)
    nf = DIFF_QK_DIM // 4
    inv = ROPE_BASE ** (-jnp.arange(nf, dtype=jnp.float32) / nf)
    ang = jnp.stack([row[:, None] * inv, col[:, None] * inv], axis=1)
    ang = jnp.stack([ang, ang], axis=2).reshape(l, DIFF_QK_DIM)
    sign = jnp.where((jnp.arange(DIFF_QK_DIM) % 32) < 16, -1.0, 1.0)
    return jnp.tile(jnp.cos(ang), (1, 2)), jnp.tile(jnp.sin(ang) * sign, (1, 2))


def _diff_attn_kernel(*refs, has_ctx):
    if has_ctx:
        q_ref, k_ref, v_ref, ck_ref, cv_ref, lam_ref, g_ref, o_ref = refs
    else:
        q_ref, k_ref, v_ref, lam_ref, g_ref, o_ref = refs
    q = q_ref[...]
    lane = lax.broadcasted_iota(jnp.int32, (1, LANES), 1)
    zero = jnp.zeros_like(q)
    qs = (jnp.where(lane < DIFF_QK_DIM, q, zero), jnp.where(lane >= DIFF_QK_DIM, q, zero))
    nt = (((1,), (1,)), ((), ()))
    k = k_ref[...]
    ck = ck_ref[...].astype(jnp.bfloat16) if has_ctx else None
    lam = lam_ref[0:1, 0:1]
    w_self, w_ctx = None, None
    for i in range(2):
        s = lax.dot_general(qs[i], k, nt, preferred_element_type=jnp.float32)
        m = jnp.max(s, axis=-1, keepdims=True)
        if has_ctx:
            sc = lax.dot_general(qs[i], ck, nt, preferred_element_type=jnp.float32)
            m = jnp.maximum(m, jnp.max(sc, axis=-1, keepdims=True))
        p = jnp.exp(s - m)
        l = jnp.sum(p, axis=-1, keepdims=True)
        if has_ctx:
            pc = jnp.exp(sc - m)
            l = l + jnp.sum(pc, axis=-1, keepdims=True)
        coef = 1.0 / l if i == 0 else -lam / l
        w_self = p * coef if i == 0 else w_self + p * coef
        if has_ctx:
            w_ctx = pc * coef if i == 0 else w_ctx + pc * coef
    o = jnp.dot(w_self.astype(jnp.bfloat16), v_ref[...].astype(jnp.bfloat16), preferred_element_type=jnp.float32)
    if has_ctx:
        o = o + jnp.dot(w_ctx.astype(jnp.bfloat16), cv_ref[...].astype(jnp.bfloat16),
                        preferred_element_type=jnp.float32)
    ms = jnp.mean(o * o, axis=-1, keepdims=True)
    o_ref[...] = o * lax.rsqrt(ms + NORM_EPS) * g_ref[...]


def diff_attention(q_bf16, k_bf16, p, ctx_k, ctx_v, lam, gain, batch, seq_len, tq=256):
    tq = min(tq, seq_len)
    nq = seq_len // tq
    has_ctx = ctx_k is not None
    v_col0 = 3 * DIFF_HEADS
    in_specs = [pl.BlockSpec((tq, LANES), lambda b, h, i: (b * nq + i, h)),
                pl.BlockSpec((seq_len, LANES), lambda b, h, i: (b, h)),
                pl.BlockSpec((seq_len, LANES), lambda b, h, i: (b, v_col0 + h))]
    args = [q_bf16, k_bf16, p]
    if has_ctx:
        lc = ctx_k.shape[0] // batch
        in_specs += [pl.BlockSpec((lc, LANES), lambda b, h, i: (b, h))] * 2
        args += [ctx_k, ctx_v]
    in_specs += [pl.BlockSpec((1, LANES), lambda b, h, i: (0, 0))] * 2
    args += [jnp.broadcast_to(lam.astype(jnp.float32), (1, LANES)), gain.astype(jnp.float32).reshape(1, LANES)]
    return pl.pallas_call(
        functools.partial(_diff_attn_kernel, has_ctx=has_ctx),
        grid=(batch, DIFF_HEADS, nq),
        in_specs=in_specs,
        out_specs=pl.BlockSpec((tq, LANES), lambda b, h, i: (b * nq + i, h)),
        out_shape=jax.ShapeDtypeStruct((batch * seq_len, DIFF_HEADS * LANES), jnp.float32),
        compiler_params=pltpu.CompilerParams(dimension_semantics=("parallel", "parallel", "arbitrary"),
                                             vmem_limit_bytes=VMEM_LIMIT_BYTES),
        name="diff_attention",
    )(*args)


NA_Q_ROWS = 8
NA_WIN_ROWS = 16


def _na_prep_kernel(q_ref, k_ref, qg_ref, kg_ref, qo_ref, ko_ref, kn_ref):
    avg = jnp.full((LANES, LANES), 1.0 / HEAD_DIM, jnp.bfloat16)
    for h in range(NA_HEADS):
        cols = slice(h * LANES, (h + 1) * LANES)
        qo_ref[:, cols] = _rms_groups(q_ref[:, cols], qg_ref[...], avg).astype(jnp.bfloat16)
        kn = _rms_groups(k_ref[:, cols], kg_ref[...], avg)
        kn_ref[:, cols] = kn
        ko_ref[:, cols] = kn.astype(jnp.bfloat16)


def na_prep(p, q_g, k_g, tr=512):
    rows = p.shape[0]
    tr = min(tr, rows)
    w = NA_HEADS * LANES
    g1 = lambda g: g.astype(jnp.float32).reshape(1, LANES)
    return pl.pallas_call(
        _na_prep_kernel,
        grid=(rows // tr,),
        in_specs=[pl.BlockSpec((tr, w), lambda i: (i, 0)),
                  pl.BlockSpec((tr, w), lambda i: (i, 1)),
                  pl.BlockSpec((1, LANES), lambda i: (0, 0)),
                  pl.BlockSpec((1, LANES), lambda i: (0, 0))],
        out_specs=[pl.BlockSpec((tr, w), lambda i: (i, 0))] * 3,
        out_shape=[jax.ShapeDtypeStruct((rows, w), jnp.bfloat16), jax.ShapeDtypeStruct((rows, w), jnp.bfloat16),
                   jax.ShapeDtypeStruct((rows, w), jnp.float32)],
        compiler_params=pltpu.CompilerParams(dimension_semantics=("parallel",),
                                             vmem_limit_bytes=VMEM_LIMIT_BYTES),
        name="na_prep",
    )(p, p, g1(q_g), g1(k_g))


def na_bias_tables(rpb, rows):
    nblk = rows // NA_Q_ROWS
    cq = jnp.arange(GRID_W)[:, None]
    ck = jnp.arange(GRID_W)[None, :]
    col_start = jnp.clip(cq - NA_KW // 2, 0, GRID_W - NA_KW)
    col_ok = (ck >= col_start) & (ck < col_start + NA_KW)
    cidx = jnp.clip(ck - cq + NA_KW - 1, 0, 2 * NA_KW - 2)
    tabs = []
    for i in (0, 1, nblk - 1):
        base = min(max(NA_Q_ROWS * i - NA_KH // 2, 0), rows - NA_WIN_ROWS)
        r = NA_Q_ROWS * i + jnp.arange(NA_Q_ROWS)[:, None]
        rk = base + jnp.arange(NA_WIN_ROWS)[None, :]
        r0 = jnp.clip(r - NA_KH // 2, 0, rows - NA_KH)
        row_ok = (rk >= r0) & (rk < r0 + NA_KH)
        ridx = jnp.clip(rk - r + NA_KH - 1, 0, 2 * NA_KH - 2)
        b = rpb[:, ridx][:, :, :, cidx]
        ok = row_ok[:, :, None, None] & col_ok[None, None, :, :]
        b = jnp.where(ok[None], b.astype(jnp.float32), NEG_INF)
        tabs.append(jnp.transpose(b, (0, 1, 3, 2, 4)).reshape(NA_HEADS, NA_Q_ROWS * GRID_W, NA_WIN_ROWS * GRID_W))
    return jnp.stack(tabs, axis=0)


def _na_attn_kernel(*refs, windowed, rows):
    nt = (((1,), (1,)), ((), ()))
    scale = HEAD_DIM ** -0.5
    if windowed:
        q_ref, k_ref, v_ref, b_ref, ck_ref, cv_ref, o_ref = refs
        i = pl.program_id(2)
        base = jnp.clip(NA_Q_ROWS * i - NA_KH // 2, 0, rows - NA_WIN_ROWS)
        k0 = pl.multiple_of(base * GRID_W, GRID_W)
        nk = NA_WIN_ROWS * GRID_W
        q = q_ref[...]
        s = lax.dot_general(q, k_ref[pl.ds(k0, nk), :], nt, preferred_element_type=jnp.float32) * scale + b_ref[0, 0]
        sc = lax.dot_general(q, ck_ref[...].astype(jnp.bfloat16), nt, preferred_element_type=jnp.float32) * scale
        m = jnp.maximum(jnp.max(s, axis=-1, keepdims=True), jnp.max(sc, axis=-1, keepdims=True))
        p = jnp.exp(s - m)
        pc = jnp.exp(sc - m)
        l = jnp.sum(p, axis=-1, keepdims=True) + jnp.sum(pc, axis=-1, keepdims=True)
        o = (jnp.dot(p.astype(jnp.bfloat16), v_ref[pl.ds(k0, nk), :].astype(jnp.bfloat16),
                     preferred_element_type=jnp.float32)
             + jnp.dot(pc.astype(jnp.bfloat16), cv_ref[...].astype(jnp.bfloat16), preferred_element_type=jnp.float32))
    else:
        q_ref, k_ref, v_ref, o_ref = refs
        s = lax.dot_general(q_ref[...], k_ref[...], nt, preferred_element_type=jnp.float32) * scale
        m = jnp.max(s, axis=-1, keepdims=True)
        p = jnp.exp(s - m)
        l = jnp.sum(p, axis=-1, keepdims=True)
        o = jnp.dot(p.astype(jnp.bfloat16), v_ref[...].astype(jnp.bfloat16), preferred_element_type=jnp.float32)
    o_ref[...] = o / l


def na_attention(q_bf16, k_bf16, p, bias_tabs, ctx_k, ctx_v, batch, seq_len):
    windowed = bias_tabs is not None
    v_col0 = 2 * NA_HEADS
    rows = seq_len // GRID_W
    if windowed:
        tq = NA_Q_ROWS * GRID_W
        nq = seq_len // tq
        lc = ctx_k.shape[0] // batch
        nk = NA_WIN_ROWS * GRID_W
        in_specs = [pl.BlockSpec((tq, LANES), lambda b, h, i: (b * nq + i, h)),
                    pl.BlockSpec((seq_len, LANES), lambda b, h, i: (b, h)),
                    pl.BlockSpec((seq_len, LANES), lambda b, h, i: (b, v_col0 + h)),
                    pl.BlockSpec((1, 1, tq, nk),
                                 lambda b, h, i: (jnp.where(i == 0, 0, jnp.where(i == nq - 1, 2, 1)), h, 0, 0)),
                    pl.BlockSpec((lc, LANES), lambda b, h, i: (b, h)),
                    pl.BlockSpec((lc, LANES), lambda b, h, i: (b, h))]
        args = [q_bf16, k_bf16, p, bias_tabs, ctx_k, ctx_v]
    else:
        tq = seq_len
        nq = 1
        in_specs = [pl.BlockSpec((tq, LANES), lambda b, h, i: (b, h)),
                    pl.BlockSpec((seq_len, LANES), lambda b, h, i: (b, h)),
                    pl.BlockSpec((seq_len, LANES), lambda b, h, i: (b, v_col0 + h))]
        args = [q_bf16, k_bf16, p]
    return pl.pallas_call(
        functools.partial(_na_attn_kernel, windowed=windowed, rows=rows),
        grid=(batch, NA_HEADS, nq),
        in_specs=in_specs,
        out_specs=pl.BlockSpec((tq, LANES), lambda b, h, i: (b * nq + i, h)),
        out_shape=jax.ShapeDtypeStruct((batch * seq_len, NA_HEADS * LANES), jnp.float32),
        compiler_params=pltpu.CompilerParams(dimension_semantics=("parallel", "parallel", "arbitrary"),
                                             vmem_limit_bytes=VMEM_LIMIT_BYTES),
        name="na_attention",
    )(*args)


def even_mixer(h, b, l, ev, ctx_kv, bias_tabs):
    (w_in, w_out, q_g, k_g, conv_w, conv_b, w1, b1, f1, w2, b2, f2, w3, decay, bias) = ev
    p = mm(h, w_in)
    q, k, kn = na_prep(p, q_g, k_g)
    if ctx_kv is None:
        oa = na_attention(q, k, p, None, None, None, b, l)
    else:
        lc = ctx_kv[0].shape[1]
        oa = na_attention(q, k, p, bias_tabs, ctx_kv[0].reshape(b * lc, MIX_WIDTH),
                          ctx_kv[1].reshape(b * lc, MIX_WIDTH), b, l)
    hb = p[:, 3 * MIX_WIDTH:].reshape(b, l, 3 * HY_WIDTH)
    ob = hyena(hb, conv_w, conv_b, (w1, b1, f1, w2, b2, f2, w3, decay), bias)
    out = mm(jnp.concatenate([oa, ob.reshape(b * l, HY_WIDTH)], axis=-1), w_out)
    shp = (b, l, NA_HEADS, HEAD_DIM)
    return out, kn.reshape(shp), p[:, 2 * MIX_WIDTH:3 * MIX_WIDTH].reshape(shp)


def odd_mixer(h, b, l, od, lam_init, h0_re, h0_im, ctx_kv, rope):
    (w_in, w_out, s5_prep, d_skip, glu_w, glu_b, q_g, k_g, lq1, lk1, lq2, lk2, subln_g) = od
    p = mm(h, w_in)
    oc, fr, fi = s5_mixer(p[:, :S5_WIDTH].reshape(b, l, S5_WIDTH), h0_re, h0_im, s5_prep, d_skip, glu_w, glu_b)
    lam = (jnp.exp(jnp.sum(lq1 * lk1).astype(jnp.float32))
           - jnp.exp(jnp.sum(lq2 * lk2).astype(jnp.float32)) + lam_init)
    cos, sin_signed = rope
    q, k, kn = diff_prep(p, q_g, k_g, cos, sin_signed, l, use_rope=ctx_kv is not None)
    gain = subln_g.astype(jnp.float32) * (1.0 - lam_init)
    if ctx_kv is None:
        o = diff_attention(q, k, p, None, None, lam, gain, b, l)
    else:
        lc = ctx_kv[0].shape[1]
        o = diff_attention(q, k, p, ctx_kv[0].reshape(b * lc, MIX_WIDTH), ctx_kv[1].reshape(b * lc, MIX_WIDTH),
                           lam, gain, b, l)
    out = mm(jnp.concatenate([oc.reshape(b * l, S5_WIDTH), o], axis=-1), w_out)
    kn = kn.reshape(b, l, DIFF_HEADS, 2, DIFF_QK_DIM)
    v = p[:, S5_WIDTH + 2 * MIX_WIDTH:].reshape(b, l, DIFF_HEADS, DIFF_V_DIM)
    return out, kn, v, fr, fi


def _top16_rows(cur, iota):
    n = cur.shape[0]
    tops = []
    for _ in range(PEER_TOPK):
        m = jnp.max(cur, axis=0, keepdims=True)
        tops.append(m)
        first = jnp.min(jnp.where(cur == m, iota, n), axis=0, keepdims=True)
        cur = jnp.where(iota == first, -jnp.inf, cur)
    return tops


def _router_kernel(q_ref, keys_ref, s0_ref, s1_ref, e0_ref, e1_ref, tau_ref, st_ref):
    tt = q_ref.shape[0]
    half = PEER_QUERY_DIM // 2
    for h in range(PEER_HEADS):
        for s in range(2):
            qs = q_ref[:, (2 * h + s) * half:(2 * h + s + 1) * half].astype(jnp.bfloat16)
            st = lax.dot_general(keys_ref[h, s], qs, (((1,), (1,)), ((), ())),
                                 preferred_element_type=jnp.float32)
            st_ref[2 * h + s] = st

    iota128 = lax.broadcasted_iota(jnp.int32, (PEER_N_KEYS, LANES), 0)
    iota256 = lax.broadcasted_iota(jnp.int32, (PEER_TOPK * PEER_TOPK, LANES), 0)
    n_chunks = tt // LANES

    def body(i, carry):
        h = i // n_chunks
        lane0 = pl.multiple_of((i % n_chunks) * LANES, LANES)
        s0 = st_ref[2 * h, :, pl.ds(lane0, LANES)]
        s1 = st_ref[2 * h + 1, :, pl.ds(lane0, LANES)]
        a = _top16_rows(s0, iota128)
        b = _top16_rows(s1, iota128)
        bmat = jnp.concatenate(b, axis=0)
        cand = jnp.concatenate([a[k] + bmat for k in range(PEER_TOPK)], axis=0)
        f = _top16_rows(cand, iota256)
        z = jnp.ones_like(f[0])
        for k in range(1, PEER_TOPK):
            z = z + jnp.exp(f[k] - f[0])
        s0_ref[h, :, pl.ds(lane0, LANES)] = s0
        s1_ref[h, :, pl.ds(lane0, LANES)] = s1
        e0_ref[h, :, pl.ds(lane0, LANES)] = jnp.exp(s0 - a[0]) / z
        e1_ref[h, :, pl.ds(lane0, LANES)] = jnp.exp(s1 - b[0])
        tau_ref[h, :, pl.ds(lane0, LANES)] = f[PEER_TOPK - 1]
        return carry

    lax.fori_loop(0, PEER_HEADS * n_chunks, body, 0)


def peer_router(q, keys_bf16, tt):
    t = q.shape[0]
    f32 = jnp.float32
    big = jax.ShapeDtypeStruct((PEER_HEADS, PEER_N_KEYS, t), f32)
    blk = pl.BlockSpec((PEER_HEADS, PEER_N_KEYS, tt), lambda i: (0, 0, i))
    return pl.pallas_call(
        _router_kernel,
        grid=(t // tt,),
        in_specs=[pl.BlockSpec((tt, q.shape[1]), lambda i: (i, 0)),
                  pl.BlockSpec(keys_bf16.shape, lambda i: (0, 0, 0, 0))],
        out_specs=[blk, blk, blk, blk, pl.BlockSpec((PEER_HEADS, 1, tt), lambda i: (0, 0, i))],
        out_shape=[big, big, big, big, jax.ShapeDtypeStruct((PEER_HEADS, 1, t), f32)],
        scratch_shapes=[pltpu.VMEM((2 * PEER_HEADS, PEER_N_KEYS, tt), f32)],
        compiler_params=pltpu.CompilerParams(dimension_semantics=("parallel",),
                                             vmem_limit_bytes=VMEM_LIMIT_BYTES),
        name="peer_router",
    )(q, keys_bf16)


def _dense_kernel(x_ref, u_ref, vt_ref, s0_ref, s1_ref, e0_ref, e1_ref, tau_ref, res_ref, gate_ref,
                  o_ref, at_ref, wt_ref, acc_ref):
    e = pl.program_id(1)
    n_e = pl.num_programs(1)
    eb = u_ref.shape[0]
    tt = x_ref.shape[0]
    n_i1 = eb // PEER_N_KEYS
    n_chunks = tt // LANES

    @pl.when(e == 0)
    def _():
        acc_ref[...] = jnp.zeros_like(acc_ref)

    at_ref[...] = lax.dot_general(u_ref[...], x_ref[...], (((1,), (1,)), ((), ())),
                                  preferred_element_type=jnp.float32)

    grp0 = pl.multiple_of((e * n_i1 // SUBLANES) * SUBLANES, SUBLANES)
    upper_half = (e * n_i1) % SUBLANES != 0

    def body(c, carry):
        lane0 = pl.multiple_of(c * LANES, LANES)
        s0g = [s0_ref[h, pl.ds(grp0, SUBLANES), pl.ds(lane0, LANES)] for h in range(PEER_HEADS)]
        e0g = [e0_ref[h, pl.ds(grp0, SUBLANES), pl.ds(lane0, LANES)] for h in range(PEER_HEADS)]
        for i1 in range(n_i1):
            g = jnp.zeros((PEER_N_KEYS, LANES), jnp.float32)
            for h in range(PEER_HEADS):
                s0row = jnp.where(upper_half, s0g[h][n_i1 + i1:n_i1 + i1 + 1], s0g[h][i1:i1 + 1])
                e0row = jnp.where(upper_half, e0g[h][n_i1 + i1:n_i1 + i1 + 1], e0g[h][i1:i1 + 1])
                taurow = tau_ref[h, :, pl.ds(lane0, LANES)]
                s1 = s1_ref[h, :, pl.ds(lane0, LANES)]
                e1 = e1_ref[h, :, pl.ds(lane0, LANES)]
                g = g + jnp.where(s0row + s1 >= taurow, e0row * e1, 0.0)
            a = at_ref[i1 * PEER_N_KEYS:(i1 + 1) * PEER_N_KEYS, pl.ds(lane0, LANES)]
            wt_ref[i1 * PEER_N_KEYS:(i1 + 1) * PEER_N_KEYS, pl.ds(lane0, LANES)] = (
                _gelu_tanh(a) * g).astype(jnp.bfloat16)
        return carry

    lax.fori_loop(0, n_chunks, body, 0)

    acc_ref[...] += jnp.dot(vt_ref[...], wt_ref[...], preferred_element_type=jnp.float32)

    @pl.when(e == n_e - 1)
    def _():
        o_ref[...] = res_ref[...] + gate_ref[0] * acc_ref[...].T


def peer_dense(x_bf16, u_bf16, vt_bf16, s0, s1, e0, e1, tau, resid, gate, tokens_per_gate, tt, eb):
    t, d = x_bf16.shape
    n_exp = u_bf16.shape[0]
    assert t % tt == 0 and n_exp % eb == 0 and tokens_per_gate % tt == 0
    assert 2 * (eb // PEER_N_KEYS) == SUBLANES
    tiles_per_gate = tokens_per_gate // tt
    rblk = pl.BlockSpec((PEER_HEADS, PEER_N_KEYS, tt), lambda i, j: (0, 0, i))
    return pl.pallas_call(
        _dense_kernel,
        grid=(t // tt, n_exp // eb),
        in_specs=[pl.BlockSpec((tt, d), lambda i, j: (i, 0)),
                  pl.BlockSpec((eb, d), lambda i, j: (j, 0)),
                  pl.BlockSpec((d, eb), lambda i, j: (0, j)),
                  rblk, rblk, rblk, rblk,
                  pl.BlockSpec((PEER_HEADS, 1, tt), lambda i, j: (0, 0, i)),
                  pl.BlockSpec((tt, d), lambda i, j: (i, 0)),
                  pl.BlockSpec((1, 1, d), lambda i, j: (i // tiles_per_gate, 0, 0))],
        out_specs=pl.BlockSpec((tt, d), lambda i, j: (i, 0)),
        out_shape=jax.ShapeDtypeStruct((t, d), jnp.float32),
        scratch_shapes=[pltpu.VMEM((eb, tt), jnp.float32),
                        pltpu.VMEM((eb, tt), jnp.bfloat16),
                        pltpu.VMEM((d, tt), jnp.float32)],
        compiler_params=pltpu.CompilerParams(dimension_semantics=("parallel", "arbitrary"),
                                             vmem_limit_bytes=VMEM_LIMIT_BYTES),
        name="peer_dense",
    )(x_bf16, u_bf16, vt_bf16, s0, s1, e0, e1, tau, resid, gate)


PEER_TOKEN_TILE = 512
PEER_EXPERT_TILE = 512


def peer_layer(x, norm_g, mods, w_q, keys, u_bf16, vt_bf16):
    b, l, d = x.shape
    h = modulate(rmsnorm(x, norm_g), mods[3], mods[4]).reshape(b * l, d)
    q = mm(h, w_q)
    gate = mods[5]
    tokens_per_gate = (b * l) // gate.shape[0]
    s0, s1, e0, e1, tau = peer_router(q, keys.astype(jnp.bfloat16), PEER_TOKEN_TILE)
    out = peer_dense(h.astype(jnp.bfloat16), u_bf16, vt_bf16, s0, s1, e0, e1, tau,
                     x.reshape(b * l, d), gate, tokens_per_gate, PEER_TOKEN_TILE, PEER_EXPERT_TILE)
    return out.reshape(b, l, d)


def kernel(x_prompt, x_sample, c, cache_na_k, cache_na_v, cache_diff_k, cache_diff_v, state_s5_re, state_s5_im, c_ctx, mod_w, mod_b, norm_mix_g, norm_ffn_g, ev_w_in, ev_w_out, na_q_g, na_k_g, na_rpb, hy_conv_w, hy_conv_b, hy_w1, hy_b1, hy_f1, hy_w2, hy_b2, hy_f2, hy_w3, hy_decay, hy_bias, od_w_in, od_w_out, s5_lam_re, s5_lam_im, s5_log_dt, s5_b_re, s5_b_im, s5_c_re, s5_c_im, s5_d, s5_glu_w, s5_glu_b, diff_q_g, diff_k_g, diff_lq1, diff_lk1, diff_lq2, diff_lk2, diff_subln_g, peer_w_q, peer_keys, peer_u, peer_v):
    bp, lp, d = x_prompt.shape
    bs, ls, _ = x_sample.shape
    rope_p, rope_s = rope_tables(lp), rope_tables(ls)
    xp, xs = x_prompt, x_sample
    new_na_k, new_na_v, new_dk, new_dv, new_sr, new_si = [], [], [], [], [], []
    for i in range(DEPTH):
        mp = adaln(c_ctx[None, :], mod_w[i], mod_b[i])
        ms = adaln(c, mod_w[i], mod_b[i])
        hp = modulate(rmsnorm(xp, norm_mix_g[i]), mp[0], mp[1]).reshape(bp * lp, d)
        hs = modulate(rmsnorm(xs, norm_mix_g[i]), ms[0], ms[1]).reshape(bs * ls, d)
        j = i // 2
        if i % 2 == 0:
            ev = (ev_w_in[j], ev_w_out[j], na_q_g[j], na_k_g[j], hy_conv_w[j], hy_conv_b[j],
                  hy_w1[j], hy_b1[j], hy_f1[j], hy_w2[j], hy_b2[j], hy_f2[j], hy_w3[j], hy_decay[j], hy_bias[j])
            bias_tabs = na_bias_tables(na_rpb[j], ls // GRID_W)
            op, kp, vp = even_mixer(hp, bp, lp, ev, None, None)
            os_, _, _ = even_mixer(hs, bs, ls, ev, (cache_na_k[:, j], cache_na_v[:, j]), bias_tabs)
            new_na_k.append(kp)
            new_na_v.append(vp)
        else:
            lam_init = 0.8 - 0.6 * math.exp(-0.3 * i)
            s5_prep = s5_prepare(s5_lam_re[j], s5_lam_im[j], s5_log_dt[j], s5_b_re[j], s5_b_im[j],
                                 s5_c_re[j], s5_c_im[j])
            od = (od_w_in[j], od_w_out[j], s5_prep, s5_d[j], s5_glu_w[j], s5_glu_b[j], diff_q_g[j], diff_k_g[j],
                  diff_lq1[j], diff_lk1[j], diff_lq2[j], diff_lk2[j], diff_subln_g[j])
            op, kp, vp, sr, si = odd_mixer(hp, bp, lp, od, lam_init, None, None, None, rope_p)
            os_, _, _, _, _ = odd_mixer(hs, bs, ls, od, lam_init, state_s5_re[:, j], state_s5_im[:, j],
                                        (cache_diff_k[:, j], cache_diff_v[:, j]), rope_s)
            new_dk.append(kp)
            new_dv.append(vp)
            new_sr.append(sr)
            new_si.append(si)
        xp = xp + mp[2] * op.reshape(bp, lp, d)
        xs = xs + ms[2] * os_.reshape(bs, ls, d)
        u_bf16 = peer_u[i].astype(jnp.bfloat16)
        vt_bf16 = peer_v[i].T.astype(jnp.bfloat16)
        xp = peer_layer(xp, norm_ffn_g[i], mp, peer_w_q[i], peer_keys[i], u_bf16, vt_bf16)
        xs = peer_layer(xs, norm_ffn_g[i], ms, peer_w_q[i], peer_keys[i], u_bf16, vt_bf16)
    return (xp, xs, jnp.stack(new_na_k, axis=1), jnp.stack(new_na_v, axis=1),
            jnp.stack(new_dk, axis=1), jnp.stack(new_dv, axis=1),
            jnp.stack(new_sr, axis=1), jnp.stack(new_si, axis=1))
```

```python
import functools
import math

import jax
import jax.numpy as jnp
from jax import lax
from jax.experimental import pallas as pl
from jax.experimental.pallas import tpu as pltpu

D_MODEL = 2048
DEPTH = 2
GRID_W = 64
MIX_WIDTH = D_MODEL // 2
HEAD_DIM = 128
NA_HEADS = MIX_WIDTH // HEAD_DIM
NA_KH = 8
NA_KW = 16
HY_WIDTH = MIX_WIDTH
HY_ORDER = 2
HY_POS_EMB = 33
S5_WIDTH = MIX_WIDTH
S5_GROUP = 16
S5_GROUPS = S5_WIDTH // S5_GROUP
S5_STATE = 64
DIFF_HEADS = MIX_WIDTH // HEAD_DIM
DIFF_QK_DIM = HEAD_DIM // 2
DIFF_V_DIM = HEAD_DIM
ROPE_BASE = 10000.0
PEER_HEADS = 8
PEER_N_KEYS = 128
PEER_QUERY_DIM = 256
PEER_TOPK = 16
NORM_EPS = 1e-6
NEG_INF = -1e30

LANES = 128
SUBLANES = 8
VMEM_LIMIT_BYTES = 56 * 1024 * 1024


def _mm_kernel(x_ref, w_ref, o_ref):
    o_ref[...] = jnp.dot(x_ref[...].astype(jnp.bfloat16), w_ref[...].astype(jnp.bfloat16),
                         preferred_element_type=jnp.float32)


def mm(x, w, tm=1024, tn=512):
    m, k = x.shape
    _, n = w.shape
    tm = min(tm, m)
    tn = min(tn, n)
    assert m % tm == 0 and n % tn == 0
    return pl.pallas_call(
        _mm_kernel,
        grid=(m // tm, n // tn),
        in_specs=[pl.BlockSpec((tm, k), lambda i, j: (i, 0)),
                  pl.BlockSpec((k, tn), lambda i, j: (0, j))],
        out_specs=pl.BlockSpec((tm, tn), lambda i, j: (i, j)),
        out_shape=jax.ShapeDtypeStruct((m, n), jnp.float32),
        compiler_params=pltpu.CompilerParams(
            dimension_semantics=("parallel", "arbitrary"),
            vmem_limit_bytes=VMEM_LIMIT_BYTES),
    )(x, w)


def rmsnorm(x, g):
    xf = x.astype(jnp.float32)
    y = xf * lax.rsqrt(jnp.mean(xf * xf, axis=-1, keepdims=True) + NORM_EPS)
    return (y * g.astype(jnp.float32)).astype(x.dtype)


def adaln(cond, w, b):
    m = jax.nn.silu(cond) @ w + b
    return jnp.split(m[:, None, :], 6, axis=-1)


def modulate(h, shift, scale):
    return h * (1.0 + scale) + shift


def hyena_filters(l, w1, b1, f1, w2, b2, f2, w3, decay):
    t = jnp.linspace(0.0, 1.0, l, dtype=jnp.float32)[:, None]
    bands = (HY_POS_EMB - 1) // 2
    w_ang = 2.0 * math.pi * jnp.arange(l, dtype=jnp.float32)[:, None] / l
    freqs = jnp.linspace(1e-4, bands - 1, bands, dtype=jnp.float32)[None, :]
    z = jnp.concatenate([t, jnp.cos(freqs * w_ang), -jnp.sin(freqs * w_ang)], axis=-1)
    h = jnp.sin(f1 * (z @ w1 + b1))
    h = jnp.sin(f2 * (h @ w2 + b2))
    h = (h @ w3).reshape(l, 2, HY_ORDER, HY_WIDTH).astype(jnp.float32)
    h = h * jnp.exp(-t.reshape(l, 1, 1, 1) * jnp.abs(decay.astype(jnp.float32)))
    h_f, h_b = h[:, 0], h[:, 1]
    zero = jnp.zeros((1, HY_ORDER, HY_WIDTH), jnp.float32)
    return jnp.concatenate([h_f, zero, h_b[1:][::-1]], axis=0)


def _split_bf16(x):
    hi = x.astype(jnp.bfloat16)
    lo = (x - hi.astype(jnp.float32)).astype(jnp.bfloat16)
    return hi, lo


def _dot3(a_hi, a_lo, x):
    m = a_hi.shape[0]
    xh, xl = _split_bf16(x)
    r = jnp.dot(jnp.concatenate([a_hi, a_lo], axis=0), xh, preferred_element_type=jnp.float32)
    return r[:m] + r[m:] + jnp.dot(a_hi, xl, preferred_element_type=jnp.float32)


def hyena_factors(l):
    n = 2 * l
    n2 = 64 if n >= 8192 else 16
    return n // n2, n2


def hyena_tables(l):
    n = 2 * l
    n1, n2 = hyena_factors(l)
    t = (n2 * jnp.arange(n1 // 2)[None, None, :] + jnp.arange(n2)[:, None, None])
    k1 = jnp.arange(n1)[None, :, None]
    ang = (2.0 * math.pi / n) * ((t * k1) % n).astype(jnp.float32)
    fa = jnp.concatenate([jnp.cos(ang), -jnp.sin(ang)], axis=1)
    fc = jnp.swapaxes(fa, 1, 2) / n
    a2 = (2.0 * math.pi / n2) * ((jnp.arange(n2)[:, None] * jnp.arange(n2)[None, :]) % n2).astype(jnp.float32)
    c, s = jnp.cos(a2), jnp.sin(a2)
    fb = jnp.concatenate([jnp.concatenate([c, s], axis=1), jnp.concatenate([-s, c], axis=1)], axis=0)
    fbi = jnp.concatenate([jnp.concatenate([c, -s], axis=1), jnp.concatenate([s, c], axis=1)], axis=0)
    return tuple(_split_bf16(x) for x in (fa, fc, fb, fbi))


def hyena_spectrum(filt, l):
    n1, n2 = hyena_factors(l)
    kf = jnp.fft.fft(filt, axis=0)
    kf = jnp.stack([jnp.real(kf), jnp.imag(kf)], axis=0).astype(jnp.float32)
    kf = kf.reshape(2, n2, n1, HY_ORDER, HY_WIDTH)
    return jnp.transpose(kf, (3, 0, 2, 1, 4))


def _hy_stage_a_kernel(z_ref, fh_ref, fl_ref, o_ref):
    o_ref[0] = _dot3(fh_ref[0], fl_ref[0], z_ref[0])


def _hy_stage_b_kernel(s_ref, k_ref, fh_ref, fl_ref, gh_ref, gl_ref, o_ref):
    n2 = s_ref.shape[3]
    y = jnp.concatenate([s_ref[0, 0, 0], s_ref[0, 1, 0]], axis=0)
    z = _dot3(fh_ref[...], fl_ref[...], y)
    zr, zi = z[:n2], z[n2:]
    kr, ki = k_ref[0, 0], k_ref[1, 0]
    p = jnp.concatenate([zr * kr - zi * ki, zr * ki + zi * kr], axis=0)
    q = _dot3(gh_ref[...], gl_ref[...], p)
    o_ref[0, 0, 0] = q[:n2]
    o_ref[0, 1, 0] = q[n2:]


def _hy_stage_c_kernel(q_ref, fh_ref, fl_ref, z_ref, g_ref, b_ref, o_ref):
    y = _dot3(fh_ref[0], fl_ref[0], q_ref[0])
    o_ref[0] = g_ref[0] * (y + z_ref[0] * b_ref[...])


def hyena_long_conv(z, z_col, gate, gate_col, bias_o, spec_o, tables, l):
    (fah, fal), (fch, fcl), (fbh, fbl), (fgh, fgl) = tables
    b = z.shape[0]
    c = HY_WIDTH
    n1, n2 = hyena_factors(l)
    h1 = n1 // 2
    cp = pltpu.CompilerParams(dimension_semantics=("parallel", "parallel"), vmem_limit_bytes=VMEM_LIMIT_BYTES)
    zblocks = z.shape[-1] // c
    gblocks = gate.shape[-1] // c
    zv = z.reshape(b, h1, n2 * z.shape[-1])
    gv = gate.reshape(b, h1, n2 * gate.shape[-1])
    s1 = pl.pallas_call(
        _hy_stage_a_kernel,
        grid=(b, n2),
        in_specs=[pl.BlockSpec((1, h1, c), lambda i, j: (i, 0, j * zblocks + z_col)),
                  pl.BlockSpec((1, 2 * n1, h1), lambda i, j: (j, 0, 0)),
                  pl.BlockSpec((1, 2 * n1, h1), lambda i, j: (j, 0, 0))],
        out_specs=pl.BlockSpec((1, 2 * n1, c), lambda i, j: (i, 0, j)),
        out_shape=jax.ShapeDtypeStruct((b, 2 * n1, n2 * c), jnp.float32),
        compiler_params=cp, name="hyena_stage_a",
    )(zv, fah, fal)
    s1 = s1.reshape(b, 2, n1, n2, c)
    mat = pl.BlockSpec((2 * n2, 2 * n2), lambda k, i: (0, 0))
    q = pl.pallas_call(
        _hy_stage_b_kernel,
        grid=(n1, b),
        in_specs=[pl.BlockSpec((1, 2, 1, n2, c), lambda k, i: (i, 0, k, 0, 0)),
                  pl.BlockSpec((2, 1, n2, c), lambda k, i: (0, k, 0, 0)),
                  mat, mat, mat, mat],
        out_specs=pl.BlockSpec((1, 2, 1, n2, c), lambda k, i: (i, 0, k, 0, 0)),
        out_shape=jax.ShapeDtypeStruct((b, 2, n1, n2, c), jnp.float32),
        compiler_params=cp, name="hyena_stage_b",
    )(s1, spec_o, fbh, fbl, fgh, fgl)
    q = q.reshape(b, 2 * n1, n2 * c)
    out = pl.pallas_call(
        _hy_stage_c_kernel,
        grid=(b, n2),
        in_specs=[pl.BlockSpec((1, 2 * n1, c), lambda i, j: (i, 0, j)),
                  pl.BlockSpec((1, h1, 2 * n1), lambda i, j: (j, 0, 0)),
                  pl.BlockSpec((1, h1, 2 * n1), lambda i, j: (j, 0, 0)),
                  pl.BlockSpec((1, h1, c), lambda i, j: (i, 0, j * zblocks + z_col)),
                  pl.BlockSpec((1, h1, c), lambda i, j: (i, 0, j * gblocks + gate_col)),
                  pl.BlockSpec((1, c), lambda i, j: (0, 0))],
        out_specs=pl.BlockSpec((1, h1, c), lambda i, j: (i, 0, j)),
        out_shape=jax.ShapeDtypeStruct((b, h1, n2 * c), jnp.float32),
        compiler_params=cp, name="hyena_stage_c",
    )(q, fch, fcl, zv, gv, bias_o.astype(jnp.float32).reshape(1, c))
    return out.reshape(b, l, c)


def hyena(u, conv_w, conv_b, filt, bias):
    b, l, _ = u.shape
    up = jnp.pad(u, ((0, 0), (1, 1), (0, 0)))
    u = up[:, :-2] * conv_w[0] + up[:, 1:-1] * conv_w[1] + up[:, 2:] * conv_w[2] + conv_b
    tables = hyena_tables(l)
    spec = hyena_spectrum(hyena_filters(l, *filt), l)
    z, z_col = u, 0
    for o in range(HY_ORDER):
        z = hyena_long_conv(z, z_col, u, 1 + o, bias[o], spec[o], tables, l)
        z_col = 0
    return z


S5_TILE_GROUPS = 8
S5_TILES = S5_GROUPS // S5_TILE_GROUPS
S5_TILE_IN = S5_TILE_GROUPS * S5_GROUP
S5_TILE_STATE = S5_TILE_GROUPS * S5_STATE
S5_ROWS_PER_STEP = 2048


def s5_prepare(lam_re, lam_im, log_dt, b_re, b_im, c_re, c_im):
    f32 = jnp.float32
    lam_re, lam_im = lam_re.astype(f32), lam_im.astype(f32)
    dt = jnp.exp(log_dt.astype(f32))[..., None]
    mag = jnp.exp(lam_re * dt)
    ar, ai = mag * jnp.cos(lam_im * dt), mag * jnp.sin(lam_im * dt)
    den = lam_re * lam_re + lam_im * lam_im
    cr = ((ar - 1.0) * lam_re + ai * lam_im) / den
    ci = (ai * lam_re - (ar - 1.0) * lam_im) / den
    bbr = cr[..., None] * b_re - ci[..., None] * b_im
    bbi = cr[..., None] * b_im + ci[..., None] * b_re
    eye = jnp.eye(S5_TILE_GROUPS, dtype=f32)

    def tile_in(bb):
        x = bb.reshape(2, S5_TILES, S5_TILE_GROUPS, S5_STATE, S5_GROUP)
        x = jnp.einsum('dtgnp,gh->dtgphn', x, eye)
        return x.reshape(2, S5_TILES, S5_TILE_IN, S5_TILE_STATE)

    def tile_out(cc):
        x = cc.astype(f32).reshape(2, S5_TILES, S5_TILE_GROUPS, S5_GROUP, S5_STATE)
        x = jnp.einsum('dtgpn,gh->dtgnhp', x, eye)
        return x.reshape(2, S5_TILES, S5_TILE_STATE, S5_TILE_IN)

    win = jnp.concatenate([tile_in(bbr), tile_in(bbi)], axis=-1).astype(jnp.bfloat16)
    wout = jnp.concatenate([tile_out(c_re), -tile_out(c_im)], axis=-2).astype(jnp.bfloat16)
    a = jnp.stack([ar.reshape(2, S5_TILES, S5_TILE_STATE), ai.reshape(2, S5_TILES, S5_TILE_STATE)], axis=2)
    return win, wout, a


def _s5_scan_kernel(u_ref, win_ref, wout_ref, a_ref, h0_ref, y_ref, fin_ref, bu_ref, st_ref, *, batch):
    d = pl.program_id(0)
    c = pl.program_id(2)
    n_c = pl.num_programs(2)
    ns = S5_TILE_STATE
    steps = u_ref.shape[0] // batch

    @pl.when(c == 0)
    def _():
        st_ref[...] = h0_ref[0, 0]

    bu_ref[...] = jnp.dot(u_ref[...].astype(jnp.bfloat16), win_ref[0, 0], preferred_element_type=jnp.float32)
    ar = jnp.broadcast_to(a_ref[0, 0, 0:1, :], (batch, ns))
    ai = jnp.broadcast_to(a_ref[0, 0, 1:2, :], (batch, ns))

    def step(t, carry):
        hr, hi = carry
        te = jnp.where(d == 0, t, steps - 1 - t)
        r0 = pl.multiple_of(te * batch, batch)
        nr = ar * hr - ai * hi + bu_ref[pl.ds(r0, batch), 0:ns]
        ni = ar * hi + ai * hr + bu_ref[pl.ds(r0, batch), ns:2 * ns]
        bu_ref[pl.ds(r0, batch), 0:ns] = nr
        bu_ref[pl.ds(r0, batch), ns:2 * ns] = ni
        return nr, ni

    hr, hi = lax.fori_loop(0, steps, step, (st_ref[:, 0:ns], st_ref[:, ns:2 * ns]), unroll=4)
    st_ref[:, 0:ns] = hr
    st_ref[:, ns:2 * ns] = hi
    y_ref[0] = jnp.dot(bu_ref[...].astype(jnp.bfloat16), wout_ref[0, 0], preferred_element_type=jnp.float32)

    @pl.when(c == n_c - 1)
    def _():
        fin_ref[0, 0] = st_ref[...]


def s5_scan(u_tm, win, wout, a, h0, batch):
    rows = u_tm.shape[0]
    r = min(S5_ROWS_PER_STEP, rows)
    assert rows % r == 0 and r % batch == 0
    n_c = rows // r

    def chunk(d, c):
        return c + d * (n_c - 1 - 2 * c)

    return pl.pallas_call(
        functools.partial(_s5_scan_kernel, batch=batch),
        grid=(2, S5_TILES, n_c),
        in_specs=[pl.BlockSpec((r, S5_TILE_IN), lambda d, j, c: (chunk(d, c), j)),
                  pl.BlockSpec((1, 1, S5_TILE_IN, 2 * S5_TILE_STATE), lambda d, j, c: (d, j, 0, 0)),
                  pl.BlockSpec((1, 1, 2 * S5_TILE_STATE, S5_TILE_IN), lambda d, j, c: (d, j, 0, 0)),
                  pl.BlockSpec((1, 1, 2, S5_TILE_STATE), lambda d, j, c: (d, j, 0, 0)),
                  pl.BlockSpec((1, 1, batch, 2 * S5_TILE_STATE), lambda d, j, c: (d, j, 0, 0))],
        out_specs=[pl.BlockSpec((1, r, S5_TILE_IN), lambda d, j, c: (d, chunk(d, c), j)),
                   pl.BlockSpec((1, 1, batch, 2 * S5_TILE_STATE), lambda d, j, c: (d, j, 0, 0))],
        out_shape=[jax.ShapeDtypeStruct((2, rows, S5_WIDTH), jnp.float32),
                   jax.ShapeDtypeStruct((2, S5_TILES, batch, 2 * S5_TILE_STATE), jnp.float32)],
        scratch_shapes=[pltpu.VMEM((r, 2 * S5_TILE_STATE), jnp.float32),
                        pltpu.VMEM((batch, 2 * S5_TILE_STATE), jnp.float32)],
        compiler_params=pltpu.CompilerParams(dimension_semantics=("parallel", "parallel", "arbitrary"),
                                             vmem_limit_bytes=VMEM_LIMIT_BYTES),
        name="s5_scan",
    )(u_tm, win, wout, a, h0)


def _gelu_tanh(x):
    return 0.5 * x * (1.0 + jnp.tanh(math.sqrt(2.0 / math.pi) * (x + 0.044715 * (x * x * x))))


def _s5_glu_kernel(u_ref, y_ref, d_ref, w_ref, b_ref, o_ref):
    y = d_ref[...] * u_ref[...] + y_ref[0] + y_ref[1]
    y = _gelu_tanh(y)
    z = jnp.dot(y.astype(jnp.bfloat16), w_ref[...], preferred_element_type=jnp.float32) + b_ref[...]
    o_ref[...] = y * (1.0 / (1.0 + jnp.exp(-z)))


def s5_glu(u_tm, y, d_skip, glu_w_bf16, glu_b, tr=1024):
    rows, w = u_tm.shape
    tr = min(tr, rows)
    assert rows % tr == 0
    return pl.pallas_call(
        _s5_glu_kernel,
        grid=(rows // tr,),
        in_specs=[pl.BlockSpec((tr, w), lambda i: (i, 0)),
                  pl.BlockSpec((2, tr, w), lambda i: (0, i, 0)),
                  pl.BlockSpec((1, w), lambda i: (0, 0)),
                  pl.BlockSpec((w, w), lambda i: (0, 0)),
                  pl.BlockSpec((1, w), lambda i: (0, 0))],
        out_specs=pl.BlockSpec((tr, w), lambda i: (i, 0)),
        out_shape=jax.ShapeDtypeStruct((rows, w), jnp.float32),
        compiler_params=pltpu.CompilerParams(dimension_semantics=("parallel",),
                                             vmem_limit_bytes=VMEM_LIMIT_BYTES),
        name="s5_glu",
    )(u_tm, y, d_skip.reshape(1, w), glu_w_bf16, glu_b.reshape(1, w))


def s5_mixer(u, h0_re, h0_im, prep, d_skip, glu_w, glu_b):
    win, wout, a = prep
    b, l, w = u.shape
    u_tm = jnp.swapaxes(u, 0, 1).reshape(l * b, w)
    if h0_re is None:
        h0 = jnp.zeros((2, S5_TILES, b, 2 * S5_TILE_STATE), jnp.float32)
    else:
        def tiles(h):
            return jnp.transpose(h.astype(jnp.float32).reshape(b, 2, S5_TILES, S5_TILE_STATE), (1, 2, 0, 3))
        h0 = jnp.concatenate([tiles(h0_re), tiles(h0_im)], axis=-1)
    y, fin = s5_scan(u_tm, win, wout, a, h0, b)
    out_tm = s5_glu(u_tm, y, d_skip, glu_w.astype(jnp.bfloat16), glu_b)
    out = jnp.swapaxes(out_tm.reshape(l, b, w), 0, 1)

    def untile(f):
        return jnp.transpose(f, (2, 0, 1, 3)).reshape(b, 2, S5_GROUPS, S5_STATE)
    return out, untile(fin[..., :S5_TILE_STATE]), untile(fin[..., S5_TILE_STATE:])


def _half_mean_matrix():
    i = lax.broadcasted_iota(jnp.int32, (LANES, LANES), 0) // DIFF_QK_DIM
    j = lax.broadcasted_iota(jnp.int32, (LANES, LANES), 1) // DIFF_QK_DIM
    return jnp.where(i == j, 1.0 / DIFF_QK_DIM, 0.0).astype(jnp.bfloat16)


def _rms_groups(x, gain, avg):
    xx = x * x
    hi = xx.astype(jnp.bfloat16)
    lo = (xx - hi.astype(jnp.float32)).astype(jnp.bfloat16)
    ms = (jnp.dot(hi, avg, preferred_element_type=jnp.float32)
          + jnp.dot(lo, avg, preferred_element_type=jnp.float32))
    return x * lax.rsqrt(ms + NORM_EPS) * gain


def _rope_lanes(y, cos, sin_signed, first_half):
    rot = jnp.where(first_half, pltpu.roll(y, LANES - 16, 1), pltpu.roll(y, 16, 1))
    return y * cos + rot * sin_signed


def _diff_prep_kernel(q_ref, k_ref, qg_ref, kg_ref, cos_ref, sin_ref, qo_ref, ko_ref, kn_ref, *, use_rope):
    avg = _half_mean_matrix()
    lane = lax.broadcasted_iota(jnp.int32, (1, LANES), 1)
    first_half = (lane % 32) < 16
    scale = DIFF_QK_DIM ** -0.5
    for h in range(DIFF_HEADS):
        cols = slice(h * LANES, (h + 1) * LANES)
        qn = _rms_groups(q_ref[:, cols], qg_ref[...], avg)
        kn = _rms_groups(k_ref[:, cols], kg_ref[...], avg)
        kn_ref[:, cols] = kn
        if use_rope:
            qn = _rope_lanes(qn, cos_ref[...], sin_ref[...], first_half)
            kn = _rope_lanes(kn, cos_ref[...], sin_ref[...], first_half)
        qo_ref[:, cols] = (qn * scale).astype(jnp.bfloat16)
        ko_ref[:, cols] = kn.astype(jnp.bfloat16)


def diff_prep(p, q_g, k_g, cos, sin_signed, seq_len, use_rope, tr=512):
    rows = p.shape[0]
    tr = min(tr, seq_len)
    assert seq_len % tr == 0
    w = DIFF_HEADS * LANES
    per_seq = seq_len // tr
    g2 = lambda g: jnp.tile(g.astype(jnp.float32), 2).reshape(1, LANES)
    return pl.pallas_call(
        functools.partial(_diff_prep_kernel, use_rope=use_rope),
        grid=(rows // tr,),
        in_specs=[pl.BlockSpec((tr, w), lambda i: (i, 1)),
                  pl.BlockSpec((tr, w), lambda i: (i, 2)),
                  pl.BlockSpec((1, LANES), lambda i: (0, 0)),
                  pl.BlockSpec((1, LANES), lambda i: (0, 0)),
                  pl.BlockSpec((tr, LANES), lambda i: (i % per_seq, 0)),
                  pl.BlockSpec((tr, LANES), lambda i: (i % per_seq, 0))],
        out_specs=[pl.BlockSpec((tr, w), lambda i: (i, 0))] * 3,
        out_shape=[jax.ShapeDtypeStruct((rows, w), jnp.bfloat16), jax.ShapeDtypeStruct((rows, w), jnp.bfloat16),
                   jax.ShapeDtypeStruct((rows, w), jnp.float32)],
        compiler_params=pltpu.CompilerParams(dimension_semantics=("parallel",),
                                             vmem_limit_bytes=VMEM_LIMIT_BYTES),
        name="diff_prep",
    )(p, p, g2(q_g), g2(k_g), cos, sin_signed)


def rope_tables(l):
    t = jnp.arange(l)
    row = (t // GRID_W).astype(jnp.float32)
    col = (t % GRID_W).astype(jnp.float32)
    nf = DIFF_QK_DIM // 4
    inv = ROPE_BASE ** (-jnp.arange(nf, dtype=jnp.float32) / nf)
    ang = jnp.stack([row[:, None] * inv, col[:, None] * inv], axis=1)
    ang = jnp.stack([ang, ang], axis=2).reshape(l, DIFF_QK_DIM)
    sign = jnp.where((jnp.arange(DIFF_QK_DIM) % 32) < 16, -1.0, 1.0)
    return jnp.tile(jnp.cos(ang), (1, 2)), jnp.tile(jnp.sin(ang) * sign, (1, 2))


def _diff_attn_kernel(*refs, has_ctx):
    if has_ctx:
        q_ref, k_ref, v_ref, ck_ref, cv_ref, lam_ref, g_ref, o_ref = refs
    else:
        q_ref, k_ref, v_ref, lam_ref, g_ref, o_ref = refs
    q = q_ref[...]
    lane = lax.broadcasted_iota(jnp.int32, (1, LANES), 1)
    zero = jnp.zeros_like(q)
    qs = (jnp.where(lane < DIFF_QK_DIM, q, zero), jnp.where(lane >= DIFF_QK_DIM, q, zero))
    nt = (((1,), (1,)), ((), ()))
    k = k_ref[...]
    ck = ck_ref[...].astype(jnp.bfloat16) if has_ctx else None
    lam = lam_ref[0:1, 0:1]
    w_self, w_ctx = None, None
    for i in range(2):
        s = lax.dot_general(qs[i], k, nt, preferred_element_type=jnp.float32)
        m = jnp.max(s, axis=-1, keepdims=True)
        if has_ctx:
            sc = lax.dot_general(qs[i], ck, nt, preferred_element_type=jnp.float32)
            m = jnp.maximum(m, jnp.max(sc, axis=-1, keepdims=True))
        p = jnp.exp(s - m)
        l = jnp.sum(p, axis=-1, keepdims=True)
        if has_ctx:
            pc = jnp.exp(sc - m)
            l = l + jnp.sum(pc, axis=-1, keepdims=True)
        coef = 1.0 / l if i == 0 else -lam / l
        w_self = p * coef if i == 0 else w_self + p * coef
        if has_ctx:
            w_ctx = pc * coef if i == 0 else w_ctx + pc * coef
    o = jnp.dot(w_self.astype(jnp.bfloat16), v_ref[...].astype(jnp.bfloat16), preferred_element_type=jnp.float32)
    if has_ctx:
        o = o + jnp.dot(w_ctx.astype(jnp.bfloat16), cv_ref[...].astype(jnp.bfloat16),
                        preferred_element_type=jnp.float32)
    ms = jnp.mean(o * o, axis=-1, keepdims=True)
    o_ref[...] = o * lax.rsqrt(ms + NORM_EPS) * g_ref[...]


def diff_attention(q_bf16, k_bf16, p, ctx_k, ctx_v, lam, gain, batch, seq_len, tq=256):
    tq = min(tq, seq_len)
    nq = seq_len // tq
    has_ctx = ctx_k is not None
    v_col0 = 3 * DIFF_HEADS
    in_specs = [pl.BlockSpec((tq, LANES), lambda b, h, i: (b * nq + i, h)),
                pl.BlockSpec((seq_len, LANES), lambda b, h, i: (b, h)),
                pl.BlockSpec((seq_len, LANES), lambda b, h, i: (b, v_col0 + h))]
    args = [q_bf16, k_bf16, p]
    if has_ctx:
        lc = ctx_k.shape[0] // batch
        in_specs += [pl.BlockSpec((lc, LANES), lambda b, h, i: (b, h))] * 2
        args += [ctx_k, ctx_v]
    in_specs += [pl.BlockSpec((1, LANES), lambda b, h, i: (0, 0))] * 2
    args += [jnp.broadcast_to(lam.astype(jnp.float32), (1, LANES)), gain.astype(jnp.float32).reshape(1, LANES)]
    return pl.pallas_call(
        functools.partial(_diff_attn_kernel, has_ctx=has_ctx),
        grid=(batch, DIFF_HEADS, nq),
        in_specs=in_specs,
        out_specs=pl.BlockSpec((tq, LANES), lambda b, h, i: (b * nq + i, h)),
        out_shape=jax.ShapeDtypeStruct((batch * seq_len, DIFF_HEADS * LANES), jnp.float32),
        compiler_params=pltpu.CompilerParams(dimension_semantics=("parallel", "parallel", "arbitrary"),
                                             vmem_limit_bytes=VMEM_LIMIT_BYTES),
        name="diff_attention",
    )(*args)


NA_Q_ROWS = 8
NA_WIN_ROWS = 16


def _na_prep_kernel(q_ref, k_ref, qg_ref, kg_ref, qo_ref, ko_ref, kn_ref):
    avg = jnp.full((LANES, LANES), 1.0 / HEAD_DIM, jnp.bfloat16)
    for h in range(NA_HEADS):
        cols = slice(h * LANES, (h + 1) * LANES)
        qo_ref[:, cols] = _rms_groups(q_ref[:, cols], qg_ref[...], avg).astype(jnp.bfloat16)
        kn = _rms_groups(k_ref[:, cols], kg_ref[...], avg)
        kn_ref[:, cols] = kn
        ko_ref[:, cols] = kn.astype(jnp.bfloat16)


def na_prep(p, q_g, k_g, tr=512):
    rows = p.shape[0]
    tr = min(tr, rows)
    w = NA_HEADS * LANES
    g1 = lambda g: g.astype(jnp.float32).reshape(1, LANES)
    return pl.pallas_call(
        _na_prep_kernel,
        grid=(rows // tr,),
        in_specs=[pl.BlockSpec((tr, w), lambda i: (i, 0)),
                  pl.BlockSpec((tr, w), lambda i: (i, 1)),
                  pl.BlockSpec((1, LANES), lambda i: (0, 0)),
                  pl.BlockSpec((1, LANES), lambda i: (0, 0))],
        out_specs=[pl.BlockSpec((tr, w), lambda i: (i, 0))] * 3,
        out_shape=[jax.ShapeDtypeStruct((rows, w), jnp.bfloat16), jax.ShapeDtypeStruct((rows, w), jnp.bfloat16),
                   jax.ShapeDtypeStruct((rows, w), jnp.float32)],
        compiler_params=pltpu.CompilerParams(dimension_semantics=("parallel",),
                                             vmem_limit_bytes=VMEM_LIMIT_BYTES),
        name="na_prep",
    )(p, p, g1(q_g), g1(k_g))


def na_bias_tables(rpb, rows):
    nblk = rows // NA_Q_ROWS
    cq = jnp.arange(GRID_W)[:, None]
    ck = jnp.arange(GRID_W)[None, :]
    col_start = jnp.clip(cq - NA_KW // 2, 0, GRID_W - NA_KW)
    col_ok = (ck >= col_start) & (ck < col_start + NA_KW)
    cidx = jnp.clip(ck - cq + NA_KW - 1, 0, 2 * NA_KW - 2)
    tabs = []
    for i in (0, 1, nblk - 1):
        base = min(max(NA_Q_ROWS * i - NA_KH // 2, 0), rows - NA_WIN_ROWS)
        r = NA_Q_ROWS * i + jnp.arange(NA_Q_ROWS)[:, None]
        rk = base + jnp.arange(NA_WIN_ROWS)[None, :]
        r0 = jnp.clip(r - NA_KH // 2, 0, rows - NA_KH)
        row_ok = (rk >= r0) & (rk < r0 + NA_KH)
        ridx = jnp.clip(rk - r + NA_KH - 1, 0, 2 * NA_KH - 2)
        b = rpb[:, ridx][:, :, :, cidx]
        ok = row_ok[:, :, None, None] & col_ok[None, None, :, :]
        b = jnp.where(ok[None], b.astype(jnp.float32), NEG_INF)
        tabs.append(jnp.transpose(b, (0, 1, 3, 2, 4)).reshape(NA_HEADS, NA_Q_ROWS * GRID_W, NA_WIN_ROWS * GRID_W))
    return jnp.stack(tabs, axis=0)


def _na_attn_kernel(*refs, windowed, rows):
    nt = (((1,), (1,)), ((), ()))
    scale = HEAD_DIM ** -0.5
    if windowed:
        q_ref, k_ref, v_ref, b_ref, ck_ref, cv_ref, o_ref = refs
        i = pl.program_id(2)
        base = jnp.clip(NA_Q_ROWS * i - NA_KH // 2, 0, rows - NA_WIN_ROWS)
        k0 = pl.multiple_of(base * GRID_W, GRID_W)
        nk = NA_WIN_ROWS * GRID_W
        q = q_ref[...]
        s = lax.dot_general(q, k_ref[pl.ds(k0, nk), :], nt, preferred_element_type=jnp.float32) * scale + b_ref[0, 0]
        sc = lax.dot_general(q, ck_ref[...].astype(jnp.bfloat16), nt, preferred_element_type=jnp.float32) * scale
        m = jnp.maximum(jnp.max(s, axis=-1, keepdims=True), jnp.max(sc, axis=-1, keepdims=True))
        p = jnp.exp(s - m)
        pc = jnp.exp(sc - m)
        l = jnp.sum(p, axis=-1, keepdims=True) + jnp.sum(pc, axis=-1, keepdims=True)
        o = (jnp.dot(p.astype(jnp.bfloat16), v_ref[pl.ds(k0, nk), :].astype(jnp.bfloat16),
                     preferred_element_type=jnp.float32)
             + jnp.dot(pc.astype(jnp.bfloat16), cv_ref[...].astype(jnp.bfloat16), preferred_element_type=jnp.float32))
    else:
        q_ref, k_ref, v_ref, o_ref = refs
        s = lax.dot_general(q_ref[...], k_ref[...], nt, preferred_element_type=jnp.float32) * scale
        m = jnp.max(s, axis=-1, keepdims=True)
        p = jnp.exp(s - m)
        l = jnp.sum(p, axis=-1, keepdims=True)
        o = jnp.dot(p.astype(jnp.bfloat16), v_ref[...].astype(jnp.bfloat16), preferred_element_type=jnp.float32)
    o_ref[...] = o / l


def na_attention(q_bf16, k_bf16, p, bias_tabs, ctx_k, ctx_v, batch, seq_len):
    windowed = bias_tabs is not None
    v_col0 = 2 * NA_HEADS
    rows = seq_len // GRID_W
    if windowed:
        tq = NA_Q_ROWS * GRID_W
        nq = seq_len // tq
        lc = ctx_k.shape[0] // batch
        nk = NA_WIN_ROWS * GRID_W
        in_specs = [pl.BlockSpec((tq, LANES), lambda b, h, i: (b * nq + i, h)),
                    pl.BlockSpec((seq_len, LANES), lambda b, h, i: (b, h)),
                    pl.BlockSpec((seq_len, LANES), lambda b, h, i: (b, v_col0 + h)),
                    pl.BlockSpec((1, 1, tq, nk),
                                 lambda b, h, i: (jnp.where(i == 0, 0, jnp.where(i == nq - 1, 2, 1)), h, 0, 0)),
                    pl.BlockSpec((lc, LANES), lambda b, h, i: (b, h)),
                    pl.BlockSpec((lc, LANES), lambda b, h, i: (b, h))]
        args = [q_bf16, k_bf16, p, bias_tabs, ctx_k, ctx_v]
    else:
        tq = seq_len
        nq = 1
        in_specs = [pl.BlockSpec((tq, LANES), lambda b, h, i: (b, h)),
                    pl.BlockSpec((seq_len, LANES), lambda b, h, i: (b, h)),
                    pl.BlockSpec((seq_len, LANES), lambda b, h, i: (b, v_col0 + h))]
        args = [q_bf16, k_bf16, p]
    return pl.pallas_call(
        functools.partial(_na_attn_kernel, windowed=windowed, rows=rows),
        grid=(batch, NA_HEADS, nq),
        in_specs=in_specs,
        out_specs=pl.BlockSpec((tq, LANES), lambda b, h, i: (b * nq + i, h)),
        out_shape=jax.ShapeDtypeStruct((batch * seq_len, NA_HEADS * LANES), jnp.float32),
        compiler_params=pltpu.CompilerParams(dimension_semantics=("parallel", "parallel", "arbitrary"),
                                             vmem_limit_bytes=VMEM_LIMIT_BYTES),
        name="na_attention",
    )(*args)


def even_mixer(h, b, l, ev, ctx_kv, bias_tabs):
    (w_in, w_out, q_g, k_g, conv_w, conv_b, w1, b1, f1, w2, b2, f2, w3, decay, bias) = ev
    p = mm(h, w_in)
    q, k, kn = na_prep(p, q_g, k_g)
    if ctx_kv is None:
        oa = na_attention(q, k, p, None, None, None, b, l)
    else:
        lc = ctx_kv[0].shape[1]
        oa = na_attention(q, k, p, bias_tabs, ctx_kv[0].reshape(b * lc, MIX_WIDTH),
                          ctx_kv[1].reshape(b * lc, MIX_WIDTH), b, l)
    hb = p[:, 3 * MIX_WIDTH:].reshape(b, l, 3 * HY_WIDTH)
    ob = hyena(hb, conv_w, conv_b, (w1, b1, f1, w2, b2, f2, w3, decay), bias)
    out = mm(jnp.concatenate([oa, ob.reshape(b * l, HY_WIDTH)], axis=-1), w_out)
    shp = (b, l, NA_HEADS, HEAD_DIM)
    return out, kn.reshape(shp), p[:, 2 * MIX_WIDTH:3 * MIX_WIDTH].reshape(shp)


def odd_mixer(h, b, l, od, lam_init, h0_re, h0_im, ctx_kv, rope):
    (w_in, w_out, s5_prep, d_skip, glu_w, glu_b, q_g, k_g, lq1, lk1, lq2, lk2, subln_g) = od
    p = mm(h, w_in)
    oc, fr, fi = s5_mixer(p[:, :S5_WIDTH].reshape(b, l, S5_WIDTH), h0_re, h0_im, s5_prep, d_skip, glu_w, glu_b)
    lam = (jnp.exp(jnp.sum(lq1 * lk1).astype(jnp.float32))
           - jnp.exp(jnp.sum(lq2 * lk2).astype(jnp.float32)) + lam_init)
    cos, sin_signed = rope
    q, k, kn = diff_prep(p, q_g, k_g, cos, sin_signed, l, use_rope=ctx_kv is not None)
    gain = subln_g.astype(jnp.float32) * (1.0 - lam_init)
    if ctx_kv is None:
        o = diff_attention(q, k, p, None, None, lam, gain, b, l)
    else:
        lc = ctx_kv[0].shape[1]
        o = diff_attention(q, k, p, ctx_kv[0].reshape(b * lc, MIX_WIDTH), ctx_kv[1].reshape(b * lc, MIX_WIDTH),
                           lam, gain, b, l)
    out = mm(jnp.concatenate([oc.reshape(b * l, S5_WIDTH), o], axis=-1), w_out)
    kn = kn.reshape(b, l, DIFF_HEADS, 2, DIFF_QK_DIM)
    v = p[:, S5_WIDTH + 2 * MIX_WIDTH:].reshape(b, l, DIFF_HEADS, DIFF_V_DIM)
    return out, kn, v, fr, fi


PEER_TOKEN_TILE = 512
PEER_EXPERT_TILE = 512
PEER_SUB_EXPERTS = 256


def _top16_rows(cur, iota):
    n = cur.shape[0]
    tops = []
    for _ in range(PEER_TOPK):
        m = jnp.max(cur, axis=0, keepdims=True)
        tops.append(m)
        first = jnp.min(jnp.where(cur == m, iota, n), axis=0, keepdims=True)
        cur = jnp.where(iota == first, -jnp.inf, cur)
    return tops


def _router_kernel(q_ref, keys_ref, s0_ref, s1_ref, e0_ref, e1_ref, tau_ref, st_ref):
    tt = q_ref.shape[0]
    half = PEER_QUERY_DIM // 2
    for h in range(PEER_HEADS):
        for s in range(2):
            qs = q_ref[:, (2 * h + s) * half:(2 * h + s + 1) * half].astype(jnp.bfloat16)
            st = lax.dot_general(keys_ref[h, s], qs, (((1,), (1,)), ((), ())),
                                 preferred_element_type=jnp.float32)
            st_ref[2 * h + s] = st

    iota128 = lax.broadcasted_iota(jnp.int32, (PEER_N_KEYS, LANES), 0)
    iota256 = lax.broadcasted_iota(jnp.int32, (PEER_TOPK * PEER_TOPK, LANES), 0)
    n_chunks = tt // LANES

    def body(i, carry):
        h = i // n_chunks
        lane0 = pl.multiple_of((i % n_chunks) * LANES, LANES)
        s0 = st_ref[2 * h, :, pl.ds(lane0, LANES)]
        s1 = st_ref[2 * h + 1, :, pl.ds(lane0, LANES)]
        a = _top16_rows(s0, iota128)
        b = _top16_rows(s1, iota128)
        bmat = jnp.concatenate(b, axis=0)
        cand = jnp.concatenate([a[k] + bmat for k in range(PEER_TOPK)], axis=0)
        f = _top16_rows(cand, iota256)
        z = jnp.ones_like(f[0])
        for k in range(1, PEER_TOPK):
            z = z + jnp.exp(f[k] - f[0])
        s0_ref[h, :, pl.ds(lane0, LANES)] = s0
        s1_ref[h, :, pl.ds(lane0, LANES)] = s1
        e0_ref[h, :, pl.ds(lane0, LANES)] = jnp.exp(s0 - a[0]) / z
        e1_ref[h, :, pl.ds(lane0, LANES)] = jnp.exp(s1 - b[0])
        tau_ref[h, :, pl.ds(lane0, LANES)] = f[PEER_TOPK - 1]
        return carry

    lax.fori_loop(0, PEER_HEADS * n_chunks, body, 0)


def peer_router(q, keys_bf16, tt):
    t = q.shape[0]
    f32 = jnp.float32
    big = jax.ShapeDtypeStruct((PEER_HEADS, PEER_N_KEYS, t), f32)
    blk = pl.BlockSpec((PEER_HEADS, PEER_N_KEYS, tt), lambda i: (0, 0, i))
    return pl.pallas_call(
        _router_kernel,
        grid=(t // tt,),
        in_specs=[pl.BlockSpec((tt, q.shape[1]), lambda i: (i, 0)),
                  pl.BlockSpec(keys_bf16.shape, lambda i: (0, 0, 0, 0))],
        out_specs=[blk, blk, blk, blk, pl.BlockSpec((PEER_HEADS, 1, tt), lambda i: (0, 0, i))],
        out_shape=[big, big, big, big, jax.ShapeDtypeStruct((PEER_HEADS, 1, t), f32)],
        scratch_shapes=[pltpu.VMEM((2 * PEER_HEADS, PEER_N_KEYS, tt), f32)],
        compiler_params=pltpu.CompilerParams(dimension_semantics=("parallel",),
                                             vmem_limit_bytes=VMEM_LIMIT_BYTES),
        name="peer_router",
    )(q, keys_bf16)


def _dense_kernel(x_ref, u_ref, vt_ref, s0_ref, s1_ref, e0_ref, e1_ref, tau_ref, res_ref, gate_ref,
                  o_ref, at_ref, wt_ref, acc_ref):
    e = pl.program_id(1)
    n_e = pl.num_programs(1)
    eb = u_ref.shape[0]
    tt = x_ref.shape[0]
    n_i1 = eb // PEER_N_KEYS
    n_chunks = tt // LANES

    n_sub = eb // PEER_SUB_EXPERTS
    i1_per_sub = PEER_SUB_EXPERTS // PEER_N_KEYS
    nt = (((1,), (1,)), ((), ()))

    @pl.when(e == 0)
    def _():
        acc_ref[...] = jnp.zeros_like(acc_ref)

    grp0 = pl.multiple_of((e * n_i1 // SUBLANES) * SUBLANES, SUBLANES)
    upper_half = (e * n_i1) % SUBLANES != 0
    x = x_ref[...]
    for sb in range(n_sub):
        rows = slice(sb * PEER_SUB_EXPERTS, (sb + 1) * PEER_SUB_EXPERTS)
        at_ref[rows, :] = lax.dot_general(u_ref[rows, :], x, nt, preferred_element_type=jnp.float32)
    for sb in range(n_sub):
        for c in range(n_chunks):
            lanes = slice(c * LANES, (c + 1) * LANES)
            for k in range(i1_per_sub):
                i1 = sb * i1_per_sub + k
                g = jnp.zeros((PEER_N_KEYS, LANES), jnp.float32)
                for h in range(PEER_HEADS):
                    s0g = s0_ref[h, pl.ds(grp0, SUBLANES), lanes]
                    e0g = e0_ref[h, pl.ds(grp0, SUBLANES), lanes]
                    s0row = jnp.where(upper_half, s0g[n_i1 + i1:n_i1 + i1 + 1], s0g[i1:i1 + 1])
                    e0row = jnp.where(upper_half, e0g[n_i1 + i1:n_i1 + i1 + 1], e0g[i1:i1 + 1])
                    g = g + jnp.where(s0row + s1_ref[h, :, lanes] >= tau_ref[h, :, lanes],
                                      e0row * e1_ref[h, :, lanes], 0.0)
                r = slice(i1 * PEER_N_KEYS, (i1 + 1) * PEER_N_KEYS)
                wt_ref[r, lanes] = (_gelu_tanh(at_ref[r, lanes]) * g).astype(jnp.bfloat16)
    for sb in range(n_sub):
        rows = slice(sb * PEER_SUB_EXPERTS, (sb + 1) * PEER_SUB_EXPERTS)
        acc_ref[...] += jnp.dot(vt_ref[:, rows], wt_ref[rows, :], preferred_element_type=jnp.float32)

    @pl.when(e == n_e - 1)
    def _():
        o_ref[...] = res_ref[...] + gate_ref[0] * acc_ref[...].T


def peer_dense(x_bf16, u_bf16, vt_bf16, s0, s1, e0, e1, tau, resid, gate, tokens_per_gate, tt, eb):
    t, d = x_bf16.shape
    n_exp = u_bf16.shape[0]
    assert t % tt == 0 and n_exp % eb == 0 and tokens_per_gate % tt == 0
    assert 2 * (eb // PEER_N_KEYS) == SUBLANES
    tiles_per_gate = tokens_per_gate // tt
    rblk = pl.BlockSpec((PEER_HEADS, PEER_N_KEYS, tt), lambda i, j: (0, 0, i))
    return pl.pallas_call(
        _dense_kernel,
        grid=(t // tt, n_exp // eb),
        in_specs=[pl.BlockSpec((tt, d), lambda i, j: (i, 0)),
                  pl.BlockSpec((eb, d), lambda i, j: (j, 0)),
                  pl.BlockSpec((d, eb), lambda i, j: (0, j)),
                  rblk, rblk, rblk, rblk,
                  pl.BlockSpec((PEER_HEADS, 1, tt), lambda i, j: (0, 0, i)),
                  pl.BlockSpec((tt, d), lambda i, j: (i, 0)),
                  pl.BlockSpec((1, 1, d), lambda i, j: (i // tiles_per_gate, 0, 0))],
        out_specs=pl.BlockSpec((tt, d), lambda i, j: (i, 0)),
        out_shape=jax.ShapeDtypeStruct((t, d), jnp.float32),
        scratch_shapes=[pltpu.VMEM((eb, tt), jnp.float32),
                        pltpu.VMEM((eb, tt), jnp.bfloat16),
                        pltpu.VMEM((d, tt), jnp.float32)],
        compiler_params=pltpu.CompilerParams(dimension_semantics=("parallel", "arbitrary"),
                                             vmem_limit_bytes=VMEM_LIMIT_BYTES),
        name="peer_dense",
    )(x_bf16, u_bf16, vt_bf16, s0, s1, e0, e1, tau, resid, gate)


def peer_layer(x, norm_g, mods, w_q, keys, u_bf16, vt_bf16):
    b, l, d = x.shape
    h = modulate(rmsnorm(x, norm_g), mods[3], mods[4]).reshape(b * l, d)
    q = mm(h, w_q)
    gate = mods[5]
    tokens_per_gate = (b * l) // gate.shape[0]
    s0, s1, e0, e1, tau = peer_router(q, keys.astype(jnp.bfloat16), PEER_TOKEN_TILE)
    out = peer_dense(h.astype(jnp.bfloat16), u_bf16, vt_bf16, s0, s1, e0, e1, tau,
                     x.reshape(b * l, d), gate, tokens_per_gate, PEER_TOKEN_TILE, PEER_EXPERT_TILE)
    return out.reshape(b, l, d)


def kernel(x_prompt, x_sample, c, cache_na_k, cache_na_v, cache_diff_k, cache_diff_v, state_s5_re, state_s5_im, c_ctx, mod_w, mod_b, norm_mix_g, norm_ffn_g, ev_w_in, ev_w_out, na_q_g, na_k_g, na_rpb, hy_conv_w, hy_conv_b, hy_w1, hy_b1, hy_f1, hy_w2, hy_b2, hy_f2, hy_w3, hy_decay, hy_bias, od_w_in, od_w_out, s5_lam_re, s5_lam_im, s5_log_dt, s5_b_re, s5_b_im, s5_c_re, s5_c_im, s5_d, s5_glu_w, s5_glu_b, diff_q_g, diff_k_g, diff_lq1, diff_lk1, diff_lq2, diff_lk2, diff_subln_g, peer_w_q, peer_keys, peer_u, peer_v):
    bp, lp, d = x_prompt.shape
    bs, ls, _ = x_sample.shape
    rope_p, rope_s = rope_tables(lp), rope_tables(ls)
    xp, xs = x_prompt, x_sample
    new_na_k, new_na_v, new_dk, new_dv, new_sr, new_si = [], [], [], [], [], []
    for i in range(DEPTH):
        mp = adaln(c_ctx[None, :], mod_w[i], mod_b[i])
        ms = adaln(c, mod_w[i], mod_b[i])
        hp = modulate(rmsnorm(xp, norm_mix_g[i]), mp[0], mp[1]).reshape(bp * lp, d)
        hs = modulate(rmsnorm(xs, norm_mix_g[i]), ms[0], ms[1]).reshape(bs * ls, d)
        j = i // 2
        if i % 2 == 0:
            ev = (ev_w_in[j], ev_w_out[j], na_q_g[j], na_k_g[j], hy_conv_w[j], hy_conv_b[j],
                  hy_w1[j], hy_b1[j], hy_f1[j], hy_w2[j], hy_b2[j], hy_f2[j], hy_w3[j], hy_decay[j], hy_bias[j])
            bias_tabs = na_bias_tables(na_rpb[j], ls // GRID_W)
            op, kp, vp = even_mixer(hp, bp, lp, ev, None, None)
            os_, _, _ = even_mixer(hs, bs, ls, ev, (cache_na_k[:, j], cache_na_v[:, j]), bias_tabs)
            new_na_k.append(kp)
            new_na_v.append(vp)
        else:
            lam_init = 0.8 - 0.6 * math.exp(-0.3 * i)
            s5_prep = s5_prepare(s5_lam_re[j], s5_lam_im[j], s5_log_dt[j], s5_b_re[j], s5_b_im[j],
                                 s5_c_re[j], s5_c_im[j])
            od = (od_w_in[j], od_w_out[j], s5_prep, s5_d[j], s5_glu_w[j], s5_glu_b[j], diff_q_g[j], diff_k_g[j],
                  diff_lq1[j], diff_lk1[j], diff_lq2[j], diff_lk2[j], diff_subln_g[j])
            op, kp, vp, sr, si = odd_mixer(hp, bp, lp, od, lam_init, None, None, None, rope_p)
            os_, _, _, _, _ = odd_mixer(hs, bs, ls, od, lam_init, state_s5_re[:, j], state_s5_im[:, j],
                                        (cache_diff_k[:, j], cache_diff_v[:, j]), rope_s)
            new_dk.append(kp)
            new_dv.append(vp)
            new_sr.append(sr)
            new_si.append(si)
        xp = xp + mp[2] * op.reshape(bp, lp, d)
        xs = xs + ms[2] * os_.reshape(bs, ls, d)
        u_bf16 = peer_u[i].astype(jnp.bfloat16)
        vt_bf16 = peer_v[i].T.astype(jnp.bfloat16)
        xp = peer_layer(xp, norm_ffn_g[i], mp, peer_w_q[i], peer_keys[i], u_bf16, vt_bf16)
        xs = peer_layer(xs, norm_ffn_g[i], ms, peer_w_q[i], peer_keys[i], u_bf16, vt_bf16)
    return (xp, xs, jnp.stack(new_na_k, axis=1), jnp.stack(new_na_v, axis=1),
            jnp.stack(new_dk, axis=1), jnp.stack(new_dv, axis=1),
            jnp.stack(new_sr, axis=1), jnp.stack(new_si, axis=1))
```

```python
import functools
import math

import jax
import jax.numpy as jnp
from jax import lax
from jax.experimental import pallas as pl
from jax.experimental.pallas import tpu as pltpu

D_MODEL = 2048
DEPTH = 2
GRID_W = 64
MIX_WIDTH = D_MODEL // 2
HEAD_DIM = 128
NA_HEADS = MIX_WIDTH // HEAD_DIM
NA_KH = 8
NA_KW = 16
HY_WIDTH = MIX_WIDTH
HY_ORDER = 2
HY_POS_EMB = 33
S5_WIDTH = MIX_WIDTH
S5_GROUP = 16
S5_GROUPS = S5_WIDTH // S5_GROUP
S5_STATE = 64
DIFF_HEADS = MIX_WIDTH // HEAD_DIM
DIFF_QK_DIM = HEAD_DIM // 2
DIFF_V_DIM = HEAD_DIM
ROPE_BASE = 10000.0
PEER_HEADS = 8
PEER_N_KEYS = 128
PEER_QUERY_DIM = 256
PEER_TOPK = 16
NORM_EPS = 1e-6
NEG_INF = -1e30

LANES = 128
SUBLANES = 8
VMEM_LIMIT_BYTES = 56 * 1024 * 1024


MM_ROW_TILE = 1024
MM_COL_TILE = 512


def _mm_norm_kernel(x_ref, g_ref, sc_ref, sh_ref, w_ref, *out_refs, emit_h):
    if emit_h:
        o_ref, h_ref, hs_ref = out_refs
    else:
        o_ref, hs_ref = out_refs

    @pl.when(pl.program_id(1) == 0)
    def _():
        x = x_ref[...]
        y = x * lax.rsqrt(jnp.mean(x * x, axis=-1, keepdims=True) + NORM_EPS) * g_ref[...]
        hs_ref[...] = (y * (1.0 + sc_ref[0]) + sh_ref[0]).astype(jnp.bfloat16)
        if emit_h:
            h_ref[...] = hs_ref[...]

    o_ref[...] = jnp.dot(hs_ref[...], w_ref[...].astype(jnp.bfloat16), preferred_element_type=jnp.float32)


def mm_norm(x, norm_g, shift, scale, w, emit_h=False):
    m, k = x.shape
    n = w.shape[1]
    tm, tn = min(MM_ROW_TILE, m), min(MM_COL_TILE, n)
    nb = scale.shape[0]
    assert m % tm == 0 and n % tn == 0 and (m // nb) % tm == 0
    tiles_per_mod = (m // nb) // tm
    mod = pl.BlockSpec((1, 1, k), lambda i, j: (i // tiles_per_mod, 0, 0))
    out_specs = [pl.BlockSpec((tm, tn), lambda i, j: (i, j))]
    out_shape = [jax.ShapeDtypeStruct((m, n), jnp.float32)]
    if emit_h:
        out_specs.append(pl.BlockSpec((tm, k), lambda i, j: (i, 0)))
        out_shape.append(jax.ShapeDtypeStruct((m, k), jnp.bfloat16))
    res = pl.pallas_call(
        functools.partial(_mm_norm_kernel, emit_h=emit_h),
        grid=(m // tm, n // tn),
        in_specs=[pl.BlockSpec((tm, k), lambda i, j: (i, 0)),
                  pl.BlockSpec((1, k), lambda i, j: (0, 0)),
                  mod, mod,
                  pl.BlockSpec((k, tn), lambda i, j: (0, j))],
        out_specs=out_specs,
        out_shape=out_shape,
        scratch_shapes=[pltpu.VMEM((tm, k), jnp.bfloat16)],
        compiler_params=pltpu.CompilerParams(dimension_semantics=("parallel", "arbitrary"),
                                             vmem_limit_bytes=VMEM_LIMIT_BYTES),
        name="mm_norm",
    )(x, norm_g.astype(jnp.float32).reshape(1, k), scale, shift, w)
    return res if emit_h else res[0]


def _mm_pair_res_kernel(a_ref, b_ref, wa_ref, wb_ref, r_ref, g_ref, o_ref):
    y = (jnp.dot(a_ref[...].astype(jnp.bfloat16), wa_ref[...].astype(jnp.bfloat16), preferred_element_type=jnp.float32)
         + jnp.dot(b_ref[...].astype(jnp.bfloat16), wb_ref[...].astype(jnp.bfloat16), preferred_element_type=jnp.float32))
    o_ref[...] = r_ref[...] + g_ref[0] * y


def mm_pair_res(a, b, w, resid, gate):
    m, kh = a.shape
    n = w.shape[1]
    tm, tn = min(MM_ROW_TILE, m), min(MM_COL_TILE, n)
    nb = gate.shape[0]
    assert m % tm == 0 and n % tn == 0 and (m // nb) % tm == 0 and w.shape[0] == 2 * kh
    tiles_per_mod = (m // nb) // tm
    return pl.pallas_call(
        _mm_pair_res_kernel,
        grid=(m // tm, n // tn),
        in_specs=[pl.BlockSpec((tm, kh), lambda i, j: (i, 0)),
                  pl.BlockSpec((tm, kh), lambda i, j: (i, 0)),
                  pl.BlockSpec((kh, tn), lambda i, j: (0, j)),
                  pl.BlockSpec((kh, tn), lambda i, j: (1, j)),
                  pl.BlockSpec((tm, tn), lambda i, j: (i, j)),
                  pl.BlockSpec((1, 1, tn), lambda i, j: (i // tiles_per_mod, 0, j))],
        out_specs=pl.BlockSpec((tm, tn), lambda i, j: (i, j)),
        out_shape=jax.ShapeDtypeStruct((m, n), jnp.float32),
        compiler_params=pltpu.CompilerParams(dimension_semantics=("parallel", "arbitrary"),
                                             vmem_limit_bytes=VMEM_LIMIT_BYTES),
        name="mm_pair_res",
    )(a, b, w, w, resid, gate)


def adaln(cond, w, b):
    m = jax.nn.silu(cond) @ w + b
    return jnp.split(m[:, None, :], 6, axis=-1)


def hyena_filters(l, w1, b1, f1, w2, b2, f2, w3, decay):
    t = jnp.linspace(0.0, 1.0, l, dtype=jnp.float32)[:, None]
    bands = (HY_POS_EMB - 1) // 2
    w_ang = 2.0 * math.pi * jnp.arange(l, dtype=jnp.float32)[:, None] / l
    freqs = jnp.linspace(1e-4, bands - 1, bands, dtype=jnp.float32)[None, :]
    z = jnp.concatenate([t, jnp.cos(freqs * w_ang), -jnp.sin(freqs * w_ang)], axis=-1)
    h = jnp.sin(f1 * (z @ w1 + b1))
    h = jnp.sin(f2 * (h @ w2 + b2))
    h = (h @ w3).reshape(l, 2, HY_ORDER, HY_WIDTH).astype(jnp.float32)
    h = h * jnp.exp(-t.reshape(l, 1, 1, 1) * jnp.abs(decay.astype(jnp.float32)))
    h_f, h_b = h[:, 0], h[:, 1]
    zero = jnp.zeros((1, HY_ORDER, HY_WIDTH), jnp.float32)
    return jnp.concatenate([h_f, zero, h_b[1:][::-1]], axis=0)


def _split_bf16(x):
    hi = x.astype(jnp.bfloat16)
    lo = (x - hi.astype(jnp.float32)).astype(jnp.bfloat16)
    return hi, lo


def _dot3(a_hi, a_lo, x):
    m = a_hi.shape[0]
    xh, xl = _split_bf16(x)
    r = jnp.dot(jnp.concatenate([a_hi, a_lo], axis=0), xh, preferred_element_type=jnp.float32)
    return r[:m] + r[m:] + jnp.dot(a_hi, xl, preferred_element_type=jnp.float32)


def hyena_factors(l):
    n = 2 * l
    n2 = 64 if n >= 8192 else 16
    return n // n2, n2


def hyena_tables(l):
    n = 2 * l
    n1, n2 = hyena_factors(l)
    t = (n2 * jnp.arange(n1 // 2)[None, None, :] + jnp.arange(n2)[:, None, None])
    k1 = jnp.arange(n1)[None, :, None]
    ang = (2.0 * math.pi / n) * ((t * k1) % n).astype(jnp.float32)
    fa = jnp.concatenate([jnp.cos(ang), -jnp.sin(ang)], axis=1)
    fc = jnp.swapaxes(fa, 1, 2) / n
    a2 = (2.0 * math.pi / n2) * ((jnp.arange(n2)[:, None] * jnp.arange(n2)[None, :]) % n2).astype(jnp.float32)
    c, s = jnp.cos(a2), jnp.sin(a2)
    fb = jnp.concatenate([jnp.concatenate([c, s], axis=1), jnp.concatenate([-s, c], axis=1)], axis=0)
    fbi = jnp.concatenate([jnp.concatenate([c, -s], axis=1), jnp.concatenate([s, c], axis=1)], axis=0)
    return tuple(_split_bf16(x) for x in (fa, fc, fb, fbi))


def hyena_spectrum(filt, l):
    n1, n2 = hyena_factors(l)
    kf = jnp.fft.fft(filt, axis=0)
    kf = jnp.stack([jnp.real(kf), jnp.imag(kf)], axis=0).astype(jnp.float32)
    kf = kf.reshape(2, n2, n1, HY_ORDER, HY_WIDTH)
    return jnp.transpose(kf, (3, 0, 2, 1, 4))


def _hy_stage_a_kernel(z_ref, fh_ref, fl_ref, o_ref):
    o_ref[0] = _dot3(fh_ref[0], fl_ref[0], z_ref[0])


def _hy_stage_b_kernel(s_ref, k_ref, fh_ref, fl_ref, gh_ref, gl_ref, o_ref):
    n2 = s_ref.shape[3]
    y = jnp.concatenate([s_ref[0, 0, 0], s_ref[0, 1, 0]], axis=0)
    z = _dot3(fh_ref[...], fl_ref[...], y)
    zr, zi = z[:n2], z[n2:]
    kr, ki = k_ref[0, 0], k_ref[1, 0]
    p = jnp.concatenate([zr * kr - zi * ki, zr * ki + zi * kr], axis=0)
    q = _dot3(gh_ref[...], gl_ref[...], p)
    o_ref[0, 0, 0] = q[:n2]
    o_ref[0, 1, 0] = q[n2:]


def _hy_stage_c_kernel(q_ref, fh_ref, fl_ref, z_ref, g_ref, b_ref, o_ref):
    y = _dot3(fh_ref[0], fl_ref[0], q_ref[0])
    o_ref[0] = g_ref[0] * (y + z_ref[0] * b_ref[...])


def hyena_long_conv(z, z_col, gate, gate_col, bias_o, spec_o, tables, l):
    (fah, fal), (fch, fcl), (fbh, fbl), (fgh, fgl) = tables
    b = z.shape[0]
    c = HY_WIDTH
    n1, n2 = hyena_factors(l)
    h1 = n1 // 2
    cp = pltpu.CompilerParams(dimension_semantics=("parallel", "parallel"), vmem_limit_bytes=VMEM_LIMIT_BYTES)
    zblocks = z.shape[-1] // c
    gblocks = gate.shape[-1] // c
    zv = z.reshape(b, h1, n2 * z.shape[-1])
    gv = gate.reshape(b, h1, n2 * gate.shape[-1])
    s1 = pl.pallas_call(
        _hy_stage_a_kernel,
        grid=(b, n2),
        in_specs=[pl.BlockSpec((1, h1, c), lambda i, j: (i, 0, j * zblocks + z_col)),
                  pl.BlockSpec((1, 2 * n1, h1), lambda i, j: (j, 0, 0)),
                  pl.BlockSpec((1, 2 * n1, h1), lambda i, j: (j, 0, 0))],
        out_specs=pl.BlockSpec((1, 2 * n1, c), lambda i, j: (i, 0, j)),
        out_shape=jax.ShapeDtypeStruct((b, 2 * n1, n2 * c), jnp.float32),
        compiler_params=cp, name="hyena_stage_a",
    )(zv, fah, fal)
    s1 = s1.reshape(b, 2, n1, n2, c)
    mat = pl.BlockSpec((2 * n2, 2 * n2), lambda k, i: (0, 0))
    q = pl.pallas_call(
        _hy_stage_b_kernel,
        grid=(n1, b),
        in_specs=[pl.BlockSpec((1, 2, 1, n2, c), lambda k, i: (i, 0, k, 0, 0)),
                  pl.BlockSpec((2, 1, n2, c), lambda k, i: (0, k, 0, 0)),
                  mat, mat, mat, mat],
        out_specs=pl.BlockSpec((1, 2, 1, n2, c), lambda k, i: (i, 0, k, 0, 0)),
        out_shape=jax.ShapeDtypeStruct((b, 2, n1, n2, c), jnp.float32),
        compiler_params=cp, name="hyena_stage_b",
    )(s1, spec_o, fbh, fbl, fgh, fgl)
    q = q.reshape(b, 2 * n1, n2 * c)
    out = pl.pallas_call(
        _hy_stage_c_kernel,
        grid=(b, n2),
        in_specs=[pl.BlockSpec((1, 2 * n1, c), lambda i, j: (i, 0, j)),
                  pl.BlockSpec((1, h1, 2 * n1), lambda i, j: (j, 0, 0)),
                  pl.BlockSpec((1, h1, 2 * n1), lambda i, j: (j, 0, 0)),
                  pl.BlockSpec((1, h1, c), lambda i, j: (i, 0, j * zblocks + z_col)),
                  pl.BlockSpec((1, h1, c), lambda i, j: (i, 0, j * gblocks + gate_col)),
                  pl.BlockSpec((1, c), lambda i, j: (0, 0))],
        out_specs=pl.BlockSpec((1, h1, c), lambda i, j: (i, 0, j)),
        out_shape=jax.ShapeDtypeStruct((b, h1, n2 * c), jnp.float32),
        compiler_params=cp, name="hyena_stage_c",
    )(q, fch, fcl, zv, gv, bias_o.astype(jnp.float32).reshape(1, c))
    return out.reshape(b, l, c)


def hyena(u, conv_w, conv_b, filt, bias):
    b, l, _ = u.shape
    up = jnp.pad(u, ((0, 0), (1, 1), (0, 0)))
    u = up[:, :-2] * conv_w[0] + up[:, 1:-1] * conv_w[1] + up[:, 2:] * conv_w[2] + conv_b
    tables = hyena_tables(l)
    spec = hyena_spectrum(hyena_filters(l, *filt), l)
    z, z_col = u, 0
    for o in range(HY_ORDER):
        z = hyena_long_conv(z, z_col, u, 1 + o, bias[o], spec[o], tables, l)
        z_col = 0
    return z


S5_TILE_GROUPS = 8
S5_TILES = S5_GROUPS // S5_TILE_GROUPS
S5_TILE_IN = S5_TILE_GROUPS * S5_GROUP
S5_TILE_STATE = S5_TILE_GROUPS * S5_STATE
S5_ROWS_PER_STEP = 2048


def s5_prepare(lam_re, lam_im, log_dt, b_re, b_im, c_re, c_im):
    f32 = jnp.float32
    lam_re, lam_im = lam_re.astype(f32), lam_im.astype(f32)
    dt = jnp.exp(log_dt.astype(f32))[..., None]
    mag = jnp.exp(lam_re * dt)
    ar, ai = mag * jnp.cos(lam_im * dt), mag * jnp.sin(lam_im * dt)
    den = lam_re * lam_re + lam_im * lam_im
    cr = ((ar - 1.0) * lam_re + ai * lam_im) / den
    ci = (ai * lam_re - (ar - 1.0) * lam_im) / den
    bbr = cr[..., None] * b_re - ci[..., None] * b_im
    bbi = cr[..., None] * b_im + ci[..., None] * b_re
    eye = jnp.eye(S5_TILE_GROUPS, dtype=f32)

    def tile_in(bb):
        x = bb.reshape(2, S5_TILES, S5_TILE_GROUPS, S5_STATE, S5_GROUP)
        x = jnp.einsum('dtgnp,gh->dtgphn', x, eye)
        return x.reshape(2, S5_TILES, S5_TILE_IN, S5_TILE_STATE)

    def tile_out(cc):
        x = cc.astype(f32).reshape(2, S5_TILES, S5_TILE_GROUPS, S5_GROUP, S5_STATE)
        x = jnp.einsum('dtgpn,gh->dtgnhp', x, eye)
        return x.reshape(2, S5_TILES, S5_TILE_STATE, S5_TILE_IN)

    win = jnp.concatenate([tile_in(bbr), tile_in(bbi)], axis=-1).astype(jnp.bfloat16)
    wout = jnp.concatenate([tile_out(c_re), -tile_out(c_im)], axis=-2).astype(jnp.bfloat16)
    a = jnp.stack([ar.reshape(2, S5_TILES, S5_TILE_STATE), ai.reshape(2, S5_TILES, S5_TILE_STATE)], axis=2)
    return win, wout, a


def _s5_scan_kernel(u_ref, win_ref, wout_ref, a_ref, h0_ref, y_ref, fin_ref, bu_ref, st_ref, *, batch):
    d = pl.program_id(0)
    c = pl.program_id(2)
    n_c = pl.num_programs(2)
    ns = S5_TILE_STATE
    steps = u_ref.shape[0] // batch

    @pl.when(c == 0)
    def _():
        st_ref[...] = h0_ref[0, 0]

    bu_ref[...] = jnp.dot(u_ref[...].astype(jnp.bfloat16), win_ref[0, 0], preferred_element_type=jnp.float32)
    ar = jnp.broadcast_to(a_ref[0, 0, 0:1, :], (batch, ns))
    ai = jnp.broadcast_to(a_ref[0, 0, 1:2, :], (batch, ns))

    def step(t, carry):
        hr, hi = carry
        te = jnp.where(d == 0, t, steps - 1 - t)
        r0 = pl.multiple_of(te * batch, batch)
        nr = ar * hr - ai * hi + bu_ref[pl.ds(r0, batch), 0:ns]
        ni = ar * hi + ai * hr + bu_ref[pl.ds(r0, batch), ns:2 * ns]
        bu_ref[pl.ds(r0, batch), 0:ns] = nr
        bu_ref[pl.ds(r0, batch), ns:2 * ns] = ni
        return nr, ni

    hr, hi = lax.fori_loop(0, steps, step, (st_ref[:, 0:ns], st_ref[:, ns:2 * ns]), unroll=4)
    st_ref[:, 0:ns] = hr
    st_ref[:, ns:2 * ns] = hi
    y_ref[0] = jnp.dot(bu_ref[...].astype(jnp.bfloat16), wout_ref[0, 0], preferred_element_type=jnp.float32)

    @pl.when(c == n_c - 1)
    def _():
        fin_ref[0, 0] = st_ref[...]


def s5_scan(u_tm, win, wout, a, h0, batch):
    rows = u_tm.shape[0]
    r = min(S5_ROWS_PER_STEP, rows)
    assert rows % r == 0 and r % batch == 0
    n_c = rows // r

    def chunk(d, c):
        return c + d * (n_c - 1 - 2 * c)

    return pl.pallas_call(
        functools.partial(_s5_scan_kernel, batch=batch),
        grid=(2, S5_TILES, n_c),
        in_specs=[pl.BlockSpec((r, S5_TILE_IN), lambda d, j, c: (chunk(d, c), j)),
                  pl.BlockSpec((1, 1, S5_TILE_IN, 2 * S5_TILE_STATE), lambda d, j, c: (d, j, 0, 0)),
                  pl.BlockSpec((1, 1, 2 * S5_TILE_STATE, S5_TILE_IN), lambda d, j, c: (d, j, 0, 0)),
                  pl.BlockSpec((1, 1, 2, S5_TILE_STATE), lambda d, j, c: (d, j, 0, 0)),
                  pl.BlockSpec((1, 1, batch, 2 * S5_TILE_STATE), lambda d, j, c: (d, j, 0, 0))],
        out_specs=[pl.BlockSpec((1, r, S5_TILE_IN), lambda d, j, c: (d, chunk(d, c), j)),
                   pl.BlockSpec((1, 1, batch, 2 * S5_TILE_STATE), lambda d, j, c: (d, j, 0, 0))],
        out_shape=[jax.ShapeDtypeStruct((2, rows, S5_WIDTH), jnp.float32),
                   jax.ShapeDtypeStruct((2, S5_TILES, batch, 2 * S5_TILE_STATE), jnp.float32)],
        scratch_shapes=[pltpu.VMEM((r, 2 * S5_TILE_STATE), jnp.float32),
                        pltpu.VMEM((batch, 2 * S5_TILE_STATE), jnp.float32)],
        compiler_params=pltpu.CompilerParams(dimension_semantics=("parallel", "parallel", "arbitrary"),
                                             vmem_limit_bytes=VMEM_LIMIT_BYTES),
        name="s5_scan",
    )(u_tm, win, wout, a, h0)


def _gelu_tanh(x):
    return 0.5 * x * (1.0 + jnp.tanh(math.sqrt(2.0 / math.pi) * (x + 0.044715 * (x * x * x))))


def _s5_glu_kernel(u_ref, y_ref, d_ref, w_ref, b_ref, o_ref):
    y = d_ref[...] * u_ref[...] + y_ref[0] + y_ref[1]
    y = _gelu_tanh(y)
    z = jnp.dot(y.astype(jnp.bfloat16), w_ref[...], preferred_element_type=jnp.float32) + b_ref[...]
    o_ref[...] = y * (1.0 / (1.0 + jnp.exp(-z)))


def s5_glu(u_tm, y, d_skip, glu_w_bf16, glu_b, tr=1024):
    rows, w = u_tm.shape
    tr = min(tr, rows)
    assert rows % tr == 0
    return pl.pallas_call(
        _s5_glu_kernel,
        grid=(rows // tr,),
        in_specs=[pl.BlockSpec((tr, w), lambda i: (i, 0)),
                  pl.BlockSpec((2, tr, w), lambda i: (0, i, 0)),
                  pl.BlockSpec((1, w), lambda i: (0, 0)),
                  pl.BlockSpec((w, w), lambda i: (0, 0)),
                  pl.BlockSpec((1, w), lambda i: (0, 0))],
        out_specs=pl.BlockSpec((tr, w), lambda i: (i, 0)),
        out_shape=jax.ShapeDtypeStruct((rows, w), jnp.float32),
        compiler_params=pltpu.CompilerParams(dimension_semantics=("parallel",),
                                             vmem_limit_bytes=VMEM_LIMIT_BYTES),
        name="s5_glu",
    )(u_tm, y, d_skip.reshape(1, w), glu_w_bf16, glu_b.reshape(1, w))


def s5_mixer(u, h0_re, h0_im, prep, d_skip, glu_w, glu_b):
    win, wout, a = prep
    b, l, w = u.shape
    u_tm = jnp.swapaxes(u, 0, 1).reshape(l * b, w)
    if h0_re is None:
        h0 = jnp.zeros((2, S5_TILES, b, 2 * S5_TILE_STATE), jnp.float32)
    else:
        def tiles(h):
            return jnp.transpose(h.astype(jnp.float32).reshape(b, 2, S5_TILES, S5_TILE_STATE), (1, 2, 0, 3))
        h0 = jnp.concatenate([tiles(h0_re), tiles(h0_im)], axis=-1)
    y, fin = s5_scan(u_tm, win, wout, a, h0, b)
    out_tm = s5_glu(u_tm, y, d_skip, glu_w.astype(jnp.bfloat16), glu_b)
    out = jnp.swapaxes(out_tm.reshape(l, b, w), 0, 1)

    def untile(f):
        return jnp.transpose(f, (2, 0, 1, 3)).reshape(b, 2, S5_GROUPS, S5_STATE)
    return out, untile(fin[..., :S5_TILE_STATE]), untile(fin[..., S5_TILE_STATE:])


def _half_mean_matrix():
    i = lax.broadcasted_iota(jnp.int32, (LANES, LANES), 0) // DIFF_QK_DIM
    j = lax.broadcasted_iota(jnp.int32, (LANES, LANES), 1) // DIFF_QK_DIM
    return jnp.where(i == j, 1.0 / DIFF_QK_DIM, 0.0).astype(jnp.bfloat16)


def _rms_groups(x, gain, avg):
    xx = x * x
    hi = xx.astype(jnp.bfloat16)
    lo = (xx - hi.astype(jnp.float32)).astype(jnp.bfloat16)
    ms = (jnp.dot(hi, avg, preferred_element_type=jnp.float32)
          + jnp.dot(lo, avg, preferred_element_type=jnp.float32))
    return x * lax.rsqrt(ms + NORM_EPS) * gain


def _rope_lanes(y, cos, sin_signed, first_half):
    rot = jnp.where(first_half, pltpu.roll(y, LANES - 16, 1), pltpu.roll(y, 16, 1))
    return y * cos + rot * sin_signed


def _diff_prep_kernel(q_ref, k_ref, qg_ref, kg_ref, cos_ref, sin_ref, qo_ref, ko_ref, kn_ref, *, use_rope):
    avg = _half_mean_matrix()
    lane = lax.broadcasted_iota(jnp.int32, (1, LANES), 1)
    first_half = (lane % 32) < 16
    scale = DIFF_QK_DIM ** -0.5
    for h in range(DIFF_HEADS):
        cols = slice(h * LANES, (h + 1) * LANES)
        qn = _rms_groups(q_ref[:, cols], qg_ref[...], avg)
        kn = _rms_groups(k_ref[:, cols], kg_ref[...], avg)
        kn_ref[:, cols] = kn
        if use_rope:
            qn = _rope_lanes(qn, cos_ref[...], sin_ref[...], first_half)
            kn = _rope_lanes(kn, cos_ref[...], sin_ref[...], first_half)
        qo_ref[:, cols] = (qn * scale).astype(jnp.bfloat16)
        ko_ref[:, cols] = kn.astype(jnp.bfloat16)


def diff_prep(p, q_g, k_g, cos, sin_signed, seq_len, use_rope, tr=512):
    rows = p.shape[0]
    tr = min(tr, seq_len)
    assert seq_len % tr == 0
    w = DIFF_HEADS * LANES
    per_seq = seq_len // tr
    g2 = lambda g: jnp.tile(g.astype(jnp.float32), 2).reshape(1, LANES)
    return pl.pallas_call(
        functools.partial(_diff_prep_kernel, use_rope=use_rope),
        grid=(rows // tr,),
        in_specs=[pl.BlockSpec((tr, w), lambda i: (i, 1)),
                  pl.BlockSpec((tr, w), lambda i: (i, 2)),
                  pl.BlockSpec((1, LANES), lambda i: (0, 0)),
                  pl.BlockSpec((1, LANES), lambda i: (0, 0)),
                  pl.BlockSpec((tr, LANES), lambda i: (i % per_seq, 0)),
                  pl.BlockSpec((tr, LANES), lambda i: (i % per_seq, 0))],
        out_specs=[pl.BlockSpec((tr, w), lambda i: (i, 0))] * 3,
        out_shape=[jax.ShapeDtypeStruct((rows, w), jnp.bfloat16), jax.ShapeDtypeStruct((rows, w), jnp.bfloat16),
                   jax.ShapeDtypeStruct((rows, w), jnp.float32)],
        compiler_params=pltpu.CompilerParams(dimension_semantics=("parallel",),
                                             vmem_limit_bytes=VMEM_LIMIT_BYTES),
        name="diff_prep",
    )(p, p, g2(q_g), g2(k_g), cos, sin_signed)


def rope_tables(l):
    t = jnp.arange(l)
    row = (t // GRID_W).astype(jnp.float32)
    col = (t % GRID_W).astype(jnp.float32)
    nf = DIFF_QK_DIM // 4
    inv = ROPE_BASE ** (-jnp.arange(nf, dtype=jnp.float32) / nf)
    ang = jnp.stack([row[:, None] * inv, col[:, None] * inv], axis=1)
    ang = jnp.stack([ang, ang], axis=2).reshape(l, DIFF_QK_DIM)
    sign = jnp.where((jnp.arange(DIFF_QK_DIM) % 32) < 16, -1.0, 1.0)
    return jnp.tile(jnp.cos(ang), (1, 2)), jnp.tile(jnp.sin(ang) * sign, (1, 2))


def _diff_attn_kernel(*refs, has_ctx):
    if has_ctx:
        q_ref, k_ref, v_ref, ck_ref, cv_ref, lam_ref, g_ref, o_ref = refs
    else:
        q_ref, k_ref, v_ref, lam_ref, g_ref, o_ref = refs
    q = q_ref[...]
    lane = lax.broadcasted_iota(jnp.int32, (1, LANES), 1)
    zero = jnp.zeros_like(q)
    qs = (jnp.where(lane < DIFF_QK_DIM, q, zero), jnp.where(lane >= DIFF_QK_DIM, q, zero))
    nt = (((1,), (1,)), ((), ()))
    k = k_ref[...]
    ck = ck_ref[...].astype(jnp.bfloat16) if has_ctx else None
    lam = lam_ref[0:1, 0:1]
    w_self, w_ctx = None, None
    for i in range(2):
        s = lax.dot_general(qs[i], k, nt, preferred_element_type=jnp.float32)
        m = jnp.max(s, axis=-1, keepdims=True)
        if has_ctx:
            sc = lax.dot_general(qs[i], ck, nt, preferred_element_type=jnp.float32)
            m = jnp.maximum(m, jnp.max(sc, axis=-1, keepdims=True))
        p = jnp.exp(s - m)
        l = jnp.sum(p, axis=-1, keepdims=True)
        if has_ctx:
            pc = jnp.exp(sc - m)
            l = l + jnp.sum(pc, axis=-1, keepdims=True)
        coef = 1.0 / l if i == 0 else -lam / l
        w_self = p * coef if i == 0 else w_self + p * coef
        if has_ctx:
            w_ctx = pc * coef if i == 0 else w_ctx + pc * coef
    o = jnp.dot(w_self.astype(jnp.bfloat16), v_ref[...].astype(jnp.bfloat16), preferred_element_type=jnp.float32)
    if has_ctx:
        o = o + jnp.dot(w_ctx.astype(jnp.bfloat16), cv_ref[...].astype(jnp.bfloat16),
                        preferred_element_type=jnp.float32)
    ms = jnp.mean(o * o, axis=-1, keepdims=True)
    o_ref[...] = o * lax.rsqrt(ms + NORM_EPS) * g_ref[...]


def diff_attention(q_bf16, k_bf16, p, ctx_k, ctx_v, lam, gain, batch, seq_len, tq=256):
    tq = min(tq, seq_len)
    nq = seq_len // tq
    has_ctx = ctx_k is not None
    v_col0 = 3 * DIFF_HEADS
    in_specs = [pl.BlockSpec((tq, LANES), lambda b, h, i: (b * nq + i, h)),
                pl.BlockSpec((seq_len, LANES), lambda b, h, i: (b, h)),
                pl.BlockSpec((seq_len, LANES), lambda b, h, i: (b, v_col0 + h))]
    args = [q_bf16, k_bf16, p]
    if has_ctx:
        lc = ctx_k.shape[0] // batch
        in_specs += [pl.BlockSpec((lc, LANES), lambda b, h, i: (b, h))] * 2
        args += [ctx_k, ctx_v]
    in_specs += [pl.BlockSpec((1, LANES), lambda b, h, i: (0, 0))] * 2
    args += [jnp.broadcast_to(lam.astype(jnp.float32), (1, LANES)), gain.astype(jnp.float32).reshape(1, LANES)]
    return pl.pallas_call(
        functools.partial(_diff_attn_kernel, has_ctx=has_ctx),
        grid=(batch, DIFF_HEADS, nq),
        in_specs=in_specs,
        out_specs=pl.BlockSpec((tq, LANES), lambda b, h, i: (b * nq + i, h)),
        out_shape=jax.ShapeDtypeStruct((batch * seq_len, DIFF_HEADS * LANES), jnp.float32),
        compiler_params=pltpu.CompilerParams(dimension_semantics=("parallel", "parallel", "arbitrary"),
                                             vmem_limit_bytes=VMEM_LIMIT_BYTES),
        name="diff_attention",
    )(*args)


NA_Q_ROWS = 8
NA_WIN_ROWS = 16


def _na_prep_kernel(q_ref, k_ref, qg_ref, kg_ref, qo_ref, ko_ref, kn_ref):
    avg = jnp.full((LANES, LANES), 1.0 / HEAD_DIM, jnp.bfloat16)
    for h in range(NA_HEADS):
        cols = slice(h * LANES, (h + 1) * LANES)
        qo_ref[:, cols] = _rms_groups(q_ref[:, cols], qg_ref[...], avg).astype(jnp.bfloat16)
        kn = _rms_groups(k_ref[:, cols], kg_ref[...], avg)
        kn_ref[:, cols] = kn
        ko_ref[:, cols] = kn.astype(jnp.bfloat16)


def na_prep(p, q_g, k_g, tr=512):
    rows = p.shape[0]
    tr = min(tr, rows)
    w = NA_HEADS * LANES
    g1 = lambda g: g.astype(jnp.float32).reshape(1, LANES)
    return pl.pallas_call(
        _na_prep_kernel,
        grid=(rows // tr,),
        in_specs=[pl.BlockSpec((tr, w), lambda i: (i, 0)),
                  pl.BlockSpec((tr, w), lambda i: (i, 1)),
                  pl.BlockSpec((1, LANES), lambda i: (0, 0)),
                  pl.BlockSpec((1, LANES), lambda i: (0, 0))],
        out_specs=[pl.BlockSpec((tr, w), lambda i: (i, 0))] * 3,
        out_shape=[jax.ShapeDtypeStruct((rows, w), jnp.bfloat16), jax.ShapeDtypeStruct((rows, w), jnp.bfloat16),
                   jax.ShapeDtypeStruct((rows, w), jnp.float32)],
        compiler_params=pltpu.CompilerParams(dimension_semantics=("parallel",),
                                             vmem_limit_bytes=VMEM_LIMIT_BYTES),
        name="na_prep",
    )(p, p, g1(q_g), g1(k_g))


def na_bias_tables(rpb, rows):
    nblk = rows // NA_Q_ROWS
    cq = jnp.arange(GRID_W)[:, None]
    ck = jnp.arange(GRID_W)[None, :]
    col_start = jnp.clip(cq - NA_KW // 2, 0, GRID_W - NA_KW)
    col_ok = (ck >= col_start) & (ck < col_start + NA_KW)
    cidx = jnp.clip(ck - cq + NA_KW - 1, 0, 2 * NA_KW - 2)
    tabs = []
    for i in (0, 1, nblk - 1):
        base = min(max(NA_Q_ROWS * i - NA_KH // 2, 0), rows - NA_WIN_ROWS)
        r = NA_Q_ROWS * i + jnp.arange(NA_Q_ROWS)[:, None]
        rk = base + jnp.arange(NA_WIN_ROWS)[None, :]
        r0 = jnp.clip(r - NA_KH // 2, 0, rows - NA_KH)
        row_ok = (rk >= r0) & (rk < r0 + NA_KH)
        ridx = jnp.clip(rk - r + NA_KH - 1, 0, 2 * NA_KH - 2)
        b = rpb[:, ridx][:, :, :, cidx]
        ok = row_ok[:, :, None, None] & col_ok[None, None, :, :]
        b = jnp.where(ok[None], b.astype(jnp.float32), NEG_INF)
        tabs.append(jnp.transpose(b, (0, 1, 3, 2, 4)).reshape(NA_HEADS, NA_Q_ROWS * GRID_W, NA_WIN_ROWS * GRID_W))
    return jnp.stack(tabs, axis=0)


def _na_attn_kernel(*refs, windowed, rows):
    nt = (((1,), (1,)), ((), ()))
    scale = HEAD_DIM ** -0.5
    if windowed:
        q_ref, k_ref, v_ref, b_ref, ck_ref, cv_ref, o_ref = refs
        i = pl.program_id(2)
        base = jnp.clip(NA_Q_ROWS * i - NA_KH // 2, 0, rows - NA_WIN_ROWS)
        k0 = pl.multiple_of(base * GRID_W, GRID_W)
        nk = NA_WIN_ROWS * GRID_W
        q = q_ref[...]
        s = lax.dot_general(q, k_ref[pl.ds(k0, nk), :], nt, preferred_element_type=jnp.float32) * scale + b_ref[0, 0]
        sc = lax.dot_general(q, ck_ref[...].astype(jnp.bfloat16), nt, preferred_element_type=jnp.float32) * scale
        m = jnp.maximum(jnp.max(s, axis=-1, keepdims=True), jnp.max(sc, axis=-1, keepdims=True))
        p = jnp.exp(s - m)
        pc = jnp.exp(sc - m)
        l = jnp.sum(p, axis=-1, keepdims=True) + jnp.sum(pc, axis=-1, keepdims=True)
        o = (jnp.dot(p.astype(jnp.bfloat16), v_ref[pl.ds(k0, nk), :].astype(jnp.bfloat16),
                     preferred_element_type=jnp.float32)
             + jnp.dot(pc.astype(jnp.bfloat16), cv_ref[...].astype(jnp.bfloat16), preferred_element_type=jnp.float32))
    else:
        q_ref, k_ref, v_ref, o_ref = refs
        s = lax.dot_general(q_ref[...], k_ref[...], nt, preferred_element_type=jnp.float32) * scale
        m = jnp.max(s, axis=-1, keepdims=True)
        p = jnp.exp(s - m)
        l = jnp.sum(p, axis=-1, keepdims=True)
        o = jnp.dot(p.astype(jnp.bfloat16), v_ref[...].astype(jnp.bfloat16), preferred_element_type=jnp.float32)
    o_ref[...] = o / l


def na_attention(q_bf16, k_bf16, p, bias_tabs, ctx_k, ctx_v, batch, seq_len):
    windowed = bias_tabs is not None
    v_col0 = 2 * NA_HEADS
    rows = seq_len // GRID_W
    if windowed:
        tq = NA_Q_ROWS * GRID_W
        nq = seq_len // tq
        lc = ctx_k.shape[0] // batch
        nk = NA_WIN_ROWS * GRID_W
        in_specs = [pl.BlockSpec((tq, LANES), lambda b, h, i: (b * nq + i, h)),
                    pl.BlockSpec((seq_len, LANES), lambda b, h, i: (b, h)),
                    pl.BlockSpec((seq_len, LANES), lambda b, h, i: (b, v_col0 + h)),
                    pl.BlockSpec((1, 1, tq, nk),
                                 lambda b, h, i: (jnp.where(i == 0, 0, jnp.where(i == nq - 1, 2, 1)), h, 0, 0)),
                    pl.BlockSpec((lc, LANES), lambda b, h, i: (b, h)),
                    pl.BlockSpec((lc, LANES), lambda b, h, i: (b, h))]
        args = [q_bf16, k_bf16, p, bias_tabs, ctx_k, ctx_v]
    else:
        tq = seq_len
        nq = 1
        in_specs = [pl.BlockSpec((tq, LANES), lambda b, h, i: (b, h)),
                    pl.BlockSpec((seq_len, LANES), lambda b, h, i: (b, h)),
                    pl.BlockSpec((seq_len, LANES), lambda b, h, i: (b, v_col0 + h))]
        args = [q_bf16, k_bf16, p]
    return pl.pallas_call(
        functools.partial(_na_attn_kernel, windowed=windowed, rows=rows),
        grid=(batch, NA_HEADS, nq),
        in_specs=in_specs,
        out_specs=pl.BlockSpec((tq, LANES), lambda b, h, i: (b * nq + i, h)),
        out_shape=jax.ShapeDtypeStruct((batch * seq_len, NA_HEADS * LANES), jnp.float32),
        compiler_params=pltpu.CompilerParams(dimension_semantics=("parallel", "parallel", "arbitrary"),
                                             vmem_limit_bytes=VMEM_LIMIT_BYTES),
        name="na_attention",
    )(*args)


def even_mixer(x, mods, norm_g, b, l, ev, ctx_kv, bias_tabs):
    (w_in, w_out, q_g, k_g, conv_w, conv_b, w1, b1, f1, w2, b2, f2, w3, decay, bias) = ev
    p = mm_norm(x, norm_g, mods[0], mods[1], w_in)
    q, k, kn = na_prep(p, q_g, k_g)
    if ctx_kv is None:
        oa = na_attention(q, k, p, None, None, None, b, l)
    else:
        lc = ctx_kv[0].shape[1]
        oa = na_attention(q, k, p, bias_tabs, ctx_kv[0].reshape(b * lc, MIX_WIDTH),
                          ctx_kv[1].reshape(b * lc, MIX_WIDTH), b, l)
    hb = p[:, 3 * MIX_WIDTH:].reshape(b, l, 3 * HY_WIDTH)
    ob = hyena(hb, conv_w, conv_b, (w1, b1, f1, w2, b2, f2, w3, decay), bias)
    out = mm_pair_res(oa, ob.reshape(b * l, HY_WIDTH), w_out, x, mods[2])
    shp = (b, l, NA_HEADS, HEAD_DIM)
    return out, kn.reshape(shp), p[:, 2 * MIX_WIDTH:3 * MIX_WIDTH].reshape(shp)


def odd_mixer(x, mods, norm_g, b, l, od, lam_init, h0_re, h0_im, ctx_kv, rope):
    (w_in, w_out, s5_prep, d_skip, glu_w, glu_b, q_g, k_g, lq1, lk1, lq2, lk2, subln_g) = od
    p = mm_norm(x, norm_g, mods[0], mods[1], w_in)
    oc, fr, fi = s5_mixer(p[:, :S5_WIDTH].reshape(b, l, S5_WIDTH), h0_re, h0_im, s5_prep, d_skip, glu_w, glu_b)
    lam = (jnp.exp(jnp.sum(lq1 * lk1).astype(jnp.float32))
           - jnp.exp(jnp.sum(lq2 * lk2).astype(jnp.float32)) + lam_init)
    cos, sin_signed = rope
    q, k, kn = diff_prep(p, q_g, k_g, cos, sin_signed, l, use_rope=ctx_kv is not None)
    gain = subln_g.astype(jnp.float32) * (1.0 - lam_init)
    if ctx_kv is None:
        o = diff_attention(q, k, p, None, None, lam, gain, b, l)
    else:
        lc = ctx_kv[0].shape[1]
        o = diff_attention(q, k, p, ctx_kv[0].reshape(b * lc, MIX_WIDTH), ctx_kv[1].reshape(b * lc, MIX_WIDTH),
                           lam, gain, b, l)
    out = mm_pair_res(oc.reshape(b * l, S5_WIDTH), o, w_out, x, mods[2])
    kn = kn.reshape(b, l, DIFF_HEADS, 2, DIFF_QK_DIM)
    v = p[:, S5_WIDTH + 2 * MIX_WIDTH:].reshape(b, l, DIFF_HEADS, DIFF_V_DIM)
    return out, kn, v, fr, fi


PEER_TOKEN_TILE = 512
PEER_EXPERT_TILE = 1024
PEER_SUB_EXPERTS = 256


def _top16_rows(cur, iota):
    n = cur.shape[0]
    tops, idxs = [], []
    for _ in range(PEER_TOPK):
        m = jnp.max(cur, axis=0, keepdims=True)
        first = jnp.min(jnp.where(cur == m, iota, n), axis=0, keepdims=True)
        tops.append(m)
        idxs.append(first)
        cur = jnp.where(iota == first, -jnp.inf, cur)
    return tops, idxs


def _router_kernel(q_ref, keys_ref, th_ref, s1_ref, e0_ref, e1_ref, st_ref):
    tt = q_ref.shape[0]
    half = PEER_QUERY_DIM // 2
    for h in range(PEER_HEADS):
        for s in range(2):
            qs = q_ref[:, (2 * h + s) * half:(2 * h + s + 1) * half].astype(jnp.bfloat16)
            st = lax.dot_general(keys_ref[h, s], qs, (((1,), (1,)), ((), ())),
                                 preferred_element_type=jnp.float32)
            st_ref[2 * h + s] = st

    iota128 = lax.broadcasted_iota(jnp.int32, (PEER_N_KEYS, LANES), 0)
    iota256 = lax.broadcasted_iota(jnp.int32, (PEER_TOPK * PEER_TOPK, LANES), 0)
    n_chunks = tt // LANES

    def body(i, carry):
        h = i // n_chunks
        lane0 = pl.multiple_of((i % n_chunks) * LANES, LANES)
        s0 = st_ref[2 * h, :, pl.ds(lane0, LANES)]
        s1 = st_ref[2 * h + 1, :, pl.ds(lane0, LANES)]
        a, a_idx = _top16_rows(s0, iota128)
        b, _ = _top16_rows(s1, iota128)
        bmat = jnp.concatenate(b, axis=0)
        sums = [a[k] + bmat for k in range(PEER_TOPK)]
        f, _ = _top16_rows(jnp.concatenate(sums, axis=0), iota256)
        tau = f[PEER_TOPK - 1]
        z = jnp.ones_like(f[0])
        for k in range(1, PEER_TOPK):
            z = z + jnp.exp(f[k] - f[0])
        th = jnp.full((PEER_N_KEYS, LANES), jnp.inf, jnp.float32)
        for k in range(PEER_TOPK):
            th_k = jnp.min(jnp.where(sums[k] >= tau, bmat, jnp.inf), axis=0, keepdims=True)
            th = jnp.where(iota128 == a_idx[k], th_k, th)
        th_ref[h, :, pl.ds(lane0, LANES)] = th
        s1_ref[h, :, pl.ds(lane0, LANES)] = s1
        e0_ref[h, :, pl.ds(lane0, LANES)] = jnp.exp(s0 - a[0]) / z
        e1_ref[h, :, pl.ds(lane0, LANES)] = jnp.exp(s1 - b[0])
        return carry

    lax.fori_loop(0, PEER_HEADS * n_chunks, body, 0)


def peer_router(q, keys_bf16, tt):
    t = q.shape[0]
    f32 = jnp.float32
    big = jax.ShapeDtypeStruct((PEER_HEADS, PEER_N_KEYS, t), f32)
    blk = pl.BlockSpec((PEER_HEADS, PEER_N_KEYS, tt), lambda i: (0, 0, i))
    return pl.pallas_call(
        _router_kernel,
        grid=(t // tt,),
        in_specs=[pl.BlockSpec((tt, q.shape[1]), lambda i: (i, 0)),
                  pl.BlockSpec(keys_bf16.shape, lambda i: (0, 0, 0, 0))],
        out_specs=[blk, blk, blk, blk],
        out_shape=[big, big, big, big],
        scratch_shapes=[pltpu.VMEM((2 * PEER_HEADS, PEER_N_KEYS, tt), f32)],
        compiler_params=pltpu.CompilerParams(dimension_semantics=("parallel",),
                                             vmem_limit_bytes=VMEM_LIMIT_BYTES),
        name="peer_router",
    )(q, keys_bf16)


def _dense_kernel(x_ref, u_ref, vt_ref, th_ref, s1_ref, e0_ref, e1_ref, res_ref, gate_ref,
                  o_ref, at_ref, wt_ref, acc_ref):
    e = pl.program_id(1)
    n_e = pl.num_programs(1)
    eb = u_ref.shape[0]
    tt = x_ref.shape[0]
    n_chunks = tt // LANES
    n_sub = eb // PEER_SUB_EXPERTS
    i1_per_sub = PEER_SUB_EXPERTS // PEER_N_KEYS
    nt = (((1,), (1,)), ((), ()))

    @pl.when(e == 0)
    def _():
        acc_ref[...] = jnp.zeros_like(acc_ref)

    x = x_ref[...]
    for sb in range(n_sub):
        rows = slice(sb * PEER_SUB_EXPERTS, (sb + 1) * PEER_SUB_EXPERTS)
        at_ref[rows, :] = lax.dot_general(u_ref[rows, :], x, nt, preferred_element_type=jnp.float32)
    for sb in range(n_sub):
        for c in range(n_chunks):
            lanes = slice(c * LANES, (c + 1) * LANES)
            for k in range(i1_per_sub):
                i1 = sb * i1_per_sub + k
                g = jnp.zeros((PEER_N_KEYS, LANES), jnp.float32)
                for h in range(PEER_HEADS):
                    throw = th_ref[h, i1:i1 + 1, lanes]
                    e0row = e0_ref[h, i1:i1 + 1, lanes]
                    g = g + jnp.where(s1_ref[h, :, lanes] >= throw, e0row * e1_ref[h, :, lanes], 0.0)
                r = slice(i1 * PEER_N_KEYS, (i1 + 1) * PEER_N_KEYS)
                wt_ref[r, lanes] = (_gelu_tanh(at_ref[r, lanes]) * g).astype(jnp.bfloat16)
    acc_ref[...] += jnp.dot(vt_ref[...], wt_ref[...], preferred_element_type=jnp.float32)

    @pl.when(e == n_e - 1)
    def _():
        o_ref[...] = res_ref[...] + gate_ref[0] * acc_ref[...].T


def peer_dense(x_bf16, u_bf16, vt_bf16, th, s1, e0, e1, resid, gate, tokens_per_gate, tt, eb):
    t, d = x_bf16.shape
    n_exp = u_bf16.shape[0]
    assert t % tt == 0 and n_exp % eb == 0 and tokens_per_gate % tt == 0
    assert eb // PEER_N_KEYS == SUBLANES
    tiles_per_gate = tokens_per_gate // tt
    rblk = pl.BlockSpec((PEER_HEADS, PEER_N_KEYS, tt), lambda i, j: (0, 0, i))
    gblk = pl.BlockSpec((PEER_HEADS, SUBLANES, tt), lambda i, j: (0, j, i))
    return pl.pallas_call(
        _dense_kernel,
        grid=(t // tt, n_exp // eb),
        in_specs=[pl.BlockSpec((tt, d), lambda i, j: (i, 0)),
                  pl.BlockSpec((eb, d), lambda i, j: (j, 0)),
                  pl.BlockSpec((d, eb), lambda i, j: (0, j)),
                  gblk, rblk, gblk, rblk,
                  pl.BlockSpec((tt, d), lambda i, j: (i, 0)),
                  pl.BlockSpec((1, 1, d), lambda i, j: (i // tiles_per_gate, 0, 0))],
        out_specs=pl.BlockSpec((tt, d), lambda i, j: (i, 0)),
        out_shape=jax.ShapeDtypeStruct((t, d), jnp.float32),
        scratch_shapes=[pltpu.VMEM((eb, tt), jnp.float32),
                        pltpu.VMEM((eb, tt), jnp.bfloat16),
                        pltpu.VMEM((d, tt), jnp.float32)],
        compiler_params=pltpu.CompilerParams(dimension_semantics=("parallel", "arbitrary"),
                                             vmem_limit_bytes=VMEM_LIMIT_BYTES),
        name="peer_dense",
    )(x_bf16, u_bf16, vt_bf16, th, s1, e0, e1, resid, gate)


def peer_layer(x, norm_g, mods, w_q, keys, u_bf16, vt_bf16):
    t, d = x.shape
    q, h = mm_norm(x, norm_g, mods[3], mods[4], w_q, emit_h=True)
    gate = mods[5]
    th, s1, e0, e1 = peer_router(q, keys.astype(jnp.bfloat16), PEER_TOKEN_TILE)
    return peer_dense(h, u_bf16, vt_bf16, th, s1, e0, e1, x, gate, t // gate.shape[0],
                      PEER_TOKEN_TILE, PEER_EXPERT_TILE)


def kernel(x_prompt, x_sample, c, cache_na_k, cache_na_v, cache_diff_k, cache_diff_v, state_s5_re, state_s5_im, c_ctx, mod_w, mod_b, norm_mix_g, norm_ffn_g, ev_w_in, ev_w_out, na_q_g, na_k_g, na_rpb, hy_conv_w, hy_conv_b, hy_w1, hy_b1, hy_f1, hy_w2, hy_b2, hy_f2, hy_w3, hy_decay, hy_bias, od_w_in, od_w_out, s5_lam_re, s5_lam_im, s5_log_dt, s5_b_re, s5_b_im, s5_c_re, s5_c_im, s5_d, s5_glu_w, s5_glu_b, diff_q_g, diff_k_g, diff_lq1, diff_lk1, diff_lq2, diff_lk2, diff_subln_g, peer_w_q, peer_keys, peer_u, peer_v):
    bp, lp, d = x_prompt.shape
    bs, ls, _ = x_sample.shape
    rope_p, rope_s = rope_tables(lp), rope_tables(ls)
    xp, xs = x_prompt.reshape(bp * lp, d), x_sample.reshape(bs * ls, d)
    new_na_k, new_na_v, new_dk, new_dv, new_sr, new_si = [], [], [], [], [], []
    for i in range(DEPTH):
        mp = adaln(c_ctx[None, :], mod_w[i], mod_b[i])
        ms = adaln(c, mod_w[i], mod_b[i])
        j = i // 2
        if i % 2 == 0:
            ev = (ev_w_in[j], ev_w_out[j], na_q_g[j], na_k_g[j], hy_conv_w[j], hy_conv_b[j],
                  hy_w1[j], hy_b1[j], hy_f1[j], hy_w2[j], hy_b2[j], hy_f2[j], hy_w3[j], hy_decay[j], hy_bias[j])
            bias_tabs = na_bias_tables(na_rpb[j], ls // GRID_W)
            xp, kp, vp = even_mixer(xp, mp, norm_mix_g[i], bp, lp, ev, None, None)
            xs, _, _ = even_mixer(xs, ms, norm_mix_g[i], bs, ls, ev, (cache_na_k[:, j], cache_na_v[:, j]), bias_tabs)
            new_na_k.append(kp)
            new_na_v.append(vp)
        else:
            lam_init = 0.8 - 0.6 * math.exp(-0.3 * i)
            s5_prep = s5_prepare(s5_lam_re[j], s5_lam_im[j], s5_log_dt[j], s5_b_re[j], s5_b_im[j],
                                 s5_c_re[j], s5_c_im[j])
            od = (od_w_in[j], od_w_out[j], s5_prep, s5_d[j], s5_glu_w[j], s5_glu_b[j], diff_q_g[j], diff_k_g[j],
                  diff_lq1[j], diff_lk1[j], diff_lq2[j], diff_lk2[j], diff_subln_g[j])
            xp, kp, vp, sr, si = odd_mixer(xp, mp, norm_mix_g[i], bp, lp, od, lam_init, None, None, None, rope_p)
            xs, _, _, _, _ = odd_mixer(xs, ms, norm_mix_g[i], bs, ls, od, lam_init, state_s5_re[:, j],
                                       state_s5_im[:, j], (cache_diff_k[:, j], cache_diff_v[:, j]), rope_s)
            new_dk.append(kp)
            new_dv.append(vp)
            new_sr.append(sr)
            new_si.append(si)
        u_bf16 = peer_u[i].astype(jnp.bfloat16)
        vt_bf16 = peer_v[i].T.astype(jnp.bfloat16)
        xp = peer_layer(xp, norm_ffn_g[i], mp, peer_w_q[i], peer_keys[i], u_bf16, vt_bf16)
        xs = peer_layer(xs, norm_ffn_g[i], ms, peer_w_q[i], peer_keys[i], u_bf16, vt_bf16)
    return (xp.reshape(bp, lp, d), xs.reshape(bs, ls, d), jnp.stack(new_na_k, axis=1), jnp.stack(new_na_v, axis=1),
            jnp.stack(new_dk, axis=1), jnp.stack(new_dv, axis=1),
            jnp.stack(new_sr, axis=1), jnp.stack(new_si, axis=1))
```

```python
import functools
import math

import jax
import jax.numpy as jnp
from jax import lax
from jax.experimental import pallas as pl
from jax.experimental.pallas import tpu as pltpu

D_MODEL = 2048
DEPTH = 2
GRID_W = 64
MIX_WIDTH = D_MODEL // 2
HEAD_DIM = 128
NA_HEADS = MIX_WIDTH // HEAD_DIM
NA_KH = 8
NA_KW = 16
HY_WIDTH = MIX_WIDTH
HY_ORDER = 2
HY_POS_EMB = 33
S5_WIDTH = MIX_WIDTH
S5_GROUP = 16
S5_GROUPS = S5_WIDTH // S5_GROUP
S5_STATE = 64
DIFF_HEADS = MIX_WIDTH // HEAD_DIM
DIFF_QK_DIM = HEAD_DIM // 2
DIFF_V_DIM = HEAD_DIM
ROPE_BASE = 10000.0
PEER_HEADS = 8
PEER_N_KEYS = 128
PEER_QUERY_DIM = 256
PEER_TOPK = 16
NORM_EPS = 1e-6
NEG_INF = -1e30

LANES = 128
SUBLANES = 8
VMEM_LIMIT_BYTES = 56 * 1024 * 1024


MM_ROW_TILE = 1024
MM_COL_TILE = 512


def _mm_norm_kernel(x_ref, g_ref, sc_ref, sh_ref, w_ref, *out_refs, emit_h):
    if emit_h:
        o_ref, h_ref, hs_ref = out_refs
    else:
        o_ref, hs_ref = out_refs

    @pl.when(pl.program_id(1) == 0)
    def _():
        x = x_ref[...]
        y = x * lax.rsqrt(jnp.mean(x * x, axis=-1, keepdims=True) + NORM_EPS) * g_ref[...]
        hs_ref[...] = (y * (1.0 + sc_ref[0]) + sh_ref[0]).astype(jnp.bfloat16)
        if emit_h:
            h_ref[...] = hs_ref[...]

    o_ref[...] = jnp.dot(hs_ref[...], w_ref[...].astype(jnp.bfloat16), preferred_element_type=jnp.float32)


def mm_norm(x, norm_g, shift, scale, w, emit_h=False):
    m, k = x.shape
    n = w.shape[1]
    tm, tn = min(MM_ROW_TILE, m), min(MM_COL_TILE, n)
    nb = scale.shape[0]
    assert m % tm == 0 and n % tn == 0 and (m // nb) % tm == 0
    tiles_per_mod = (m // nb) // tm
    mod = pl.BlockSpec((1, 1, k), lambda i, j: (i // tiles_per_mod, 0, 0))
    out_specs = [pl.BlockSpec((tm, tn), lambda i, j: (i, j))]
    out_shape = [jax.ShapeDtypeStruct((m, n), jnp.float32)]
    if emit_h:
        out_specs.append(pl.BlockSpec((tm, k), lambda i, j: (i, 0)))
        out_shape.append(jax.ShapeDtypeStruct((m, k), jnp.bfloat16))
    res = pl.pallas_call(
        functools.partial(_mm_norm_kernel, emit_h=emit_h),
        grid=(m // tm, n // tn),
        in_specs=[pl.BlockSpec((tm, k), lambda i, j: (i, 0)),
                  pl.BlockSpec((1, k), lambda i, j: (0, 0)),
                  mod, mod,
                  pl.BlockSpec((k, tn), lambda i, j: (0, j))],
        out_specs=out_specs,
        out_shape=out_shape,
        scratch_shapes=[pltpu.VMEM((tm, k), jnp.bfloat16)],
        compiler_params=pltpu.CompilerParams(dimension_semantics=("parallel", "arbitrary"),
                                             vmem_limit_bytes=VMEM_LIMIT_BYTES),
        name="mm_norm",
    )(x, norm_g.astype(jnp.float32).reshape(1, k), scale, shift, w)
    return res if emit_h else res[0]


def _mm_pair_res_kernel(a_ref, b_ref, wa_ref, wb_ref, r_ref, g_ref, o_ref):
    y = (jnp.dot(a_ref[...].astype(jnp.bfloat16), wa_ref[...].astype(jnp.bfloat16), preferred_element_type=jnp.float32)
         + jnp.dot(b_ref[...].astype(jnp.bfloat16), wb_ref[...].astype(jnp.bfloat16), preferred_element_type=jnp.float32))
    o_ref[...] = r_ref[...] + g_ref[0] * y


def mm_pair_res(a, b, w, resid, gate):
    m, kh = a.shape
    n = w.shape[1]
    tm, tn = min(MM_ROW_TILE, m), min(MM_COL_TILE, n)
    nb = gate.shape[0]
    assert m % tm == 0 and n % tn == 0 and (m // nb) % tm == 0 and w.shape[0] == 2 * kh
    tiles_per_mod = (m // nb) // tm
    return pl.pallas_call(
        _mm_pair_res_kernel,
        grid=(m // tm, n // tn),
        in_specs=[pl.BlockSpec((tm, kh), lambda i, j: (i, 0)),
                  pl.BlockSpec((tm, kh), lambda i, j: (i, 0)),
                  pl.BlockSpec((kh, tn), lambda i, j: (0, j)),
                  pl.BlockSpec((kh, tn), lambda i, j: (1, j)),
                  pl.BlockSpec((tm, tn), lambda i, j: (i, j)),
                  pl.BlockSpec((1, 1, tn), lambda i, j: (i // tiles_per_mod, 0, j))],
        out_specs=pl.BlockSpec((tm, tn), lambda i, j: (i, j)),
        out_shape=jax.ShapeDtypeStruct((m, n), jnp.float32),
        compiler_params=pltpu.CompilerParams(dimension_semantics=("parallel", "arbitrary"),
                                             vmem_limit_bytes=VMEM_LIMIT_BYTES),
        name="mm_pair_res",
    )(a, b, w, w, resid, gate)


def adaln(cond, w, b):
    m = jax.nn.silu(cond) @ w + b
    return jnp.split(m[:, None, :], 6, axis=-1)


def hyena_filters(l, w1, b1, f1, w2, b2, f2, w3, decay):
    t = jnp.linspace(0.0, 1.0, l, dtype=jnp.float32)[:, None]
    bands = (HY_POS_EMB - 1) // 2
    w_ang = 2.0 * math.pi * jnp.arange(l, dtype=jnp.float32)[:, None] / l
    freqs = jnp.linspace(1e-4, bands - 1, bands, dtype=jnp.float32)[None, :]
    z = jnp.concatenate([t, jnp.cos(freqs * w_ang), -jnp.sin(freqs * w_ang)], axis=-1)
    h = jnp.sin(f1 * (z @ w1 + b1))
    h = jnp.sin(f2 * (h @ w2 + b2))
    h = (h @ w3).reshape(l, 2, HY_ORDER, HY_WIDTH).astype(jnp.float32)
    h = h * jnp.exp(-t.reshape(l, 1, 1, 1) * jnp.abs(decay.astype(jnp.float32)))
    h_f, h_b = h[:, 0], h[:, 1]
    zero = jnp.zeros((1, HY_ORDER, HY_WIDTH), jnp.float32)
    return jnp.concatenate([h_f, zero, h_b[1:][::-1]], axis=0)


def _split_bf16(x):
    hi = x.astype(jnp.bfloat16)
    lo = (x - hi.astype(jnp.float32)).astype(jnp.bfloat16)
    return hi, lo


def _dot3(a_hi, a_lo, x):
    m = a_hi.shape[0]
    xh, xl = _split_bf16(x)
    r = jnp.dot(jnp.concatenate([a_hi, a_lo], axis=0), xh, preferred_element_type=jnp.float32)
    return r[:m] + r[m:] + jnp.dot(a_hi, xl, preferred_element_type=jnp.float32)


HY_STEP_ROWS = 256


def hyena_factors(l):
    n = 2 * l
    n2 = 64 if n >= 8192 else 16
    return n // n2, n2


def hyena_tables(l):
    n = 2 * l
    n1, n2 = hyena_factors(l)
    t = (n2 * jnp.arange(n1 // 2)[None, None, :] + jnp.arange(n2)[:, None, None])
    k1 = jnp.arange(n1)[None, :, None]
    ang = (2.0 * math.pi / n) * ((t * k1) % n).astype(jnp.float32)
    fa = jnp.concatenate([jnp.cos(ang), -jnp.sin(ang)], axis=1)
    fc = jnp.swapaxes(fa, 1, 2) / n
    a2 = (2.0 * math.pi / n2) * ((jnp.arange(n2)[:, None] * jnp.arange(n2)[None, :]) % n2).astype(jnp.float32)
    c, s = jnp.cos(a2), jnp.sin(a2)
    fb = jnp.concatenate([jnp.concatenate([c, s], axis=1), jnp.concatenate([-s, c], axis=1)], axis=0)
    fbi = jnp.concatenate([jnp.concatenate([c, -s], axis=1), jnp.concatenate([s, c], axis=1)], axis=0)
    return tuple(_split_bf16(x) for x in (fa, fc, fb, fbi))


def hyena_spectrum(filt, l):
    n1, n2 = hyena_factors(l)
    kf = jnp.fft.fft(filt, axis=0)
    kf = jnp.stack([jnp.real(kf), jnp.imag(kf)], axis=0).astype(jnp.float32)
    kf = kf.reshape(2, n2, n1, HY_ORDER, HY_WIDTH)
    return jnp.transpose(kf, (3, 0, 2, 1, 4))


def _hy_stage_a_kernel(z_ref, fh_ref, fl_ref, o_ref):
    for i in range(z_ref.shape[0]):
        o_ref[i] = _dot3(fh_ref[0], fl_ref[0], z_ref[i])


def _hy_stage_b_kernel(s_ref, k_ref, fh_ref, fl_ref, gh_ref, gl_ref, o_ref):
    n2 = s_ref.shape[3]
    kr, ki = k_ref[0, 0], k_ref[1, 0]
    for i in range(s_ref.shape[0]):
        y = jnp.concatenate([s_ref[i, 0, 0], s_ref[i, 1, 0]], axis=0)
        z = _dot3(fh_ref[...], fl_ref[...], y)
        zr, zi = z[:n2], z[n2:]
        p = jnp.concatenate([zr * kr - zi * ki, zr * ki + zi * kr], axis=0)
        q = _dot3(gh_ref[...], gl_ref[...], p)
        o_ref[i, 0, 0] = q[:n2]
        o_ref[i, 1, 0] = q[n2:]


def _hy_stage_c_kernel(q_ref, fh_ref, fl_ref, z_ref, g_ref, b_ref, o_ref):
    for i in range(q_ref.shape[0]):
        y = _dot3(fh_ref[0], fl_ref[0], q_ref[i])
        o_ref[i] = g_ref[i] * (y + z_ref[i] * b_ref[...])


def hyena_long_conv(z, z_col, gate, gate_col, bias_o, spec_o, tables, l):
    (fah, fal), (fch, fcl), (fbh, fbl), (fgh, fgl) = tables
    b = z.shape[0]
    c = HY_WIDTH
    n1, n2 = hyena_factors(l)
    h1 = n1 // 2
    bb = max(1, min(b, HY_STEP_ROWS // n1))
    assert b % bb == 0
    cp = pltpu.CompilerParams(dimension_semantics=("parallel", "parallel"), vmem_limit_bytes=VMEM_LIMIT_BYTES)
    zblocks = z.shape[-1] // c
    gblocks = gate.shape[-1] // c
    zv = z.reshape(b, h1, n2 * z.shape[-1])
    gv = gate.reshape(b, h1, n2 * gate.shape[-1])
    s1 = pl.pallas_call(
        _hy_stage_a_kernel,
        grid=(b // bb, n2),
        in_specs=[pl.BlockSpec((bb, h1, c), lambda i, j: (i, 0, j * zblocks + z_col)),
                  pl.BlockSpec((1, 2 * n1, h1), lambda i, j: (j, 0, 0)),
                  pl.BlockSpec((1, 2 * n1, h1), lambda i, j: (j, 0, 0))],
        out_specs=pl.BlockSpec((bb, 2 * n1, c), lambda i, j: (i, 0, j)),
        out_shape=jax.ShapeDtypeStruct((b, 2 * n1, n2 * c), jnp.float32),
        compiler_params=cp, name="hyena_stage_a",
    )(zv, fah, fal)
    s1 = s1.reshape(b, 2, n1, n2, c)
    mat = pl.BlockSpec((2 * n2, 2 * n2), lambda k, i: (0, 0))
    q = pl.pallas_call(
        _hy_stage_b_kernel,
        grid=(n1, b // bb),
        in_specs=[pl.BlockSpec((bb, 2, 1, n2, c), lambda k, i: (i, 0, k, 0, 0)),
                  pl.BlockSpec((2, 1, n2, c), lambda k, i: (0, k, 0, 0)),
                  mat, mat, mat, mat],
        out_specs=pl.BlockSpec((bb, 2, 1, n2, c), lambda k, i: (i, 0, k, 0, 0)),
        out_shape=jax.ShapeDtypeStruct((b, 2, n1, n2, c), jnp.float32),
        compiler_params=cp, name="hyena_stage_b",
    )(s1, spec_o, fbh, fbl, fgh, fgl)
    q = q.reshape(b, 2 * n1, n2 * c)
    out = pl.pallas_call(
        _hy_stage_c_kernel,
        grid=(b // bb, n2),
        in_specs=[pl.BlockSpec((bb, 2 * n1, c), lambda i, j: (i, 0, j)),
                  pl.BlockSpec((1, h1, 2 * n1), lambda i, j: (j, 0, 0)),
                  pl.BlockSpec((1, h1, 2 * n1), lambda i, j: (j, 0, 0)),
                  pl.BlockSpec((bb, h1, c), lambda i, j: (i, 0, j * zblocks + z_col)),
                  pl.BlockSpec((bb, h1, c), lambda i, j: (i, 0, j * gblocks + gate_col)),
                  pl.BlockSpec((1, c), lambda i, j: (0, 0))],
        out_specs=pl.BlockSpec((bb, h1, c), lambda i, j: (i, 0, j)),
        out_shape=jax.ShapeDtypeStruct((b, h1, n2 * c), jnp.float32),
        compiler_params=cp, name="hyena_stage_c",
    )(q, fch, fcl, zv, gv, bias_o.astype(jnp.float32).reshape(1, c))
    return out.reshape(b, l, c)


def hyena(u, conv_w, conv_b, filt, bias):
    b, l, _ = u.shape
    up = jnp.pad(u, ((0, 0), (1, 1), (0, 0)))
    u = up[:, :-2] * conv_w[0] + up[:, 1:-1] * conv_w[1] + up[:, 2:] * conv_w[2] + conv_b
    tables = hyena_tables(l)
    spec = hyena_spectrum(hyena_filters(l, *filt), l)
    z, z_col = u, 0
    for o in range(HY_ORDER):
        z = hyena_long_conv(z, z_col, u, 1 + o, bias[o], spec[o], tables, l)
        z_col = 0
    return z


S5_TILE_GROUPS = 8
S5_TILES = S5_GROUPS // S5_TILE_GROUPS
S5_TILE_IN = S5_TILE_GROUPS * S5_GROUP
S5_TILE_STATE = S5_TILE_GROUPS * S5_STATE
S5_ROWS_PER_STEP = 2048


def s5_prepare(lam_re, lam_im, log_dt, b_re, b_im, c_re, c_im):
    f32 = jnp.float32
    lam_re, lam_im = lam_re.astype(f32), lam_im.astype(f32)
    dt = jnp.exp(log_dt.astype(f32))[..., None]
    mag = jnp.exp(lam_re * dt)
    ar, ai = mag * jnp.cos(lam_im * dt), mag * jnp.sin(lam_im * dt)
    den = lam_re * lam_re + lam_im * lam_im
    cr = ((ar - 1.0) * lam_re + ai * lam_im) / den
    ci = (ai * lam_re - (ar - 1.0) * lam_im) / den
    bbr = cr[..., None] * b_re - ci[..., None] * b_im
    bbi = cr[..., None] * b_im + ci[..., None] * b_re
    eye = jnp.eye(S5_TILE_GROUPS, dtype=f32)

    def tile_in(bb):
        x = bb.reshape(2, S5_TILES, S5_TILE_GROUPS, S5_STATE, S5_GROUP)
        x = jnp.einsum('dtgnp,gh->dtgphn', x, eye)
        return x.reshape(2, S5_TILES, S5_TILE_IN, S5_TILE_STATE)

    def tile_out(cc):
        x = cc.astype(f32).reshape(2, S5_TILES, S5_TILE_GROUPS, S5_GROUP, S5_STATE)
        x = jnp.einsum('dtgpn,gh->dtgnhp', x, eye)
        return x.reshape(2, S5_TILES, S5_TILE_STATE, S5_TILE_IN)

    win = jnp.concatenate([tile_in(bbr), tile_in(bbi)], axis=-1).astype(jnp.bfloat16)
    wout = jnp.concatenate([tile_out(c_re), -tile_out(c_im)], axis=-2).astype(jnp.bfloat16)
    a = jnp.stack([ar.reshape(2, S5_TILES, S5_TILE_STATE), ai.reshape(2, S5_TILES, S5_TILE_STATE)], axis=2)
    return win, wout, a


def _s5_scan_kernel(u_ref, win_ref, wout_ref, a_ref, h0_ref, y_ref, fin_ref, bu_ref, st_ref, *, batch):
    d = pl.program_id(0)
    c = pl.program_id(2)
    n_c = pl.num_programs(2)
    ns = S5_TILE_STATE
    steps = u_ref.shape[0] // batch

    @pl.when(c == 0)
    def _():
        st_ref[...] = h0_ref[0, 0]

    bu_ref[...] = jnp.dot(u_ref[...].astype(jnp.bfloat16), win_ref[0, 0], preferred_element_type=jnp.float32)
    ar = jnp.broadcast_to(a_ref[0, 0, 0:1, :], (batch, ns))
    ai = jnp.broadcast_to(a_ref[0, 0, 1:2, :], (batch, ns))

    def step(t, carry):
        hr, hi = carry
        te = jnp.where(d == 0, t, steps - 1 - t)
        r0 = pl.multiple_of(te * batch, batch)
        nr = ar * hr - ai * hi + bu_ref[pl.ds(r0, batch), 0:ns]
        ni = ar * hi + ai * hr + bu_ref[pl.ds(r0, batch), ns:2 * ns]
        bu_ref[pl.ds(r0, batch), 0:ns] = nr
        bu_ref[pl.ds(r0, batch), ns:2 * ns] = ni
        return nr, ni

    hr, hi = lax.fori_loop(0, steps, step, (st_ref[:, 0:ns], st_ref[:, ns:2 * ns]), unroll=4)
    st_ref[:, 0:ns] = hr
    st_ref[:, ns:2 * ns] = hi
    y_ref[0] = jnp.dot(bu_ref[...].astype(jnp.bfloat16), wout_ref[0, 0], preferred_element_type=jnp.float32)

    @pl.when(c == n_c - 1)
    def _():
        fin_ref[0, 0] = st_ref[...]


def s5_scan(u_tm, win, wout, a, h0, batch):
    rows = u_tm.shape[0]
    r = min(S5_ROWS_PER_STEP, rows)
    assert rows % r == 0 and r % batch == 0
    n_c = rows // r

    def chunk(d, c):
        return c + d * (n_c - 1 - 2 * c)

    return pl.pallas_call(
        functools.partial(_s5_scan_kernel, batch=batch),
        grid=(2, S5_TILES, n_c),
        in_specs=[pl.BlockSpec((r, S5_TILE_IN), lambda d, j, c: (chunk(d, c), j)),
                  pl.BlockSpec((1, 1, S5_TILE_IN, 2 * S5_TILE_STATE), lambda d, j, c: (d, j, 0, 0)),
                  pl.BlockSpec((1, 1, 2 * S5_TILE_STATE, S5_TILE_IN), lambda d, j, c: (d, j, 0, 0)),
                  pl.BlockSpec((1, 1, 2, S5_TILE_STATE), lambda d, j, c: (d, j, 0, 0)),
                  pl.BlockSpec((1, 1, batch, 2 * S5_TILE_STATE), lambda d, j, c: (d, j, 0, 0))],
        out_specs=[pl.BlockSpec((1, r, S5_TILE_IN), lambda d, j, c: (d, chunk(d, c), j)),
                   pl.BlockSpec((1, 1, batch, 2 * S5_TILE_STATE), lambda d, j, c: (d, j, 0, 0))],
        out_shape=[jax.ShapeDtypeStruct((2, rows, S5_WIDTH), jnp.float32),
                   jax.ShapeDtypeStruct((2, S5_TILES, batch, 2 * S5_TILE_STATE), jnp.float32)],
        scratch_shapes=[pltpu.VMEM((r, 2 * S5_TILE_STATE), jnp.float32),
                        pltpu.VMEM((batch, 2 * S5_TILE_STATE), jnp.float32)],
        compiler_params=pltpu.CompilerParams(dimension_semantics=("parallel", "parallel", "arbitrary"),
                                             vmem_limit_bytes=VMEM_LIMIT_BYTES),
        name="s5_scan",
    )(u_tm, win, wout, a, h0)


def _gelu_tanh(x):
    return 0.5 * x * (1.0 + jnp.tanh(math.sqrt(2.0 / math.pi) * (x + 0.044715 * (x * x * x))))


def _s5_glu_kernel(u_ref, y_ref, d_ref, w_ref, b_ref, o_ref):
    y = d_ref[...] * u_ref[...] + y_ref[0] + y_ref[1]
    y = _gelu_tanh(y)
    z = jnp.dot(y.astype(jnp.bfloat16), w_ref[...], preferred_element_type=jnp.float32) + b_ref[...]
    o_ref[...] = y * (1.0 / (1.0 + jnp.exp(-z)))


def s5_glu(u_tm, y, d_skip, glu_w_bf16, glu_b, tr=1024):
    rows, w = u_tm.shape
    tr = min(tr, rows)
    assert rows % tr == 0
    return pl.pallas_call(
        _s5_glu_kernel,
        grid=(rows // tr,),
        in_specs=[pl.BlockSpec((tr, w), lambda i: (i, 0)),
                  pl.BlockSpec((2, tr, w), lambda i: (0, i, 0)),
                  pl.BlockSpec((1, w), lambda i: (0, 0)),
                  pl.BlockSpec((w, w), lambda i: (0, 0)),
                  pl.BlockSpec((1, w), lambda i: (0, 0))],
        out_specs=pl.BlockSpec((tr, w), lambda i: (i, 0)),
        out_shape=jax.ShapeDtypeStruct((rows, w), jnp.float32),
        compiler_params=pltpu.CompilerParams(dimension_semantics=("parallel",),
                                             vmem_limit_bytes=VMEM_LIMIT_BYTES),
        name="s5_glu",
    )(u_tm, y, d_skip.reshape(1, w), glu_w_bf16, glu_b.reshape(1, w))


def s5_mixer(u, h0_re, h0_im, prep, d_skip, glu_w, glu_b):
    win, wout, a = prep
    b, l, w = u.shape
    u_tm = jnp.swapaxes(u, 0, 1).reshape(l * b, w)
    if h0_re is None:
        h0 = jnp.zeros((2, S5_TILES, b, 2 * S5_TILE_STATE), jnp.float32)
    else:
        def tiles(h):
            return jnp.transpose(h.astype(jnp.float32).reshape(b, 2, S5_TILES, S5_TILE_STATE), (1, 2, 0, 3))
        h0 = jnp.concatenate([tiles(h0_re), tiles(h0_im)], axis=-1)
    y, fin = s5_scan(u_tm, win, wout, a, h0, b)
    out_tm = s5_glu(u_tm, y, d_skip, glu_w.astype(jnp.bfloat16), glu_b)
    out = jnp.swapaxes(out_tm.reshape(l, b, w), 0, 1)

    def untile(f):
        return jnp.transpose(f, (2, 0, 1, 3)).reshape(b, 2, S5_GROUPS, S5_STATE)
    return out, untile(fin[..., :S5_TILE_STATE]), untile(fin[..., S5_TILE_STATE:])


def _half_mean_matrix():
    i = lax.broadcasted_iota(jnp.int32, (LANES, LANES), 0) // DIFF_QK_DIM
    j = lax.broadcasted_iota(jnp.int32, (LANES, LANES), 1) // DIFF_QK_DIM
    return jnp.where(i == j, 1.0 / DIFF_QK_DIM, 0.0).astype(jnp.bfloat16)


def _rms_groups(x, gain, avg):
    xx = x * x
    hi = xx.astype(jnp.bfloat16)
    lo = (xx - hi.astype(jnp.float32)).astype(jnp.bfloat16)
    ms = (jnp.dot(hi, avg, preferred_element_type=jnp.float32)
          + jnp.dot(lo, avg, preferred_element_type=jnp.float32))
    return x * lax.rsqrt(ms + NORM_EPS) * gain


def _rope_lanes(y, cos, sin_signed, first_half):
    rot = jnp.where(first_half, pltpu.roll(y, LANES - 16, 1), pltpu.roll(y, 16, 1))
    return y * cos + rot * sin_signed


def _diff_prep_kernel(q_ref, k_ref, qg_ref, kg_ref, cos_ref, sin_ref, qo_ref, ko_ref, kn_ref, *, use_rope):
    avg = _half_mean_matrix()
    lane = lax.broadcasted_iota(jnp.int32, (1, LANES), 1)
    first_half = (lane % 32) < 16
    scale = DIFF_QK_DIM ** -0.5
    for h in range(DIFF_HEADS):
        cols = slice(h * LANES, (h + 1) * LANES)
        qn = _rms_groups(q_ref[:, cols], qg_ref[...], avg)
        kn = _rms_groups(k_ref[:, cols], kg_ref[...], avg)
        kn_ref[:, cols] = kn
        if use_rope:
            qn = _rope_lanes(qn, cos_ref[...], sin_ref[...], first_half)
            kn = _rope_lanes(kn, cos_ref[...], sin_ref[...], first_half)
        qo_ref[:, cols] = (qn * scale).astype(jnp.bfloat16)
        ko_ref[:, cols] = kn.astype(jnp.bfloat16)


def diff_prep(p, q_g, k_g, cos, sin_signed, seq_len, use_rope, tr=512):
    rows = p.shape[0]
    tr = min(tr, seq_len)
    assert seq_len % tr == 0
    w = DIFF_HEADS * LANES
    per_seq = seq_len // tr
    g2 = lambda g: jnp.tile(g.astype(jnp.float32), 2).reshape(1, LANES)
    return pl.pallas_call(
        functools.partial(_diff_prep_kernel, use_rope=use_rope),
        grid=(rows // tr,),
        in_specs=[pl.BlockSpec((tr, w), lambda i: (i, 1)),
                  pl.BlockSpec((tr, w), lambda i: (i, 2)),
                  pl.BlockSpec((1, LANES), lambda i: (0, 0)),
                  pl.BlockSpec((1, LANES), lambda i: (0, 0)),
                  pl.BlockSpec((tr, LANES), lambda i: (i % per_seq, 0)),
                  pl.BlockSpec((tr, LANES), lambda i: (i % per_seq, 0))],
        out_specs=[pl.BlockSpec((tr, w), lambda i: (i, 0))] * 3,
        out_shape=[jax.ShapeDtypeStruct((rows, w), jnp.bfloat16), jax.ShapeDtypeStruct((rows, w), jnp.bfloat16),
                   jax.ShapeDtypeStruct((rows, w), jnp.float32)],
        compiler_params=pltpu.CompilerParams(dimension_semantics=("parallel",),
                                             vmem_limit_bytes=VMEM_LIMIT_BYTES),
        name="diff_prep",
    )(p, p, g2(q_g), g2(k_g), cos, sin_signed)


def rope_tables(l):
    t = jnp.arange(l)
    row = (t // GRID_W).astype(jnp.float32)
    col = (t % GRID_W).astype(jnp.float32)
    nf = DIFF_QK_DIM // 4
    inv = ROPE_BASE ** (-jnp.arange(nf, dtype=jnp.float32) / nf)
    ang = jnp.stack([row[:, None] * inv, col[:, None] * inv], axis=1)
    ang = jnp.stack([ang, ang], axis=2).reshape(l, DIFF_QK_DIM)
    sign = jnp.where((jnp.arange(DIFF_QK_DIM) % 32) < 16, -1.0, 1.0)
    return jnp.tile(jnp.cos(ang), (1, 2)), jnp.tile(jnp.sin(ang) * sign, (1, 2))


def _diff_attn_kernel(*refs, has_ctx):
    if has_ctx:
        q_ref, k_ref, v_ref, ck_ref, cv_ref, lam_ref, g_ref, o_ref = refs
    else:
        q_ref, k_ref, v_ref, lam_ref, g_ref, o_ref = refs
    q = q_ref[...]
    lane = lax.broadcasted_iota(jnp.int32, (1, LANES), 1)
    zero = jnp.zeros_like(q)
    qs = (jnp.where(lane < DIFF_QK_DIM, q, zero), jnp.where(lane >= DIFF_QK_DIM, q, zero))
    nt = (((1,), (1,)), ((), ()))
    k = k_ref[...]
    ck = ck_ref[...].astype(jnp.bfloat16) if has_ctx else None
    lam = lam_ref[0:1, 0:1]
    w_self, w_ctx = None, None
    for i in range(2):
        s = lax.dot_general(qs[i], k, nt, preferred_element_type=jnp.float32)
        m = jnp.max(s, axis=-1, keepdims=True)
        if has_ctx:
            sc = lax.dot_general(qs[i], ck, nt, preferred_element_type=jnp.float32)
            m = jnp.maximum(m, jnp.max(sc, axis=-1, keepdims=True))
        p = jnp.exp(s - m)
        l = jnp.sum(p, axis=-1, keepdims=True)
        if has_ctx:
            pc = jnp.exp(sc - m)
            l = l + jnp.sum(pc, axis=-1, keepdims=True)
        coef = 1.0 / l if i == 0 else -lam / l
        w_self = p * coef if i == 0 else w_self + p * coef
        if has_ctx:
            w_ctx = pc * coef if i == 0 else w_ctx + pc * coef
    o = jnp.dot(w_self.astype(jnp.bfloat16), v_ref[...].astype(jnp.bfloat16), preferred_element_type=jnp.float32)
    if has_ctx:
        o = o + jnp.dot(w_ctx.astype(jnp.bfloat16), cv_ref[...].astype(jnp.bfloat16),
                        preferred_element_type=jnp.float32)
    ms = jnp.mean(o * o, axis=-1, keepdims=True)
    o_ref[...] = o * lax.rsqrt(ms + NORM_EPS) * g_ref[...]


def diff_attention(q_bf16, k_bf16, p, ctx_k, ctx_v, lam, gain, batch, seq_len, tq=256):
    tq = min(tq, seq_len)
    nq = seq_len // tq
    has_ctx = ctx_k is not None
    v_col0 = 3 * DIFF_HEADS
    in_specs = [pl.BlockSpec((tq, LANES), lambda b, h, i: (b * nq + i, h)),
                pl.BlockSpec((seq_len, LANES), lambda b, h, i: (b, h)),
                pl.BlockSpec((seq_len, LANES), lambda b, h, i: (b, v_col0 + h))]
    args = [q_bf16, k_bf16, p]
    if has_ctx:
        lc = ctx_k.shape[0] // batch
        in_specs += [pl.BlockSpec((lc, LANES), lambda b, h, i: (b, h))] * 2
        args += [ctx_k, ctx_v]
    in_specs += [pl.BlockSpec((1, LANES), lambda b, h, i: (0, 0))] * 2
    args += [jnp.broadcast_to(lam.astype(jnp.float32), (1, LANES)), gain.astype(jnp.float32).reshape(1, LANES)]
    return pl.pallas_call(
        functools.partial(_diff_attn_kernel, has_ctx=has_ctx),
        grid=(batch, DIFF_HEADS, nq),
        in_specs=in_specs,
        out_specs=pl.BlockSpec((tq, LANES), lambda b, h, i: (b * nq + i, h)),
        out_shape=jax.ShapeDtypeStruct((batch * seq_len, DIFF_HEADS * LANES), jnp.float32),
        compiler_params=pltpu.CompilerParams(dimension_semantics=("parallel", "parallel", "arbitrary"),
                                             vmem_limit_bytes=VMEM_LIMIT_BYTES),
        name="diff_attention",
    )(*args)


NA_Q_ROWS = 8
NA_WIN_ROWS = 16


def _na_prep_kernel(q_ref, k_ref, qg_ref, kg_ref, qo_ref, ko_ref, kn_ref):
    avg = jnp.full((LANES, LANES), 1.0 / HEAD_DIM, jnp.bfloat16)
    for h in range(NA_HEADS):
        cols = slice(h * LANES, (h + 1) * LANES)
        qo_ref[:, cols] = _rms_groups(q_ref[:, cols], qg_ref[...], avg).astype(jnp.bfloat16)
        kn = _rms_groups(k_ref[:, cols], kg_ref[...], avg)
        kn_ref[:, cols] = kn
        ko_ref[:, cols] = kn.astype(jnp.bfloat16)


def na_prep(p, q_g, k_g, tr=512):
    rows = p.shape[0]
    tr = min(tr, rows)
    w = NA_HEADS * LANES
    g1 = lambda g: g.astype(jnp.float32).reshape(1, LANES)
    return pl.pallas_call(
        _na_prep_kernel,
        grid=(rows // tr,),
        in_specs=[pl.BlockSpec((tr, w), lambda i: (i, 0)),
                  pl.BlockSpec((tr, w), lambda i: (i, 1)),
                  pl.BlockSpec((1, LANES), lambda i: (0, 0)),
                  pl.BlockSpec((1, LANES), lambda i: (0, 0))],
        out_specs=[pl.BlockSpec((tr, w), lambda i: (i, 0))] * 3,
        out_shape=[jax.ShapeDtypeStruct((rows, w), jnp.bfloat16), jax.ShapeDtypeStruct((rows, w), jnp.bfloat16),
                   jax.ShapeDtypeStruct((rows, w), jnp.float32)],
        compiler_params=pltpu.CompilerParams(dimension_semantics=("parallel",),
                                             vmem_limit_bytes=VMEM_LIMIT_BYTES),
        name="na_prep",
    )(p, p, g1(q_g), g1(k_g))


def na_bias_tables(rpb, rows):
    nblk = rows // NA_Q_ROWS
    cq = jnp.arange(GRID_W)[:, None]
    ck = jnp.arange(GRID_W)[None, :]
    col_start = jnp.clip(cq - NA_KW // 2, 0, GRID_W - NA_KW)
    col_ok = (ck >= col_start) & (ck < col_start + NA_KW)
    cidx = jnp.clip(ck - cq + NA_KW - 1, 0, 2 * NA_KW - 2)
    tabs = []
    for i in (0, 1, nblk - 1):
        base = min(max(NA_Q_ROWS * i - NA_KH // 2, 0), rows - NA_WIN_ROWS)
        r = NA_Q_ROWS * i + jnp.arange(NA_Q_ROWS)[:, None]
        rk = base + jnp.arange(NA_WIN_ROWS)[None, :]
        r0 = jnp.clip(r - NA_KH // 2, 0, rows - NA_KH)
        row_ok = (rk >= r0) & (rk < r0 + NA_KH)
        ridx = jnp.clip(rk - r + NA_KH - 1, 0, 2 * NA_KH - 2)
        b = rpb[:, ridx][:, :, :, cidx]
        ok = row_ok[:, :, None, None] & col_ok[None, None, :, :]
        b = jnp.where(ok[None], b.astype(jnp.float32), NEG_INF)
        tabs.append(jnp.transpose(b, (0, 1, 3, 2, 4)).reshape(NA_HEADS, NA_Q_ROWS * GRID_W, NA_WIN_ROWS * GRID_W))
    return jnp.stack(tabs, axis=0)


def _na_attn_kernel(*refs, windowed, rows):
    nt = (((1,), (1,)), ((), ()))
    scale = HEAD_DIM ** -0.5
    if windowed:
        q_ref, k_ref, v_ref, b_ref, ck_ref, cv_ref, o_ref = refs
        i = pl.program_id(2)
        base = jnp.clip(NA_Q_ROWS * i - NA_KH // 2, 0, rows - NA_WIN_ROWS)
        k0 = pl.multiple_of(base * GRID_W, GRID_W)
        nk = NA_WIN_ROWS * GRID_W
        q = q_ref[...]
        s = lax.dot_general(q, k_ref[pl.ds(k0, nk), :], nt, preferred_element_type=jnp.float32) * scale + b_ref[0, 0]
        sc = lax.dot_general(q, ck_ref[...].astype(jnp.bfloat16), nt, preferred_element_type=jnp.float32) * scale
        m = jnp.maximum(jnp.max(s, axis=-1, keepdims=True), jnp.max(sc, axis=-1, keepdims=True))
        p = jnp.exp(s - m)
        pc = jnp.exp(sc - m)
        l = jnp.sum(p, axis=-1, keepdims=True) + jnp.sum(pc, axis=-1, keepdims=True)
        o = (jnp.dot(p.astype(jnp.bfloat16), v_ref[pl.ds(k0, nk), :].astype(jnp.bfloat16),
                     preferred_element_type=jnp.float32)
             + jnp.dot(pc.astype(jnp.bfloat16), cv_ref[...].astype(jnp.bfloat16), preferred_element_type=jnp.float32))
    else:
        q_ref, k_ref, v_ref, o_ref = refs
        s = lax.dot_general(q_ref[...], k_ref[...], nt, preferred_element_type=jnp.float32) * scale
        m = jnp.max(s, axis=-1, keepdims=True)
        p = jnp.exp(s - m)
        l = jnp.sum(p, axis=-1, keepdims=True)
        o = jnp.dot(p.astype(jnp.bfloat16), v_ref[...].astype(jnp.bfloat16), preferred_element_type=jnp.float32)
    o_ref[...] = o / l


def na_attention(q_bf16, k_bf16, p, bias_tabs, ctx_k, ctx_v, batch, seq_len):
    windowed = bias_tabs is not None
    v_col0 = 2 * NA_HEADS
    rows = seq_len // GRID_W
    if windowed:
        tq = NA_Q_ROWS * GRID_W
        nq = seq_len // tq
        lc = ctx_k.shape[0] // batch
        nk = NA_WIN_ROWS * GRID_W
        in_specs = [pl.BlockSpec((tq, LANES), lambda b, h, i: (b * nq + i, h)),
                    pl.BlockSpec((seq_len, LANES), lambda b, h, i: (b, h)),
                    pl.BlockSpec((seq_len, LANES), lambda b, h, i: (b, v_col0 + h)),
                    pl.BlockSpec((1, 1, tq, nk),
                                 lambda b, h, i: (jnp.where(i == 0, 0, jnp.where(i == nq - 1, 2, 1)), h, 0, 0)),
                    pl.BlockSpec((lc, LANES), lambda b, h, i: (b, h)),
                    pl.BlockSpec((lc, LANES), lambda b, h, i: (b, h))]
        args = [q_bf16, k_bf16, p, bias_tabs, ctx_k, ctx_v]
    else:
        tq = seq_len
        nq = 1
        in_specs = [pl.BlockSpec((tq, LANES), lambda b, h, i: (b, h)),
                    pl.BlockSpec((seq_len, LANES), lambda b, h, i: (b, h)),
                    pl.BlockSpec((seq_len, LANES), lambda b, h, i: (b, v_col0 + h))]
        args = [q_bf16, k_bf16, p]
    return pl.pallas_call(
        functools.partial(_na_attn_kernel, windowed=windowed, rows=rows),
        grid=(batch, NA_HEADS, nq),
        in_specs=in_specs,
        out_specs=pl.BlockSpec((tq, LANES), lambda b, h, i: (b * nq + i, h)),
        out_shape=jax.ShapeDtypeStruct((batch * seq_len, NA_HEADS * LANES), jnp.float32),
        compiler_params=pltpu.CompilerParams(dimension_semantics=("parallel", "parallel", "arbitrary"),
                                             vmem_limit_bytes=VMEM_LIMIT_BYTES),
        name="na_attention",
    )(*args)


def even_mixer(x, mods, norm_g, b, l, ev, ctx_kv, bias_tabs):
    (w_in, w_out, q_g, k_g, conv_w, conv_b, w1, b1, f1, w2, b2, f2, w3, decay, bias) = ev
    p = mm_norm(x, norm_g, mods[0], mods[1], w_in)
    q, k, kn = na_prep(p, q_g, k_g)
    if ctx_kv is None:
        oa = na_attention(q, k, p, None, None, None, b, l)
    else:
        lc = ctx_kv[0].shape[1]
        oa = na_attention(q, k, p, bias_tabs, ctx_kv[0].reshape(b * lc, MIX_WIDTH),
                          ctx_kv[1].reshape(b * lc, MIX_WIDTH), b, l)
    hb = p[:, 3 * MIX_WIDTH:].reshape(b, l, 3 * HY_WIDTH)
    ob = hyena(hb, conv_w, conv_b, (w1, b1, f1, w2, b2, f2, w3, decay), bias)
    out = mm_pair_res(oa, ob.reshape(b * l, HY_WIDTH), w_out, x, mods[2])
    shp = (b, l, NA_HEADS, HEAD_DIM)
    return out, kn.reshape(shp), p[:, 2 * MIX_WIDTH:3 * MIX_WIDTH].reshape(shp)


def odd_mixer(x, mods, norm_g, b, l, od, lam_init, h0_re, h0_im, ctx_kv, rope):
    (w_in, w_out, s5_prep, d_skip, glu_w, glu_b, q_g, k_g, lq1, lk1, lq2, lk2, subln_g) = od
    p = mm_norm(x, norm_g, mods[0], mods[1], w_in)
    oc, fr, fi = s5_mixer(p[:, :S5_WIDTH].reshape(b, l, S5_WIDTH), h0_re, h0_im, s5_prep, d_skip, glu_w, glu_b)
    lam = (jnp.exp(jnp.sum(lq1 * lk1).astype(jnp.float32))
           - jnp.exp(jnp.sum(lq2 * lk2).astype(jnp.float32)) + lam_init)
    cos, sin_signed = rope
    q, k, kn = diff_prep(p, q_g, k_g, cos, sin_signed, l, use_rope=ctx_kv is not None)
    gain = subln_g.astype(jnp.float32) * (1.0 - lam_init)
    if ctx_kv is None:
        o = diff_attention(q, k, p, None, None, lam, gain, b, l)
    else:
        lc = ctx_kv[0].shape[1]
        o = diff_attention(q, k, p, ctx_kv[0].reshape(b * lc, MIX_WIDTH), ctx_kv[1].reshape(b * lc, MIX_WIDTH),
                           lam, gain, b, l)
    out = mm_pair_res(oc.reshape(b * l, S5_WIDTH), o, w_out, x, mods[2])
    kn = kn.reshape(b, l, DIFF_HEADS, 2, DIFF_QK_DIM)
    v = p[:, S5_WIDTH + 2 * MIX_WIDTH:].reshape(b, l, DIFF_HEADS, DIFF_V_DIM)
    return out, kn, v, fr, fi


PEER_TOKEN_TILE = 512
PEER_EXPERT_TILE = 1024
PEER_SUB_EXPERTS = 256


def _top16_rows(cur, iota):
    n = cur.shape[0]
    tops, idxs = [], []
    for _ in range(PEER_TOPK):
        m = jnp.max(cur, axis=0, keepdims=True)
        first = jnp.min(jnp.where(cur == m, iota, n), axis=0, keepdims=True)
        tops.append(m)
        idxs.append(first)
        cur = jnp.where(iota == first, -jnp.inf, cur)
    return tops, idxs


def _router_kernel(q_ref, keys_ref, th_ref, s1_ref, e0_ref, e1_ref, st_ref):
    tt = q_ref.shape[0]
    half = PEER_QUERY_DIM // 2
    for h in range(PEER_HEADS):
        for s in range(2):
            qs = q_ref[:, (2 * h + s) * half:(2 * h + s + 1) * half].astype(jnp.bfloat16)
            st = lax.dot_general(keys_ref[h, s], qs, (((1,), (1,)), ((), ())),
                                 preferred_element_type=jnp.float32)
            st_ref[2 * h + s] = st

    iota128 = lax.broadcasted_iota(jnp.int32, (PEER_N_KEYS, LANES), 0)
    iota8 = lax.broadcasted_iota(jnp.int32, (SUBLANES, LANES), 0)
    iota_cand = lax.broadcasted_iota(jnp.int32, (PEER_TOPK + SUBLANES * SUBLANES, LANES), 0)
    n_chunks = tt // LANES

    def body(i, carry):
        h = i // n_chunks
        lane0 = pl.multiple_of((i % n_chunks) * LANES, LANES)
        s0 = st_ref[2 * h, :, pl.ds(lane0, LANES)]
        s1 = st_ref[2 * h + 1, :, pl.ds(lane0, LANES)]
        a, a_idx = _top16_rows(s0, iota128)
        b, _ = _top16_rows(s1, iota128)
        bmat = jnp.concatenate(b, axis=0)
        sums = [a[k] + bmat for k in range(PEER_TOPK)]
        cand = [sums[0], sums[1][:SUBLANES]]
        for k in range(2, SUBLANES):
            cand.append(jnp.where(iota8 < PEER_TOPK // (k + 1), sums[k][:SUBLANES], -jnp.inf))
        cand.append(jnp.concatenate(a[SUBLANES:], axis=0) + b[0])
        f, _ = _top16_rows(jnp.concatenate(cand, axis=0), iota_cand)
        tau = f[PEER_TOPK - 1]
        z = jnp.ones_like(f[0])
        for k in range(1, PEER_TOPK):
            z = z + jnp.exp(f[k] - f[0])
        th = jnp.full((PEER_N_KEYS, LANES), jnp.inf, jnp.float32)
        for k in range(PEER_TOPK):
            th_k = jnp.min(jnp.where(sums[k] >= tau, bmat, jnp.inf), axis=0, keepdims=True)
            th = jnp.where(iota128 == a_idx[k], th_k, th)
        th_ref[h, :, pl.ds(lane0, LANES)] = th
        s1_ref[h, :, pl.ds(lane0, LANES)] = s1
        e0_ref[h, :, pl.ds(lane0, LANES)] = jnp.exp(s0 - a[0]) / z
        e1_ref[h, :, pl.ds(lane0, LANES)] = jnp.exp(s1 - b[0])
        return carry

    lax.fori_loop(0, PEER_HEADS * n_chunks, body, 0)


def peer_router(q, keys_bf16, tt):
    t = q.shape[0]
    f32 = jnp.float32
    big = jax.ShapeDtypeStruct((PEER_HEADS, PEER_N_KEYS, t), f32)
    blk = pl.BlockSpec((PEER_HEADS, PEER_N_KEYS, tt), lambda i: (0, 0, i))
    return pl.pallas_call(
        _router_kernel,
        grid=(t // tt,),
        in_specs=[pl.BlockSpec((tt, q.shape[1]), lambda i: (i, 0)),
                  pl.BlockSpec(keys_bf16.shape, lambda i: (0, 0, 0, 0))],
        out_specs=[blk, blk, blk, blk],
        out_shape=[big, big, big, big],
        scratch_shapes=[pltpu.VMEM((2 * PEER_HEADS, PEER_N_KEYS, tt), f32)],
        compiler_params=pltpu.CompilerParams(dimension_semantics=("parallel",),
                                             vmem_limit_bytes=VMEM_LIMIT_BYTES),
        name="peer_router",
    )(q, keys_bf16)


def _dense_kernel(x_ref, u_ref, vt_ref, th_ref, s1_ref, e0_ref, e1_ref, res_ref, gate_ref,
                  o_ref, at_ref, wt_ref, acc_ref):
    e = pl.program_id(1)
    n_e = pl.num_programs(1)
    eb = u_ref.shape[0]
    tt = x_ref.shape[0]
    n_chunks = tt // LANES
    n_sub = eb // PEER_SUB_EXPERTS
    i1_per_sub = PEER_SUB_EXPERTS // PEER_N_KEYS
    nt = (((1,), (1,)), ((), ()))

    @pl.when(e == 0)
    def _():
        acc_ref[...] = jnp.zeros_like(acc_ref)

    x = x_ref[...]
    for sb in range(n_sub):
        rows = slice(sb * PEER_SUB_EXPERTS, (sb + 1) * PEER_SUB_EXPERTS)
        at_ref[rows, :] = lax.dot_general(u_ref[rows, :], x, nt, preferred_element_type=jnp.float32)
    for sb in range(n_sub):
        for c in range(n_chunks):
            lanes = slice(c * LANES, (c + 1) * LANES)
            for k in range(i1_per_sub):
                i1 = sb * i1_per_sub + k
                g = jnp.zeros((PEER_N_KEYS, LANES), jnp.float32)
                for h in range(PEER_HEADS):
                    throw = th_ref[h, i1:i1 + 1, lanes]
                    e0row = e0_ref[h, i1:i1 + 1, lanes]
                    g = g + jnp.where(s1_ref[h, :, lanes] >= throw, e0row * e1_ref[h, :, lanes], 0.0)
                r = slice(i1 * PEER_N_KEYS, (i1 + 1) * PEER_N_KEYS)
                wt_ref[r, lanes] = (_gelu_tanh(at_ref[r, lanes]) * g).astype(jnp.bfloat16)
    acc_ref[...] += jnp.dot(vt_ref[...], wt_ref[...], preferred_element_type=jnp.float32)

    @pl.when(e == n_e - 1)
    def _():
        o_ref[...] = res_ref[...] + gate_ref[0] * acc_ref[...].T


def peer_dense(x_bf16, u_bf16, vt_bf16, th, s1, e0, e1, resid, gate, tokens_per_gate, tt, eb):
    t, d = x_bf16.shape
    n_exp = u_bf16.shape[0]
    assert t % tt == 0 and n_exp % eb == 0 and tokens_per_gate % tt == 0
    assert eb // PEER_N_KEYS == SUBLANES
    tiles_per_gate = tokens_per_gate // tt
    rblk = pl.BlockSpec((PEER_HEADS, PEER_N_KEYS, tt), lambda i, j: (0, 0, i))
    gblk = pl.BlockSpec((PEER_HEADS, SUBLANES, tt), lambda i, j: (0, j, i))
    return pl.pallas_call(
        _dense_kernel,
        grid=(t // tt, n_exp // eb),
        in_specs=[pl.BlockSpec((tt, d), lambda i, j: (i, 0)),
                  pl.BlockSpec((eb, d), lambda i, j: (j, 0)),
                  pl.BlockSpec((d, eb), lambda i, j: (0, j)),
                  gblk, rblk, gblk, rblk,
                  pl.BlockSpec((tt, d), lambda i, j: (i, 0)),
                  pl.BlockSpec((1, 1, d), lambda i, j: (i // tiles_per_gate, 0, 0))],
        out_specs=pl.BlockSpec((tt, d), lambda i, j: (i, 0)),
        out_shape=jax.ShapeDtypeStruct((t, d), jnp.float32),
        scratch_shapes=[pltpu.VMEM((eb, tt), jnp.float32),
                        pltpu.VMEM((eb, tt), jnp.bfloat16),
                        pltpu.VMEM((d, tt), jnp.float32)],
        compiler_params=pltpu.CompilerParams(dimension_semantics=("parallel", "arbitrary"),
                                             vmem_limit_bytes=VMEM_LIMIT_BYTES),
        name="peer_dense",
    )(x_bf16, u_bf16, vt_bf16, th, s1, e0, e1, resid, gate)


def peer_layer(x, norm_g, mods, w_q, keys, u_bf16, vt_bf16):
    t, d = x.shape
    q, h = mm_norm(x, norm_g, mods[3], mods[4], w_q, emit_h=True)
    gate = mods[5]
    th, s1, e0, e1 = peer_router(q, keys.astype(jnp.bfloat16), PEER_TOKEN_TILE)
    return peer_dense(h, u_bf16, vt_bf16, th, s1, e0, e1, x, gate, t // gate.shape[0],
                      PEER_TOKEN_TILE, PEER_EXPERT_TILE)


def kernel(x_prompt, x_sample, c, cache_na_k, cache_na_v, cache_diff_k, cache_diff_v, state_s5_re, state_s5_im, c_ctx, mod_w, mod_b, norm_mix_g, norm_ffn_g, ev_w_in, ev_w_out, na_q_g, na_k_g, na_rpb, hy_conv_w, hy_conv_b, hy_w1, hy_b1, hy_f1, hy_w2, hy_b2, hy_f2, hy_w3, hy_decay, hy_bias, od_w_in, od_w_out, s5_lam_re, s5_lam_im, s5_log_dt, s5_b_re, s5_b_im, s5_c_re, s5_c_im, s5_d, s5_glu_w, s5_glu_b, diff_q_g, diff_k_g, diff_lq1, diff_lk1, diff_lq2, diff_lk2, diff_subln_g, peer_w_q, peer_keys, peer_u, peer_v):
    bp, lp, d = x_prompt.shape
    bs, ls, _ = x_sample.shape
    rope_p, rope_s = rope_tables(lp), rope_tables(ls)
    xp, xs = x_prompt.reshape(bp * lp, d), x_sample.reshape(bs * ls, d)
    new_na_k, new_na_v, new_dk, new_dv, new_sr, new_si = [], [], [], [], [], []
    for i in range(DEPTH):
        mp = adaln(c_ctx[None, :], mod_w[i], mod_b[i])
        ms = adaln(c, mod_w[i], mod_b[i])
        j = i // 2
        if i % 2 == 0:
            ev = (ev_w_in[j], ev_w_out[j], na_q_g[j], na_k_g[j], hy_conv_w[j], hy_conv_b[j],
                  hy_w1[j], hy_b1[j], hy_f1[j], hy_w2[j], hy_b2[j], hy_f2[j], hy_w3[j], hy_decay[j], hy_bias[j])
            bias_tabs = na_bias_tables(na_rpb[j], ls // GRID_W)
            xp, kp, vp = even_mixer(xp, mp, norm_mix_g[i], bp, lp, ev, None, None)
            xs, _, _ = even_mixer(xs, ms, norm_mix_g[i], bs, ls, ev, (cache_na_k[:, j], cache_na_v[:, j]), bias_tabs)
            new_na_k.append(kp)
            new_na_v.append(vp)
        else:
            lam_init = 0.8 - 0.6 * math.exp(-0.3 * i)
            s5_prep = s5_prepare(s5_lam_re[j], s5_lam_im[j], s5_log_dt[j], s5_b_re[j], s5_b_im[j],
                                 s5_c_re[j], s5_c_im[j])
            od = (od_w_in[j], od_w_out[j], s5_prep, s5_d[j], s5_glu_w[j], s5_glu_b[j], diff_q_g[j], diff_k_g[j],
                  diff_lq1[j], diff_lk1[j], diff_lq2[j], diff_lk2[j], diff_subln_g[j])
            xp, kp, vp, sr, si = odd_mixer(xp, mp, norm_mix_g[i], bp, lp, od, lam_init, None, None, None, rope_p)
            xs, _, _, _, _ = odd_mixer(xs, ms, norm_mix_g[i], bs, ls, od, lam_init, state_s5_re[:, j],
                                       state_s5_im[:, j], (cache_diff_k[:, j], cache_diff_v[:, j]), rope_s)
            new_dk.append(kp)
            new_dv.append(vp)
            new_sr.append(sr)
            new_si.append(si)
        u_bf16 = peer_u[i].astype(jnp.bfloat16)
        vt_bf16 = peer_v[i].T.astype(jnp.bfloat16)
        xp = peer_layer(xp, norm_ffn_g[i], mp, peer_w_q[i], peer_keys[i], u_bf16, vt_bf16)
        xs = peer_layer(xs, norm_ffn_g[i], ms, peer_w_q[i], peer_keys[i], u_bf16, vt_bf16)
    return (xp.reshape(bp, lp, d), xs.reshape(bs, ls, d), jnp.stack(new_na_k, axis=1), jnp.stack(new_na_v, axis=1),
            jnp.stack(new_dk, axis=1), jnp.stack(new_dv, axis=1),
            jnp.stack(new_sr, axis=1), jnp.stack(new_si, axis=1))
```

```python
import functools
import math

import jax
import jax.numpy as jnp
from jax import lax
from jax.experimental import pallas as pl
from jax.experimental.pallas import tpu as pltpu

D_MODEL = 2048
DEPTH = 2
GRID_W = 64
MIX_WIDTH = D_MODEL // 2
HEAD_DIM = 128
NA_HEADS = MIX_WIDTH // HEAD_DIM
NA_KH = 8
NA_KW = 16
HY_WIDTH = MIX_WIDTH
HY_ORDER = 2
HY_POS_EMB = 33
S5_WIDTH = MIX_WIDTH
S5_GROUP = 16
S5_GROUPS = S5_WIDTH // S5_GROUP
S5_STATE = 64
DIFF_HEADS = MIX_WIDTH // HEAD_DIM
DIFF_QK_DIM = HEAD_DIM // 2
DIFF_V_DIM = HEAD_DIM
ROPE_BASE = 10000.0
PEER_HEADS = 8
PEER_N_KEYS = 128
PEER_QUERY_DIM = 256
PEER_TOPK = 16
NORM_EPS = 1e-6
NEG_INF = -1e30

LANES = 128
SUBLANES = 8
VMEM_LIMIT_BYTES = 56 * 1024 * 1024


MM_ROW_TILE = 1024
MM_COL_TILE = 512


def _mm_norm_kernel(x_ref, g_ref, sc_ref, sh_ref, w_ref, *out_refs, emit_h):
    if emit_h:
        o_ref, h_ref, hs_ref = out_refs
    else:
        o_ref, hs_ref = out_refs

    @pl.when(pl.program_id(1) == 0)
    def _():
        x = x_ref[...]
        y = x * lax.rsqrt(jnp.mean(x * x, axis=-1, keepdims=True) + NORM_EPS) * g_ref[...]
        hs_ref[...] = (y * (1.0 + sc_ref[0]) + sh_ref[0]).astype(jnp.bfloat16)
        if emit_h:
            h_ref[...] = hs_ref[...]

    o_ref[...] = jnp.dot(hs_ref[...], w_ref[...].astype(jnp.bfloat16), preferred_element_type=jnp.float32)


def mm_norm(x, norm_g, shift, scale, w, emit_h=False):
    m, k = x.shape
    n = w.shape[1]
    tm, tn = min(MM_ROW_TILE, m), min(MM_COL_TILE, n)
    nb = scale.shape[0]
    assert m % tm == 0 and n % tn == 0 and (m // nb) % tm == 0
    tiles_per_mod = (m // nb) // tm
    mod = pl.BlockSpec((1, 1, k), lambda i, j: (i // tiles_per_mod, 0, 0))
    out_specs = [pl.BlockSpec((tm, tn), lambda i, j: (i, j))]
    out_shape = [jax.ShapeDtypeStruct((m, n), jnp.float32)]
    if emit_h:
        out_specs.append(pl.BlockSpec((tm, k), lambda i, j: (i, 0)))
        out_shape.append(jax.ShapeDtypeStruct((m, k), jnp.bfloat16))
    res = pl.pallas_call(
        functools.partial(_mm_norm_kernel, emit_h=emit_h),
        grid=(m // tm, n // tn),
        in_specs=[pl.BlockSpec((tm, k), lambda i, j: (i, 0)),
                  pl.BlockSpec((1, k), lambda i, j: (0, 0)),
                  mod, mod,
                  pl.BlockSpec((k, tn), lambda i, j: (0, j))],
        out_specs=out_specs,
        out_shape=out_shape,
        scratch_shapes=[pltpu.VMEM((tm, k), jnp.bfloat16)],
        compiler_params=pltpu.CompilerParams(dimension_semantics=("parallel", "arbitrary"),
                                             vmem_limit_bytes=VMEM_LIMIT_BYTES),
        name="mm_norm",
    )(x, norm_g.astype(jnp.float32).reshape(1, k), scale, shift, w)
    return res if emit_h else res[0]


def _mm_pair_res_kernel(a_ref, b_ref, wa_ref, wb_ref, r_ref, g_ref, o_ref):
    y = (jnp.dot(a_ref[...].astype(jnp.bfloat16), wa_ref[...].astype(jnp.bfloat16), preferred_element_type=jnp.float32)
         + jnp.dot(b_ref[...].astype(jnp.bfloat16), wb_ref[...].astype(jnp.bfloat16), preferred_element_type=jnp.float32))
    o_ref[...] = r_ref[...] + g_ref[0] * y


def mm_pair_res(a, b, w, resid, gate):
    m, kh = a.shape
    n = w.shape[1]
    tm, tn = min(MM_ROW_TILE, m), min(MM_COL_TILE, n)
    nb = gate.shape[0]
    assert m % tm == 0 and n % tn == 0 and (m // nb) % tm == 0 and w.shape[0] == 2 * kh
    tiles_per_mod = (m // nb) // tm
    return pl.pallas_call(
        _mm_pair_res_kernel,
        grid=(m // tm, n // tn),
        in_specs=[pl.BlockSpec((tm, kh), lambda i, j: (i, 0)),
                  pl.BlockSpec((tm, kh), lambda i, j: (i, 0)),
                  pl.BlockSpec((kh, tn), lambda i, j: (0, j)),
                  pl.BlockSpec((kh, tn), lambda i, j: (1, j)),
                  pl.BlockSpec((tm, tn), lambda i, j: (i, j)),
                  pl.BlockSpec((1, 1, tn), lambda i, j: (i // tiles_per_mod, 0, j))],
        out_specs=pl.BlockSpec((tm, tn), lambda i, j: (i, j)),
        out_shape=jax.ShapeDtypeStruct((m, n), jnp.float32),
        compiler_params=pltpu.CompilerParams(dimension_semantics=("parallel", "arbitrary"),
                                             vmem_limit_bytes=VMEM_LIMIT_BYTES),
        name="mm_pair_res",
    )(a, b, w, w, resid, gate)


def adaln(cond, w, b):
    m = jax.nn.silu(cond) @ w + b
    return jnp.split(m[:, None, :], 6, axis=-1)


def hyena_filters(l, w1, b1, f1, w2, b2, f2, w3, decay):
    t = jnp.linspace(0.0, 1.0, l, dtype=jnp.float32)[:, None]
    bands = (HY_POS_EMB - 1) // 2
    w_ang = 2.0 * math.pi * jnp.arange(l, dtype=jnp.float32)[:, None] / l
    freqs = jnp.linspace(1e-4, bands - 1, bands, dtype=jnp.float32)[None, :]
    z = jnp.concatenate([t, jnp.cos(freqs * w_ang), -jnp.sin(freqs * w_ang)], axis=-1)
    h = jnp.sin(f1 * (z @ w1 + b1))
    h = jnp.sin(f2 * (h @ w2 + b2))
    h = (h @ w3).reshape(l, 2, HY_ORDER, HY_WIDTH).astype(jnp.float32)
    h = h * jnp.exp(-t.reshape(l, 1, 1, 1) * jnp.abs(decay.astype(jnp.float32)))
    h_f, h_b = h[:, 0], h[:, 1]
    zero = jnp.zeros((1, HY_ORDER, HY_WIDTH), jnp.float32)
    return jnp.concatenate([h_f, zero, h_b[1:][::-1]], axis=0)


def _split_bf16(x):
    hi = x.astype(jnp.bfloat16)
    lo = (x - hi.astype(jnp.float32)).astype(jnp.bfloat16)
    return hi, lo


def _dot3(a_hi, a_lo, x):
    m = a_hi.shape[0]
    xh, xl = _split_bf16(x)
    r = jnp.dot(jnp.concatenate([a_hi, a_lo], axis=0), xh, preferred_element_type=jnp.float32)
    return r[:m] + r[m:] + jnp.dot(a_hi, xl, preferred_element_type=jnp.float32)


HY_STEP_ROWS = 256


def hyena_factors(l):
    n = 2 * l
    n2 = 64 if n >= 8192 else 16
    return n // n2, n2


def hyena_tables(l):
    n = 2 * l
    n1, n2 = hyena_factors(l)
    t = (n2 * jnp.arange(n1 // 2)[None, None, :] + jnp.arange(n2)[:, None, None])
    k1 = jnp.arange(n1)[None, :, None]
    ang = (2.0 * math.pi / n) * ((t * k1) % n).astype(jnp.float32)
    fa = jnp.concatenate([jnp.cos(ang), -jnp.sin(ang)], axis=1)
    fc = jnp.swapaxes(fa, 1, 2) / n
    a2 = (2.0 * math.pi / n2) * ((jnp.arange(n2)[:, None] * jnp.arange(n2)[None, :]) % n2).astype(jnp.float32)
    c, s = jnp.cos(a2), jnp.sin(a2)
    fb = jnp.concatenate([jnp.concatenate([c, s], axis=1), jnp.concatenate([-s, c], axis=1)], axis=0)
    fbi = jnp.concatenate([jnp.concatenate([c, -s], axis=1), jnp.concatenate([s, c], axis=1)], axis=0)
    return tuple(_split_bf16(x) for x in (fa, fc, fb, fbi))


def hyena_spectrum(filt, l):
    n1, n2 = hyena_factors(l)
    kf = jnp.fft.fft(filt, axis=0)
    kf = jnp.stack([jnp.real(kf), jnp.imag(kf)], axis=0).astype(jnp.float32)
    kf = kf.reshape(2, n2, n1, HY_ORDER, HY_WIDTH)
    return jnp.transpose(kf, (3, 0, 2, 1, 4))


def _hy_stage_a_kernel(z_ref, fh_ref, fl_ref, o_ref):
    for i in range(z_ref.shape[0]):
        o_ref[i] = _dot3(fh_ref[0], fl_ref[0], z_ref[i])


def _hy_stage_b_kernel(s_ref, k_ref, fh_ref, fl_ref, gh_ref, gl_ref, o_ref):
    n2 = s_ref.shape[3]
    kr, ki = k_ref[0, 0], k_ref[1, 0]
    for i in range(s_ref.shape[0]):
        y = jnp.concatenate([s_ref[i, 0, 0], s_ref[i, 1, 0]], axis=0)
        z = _dot3(fh_ref[...], fl_ref[...], y)
        zr, zi = z[:n2], z[n2:]
        p = jnp.concatenate([zr * kr - zi * ki, zr * ki + zi * kr], axis=0)
        q = _dot3(gh_ref[...], gl_ref[...], p)
        o_ref[i, 0, 0] = q[:n2]
        o_ref[i, 1, 0] = q[n2:]


def _hy_stage_c_kernel(q_ref, fh_ref, fl_ref, z_ref, g_ref, b_ref, o_ref):
    for i in range(q_ref.shape[0]):
        y = _dot3(fh_ref[0], fl_ref[0], q_ref[i])
        o_ref[i] = g_ref[i] * (y + z_ref[i] * b_ref[...])


def hyena_long_conv(z, z_col, gate, gate_col, bias_o, spec_o, tables, l):
    (fah, fal), (fch, fcl), (fbh, fbl), (fgh, fgl) = tables
    b = z.shape[0]
    c = HY_WIDTH
    n1, n2 = hyena_factors(l)
    h1 = n1 // 2
    bb = max(1, min(b, HY_STEP_ROWS // n1))
    assert b % bb == 0
    cp = pltpu.CompilerParams(dimension_semantics=("parallel", "parallel"), vmem_limit_bytes=VMEM_LIMIT_BYTES)
    zblocks = z.shape[-1] // c
    gblocks = gate.shape[-1] // c
    zv = z.reshape(b, h1, n2 * z.shape[-1])
    gv = gate.reshape(b, h1, n2 * gate.shape[-1])
    s1 = pl.pallas_call(
        _hy_stage_a_kernel,
        grid=(b // bb, n2),
        in_specs=[pl.BlockSpec((bb, h1, c), lambda i, j: (i, 0, j * zblocks + z_col)),
                  pl.BlockSpec((1, 2 * n1, h1), lambda i, j: (j, 0, 0)),
                  pl.BlockSpec((1, 2 * n1, h1), lambda i, j: (j, 0, 0))],
        out_specs=pl.BlockSpec((bb, 2 * n1, c), lambda i, j: (i, 0, j)),
        out_shape=jax.ShapeDtypeStruct((b, 2 * n1, n2 * c), jnp.float32),
        compiler_params=cp, name="hyena_stage_a",
    )(zv, fah, fal)
    s1 = s1.reshape(b, 2, n1, n2, c)
    mat = pl.BlockSpec((2 * n2, 2 * n2), lambda k, i: (0, 0))
    q = pl.pallas_call(
        _hy_stage_b_kernel,
        grid=(n1, b // bb),
        in_specs=[pl.BlockSpec((bb, 2, 1, n2, c), lambda k, i: (i, 0, k, 0, 0)),
                  pl.BlockSpec((2, 1, n2, c), lambda k, i: (0, k, 0, 0)),
                  mat, mat, mat, mat],
        out_specs=pl.BlockSpec((bb, 2, 1, n2, c), lambda k, i: (i, 0, k, 0, 0)),
        out_shape=jax.ShapeDtypeStruct((b, 2, n1, n2, c), jnp.float32),
        compiler_params=cp, name="hyena_stage_b",
    )(s1, spec_o, fbh, fbl, fgh, fgl)
    q = q.reshape(b, 2 * n1, n2 * c)
    out = pl.pallas_call(
        _hy_stage_c_kernel,
        grid=(b // bb, n2),
        in_specs=[pl.BlockSpec((bb, 2 * n1, c), lambda i, j: (i, 0, j)),
                  pl.BlockSpec((1, h1, 2 * n1), lambda i, j: (j, 0, 0)),
                  pl.BlockSpec((1, h1, 2 * n1), lambda i, j: (j, 0, 0)),
                  pl.BlockSpec((bb, h1, c), lambda i, j: (i, 0, j * zblocks + z_col)),
                  pl.BlockSpec((bb, h1, c), lambda i, j: (i, 0, j * gblocks + gate_col)),
                  pl.BlockSpec((1, c), lambda i, j: (0, 0))],
        out_specs=pl.BlockSpec((bb, h1, c), lambda i, j: (i, 0, j)),
        out_shape=jax.ShapeDtypeStruct((b, h1, n2 * c), jnp.float32),
        compiler_params=cp, name="hyena_stage_c",
    )(q, fch, fcl, zv, gv, bias_o.astype(jnp.float32).reshape(1, c))
    return out.reshape(b, l, c)


def hyena(u, conv_w, conv_b, filt, bias):
    b, l, _ = u.shape
    up = jnp.pad(u, ((0, 0), (1, 1), (0, 0)))
    u = up[:, :-2] * conv_w[0] + up[:, 1:-1] * conv_w[1] + up[:, 2:] * conv_w[2] + conv_b
    tables = hyena_tables(l)
    spec = hyena_spectrum(hyena_filters(l, *filt), l)
    z, z_col = u, 0
    for o in range(HY_ORDER):
        z = hyena_long_conv(z, z_col, u, 1 + o, bias[o], spec[o], tables, l)
        z_col = 0
    return z


S5_TILE_GROUPS = 8
S5_TILES = S5_GROUPS // S5_TILE_GROUPS
S5_TILE_IN = S5_TILE_GROUPS * S5_GROUP
S5_TILE_STATE = S5_TILE_GROUPS * S5_STATE
S5_ROWS_PER_STEP = 2048


def s5_prepare(lam_re, lam_im, log_dt, b_re, b_im, c_re, c_im):
    f32 = jnp.float32
    lam_re, lam_im = lam_re.astype(f32), lam_im.astype(f32)
    dt = jnp.exp(log_dt.astype(f32))[..., None]
    mag = jnp.exp(lam_re * dt)
    ar, ai = mag * jnp.cos(lam_im * dt), mag * jnp.sin(lam_im * dt)
    den = lam_re * lam_re + lam_im * lam_im
    cr = ((ar - 1.0) * lam_re + ai * lam_im) / den
    ci = (ai * lam_re - (ar - 1.0) * lam_im) / den
    bbr = cr[..., None] * b_re - ci[..., None] * b_im
    bbi = cr[..., None] * b_im + ci[..., None] * b_re
    eye = jnp.eye(S5_TILE_GROUPS, dtype=f32)

    def tile_in(bb):
        x = bb.reshape(2, S5_TILES, S5_TILE_GROUPS, S5_STATE, S5_GROUP)
        x = jnp.einsum('dtgnp,gh->dtgphn', x, eye)
        return x.reshape(2, S5_TILES, S5_TILE_IN, S5_TILE_STATE)

    def tile_out(cc):
        x = cc.astype(f32).reshape(2, S5_TILES, S5_TILE_GROUPS, S5_GROUP, S5_STATE)
        x = jnp.einsum('dtgpn,gh->dtgnhp', x, eye)
        return x.reshape(2, S5_TILES, S5_TILE_STATE, S5_TILE_IN)

    win = jnp.concatenate([tile_in(bbr), tile_in(bbi)], axis=-1).astype(jnp.bfloat16)
    wout = jnp.concatenate([tile_out(c_re), -tile_out(c_im)], axis=-2).astype(jnp.bfloat16)
    a = jnp.stack([ar.reshape(2, S5_TILES, S5_TILE_STATE), ai.reshape(2, S5_TILES, S5_TILE_STATE)], axis=2)
    return win, wout, a


def _s5_scan_kernel(u_ref, win_ref, wout_ref, a_ref, h0_ref, y_ref, fin_ref, bu_ref, st_ref, *, batch):
    d = pl.program_id(0)
    c = pl.program_id(2)
    n_c = pl.num_programs(2)
    ns = S5_TILE_STATE
    steps = u_ref.shape[0] // batch

    @pl.when(c == 0)
    def _():
        st_ref[...] = h0_ref[0, 0]

    bu_ref[...] = jnp.dot(u_ref[...].astype(jnp.bfloat16), win_ref[0, 0], preferred_element_type=jnp.float32)
    ar = jnp.broadcast_to(a_ref[0, 0, 0:1, :], (batch, ns))
    ai = jnp.broadcast_to(a_ref[0, 0, 1:2, :], (batch, ns))

    def step(t, carry):
        hr, hi = carry
        te = jnp.where(d == 0, t, steps - 1 - t)
        r0 = pl.multiple_of(te * batch, batch)
        nr = ar * hr - ai * hi + bu_ref[pl.ds(r0, batch), 0:ns]
        ni = ar * hi + ai * hr + bu_ref[pl.ds(r0, batch), ns:2 * ns]
        bu_ref[pl.ds(r0, batch), 0:ns] = nr
        bu_ref[pl.ds(r0, batch), ns:2 * ns] = ni
        return nr, ni

    hr, hi = lax.fori_loop(0, steps, step, (st_ref[:, 0:ns], st_ref[:, ns:2 * ns]), unroll=4)
    st_ref[:, 0:ns] = hr
    st_ref[:, ns:2 * ns] = hi
    y_ref[0] = jnp.dot(bu_ref[...].astype(jnp.bfloat16), wout_ref[0, 0], preferred_element_type=jnp.float32)

    @pl.when(c == n_c - 1)
    def _():
        fin_ref[0, 0] = st_ref[...]


def s5_scan(u_tm, win, wout, a, h0, batch):
    rows = u_tm.shape[0]
    r = min(S5_ROWS_PER_STEP, rows)
    assert rows % r == 0 and r % batch == 0
    n_c = rows // r

    def chunk(d, c):
        return c + d * (n_c - 1 - 2 * c)

    return pl.pallas_call(
        functools.partial(_s5_scan_kernel, batch=batch),
        grid=(2, S5_TILES, n_c),
        in_specs=[pl.BlockSpec((r, S5_TILE_IN), lambda d, j, c: (chunk(d, c), j)),
                  pl.BlockSpec((1, 1, S5_TILE_IN, 2 * S5_TILE_STATE), lambda d, j, c: (d, j, 0, 0)),
                  pl.BlockSpec((1, 1, 2 * S5_TILE_STATE, S5_TILE_IN), lambda d, j, c: (d, j, 0, 0)),
                  pl.BlockSpec((1, 1, 2, S5_TILE_STATE), lambda d, j, c: (d, j, 0, 0)),
                  pl.BlockSpec((1, 1, batch, 2 * S5_TILE_STATE), lambda d, j, c: (d, j, 0, 0))],
        out_specs=[pl.BlockSpec((1, r, S5_TILE_IN), lambda d, j, c: (d, chunk(d, c), j)),
                   pl.BlockSpec((1, 1, batch, 2 * S5_TILE_STATE), lambda d, j, c: (d, j, 0, 0))],
        out_shape=[jax.ShapeDtypeStruct((2, rows, S5_WIDTH), jnp.float32),
                   jax.ShapeDtypeStruct((2, S5_TILES, batch, 2 * S5_TILE_STATE), jnp.float32)],
        scratch_shapes=[pltpu.VMEM((r, 2 * S5_TILE_STATE), jnp.float32),
                        pltpu.VMEM((batch, 2 * S5_TILE_STATE), jnp.float32)],
        compiler_params=pltpu.CompilerParams(dimension_semantics=("parallel", "parallel", "arbitrary"),
                                             vmem_limit_bytes=VMEM_LIMIT_BYTES),
        name="s5_scan",
    )(u_tm, win, wout, a, h0)


def _gelu_tanh(x):
    return 0.5 * x * (1.0 + jnp.tanh(math.sqrt(2.0 / math.pi) * (x + 0.044715 * (x * x * x))))


def _s5_glu_kernel(u_ref, y_ref, d_ref, w_ref, b_ref, o_ref):
    y = d_ref[...] * u_ref[...] + y_ref[0] + y_ref[1]
    y = _gelu_tanh(y)
    z = jnp.dot(y.astype(jnp.bfloat16), w_ref[...], preferred_element_type=jnp.float32) + b_ref[...]
    o_ref[...] = y * (1.0 / (1.0 + jnp.exp(-z)))


def s5_glu(u_tm, y, d_skip, glu_w_bf16, glu_b, tr=1024):
    rows, w = u_tm.shape
    tr = min(tr, rows)
    assert rows % tr == 0
    return pl.pallas_call(
        _s5_glu_kernel,
        grid=(rows // tr,),
        in_specs=[pl.BlockSpec((tr, w), lambda i: (i, 0)),
                  pl.BlockSpec((2, tr, w), lambda i: (0, i, 0)),
                  pl.BlockSpec((1, w), lambda i: (0, 0)),
                  pl.BlockSpec((w, w), lambda i: (0, 0)),
                  pl.BlockSpec((1, w), lambda i: (0, 0))],
        out_specs=pl.BlockSpec((tr, w), lambda i: (i, 0)),
        out_shape=jax.ShapeDtypeStruct((rows, w), jnp.float32),
        compiler_params=pltpu.CompilerParams(dimension_semantics=("parallel",),
                                             vmem_limit_bytes=VMEM_LIMIT_BYTES),
        name="s5_glu",
    )(u_tm, y, d_skip.reshape(1, w), glu_w_bf16, glu_b.reshape(1, w))


def s5_mixer(u, h0_re, h0_im, prep, d_skip, glu_w, glu_b):
    win, wout, a = prep
    b, l, w = u.shape
    u_tm = jnp.swapaxes(u, 0, 1).reshape(l * b, w)
    if h0_re is None:
        h0 = jnp.zeros((2, S5_TILES, b, 2 * S5_TILE_STATE), jnp.float32)
    else:
        def tiles(h):
            return jnp.transpose(h.astype(jnp.float32).reshape(b, 2, S5_TILES, S5_TILE_STATE), (1, 2, 0, 3))
        h0 = jnp.concatenate([tiles(h0_re), tiles(h0_im)], axis=-1)
    y, fin = s5_scan(u_tm, win, wout, a, h0, b)
    out_tm = s5_glu(u_tm, y, d_skip, glu_w.astype(jnp.bfloat16), glu_b)
    out = jnp.swapaxes(out_tm.reshape(l, b, w), 0, 1)

    def untile(f):
        return jnp.transpose(f, (2, 0, 1, 3)).reshape(b, 2, S5_GROUPS, S5_STATE)
    return out, untile(fin[..., :S5_TILE_STATE]), untile(fin[..., S5_TILE_STATE:])


def _half_mean_matrix():
    i = lax.broadcasted_iota(jnp.int32, (LANES, LANES), 0) // DIFF_QK_DIM
    j = lax.broadcasted_iota(jnp.int32, (LANES, LANES), 1) // DIFF_QK_DIM
    return jnp.where(i == j, 1.0 / DIFF_QK_DIM, 0.0).astype(jnp.bfloat16)


def _rms_groups(x, gain, avg):
    xx = x * x
    hi = xx.astype(jnp.bfloat16)
    lo = (xx - hi.astype(jnp.float32)).astype(jnp.bfloat16)
    ms = (jnp.dot(hi, avg, preferred_element_type=jnp.float32)
          + jnp.dot(lo, avg, preferred_element_type=jnp.float32))
    return x * lax.rsqrt(ms + NORM_EPS) * gain


def _rope_lanes(y, cos, sin_signed, first_half):
    rot = jnp.where(first_half, pltpu.roll(y, LANES - 16, 1), pltpu.roll(y, 16, 1))
    return y * cos + rot * sin_signed


def _diff_prep_kernel(q_ref, k_ref, qg_ref, kg_ref, cos_ref, sin_ref, qo_ref, ko_ref, kn_ref, *, use_rope):
    avg = _half_mean_matrix()
    lane = lax.broadcasted_iota(jnp.int32, (1, LANES), 1)
    first_half = (lane % 32) < 16
    scale = DIFF_QK_DIM ** -0.5
    for h in range(DIFF_HEADS):
        cols = slice(h * LANES, (h + 1) * LANES)
        qn = _rms_groups(q_ref[:, cols], qg_ref[...], avg)
        kn = _rms_groups(k_ref[:, cols], kg_ref[...], avg)
        kn_ref[:, cols] = kn
        if use_rope:
            qn = _rope_lanes(qn, cos_ref[...], sin_ref[...], first_half)
            kn = _rope_lanes(kn, cos_ref[...], sin_ref[...], first_half)
        qo_ref[:, cols] = (qn * scale).astype(jnp.bfloat16)
        ko_ref[:, cols] = kn.astype(jnp.bfloat16)


def diff_prep(p, q_g, k_g, cos, sin_signed, seq_len, use_rope, tr=512):
    rows = p.shape[0]
    tr = min(tr, seq_len)
    assert seq_len % tr == 0
    w = DIFF_HEADS * LANES
    per_seq = seq_len // tr
    g2 = lambda g: jnp.tile(g.astype(jnp.float32), 2).reshape(1, LANES)
    return pl.pallas_call(
        functools.partial(_diff_prep_kernel, use_rope=use_rope),
        grid=(rows // tr,),
        in_specs=[pl.BlockSpec((tr, w), lambda i: (i, 1)),
                  pl.BlockSpec((tr, w), lambda i: (i, 2)),
                  pl.BlockSpec((1, LANES), lambda i: (0, 0)),
                  pl.BlockSpec((1, LANES), lambda i: (0, 0)),
                  pl.BlockSpec((tr, LANES), lambda i: (i % per_seq, 0)),
                  pl.BlockSpec((tr, LANES), lambda i: (i % per_seq, 0))],
        out_specs=[pl.BlockSpec((tr, w), lambda i: (i, 0))] * 3,
        out_shape=[jax.ShapeDtypeStruct((rows, w), jnp.bfloat16), jax.ShapeDtypeStruct((rows, w), jnp.bfloat16),
                   jax.ShapeDtypeStruct((rows, w), jnp.float32)],
        compiler_params=pltpu.CompilerParams(dimension_semantics=("parallel",),
                                             vmem_limit_bytes=VMEM_LIMIT_BYTES),
        name="diff_prep",
    )(p, p, g2(q_g), g2(k_g), cos, sin_signed)


def rope_tables(l):
    t = jnp.arange(l)
    row = (t // GRID_W).astype(jnp.float32)
    col = (t % GRID_W).astype(jnp.float32)
    nf = DIFF_QK_DIM // 4
    inv = ROPE_BASE ** (-jnp.arange(nf, dtype=jnp.float32) / nf)
    ang = jnp.stack([row[:, None] * inv, col[:, None] * inv], axis=1)
    ang = jnp.stack([ang, ang], axis=2).reshape(l, DIFF_QK_DIM)
    sign = jnp.where((jnp.arange(DIFF_QK_DIM) % 32) < 16, -1.0, 1.0)
    return jnp.tile(jnp.cos(ang), (1, 2)), jnp.tile(jnp.sin(ang) * sign, (1, 2))


def _diff_attn_kernel(*refs, has_ctx):
    if has_ctx:
        q_ref, k_ref, v_ref, ck_ref, cv_ref, lam_ref, g_ref, o_ref = refs
    else:
        q_ref, k_ref, v_ref, lam_ref, g_ref, o_ref = refs
    q = q_ref[...]
    lane = lax.broadcasted_iota(jnp.int32, (1, LANES), 1)
    zero = jnp.zeros_like(q)
    qs = (jnp.where(lane < DIFF_QK_DIM, q, zero), jnp.where(lane >= DIFF_QK_DIM, q, zero))
    nt = (((1,), (1,)), ((), ()))
    k = k_ref[...]
    ck = ck_ref[...].astype(jnp.bfloat16) if has_ctx else None
    lam = lam_ref[0:1, 0:1]
    w_self, w_ctx = None, None
    for i in range(2):
        s = lax.dot_general(qs[i], k, nt, preferred_element_type=jnp.float32)
        m = jnp.max(s, axis=-1, keepdims=True)
        if has_ctx:
            sc = lax.dot_general(qs[i], ck, nt, preferred_element_type=jnp.float32)
            m = jnp.maximum(m, jnp.max(sc, axis=-1, keepdims=True))
        p = jnp.exp(s - m)
        l = jnp.sum(p, axis=-1, keepdims=True)
        if has_ctx:
            pc = jnp.exp(sc - m)
            l = l + jnp.sum(pc, axis=-1, keepdims=True)
        coef = 1.0 / l if i == 0 else -lam / l
        w_self = p * coef if i == 0 else w_self + p * coef
        if has_ctx:
            w_ctx = pc * coef if i == 0 else w_ctx + pc * coef
    o = jnp.dot(w_self.astype(jnp.bfloat16), v_ref[...].astype(jnp.bfloat16), preferred_element_type=jnp.float32)
    if has_ctx:
        o = o + jnp.dot(w_ctx.astype(jnp.bfloat16), cv_ref[...].astype(jnp.bfloat16),
                        preferred_element_type=jnp.float32)
    ms = jnp.mean(o * o, axis=-1, keepdims=True)
    o_ref[...] = o * lax.rsqrt(ms + NORM_EPS) * g_ref[...]


def diff_attention(q_bf16, k_bf16, p, ctx_k, ctx_v, lam, gain, batch, seq_len, tq=256):
    tq = min(tq, seq_len)
    nq = seq_len // tq
    has_ctx = ctx_k is not None
    v_col0 = 3 * DIFF_HEADS
    in_specs = [pl.BlockSpec((tq, LANES), lambda b, h, i: (b * nq + i, h)),
                pl.BlockSpec((seq_len, LANES), lambda b, h, i: (b, h)),
                pl.BlockSpec((seq_len, LANES), lambda b, h, i: (b, v_col0 + h))]
    args = [q_bf16, k_bf16, p]
    if has_ctx:
        lc = ctx_k.shape[0] // batch
        in_specs += [pl.BlockSpec((lc, LANES), lambda b, h, i: (b, h))] * 2
        args += [ctx_k, ctx_v]
    in_specs += [pl.BlockSpec((1, LANES), lambda b, h, i: (0, 0))] * 2
    args += [jnp.broadcast_to(lam.astype(jnp.float32), (1, LANES)), gain.astype(jnp.float32).reshape(1, LANES)]
    return pl.pallas_call(
        functools.partial(_diff_attn_kernel, has_ctx=has_ctx),
        grid=(batch, DIFF_HEADS, nq),
        in_specs=in_specs,
        out_specs=pl.BlockSpec((tq, LANES), lambda b, h, i: (b * nq + i, h)),
        out_shape=jax.ShapeDtypeStruct((batch * seq_len, DIFF_HEADS * LANES), jnp.float32),
        compiler_params=pltpu.CompilerParams(dimension_semantics=("parallel", "parallel", "arbitrary"),
                                             vmem_limit_bytes=VMEM_LIMIT_BYTES),
        name="diff_attention",
    )(*args)


NA_Q_ROWS = 8
NA_WIN_ROWS = 16


def _na_prep_kernel(q_ref, k_ref, qg_ref, kg_ref, qo_ref, ko_ref, kn_ref):
    avg = jnp.full((LANES, LANES), 1.0 / HEAD_DIM, jnp.bfloat16)
    for h in range(NA_HEADS):
        cols = slice(h * LANES, (h + 1) * LANES)
        qo_ref[:, cols] = _rms_groups(q_ref[:, cols], qg_ref[...], avg).astype(jnp.bfloat16)
        kn = _rms_groups(k_ref[:, cols], kg_ref[...], avg)
        kn_ref[:, cols] = kn
        ko_ref[:, cols] = kn.astype(jnp.bfloat16)


def na_prep(p, q_g, k_g, tr=512):
    rows = p.shape[0]
    tr = min(tr, rows)
    w = NA_HEADS * LANES
    g1 = lambda g: g.astype(jnp.float32).reshape(1, LANES)
    return pl.pallas_call(
        _na_prep_kernel,
        grid=(rows // tr,),
        in_specs=[pl.BlockSpec((tr, w), lambda i: (i, 0)),
                  pl.BlockSpec((tr, w), lambda i: (i, 1)),
                  pl.BlockSpec((1, LANES), lambda i: (0, 0)),
                  pl.BlockSpec((1, LANES), lambda i: (0, 0))],
        out_specs=[pl.BlockSpec((tr, w), lambda i: (i, 0))] * 3,
        out_shape=[jax.ShapeDtypeStruct((rows, w), jnp.bfloat16), jax.ShapeDtypeStruct((rows, w), jnp.bfloat16),
                   jax.ShapeDtypeStruct((rows, w), jnp.float32)],
        compiler_params=pltpu.CompilerParams(dimension_semantics=("parallel",),
                                             vmem_limit_bytes=VMEM_LIMIT_BYTES),
        name="na_prep",
    )(p, p, g1(q_g), g1(k_g))


def na_bias_tables(rpb, rows):
    nblk = rows // NA_Q_ROWS
    cq = jnp.arange(GRID_W)[:, None]
    ck = jnp.arange(GRID_W)[None, :]
    col_start = jnp.clip(cq - NA_KW // 2, 0, GRID_W - NA_KW)
    col_ok = (ck >= col_start) & (ck < col_start + NA_KW)
    cidx = jnp.clip(ck - cq + NA_KW - 1, 0, 2 * NA_KW - 2)
    tabs = []
    for i in (0, 1, nblk - 1):
        base = min(max(NA_Q_ROWS * i - NA_KH // 2, 0), rows - NA_WIN_ROWS)
        r = NA_Q_ROWS * i + jnp.arange(NA_Q_ROWS)[:, None]
        rk = base + jnp.arange(NA_WIN_ROWS)[None, :]
        r0 = jnp.clip(r - NA_KH // 2, 0, rows - NA_KH)
        row_ok = (rk >= r0) & (rk < r0 + NA_KH)
        ridx = jnp.clip(rk - r + NA_KH - 1, 0, 2 * NA_KH - 2)
        b = rpb[:, ridx][:, :, :, cidx]
        ok = row_ok[:, :, None, None] & col_ok[None, None, :, :]
        b = jnp.where(ok[None], b.astype(jnp.float32), NEG_INF)
        tabs.append(jnp.transpose(b, (0, 1, 3, 2, 4)).reshape(NA_HEADS, NA_Q_ROWS * GRID_W, NA_WIN_ROWS * GRID_W))
    return jnp.stack(tabs, axis=0)


def _na_attn_kernel(*refs, windowed, rows):
    nt = (((1,), (1,)), ((), ()))
    scale = HEAD_DIM ** -0.5
    if windowed:
        q_ref, k_ref, v_ref, b_ref, ck_ref, cv_ref, o_ref = refs
        i = pl.program_id(2)
        base = jnp.clip(NA_Q_ROWS * i - NA_KH // 2, 0, rows - NA_WIN_ROWS)
        k0 = pl.multiple_of(base * GRID_W, GRID_W)
        nk = NA_WIN_ROWS * GRID_W
        q = q_ref[...]
        s = lax.dot_general(q, k_ref[pl.ds(k0, nk), :], nt, preferred_element_type=jnp.float32) * scale + b_ref[0, 0]
        sc = lax.dot_general(q, ck_ref[...].astype(jnp.bfloat16), nt, preferred_element_type=jnp.float32) * scale
        m = jnp.maximum(jnp.max(s, axis=-1, keepdims=True), jnp.max(sc, axis=-1, keepdims=True))
        p = jnp.exp(s - m)
        pc = jnp.exp(sc - m)
        l = jnp.sum(p, axis=-1, keepdims=True) + jnp.sum(pc, axis=-1, keepdims=True)
        o = (jnp.dot(p.astype(jnp.bfloat16), v_ref[pl.ds(k0, nk), :].astype(jnp.bfloat16),
                     preferred_element_type=jnp.float32)
             + jnp.dot(pc.astype(jnp.bfloat16), cv_ref[...].astype(jnp.bfloat16), preferred_element_type=jnp.float32))
    else:
        q_ref, k_ref, v_ref, o_ref = refs
        s = lax.dot_general(q_ref[...], k_ref[...], nt, preferred_element_type=jnp.float32) * scale
        m = jnp.max(s, axis=-1, keepdims=True)
        p = jnp.exp(s - m)
        l = jnp.sum(p, axis=-1, keepdims=True)
        o = jnp.dot(p.astype(jnp.bfloat16), v_ref[...].astype(jnp.bfloat16), preferred_element_type=jnp.float32)
    o_ref[...] = o / l


def na_attention(q_bf16, k_bf16, p, bias_tabs, ctx_k, ctx_v, batch, seq_len):
    windowed = bias_tabs is not None
    v_col0 = 2 * NA_HEADS
    rows = seq_len // GRID_W
    if windowed:
        tq = NA_Q_ROWS * GRID_W
        nq = seq_len // tq
        lc = ctx_k.shape[0] // batch
        nk = NA_WIN_ROWS * GRID_W
        in_specs = [pl.BlockSpec((tq, LANES), lambda b, h, i: (b * nq + i, h)),
                    pl.BlockSpec((seq_len, LANES), lambda b, h, i: (b, h)),
                    pl.BlockSpec((seq_len, LANES), lambda b, h, i: (b, v_col0 + h)),
                    pl.BlockSpec((1, 1, tq, nk),
                                 lambda b, h, i: (jnp.where(i == 0, 0, jnp.where(i == nq - 1, 2, 1)), h, 0, 0)),
                    pl.BlockSpec((lc, LANES), lambda b, h, i: (b, h)),
                    pl.BlockSpec((lc, LANES), lambda b, h, i: (b, h))]
        args = [q_bf16, k_bf16, p, bias_tabs, ctx_k, ctx_v]
    else:
        tq = seq_len
        nq = 1
        in_specs = [pl.BlockSpec((tq, LANES), lambda b, h, i: (b, h)),
                    pl.BlockSpec((seq_len, LANES), lambda b, h, i: (b, h)),
                    pl.BlockSpec((seq_len, LANES), lambda b, h, i: (b, v_col0 + h))]
        args = [q_bf16, k_bf16, p]
    return pl.pallas_call(
        functools.partial(_na_attn_kernel, windowed=windowed, rows=rows),
        grid=(batch, NA_HEADS, nq),
        in_specs=in_specs,
        out_specs=pl.BlockSpec((tq, LANES), lambda b, h, i: (b * nq + i, h)),
        out_shape=jax.ShapeDtypeStruct((batch * seq_len, NA_HEADS * LANES), jnp.float32),
        compiler_params=pltpu.CompilerParams(dimension_semantics=("parallel", "parallel", "arbitrary"),
                                             vmem_limit_bytes=VMEM_LIMIT_BYTES),
        name="na_attention",
    )(*args)


def even_mixer(x, mods, norm_g, b, l, ev, ctx_kv, bias_tabs):
    (w_in, w_out, q_g, k_g, conv_w, conv_b, w1, b1, f1, w2, b2, f2, w3, decay, bias) = ev
    p = mm_norm(x, norm_g, mods[0], mods[1], w_in)
    q, k, kn = na_prep(p, q_g, k_g)
    if ctx_kv is None:
        oa = na_attention(q, k, p, None, None, None, b, l)
    else:
        lc = ctx_kv[0].shape[1]
        oa = na_attention(q, k, p, bias_tabs, ctx_kv[0].reshape(b * lc, MIX_WIDTH),
                          ctx_kv[1].reshape(b * lc, MIX_WIDTH), b, l)
    hb = p[:, 3 * MIX_WIDTH:].reshape(b, l, 3 * HY_WIDTH)
    ob = hyena(hb, conv_w, conv_b, (w1, b1, f1, w2, b2, f2, w3, decay), bias)
    out = mm_pair_res(oa, ob.reshape(b * l, HY_WIDTH), w_out, x, mods[2])
    shp = (b, l, NA_HEADS, HEAD_DIM)
    return out, kn.reshape(shp), p[:, 2 * MIX_WIDTH:3 * MIX_WIDTH].reshape(shp)


def odd_mixer(x, mods, norm_g, b, l, od, lam_init, h0_re, h0_im, ctx_kv, rope):
    (w_in, w_out, s5_prep, d_skip, glu_w, glu_b, q_g, k_g, lq1, lk1, lq2, lk2, subln_g) = od
    p = mm_norm(x, norm_g, mods[0], mods[1], w_in)
    oc, fr, fi = s5_mixer(p[:, :S5_WIDTH].reshape(b, l, S5_WIDTH), h0_re, h0_im, s5_prep, d_skip, glu_w, glu_b)
    lam = (jnp.exp(jnp.sum(lq1 * lk1).astype(jnp.float32))
           - jnp.exp(jnp.sum(lq2 * lk2).astype(jnp.float32)) + lam_init)
    cos, sin_signed = rope
    q, k, kn = diff_prep(p, q_g, k_g, cos, sin_signed, l, use_rope=ctx_kv is not None)
    gain = subln_g.astype(jnp.float32) * (1.0 - lam_init)
    if ctx_kv is None:
        o = diff_attention(q, k, p, None, None, lam, gain, b, l)
    else:
        lc = ctx_kv[0].shape[1]
        o = diff_attention(q, k, p, ctx_kv[0].reshape(b * lc, MIX_WIDTH), ctx_kv[1].reshape(b * lc, MIX_WIDTH),
                           lam, gain, b, l)
    out = mm_pair_res(oc.reshape(b * l, S5_WIDTH), o, w_out, x, mods[2])
    kn = kn.reshape(b, l, DIFF_HEADS, 2, DIFF_QK_DIM)
    v = p[:, S5_WIDTH + 2 * MIX_WIDTH:].reshape(b, l, DIFF_HEADS, DIFF_V_DIM)
    return out, kn, v, fr, fi


PEER_TOKEN_TILE = 512
PEER_EXPERT_TILE = 1024
PEER_SUB_EXPERTS = 256


def _top16_rows(cur, iota):
    n = cur.shape[0]
    tops, idxs = [], []
    for _ in range(PEER_TOPK):
        m = jnp.max(cur, axis=0, keepdims=True)
        first = jnp.min(jnp.where(cur == m, iota, n), axis=0, keepdims=True)
        tops.append(m)
        idxs.append(first)
        cur = jnp.where(iota == first, -jnp.inf, cur)
    return tops, idxs


def _router_kernel(q_ref, keys_ref, m_ref, r1_ref, e0_ref, e1_ref, st_ref):
    tt = q_ref.shape[0]
    half = PEER_QUERY_DIM // 2
    for h in range(PEER_HEADS):
        for s in range(2):
            qs = q_ref[:, (2 * h + s) * half:(2 * h + s + 1) * half].astype(jnp.bfloat16)
            st = lax.dot_general(keys_ref[h, s], qs, (((1,), (1,)), ((), ())),
                                 preferred_element_type=jnp.float32)
            st_ref[2 * h + s] = st

    iota128 = lax.broadcasted_iota(jnp.int32, (PEER_N_KEYS, LANES), 0)
    iota8 = lax.broadcasted_iota(jnp.int32, (SUBLANES, LANES), 0)
    iota_cand = lax.broadcasted_iota(jnp.int32, (PEER_TOPK + SUBLANES * SUBLANES, LANES), 0)
    n_chunks = tt // LANES

    def body(i, carry):
        h = i // n_chunks
        lane0 = pl.multiple_of((i % n_chunks) * LANES, LANES)
        s0 = st_ref[2 * h, :, pl.ds(lane0, LANES)]
        s1 = st_ref[2 * h + 1, :, pl.ds(lane0, LANES)]
        a, a_idx = _top16_rows(s0, iota128)
        b, b_idx = _top16_rows(s1, iota128)
        bmat = jnp.concatenate(b, axis=0)
        sums = [a[k] + bmat for k in range(PEER_TOPK)]
        cand = [sums[0], sums[1][:SUBLANES]]
        for k in range(2, SUBLANES):
            cand.append(jnp.where(iota8 < PEER_TOPK // (k + 1), sums[k][:SUBLANES], -jnp.inf))
        cand.append(jnp.concatenate(a[SUBLANES:], axis=0) + b[0])
        f, _ = _top16_rows(jnp.concatenate(cand, axis=0), iota_cand)
        tau = f[PEER_TOPK - 1]
        z = jnp.ones_like(f[0])
        for k in range(1, PEER_TOPK):
            z = z + jnp.exp(f[k] - f[0])
        m = jnp.zeros((PEER_N_KEYS, LANES), jnp.float32)
        r1 = jnp.full((PEER_N_KEYS, LANES), float(PEER_TOPK), jnp.float32)
        for k in range(PEER_TOPK):
            m_k = jnp.sum(jnp.where(sums[k] >= tau, 1.0, 0.0), axis=0, keepdims=True)
            m = jnp.where(iota128 == a_idx[k], m_k, m)
            r1 = jnp.where(iota128 == b_idx[k], float(k), r1)
        m_ref[h, :, pl.ds(lane0, LANES)] = m
        r1_ref[h, :, pl.ds(lane0, LANES)] = r1.astype(jnp.bfloat16)
        e0_ref[h, :, pl.ds(lane0, LANES)] = jnp.exp(s0 - a[0]) / z
        e1_ref[h, :, pl.ds(lane0, LANES)] = jnp.exp(s1 - b[0]).astype(jnp.bfloat16)
        return carry

    lax.fori_loop(0, PEER_HEADS * n_chunks, body, 0)


def peer_router(q, keys_bf16, tt):
    t = q.shape[0]
    f32 = jnp.float32
    big = jax.ShapeDtypeStruct((PEER_HEADS, PEER_N_KEYS, t), f32)
    half = jax.ShapeDtypeStruct((PEER_HEADS, PEER_N_KEYS, t), jnp.bfloat16)
    blk = pl.BlockSpec((PEER_HEADS, PEER_N_KEYS, tt), lambda i: (0, 0, i))
    return pl.pallas_call(
        _router_kernel,
        grid=(t // tt,),
        in_specs=[pl.BlockSpec((tt, q.shape[1]), lambda i: (i, 0)),
                  pl.BlockSpec(keys_bf16.shape, lambda i: (0, 0, 0, 0))],
        out_specs=[blk, blk, blk, blk],
        out_shape=[big, half, big, half],
        scratch_shapes=[pltpu.VMEM((2 * PEER_HEADS, PEER_N_KEYS, tt), f32)],
        compiler_params=pltpu.CompilerParams(dimension_semantics=("parallel",),
                                             vmem_limit_bytes=VMEM_LIMIT_BYTES),
        name="peer_router",
    )(q, keys_bf16)


def _dense_kernel(x_ref, u_ref, vt_ref, m_ref, r1_ref, e0_ref, e1_ref, res_ref, gate_ref,
                  o_ref, at_ref, wt_ref, acc_ref):
    e = pl.program_id(1)
    n_e = pl.num_programs(1)
    eb = u_ref.shape[0]
    tt = x_ref.shape[0]
    n_chunks = tt // LANES
    n_sub = eb // PEER_SUB_EXPERTS
    i1_per_sub = PEER_SUB_EXPERTS // PEER_N_KEYS
    nt = (((1,), (1,)), ((), ()))

    @pl.when(e == 0)
    def _():
        acc_ref[...] = jnp.zeros_like(acc_ref)

    x = x_ref[...]
    for sb in range(n_sub):
        rows = slice(sb * PEER_SUB_EXPERTS, (sb + 1) * PEER_SUB_EXPERTS)
        at_ref[rows, :] = lax.dot_general(u_ref[rows, :], x, nt, preferred_element_type=jnp.float32)
    for sb in range(n_sub):
        for c in range(n_chunks):
            lanes = slice(c * LANES, (c + 1) * LANES)
            for k in range(i1_per_sub):
                i1 = sb * i1_per_sub + k
                zero = jnp.zeros((PEER_N_KEYS, LANES), jnp.bfloat16)
                g = zero
                for h in range(PEER_HEADS):
                    mrow = m_ref[h, i1:i1 + 1, lanes].astype(jnp.bfloat16)
                    e0row = e0_ref[h, i1:i1 + 1, lanes].astype(jnp.bfloat16)
                    g = g + jnp.where(r1_ref[h, :, lanes] < mrow, e0row * e1_ref[h, :, lanes], zero)
                r = slice(i1 * PEER_N_KEYS, (i1 + 1) * PEER_N_KEYS)
                wt_ref[r, lanes] = _gelu_tanh(at_ref[r, lanes]).astype(jnp.bfloat16) * g
    acc_ref[...] += jnp.dot(vt_ref[...], wt_ref[...], preferred_element_type=jnp.float32)

    @pl.when(e == n_e - 1)
    def _():
        o_ref[...] = res_ref[...] + gate_ref[0] * acc_ref[...].T


def peer_dense(x_bf16, u_bf16, vt_bf16, m, r1, e0, e1, resid, gate, tokens_per_gate, tt, eb):
    t, d = x_bf16.shape
    n_exp = u_bf16.shape[0]
    assert t % tt == 0 and n_exp % eb == 0 and tokens_per_gate % tt == 0
    assert eb // PEER_N_KEYS == SUBLANES
    tiles_per_gate = tokens_per_gate // tt
    rblk = pl.BlockSpec((PEER_HEADS, PEER_N_KEYS, tt), lambda i, j: (0, 0, i))
    gblk = pl.BlockSpec((PEER_HEADS, SUBLANES, tt), lambda i, j: (0, j, i))
    return pl.pallas_call(
        _dense_kernel,
        grid=(t // tt, n_exp // eb),
        in_specs=[pl.BlockSpec((tt, d), lambda i, j: (i, 0)),
                  pl.BlockSpec((eb, d), lambda i, j: (j, 0)),
                  pl.BlockSpec((d, eb), lambda i, j: (0, j)),
                  gblk, rblk, gblk, rblk,
                  pl.BlockSpec((tt, d), lambda i, j: (i, 0)),
                  pl.BlockSpec((1, 1, d), lambda i, j: (i // tiles_per_gate, 0, 0))],
        out_specs=pl.BlockSpec((tt, d), lambda i, j: (i, 0)),
        out_shape=jax.ShapeDtypeStruct((t, d), jnp.float32),
        scratch_shapes=[pltpu.VMEM((eb, tt), jnp.float32),
                        pltpu.VMEM((eb, tt), jnp.bfloat16),
                        pltpu.VMEM((d, tt), jnp.float32)],
        compiler_params=pltpu.CompilerParams(dimension_semantics=("parallel", "arbitrary"),
                                             vmem_limit_bytes=VMEM_LIMIT_BYTES),
        name="peer_dense",
    )(x_bf16, u_bf16, vt_bf16, m, r1, e0, e1, resid, gate)


def peer_layer(x, norm_g, mods, w_q, keys, u_bf16, vt_bf16):
    t, d = x.shape
    q, h = mm_norm(x, norm_g, mods[3], mods[4], w_q, emit_h=True)
    gate = mods[5]
    m, r1, e0, e1 = peer_router(q, keys.astype(jnp.bfloat16), PEER_TOKEN_TILE)
    return peer_dense(h, u_bf16, vt_bf16, m, r1, e0, e1, x, gate, t // gate.shape[0],
                      PEER_TOKEN_TILE, PEER_EXPERT_TILE)


def kernel(x_prompt, x_sample, c, cache_na_k, cache_na_v, cache_diff_k, cache_diff_v, state_s5_re, state_s5_im, c_ctx, mod_w, mod_b, norm_mix_g, norm_ffn_g, ev_w_in, ev_w_out, na_q_g, na_k_g, na_rpb, hy_conv_w, hy_conv_b, hy_w1, hy_b1, hy_f1, hy_w2, hy_b2, hy_f2, hy_w3, hy_decay, hy_bias, od_w_in, od_w_out, s5_lam_re, s5_lam_im, s5_log_dt, s5_b_re, s5_b_im, s5_c_re, s5_c_im, s5_d, s5_glu_w, s5_glu_b, diff_q_g, diff_k_g, diff_lq1, diff_lk1, diff_lq2, diff_lk2, diff_subln_g, peer_w_q, peer_keys, peer_u, peer_v):
    bp, lp, d = x_prompt.shape
    bs, ls, _ = x_sample.shape
    rope_p, rope_s = rope_tables(lp), rope_tables(ls)
    xp, xs = x_prompt.reshape(bp * lp, d), x_sample.reshape(bs * ls, d)
    new_na_k, new_na_v, new_dk, new_dv, new_sr, new_si = [], [], [], [], [], []
    for i in range(DEPTH):
        mp = adaln(c_ctx[None, :], mod_w[i], mod_b[i])
        ms = adaln(c, mod_w[i], mod_b[i])
        j = i // 2
        if i % 2 == 0:
            ev = (ev_w_in[j], ev_w_out[j], na_q_g[j], na_k_g[j], hy_conv_w[j], hy_conv_b[j],
                  hy_w1[j], hy_b1[j], hy_f1[j], hy_w2[j], hy_b2[j], hy_f2[j], hy_w3[j], hy_decay[j], hy_bias[j])
            bias_tabs = na_bias_tables(na_rpb[j], ls // GRID_W)
            xp, kp, vp = even_mixer(xp, mp, norm_mix_g[i], bp, lp, ev, None, None)
            xs, _, _ = even_mixer(xs, ms, norm_mix_g[i], bs, ls, ev, (cache_na_k[:, j], cache_na_v[:, j]), bias_tabs)
            new_na_k.append(kp)
            new_na_v.append(vp)
        else:
            lam_init = 0.8 - 0.6 * math.exp(-0.3 * i)
            s5_prep = s5_prepare(s5_lam_re[j], s5_lam_im[j], s5_log_dt[j], s5_b_re[j], s5_b_im[j],
                                 s5_c_re[j], s5_c_im[j])
            od = (od_w_in[j], od_w_out[j], s5_prep, s5_d[j], s5_glu_w[j], s5_glu_b[j], diff_q_g[j], diff_k_g[j],
                  diff_lq1[j], diff_lk1[j], diff_lq2[j], diff_lk2[j], diff_subln_g[j])
            xp, kp, vp, sr, si = odd_mixer(xp, mp, norm_mix_g[i], bp, lp, od, lam_init, None, None, None, rope_p)
            xs, _, _, _, _ = odd_mixer(xs, ms, norm_mix_g[i], bs, ls, od, lam_init, state_s5_re[:, j],
                                       state_s5_im[:, j], (cache_diff_k[:, j], cache_diff_v[:, j]), rope_s)
            new_dk.append(kp)
            new_dv.append(vp)
            new_sr.append(sr)
            new_si.append(si)
        u_bf16 = peer_u[i].astype(jnp.bfloat16)
        vt_bf16 = peer_v[i].T.astype(jnp.bfloat16)
        xp = peer_layer(xp, norm_ffn_g[i], mp, peer_w_q[i], peer_keys[i], u_bf16, vt_bf16)
        xs = peer_layer(xs, norm_ffn_g[i], ms, peer_w_q[i], peer_keys[i], u_bf16, vt_bf16)
    return (xp.reshape(bp, lp, d), xs.reshape(bs, ls, d), jnp.stack(new_na_k, axis=1), jnp.stack(new_na_v, axis=1),
            jnp.stack(new_dk, axis=1), jnp.stack(new_dv, axis=1),
            jnp.stack(new_sr, axis=1), jnp.stack(new_si, axis=1))
```

```python
import functools
import math

import jax
import jax.numpy as jnp
from jax import lax
from jax.experimental import pallas as pl
from jax.experimental.pallas import tpu as pltpu

D_MODEL = 2048
DEPTH = 2
GRID_W = 64
MIX_WIDTH = D_MODEL // 2
HEAD_DIM = 128
NA_HEADS = MIX_WIDTH // HEAD_DIM
NA_KH = 8
NA_KW = 16
HY_WIDTH = MIX_WIDTH
HY_ORDER = 2
HY_POS_EMB = 33
S5_WIDTH = MIX_WIDTH
S5_GROUP = 16
S5_GROUPS = S5_WIDTH // S5_GROUP
S5_STATE = 64
DIFF_HEADS = MIX_WIDTH // HEAD_DIM
DIFF_QK_DIM = HEAD_DIM // 2
DIFF_V_DIM = HEAD_DIM
ROPE_BASE = 10000.0
PEER_HEADS = 8
PEER_N_KEYS = 128
PEER_QUERY_DIM = 256
PEER_TOPK = 16
NORM_EPS = 1e-6
NEG_INF = -1e30

LANES = 128
SUBLANES = 8
VMEM_LIMIT_BYTES = 56 * 1024 * 1024


MM_ROW_TILE = 1024
MM_COL_TILE = 512


def _mm_norm_kernel(x_ref, g_ref, sc_ref, sh_ref, w_ref, *out_refs, emit_h):
    if emit_h:
        o_ref, h_ref, hs_ref = out_refs
    else:
        o_ref, hs_ref = out_refs

    @pl.when(pl.program_id(1) == 0)
    def _():
        x = x_ref[...]
        y = x * lax.rsqrt(jnp.mean(x * x, axis=-1, keepdims=True) + NORM_EPS) * g_ref[...]
        hs_ref[...] = (y * (1.0 + sc_ref[0]) + sh_ref[0]).astype(jnp.bfloat16)
        if emit_h:
            h_ref[...] = hs_ref[...]

    o_ref[...] = jnp.dot(hs_ref[...], w_ref[...].astype(jnp.bfloat16), preferred_element_type=jnp.float32)


def mm_norm(x, norm_g, shift, scale, w, emit_h=False):
    m, k = x.shape
    n = w.shape[1]
    tm, tn = min(MM_ROW_TILE, m), min(MM_COL_TILE, n)
    nb = scale.shape[0]
    assert m % tm == 0 and n % tn == 0 and (m // nb) % tm == 0
    tiles_per_mod = (m // nb) // tm
    mod = pl.BlockSpec((1, 1, k), lambda i, j: (i // tiles_per_mod, 0, 0))
    out_specs = [pl.BlockSpec((tm, tn), lambda i, j: (i, j))]
    out_shape = [jax.ShapeDtypeStruct((m, n), jnp.float32)]
    if emit_h:
        out_specs.append(pl.BlockSpec((tm, k), lambda i, j: (i, 0)))
        out_shape.append(jax.ShapeDtypeStruct((m, k), jnp.bfloat16))
    res = pl.pallas_call(
        functools.partial(_mm_norm_kernel, emit_h=emit_h),
        grid=(m // tm, n // tn),
        in_specs=[pl.BlockSpec((tm, k), lambda i, j: (i, 0)),
                  pl.BlockSpec((1, k), lambda i, j: (0, 0)),
                  mod, mod,
                  pl.BlockSpec((k, tn), lambda i, j: (0, j))],
        out_specs=out_specs,
        out_shape=out_shape,
        scratch_shapes=[pltpu.VMEM((tm, k), jnp.bfloat16)],
        compiler_params=pltpu.CompilerParams(dimension_semantics=("parallel", "arbitrary"),
                                             vmem_limit_bytes=VMEM_LIMIT_BYTES),
        name="mm_norm",
    )(x, norm_g.astype(jnp.float32).reshape(1, k), scale, shift, w)
    return res if emit_h else res[0]


def _mm_pair_res_kernel(a_ref, b_ref, wa_ref, wb_ref, r_ref, g_ref, o_ref):
    y = (jnp.dot(a_ref[...].astype(jnp.bfloat16), wa_ref[...].astype(jnp.bfloat16), preferred_element_type=jnp.float32)
         + jnp.dot(b_ref[...].astype(jnp.bfloat16), wb_ref[...].astype(jnp.bfloat16), preferred_element_type=jnp.float32))
    o_ref[...] = r_ref[...] + g_ref[0] * y


def mm_pair_res(a, b, w, resid, gate):
    m, kh = a.shape
    n = w.shape[1]
    tm, tn = min(MM_ROW_TILE, m), min(MM_COL_TILE, n)
    nb = gate.shape[0]
    assert m % tm == 0 and n % tn == 0 and (m // nb) % tm == 0 and w.shape[0] == 2 * kh
    tiles_per_mod = (m // nb) // tm
    return pl.pallas_call(
        _mm_pair_res_kernel,
        grid=(m // tm, n // tn),
        in_specs=[pl.BlockSpec((tm, kh), lambda i, j: (i, 0)),
                  pl.BlockSpec((tm, kh), lambda i, j: (i, 0)),
                  pl.BlockSpec((kh, tn), lambda i, j: (0, j)),
                  pl.BlockSpec((kh, tn), lambda i, j: (1, j)),
                  pl.BlockSpec((tm, tn), lambda i, j: (i, j)),
                  pl.BlockSpec((1, 1, tn), lambda i, j: (i // tiles_per_mod, 0, j))],
        out_specs=pl.BlockSpec((tm, tn), lambda i, j: (i, j)),
        out_shape=jax.ShapeDtypeStruct((m, n), jnp.float32),
        compiler_params=pltpu.CompilerParams(dimension_semantics=("parallel", "arbitrary"),
                                             vmem_limit_bytes=VMEM_LIMIT_BYTES),
        name="mm_pair_res",
    )(a, b, w, w, resid, gate)


def adaln(cond, w, b):
    m = jax.nn.silu(cond) @ w + b
    return jnp.split(m[:, None, :], 6, axis=-1)


def hyena_filters(l, w1, b1, f1, w2, b2, f2, w3, decay):
    t = jnp.linspace(0.0, 1.0, l, dtype=jnp.float32)[:, None]
    bands = (HY_POS_EMB - 1) // 2
    w_ang = 2.0 * math.pi * jnp.arange(l, dtype=jnp.float32)[:, None] / l
    freqs = jnp.linspace(1e-4, bands - 1, bands, dtype=jnp.float32)[None, :]
    z = jnp.concatenate([t, jnp.cos(freqs * w_ang), -jnp.sin(freqs * w_ang)], axis=-1)
    h = jnp.sin(f1 * (z @ w1 + b1))
    h = jnp.sin(f2 * (h @ w2 + b2))
    h = (h @ w3).reshape(l, 2, HY_ORDER, HY_WIDTH).astype(jnp.float32)
    h = h * jnp.exp(-t.reshape(l, 1, 1, 1) * jnp.abs(decay.astype(jnp.float32)))
    h_f, h_b = h[:, 0], h[:, 1]
    zero = jnp.zeros((1, HY_ORDER, HY_WIDTH), jnp.float32)
    return jnp.concatenate([h_f, zero, h_b[1:][::-1]], axis=0)


def _split_bf16(x):
    hi = x.astype(jnp.bfloat16)
    lo = (x - hi.astype(jnp.float32)).astype(jnp.bfloat16)
    return hi, lo


def _dot3(a_hi, a_lo, x):
    m = a_hi.shape[0]
    xh, xl = _split_bf16(x)
    r = jnp.dot(jnp.concatenate([a_hi, a_lo], axis=0), xh, preferred_element_type=jnp.float32)
    return r[:m] + r[m:] + jnp.dot(a_hi, xl, preferred_element_type=jnp.float32)


HY_STEP_ROWS = 256


def hyena_factors(l):
    n = 2 * l
    n2 = 64 if n >= 8192 else 16
    return n // n2, n2


def hyena_tables(l):
    n = 2 * l
    n1, n2 = hyena_factors(l)
    t = (n2 * jnp.arange(n1 // 2)[None, None, :] + jnp.arange(n2)[:, None, None])
    k1 = jnp.arange(n1)[None, :, None]
    ang = (2.0 * math.pi / n) * ((t * k1) % n).astype(jnp.float32)
    fa = jnp.concatenate([jnp.cos(ang), -jnp.sin(ang)], axis=1)
    fc = jnp.swapaxes(fa, 1, 2) / n
    a2 = (2.0 * math.pi / n2) * ((jnp.arange(n2)[:, None] * jnp.arange(n2)[None, :]) % n2).astype(jnp.float32)
    c, s = jnp.cos(a2), jnp.sin(a2)
    fb = jnp.concatenate([jnp.concatenate([c, s], axis=1), jnp.concatenate([-s, c], axis=1)], axis=0)
    fbi = jnp.concatenate([jnp.concatenate([c, -s], axis=1), jnp.concatenate([s, c], axis=1)], axis=0)
    return tuple(_split_bf16(x) for x in (fa, fc, fb, fbi))


def hyena_spectrum(filt, l):
    n1, n2 = hyena_factors(l)
    kf = jnp.fft.fft(filt, axis=0)
    kf = jnp.stack([jnp.real(kf), jnp.imag(kf)], axis=0).astype(jnp.float32)
    kf = kf.reshape(2, n2, n1, HY_ORDER, HY_WIDTH)
    return jnp.transpose(kf, (3, 0, 2, 1, 4))


def _hy_stage_a_kernel(z_ref, fh_ref, fl_ref, o_ref):
    for i in range(z_ref.shape[0]):
        o_ref[i] = _dot3(fh_ref[0], fl_ref[0], z_ref[i])


def _hy_stage_b_kernel(s_ref, k_ref, fh_ref, fl_ref, gh_ref, gl_ref, o_ref):
    n2 = s_ref.shape[3]
    kr, ki = k_ref[0, 0], k_ref[1, 0]
    for i in range(s_ref.shape[0]):
        y = jnp.concatenate([s_ref[i, 0, 0], s_ref[i, 1, 0]], axis=0)
        z = _dot3(fh_ref[...], fl_ref[...], y)
        zr, zi = z[:n2], z[n2:]
        p = jnp.concatenate([zr * kr - zi * ki, zr * ki + zi * kr], axis=0)
        q = _dot3(gh_ref[...], gl_ref[...], p)
        o_ref[i, 0, 0] = q[:n2]
        o_ref[i, 1, 0] = q[n2:]


def _hy_stage_c_kernel(q_ref, fh_ref, fl_ref, z_ref, g_ref, b_ref, o_ref):
    for i in range(q_ref.shape[0]):
        y = _dot3(fh_ref[0], fl_ref[0], q_ref[i])
        o_ref[i] = g_ref[i] * (y + z_ref[i] * b_ref[...])


def hyena_long_conv(z, z_col, gate, gate_col, bias_o, spec_o, tables, l):
    (fah, fal), (fch, fcl), (fbh, fbl), (fgh, fgl) = tables
    b = z.shape[0]
    c = HY_WIDTH
    n1, n2 = hyena_factors(l)
    h1 = n1 // 2
    bb = max(1, min(b, HY_STEP_ROWS // n1))
    assert b % bb == 0
    cp = pltpu.CompilerParams(dimension_semantics=("parallel", "parallel"), vmem_limit_bytes=VMEM_LIMIT_BYTES)
    zblocks = z.shape[-1] // c
    gblocks = gate.shape[-1] // c
    zv = z.reshape(b, h1, n2 * z.shape[-1])
    gv = gate.reshape(b, h1, n2 * gate.shape[-1])
    s1 = pl.pallas_call(
        _hy_stage_a_kernel,
        grid=(b // bb, n2),
        in_specs=[pl.BlockSpec((bb, h1, c), lambda i, j: (i, 0, j * zblocks + z_col)),
                  pl.BlockSpec((1, 2 * n1, h1), lambda i, j: (j, 0, 0)),
                  pl.BlockSpec((1, 2 * n1, h1), lambda i, j: (j, 0, 0))],
        out_specs=pl.BlockSpec((bb, 2 * n1, c), lambda i, j: (i, 0, j)),
        out_shape=jax.ShapeDtypeStruct((b, 2 * n1, n2 * c), jnp.float32),
        compiler_params=cp, name="hyena_stage_a",
    )(zv, fah, fal)
    s1 = s1.reshape(b, 2, n1, n2, c)
    mat = pl.BlockSpec((2 * n2, 2 * n2), lambda k, i: (0, 0))
    q = pl.pallas_call(
        _hy_stage_b_kernel,
        grid=(n1, b // bb),
        in_specs=[pl.BlockSpec((bb, 2, 1, n2, c), lambda k, i: (i, 0, k, 0, 0)),
                  pl.BlockSpec((2, 1, n2, c), lambda k, i: (0, k, 0, 0)),
                  mat, mat, mat, mat],
        out_specs=pl.BlockSpec((bb, 2, 1, n2, c), lambda k, i: (i, 0, k, 0, 0)),
        out_shape=jax.ShapeDtypeStruct((b, 2, n1, n2, c), jnp.float32),
        compiler_params=cp, name="hyena_stage_b",
    )(s1, spec_o, fbh, fbl, fgh, fgl)
    q = q.reshape(b, 2 * n1, n2 * c)
    out = pl.pallas_call(
        _hy_stage_c_kernel,
        grid=(b // bb, n2),
        in_specs=[pl.BlockSpec((bb, 2 * n1, c), lambda i, j: (i, 0, j)),
                  pl.BlockSpec((1, h1, 2 * n1), lambda i, j: (j, 0, 0)),
                  pl.BlockSpec((1, h1, 2 * n1), lambda i, j: (j, 0, 0)),
                  pl.BlockSpec((bb, h1, c), lambda i, j: (i, 0, j * zblocks + z_col)),
                  pl.BlockSpec((bb, h1, c), lambda i, j: (i, 0, j * gblocks + gate_col)),
                  pl.BlockSpec((1, c), lambda i, j: (0, 0))],
        out_specs=pl.BlockSpec((bb, h1, c), lambda i, j: (i, 0, j)),
        out_shape=jax.ShapeDtypeStruct((b, h1, n2 * c), jnp.float32),
        compiler_params=cp, name="hyena_stage_c",
    )(q, fch, fcl, zv, gv, bias_o.astype(jnp.float32).reshape(1, c))
    return out.reshape(b, l, c)


def hyena(u, conv_w, conv_b, filt, bias):
    b, l, _ = u.shape
    up = jnp.pad(u, ((0, 0), (1, 1), (0, 0)))
    u = up[:, :-2] * conv_w[0] + up[:, 1:-1] * conv_w[1] + up[:, 2:] * conv_w[2] + conv_b
    tables = hyena_tables(l)
    spec = hyena_spectrum(hyena_filters(l, *filt), l)
    z, z_col = u, 0
    for o in range(HY_ORDER):
        z = hyena_long_conv(z, z_col, u, 1 + o, bias[o], spec[o], tables, l)
        z_col = 0
    return z


S5_TILE_GROUPS = 8
S5_TILES = S5_GROUPS // S5_TILE_GROUPS
S5_TILE_IN = S5_TILE_GROUPS * S5_GROUP
S5_TILE_STATE = S5_TILE_GROUPS * S5_STATE
S5_ROWS_PER_STEP = 2048


def s5_prepare(lam_re, lam_im, log_dt, b_re, b_im, c_re, c_im):
    f32 = jnp.float32
    lam_re, lam_im = lam_re.astype(f32), lam_im.astype(f32)
    dt = jnp.exp(log_dt.astype(f32))[..., None]
    mag = jnp.exp(lam_re * dt)
    ar, ai = mag * jnp.cos(lam_im * dt), mag * jnp.sin(lam_im * dt)
    den = lam_re * lam_re + lam_im * lam_im
    cr = ((ar - 1.0) * lam_re + ai * lam_im) / den
    ci = (ai * lam_re - (ar - 1.0) * lam_im) / den
    bbr = cr[..., None] * b_re - ci[..., None] * b_im
    bbi = cr[..., None] * b_im + ci[..., None] * b_re
    eye = jnp.eye(S5_TILE_GROUPS, dtype=f32)

    def tile_in(bb):
        x = bb.reshape(2, S5_TILES, S5_TILE_GROUPS, S5_STATE, S5_GROUP)
        x = jnp.einsum('dtgnp,gh->dtgphn', x, eye)
        return x.reshape(2, S5_TILES, S5_TILE_IN, S5_TILE_STATE)

    def tile_out(cc):
        x = cc.astype(f32).reshape(2, S5_TILES, S5_TILE_GROUPS, S5_GROUP, S5_STATE)
        x = jnp.einsum('dtgpn,gh->dtgnhp', x, eye)
        return x.reshape(2, S5_TILES, S5_TILE_STATE, S5_TILE_IN)

    win = jnp.concatenate([tile_in(bbr), tile_in(bbi)], axis=-1).astype(jnp.bfloat16)
    wout = jnp.concatenate([tile_out(c_re), -tile_out(c_im)], axis=-2).astype(jnp.bfloat16)
    a = jnp.stack([ar.reshape(2, S5_TILES, S5_TILE_STATE), ai.reshape(2, S5_TILES, S5_TILE_STATE)], axis=2)
    return win, wout, a


def _s5_scan_kernel(u_ref, win_ref, wout_ref, a_ref, h0_ref, y_ref, fin_ref, bu_ref, st_ref, *, batch):
    d = pl.program_id(0)
    c = pl.program_id(2)
    n_c = pl.num_programs(2)
    ns = S5_TILE_STATE
    steps = u_ref.shape[0] // batch

    @pl.when(c == 0)
    def _():
        st_ref[...] = h0_ref[0, 0]

    bu_ref[...] = jnp.dot(u_ref[...].astype(jnp.bfloat16), win_ref[0, 0], preferred_element_type=jnp.float32)
    ar = jnp.broadcast_to(a_ref[0, 0, 0:1, :], (batch, ns))
    ai = jnp.broadcast_to(a_ref[0, 0, 1:2, :], (batch, ns))

    def step(t, carry):
        hr, hi = carry
        te = jnp.where(d == 0, t, steps - 1 - t)
        r0 = pl.multiple_of(te * batch, batch)
        nr = ar * hr - ai * hi + bu_ref[pl.ds(r0, batch), 0:ns]
        ni = ar * hi + ai * hr + bu_ref[pl.ds(r0, batch), ns:2 * ns]
        bu_ref[pl.ds(r0, batch), 0:ns] = nr
        bu_ref[pl.ds(r0, batch), ns:2 * ns] = ni
        return nr, ni

    hr, hi = lax.fori_loop(0, steps, step, (st_ref[:, 0:ns], st_ref[:, ns:2 * ns]), unroll=4)
    st_ref[:, 0:ns] = hr
    st_ref[:, ns:2 * ns] = hi
    y_ref[0] = jnp.dot(bu_ref[...].astype(jnp.bfloat16), wout_ref[0, 0], preferred_element_type=jnp.float32)

    @pl.when(c == n_c - 1)
    def _():
        fin_ref[0, 0] = st_ref[...]


def s5_scan(u_tm, win, wout, a, h0, batch):
    rows = u_tm.shape[0]
    r = min(S5_ROWS_PER_STEP, rows)
    assert rows % r == 0 and r % batch == 0
    n_c = rows // r

    def chunk(d, c):
        return c + d * (n_c - 1 - 2 * c)

    return pl.pallas_call(
        functools.partial(_s5_scan_kernel, batch=batch),
        grid=(2, S5_TILES, n_c),
        in_specs=[pl.BlockSpec((r, S5_TILE_IN), lambda d, j, c: (chunk(d, c), j)),
                  pl.BlockSpec((1, 1, S5_TILE_IN, 2 * S5_TILE_STATE), lambda d, j, c: (d, j, 0, 0)),
                  pl.BlockSpec((1, 1, 2 * S5_TILE_STATE, S5_TILE_IN), lambda d, j, c: (d, j, 0, 0)),
                  pl.BlockSpec((1, 1, 2, S5_TILE_STATE), lambda d, j, c: (d, j, 0, 0)),
                  pl.BlockSpec((1, 1, batch, 2 * S5_TILE_STATE), lambda d, j, c: (d, j, 0, 0))],
        out_specs=[pl.BlockSpec((1, r, S5_TILE_IN), lambda d, j, c: (d, chunk(d, c), j)),
                   pl.BlockSpec((1, 1, batch, 2 * S5_TILE_STATE), lambda d, j, c: (d, j, 0, 0))],
        out_shape=[jax.ShapeDtypeStruct((2, rows, S5_WIDTH), jnp.float32),
                   jax.ShapeDtypeStruct((2, S5_TILES, batch, 2 * S5_TILE_STATE), jnp.float32)],
        scratch_shapes=[pltpu.VMEM((r, 2 * S5_TILE_STATE), jnp.float32),
                        pltpu.VMEM((batch, 2 * S5_TILE_STATE), jnp.float32)],
        compiler_params=pltpu.CompilerParams(dimension_semantics=("parallel", "parallel", "arbitrary"),
                                             vmem_limit_bytes=VMEM_LIMIT_BYTES),
        name="s5_scan",
    )(u_tm, win, wout, a, h0)


def _gelu_tanh(x):
    return 0.5 * x * (1.0 + jnp.tanh(math.sqrt(2.0 / math.pi) * (x + 0.044715 * (x * x * x))))


def _s5_glu_kernel(u_ref, y_ref, d_ref, w_ref, b_ref, o_ref):
    y = d_ref[...] * u_ref[...] + y_ref[0] + y_ref[1]
    y = _gelu_tanh(y)
    z = jnp.dot(y.astype(jnp.bfloat16), w_ref[...], preferred_element_type=jnp.float32) + b_ref[...]
    o_ref[...] = y * (1.0 / (1.0 + jnp.exp(-z)))


def s5_glu(u_tm, y, d_skip, glu_w_bf16, glu_b, tr=1024):
    rows, w = u_tm.shape
    tr = min(tr, rows)
    assert rows % tr == 0
    return pl.pallas_call(
        _s5_glu_kernel,
        grid=(rows // tr,),
        in_specs=[pl.BlockSpec((tr, w), lambda i: (i, 0)),
                  pl.BlockSpec((2, tr, w), lambda i: (0, i, 0)),
                  pl.BlockSpec((1, w), lambda i: (0, 0)),
                  pl.BlockSpec((w, w), lambda i: (0, 0)),
                  pl.BlockSpec((1, w), lambda i: (0, 0))],
        out_specs=pl.BlockSpec((tr, w), lambda i: (i, 0)),
        out_shape=jax.ShapeDtypeStruct((rows, w), jnp.float32),
        compiler_params=pltpu.CompilerParams(dimension_semantics=("parallel",),
                                             vmem_limit_bytes=VMEM_LIMIT_BYTES),
        name="s5_glu",
    )(u_tm, y, d_skip.reshape(1, w), glu_w_bf16, glu_b.reshape(1, w))


def s5_mixer(u, h0_re, h0_im, prep, d_skip, glu_w, glu_b):
    win, wout, a = prep
    b, l, w = u.shape
    u_tm = jnp.swapaxes(u, 0, 1).reshape(l * b, w)
    if h0_re is None:
        h0 = jnp.zeros((2, S5_TILES, b, 2 * S5_TILE_STATE), jnp.float32)
    else:
        def tiles(h):
            return jnp.transpose(h.astype(jnp.float32).reshape(b, 2, S5_TILES, S5_TILE_STATE), (1, 2, 0, 3))
        h0 = jnp.concatenate([tiles(h0_re), tiles(h0_im)], axis=-1)
    y, fin = s5_scan(u_tm, win, wout, a, h0, b)
    out_tm = s5_glu(u_tm, y, d_skip, glu_w.astype(jnp.bfloat16), glu_b)
    out = jnp.swapaxes(out_tm.reshape(l, b, w), 0, 1)

    def untile(f):
        return jnp.transpose(f, (2, 0, 1, 3)).reshape(b, 2, S5_GROUPS, S5_STATE)
    return out, untile(fin[..., :S5_TILE_STATE]), untile(fin[..., S5_TILE_STATE:])


def _half_mean_matrix():
    i = lax.broadcasted_iota(jnp.int32, (LANES, LANES), 0) // DIFF_QK_DIM
    j = lax.broadcasted_iota(jnp.int32, (LANES, LANES), 1) // DIFF_QK_DIM
    return jnp.where(i == j, 1.0 / DIFF_QK_DIM, 0.0).astype(jnp.bfloat16)


def _rms_groups(x, gain, avg):
    xx = x * x
    hi = xx.astype(jnp.bfloat16)
    lo = (xx - hi.astype(jnp.float32)).astype(jnp.bfloat16)
    ms = (jnp.dot(hi, avg, preferred_element_type=jnp.float32)
          + jnp.dot(lo, avg, preferred_element_type=jnp.float32))
    return x * lax.rsqrt(ms + NORM_EPS) * gain


def _rope_lanes(y, cos, sin_signed, first_half):
    rot = jnp.where(first_half, pltpu.roll(y, LANES - 16, 1), pltpu.roll(y, 16, 1))
    return y * cos + rot * sin_signed


def _diff_prep_kernel(q_ref, k_ref, qg_ref, kg_ref, cos_ref, sin_ref, qo_ref, ko_ref, kn_ref, *, use_rope):
    avg = _half_mean_matrix()
    lane = lax.broadcasted_iota(jnp.int32, (1, LANES), 1)
    first_half = (lane % 32) < 16
    scale = DIFF_QK_DIM ** -0.5
    for h in range(DIFF_HEADS):
        cols = slice(h * LANES, (h + 1) * LANES)
        qn = _rms_groups(q_ref[:, cols], qg_ref[...], avg)
        kn = _rms_groups(k_ref[:, cols], kg_ref[...], avg)
        kn_ref[:, cols] = kn
        if use_rope:
            qn = _rope_lanes(qn, cos_ref[...], sin_ref[...], first_half)
            kn = _rope_lanes(kn, cos_ref[...], sin_ref[...], first_half)
        qo_ref[:, cols] = (qn * scale).astype(jnp.bfloat16)
        ko_ref[:, cols] = kn.astype(jnp.bfloat16)


def diff_prep(p, q_g, k_g, cos, sin_signed, seq_len, use_rope, tr=512):
    rows = p.shape[0]
    tr = min(tr, seq_len)
    assert seq_len % tr == 0
    w = DIFF_HEADS * LANES
    per_seq = seq_len // tr
    g2 = lambda g: jnp.tile(g.astype(jnp.float32), 2).reshape(1, LANES)
    return pl.pallas_call(
        functools.partial(_diff_prep_kernel, use_rope=use_rope),
        grid=(rows // tr,),
        in_specs=[pl.BlockSpec((tr, w), lambda i: (i, 1)),
                  pl.BlockSpec((tr, w), lambda i: (i, 2)),
                  pl.BlockSpec((1, LANES), lambda i: (0, 0)),
                  pl.BlockSpec((1, LANES), lambda i: (0, 0)),
                  pl.BlockSpec((tr, LANES), lambda i: (i % per_seq, 0)),
                  pl.BlockSpec((tr, LANES), lambda i: (i % per_seq, 0))],
        out_specs=[pl.BlockSpec((tr, w), lambda i: (i, 0))] * 3,
        out_shape=[jax.ShapeDtypeStruct((rows, w), jnp.bfloat16), jax.ShapeDtypeStruct((rows, w), jnp.bfloat16),
                   jax.ShapeDtypeStruct((rows, w), jnp.float32)],
        compiler_params=pltpu.CompilerParams(dimension_semantics=("parallel",),
                                             vmem_limit_bytes=VMEM_LIMIT_BYTES),
        name="diff_prep",
    )(p, p, g2(q_g), g2(k_g), cos, sin_signed)


def rope_tables(l):
    t = jnp.arange(l)
    row = (t // GRID_W).astype(jnp.float32)
    col = (t % GRID_W).astype(jnp.float32)
    nf = DIFF_QK_DIM // 4
    inv = ROPE_BASE ** (-jnp.arange(nf, dtype=jnp.float32) / nf)
    ang = jnp.stack([row[:, None] * inv, col[:, None] * inv], axis=1)
    ang = jnp.stack([ang, ang], axis=2).reshape(l, DIFF_QK_DIM)
    sign = jnp.where((jnp.arange(DIFF_QK_DIM) % 32) < 16, -1.0, 1.0)
    return jnp.tile(jnp.cos(ang), (1, 2)), jnp.tile(jnp.sin(ang) * sign, (1, 2))


def _diff_attn_kernel(*refs, has_ctx):
    if has_ctx:
        q_ref, k_ref, v_ref, ck_ref, cv_ref, lam_ref, g_ref, o_ref = refs
    else:
        q_ref, k_ref, v_ref, lam_ref, g_ref, o_ref = refs
    q = q_ref[...]
    lane = lax.broadcasted_iota(jnp.int32, (1, LANES), 1)
    zero = jnp.zeros_like(q)
    qs = (jnp.where(lane < DIFF_QK_DIM, q, zero), jnp.where(lane >= DIFF_QK_DIM, q, zero))
    nt = (((1,), (1,)), ((), ()))
    k = k_ref[...]
    ck = ck_ref[...].astype(jnp.bfloat16) if has_ctx else None
    lam = lam_ref[0:1, 0:1]
    w_self, w_ctx = None, None
    for i in range(2):
        s = lax.dot_general(qs[i], k, nt, preferred_element_type=jnp.float32)
        m = jnp.max(s, axis=-1, keepdims=True)
        if has_ctx:
            sc = lax.dot_general(qs[i], ck, nt, preferred_element_type=jnp.float32)
            m = jnp.maximum(m, jnp.max(sc, axis=-1, keepdims=True))
        p = jnp.exp(s - m)
        l = jnp.sum(p, axis=-1, keepdims=True)
        if has_ctx:
            pc = jnp.exp(sc - m)
            l = l + jnp.sum(pc, axis=-1, keepdims=True)
        coef = 1.0 / l if i == 0 else -lam / l
        w_self = p * coef if i == 0 else w_self + p * coef
        if has_ctx:
            w_ctx = pc * coef if i == 0 else w_ctx + pc * coef
    o = jnp.dot(w_self.astype(jnp.bfloat16), v_ref[...].astype(jnp.bfloat16), preferred_element_type=jnp.float32)
    if has_ctx:
        o = o + jnp.dot(w_ctx.astype(jnp.bfloat16), cv_ref[...].astype(jnp.bfloat16),
                        preferred_element_type=jnp.float32)
    ms = jnp.mean(o * o, axis=-1, keepdims=True)
    o_ref[...] = o * lax.rsqrt(ms + NORM_EPS) * g_ref[...]


def diff_attention(q_bf16, k_bf16, p, ctx_k, ctx_v, lam, gain, batch, seq_len, tq=256):
    tq = min(tq, seq_len)
    nq = seq_len // tq
    has_ctx = ctx_k is not None
    v_col0 = 3 * DIFF_HEADS
    in_specs = [pl.BlockSpec((tq, LANES), lambda b, h, i: (b * nq + i, h)),
                pl.BlockSpec((seq_len, LANES), lambda b, h, i: (b, h)),
                pl.BlockSpec((seq_len, LANES), lambda b, h, i: (b, v_col0 + h))]
    args = [q_bf16, k_bf16, p]
    if has_ctx:
        lc = ctx_k.shape[0] // batch
        in_specs += [pl.BlockSpec((lc, LANES), lambda b, h, i: (b, h))] * 2
        args += [ctx_k, ctx_v]
    in_specs += [pl.BlockSpec((1, LANES), lambda b, h, i: (0, 0))] * 2
    args += [jnp.broadcast_to(lam.astype(jnp.float32), (1, LANES)), gain.astype(jnp.float32).reshape(1, LANES)]
    return pl.pallas_call(
        functools.partial(_diff_attn_kernel, has_ctx=has_ctx),
        grid=(batch, DIFF_HEADS, nq),
        in_specs=in_specs,
        out_specs=pl.BlockSpec((tq, LANES), lambda b, h, i: (b * nq + i, h)),
        out_shape=jax.ShapeDtypeStruct((batch * seq_len, DIFF_HEADS * LANES), jnp.float32),
        compiler_params=pltpu.CompilerParams(dimension_semantics=("parallel", "parallel", "arbitrary"),
                                             vmem_limit_bytes=VMEM_LIMIT_BYTES),
        name="diff_attention",
    )(*args)


NA_Q_ROWS = 8
NA_WIN_ROWS = 16


def _na_prep_kernel(q_ref, k_ref, qg_ref, kg_ref, qo_ref, ko_ref, kn_ref):
    avg = jnp.full((LANES, LANES), 1.0 / HEAD_DIM, jnp.bfloat16)
    for h in range(NA_HEADS):
        cols = slice(h * LANES, (h + 1) * LANES)
        qo_ref[:, cols] = _rms_groups(q_ref[:, cols], qg_ref[...], avg).astype(jnp.bfloat16)
        kn = _rms_groups(k_ref[:, cols], kg_ref[...], avg)
        kn_ref[:, cols] = kn
        ko_ref[:, cols] = kn.astype(jnp.bfloat16)


def na_prep(p, q_g, k_g, tr=512):
    rows = p.shape[0]
    tr = min(tr, rows)
    w = NA_HEADS * LANES
    g1 = lambda g: g.astype(jnp.float32).reshape(1, LANES)
    return pl.pallas_call(
        _na_prep_kernel,
        grid=(rows // tr,),
        in_specs=[pl.BlockSpec((tr, w), lambda i: (i, 0)),
                  pl.BlockSpec((tr, w), lambda i: (i, 1)),
                  pl.BlockSpec((1, LANES), lambda i: (0, 0)),
                  pl.BlockSpec((1, LANES), lambda i: (0, 0))],
        out_specs=[pl.BlockSpec((tr, w), lambda i: (i, 0))] * 3,
        out_shape=[jax.ShapeDtypeStruct((rows, w), jnp.bfloat16), jax.ShapeDtypeStruct((rows, w), jnp.bfloat16),
                   jax.ShapeDtypeStruct((rows, w), jnp.float32)],
        compiler_params=pltpu.CompilerParams(dimension_semantics=("parallel",),
                                             vmem_limit_bytes=VMEM_LIMIT_BYTES),
        name="na_prep",
    )(p, p, g1(q_g), g1(k_g))


def na_bias_tables(rpb, rows):
    nblk = rows // NA_Q_ROWS
    cq = jnp.arange(GRID_W)[:, None]
    ck = jnp.arange(GRID_W)[None, :]
    col_start = jnp.clip(cq - NA_KW // 2, 0, GRID_W - NA_KW)
    col_ok = (ck >= col_start) & (ck < col_start + NA_KW)
    cidx = jnp.clip(ck - cq + NA_KW - 1, 0, 2 * NA_KW - 2)
    tabs = []
    for i in (0, 1, nblk - 1):
        base = min(max(NA_Q_ROWS * i - NA_KH // 2, 0), rows - NA_WIN_ROWS)
        r = NA_Q_ROWS * i + jnp.arange(NA_Q_ROWS)[:, None]
        rk = base + jnp.arange(NA_WIN_ROWS)[None, :]
        r0 = jnp.clip(r - NA_KH // 2, 0, rows - NA_KH)
        row_ok = (rk >= r0) & (rk < r0 + NA_KH)
        ridx = jnp.clip(rk - r + NA_KH - 1, 0, 2 * NA_KH - 2)
        b = rpb[:, ridx][:, :, :, cidx]
        ok = row_ok[:, :, None, None] & col_ok[None, None, :, :]
        b = jnp.where(ok[None], b.astype(jnp.float32), NEG_INF)
        tabs.append(jnp.transpose(b, (0, 1, 3, 2, 4)).reshape(NA_HEADS, NA_Q_ROWS * GRID_W, NA_WIN_ROWS * GRID_W))
    return jnp.stack(tabs, axis=0)


def _na_attn_kernel(*refs, windowed, rows):
    nt = (((1,), (1,)), ((), ()))
    scale = HEAD_DIM ** -0.5
    if windowed:
        q_ref, k_ref, v_ref, b_ref, ck_ref, cv_ref, o_ref = refs
        i = pl.program_id(2)
        base = jnp.clip(NA_Q_ROWS * i - NA_KH // 2, 0, rows - NA_WIN_ROWS)
        k0 = pl.multiple_of(base * GRID_W, GRID_W)
        nk = NA_WIN_ROWS * GRID_W
        q = q_ref[...]
        s = lax.dot_general(q, k_ref[pl.ds(k0, nk), :], nt, preferred_element_type=jnp.float32) * scale + b_ref[0, 0]
        sc = lax.dot_general(q, ck_ref[...].astype(jnp.bfloat16), nt, preferred_element_type=jnp.float32) * scale
        m = jnp.maximum(jnp.max(s, axis=-1, keepdims=True), jnp.max(sc, axis=-1, keepdims=True))
        p = jnp.exp(s - m)
        pc = jnp.exp(sc - m)
        l = jnp.sum(p, axis=-1, keepdims=True) + jnp.sum(pc, axis=-1, keepdims=True)
        o = (jnp.dot(p.astype(jnp.bfloat16), v_ref[pl.ds(k0, nk), :].astype(jnp.bfloat16),
                     preferred_element_type=jnp.float32)
             + jnp.dot(pc.astype(jnp.bfloat16), cv_ref[...].astype(jnp.bfloat16), preferred_element_type=jnp.float32))
    else:
        q_ref, k_ref, v_ref, o_ref = refs
        s = lax.dot_general(q_ref[...], k_ref[...], nt, preferred_element_type=jnp.float32) * scale
        m = jnp.max(s, axis=-1, keepdims=True)
        p = jnp.exp(s - m)
        l = jnp.sum(p, axis=-1, keepdims=True)
        o = jnp.dot(p.astype(jnp.bfloat16), v_ref[...].astype(jnp.bfloat16), preferred_element_type=jnp.float32)
    o_ref[...] = o / l


def na_attention(q_bf16, k_bf16, p, bias_tabs, ctx_k, ctx_v, batch, seq_len):
    windowed = bias_tabs is not None
    v_col0 = 2 * NA_HEADS
    rows = seq_len // GRID_W
    if windowed:
        tq = NA_Q_ROWS * GRID_W
        nq = seq_len // tq
        lc = ctx_k.shape[0] // batch
        nk = NA_WIN_ROWS * GRID_W
        in_specs = [pl.BlockSpec((tq, LANES), lambda b, h, i: (b * nq + i, h)),
                    pl.BlockSpec((seq_len, LANES), lambda b, h, i: (b, h)),
                    pl.BlockSpec((seq_len, LANES), lambda b, h, i: (b, v_col0 + h)),
                    pl.BlockSpec((1, 1, tq, nk),
                                 lambda b, h, i: (jnp.where(i == 0, 0, jnp.where(i == nq - 1, 2, 1)), h, 0, 0)),
                    pl.BlockSpec((lc, LANES), lambda b, h, i: (b, h)),
                    pl.BlockSpec((lc, LANES), lambda b, h, i: (b, h))]
        args = [q_bf16, k_bf16, p, bias_tabs, ctx_k, ctx_v]
    else:
        tq = seq_len
        nq = 1
        in_specs = [pl.BlockSpec((tq, LANES), lambda b, h, i: (b, h)),
                    pl.BlockSpec((seq_len, LANES), lambda b, h, i: (b, h)),
                    pl.BlockSpec((seq_len, LANES), lambda b, h, i: (b, v_col0 + h))]
        args = [q_bf16, k_bf16, p]
    return pl.pallas_call(
        functools.partial(_na_attn_kernel, windowed=windowed, rows=rows),
        grid=(batch, NA_HEADS, nq),
        in_specs=in_specs,
        out_specs=pl.BlockSpec((tq, LANES), lambda b, h, i: (b * nq + i, h)),
        out_shape=jax.ShapeDtypeStruct((batch * seq_len, NA_HEADS * LANES), jnp.float32),
        compiler_params=pltpu.CompilerParams(dimension_semantics=("parallel", "parallel", "arbitrary"),
                                             vmem_limit_bytes=VMEM_LIMIT_BYTES),
        name="na_attention",
    )(*args)


def even_mixer(x, mods, norm_g, b, l, ev, ctx_kv, bias_tabs):
    (w_in, w_out, q_g, k_g, conv_w, conv_b, w1, b1, f1, w2, b2, f2, w3, decay, bias) = ev
    p = mm_norm(x, norm_g, mods[0], mods[1], w_in)
    q, k, kn = na_prep(p, q_g, k_g)
    if ctx_kv is None:
        oa = na_attention(q, k, p, None, None, None, b, l)
    else:
        lc = ctx_kv[0].shape[1]
        oa = na_attention(q, k, p, bias_tabs, ctx_kv[0].reshape(b * lc, MIX_WIDTH),
                          ctx_kv[1].reshape(b * lc, MIX_WIDTH), b, l)
    hb = p[:, 3 * MIX_WIDTH:].reshape(b, l, 3 * HY_WIDTH)
    ob = hyena(hb, conv_w, conv_b, (w1, b1, f1, w2, b2, f2, w3, decay), bias)
    out = mm_pair_res(oa, ob.reshape(b * l, HY_WIDTH), w_out, x, mods[2])
    shp = (b, l, NA_HEADS, HEAD_DIM)
    return out, kn.reshape(shp), p[:, 2 * MIX_WIDTH:3 * MIX_WIDTH].reshape(shp)


def odd_mixer(x, mods, norm_g, b, l, od, lam_init, h0_re, h0_im, ctx_kv, rope):
    (w_in, w_out, s5_prep, d_skip, glu_w, glu_b, q_g, k_g, lq1, lk1, lq2, lk2, subln_g) = od
    p = mm_norm(x, norm_g, mods[0], mods[1], w_in)
    oc, fr, fi = s5_mixer(p[:, :S5_WIDTH].reshape(b, l, S5_WIDTH), h0_re, h0_im, s5_prep, d_skip, glu_w, glu_b)
    lam = (jnp.exp(jnp.sum(lq1 * lk1).astype(jnp.float32))
           - jnp.exp(jnp.sum(lq2 * lk2).astype(jnp.float32)) + lam_init)
    cos, sin_signed = rope
    q, k, kn = diff_prep(p, q_g, k_g, cos, sin_signed, l, use_rope=ctx_kv is not None)
    gain = subln_g.astype(jnp.float32) * (1.0 - lam_init)
    if ctx_kv is None:
        o = diff_attention(q, k, p, None, None, lam, gain, b, l)
    else:
        lc = ctx_kv[0].shape[1]
        o = diff_attention(q, k, p, ctx_kv[0].reshape(b * lc, MIX_WIDTH), ctx_kv[1].reshape(b * lc, MIX_WIDTH),
                           lam, gain, b, l)
    out = mm_pair_res(oc.reshape(b * l, S5_WIDTH), o, w_out, x, mods[2])
    kn = kn.reshape(b, l, DIFF_HEADS, 2, DIFF_QK_DIM)
    v = p[:, S5_WIDTH + 2 * MIX_WIDTH:].reshape(b, l, DIFF_HEADS, DIFF_V_DIM)
    return out, kn, v, fr, fi


PEER_TOKEN_TILE = 512
PEER_EXPERT_TILE = 1024
PEER_SUB_EXPERTS = 256


def _top16_rows(cur, iota):
    n = cur.shape[0]
    tops, idxs = [], []
    for _ in range(PEER_TOPK):
        m = jnp.max(cur, axis=0, keepdims=True)
        first = jnp.min(jnp.where(cur == m, iota, n), axis=0, keepdims=True)
        tops.append(m)
        idxs.append(first)
        cur = jnp.where(iota == first, -jnp.inf, cur)
    return tops, idxs


def _router_kernel(q_ref, keys_ref, m_ref, r1_ref, e0_ref, e1_ref, st_ref):
    tt = q_ref.shape[0]
    half = PEER_QUERY_DIM // 2
    for h in range(PEER_HEADS):
        for s in range(2):
            qs = q_ref[:, (2 * h + s) * half:(2 * h + s + 1) * half].astype(jnp.bfloat16)
            st = lax.dot_general(keys_ref[h, s], qs, (((1,), (1,)), ((), ())),
                                 preferred_element_type=jnp.float32)
            st_ref[2 * h + s] = st

    iota128 = lax.broadcasted_iota(jnp.int32, (PEER_N_KEYS, LANES), 0)
    iota8 = lax.broadcasted_iota(jnp.int32, (SUBLANES, LANES), 0)
    iota_cand = lax.broadcasted_iota(jnp.int32, (PEER_TOPK + SUBLANES * SUBLANES, LANES), 0)
    n_chunks = tt // LANES

    def body(i, carry):
        h = i // n_chunks
        lane0 = pl.multiple_of((i % n_chunks) * LANES, LANES)
        s0 = st_ref[2 * h, :, pl.ds(lane0, LANES)]
        s1 = st_ref[2 * h + 1, :, pl.ds(lane0, LANES)]
        a, a_idx = _top16_rows(s0, iota128)
        b, b_idx = _top16_rows(s1, iota128)
        bmat = jnp.concatenate(b, axis=0)
        sums = [a[k] + bmat for k in range(PEER_TOPK)]
        cand = [sums[0], sums[1][:SUBLANES]]
        for k in range(2, SUBLANES):
            cand.append(jnp.where(iota8 < PEER_TOPK // (k + 1), sums[k][:SUBLANES], -jnp.inf))
        cand.append(jnp.concatenate(a[SUBLANES:], axis=0) + b[0])
        f, _ = _top16_rows(jnp.concatenate(cand, axis=0), iota_cand)
        tau = f[PEER_TOPK - 1]
        z = jnp.ones_like(f[0])
        for k in range(1, PEER_TOPK):
            z = z + jnp.exp(f[k] - f[0])
        m = jnp.zeros((PEER_N_KEYS, LANES), jnp.float32)
        r1 = jnp.full((PEER_N_KEYS, LANES), float(PEER_TOPK), jnp.float32)
        for k in range(PEER_TOPK):
            m_k = jnp.sum(jnp.where(sums[k] >= tau, 1.0, 0.0), axis=0, keepdims=True)
            m = jnp.where(iota128 == a_idx[k], m_k, m)
            r1 = jnp.where(iota128 == b_idx[k], float(k), r1)
        m_ref[h, :, pl.ds(lane0, LANES)] = m
        r1_ref[h, :, pl.ds(lane0, LANES)] = r1.astype(jnp.bfloat16)
        e0_ref[h, :, pl.ds(lane0, LANES)] = jnp.exp(s0 - a[0]) / z
        e1_ref[h, :, pl.ds(lane0, LANES)] = jnp.exp(s1 - b[0]).astype(jnp.bfloat16)
        return carry

    lax.fori_loop(0, PEER_HEADS * n_chunks, body, 0)


def peer_router(q, keys_bf16, tt):
    t = q.shape[0]
    f32 = jnp.float32
    big = jax.ShapeDtypeStruct((PEER_HEADS, PEER_N_KEYS, t), f32)
    half = jax.ShapeDtypeStruct((PEER_HEADS, PEER_N_KEYS, t), jnp.bfloat16)
    blk = pl.BlockSpec((PEER_HEADS, PEER_N_KEYS, tt), lambda i: (0, 0, i))
    return pl.pallas_call(
        _router_kernel,
        grid=(t // tt,),
        in_specs=[pl.BlockSpec((tt, q.shape[1]), lambda i: (i, 0)),
                  pl.BlockSpec(keys_bf16.shape, lambda i: (0, 0, 0, 0))],
        out_specs=[blk, blk, blk, blk],
        out_shape=[big, half, big, half],
        scratch_shapes=[pltpu.VMEM((2 * PEER_HEADS, PEER_N_KEYS, tt), f32)],
        compiler_params=pltpu.CompilerParams(dimension_semantics=("parallel",),
                                             vmem_limit_bytes=VMEM_LIMIT_BYTES),
        name="peer_router",
    )(q, keys_bf16)


def _dense_kernel(x_ref, u_ref, vt_ref, m_ref, r1_ref, e0_ref, e1_ref, res_ref, gate_ref,
                  o_ref, at_ref, wt_ref, acc_ref):
    e = pl.program_id(1)
    n_e = pl.num_programs(1)
    eb = u_ref.shape[0]
    tt = x_ref.shape[0]
    n_chunks = tt // LANES
    n_sub = eb // PEER_SUB_EXPERTS
    i1_per_sub = PEER_SUB_EXPERTS // PEER_N_KEYS
    nt = (((1,), (1,)), ((), ()))

    @pl.when(e == 0)
    def _():
        acc_ref[...] = jnp.zeros_like(acc_ref)

    x = x_ref[...]
    for sb in range(n_sub):
        rows = slice(sb * PEER_SUB_EXPERTS, (sb + 1) * PEER_SUB_EXPERTS)
        at_ref[rows, :] = lax.dot_general(u_ref[rows, :], x, nt, preferred_element_type=jnp.float32)
    for sb in range(n_sub):
        for c in range(n_chunks):
            lanes = slice(c * LANES, (c + 1) * LANES)
            for k in range(i1_per_sub):
                i1 = sb * i1_per_sub + k
                zero = jnp.zeros((PEER_N_KEYS, LANES), jnp.bfloat16)
                g = zero
                for h in range(PEER_HEADS):
                    mrow = m_ref[h, i1:i1 + 1, lanes].astype(jnp.bfloat16)
                    e0row = e0_ref[h, i1:i1 + 1, lanes].astype(jnp.bfloat16)
                    g = g + jnp.where(r1_ref[h, :, lanes] < mrow, e0row * e1_ref[h, :, lanes], zero)
                r = slice(i1 * PEER_N_KEYS, (i1 + 1) * PEER_N_KEYS)
                wt_ref[r, lanes] = _gelu_tanh(at_ref[r, lanes]).astype(jnp.bfloat16) * g
    acc_ref[...] += jnp.dot(vt_ref[0], wt_ref[...], preferred_element_type=jnp.float32)

    @pl.when(e == n_e - 1)
    def _():
        o_ref[...] = res_ref[...] + gate_ref[0] * acc_ref[...].T


def peer_dense(x_bf16, u_bf16, vt_bf16, m, r1, e0, e1, resid, gate, tokens_per_gate, tt, eb):
    t, d = x_bf16.shape
    n_exp = u_bf16.shape[0]
    assert t % tt == 0 and n_exp % eb == 0 and tokens_per_gate % tt == 0
    assert eb // PEER_N_KEYS == SUBLANES
    tiles_per_gate = tokens_per_gate // tt
    rblk = pl.BlockSpec((PEER_HEADS, PEER_N_KEYS, tt), lambda i, j: (0, 0, i))
    gblk = pl.BlockSpec((PEER_HEADS, SUBLANES, tt), lambda i, j: (0, j, i))
    return pl.pallas_call(
        _dense_kernel,
        grid=(t // tt, n_exp // eb),
        in_specs=[pl.BlockSpec((tt, d), lambda i, j: (i, 0)),
                  pl.BlockSpec((eb, d), lambda i, j: (j, 0)),
                  pl.BlockSpec((1, d, eb), lambda i, j: (j, 0, 0)),
                  gblk, rblk, gblk, rblk,
                  pl.BlockSpec((tt, d), lambda i, j: (i, 0)),
                  pl.BlockSpec((1, 1, d), lambda i, j: (i // tiles_per_gate, 0, 0))],
        out_specs=pl.BlockSpec((tt, d), lambda i, j: (i, 0)),
        out_shape=jax.ShapeDtypeStruct((t, d), jnp.float32),
        scratch_shapes=[pltpu.VMEM((eb, tt), jnp.float32),
                        pltpu.VMEM((eb, tt), jnp.bfloat16),
                        pltpu.VMEM((d, tt), jnp.float32)],
        compiler_params=pltpu.CompilerParams(dimension_semantics=("parallel", "arbitrary"),
                                             vmem_limit_bytes=VMEM_LIMIT_BYTES),
        name="peer_dense",
    )(x_bf16, u_bf16, vt_bf16, m, r1, e0, e1, resid, gate)


def peer_layer(x, norm_g, mods, w_q, keys, u_bf16, vt_bf16):
    t, d = x.shape
    q, h = mm_norm(x, norm_g, mods[3], mods[4], w_q, emit_h=True)
    gate = mods[5]
    m, r1, e0, e1 = peer_router(q, keys.astype(jnp.bfloat16), PEER_TOKEN_TILE)
    return peer_dense(h, u_bf16, vt_bf16, m, r1, e0, e1, x, gate, t // gate.shape[0],
                      PEER_TOKEN_TILE, PEER_EXPERT_TILE)


def kernel(x_prompt, x_sample, c, cache_na_k, cache_na_v, cache_diff_k, cache_diff_v, state_s5_re, state_s5_im, c_ctx, mod_w, mod_b, norm_mix_g, norm_ffn_g, ev_w_in, ev_w_out, na_q_g, na_k_g, na_rpb, hy_conv_w, hy_conv_b, hy_w1, hy_b1, hy_f1, hy_w2, hy_b2, hy_f2, hy_w3, hy_decay, hy_bias, od_w_in, od_w_out, s5_lam_re, s5_lam_im, s5_log_dt, s5_b_re, s5_b_im, s5_c_re, s5_c_im, s5_d, s5_glu_w, s5_glu_b, diff_q_g, diff_k_g, diff_lq1, diff_lk1, diff_lq2, diff_lk2, diff_subln_g, peer_w_q, peer_keys, peer_u, peer_v):
    bp, lp, d = x_prompt.shape
    bs, ls, _ = x_sample.shape
    rope_p, rope_s = rope_tables(lp), rope_tables(ls)
    xp, xs = x_prompt.reshape(bp * lp, d), x_sample.reshape(bs * ls, d)
    new_na_k, new_na_v, new_dk, new_dv, new_sr, new_si = [], [], [], [], [], []
    for i in range(DEPTH):
        mp = adaln(c_ctx[None, :], mod_w[i], mod_b[i])
        ms = adaln(c, mod_w[i], mod_b[i])
        j = i // 2
        if i % 2 == 0:
            ev = (ev_w_in[j], ev_w_out[j], na_q_g[j], na_k_g[j], hy_conv_w[j], hy_conv_b[j],
                  hy_w1[j], hy_b1[j], hy_f1[j], hy_w2[j], hy_b2[j], hy_f2[j], hy_w3[j], hy_decay[j], hy_bias[j])
            bias_tabs = na_bias_tables(na_rpb[j], ls // GRID_W)
            xp, kp, vp = even_mixer(xp, mp, norm_mix_g[i], bp, lp, ev, None, None)
            xs, _, _ = even_mixer(xs, ms, norm_mix_g[i], bs, ls, ev, (cache_na_k[:, j], cache_na_v[:, j]), bias_tabs)
            new_na_k.append(kp)
            new_na_v.append(vp)
        else:
            lam_init = 0.8 - 0.6 * math.exp(-0.3 * i)
            s5_prep = s5_prepare(s5_lam_re[j], s5_lam_im[j], s5_log_dt[j], s5_b_re[j], s5_b_im[j],
                                 s5_c_re[j], s5_c_im[j])
            od = (od_w_in[j], od_w_out[j], s5_prep, s5_d[j], s5_glu_w[j], s5_glu_b[j], diff_q_g[j], diff_k_g[j],
                  diff_lq1[j], diff_lk1[j], diff_lq2[j], diff_lk2[j], diff_subln_g[j])
            xp, kp, vp, sr, si = odd_mixer(xp, mp, norm_mix_g[i], bp, lp, od, lam_init, None, None, None, rope_p)
            xs, _, _, _, _ = odd_mixer(xs, ms, norm_mix_g[i], bs, ls, od, lam_init, state_s5_re[:, j],
                                       state_s5_im[:, j], (cache_diff_k[:, j], cache_diff_v[:, j]), rope_s)
            new_dk.append(kp)
            new_dv.append(vp)
            new_sr.append(sr)
            new_si.append(si)
        u_bf16 = peer_u[i].astype(jnp.bfloat16)
        vt_bf16 = jnp.swapaxes(peer_v[i].reshape(-1, PEER_EXPERT_TILE, d), 1, 2).astype(jnp.bfloat16)
        xp = peer_layer(xp, norm_ffn_g[i], mp, peer_w_q[i], peer_keys[i], u_bf16, vt_bf16)
        xs = peer_layer(xs, norm_ffn_g[i], ms, peer_w_q[i], peer_keys[i], u_bf16, vt_bf16)
    return (xp.reshape(bp, lp, d), xs.reshape(bs, ls, d), jnp.stack(new_na_k, axis=1), jnp.stack(new_na_v, axis=1),
            jnp.stack(new_dk, axis=1), jnp.stack(new_dv, axis=1),
            jnp.stack(new_sr, axis=1), jnp.stack(new_si, axis=1))
```

```python
import functools
import math

import jax
import jax.numpy as jnp
from jax import lax
from jax.experimental import pallas as pl
from jax.experimental.pallas import tpu as pltpu

D_MODEL = 2048
DEPTH = 2
GRID_W = 64
MIX_WIDTH = D_MODEL // 2
HEAD_DIM = 128
NA_HEADS = MIX_WIDTH // HEAD_DIM
NA_KH = 8
NA_KW = 16
HY_WIDTH = MIX_WIDTH
HY_ORDER = 2
HY_POS_EMB = 33
S5_WIDTH = MIX_WIDTH
S5_GROUP = 16
S5_GROUPS = S5_WIDTH // S5_GROUP
S5_STATE = 64
DIFF_HEADS = MIX_WIDTH // HEAD_DIM
DIFF_QK_DIM = HEAD_DIM // 2
DIFF_V_DIM = HEAD_DIM
ROPE_BASE = 10000.0
PEER_HEADS = 8
PEER_N_KEYS = 128
PEER_QUERY_DIM = 256
PEER_TOPK = 16
NORM_EPS = 1e-6
NEG_INF = -1e30

LANES = 128
SUBLANES = 8
VMEM_LIMIT_BYTES = 56 * 1024 * 1024


MM_ROW_TILE = 1024
MM_COL_TILE = 512


def _mm_norm_kernel(x_ref, g_ref, sc_ref, sh_ref, w_ref, *out_refs, emit_h):
    if emit_h:
        o_ref, h_ref, hs_ref = out_refs
    else:
        o_ref, hs_ref = out_refs

    @pl.when(pl.program_id(1) == 0)
    def _():
        x = x_ref[...]
        y = x * lax.rsqrt(jnp.mean(x * x, axis=-1, keepdims=True) + NORM_EPS) * g_ref[...]
        hs_ref[...] = (y * (1.0 + sc_ref[0]) + sh_ref[0]).astype(jnp.bfloat16)
        if emit_h:
            h_ref[...] = hs_ref[...]

    o_ref[...] = jnp.dot(hs_ref[...], w_ref[...].astype(jnp.bfloat16), preferred_element_type=jnp.float32)


def mm_norm(x, norm_g, shift, scale, w, emit_h=False):
    m, k = x.shape
    n = w.shape[1]
    tm, tn = min(MM_ROW_TILE, m), min(MM_COL_TILE, n)
    nb = scale.shape[0]
    assert m % tm == 0 and n % tn == 0 and (m // nb) % tm == 0
    tiles_per_mod = (m // nb) // tm
    mod = pl.BlockSpec((1, 1, k), lambda i, j: (i // tiles_per_mod, 0, 0))
    out_specs = [pl.BlockSpec((tm, tn), lambda i, j: (i, j))]
    out_shape = [jax.ShapeDtypeStruct((m, n), jnp.float32)]
    if emit_h:
        out_specs.append(pl.BlockSpec((tm, k), lambda i, j: (i, 0)))
        out_shape.append(jax.ShapeDtypeStruct((m, k), jnp.bfloat16))
    res = pl.pallas_call(
        functools.partial(_mm_norm_kernel, emit_h=emit_h),
        grid=(m // tm, n // tn),
        in_specs=[pl.BlockSpec((tm, k), lambda i, j: (i, 0)),
                  pl.BlockSpec((1, k), lambda i, j: (0, 0)),
                  mod, mod,
                  pl.BlockSpec((k, tn), lambda i, j: (0, j))],
        out_specs=out_specs,
        out_shape=out_shape,
        scratch_shapes=[pltpu.VMEM((tm, k), jnp.bfloat16)],
        compiler_params=pltpu.CompilerParams(dimension_semantics=("parallel", "arbitrary"),
                                             vmem_limit_bytes=VMEM_LIMIT_BYTES),
        name="mm_norm",
    )(x, norm_g.astype(jnp.float32).reshape(1, k), scale, shift, w)
    return res if emit_h else res[0]


def _mm_pair_res_kernel(a_ref, b_ref, wa_ref, wb_ref, r_ref, g_ref, o_ref):
    y = (jnp.dot(a_ref[...].astype(jnp.bfloat16), wa_ref[...].astype(jnp.bfloat16), preferred_element_type=jnp.float32)
         + jnp.dot(b_ref[...].astype(jnp.bfloat16), wb_ref[...].astype(jnp.bfloat16), preferred_element_type=jnp.float32))
    o_ref[...] = r_ref[...] + g_ref[0] * y


def mm_pair_res(a, b, w, resid, gate):
    m, kh = a.shape
    n = w.shape[1]
    tm, tn = min(MM_ROW_TILE, m), min(MM_COL_TILE, n)
    nb = gate.shape[0]
    assert m % tm == 0 and n % tn == 0 and (m // nb) % tm == 0 and w.shape[0] == 2 * kh
    tiles_per_mod = (m // nb) // tm
    return pl.pallas_call(
        _mm_pair_res_kernel,
        grid=(m // tm, n // tn),
        in_specs=[pl.BlockSpec((tm, kh), lambda i, j: (i, 0)),
                  pl.BlockSpec((tm, kh), lambda i, j: (i, 0)),
                  pl.BlockSpec((kh, tn), lambda i, j: (0, j)),
                  pl.BlockSpec((kh, tn), lambda i, j: (1, j)),
                  pl.BlockSpec((tm, tn), lambda i, j: (i, j)),
                  pl.BlockSpec((1, 1, tn), lambda i, j: (i // tiles_per_mod, 0, j))],
        out_specs=pl.BlockSpec((tm, tn), lambda i, j: (i, j)),
        out_shape=jax.ShapeDtypeStruct((m, n), jnp.float32),
        compiler_params=pltpu.CompilerParams(dimension_semantics=("parallel", "arbitrary"),
                                             vmem_limit_bytes=VMEM_LIMIT_BYTES),
        name="mm_pair_res",
    )(a, b, w, w, resid, gate)


def adaln(cond, w, b):
    m = jax.nn.silu(cond) @ w + b
    return jnp.split(m[:, None, :], 6, axis=-1)


def hyena_filters(l, w1, b1, f1, w2, b2, f2, w3, decay):
    t = jnp.linspace(0.0, 1.0, l, dtype=jnp.float32)[:, None]
    bands = (HY_POS_EMB - 1) // 2
    w_ang = 2.0 * math.pi * jnp.arange(l, dtype=jnp.float32)[:, None] / l
    freqs = jnp.linspace(1e-4, bands - 1, bands, dtype=jnp.float32)[None, :]
    z = jnp.concatenate([t, jnp.cos(freqs * w_ang), -jnp.sin(freqs * w_ang)], axis=-1)
    h = jnp.sin(f1 * (z @ w1 + b1))
    h = jnp.sin(f2 * (h @ w2 + b2))
    h = (h @ w3).reshape(l, 2, HY_ORDER, HY_WIDTH).astype(jnp.float32)
    h = h * jnp.exp(-t.reshape(l, 1, 1, 1) * jnp.abs(decay.astype(jnp.float32)))
    h_f, h_b = h[:, 0], h[:, 1]
    zero = jnp.zeros((1, HY_ORDER, HY_WIDTH), jnp.float32)
    return jnp.concatenate([h_f, zero, h_b[1:][::-1]], axis=0)


def _split_bf16(x):
    hi = x.astype(jnp.bfloat16)
    lo = (x - hi.astype(jnp.float32)).astype(jnp.bfloat16)
    return hi, lo


def _dot3(a_hi, a_lo, x):
    m = a_hi.shape[0]
    xh, xl = _split_bf16(x)
    r = jnp.dot(jnp.concatenate([a_hi, a_lo], axis=0), xh, preferred_element_type=jnp.float32)
    return r[:m] + r[m:] + jnp.dot(a_hi, xl, preferred_element_type=jnp.float32)


HY_STEP_ROWS = 256


def hyena_factors(l):
    n = 2 * l
    n2 = 64 if n >= 8192 else 16
    return n // n2, n2


def hyena_tables(l):
    n = 2 * l
    n1, n2 = hyena_factors(l)
    t = (n2 * jnp.arange(n1 // 2)[None, None, :] + jnp.arange(n2)[:, None, None])
    k1 = jnp.arange(n1)[None, :, None]
    ang = (2.0 * math.pi / n) * ((t * k1) % n).astype(jnp.float32)
    fa = jnp.concatenate([jnp.cos(ang), -jnp.sin(ang)], axis=1)
    fc = jnp.swapaxes(fa, 1, 2) / n
    a2 = (2.0 * math.pi / n2) * ((jnp.arange(n2)[:, None] * jnp.arange(n2)[None, :]) % n2).astype(jnp.float32)
    c, s = jnp.cos(a2), jnp.sin(a2)
    fb = jnp.concatenate([jnp.concatenate([c, s], axis=1), jnp.concatenate([-s, c], axis=1)], axis=0)
    fbi = jnp.concatenate([jnp.concatenate([c, -s], axis=1), jnp.concatenate([s, c], axis=1)], axis=0)
    return tuple(_split_bf16(x) for x in (fa, fc, fb, fbi))


def hyena_spectrum(filt, l):
    n1, n2 = hyena_factors(l)
    kf = jnp.fft.fft(filt, axis=0)
    kf = jnp.stack([jnp.real(kf), jnp.imag(kf)], axis=0).astype(jnp.float32)
    kf = kf.reshape(2, n2, n1, HY_ORDER, HY_WIDTH)
    return jnp.transpose(kf, (3, 0, 2, 1, 4))


def _hy_stage_a_kernel(z_ref, fh_ref, fl_ref, o_ref):
    for i in range(z_ref.shape[0]):
        o_ref[i] = _dot3(fh_ref[0], fl_ref[0], z_ref[i])


def _hy_stage_b_kernel(s_ref, k_ref, fh_ref, fl_ref, gh_ref, gl_ref, o_ref):
    n2 = s_ref.shape[3]
    kr, ki = k_ref[0, 0], k_ref[1, 0]
    for i in range(s_ref.shape[0]):
        y = jnp.concatenate([s_ref[i, 0, 0], s_ref[i, 1, 0]], axis=0)
        z = _dot3(fh_ref[...], fl_ref[...], y)
        zr, zi = z[:n2], z[n2:]
        p = jnp.concatenate([zr * kr - zi * ki, zr * ki + zi * kr], axis=0)
        q = _dot3(gh_ref[...], gl_ref[...], p)
        o_ref[i, 0, 0] = q[:n2]
        o_ref[i, 1, 0] = q[n2:]


def _hy_stage_c_kernel(q_ref, fh_ref, fl_ref, z_ref, g_ref, b_ref, o_ref):
    for i in range(q_ref.shape[0]):
        y = _dot3(fh_ref[0], fl_ref[0], q_ref[i])
        o_ref[i] = g_ref[i] * (y + z_ref[i] * b_ref[...])


def hyena_long_conv(z, z_col, gate, gate_col, bias_o, spec_o, tables, l):
    (fah, fal), (fch, fcl), (fbh, fbl), (fgh, fgl) = tables
    b = z.shape[0]
    c = HY_WIDTH
    n1, n2 = hyena_factors(l)
    h1 = n1 // 2
    bb = max(1, min(b, HY_STEP_ROWS // n1))
    assert b % bb == 0
    cp = pltpu.CompilerParams(dimension_semantics=("parallel", "parallel"), vmem_limit_bytes=VMEM_LIMIT_BYTES)
    zblocks = z.shape[-1] // c
    gblocks = gate.shape[-1] // c
    zv = z.reshape(b, h1, n2 * z.shape[-1])
    gv = gate.reshape(b, h1, n2 * gate.shape[-1])
    s1 = pl.pallas_call(
        _hy_stage_a_kernel,
        grid=(b // bb, n2),
        in_specs=[pl.BlockSpec((bb, h1, c), lambda i, j: (i, 0, j * zblocks + z_col)),
                  pl.BlockSpec((1, 2 * n1, h1), lambda i, j: (j, 0, 0)),
                  pl.BlockSpec((1, 2 * n1, h1), lambda i, j: (j, 0, 0))],
        out_specs=pl.BlockSpec((bb, 2 * n1, c), lambda i, j: (i, 0, j)),
        out_shape=jax.ShapeDtypeStruct((b, 2 * n1, n2 * c), jnp.float32),
        compiler_params=cp, name="hyena_stage_a",
    )(zv, fah, fal)
    s1 = s1.reshape(b, 2, n1, n2, c)
    mat = pl.BlockSpec((2 * n2, 2 * n2), lambda k, i: (0, 0))
    q = pl.pallas_call(
        _hy_stage_b_kernel,
        grid=(n1, b // bb),
        in_specs=[pl.BlockSpec((bb, 2, 1, n2, c), lambda k, i: (i, 0, k, 0, 0)),
                  pl.BlockSpec((2, 1, n2, c), lambda k, i: (0, k, 0, 0)),
                  mat, mat, mat, mat],
        out_specs=pl.BlockSpec((bb, 2, 1, n2, c), lambda k, i: (i, 0, k, 0, 0)),
        out_shape=jax.ShapeDtypeStruct((b, 2, n1, n2, c), jnp.float32),
        compiler_params=cp, name="hyena_stage_b",
    )(s1, spec_o, fbh, fbl, fgh, fgl)
    q = q.reshape(b, 2 * n1, n2 * c)
    out = pl.pallas_call(
        _hy_stage_c_kernel,
        grid=(b // bb, n2),
        in_specs=[pl.BlockSpec((bb, 2 * n1, c), lambda i, j: (i, 0, j)),
                  pl.BlockSpec((1, h1, 2 * n1), lambda i, j: (j, 0, 0)),
                  pl.BlockSpec((1, h1, 2 * n1), lambda i, j: (j, 0, 0)),
                  pl.BlockSpec((bb, h1, c), lambda i, j: (i, 0, j * zblocks + z_col)),
                  pl.BlockSpec((bb, h1, c), lambda i, j: (i, 0, j * gblocks + gate_col)),
                  pl.BlockSpec((1, c), lambda i, j: (0, 0))],
        out_specs=pl.BlockSpec((bb, h1, c), lambda i, j: (i, 0, j)),
        out_shape=jax.ShapeDtypeStruct((b, h1, n2 * c), jnp.float32),
        compiler_params=cp, name="hyena_stage_c",
    )(q, fch, fcl, zv, gv, bias_o.astype(jnp.float32).reshape(1, c))
    return out.reshape(b, l, c)


def hyena(u, conv_w, conv_b, filt, bias):
    b, l, _ = u.shape
    up = jnp.pad(u, ((0, 0), (1, 1), (0, 0)))
    u = up[:, :-2] * conv_w[0] + up[:, 1:-1] * conv_w[1] + up[:, 2:] * conv_w[2] + conv_b
    tables = hyena_tables(l)
    spec = hyena_spectrum(hyena_filters(l, *filt), l)
    z, z_col = u, 0
    for o in range(HY_ORDER):
        z = hyena_long_conv(z, z_col, u, 1 + o, bias[o], spec[o], tables, l)
        z_col = 0
    return z


S5_TILE_GROUPS = 8
S5_TILES = S5_GROUPS // S5_TILE_GROUPS
S5_TILE_IN = S5_TILE_GROUPS * S5_GROUP
S5_TILE_STATE = S5_TILE_GROUPS * S5_STATE
S5_ROWS_PER_STEP = 2048


def s5_prepare(lam_re, lam_im, log_dt, b_re, b_im, c_re, c_im):
    f32 = jnp.float32
    lam_re, lam_im = lam_re.astype(f32), lam_im.astype(f32)
    dt = jnp.exp(log_dt.astype(f32))[..., None]
    mag = jnp.exp(lam_re * dt)
    ar, ai = mag * jnp.cos(lam_im * dt), mag * jnp.sin(lam_im * dt)
    den = lam_re * lam_re + lam_im * lam_im
    cr = ((ar - 1.0) * lam_re + ai * lam_im) / den
    ci = (ai * lam_re - (ar - 1.0) * lam_im) / den
    bbr = cr[..., None] * b_re - ci[..., None] * b_im
    bbi = cr[..., None] * b_im + ci[..., None] * b_re
    eye = jnp.eye(S5_TILE_GROUPS, dtype=f32)

    def tile_in(bb):
        x = bb.reshape(2, S5_TILES, S5_TILE_GROUPS, S5_STATE, S5_GROUP)
        x = jnp.einsum('dtgnp,gh->dtgphn', x, eye)
        return x.reshape(2, S5_TILES, S5_TILE_IN, S5_TILE_STATE)

    def tile_out(cc):
        x = cc.astype(f32).reshape(2, S5_TILES, S5_TILE_GROUPS, S5_GROUP, S5_STATE)
        x = jnp.einsum('dtgpn,gh->dtgnhp', x, eye)
        return x.reshape(2, S5_TILES, S5_TILE_STATE, S5_TILE_IN)

    win = jnp.concatenate([tile_in(bbr), tile_in(bbi)], axis=-1).astype(jnp.bfloat16)
    wout = jnp.concatenate([tile_out(c_re), -tile_out(c_im)], axis=-2).astype(jnp.bfloat16)
    a = jnp.stack([ar.reshape(2, S5_TILES, S5_TILE_STATE), ai.reshape(2, S5_TILES, S5_TILE_STATE)], axis=2)
    return win, wout, a


def _s5_scan_kernel(u_ref, win_ref, wout_ref, a_ref, h0_ref, y_ref, fin_ref, bu_ref, st_ref, *, batch):
    d = pl.program_id(0)
    c = pl.program_id(2)
    n_c = pl.num_programs(2)
    ns = S5_TILE_STATE
    steps = u_ref.shape[0] // batch

    @pl.when(c == 0)
    def _():
        st_ref[...] = h0_ref[0, 0]

    bu_ref[...] = jnp.dot(u_ref[...].astype(jnp.bfloat16), win_ref[0, 0], preferred_element_type=jnp.float32)
    ar = jnp.broadcast_to(a_ref[0, 0, 0:1, :], (batch, ns))
    ai = jnp.broadcast_to(a_ref[0, 0, 1:2, :], (batch, ns))

    def step(t, carry):
        hr, hi = carry
        te = jnp.where(d == 0, t, steps - 1 - t)
        r0 = pl.multiple_of(te * batch, batch)
        nr = ar * hr - ai * hi + bu_ref[pl.ds(r0, batch), 0:ns]
        ni = ar * hi + ai * hr + bu_ref[pl.ds(r0, batch), ns:2 * ns]
        bu_ref[pl.ds(r0, batch), 0:ns] = nr
        bu_ref[pl.ds(r0, batch), ns:2 * ns] = ni
        return nr, ni

    hr, hi = lax.fori_loop(0, steps, step, (st_ref[:, 0:ns], st_ref[:, ns:2 * ns]), unroll=4)
    st_ref[:, 0:ns] = hr
    st_ref[:, ns:2 * ns] = hi
    y_ref[0] = jnp.dot(bu_ref[...].astype(jnp.bfloat16), wout_ref[0, 0], preferred_element_type=jnp.float32)

    @pl.when(c == n_c - 1)
    def _():
        fin_ref[0, 0] = st_ref[...]


def s5_scan(u_tm, win, wout, a, h0, batch):
    rows = u_tm.shape[0]
    r = min(S5_ROWS_PER_STEP, rows)
    assert rows % r == 0 and r % batch == 0
    n_c = rows // r

    def chunk(d, c):
        return c + d * (n_c - 1 - 2 * c)

    return pl.pallas_call(
        functools.partial(_s5_scan_kernel, batch=batch),
        grid=(2, S5_TILES, n_c),
        in_specs=[pl.BlockSpec((r, S5_TILE_IN), lambda d, j, c: (chunk(d, c), j)),
                  pl.BlockSpec((1, 1, S5_TILE_IN, 2 * S5_TILE_STATE), lambda d, j, c: (d, j, 0, 0)),
                  pl.BlockSpec((1, 1, 2 * S5_TILE_STATE, S5_TILE_IN), lambda d, j, c: (d, j, 0, 0)),
                  pl.BlockSpec((1, 1, 2, S5_TILE_STATE), lambda d, j, c: (d, j, 0, 0)),
                  pl.BlockSpec((1, 1, batch, 2 * S5_TILE_STATE), lambda d, j, c: (d, j, 0, 0))],
        out_specs=[pl.BlockSpec((1, r, S5_TILE_IN), lambda d, j, c: (d, chunk(d, c), j)),
                   pl.BlockSpec((1, 1, batch, 2 * S5_TILE_STATE), lambda d, j, c: (d, j, 0, 0))],
        out_shape=[jax.ShapeDtypeStruct((2, rows, S5_WIDTH), jnp.float32),
                   jax.ShapeDtypeStruct((2, S5_TILES, batch, 2 * S5_TILE_STATE), jnp.float32)],
        scratch_shapes=[pltpu.VMEM((r, 2 * S5_TILE_STATE), jnp.float32),
                        pltpu.VMEM((batch, 2 * S5_TILE_STATE), jnp.float32)],
        compiler_params=pltpu.CompilerParams(dimension_semantics=("parallel", "parallel", "arbitrary"),
                                             vmem_limit_bytes=VMEM_LIMIT_BYTES),
        name="s5_scan",
    )(u_tm, win, wout, a, h0)


def _gelu_tanh(x):
    return 0.5 * x * (1.0 + jnp.tanh(math.sqrt(2.0 / math.pi) * (x + 0.044715 * (x * x * x))))


def _s5_glu_kernel(u_ref, y_ref, d_ref, w_ref, b_ref, o_ref):
    y = d_ref[...] * u_ref[...] + y_ref[0] + y_ref[1]
    y = _gelu_tanh(y)
    z = jnp.dot(y.astype(jnp.bfloat16), w_ref[...], preferred_element_type=jnp.float32) + b_ref[...]
    o_ref[...] = y * (1.0 / (1.0 + jnp.exp(-z)))


def s5_glu(u_tm, y, d_skip, glu_w_bf16, glu_b, tr=1024):
    rows, w = u_tm.shape
    tr = min(tr, rows)
    assert rows % tr == 0
    return pl.pallas_call(
        _s5_glu_kernel,
        grid=(rows // tr,),
        in_specs=[pl.BlockSpec((tr, w), lambda i: (i, 0)),
                  pl.BlockSpec((2, tr, w), lambda i: (0, i, 0)),
                  pl.BlockSpec((1, w), lambda i: (0, 0)),
                  pl.BlockSpec((w, w), lambda i: (0, 0)),
                  pl.BlockSpec((1, w), lambda i: (0, 0))],
        out_specs=pl.BlockSpec((tr, w), lambda i: (i, 0)),
        out_shape=jax.ShapeDtypeStruct((rows, w), jnp.float32),
        compiler_params=pltpu.CompilerParams(dimension_semantics=("parallel",),
                                             vmem_limit_bytes=VMEM_LIMIT_BYTES),
        name="s5_glu",
    )(u_tm, y, d_skip.reshape(1, w), glu_w_bf16, glu_b.reshape(1, w))


def s5_mixer(u, h0_re, h0_im, prep, d_skip, glu_w, glu_b):
    win, wout, a = prep
    b, l, w = u.shape
    u_tm = jnp.swapaxes(u, 0, 1).reshape(l * b, w)
    if h0_re is None:
        h0 = jnp.zeros((2, S5_TILES, b, 2 * S5_TILE_STATE), jnp.float32)
    else:
        def tiles(h):
            return jnp.transpose(h.astype(jnp.float32).reshape(b, 2, S5_TILES, S5_TILE_STATE), (1, 2, 0, 3))
        h0 = jnp.concatenate([tiles(h0_re), tiles(h0_im)], axis=-1)
    y, fin = s5_scan(u_tm, win, wout, a, h0, b)
    out_tm = s5_glu(u_tm, y, d_skip, glu_w.astype(jnp.bfloat16), glu_b)
    out = jnp.swapaxes(out_tm.reshape(l, b, w), 0, 1)

    def untile(f):
        return jnp.transpose(f, (2, 0, 1, 3)).reshape(b, 2, S5_GROUPS, S5_STATE)
    return out, untile(fin[..., :S5_TILE_STATE]), untile(fin[..., S5_TILE_STATE:])


def _half_mean_matrix():
    i = lax.broadcasted_iota(jnp.int32, (LANES, LANES), 0) // DIFF_QK_DIM
    j = lax.broadcasted_iota(jnp.int32, (LANES, LANES), 1) // DIFF_QK_DIM
    return jnp.where(i == j, 1.0 / DIFF_QK_DIM, 0.0).astype(jnp.bfloat16)


def _rms_groups(x, gain, avg):
    xx = x * x
    hi = xx.astype(jnp.bfloat16)
    lo = (xx - hi.astype(jnp.float32)).astype(jnp.bfloat16)
    ms = (jnp.dot(hi, avg, preferred_element_type=jnp.float32)
          + jnp.dot(lo, avg, preferred_element_type=jnp.float32))
    return x * lax.rsqrt(ms + NORM_EPS) * gain


def _rope_lanes(y, cos, sin_signed, first_half):
    rot = jnp.where(first_half, pltpu.roll(y, LANES - 16, 1), pltpu.roll(y, 16, 1))
    return y * cos + rot * sin_signed


def _diff_prep_kernel(q_ref, k_ref, qg_ref, kg_ref, cos_ref, sin_ref, qo_ref, ko_ref, kn_ref, *, use_rope):
    avg = _half_mean_matrix()
    lane = lax.broadcasted_iota(jnp.int32, (1, LANES), 1)
    first_half = (lane % 32) < 16
    scale = DIFF_QK_DIM ** -0.5
    for h in range(DIFF_HEADS):
        cols = slice(h * LANES, (h + 1) * LANES)
        qn = _rms_groups(q_ref[:, cols], qg_ref[...], avg)
        kn = _rms_groups(k_ref[:, cols], kg_ref[...], avg)
        kn_ref[:, cols] = kn
        if use_rope:
            qn = _rope_lanes(qn, cos_ref[...], sin_ref[...], first_half)
            kn = _rope_lanes(kn, cos_ref[...], sin_ref[...], first_half)
        qo_ref[:, cols] = (qn * scale).astype(jnp.bfloat16)
        ko_ref[:, cols] = kn.astype(jnp.bfloat16)


def diff_prep(p, q_g, k_g, cos, sin_signed, seq_len, use_rope, tr=512):
    rows = p.shape[0]
    tr = min(tr, seq_len)
    assert seq_len % tr == 0
    w = DIFF_HEADS * LANES
    per_seq = seq_len // tr
    g2 = lambda g: jnp.tile(g.astype(jnp.float32), 2).reshape(1, LANES)
    return pl.pallas_call(
        functools.partial(_diff_prep_kernel, use_rope=use_rope),
        grid=(rows // tr,),
        in_specs=[pl.BlockSpec((tr, w), lambda i: (i, 1)),
                  pl.BlockSpec((tr, w), lambda i: (i, 2)),
                  pl.BlockSpec((1, LANES), lambda i: (0, 0)),
                  pl.BlockSpec((1, LANES), lambda i: (0, 0)),
                  pl.BlockSpec((tr, LANES), lambda i: (i % per_seq, 0)),
                  pl.BlockSpec((tr, LANES), lambda i: (i % per_seq, 0))],
        out_specs=[pl.BlockSpec((tr, w), lambda i: (i, 0))] * 3,
        out_shape=[jax.ShapeDtypeStruct((rows, w), jnp.bfloat16), jax.ShapeDtypeStruct((rows, w), jnp.bfloat16),
                   jax.ShapeDtypeStruct((rows, w), jnp.float32)],
        compiler_params=pltpu.CompilerParams(dimension_semantics=("parallel",),
                                             vmem_limit_bytes=VMEM_LIMIT_BYTES),
        name="diff_prep",
    )(p, p, g2(q_g), g2(k_g), cos, sin_signed)


def rope_tables(l):
    t = jnp.arange(l)
    row = (t // GRID_W).astype(jnp.float32)
    col = (t % GRID_W).astype(jnp.float32)
    nf = DIFF_QK_DIM // 4
    inv = ROPE_BASE ** (-jnp.arange(nf, dtype=jnp.float32) / nf)
    ang = jnp.stack([row[:, None] * inv, col[:, None] * inv], axis=1)
    ang = jnp.stack([ang, ang], axis=2).reshape(l, DIFF_QK_DIM)
    sign = jnp.where((jnp.arange(DIFF_QK_DIM) % 32) < 16, -1.0, 1.0)
    return jnp.tile(jnp.cos(ang), (1, 2)), jnp.tile(jnp.sin(ang) * sign, (1, 2))


def _diff_attn_kernel(*refs, has_ctx):
    if has_ctx:
        q_ref, k_ref, v_ref, ck_ref, cv_ref, lam_ref, g_ref, o_ref = refs
    else:
        q_ref, k_ref, v_ref, lam_ref, g_ref, o_ref = refs
    q = q_ref[...]
    lane = lax.broadcasted_iota(jnp.int32, (1, LANES), 1)
    zero = jnp.zeros_like(q)
    qs = (jnp.where(lane < DIFF_QK_DIM, q, zero), jnp.where(lane >= DIFF_QK_DIM, q, zero))
    nt = (((1,), (1,)), ((), ()))
    k = k_ref[...]
    ck = ck_ref[...].astype(jnp.bfloat16) if has_ctx else None
    lam = lam_ref[0:1, 0:1]
    w_self, w_ctx = None, None
    for i in range(2):
        s = lax.dot_general(qs[i], k, nt, preferred_element_type=jnp.float32)
        m = jnp.max(s, axis=-1, keepdims=True)
        if has_ctx:
            sc = lax.dot_general(qs[i], ck, nt, preferred_element_type=jnp.float32)
            m = jnp.maximum(m, jnp.max(sc, axis=-1, keepdims=True))
        p = jnp.exp(s - m)
        l = jnp.sum(p, axis=-1, keepdims=True)
        if has_ctx:
            pc = jnp.exp(sc - m)
            l = l + jnp.sum(pc, axis=-1, keepdims=True)
        coef = 1.0 / l if i == 0 else -lam / l
        w_self = p * coef if i == 0 else w_self + p * coef
        if has_ctx:
            w_ctx = pc * coef if i == 0 else w_ctx + pc * coef
    o = jnp.dot(w_self.astype(jnp.bfloat16), v_ref[...].astype(jnp.bfloat16), preferred_element_type=jnp.float32)
    if has_ctx:
        o = o + jnp.dot(w_ctx.astype(jnp.bfloat16), cv_ref[...].astype(jnp.bfloat16),
                        preferred_element_type=jnp.float32)
    ms = jnp.mean(o * o, axis=-1, keepdims=True)
    o_ref[...] = o * lax.rsqrt(ms + NORM_EPS) * g_ref[...]


def diff_attention(q_bf16, k_bf16, p, ctx_k, ctx_v, lam, gain, batch, seq_len, tq=256):
    tq = min(tq, seq_len)
    nq = seq_len // tq
    has_ctx = ctx_k is not None
    v_col0 = 3 * DIFF_HEADS
    in_specs = [pl.BlockSpec((tq, LANES), lambda b, h, i: (b * nq + i, h)),
                pl.BlockSpec((seq_len, LANES), lambda b, h, i: (b, h)),
                pl.BlockSpec((seq_len, LANES), lambda b, h, i: (b, v_col0 + h))]
    args = [q_bf16, k_bf16, p]
    if has_ctx:
        lc = ctx_k.shape[0] // batch
        in_specs += [pl.BlockSpec((lc, LANES), lambda b, h, i: (b, h))] * 2
        args += [ctx_k, ctx_v]
    in_specs += [pl.BlockSpec((1, LANES), lambda b, h, i: (0, 0))] * 2
    args += [jnp.broadcast_to(lam.astype(jnp.float32), (1, LANES)), gain.astype(jnp.float32).reshape(1, LANES)]
    return pl.pallas_call(
        functools.partial(_diff_attn_kernel, has_ctx=has_ctx),
        grid=(batch, DIFF_HEADS, nq),
        in_specs=in_specs,
        out_specs=pl.BlockSpec((tq, LANES), lambda b, h, i: (b * nq + i, h)),
        out_shape=jax.ShapeDtypeStruct((batch * seq_len, DIFF_HEADS * LANES), jnp.float32),
        compiler_params=pltpu.CompilerParams(dimension_semantics=("parallel", "parallel", "arbitrary"),
                                             vmem_limit_bytes=VMEM_LIMIT_BYTES),
        name="diff_attention",
    )(*args)


NA_Q_ROWS = 8
NA_WIN_ROWS = 16


def _na_prep_kernel(q_ref, k_ref, qg_ref, kg_ref, qo_ref, ko_ref, kn_ref):
    avg = jnp.full((LANES, LANES), 1.0 / HEAD_DIM, jnp.bfloat16)
    for h in range(NA_HEADS):
        cols = slice(h * LANES, (h + 1) * LANES)
        qo_ref[:, cols] = _rms_groups(q_ref[:, cols], qg_ref[...], avg).astype(jnp.bfloat16)
        kn = _rms_groups(k_ref[:, cols], kg_ref[...], avg)
        kn_ref[:, cols] = kn
        ko_ref[:, cols] = kn.astype(jnp.bfloat16)


def na_prep(p, q_g, k_g, tr=512):
    rows = p.shape[0]
    tr = min(tr, rows)
    w = NA_HEADS * LANES
    g1 = lambda g: g.astype(jnp.float32).reshape(1, LANES)
    return pl.pallas_call(
        _na_prep_kernel,
        grid=(rows // tr,),
        in_specs=[pl.BlockSpec((tr, w), lambda i: (i, 0)),
                  pl.BlockSpec((tr, w), lambda i: (i, 1)),
                  pl.BlockSpec((1, LANES), lambda i: (0, 0)),
                  pl.BlockSpec((1, LANES), lambda i: (0, 0))],
        out_specs=[pl.BlockSpec((tr, w), lambda i: (i, 0))] * 3,
        out_shape=[jax.ShapeDtypeStruct((rows, w), jnp.bfloat16), jax.ShapeDtypeStruct((rows, w), jnp.bfloat16),
                   jax.ShapeDtypeStruct((rows, w), jnp.float32)],
        compiler_params=pltpu.CompilerParams(dimension_semantics=("parallel",),
                                             vmem_limit_bytes=VMEM_LIMIT_BYTES),
        name="na_prep",
    )(p, p, g1(q_g), g1(k_g))


def na_bias_tables(rpb, rows):
    nblk = rows // NA_Q_ROWS
    cq = jnp.arange(GRID_W)[:, None]
    ck = jnp.arange(GRID_W)[None, :]
    col_start = jnp.clip(cq - NA_KW // 2, 0, GRID_W - NA_KW)
    col_ok = (ck >= col_start) & (ck < col_start + NA_KW)
    cidx = jnp.clip(ck - cq + NA_KW - 1, 0, 2 * NA_KW - 2)
    tabs = []
    for i in (0, 1, nblk - 1):
        base = min(max(NA_Q_ROWS * i - NA_KH // 2, 0), rows - NA_WIN_ROWS)
        r = NA_Q_ROWS * i + jnp.arange(NA_Q_ROWS)[:, None]
        rk = base + jnp.arange(NA_WIN_ROWS)[None, :]
        r0 = jnp.clip(r - NA_KH // 2, 0, rows - NA_KH)
        row_ok = (rk >= r0) & (rk < r0 + NA_KH)
        ridx = jnp.clip(rk - r + NA_KH - 1, 0, 2 * NA_KH - 2)
        b = rpb[:, ridx][:, :, :, cidx]
        ok = row_ok[:, :, None, None] & col_ok[None, None, :, :]
        b = jnp.where(ok[None], b.astype(jnp.float32), NEG_INF)
        tabs.append(jnp.transpose(b, (0, 1, 3, 2, 4)).reshape(NA_HEADS, NA_Q_ROWS * GRID_W, NA_WIN_ROWS * GRID_W))
    return jnp.stack(tabs, axis=0)


def _na_attn_kernel(*refs, windowed, rows):
    nt = (((1,), (1,)), ((), ()))
    scale = HEAD_DIM ** -0.5
    if windowed:
        q_ref, k_ref, v_ref, b_ref, ck_ref, cv_ref, o_ref = refs
        i = pl.program_id(2)
        base = jnp.clip(NA_Q_ROWS * i - NA_KH // 2, 0, rows - NA_WIN_ROWS)
        k0 = pl.multiple_of(base * GRID_W, GRID_W)
        nk = NA_WIN_ROWS * GRID_W
        q = q_ref[...]
        s = lax.dot_general(q, k_ref[pl.ds(k0, nk), :], nt, preferred_element_type=jnp.float32) * scale + b_ref[0, 0]
        sc = lax.dot_general(q, ck_ref[...].astype(jnp.bfloat16), nt, preferred_element_type=jnp.float32) * scale
        m = jnp.maximum(jnp.max(s, axis=-1, keepdims=True), jnp.max(sc, axis=-1, keepdims=True))
        p = jnp.exp(s - m)
        pc = jnp.exp(sc - m)
        l = jnp.sum(p, axis=-1, keepdims=True) + jnp.sum(pc, axis=-1, keepdims=True)
        o = (jnp.dot(p.astype(jnp.bfloat16), v_ref[pl.ds(k0, nk), :].astype(jnp.bfloat16),
                     preferred_element_type=jnp.float32)
             + jnp.dot(pc.astype(jnp.bfloat16), cv_ref[...].astype(jnp.bfloat16), preferred_element_type=jnp.float32))
    else:
        q_ref, k_ref, v_ref, o_ref = refs
        s = lax.dot_general(q_ref[...], k_ref[...], nt, preferred_element_type=jnp.float32) * scale
        m = jnp.max(s, axis=-1, keepdims=True)
        p = jnp.exp(s - m)
        l = jnp.sum(p, axis=-1, keepdims=True)
        o = jnp.dot(p.astype(jnp.bfloat16), v_ref[...].astype(jnp.bfloat16), preferred_element_type=jnp.float32)
    o_ref[...] = o / l


def na_attention(q_bf16, k_bf16, p, bias_tabs, ctx_k, ctx_v, batch, seq_len):
    windowed = bias_tabs is not None
    v_col0 = 2 * NA_HEADS
    rows = seq_len // GRID_W
    if windowed:
        tq = NA_Q_ROWS * GRID_W
        nq = seq_len // tq
        lc = ctx_k.shape[0] // batch
        nk = NA_WIN_ROWS * GRID_W
        in_specs = [pl.BlockSpec((tq, LANES), lambda b, h, i: (b * nq + i, h)),
                    pl.BlockSpec((seq_len, LANES), lambda b, h, i: (b, h)),
                    pl.BlockSpec((seq_len, LANES), lambda b, h, i: (b, v_col0 + h)),
                    pl.BlockSpec((1, 1, tq, nk),
                                 lambda b, h, i: (jnp.where(i == 0, 0, jnp.where(i == nq - 1, 2, 1)), h, 0, 0)),
                    pl.BlockSpec((lc, LANES), lambda b, h, i: (b, h)),
                    pl.BlockSpec((lc, LANES), lambda b, h, i: (b, h))]
        args = [q_bf16, k_bf16, p, bias_tabs, ctx_k, ctx_v]
    else:
        tq = seq_len
        nq = 1
        in_specs = [pl.BlockSpec((tq, LANES), lambda b, h, i: (b, h)),
                    pl.BlockSpec((seq_len, LANES), lambda b, h, i: (b, h)),
                    pl.BlockSpec((seq_len, LANES), lambda b, h, i: (b, v_col0 + h))]
        args = [q_bf16, k_bf16, p]
    return pl.pallas_call(
        functools.partial(_na_attn_kernel, windowed=windowed, rows=rows),
        grid=(batch, NA_HEADS, nq),
        in_specs=in_specs,
        out_specs=pl.BlockSpec((tq, LANES), lambda b, h, i: (b * nq + i, h)),
        out_shape=jax.ShapeDtypeStruct((batch * seq_len, NA_HEADS * LANES), jnp.float32),
        compiler_params=pltpu.CompilerParams(dimension_semantics=("parallel", "parallel", "arbitrary"),
                                             vmem_limit_bytes=VMEM_LIMIT_BYTES),
        name="na_attention",
    )(*args)


def even_mixer(x, mods, norm_g, b, l, ev, ctx_kv, bias_tabs):
    (w_in, w_out, q_g, k_g, conv_w, conv_b, w1, b1, f1, w2, b2, f2, w3, decay, bias) = ev
    p = mm_norm(x, norm_g, mods[0], mods[1], w_in)
    q, k, kn = na_prep(p, q_g, k_g)
    if ctx_kv is None:
        oa = na_attention(q, k, p, None, None, None, b, l)
    else:
        lc = ctx_kv[0].shape[1]
        oa = na_attention(q, k, p, bias_tabs, ctx_kv[0].reshape(b * lc, MIX_WIDTH),
                          ctx_kv[1].reshape(b * lc, MIX_WIDTH), b, l)
    hb = p[:, 3 * MIX_WIDTH:].reshape(b, l, 3 * HY_WIDTH)
    ob = hyena(hb, conv_w, conv_b, (w1, b1, f1, w2, b2, f2, w3, decay), bias)
    out = mm_pair_res(oa, ob.reshape(b * l, HY_WIDTH), w_out, x, mods[2])
    shp = (b, l, NA_HEADS, HEAD_DIM)
    return out, kn.reshape(shp), p[:, 2 * MIX_WIDTH:3 * MIX_WIDTH].reshape(shp)


def odd_mixer(x, mods, norm_g, b, l, od, lam_init, h0_re, h0_im, ctx_kv, rope):
    (w_in, w_out, s5_prep, d_skip, glu_w, glu_b, q_g, k_g, lq1, lk1, lq2, lk2, subln_g) = od
    p = mm_norm(x, norm_g, mods[0], mods[1], w_in)
    oc, fr, fi = s5_mixer(p[:, :S5_WIDTH].reshape(b, l, S5_WIDTH), h0_re, h0_im, s5_prep, d_skip, glu_w, glu_b)
    lam = (jnp.exp(jnp.sum(lq1 * lk1).astype(jnp.float32))
           - jnp.exp(jnp.sum(lq2 * lk2).astype(jnp.float32)) + lam_init)
    cos, sin_signed = rope
    q, k, kn = diff_prep(p, q_g, k_g, cos, sin_signed, l, use_rope=ctx_kv is not None)
    gain = subln_g.astype(jnp.float32) * (1.0 - lam_init)
    if ctx_kv is None:
        o = diff_attention(q, k, p, None, None, lam, gain, b, l)
    else:
        lc = ctx_kv[0].shape[1]
        o = diff_attention(q, k, p, ctx_kv[0].reshape(b * lc, MIX_WIDTH), ctx_kv[1].reshape(b * lc, MIX_WIDTH),
                           lam, gain, b, l)
    out = mm_pair_res(oc.reshape(b * l, S5_WIDTH), o, w_out, x, mods[2])
    kn = kn.reshape(b, l, DIFF_HEADS, 2, DIFF_QK_DIM)
    v = p[:, S5_WIDTH + 2 * MIX_WIDTH:].reshape(b, l, DIFF_HEADS, DIFF_V_DIM)
    return out, kn, v, fr, fi


PEER_TOKEN_TILE = 512
PEER_EXPERT_TILE = 1024
PEER_SUB_EXPERTS = 256


def _top16_rows(cur, iota):
    n = cur.shape[0]
    tops, idxs = [], []
    for _ in range(PEER_TOPK):
        m = jnp.max(cur, axis=0, keepdims=True)
        first = jnp.min(jnp.where(cur == m, iota, n), axis=0, keepdims=True)
        tops.append(m)
        idxs.append(first)
        cur = jnp.where(iota == first, -jnp.inf, cur)
    return tops, idxs


def _router_kernel(q_ref, keys_ref, m_ref, r1_ref, e0_ref, e1_ref, st_ref):
    tt = q_ref.shape[0]
    half = PEER_QUERY_DIM // 2
    for h in range(PEER_HEADS):
        for s in range(2):
            qs = q_ref[:, (2 * h + s) * half:(2 * h + s + 1) * half].astype(jnp.bfloat16)
            st = lax.dot_general(keys_ref[h, s], qs, (((1,), (1,)), ((), ())),
                                 preferred_element_type=jnp.float32)
            st_ref[2 * h + s] = st

    iota128 = lax.broadcasted_iota(jnp.int32, (PEER_N_KEYS, LANES), 0)
    iota8 = lax.broadcasted_iota(jnp.int32, (SUBLANES, LANES), 0)
    iota_cand = lax.broadcasted_iota(jnp.int32, (PEER_TOPK + SUBLANES * SUBLANES, LANES), 0)
    n_chunks = tt // LANES

    def body(i, carry):
        h = i // n_chunks
        lane0 = pl.multiple_of((i % n_chunks) * LANES, LANES)
        s0 = st_ref[2 * h, :, pl.ds(lane0, LANES)]
        s1 = st_ref[2 * h + 1, :, pl.ds(lane0, LANES)]
        a, a_idx = _top16_rows(s0, iota128)
        b, b_idx = _top16_rows(s1, iota128)
        bmat = jnp.concatenate(b, axis=0)
        sums = [a[k] + bmat for k in range(PEER_TOPK)]
        cand = [sums[0], sums[1][:SUBLANES]]
        for k in range(2, SUBLANES):
            cand.append(jnp.where(iota8 < PEER_TOPK // (k + 1), sums[k][:SUBLANES], -jnp.inf))
        cand.append(jnp.concatenate(a[SUBLANES:], axis=0) + b[0])
        f, _ = _top16_rows(jnp.concatenate(cand, axis=0), iota_cand)
        tau = f[PEER_TOPK - 1]
        z = jnp.ones_like(f[0])
        for k in range(1, PEER_TOPK):
            z = z + jnp.exp(f[k] - f[0])
        m = jnp.zeros((PEER_N_KEYS, LANES), jnp.float32)
        r1 = jnp.full((PEER_N_KEYS, LANES), float(PEER_TOPK), jnp.float32)
        for k in range(PEER_TOPK):
            m_k = jnp.sum(jnp.where(sums[k] >= tau, 1.0, 0.0), axis=0, keepdims=True)
            m = jnp.where(iota128 == a_idx[k], m_k, m)
            r1 = jnp.where(iota128 == b_idx[k], float(k), r1)
        m_ref[h, :, pl.ds(lane0, LANES)] = m
        r1_ref[h, :, pl.ds(lane0, LANES)] = r1.astype(jnp.bfloat16)
        e0_ref[h, :, pl.ds(lane0, LANES)] = jnp.exp(s0 - a[0]) / z
        e1_ref[h, :, pl.ds(lane0, LANES)] = jnp.exp(s1 - b[0]).astype(jnp.bfloat16)
        return carry

    lax.fori_loop(0, PEER_HEADS * n_chunks, body, 0)


def peer_router(q, keys_bf16, tt):
    t = q.shape[0]
    f32 = jnp.float32
    big = jax.ShapeDtypeStruct((PEER_HEADS, PEER_N_KEYS, t), f32)
    half = jax.ShapeDtypeStruct((PEER_HEADS, PEER_N_KEYS, t), jnp.bfloat16)
    blk = pl.BlockSpec((PEER_HEADS, PEER_N_KEYS, tt), lambda i: (0, 0, i))
    return pl.pallas_call(
        _router_kernel,
        grid=(t // tt,),
        in_specs=[pl.BlockSpec((tt, q.shape[1]), lambda i: (i, 0)),
                  pl.BlockSpec(keys_bf16.shape, lambda i: (0, 0, 0, 0))],
        out_specs=[blk, blk, blk, blk],
        out_shape=[big, half, big, half],
        scratch_shapes=[pltpu.VMEM((2 * PEER_HEADS, PEER_N_KEYS, tt), f32)],
        compiler_params=pltpu.CompilerParams(dimension_semantics=("parallel",),
                                             vmem_limit_bytes=VMEM_LIMIT_BYTES),
        name="peer_router",
    )(q, keys_bf16)


def _dense_kernel(xt_ref, u_ref, vt_ref, m_ref, r1_ref, e0_ref, e1_ref, res_ref, gate_ref,
                  o_ref, at_ref, wt_ref, acc_ref):
    e = pl.program_id(1)
    n_e = pl.num_programs(1)
    eb = u_ref.shape[0]
    tt = xt_ref.shape[1]
    n_chunks = tt // LANES
    n_sub = eb // PEER_SUB_EXPERTS
    i1_per_sub = PEER_SUB_EXPERTS // PEER_N_KEYS

    @pl.when(e == 0)
    def _():
        acc_ref[...] = jnp.zeros_like(acc_ref)

    at_ref[...] = jnp.dot(u_ref[...], xt_ref[...], preferred_element_type=jnp.float32)
    for sb in range(n_sub):
        for c in range(n_chunks):
            lanes = slice(c * LANES, (c + 1) * LANES)
            for k in range(i1_per_sub):
                i1 = sb * i1_per_sub + k
                zero = jnp.zeros((PEER_N_KEYS, LANES), jnp.bfloat16)
                g = zero
                for h in range(PEER_HEADS):
                    mrow = m_ref[h, i1:i1 + 1, lanes].astype(jnp.bfloat16)
                    e0row = e0_ref[h, i1:i1 + 1, lanes].astype(jnp.bfloat16)
                    g = g + jnp.where(r1_ref[h, :, lanes] < mrow, e0row * e1_ref[h, :, lanes], zero)
                r = slice(i1 * PEER_N_KEYS, (i1 + 1) * PEER_N_KEYS)
                wt_ref[r, lanes] = _gelu_tanh(at_ref[r, lanes]).astype(jnp.bfloat16) * g
    acc_ref[...] += jnp.dot(vt_ref[0], wt_ref[...], preferred_element_type=jnp.float32)

    @pl.when(e == n_e - 1)
    def _():
        o_ref[...] = res_ref[...] + gate_ref[0] * acc_ref[...].T


def peer_dense(xt_bf16, u_bf16, vt_bf16, m, r1, e0, e1, resid, gate, tokens_per_gate, tt, eb):
    d, t = xt_bf16.shape
    n_exp = u_bf16.shape[0]
    assert t % tt == 0 and n_exp % eb == 0 and tokens_per_gate % tt == 0
    assert eb // PEER_N_KEYS == SUBLANES
    tiles_per_gate = tokens_per_gate // tt
    rblk = pl.BlockSpec((PEER_HEADS, PEER_N_KEYS, tt), lambda i, j: (0, 0, i))
    gblk = pl.BlockSpec((PEER_HEADS, SUBLANES, tt), lambda i, j: (0, j, i))
    return pl.pallas_call(
        _dense_kernel,
        grid=(t // tt, n_exp // eb),
        in_specs=[pl.BlockSpec((d, tt), lambda i, j: (0, i)),
                  pl.BlockSpec((eb, d), lambda i, j: (j, 0)),
                  pl.BlockSpec((1, d, eb), lambda i, j: (j, 0, 0)),
                  gblk, rblk, gblk, rblk,
                  pl.BlockSpec((tt, d), lambda i, j: (i, 0)),
                  pl.BlockSpec((1, 1, d), lambda i, j: (i // tiles_per_gate, 0, 0))],
        out_specs=pl.BlockSpec((tt, d), lambda i, j: (i, 0)),
        out_shape=jax.ShapeDtypeStruct((t, d), jnp.float32),
        scratch_shapes=[pltpu.VMEM((eb, tt), jnp.float32),
                        pltpu.VMEM((eb, tt), jnp.bfloat16),
                        pltpu.VMEM((d, tt), jnp.float32)],
        compiler_params=pltpu.CompilerParams(dimension_semantics=("parallel", "arbitrary"),
                                             vmem_limit_bytes=VMEM_LIMIT_BYTES),
        name="peer_dense",
    )(xt_bf16, u_bf16, vt_bf16, m, r1, e0, e1, resid, gate)


def peer_layer(x, norm_g, mods, w_q, keys, u_bf16, vt_bf16):
    t, d = x.shape
    q, h = mm_norm(x, norm_g, mods[3], mods[4], w_q, emit_h=True)
    gate = mods[5]
    m, r1, e0, e1 = peer_router(q, keys.astype(jnp.bfloat16), PEER_TOKEN_TILE)
    return peer_dense(h.T, u_bf16, vt_bf16, m, r1, e0, e1, x, gate, t // gate.shape[0],
                      PEER_TOKEN_TILE, PEER_EXPERT_TILE)


def kernel(x_prompt, x_sample, c, cache_na_k, cache_na_v, cache_diff_k, cache_diff_v, state_s5_re, state_s5_im, c_ctx, mod_w, mod_b, norm_mix_g, norm_ffn_g, ev_w_in, ev_w_out, na_q_g, na_k_g, na_rpb, hy_conv_w, hy_conv_b, hy_w1, hy_b1, hy_f1, hy_w2, hy_b2, hy_f2, hy_w3, hy_decay, hy_bias, od_w_in, od_w_out, s5_lam_re, s5_lam_im, s5_log_dt, s5_b_re, s5_b_im, s5_c_re, s5_c_im, s5_d, s5_glu_w, s5_glu_b, diff_q_g, diff_k_g, diff_lq1, diff_lk1, diff_lq2, diff_lk2, diff_subln_g, peer_w_q, peer_keys, peer_u, peer_v):
    bp, lp, d = x_prompt.shape
    bs, ls, _ = x_sample.shape
    rope_p, rope_s = rope_tables(lp), rope_tables(ls)
    xp, xs = x_prompt.reshape(bp * lp, d), x_sample.reshape(bs * ls, d)
    new_na_k, new_na_v, new_dk, new_dv, new_sr, new_si = [], [], [], [], [], []
    for i in range(DEPTH):
        mp = adaln(c_ctx[None, :], mod_w[i], mod_b[i])
        ms = adaln(c, mod_w[i], mod_b[i])
        j = i // 2
        if i % 2 == 0:
            ev = (ev_w_in[j], ev_w_out[j], na_q_g[j], na_k_g[j], hy_conv_w[j], hy_conv_b[j],
                  hy_w1[j], hy_b1[j], hy_f1[j], hy_w2[j], hy_b2[j], hy_f2[j], hy_w3[j], hy_decay[j], hy_bias[j])
            bias_tabs = na_bias_tables(na_rpb[j], ls // GRID_W)
            xp, kp, vp = even_mixer(xp, mp, norm_mix_g[i], bp, lp, ev, None, None)
            xs, _, _ = even_mixer(xs, ms, norm_mix_g[i], bs, ls, ev, (cache_na_k[:, j], cache_na_v[:, j]), bias_tabs)
            new_na_k.append(kp)
            new_na_v.append(vp)
        else:
            lam_init = 0.8 - 0.6 * math.exp(-0.3 * i)
            s5_prep = s5_prepare(s5_lam_re[j], s5_lam_im[j], s5_log_dt[j], s5_b_re[j], s5_b_im[j],
                                 s5_c_re[j], s5_c_im[j])
            od = (od_w_in[j], od_w_out[j], s5_prep, s5_d[j], s5_glu_w[j], s5_glu_b[j], diff_q_g[j], diff_k_g[j],
                  diff_lq1[j], diff_lk1[j], diff_lq2[j], diff_lk2[j], diff_subln_g[j])
            xp, kp, vp, sr, si = odd_mixer(xp, mp, norm_mix_g[i], bp, lp, od, lam_init, None, None, None, rope_p)
            xs, _, _, _, _ = odd_mixer(xs, ms, norm_mix_g[i], bs, ls, od, lam_init, state_s5_re[:, j],
                                       state_s5_im[:, j], (cache_diff_k[:, j], cache_diff_v[:, j]), rope_s)
            new_dk.append(kp)
            new_dv.append(vp)
            new_sr.append(sr)
            new_si.append(si)
        u_bf16 = peer_u[i].astype(jnp.bfloat16)
        vt_bf16 = jnp.swapaxes(peer_v[i].reshape(-1, PEER_EXPERT_TILE, d), 1, 2).astype(jnp.bfloat16)
        xp = peer_layer(xp, norm_ffn_g[i], mp, peer_w_q[i], peer_keys[i], u_bf16, vt_bf16)
        xs = peer_layer(xs, norm_ffn_g[i], ms, peer_w_q[i], peer_keys[i], u_bf16, vt_bf16)
    return (xp.reshape(bp, lp, d), xs.reshape(bs, ls, d), jnp.stack(new_na_k, axis=1), jnp.stack(new_na_v, axis=1),
            jnp.stack(new_dk, axis=1), jnp.stack(new_dv, axis=1),
            jnp.stack(new_sr, axis=1), jnp.stack(new_si, axis=1))
```

```python
import functools
import math

import jax
import jax.numpy as jnp
from jax import lax
from jax.experimental import pallas as pl
from jax.experimental.pallas import tpu as pltpu

D_MODEL = 2048
DEPTH = 2
GRID_W = 64
MIX_WIDTH = D_MODEL // 2
HEAD_DIM = 128
NA_HEADS = MIX_WIDTH // HEAD_DIM
NA_KH = 8
NA_KW = 16
HY_WIDTH = MIX_WIDTH
HY_ORDER = 2
HY_POS_EMB = 33
S5_WIDTH = MIX_WIDTH
S5_GROUP = 16
S5_GROUPS = S5_WIDTH // S5_GROUP
S5_STATE = 64
DIFF_HEADS = MIX_WIDTH // HEAD_DIM
DIFF_QK_DIM = HEAD_DIM // 2
DIFF_V_DIM = HEAD_DIM
ROPE_BASE = 10000.0
PEER_HEADS = 8
PEER_N_KEYS = 128
PEER_QUERY_DIM = 256
PEER_TOPK = 16
NORM_EPS = 1e-6
NEG_INF = -1e30

LANES = 128
SUBLANES = 8
VMEM_LIMIT_BYTES = 56 * 1024 * 1024


MM_ROW_TILE = 1024
MM_COL_TILE = 512


def _mm_norm_kernel(x_ref, g_ref, sc_ref, sh_ref, w_ref, *out_refs, emit_h):
    if emit_h:
        o_ref, h_ref, hs_ref = out_refs
    else:
        o_ref, hs_ref = out_refs

    @pl.when(pl.program_id(1) == 0)
    def _():
        x = x_ref[...]
        y = x * lax.rsqrt(jnp.mean(x * x, axis=-1, keepdims=True) + NORM_EPS) * g_ref[...]
        hs_ref[...] = (y * (1.0 + sc_ref[0]) + sh_ref[0]).astype(jnp.bfloat16)
        if emit_h:
            h_ref[...] = hs_ref[...]

    o_ref[...] = jnp.dot(hs_ref[...], w_ref[...].astype(jnp.bfloat16), preferred_element_type=jnp.float32)


def mm_norm(x, norm_g, shift, scale, w, emit_h=False):
    m, k = x.shape
    n = w.shape[1]
    tm, tn = min(MM_ROW_TILE, m), min(MM_COL_TILE, n)
    nb = scale.shape[0]
    assert m % tm == 0 and n % tn == 0 and (m // nb) % tm == 0
    tiles_per_mod = (m // nb) // tm
    mod = pl.BlockSpec((1, 1, k), lambda i, j: (i // tiles_per_mod, 0, 0))
    out_specs = [pl.BlockSpec((tm, tn), lambda i, j: (i, j))]
    out_shape = [jax.ShapeDtypeStruct((m, n), jnp.float32)]
    if emit_h:
        out_specs.append(pl.BlockSpec((tm, k), lambda i, j: (i, 0)))
        out_shape.append(jax.ShapeDtypeStruct((m, k), jnp.bfloat16))
    res = pl.pallas_call(
        functools.partial(_mm_norm_kernel, emit_h=emit_h),
        grid=(m // tm, n // tn),
        in_specs=[pl.BlockSpec((tm, k), lambda i, j: (i, 0)),
                  pl.BlockSpec((1, k), lambda i, j: (0, 0)),
                  mod, mod,
                  pl.BlockSpec((k, tn), lambda i, j: (0, j))],
        out_specs=out_specs,
        out_shape=out_shape,
        scratch_shapes=[pltpu.VMEM((tm, k), jnp.bfloat16)],
        compiler_params=pltpu.CompilerParams(dimension_semantics=("parallel", "arbitrary"),
                                             vmem_limit_bytes=VMEM_LIMIT_BYTES),
        name="mm_norm",
    )(x, norm_g.astype(jnp.float32).reshape(1, k), scale, shift, w)
    return res if emit_h else res[0]


def _mm_pair_res_kernel(a_ref, b_ref, wa_ref, wb_ref, r_ref, g_ref, o_ref):
    y = (jnp.dot(a_ref[...].astype(jnp.bfloat16), wa_ref[...].astype(jnp.bfloat16), preferred_element_type=jnp.float32)
         + jnp.dot(b_ref[...].astype(jnp.bfloat16), wb_ref[...].astype(jnp.bfloat16), preferred_element_type=jnp.float32))
    o_ref[...] = r_ref[...] + g_ref[0] * y


def mm_pair_res(a, b, w, resid, gate):
    m, kh = a.shape
    n = w.shape[1]
    tm, tn = min(MM_ROW_TILE, m), min(MM_COL_TILE, n)
    nb = gate.shape[0]
    assert m % tm == 0 and n % tn == 0 and (m // nb) % tm == 0 and w.shape[0] == 2 * kh
    tiles_per_mod = (m // nb) // tm
    return pl.pallas_call(
        _mm_pair_res_kernel,
        grid=(m // tm, n // tn),
        in_specs=[pl.BlockSpec((tm, kh), lambda i, j: (i, 0)),
                  pl.BlockSpec((tm, kh), lambda i, j: (i, 0)),
                  pl.BlockSpec((kh, tn), lambda i, j: (0, j)),
                  pl.BlockSpec((kh, tn), lambda i, j: (1, j)),
                  pl.BlockSpec((tm, tn), lambda i, j: (i, j)),
                  pl.BlockSpec((1, 1, tn), lambda i, j: (i // tiles_per_mod, 0, j))],
        out_specs=pl.BlockSpec((tm, tn), lambda i, j: (i, j)),
        out_shape=jax.ShapeDtypeStruct((m, n), jnp.float32),
        compiler_params=pltpu.CompilerParams(dimension_semantics=("parallel", "arbitrary"),
                                             vmem_limit_bytes=VMEM_LIMIT_BYTES),
        name="mm_pair_res",
    )(a, b, w, w, resid, gate)


def adaln(cond, w, b):
    m = jax.nn.silu(cond) @ w + b
    return jnp.split(m[:, None, :], 6, axis=-1)


def hyena_filters(l, w1, b1, f1, w2, b2, f2, w3, decay):
    t = jnp.linspace(0.0, 1.0, l, dtype=jnp.float32)[:, None]
    bands = (HY_POS_EMB - 1) // 2
    w_ang = 2.0 * math.pi * jnp.arange(l, dtype=jnp.float32)[:, None] / l
    freqs = jnp.linspace(1e-4, bands - 1, bands, dtype=jnp.float32)[None, :]
    z = jnp.concatenate([t, jnp.cos(freqs * w_ang), -jnp.sin(freqs * w_ang)], axis=-1)
    h = jnp.sin(f1 * (z @ w1 + b1))
    h = jnp.sin(f2 * (h @ w2 + b2))
    h = (h @ w3).reshape(l, 2, HY_ORDER, HY_WIDTH).astype(jnp.float32)
    h = h * jnp.exp(-t.reshape(l, 1, 1, 1) * jnp.abs(decay.astype(jnp.float32)))
    h_f, h_b = h[:, 0], h[:, 1]
    zero = jnp.zeros((1, HY_ORDER, HY_WIDTH), jnp.float32)
    return jnp.concatenate([h_f, zero, h_b[1:][::-1]], axis=0)


def _split_bf16(x):
    hi = x.astype(jnp.bfloat16)
    lo = (x - hi.astype(jnp.float32)).astype(jnp.bfloat16)
    return hi, lo


def _dot3(a_hi, a_lo, x):
    m = a_hi.shape[0]
    xh, xl = _split_bf16(x)
    r = jnp.dot(jnp.concatenate([a_hi, a_lo], axis=0), xh, preferred_element_type=jnp.float32)
    return r[:m] + r[m:] + jnp.dot(a_hi, xl, preferred_element_type=jnp.float32)


HY_STEP_ROWS = 256


def hyena_factors(l):
    n = 2 * l
    n2 = 64 if n >= 8192 else 16
    return n // n2, n2


def hyena_tables(l):
    n = 2 * l
    n1, n2 = hyena_factors(l)
    t = (n2 * jnp.arange(n1 // 2)[None, None, :] + jnp.arange(n2)[:, None, None])
    k1 = jnp.arange(n1)[None, :, None]
    ang = (2.0 * math.pi / n) * ((t * k1) % n).astype(jnp.float32)
    fa = jnp.concatenate([jnp.cos(ang), -jnp.sin(ang)], axis=1)
    fc = jnp.swapaxes(fa, 1, 2) / n
    a2 = (2.0 * math.pi / n2) * ((jnp.arange(n2)[:, None] * jnp.arange(n2)[None, :]) % n2).astype(jnp.float32)
    c, s = jnp.cos(a2), jnp.sin(a2)
    fb = jnp.concatenate([jnp.concatenate([c, s], axis=1), jnp.concatenate([-s, c], axis=1)], axis=0)
    fbi = jnp.concatenate([jnp.concatenate([c, -s], axis=1), jnp.concatenate([s, c], axis=1)], axis=0)
    return tuple(_split_bf16(x) for x in (fa, fc, fb, fbi))


def hyena_spectrum(filt, l):
    n1, n2 = hyena_factors(l)
    kf = jnp.fft.fft(filt, axis=0)
    kf = jnp.stack([jnp.real(kf), jnp.imag(kf)], axis=0).astype(jnp.float32)
    kf = kf.reshape(2, n2, n1, HY_ORDER, HY_WIDTH)
    return jnp.transpose(kf, (3, 0, 2, 1, 4))


def _hy_stage_a_kernel(z_ref, fh_ref, fl_ref, o_ref):
    for i in range(z_ref.shape[0]):
        o_ref[i] = _dot3(fh_ref[0], fl_ref[0], z_ref[i])


def _hy_stage_b_kernel(s_ref, k_ref, fh_ref, fl_ref, gh_ref, gl_ref, o_ref):
    n2 = s_ref.shape[3]
    kr, ki = k_ref[0, 0], k_ref[1, 0]
    for i in range(s_ref.shape[0]):
        y = jnp.concatenate([s_ref[i, 0, 0], s_ref[i, 1, 0]], axis=0)
        z = _dot3(fh_ref[...], fl_ref[...], y)
        zr, zi = z[:n2], z[n2:]
        p = jnp.concatenate([zr * kr - zi * ki, zr * ki + zi * kr], axis=0)
        q = _dot3(gh_ref[...], gl_ref[...], p)
        o_ref[i, 0, 0] = q[:n2]
        o_ref[i, 1, 0] = q[n2:]


def _hy_stage_c_kernel(q_ref, fh_ref, fl_ref, z_ref, g_ref, b_ref, o_ref):
    for i in range(q_ref.shape[0]):
        y = _dot3(fh_ref[0], fl_ref[0], q_ref[i])
        o_ref[i] = g_ref[i] * (y + z_ref[i] * b_ref[...])


def hyena_long_conv(z, z_col, gate, gate_col, bias_o, spec_o, tables, l):
    (fah, fal), (fch, fcl), (fbh, fbl), (fgh, fgl) = tables
    b = z.shape[0]
    c = HY_WIDTH
    n1, n2 = hyena_factors(l)
    h1 = n1 // 2
    bb = max(1, min(b, HY_STEP_ROWS // n1))
    assert b % bb == 0
    cp = pltpu.CompilerParams(dimension_semantics=("parallel", "parallel"), vmem_limit_bytes=VMEM_LIMIT_BYTES)
    zblocks = z.shape[-1] // c
    gblocks = gate.shape[-1] // c
    zv = z.reshape(b, h1, n2 * z.shape[-1])
    gv = gate.reshape(b, h1, n2 * gate.shape[-1])
    s1 = pl.pallas_call(
        _hy_stage_a_kernel,
        grid=(b // bb, n2),
        in_specs=[pl.BlockSpec((bb, h1, c), lambda i, j: (i, 0, j * zblocks + z_col)),
                  pl.BlockSpec((1, 2 * n1, h1), lambda i, j: (j, 0, 0)),
                  pl.BlockSpec((1, 2 * n1, h1), lambda i, j: (j, 0, 0))],
        out_specs=pl.BlockSpec((bb, 2 * n1, c), lambda i, j: (i, 0, j)),
        out_shape=jax.ShapeDtypeStruct((b, 2 * n1, n2 * c), jnp.float32),
        compiler_params=cp, name="hyena_stage_a",
    )(zv, fah, fal)
    s1 = s1.reshape(b, 2, n1, n2, c)
    mat = pl.BlockSpec((2 * n2, 2 * n2), lambda k, i: (0, 0))
    q = pl.pallas_call(
        _hy_stage_b_kernel,
        grid=(n1, b // bb),
        in_specs=[pl.BlockSpec((bb, 2, 1, n2, c), lambda k, i: (i, 0, k, 0, 0)),
                  pl.BlockSpec((2, 1, n2, c), lambda k, i: (0, k, 0, 0)),
                  mat, mat, mat, mat],
        out_specs=pl.BlockSpec((bb, 2, 1, n2, c), lambda k, i: (i, 0, k, 0, 0)),
        out_shape=jax.ShapeDtypeStruct((b, 2, n1, n2, c), jnp.float32),
        compiler_params=cp, name="hyena_stage_b",
    )(s1, spec_o, fbh, fbl, fgh, fgl)
    q = q.reshape(b, 2 * n1, n2 * c)
    out = pl.pallas_call(
        _hy_stage_c_kernel,
        grid=(b // bb, n2),
        in_specs=[pl.BlockSpec((bb, 2 * n1, c), lambda i, j: (i, 0, j)),
                  pl.BlockSpec((1, h1, 2 * n1), lambda i, j: (j, 0, 0)),
                  pl.BlockSpec((1, h1, 2 * n1), lambda i, j: (j, 0, 0)),
                  pl.BlockSpec((bb, h1, c), lambda i, j: (i, 0, j * zblocks + z_col)),
                  pl.BlockSpec((bb, h1, c), lambda i, j: (i, 0, j * gblocks + gate_col)),
                  pl.BlockSpec((1, c), lambda i, j: (0, 0))],
        out_specs=pl.BlockSpec((bb, h1, c), lambda i, j: (i, 0, j)),
        out_shape=jax.ShapeDtypeStruct((b, h1, n2 * c), jnp.float32),
        compiler_params=cp, name="hyena_stage_c",
    )(q, fch, fcl, zv, gv, bias_o.astype(jnp.float32).reshape(1, c))
    return out.reshape(b, l, c)


def hyena(u, conv_w, conv_b, filt, bias):
    b, l, _ = u.shape
    up = jnp.pad(u, ((0, 0), (1, 1), (0, 0)))
    u = up[:, :-2] * conv_w[0] + up[:, 1:-1] * conv_w[1] + up[:, 2:] * conv_w[2] + conv_b
    tables = hyena_tables(l)
    spec = hyena_spectrum(hyena_filters(l, *filt), l)
    z, z_col = u, 0
    for o in range(HY_ORDER):
        z = hyena_long_conv(z, z_col, u, 1 + o, bias[o], spec[o], tables, l)
        z_col = 0
    return z


S5_TILE_GROUPS = 8
S5_TILES = S5_GROUPS // S5_TILE_GROUPS
S5_TILE_IN = S5_TILE_GROUPS * S5_GROUP
S5_TILE_STATE = S5_TILE_GROUPS * S5_STATE
S5_ROWS_PER_STEP = 2048


def s5_prepare(lam_re, lam_im, log_dt, b_re, b_im, c_re, c_im):
    f32 = jnp.float32
    lam_re, lam_im = lam_re.astype(f32), lam_im.astype(f32)
    dt = jnp.exp(log_dt.astype(f32))[..., None]
    mag = jnp.exp(lam_re * dt)
    ar, ai = mag * jnp.cos(lam_im * dt), mag * jnp.sin(lam_im * dt)
    den = lam_re * lam_re + lam_im * lam_im
    cr = ((ar - 1.0) * lam_re + ai * lam_im) / den
    ci = (ai * lam_re - (ar - 1.0) * lam_im) / den
    bbr = cr[..., None] * b_re - ci[..., None] * b_im
    bbi = cr[..., None] * b_im + ci[..., None] * b_re
    eye = jnp.eye(S5_TILE_GROUPS, dtype=f32)

    def tile_in(bb):
        x = bb.reshape(2, S5_TILES, S5_TILE_GROUPS, S5_STATE, S5_GROUP)
        x = jnp.einsum('dtgnp,gh->dtgphn', x, eye)
        return x.reshape(2, S5_TILES, S5_TILE_IN, S5_TILE_STATE)

    def tile_out(cc):
        x = cc.astype(f32).reshape(2, S5_TILES, S5_TILE_GROUPS, S5_GROUP, S5_STATE)
        x = jnp.einsum('dtgpn,gh->dtgnhp', x, eye)
        return x.reshape(2, S5_TILES, S5_TILE_STATE, S5_TILE_IN)

    win = jnp.concatenate([tile_in(bbr), tile_in(bbi)], axis=-1).astype(jnp.bfloat16)
    wout = jnp.concatenate([tile_out(c_re), -tile_out(c_im)], axis=-2).astype(jnp.bfloat16)
    a = jnp.stack([ar.reshape(2, S5_TILES, S5_TILE_STATE), ai.reshape(2, S5_TILES, S5_TILE_STATE)], axis=2)
    return win, wout, a


def _s5_scan_kernel(u_ref, win_ref, wout_ref, a_ref, h0_ref, y_ref, fin_ref, bu_ref, st_ref, *, batch):
    d = pl.program_id(0)
    c = pl.program_id(2)
    n_c = pl.num_programs(2)
    ns = S5_TILE_STATE
    steps = u_ref.shape[0] // batch

    @pl.when(c == 0)
    def _():
        st_ref[...] = h0_ref[0, 0]

    bu_ref[...] = jnp.dot(u_ref[...].astype(jnp.bfloat16), win_ref[0, 0], preferred_element_type=jnp.float32)
    ar = jnp.broadcast_to(a_ref[0, 0, 0:1, :], (batch, ns))
    ai = jnp.broadcast_to(a_ref[0, 0, 1:2, :], (batch, ns))

    def step(t, carry):
        hr, hi = carry
        te = jnp.where(d == 0, t, steps - 1 - t)
        r0 = pl.multiple_of(te * batch, batch)
        nr = ar * hr - ai * hi + bu_ref[pl.ds(r0, batch), 0:ns]
        ni = ar * hi + ai * hr + bu_ref[pl.ds(r0, batch), ns:2 * ns]
        bu_ref[pl.ds(r0, batch), 0:ns] = nr
        bu_ref[pl.ds(r0, batch), ns:2 * ns] = ni
        return nr, ni

    hr, hi = lax.fori_loop(0, steps, step, (st_ref[:, 0:ns], st_ref[:, ns:2 * ns]), unroll=4)
    st_ref[:, 0:ns] = hr
    st_ref[:, ns:2 * ns] = hi
    y_ref[0] = jnp.dot(bu_ref[...].astype(jnp.bfloat16), wout_ref[0, 0], preferred_element_type=jnp.float32)

    @pl.when(c == n_c - 1)
    def _():
        fin_ref[0, 0] = st_ref[...]


def s5_scan(u_tm, win, wout, a, h0, batch):
    rows = u_tm.shape[0]
    r = min(S5_ROWS_PER_STEP, rows)
    assert rows % r == 0 and r % batch == 0
    n_c = rows // r

    def chunk(d, c):
        return c + d * (n_c - 1 - 2 * c)

    return pl.pallas_call(
        functools.partial(_s5_scan_kernel, batch=batch),
        grid=(2, S5_TILES, n_c),
        in_specs=[pl.BlockSpec((r, S5_TILE_IN), lambda d, j, c: (chunk(d, c), j)),
                  pl.BlockSpec((1, 1, S5_TILE_IN, 2 * S5_TILE_STATE), lambda d, j, c: (d, j, 0, 0)),
                  pl.BlockSpec((1, 1, 2 * S5_TILE_STATE, S5_TILE_IN), lambda d, j, c: (d, j, 0, 0)),
                  pl.BlockSpec((1, 1, 2, S5_TILE_STATE), lambda d, j, c: (d, j, 0, 0)),
                  pl.BlockSpec((1, 1, batch, 2 * S5_TILE_STATE), lambda d, j, c: (d, j, 0, 0))],
        out_specs=[pl.BlockSpec((1, r, S5_TILE_IN), lambda d, j, c: (d, chunk(d, c), j)),
                   pl.BlockSpec((1, 1, batch, 2 * S5_TILE_STATE), lambda d, j, c: (d, j, 0, 0))],
        out_shape=[jax.ShapeDtypeStruct((2, rows, S5_WIDTH), jnp.float32),
                   jax.ShapeDtypeStruct((2, S5_TILES, batch, 2 * S5_TILE_STATE), jnp.float32)],
        scratch_shapes=[pltpu.VMEM((r, 2 * S5_TILE_STATE), jnp.float32),
                        pltpu.VMEM((batch, 2 * S5_TILE_STATE), jnp.float32)],
        compiler_params=pltpu.CompilerParams(dimension_semantics=("parallel", "parallel", "arbitrary"),
                                             vmem_limit_bytes=VMEM_LIMIT_BYTES),
        name="s5_scan",
    )(u_tm, win, wout, a, h0)


def _gelu_tanh(x):
    return 0.5 * x * (1.0 + jnp.tanh(math.sqrt(2.0 / math.pi) * (x + 0.044715 * (x * x * x))))


def _s5_glu_kernel(u_ref, y_ref, d_ref, w_ref, b_ref, o_ref):
    y = d_ref[...] * u_ref[...] + y_ref[0] + y_ref[1]
    y = _gelu_tanh(y)
    z = jnp.dot(y.astype(jnp.bfloat16), w_ref[...], preferred_element_type=jnp.float32) + b_ref[...]
    o_ref[...] = y * (1.0 / (1.0 + jnp.exp(-z)))


def s5_glu(u_tm, y, d_skip, glu_w_bf16, glu_b, tr=1024):
    rows, w = u_tm.shape
    tr = min(tr, rows)
    assert rows % tr == 0
    return pl.pallas_call(
        _s5_glu_kernel,
        grid=(rows // tr,),
        in_specs=[pl.BlockSpec((tr, w), lambda i: (i, 0)),
                  pl.BlockSpec((2, tr, w), lambda i: (0, i, 0)),
                  pl.BlockSpec((1, w), lambda i: (0, 0)),
                  pl.BlockSpec((w, w), lambda i: (0, 0)),
                  pl.BlockSpec((1, w), lambda i: (0, 0))],
        out_specs=pl.BlockSpec((tr, w), lambda i: (i, 0)),
        out_shape=jax.ShapeDtypeStruct((rows, w), jnp.float32),
        compiler_params=pltpu.CompilerParams(dimension_semantics=("parallel",),
                                             vmem_limit_bytes=VMEM_LIMIT_BYTES),
        name="s5_glu",
    )(u_tm, y, d_skip.reshape(1, w), glu_w_bf16, glu_b.reshape(1, w))


def s5_mixer(u, h0_re, h0_im, prep, d_skip, glu_w, glu_b):
    win, wout, a = prep
    b, l, w = u.shape
    u_tm = jnp.swapaxes(u, 0, 1).reshape(l * b, w)
    if h0_re is None:
        h0 = jnp.zeros((2, S5_TILES, b, 2 * S5_TILE_STATE), jnp.float32)
    else:
        def tiles(h):
            return jnp.transpose(h.astype(jnp.float32).reshape(b, 2, S5_TILES, S5_TILE_STATE), (1, 2, 0, 3))
        h0 = jnp.concatenate([tiles(h0_re), tiles(h0_im)], axis=-1)
    y, fin = s5_scan(u_tm, win, wout, a, h0, b)
    out_tm = s5_glu(u_tm, y, d_skip, glu_w.astype(jnp.bfloat16), glu_b)
    out = jnp.swapaxes(out_tm.reshape(l, b, w), 0, 1)

    def untile(f):
        return jnp.transpose(f, (2, 0, 1, 3)).reshape(b, 2, S5_GROUPS, S5_STATE)
    return out, untile(fin[..., :S5_TILE_STATE]), untile(fin[..., S5_TILE_STATE:])


def _half_mean_matrix():
    i = lax.broadcasted_iota(jnp.int32, (LANES, LANES), 0) // DIFF_QK_DIM
    j = lax.broadcasted_iota(jnp.int32, (LANES, LANES), 1) // DIFF_QK_DIM
    return jnp.where(i == j, 1.0 / DIFF_QK_DIM, 0.0).astype(jnp.bfloat16)


def _rms_groups(x, gain, avg):
    xx = x * x
    hi = xx.astype(jnp.bfloat16)
    lo = (xx - hi.astype(jnp.float32)).astype(jnp.bfloat16)
    ms = (jnp.dot(hi, avg, preferred_element_type=jnp.float32)
          + jnp.dot(lo, avg, preferred_element_type=jnp.float32))
    return x * lax.rsqrt(ms + NORM_EPS) * gain


def _rope_lanes(y, cos, sin_signed, first_half):
    rot = jnp.where(first_half, pltpu.roll(y, LANES - 16, 1), pltpu.roll(y, 16, 1))
    return y * cos + rot * sin_signed


def _diff_prep_kernel(q_ref, k_ref, qg_ref, kg_ref, cos_ref, sin_ref, qo_ref, ko_ref, kn_ref, *, use_rope):
    avg = _half_mean_matrix()
    lane = lax.broadcasted_iota(jnp.int32, (1, LANES), 1)
    first_half = (lane % 32) < 16
    scale = DIFF_QK_DIM ** -0.5
    for h in range(DIFF_HEADS):
        cols = slice(h * LANES, (h + 1) * LANES)
        qn = _rms_groups(q_ref[:, cols], qg_ref[...], avg)
        kn = _rms_groups(k_ref[:, cols], kg_ref[...], avg)
        kn_ref[:, cols] = kn
        if use_rope:
            qn = _rope_lanes(qn, cos_ref[...], sin_ref[...], first_half)
            kn = _rope_lanes(kn, cos_ref[...], sin_ref[...], first_half)
        qo_ref[:, cols] = (qn * scale).astype(jnp.bfloat16)
        ko_ref[:, cols] = kn.astype(jnp.bfloat16)


def diff_prep(p, q_g, k_g, cos, sin_signed, seq_len, use_rope, tr=512):
    rows = p.shape[0]
    tr = min(tr, seq_len)
    assert seq_len % tr == 0
    w = DIFF_HEADS * LANES
    per_seq = seq_len // tr
    g2 = lambda g: jnp.tile(g.astype(jnp.float32), 2).reshape(1, LANES)
    return pl.pallas_call(
        functools.partial(_diff_prep_kernel, use_rope=use_rope),
        grid=(rows // tr,),
        in_specs=[pl.BlockSpec((tr, w), lambda i: (i, 1)),
                  pl.BlockSpec((tr, w), lambda i: (i, 2)),
                  pl.BlockSpec((1, LANES), lambda i: (0, 0)),
                  pl.BlockSpec((1, LANES), lambda i: (0, 0)),
                  pl.BlockSpec((tr, LANES), lambda i: (i % per_seq, 0)),
                  pl.BlockSpec((tr, LANES), lambda i: (i % per_seq, 0))],
        out_specs=[pl.BlockSpec((tr, w), lambda i: (i, 0))] * 3,
        out_shape=[jax.ShapeDtypeStruct((rows, w), jnp.bfloat16), jax.ShapeDtypeStruct((rows, w), jnp.bfloat16),
                   jax.ShapeDtypeStruct((rows, w), jnp.float32)],
        compiler_params=pltpu.CompilerParams(dimension_semantics=("parallel",),
                                             vmem_limit_bytes=VMEM_LIMIT_BYTES),
        name="diff_prep",
    )(p, p, g2(q_g), g2(k_g), cos, sin_signed)


def rope_tables(l):
    t = jnp.arange(l)
    row = (t // GRID_W).astype(jnp.float32)
    col = (t % GRID_W).astype(jnp.float32)
    nf = DIFF_QK_DIM // 4
    inv = ROPE_BASE ** (-jnp.arange(nf, dtype=jnp.float32) / nf)
    ang = jnp.stack([row[:, None] * inv, col[:, None] * inv], axis=1)
    ang = jnp.stack([ang, ang], axis=2).reshape(l, DIFF_QK_DIM)
    sign = jnp.where((jnp.arange(DIFF_QK_DIM) % 32) < 16, -1.0, 1.0)
    return jnp.tile(jnp.cos(ang), (1, 2)), jnp.tile(jnp.sin(ang) * sign, (1, 2))


def _diff_attn_kernel(*refs, has_ctx):
    if has_ctx:
        q_ref, k_ref, v_ref, ck_ref, cv_ref, lam_ref, g_ref, o_ref = refs
    else:
        q_ref, k_ref, v_ref, lam_ref, g_ref, o_ref = refs
    q = q_ref[...]
    lane = lax.broadcasted_iota(jnp.int32, (1, LANES), 1)
    zero = jnp.zeros_like(q)
    qs = (jnp.where(lane < DIFF_QK_DIM, q, zero), jnp.where(lane >= DIFF_QK_DIM, q, zero))
    nt = (((1,), (1,)), ((), ()))
    k = k_ref[...]
    ck = ck_ref[...].astype(jnp.bfloat16) if has_ctx else None
    lam = lam_ref[0:1, 0:1]
    w_self, w_ctx = None, None
    for i in range(2):
        s = lax.dot_general(qs[i], k, nt, preferred_element_type=jnp.float32)
        m = jnp.max(s, axis=-1, keepdims=True)
        if has_ctx:
            sc = lax.dot_general(qs[i], ck, nt, preferred_element_type=jnp.float32)
            m = jnp.maximum(m, jnp.max(sc, axis=-1, keepdims=True))
        p = jnp.exp(s - m)
        l = jnp.sum(p, axis=-1, keepdims=True)
        if has_ctx:
            pc = jnp.exp(sc - m)
            l = l + jnp.sum(pc, axis=-1, keepdims=True)
        coef = 1.0 / l if i == 0 else -lam / l
        w_self = p * coef if i == 0 else w_self + p * coef
        if has_ctx:
            w_ctx = pc * coef if i == 0 else w_ctx + pc * coef
    o = jnp.dot(w_self.astype(jnp.bfloat16), v_ref[...].astype(jnp.bfloat16), preferred_element_type=jnp.float32)
    if has_ctx:
        o = o + jnp.dot(w_ctx.astype(jnp.bfloat16), cv_ref[...].astype(jnp.bfloat16),
                        preferred_element_type=jnp.float32)
    ms = jnp.mean(o * o, axis=-1, keepdims=True)
    o_ref[...] = o * lax.rsqrt(ms + NORM_EPS) * g_ref[...]


def diff_attention(q_bf16, k_bf16, p, ctx_k, ctx_v, lam, gain, batch, seq_len, tq=256):
    tq = min(tq, seq_len)
    nq = seq_len // tq
    has_ctx = ctx_k is not None
    v_col0 = 3 * DIFF_HEADS
    in_specs = [pl.BlockSpec((tq, LANES), lambda b, h, i: (b * nq + i, h)),
                pl.BlockSpec((seq_len, LANES), lambda b, h, i: (b, h)),
                pl.BlockSpec((seq_len, LANES), lambda b, h, i: (b, v_col0 + h))]
    args = [q_bf16, k_bf16, p]
    if has_ctx:
        lc = ctx_k.shape[0] // batch
        in_specs += [pl.BlockSpec((lc, LANES), lambda b, h, i: (b, h))] * 2
        args += [ctx_k, ctx_v]
    in_specs += [pl.BlockSpec((1, LANES), lambda b, h, i: (0, 0))] * 2
    args += [jnp.broadcast_to(lam.astype(jnp.float32), (1, LANES)), gain.astype(jnp.float32).reshape(1, LANES)]
    return pl.pallas_call(
        functools.partial(_diff_attn_kernel, has_ctx=has_ctx),
        grid=(batch, DIFF_HEADS, nq),
        in_specs=in_specs,
        out_specs=pl.BlockSpec((tq, LANES), lambda b, h, i: (b * nq + i, h)),
        out_shape=jax.ShapeDtypeStruct((batch * seq_len, DIFF_HEADS * LANES), jnp.float32),
        compiler_params=pltpu.CompilerParams(dimension_semantics=("parallel", "parallel", "arbitrary"),
                                             vmem_limit_bytes=VMEM_LIMIT_BYTES),
        name="diff_attention",
    )(*args)


NA_Q_ROWS = 8
NA_WIN_ROWS = 16


def _na_prep_kernel(q_ref, k_ref, qg_ref, kg_ref, qo_ref, ko_ref, kn_ref):
    avg = jnp.full((LANES, LANES), 1.0 / HEAD_DIM, jnp.bfloat16)
    for h in range(NA_HEADS):
        cols = slice(h * LANES, (h + 1) * LANES)
        qo_ref[:, cols] = _rms_groups(q_ref[:, cols], qg_ref[...], avg).astype(jnp.bfloat16)
        kn = _rms_groups(k_ref[:, cols], kg_ref[...], avg)
        kn_ref[:, cols] = kn
        ko_ref[:, cols] = kn.astype(jnp.bfloat16)


def na_prep(p, q_g, k_g, tr=512):
    rows = p.shape[0]
    tr = min(tr, rows)
    w = NA_HEADS * LANES
    g1 = lambda g: g.astype(jnp.float32).reshape(1, LANES)
    return pl.pallas_call(
        _na_prep_kernel,
        grid=(rows // tr,),
        in_specs=[pl.BlockSpec((tr, w), lambda i: (i, 0)),
                  pl.BlockSpec((tr, w), lambda i: (i, 1)),
                  pl.BlockSpec((1, LANES), lambda i: (0, 0)),
                  pl.BlockSpec((1, LANES), lambda i: (0, 0))],
        out_specs=[pl.BlockSpec((tr, w), lambda i: (i, 0))] * 3,
        out_shape=[jax.ShapeDtypeStruct((rows, w), jnp.bfloat16), jax.ShapeDtypeStruct((rows, w), jnp.bfloat16),
                   jax.ShapeDtypeStruct((rows, w), jnp.float32)],
        compiler_params=pltpu.CompilerParams(dimension_semantics=("parallel",),
                                             vmem_limit_bytes=VMEM_LIMIT_BYTES),
        name="na_prep",
    )(p, p, g1(q_g), g1(k_g))


def na_bias_tables(rpb, rows):
    nblk = rows // NA_Q_ROWS
    cq = jnp.arange(GRID_W)[:, None]
    ck = jnp.arange(GRID_W)[None, :]
    col_start = jnp.clip(cq - NA_KW // 2, 0, GRID_W - NA_KW)
    col_ok = (ck >= col_start) & (ck < col_start + NA_KW)
    cidx = jnp.clip(ck - cq + NA_KW - 1, 0, 2 * NA_KW - 2)
    tabs = []
    for i in (0, 1, nblk - 1):
        base = min(max(NA_Q_ROWS * i - NA_KH // 2, 0), rows - NA_WIN_ROWS)
        r = NA_Q_ROWS * i + jnp.arange(NA_Q_ROWS)[:, None]
        rk = base + jnp.arange(NA_WIN_ROWS)[None, :]
        r0 = jnp.clip(r - NA_KH // 2, 0, rows - NA_KH)
        row_ok = (rk >= r0) & (rk < r0 + NA_KH)
        ridx = jnp.clip(rk - r + NA_KH - 1, 0, 2 * NA_KH - 2)
        b = rpb[:, ridx][:, :, :, cidx]
        ok = row_ok[:, :, None, None] & col_ok[None, None, :, :]
        b = jnp.where(ok[None], b.astype(jnp.float32), NEG_INF)
        tabs.append(jnp.transpose(b, (0, 1, 3, 2, 4)).reshape(NA_HEADS, NA_Q_ROWS * GRID_W, NA_WIN_ROWS * GRID_W))
    return jnp.stack(tabs, axis=0)


def _na_attn_kernel(*refs, windowed, rows):
    nt = (((1,), (1,)), ((), ()))
    scale = HEAD_DIM ** -0.5
    if windowed:
        q_ref, k_ref, v_ref, b_ref, ck_ref, cv_ref, o_ref = refs
        i = pl.program_id(2)
        base = jnp.clip(NA_Q_ROWS * i - NA_KH // 2, 0, rows - NA_WIN_ROWS)
        k0 = pl.multiple_of(base * GRID_W, GRID_W)
        nk = NA_WIN_ROWS * GRID_W
        q = q_ref[...]
        s = lax.dot_general(q, k_ref[pl.ds(k0, nk), :], nt, preferred_element_type=jnp.float32) * scale + b_ref[0, 0]
        sc = lax.dot_general(q, ck_ref[...].astype(jnp.bfloat16), nt, preferred_element_type=jnp.float32) * scale
        m = jnp.maximum(jnp.max(s, axis=-1, keepdims=True), jnp.max(sc, axis=-1, keepdims=True))
        p = jnp.exp(s - m)
        pc = jnp.exp(sc - m)
        l = jnp.sum(p, axis=-1, keepdims=True) + jnp.sum(pc, axis=-1, keepdims=True)
        o = (jnp.dot(p.astype(jnp.bfloat16), v_ref[pl.ds(k0, nk), :].astype(jnp.bfloat16),
                     preferred_element_type=jnp.float32)
             + jnp.dot(pc.astype(jnp.bfloat16), cv_ref[...].astype(jnp.bfloat16), preferred_element_type=jnp.float32))
    else:
        q_ref, k_ref, v_ref, o_ref = refs
        s = lax.dot_general(q_ref[...], k_ref[...], nt, preferred_element_type=jnp.float32) * scale
        m = jnp.max(s, axis=-1, keepdims=True)
        p = jnp.exp(s - m)
        l = jnp.sum(p, axis=-1, keepdims=True)
        o = jnp.dot(p.astype(jnp.bfloat16), v_ref[...].astype(jnp.bfloat16), preferred_element_type=jnp.float32)
    o_ref[...] = o / l


def na_attention(q_bf16, k_bf16, p, bias_tabs, ctx_k, ctx_v, batch, seq_len):
    windowed = bias_tabs is not None
    v_col0 = 2 * NA_HEADS
    rows = seq_len // GRID_W
    if windowed:
        tq = NA_Q_ROWS * GRID_W
        nq = seq_len // tq
        lc = ctx_k.shape[0] // batch
        nk = NA_WIN_ROWS * GRID_W
        in_specs = [pl.BlockSpec((tq, LANES), lambda b, h, i: (b * nq + i, h)),
                    pl.BlockSpec((seq_len, LANES), lambda b, h, i: (b, h)),
                    pl.BlockSpec((seq_len, LANES), lambda b, h, i: (b, v_col0 + h)),
                    pl.BlockSpec((1, 1, tq, nk),
                                 lambda b, h, i: (jnp.where(i == 0, 0, jnp.where(i == nq - 1, 2, 1)), h, 0, 0)),
                    pl.BlockSpec((lc, LANES), lambda b, h, i: (b, h)),
                    pl.BlockSpec((lc, LANES), lambda b, h, i: (b, h))]
        args = [q_bf16, k_bf16, p, bias_tabs, ctx_k, ctx_v]
    else:
        tq = seq_len
        nq = 1
        in_specs = [pl.BlockSpec((tq, LANES), lambda b, h, i: (b, h)),
                    pl.BlockSpec((seq_len, LANES), lambda b, h, i: (b, h)),
                    pl.BlockSpec((seq_len, LANES), lambda b, h, i: (b, v_col0 + h))]
        args = [q_bf16, k_bf16, p]
    return pl.pallas_call(
        functools.partial(_na_attn_kernel, windowed=windowed, rows=rows),
        grid=(batch, NA_HEADS, nq),
        in_specs=in_specs,
        out_specs=pl.BlockSpec((tq, LANES), lambda b, h, i: (b * nq + i, h)),
        out_shape=jax.ShapeDtypeStruct((batch * seq_len, NA_HEADS * LANES), jnp.float32),
        compiler_params=pltpu.CompilerParams(dimension_semantics=("parallel", "parallel", "arbitrary"),
                                             vmem_limit_bytes=VMEM_LIMIT_BYTES),
        name="na_attention",
    )(*args)


def even_mixer(x, mods, norm_g, b, l, ev, ctx_kv, bias_tabs):
    (w_in, w_out, q_g, k_g, conv_w, conv_b, w1, b1, f1, w2, b2, f2, w3, decay, bias) = ev
    p = mm_norm(x, norm_g, mods[0], mods[1], w_in)
    q, k, kn = na_prep(p, q_g, k_g)
    if ctx_kv is None:
        oa = na_attention(q, k, p, None, None, None, b, l)
    else:
        lc = ctx_kv[0].shape[1]
        oa = na_attention(q, k, p, bias_tabs, ctx_kv[0].reshape(b * lc, MIX_WIDTH),
                          ctx_kv[1].reshape(b * lc, MIX_WIDTH), b, l)
    hb = p[:, 3 * MIX_WIDTH:].reshape(b, l, 3 * HY_WIDTH)
    ob = hyena(hb, conv_w, conv_b, (w1, b1, f1, w2, b2, f2, w3, decay), bias)
    out = mm_pair_res(oa, ob.reshape(b * l, HY_WIDTH), w_out, x, mods[2])
    shp = (b, l, NA_HEADS, HEAD_DIM)
    return out, kn.reshape(shp), p[:, 2 * MIX_WIDTH:3 * MIX_WIDTH].reshape(shp)


def odd_mixer(x, mods, norm_g, b, l, od, lam_init, h0_re, h0_im, ctx_kv, rope):
    (w_in, w_out, s5_prep, d_skip, glu_w, glu_b, q_g, k_g, lq1, lk1, lq2, lk2, subln_g) = od
    p = mm_norm(x, norm_g, mods[0], mods[1], w_in)
    oc, fr, fi = s5_mixer(p[:, :S5_WIDTH].reshape(b, l, S5_WIDTH), h0_re, h0_im, s5_prep, d_skip, glu_w, glu_b)
    lam = (jnp.exp(jnp.sum(lq1 * lk1).astype(jnp.float32))
           - jnp.exp(jnp.sum(lq2 * lk2).astype(jnp.float32)) + lam_init)
    cos, sin_signed = rope
    q, k, kn = diff_prep(p, q_g, k_g, cos, sin_signed, l, use_rope=ctx_kv is not None)
    gain = subln_g.astype(jnp.float32) * (1.0 - lam_init)
    if ctx_kv is None:
        o = diff_attention(q, k, p, None, None, lam, gain, b, l)
    else:
        lc = ctx_kv[0].shape[1]
        o = diff_attention(q, k, p, ctx_kv[0].reshape(b * lc, MIX_WIDTH), ctx_kv[1].reshape(b * lc, MIX_WIDTH),
                           lam, gain, b, l)
    out = mm_pair_res(oc.reshape(b * l, S5_WIDTH), o, w_out, x, mods[2])
    kn = kn.reshape(b, l, DIFF_HEADS, 2, DIFF_QK_DIM)
    v = p[:, S5_WIDTH + 2 * MIX_WIDTH:].reshape(b, l, DIFF_HEADS, DIFF_V_DIM)
    return out, kn, v, fr, fi


PEER_TOKEN_TILE = 512
PEER_EXPERT_TILE = 1024
PEER_SUB_EXPERTS = 256


def _top16_rows(cur, iota):
    n = cur.shape[0]
    tops, idxs = [], []
    for _ in range(PEER_TOPK):
        m = jnp.max(cur, axis=0, keepdims=True)
        first = jnp.min(jnp.where(cur == m, iota, n), axis=0, keepdims=True)
        tops.append(m)
        idxs.append(first)
        cur = jnp.where(iota == first, -jnp.inf, cur)
    return tops, idxs


def _router_kernel(q_ref, keys_ref, m_ref, r1_ref, e0_ref, e1_ref, st_ref):
    tt = q_ref.shape[0]
    half = PEER_QUERY_DIM // 2
    for h in range(PEER_HEADS):
        for s in range(2):
            qs = q_ref[:, (2 * h + s) * half:(2 * h + s + 1) * half].astype(jnp.bfloat16)
            st = lax.dot_general(keys_ref[h, s], qs, (((1,), (1,)), ((), ())),
                                 preferred_element_type=jnp.float32)
            st_ref[2 * h + s] = st

    iota128 = lax.broadcasted_iota(jnp.int32, (PEER_N_KEYS, LANES), 0)
    iota8 = lax.broadcasted_iota(jnp.int32, (SUBLANES, LANES), 0)
    iota_cand = lax.broadcasted_iota(jnp.int32, (PEER_TOPK + SUBLANES * SUBLANES, LANES), 0)
    n_chunks = tt // LANES

    def body(i, carry):
        h = i // n_chunks
        lane0 = pl.multiple_of((i % n_chunks) * LANES, LANES)
        s0 = st_ref[2 * h, :, pl.ds(lane0, LANES)]
        s1 = st_ref[2 * h + 1, :, pl.ds(lane0, LANES)]
        a, a_idx = _top16_rows(s0, iota128)
        b, b_idx = _top16_rows(s1, iota128)
        bmat = jnp.concatenate(b, axis=0)
        sums = [a[k] + bmat for k in range(PEER_TOPK)]
        cand = [sums[0], sums[1][:SUBLANES]]
        for k in range(2, SUBLANES):
            cand.append(jnp.where(iota8 < PEER_TOPK // (k + 1), sums[k][:SUBLANES], -jnp.inf))
        cand.append(jnp.concatenate(a[SUBLANES:], axis=0) + b[0])
        f, _ = _top16_rows(jnp.concatenate(cand, axis=0), iota_cand)
        tau = f[PEER_TOPK - 1]
        z = jnp.ones_like(f[0])
        for k in range(1, PEER_TOPK):
            z = z + jnp.exp(f[k] - f[0])
        m = jnp.zeros((PEER_N_KEYS, LANES), jnp.float32)
        r1 = jnp.full((PEER_N_KEYS, LANES), float(PEER_TOPK), jnp.float32)
        for k in range(PEER_TOPK):
            m_k = jnp.sum(jnp.where(sums[k] >= tau, 1.0, 0.0), axis=0, keepdims=True)
            m = jnp.where(iota128 == a_idx[k], m_k, m)
            r1 = jnp.where(iota128 == b_idx[k], float(k), r1)
        m_ref[h, :, pl.ds(lane0, LANES)] = m
        r1_ref[h, :, pl.ds(lane0, LANES)] = r1.astype(jnp.bfloat16)
        e0_ref[h, :, pl.ds(lane0, LANES)] = jnp.exp(s0 - a[0]) / z
        e1_ref[h, :, pl.ds(lane0, LANES)] = jnp.exp(s1 - b[0]).astype(jnp.bfloat16)
        return carry

    lax.fori_loop(0, PEER_HEADS * n_chunks, body, 0)


def peer_router(q, keys_bf16, tt):
    t = q.shape[0]
    f32 = jnp.float32
    big = jax.ShapeDtypeStruct((PEER_HEADS, PEER_N_KEYS, t), f32)
    half = jax.ShapeDtypeStruct((PEER_HEADS, PEER_N_KEYS, t), jnp.bfloat16)
    blk = pl.BlockSpec((PEER_HEADS, PEER_N_KEYS, tt), lambda i: (0, 0, i))
    return pl.pallas_call(
        _router_kernel,
        grid=(t // tt,),
        in_specs=[pl.BlockSpec((tt, q.shape[1]), lambda i: (i, 0)),
                  pl.BlockSpec(keys_bf16.shape, lambda i: (0, 0, 0, 0))],
        out_specs=[blk, blk, blk, blk],
        out_shape=[big, half, big, half],
        scratch_shapes=[pltpu.VMEM((2 * PEER_HEADS, PEER_N_KEYS, tt), f32)],
        compiler_params=pltpu.CompilerParams(dimension_semantics=("parallel",),
                                             vmem_limit_bytes=VMEM_LIMIT_BYTES),
        name="peer_router",
    )(q, keys_bf16)


def _dense_kernel(xt_ref, u_ref, vt_ref, m_ref, r1_ref, e0_ref, e1_ref, res_ref, gate_ref,
                  o_ref, at_ref, wt_ref, acc_ref):
    e = pl.program_id(1)
    n_e = pl.num_programs(1)
    eb = u_ref.shape[0]
    tt = xt_ref.shape[1]
    n_chunks = tt // LANES
    n_sub = eb // PEER_SUB_EXPERTS
    i1_per_sub = PEER_SUB_EXPERTS // PEER_N_KEYS

    @pl.when(e == 0)
    def _():
        acc_ref[...] = jnp.zeros_like(acc_ref)

    at_ref[...] = jnp.dot(u_ref[...], xt_ref[...], preferred_element_type=jnp.float32)
    for sb in range(n_sub):
        for c in range(n_chunks):
            lanes = slice(c * LANES, (c + 1) * LANES)
            for k in range(i1_per_sub):
                i1 = sb * i1_per_sub + k
                zero = jnp.zeros((PEER_N_KEYS, LANES), jnp.bfloat16)
                g = zero
                for h in range(PEER_HEADS):
                    mrow = m_ref[h, i1:i1 + 1, lanes].astype(jnp.bfloat16)
                    e0row = e0_ref[h, i1:i1 + 1, lanes].astype(jnp.bfloat16)
                    g = g + jnp.where(r1_ref[h, :, lanes] < mrow, e0row * e1_ref[h, :, lanes], zero)
                r = slice(i1 * PEER_N_KEYS, (i1 + 1) * PEER_N_KEYS)
                wt_ref[r, lanes] = _gelu_tanh(at_ref[r, lanes]).astype(jnp.bfloat16) * g
    acc_ref[...] += jnp.dot(vt_ref[0], wt_ref[...], preferred_element_type=jnp.float32)

    @pl.when(e == n_e - 1)
    def _():
        o_ref[...] = res_ref[...] + gate_ref[0] * acc_ref[...].T


def peer_dense(xt_bf16, u_bf16, vt_bf16, m, r1, e0, e1, resid, gate, tokens_per_gate, tt, eb):
    d, t = xt_bf16.shape
    n_exp = u_bf16.shape[0]
    assert t % tt == 0 and n_exp % eb == 0 and tokens_per_gate % tt == 0
    assert eb // PEER_N_KEYS == SUBLANES
    tiles_per_gate = tokens_per_gate // tt
    rblk = pl.BlockSpec((PEER_HEADS, PEER_N_KEYS, tt), lambda i, j: (0, 0, i))
    gblk = pl.BlockSpec((PEER_HEADS, SUBLANES, tt), lambda i, j: (0, j, i))
    return pl.pallas_call(
        _dense_kernel,
        grid=(t // tt, n_exp // eb),
        in_specs=[pl.BlockSpec((d, tt), lambda i, j: (0, i)),
                  pl.BlockSpec((eb, d), lambda i, j: (j, 0)),
                  pl.BlockSpec((1, d, eb), lambda i, j: (j, 0, 0)),
                  gblk, rblk, gblk, rblk,
                  pl.BlockSpec((tt, d), lambda i, j: (i, 0)),
                  pl.BlockSpec((1, 1, d), lambda i, j: (i // tiles_per_gate, 0, 0))],
        out_specs=pl.BlockSpec((tt, d), lambda i, j: (i, 0)),
        out_shape=jax.ShapeDtypeStruct((t, d), jnp.float32),
        scratch_shapes=[pltpu.VMEM((eb, tt), jnp.float32),
                        pltpu.VMEM((eb, tt), jnp.bfloat16),
                        pltpu.VMEM((d, tt), jnp.float32)],
        compiler_params=pltpu.CompilerParams(dimension_semantics=("parallel", "arbitrary"),
                                             vmem_limit_bytes=VMEM_LIMIT_BYTES),
        name="peer_dense",
    )(xt_bf16, u_bf16, vt_bf16, m, r1, e0, e1, resid, gate)


def peer_layer(x, norm_g, mods, w_q, keys, u_bf16, vt_bf16):
    t, d = x.shape
    q, h = mm_norm(x, norm_g, mods[3], mods[4], w_q, emit_h=True)
    gate = mods[5]
    m, r1, e0, e1 = peer_router(q, keys.astype(jnp.bfloat16), PEER_TOKEN_TILE)
    return peer_dense(h.T, u_bf16, vt_bf16, m, r1, e0, e1, x, gate, t // gate.shape[0],
                      PEER_TOKEN_TILE, PEER_EXPERT_TILE)


def kernel(x_prompt, x_sample, c, cache_na_k, cache_na_v, cache_diff_k, cache_diff_v, state_s5_re, state_s5_im, c_ctx, mod_w, mod_b, norm_mix_g, norm_ffn_g, ev_w_in, ev_w_out, na_q_g, na_k_g, na_rpb, hy_conv_w, hy_conv_b, hy_w1, hy_b1, hy_f1, hy_w2, hy_b2, hy_f2, hy_w3, hy_decay, hy_bias, od_w_in, od_w_out, s5_lam_re, s5_lam_im, s5_log_dt, s5_b_re, s5_b_im, s5_c_re, s5_c_im, s5_d, s5_glu_w, s5_glu_b, diff_q_g, diff_k_g, diff_lq1, diff_lk1, diff_lq2, diff_lk2, diff_subln_g, peer_w_q, peer_keys, peer_u, peer_v):
    bp, lp, d = x_prompt.shape
    bs, ls, _ = x_sample.shape
    rope_p, rope_s = rope_tables(lp), rope_tables(ls)
    xp, xs = x_prompt.reshape(bp * lp, d), x_sample.reshape(bs * ls, d)
    new_na_k, new_na_v, new_dk, new_dv, new_sr, new_si = [], [], [], [], [], []
    for i in range(DEPTH):
        mods = adaln(jnp.concatenate([c_ctx[None, :], c], axis=0), mod_w[i], mod_b[i])
        mp = [v[:1] for v in mods]
        ms = [v[1:] for v in mods]
        j = i // 2
        if i % 2 == 0:
            ev = (ev_w_in[j].astype(jnp.bfloat16), ev_w_out[j].astype(jnp.bfloat16), na_q_g[j], na_k_g[j],
                  hy_conv_w[j], hy_conv_b[j],
                  hy_w1[j], hy_b1[j], hy_f1[j], hy_w2[j], hy_b2[j], hy_f2[j], hy_w3[j], hy_decay[j], hy_bias[j])
            bias_tabs = na_bias_tables(na_rpb[j], ls // GRID_W)
            xp, kp, vp = even_mixer(xp, mp, norm_mix_g[i], bp, lp, ev, None, None)
            xs, _, _ = even_mixer(xs, ms, norm_mix_g[i], bs, ls, ev, (cache_na_k[:, j], cache_na_v[:, j]), bias_tabs)
            new_na_k.append(kp)
            new_na_v.append(vp)
        else:
            lam_init = 0.8 - 0.6 * math.exp(-0.3 * i)
            s5_prep = s5_prepare(s5_lam_re[j], s5_lam_im[j], s5_log_dt[j], s5_b_re[j], s5_b_im[j],
                                 s5_c_re[j], s5_c_im[j])
            od = (od_w_in[j].astype(jnp.bfloat16), od_w_out[j].astype(jnp.bfloat16), s5_prep, s5_d[j],
                  s5_glu_w[j], s5_glu_b[j], diff_q_g[j], diff_k_g[j],
                  diff_lq1[j], diff_lk1[j], diff_lq2[j], diff_lk2[j], diff_subln_g[j])
            xp, kp, vp, sr, si = odd_mixer(xp, mp, norm_mix_g[i], bp, lp, od, lam_init, None, None, None, rope_p)
            xs, _, _, _, _ = odd_mixer(xs, ms, norm_mix_g[i], bs, ls, od, lam_init, state_s5_re[:, j],
                                       state_s5_im[:, j], (cache_diff_k[:, j], cache_diff_v[:, j]), rope_s)
            new_dk.append(kp)
            new_dv.append(vp)
            new_sr.append(sr)
            new_si.append(si)
        u_bf16 = peer_u[i].astype(jnp.bfloat16)
        vt_bf16 = jnp.swapaxes(peer_v[i].reshape(-1, PEER_EXPERT_TILE, d), 1, 2).astype(jnp.bfloat16)
        w_q_bf16 = peer_w_q[i].astype(jnp.bfloat16)
        xp = peer_layer(xp, norm_ffn_g[i], mp, w_q_bf16, peer_keys[i], u_bf16, vt_bf16)
        xs = peer_layer(xs, norm_ffn_g[i], ms, w_q_bf16, peer_keys[i], u_bf16, vt_bf16)
    return (xp.reshape(bp, lp, d), xs.reshape(bs, ls, d), jnp.stack(new_na_k, axis=1), jnp.stack(new_na_v, axis=1),
            jnp.stack(new_dk, axis=1), jnp.stack(new_dv, axis=1),
            jnp.stack(new_sr, axis=1), jnp.stack(new_si, axis=1))
```

```python
import functools
import math

import jax
import jax.numpy as jnp
from jax import lax
from jax.experimental import pallas as pl
from jax.experimental.pallas import tpu as pltpu

D_MODEL = 2048
DEPTH = 2
GRID_W = 64
MIX_WIDTH = D_MODEL // 2
HEAD_DIM = 128
NA_HEADS = MIX_WIDTH // HEAD_DIM
NA_KH = 8
NA_KW = 16
HY_WIDTH = MIX_WIDTH
HY_ORDER = 2
HY_POS_EMB = 33
S5_WIDTH = MIX_WIDTH
S5_GROUP = 16
S5_GROUPS = S5_WIDTH // S5_GROUP
S5_STATE = 64
DIFF_HEADS = MIX_WIDTH // HEAD_DIM
DIFF_QK_DIM = HEAD_DIM // 2
DIFF_V_DIM = HEAD_DIM
ROPE_BASE = 10000.0
PEER_HEADS = 8
PEER_N_KEYS = 128
PEER_QUERY_DIM = 256
PEER_TOPK = 16
NORM_EPS = 1e-6
NEG_INF = -1e30

LANES = 128
SUBLANES = 8
VMEM_LIMIT_BYTES = 56 * 1024 * 1024


MM_ROW_TILE = 1024
MM_COL_TILE = 512


def _mm_norm_kernel(x_ref, g_ref, sc_ref, sh_ref, w_ref, *out_refs, emit_h):
    if emit_h:
        o_ref, h_ref, hs_ref = out_refs
    else:
        o_ref, hs_ref = out_refs

    @pl.when(pl.program_id(1) == 0)
    def _():
        x = x_ref[...]
        y = x * lax.rsqrt(jnp.mean(x * x, axis=-1, keepdims=True) + NORM_EPS) * g_ref[...]
        hs_ref[...] = (y * (1.0 + sc_ref[0]) + sh_ref[0]).astype(jnp.bfloat16)
        if emit_h:
            h_ref[...] = hs_ref[...]

    o_ref[...] = jnp.dot(hs_ref[...], w_ref[...].astype(jnp.bfloat16), preferred_element_type=jnp.float32)


def mm_norm(x, norm_g, shift, scale, w, emit_h=False):
    m, k = x.shape
    n = w.shape[1]
    tm, tn = min(MM_ROW_TILE, m), min(MM_COL_TILE, n)
    nb = scale.shape[0]
    assert m % tm == 0 and n % tn == 0 and (m // nb) % tm == 0
    tiles_per_mod = (m // nb) // tm
    mod = pl.BlockSpec((1, 1, k), lambda i, j: (i // tiles_per_mod, 0, 0))
    out_specs = [pl.BlockSpec((tm, tn), lambda i, j: (i, j))]
    out_shape = [jax.ShapeDtypeStruct((m, n), jnp.float32)]
    if emit_h:
        out_specs.append(pl.BlockSpec((tm, k), lambda i, j: (i, 0)))
        out_shape.append(jax.ShapeDtypeStruct((m, k), jnp.bfloat16))
    res = pl.pallas_call(
        functools.partial(_mm_norm_kernel, emit_h=emit_h),
        grid=(m // tm, n // tn),
        in_specs=[pl.BlockSpec((tm, k), lambda i, j: (i, 0)),
                  pl.BlockSpec((1, k), lambda i, j: (0, 0)),
                  mod, mod,
                  pl.BlockSpec((k, tn), lambda i, j: (0, j))],
        out_specs=out_specs,
        out_shape=out_shape,
        scratch_shapes=[pltpu.VMEM((tm, k), jnp.bfloat16)],
        compiler_params=pltpu.CompilerParams(dimension_semantics=("parallel", "arbitrary"),
                                             vmem_limit_bytes=VMEM_LIMIT_BYTES),
        name="mm_norm",
    )(x, norm_g.astype(jnp.float32).reshape(1, k), scale, shift, w)
    return res if emit_h else res[0]


def _mm_pair_res_kernel(a_ref, b_ref, wa_ref, wb_ref, r_ref, g_ref, o_ref):
    y = (jnp.dot(a_ref[...].astype(jnp.bfloat16), wa_ref[...].astype(jnp.bfloat16), preferred_element_type=jnp.float32)
         + jnp.dot(b_ref[...].astype(jnp.bfloat16), wb_ref[...].astype(jnp.bfloat16), preferred_element_type=jnp.float32))
    o_ref[...] = r_ref[...] + g_ref[0] * y


def mm_pair_res(a, b, w, resid, gate):
    m, kh = a.shape
    n = w.shape[1]
    tm, tn = min(MM_ROW_TILE, m), min(MM_COL_TILE, n)
    nb = gate.shape[0]
    assert m % tm == 0 and n % tn == 0 and (m // nb) % tm == 0 and w.shape[0] == 2 * kh
    tiles_per_mod = (m // nb) // tm
    return pl.pallas_call(
        _mm_pair_res_kernel,
        grid=(m // tm, n // tn),
        in_specs=[pl.BlockSpec((tm, kh), lambda i, j: (i, 0)),
                  pl.BlockSpec((tm, kh), lambda i, j: (i, 0)),
                  pl.BlockSpec((kh, tn), lambda i, j: (0, j)),
                  pl.BlockSpec((kh, tn), lambda i, j: (1, j)),
                  pl.BlockSpec((tm, tn), lambda i, j: (i, j)),
                  pl.BlockSpec((1, 1, tn), lambda i, j: (i // tiles_per_mod, 0, j))],
        out_specs=pl.BlockSpec((tm, tn), lambda i, j: (i, j)),
        out_shape=jax.ShapeDtypeStruct((m, n), jnp.float32),
        compiler_params=pltpu.CompilerParams(dimension_semantics=("parallel", "arbitrary"),
                                             vmem_limit_bytes=VMEM_LIMIT_BYTES),
        name="mm_pair_res",
    )(a, b, w, w, resid, gate)


def adaln(cond, w, b):
    m = jax.nn.silu(cond) @ w + b
    return jnp.split(m[:, None, :], 6, axis=-1)


def hyena_filters(l, w1, b1, f1, w2, b2, f2, w3, decay):
    t = jnp.linspace(0.0, 1.0, l, dtype=jnp.float32)[:, None]
    bands = (HY_POS_EMB - 1) // 2
    w_ang = 2.0 * math.pi * jnp.arange(l, dtype=jnp.float32)[:, None] / l
    freqs = jnp.linspace(1e-4, bands - 1, bands, dtype=jnp.float32)[None, :]
    z = jnp.concatenate([t, jnp.cos(freqs * w_ang), -jnp.sin(freqs * w_ang)], axis=-1)
    h = jnp.sin(f1 * (z @ w1 + b1))
    h = jnp.sin(f2 * (h @ w2 + b2))
    h = (h @ w3).reshape(l, 2, HY_ORDER, HY_WIDTH).astype(jnp.float32)
    h = h * jnp.exp(-t.reshape(l, 1, 1, 1) * jnp.abs(decay.astype(jnp.float32)))
    h_f, h_b = h[:, 0], h[:, 1]
    zero = jnp.zeros((1, HY_ORDER, HY_WIDTH), jnp.float32)
    return jnp.concatenate([h_f, zero, h_b[1:][::-1]], axis=0)


def _split_bf16(x):
    hi = x.astype(jnp.bfloat16)
    lo = (x - hi.astype(jnp.float32)).astype(jnp.bfloat16)
    return hi, lo


def _dot3(a_hi, a_lo, x):
    m = a_hi.shape[0]
    xh, xl = _split_bf16(x)
    r = jnp.dot(jnp.concatenate([a_hi, a_lo], axis=0), xh, preferred_element_type=jnp.float32)
    return r[:m] + r[m:] + jnp.dot(a_hi, xl, preferred_element_type=jnp.float32)


HY_STEP_ROWS = 256


def hyena_factors(l):
    n = 2 * l
    n2 = 64 if n >= 8192 else 16
    return n // n2, n2


def hyena_tables(l):
    n = 2 * l
    n1, n2 = hyena_factors(l)
    t = (n2 * jnp.arange(n1 // 2)[None, None, :] + jnp.arange(n2)[:, None, None])
    k1 = jnp.arange(n1)[None, :, None]
    ang = (2.0 * math.pi / n) * ((t * k1) % n).astype(jnp.float32)
    fa = jnp.concatenate([jnp.cos(ang), -jnp.sin(ang)], axis=1)
    fc = jnp.swapaxes(fa, 1, 2) / n
    a2 = (2.0 * math.pi / n2) * ((jnp.arange(n2)[:, None] * jnp.arange(n2)[None, :]) % n2).astype(jnp.float32)
    c, s = jnp.cos(a2), jnp.sin(a2)
    fb = jnp.concatenate([jnp.concatenate([c, s], axis=1), jnp.concatenate([-s, c], axis=1)], axis=0)
    fbi = jnp.concatenate([jnp.concatenate([c, -s], axis=1), jnp.concatenate([s, c], axis=1)], axis=0)
    return tuple(_split_bf16(x) for x in (fa, fc, fb, fbi))


def hyena_spectrum(filt, l):
    n1, n2 = hyena_factors(l)
    kf = jnp.fft.fft(filt, axis=0)
    kf = jnp.stack([jnp.real(kf), jnp.imag(kf)], axis=0).astype(jnp.float32)
    kf = kf.reshape(2, n2, n1, HY_ORDER, HY_WIDTH)
    return jnp.transpose(kf, (3, 0, 2, 1, 4))


def _hy_stage_a_kernel(z_ref, fh_ref, fl_ref, o_ref):
    for i in range(z_ref.shape[0]):
        o_ref[i] = _dot3(fh_ref[0], fl_ref[0], z_ref[i])


def _hy_stage_b_kernel(s_ref, k_ref, fh_ref, fl_ref, gh_ref, gl_ref, o_ref):
    n2 = s_ref.shape[3]
    kr, ki = k_ref[0, 0], k_ref[1, 0]
    for i in range(s_ref.shape[0]):
        y = jnp.concatenate([s_ref[i, 0, 0], s_ref[i, 1, 0]], axis=0)
        z = _dot3(fh_ref[...], fl_ref[...], y)
        zr, zi = z[:n2], z[n2:]
        p = jnp.concatenate([zr * kr - zi * ki, zr * ki + zi * kr], axis=0)
        q = _dot3(gh_ref[...], gl_ref[...], p)
        o_ref[i, 0, 0] = q[:n2]
        o_ref[i, 1, 0] = q[n2:]


def _hy_stage_c_kernel(q_ref, fh_ref, fl_ref, z_ref, g_ref, b_ref, o_ref):
    for i in range(q_ref.shape[0]):
        y = _dot3(fh_ref[0], fl_ref[0], q_ref[i])
        o_ref[i] = g_ref[i] * (y + z_ref[i] * b_ref[...])


def hyena_long_conv(z, z_col, gate, gate_col, bias_o, spec_o, tables, l):
    (fah, fal), (fch, fcl), (fbh, fbl), (fgh, fgl) = tables
    b = z.shape[0]
    c = HY_WIDTH
    n1, n2 = hyena_factors(l)
    h1 = n1 // 2
    bb = max(1, min(b, HY_STEP_ROWS // n1))
    assert b % bb == 0
    cp = pltpu.CompilerParams(dimension_semantics=("parallel", "parallel"), vmem_limit_bytes=VMEM_LIMIT_BYTES)
    zblocks = z.shape[-1] // c
    gblocks = gate.shape[-1] // c
    zv = z.reshape(b, h1, n2 * z.shape[-1])
    gv = gate.reshape(b, h1, n2 * gate.shape[-1])
    s1 = pl.pallas_call(
        _hy_stage_a_kernel,
        grid=(b // bb, n2),
        in_specs=[pl.BlockSpec((bb, h1, c), lambda i, j: (i, 0, j * zblocks + z_col)),
                  pl.BlockSpec((1, 2 * n1, h1), lambda i, j: (j, 0, 0)),
                  pl.BlockSpec((1, 2 * n1, h1), lambda i, j: (j, 0, 0))],
        out_specs=pl.BlockSpec((bb, 2 * n1, c), lambda i, j: (i, 0, j)),
        out_shape=jax.ShapeDtypeStruct((b, 2 * n1, n2 * c), jnp.float32),
        compiler_params=cp, name="hyena_stage_a",
    )(zv, fah, fal)
    s1 = s1.reshape(b, 2, n1, n2, c)
    mat = pl.BlockSpec((2 * n2, 2 * n2), lambda k, i: (0, 0))
    q = pl.pallas_call(
        _hy_stage_b_kernel,
        grid=(n1, b // bb),
        in_specs=[pl.BlockSpec((bb, 2, 1, n2, c), lambda k, i: (i, 0, k, 0, 0)),
                  pl.BlockSpec((2, 1, n2, c), lambda k, i: (0, k, 0, 0)),
                  mat, mat, mat, mat],
        out_specs=pl.BlockSpec((bb, 2, 1, n2, c), lambda k, i: (i, 0, k, 0, 0)),
        out_shape=jax.ShapeDtypeStruct((b, 2, n1, n2, c), jnp.float32),
        compiler_params=cp, name="hyena_stage_b",
    )(s1, spec_o, fbh, fbl, fgh, fgl)
    q = q.reshape(b, 2 * n1, n2 * c)
    out = pl.pallas_call(
        _hy_stage_c_kernel,
        grid=(b // bb, n2),
        in_specs=[pl.BlockSpec((bb, 2 * n1, c), lambda i, j: (i, 0, j)),
                  pl.BlockSpec((1, h1, 2 * n1), lambda i, j: (j, 0, 0)),
                  pl.BlockSpec((1, h1, 2 * n1), lambda i, j: (j, 0, 0)),
                  pl.BlockSpec((bb, h1, c), lambda i, j: (i, 0, j * zblocks + z_col)),
                  pl.BlockSpec((bb, h1, c), lambda i, j: (i, 0, j * gblocks + gate_col)),
                  pl.BlockSpec((1, c), lambda i, j: (0, 0))],
        out_specs=pl.BlockSpec((bb, h1, c), lambda i, j: (i, 0, j)),
        out_shape=jax.ShapeDtypeStruct((b, h1, n2 * c), jnp.float32),
        compiler_params=cp, name="hyena_stage_c",
    )(q, fch, fcl, zv, gv, bias_o.astype(jnp.float32).reshape(1, c))
    return out.reshape(b, l, c)


def hyena(u, conv_w, conv_b, filt, bias):
    b, l, _ = u.shape
    up = jnp.pad(u, ((0, 0), (1, 1), (0, 0)))
    u = up[:, :-2] * conv_w[0] + up[:, 1:-1] * conv_w[1] + up[:, 2:] * conv_w[2] + conv_b
    tables = hyena_tables(l)
    spec = hyena_spectrum(hyena_filters(l, *filt), l)
    z, z_col = u, 0
    for o in range(HY_ORDER):
        z = hyena_long_conv(z, z_col, u, 1 + o, bias[o], spec[o], tables, l)
        z_col = 0
    return z


S5_TILE_GROUPS = 8
S5_TILES = S5_GROUPS // S5_TILE_GROUPS
S5_TILE_IN = S5_TILE_GROUPS * S5_GROUP
S5_TILE_STATE = S5_TILE_GROUPS * S5_STATE
S5_ROWS_PER_STEP = 2048


def s5_prepare(lam_re, lam_im, log_dt, b_re, b_im, c_re, c_im):
    f32 = jnp.float32
    lam_re, lam_im = lam_re.astype(f32), lam_im.astype(f32)
    dt = jnp.exp(log_dt.astype(f32))[..., None]
    mag = jnp.exp(lam_re * dt)
    ar, ai = mag * jnp.cos(lam_im * dt), mag * jnp.sin(lam_im * dt)
    den = lam_re * lam_re + lam_im * lam_im
    cr = ((ar - 1.0) * lam_re + ai * lam_im) / den
    ci = (ai * lam_re - (ar - 1.0) * lam_im) / den
    bbr = cr[..., None] * b_re - ci[..., None] * b_im
    bbi = cr[..., None] * b_im + ci[..., None] * b_re
    eye = jnp.eye(S5_TILE_GROUPS, dtype=f32)

    def tile_in(bb):
        x = bb.reshape(2, S5_TILES, S5_TILE_GROUPS, S5_STATE, S5_GROUP)
        x = jnp.einsum('dtgnp,gh->dtgphn', x, eye)
        return x.reshape(2, S5_TILES, S5_TILE_IN, S5_TILE_STATE)

    def tile_out(cc):
        x = cc.astype(f32).reshape(2, S5_TILES, S5_TILE_GROUPS, S5_GROUP, S5_STATE)
        x = jnp.einsum('dtgpn,gh->dtgnhp', x, eye)
        return x.reshape(2, S5_TILES, S5_TILE_STATE, S5_TILE_IN)

    win = jnp.concatenate([tile_in(bbr), tile_in(bbi)], axis=-1).astype(jnp.bfloat16)
    wout = jnp.concatenate([tile_out(c_re), -tile_out(c_im)], axis=-2).astype(jnp.bfloat16)
    a = jnp.stack([ar.reshape(2, S5_TILES, S5_TILE_STATE), ai.reshape(2, S5_TILES, S5_TILE_STATE)], axis=2)
    return win, wout, a


def _s5_scan_kernel(u_ref, win_ref, wout_ref, a_ref, h0_ref, y_ref, fin_ref, bu_ref, st_ref, *, batch):
    d = pl.program_id(0)
    c = pl.program_id(2)
    n_c = pl.num_programs(2)
    ns = S5_TILE_STATE
    steps = u_ref.shape[0] // batch

    @pl.when(c == 0)
    def _():
        st_ref[...] = h0_ref[0, 0]

    bu_ref[...] = jnp.dot(u_ref[...].astype(jnp.bfloat16), win_ref[0, 0], preferred_element_type=jnp.float32)
    ar = jnp.broadcast_to(a_ref[0, 0, 0:1, :], (batch, ns))
    ai = jnp.broadcast_to(a_ref[0, 0, 1:2, :], (batch, ns))

    def step(t, carry):
        hr, hi = carry
        te = jnp.where(d == 0, t, steps - 1 - t)
        r0 = pl.multiple_of(te * batch, batch)
        nr = ar * hr - ai * hi + bu_ref[pl.ds(r0, batch), 0:ns]
        ni = ar * hi + ai * hr + bu_ref[pl.ds(r0, batch), ns:2 * ns]
        bu_ref[pl.ds(r0, batch), 0:ns] = nr
        bu_ref[pl.ds(r0, batch), ns:2 * ns] = ni
        return nr, ni

    hr, hi = lax.fori_loop(0, steps, step, (st_ref[:, 0:ns], st_ref[:, ns:2 * ns]), unroll=4)
    st_ref[:, 0:ns] = hr
    st_ref[:, ns:2 * ns] = hi
    y_ref[0] = jnp.dot(bu_ref[...].astype(jnp.bfloat16), wout_ref[0, 0], preferred_element_type=jnp.float32)

    @pl.when(c == n_c - 1)
    def _():
        fin_ref[0, 0] = st_ref[...]


def s5_scan(u_tm, win, wout, a, h0, batch):
    rows = u_tm.shape[0]
    r = min(S5_ROWS_PER_STEP, rows)
    assert rows % r == 0 and r % batch == 0
    n_c = rows // r

    def chunk(d, c):
        return c + d * (n_c - 1 - 2 * c)

    return pl.pallas_call(
        functools.partial(_s5_scan_kernel, batch=batch),
        grid=(2, S5_TILES, n_c),
        in_specs=[pl.BlockSpec((r, S5_TILE_IN), lambda d, j, c: (chunk(d, c), j)),
                  pl.BlockSpec((1, 1, S5_TILE_IN, 2 * S5_TILE_STATE), lambda d, j, c: (d, j, 0, 0)),
                  pl.BlockSpec((1, 1, 2 * S5_TILE_STATE, S5_TILE_IN), lambda d, j, c: (d, j, 0, 0)),
                  pl.BlockSpec((1, 1, 2, S5_TILE_STATE), lambda d, j, c: (d, j, 0, 0)),
                  pl.BlockSpec((1, 1, batch, 2 * S5_TILE_STATE), lambda d, j, c: (d, j, 0, 0))],
        out_specs=[pl.BlockSpec((1, r, S5_TILE_IN), lambda d, j, c: (d, chunk(d, c), j)),
                   pl.BlockSpec((1, 1, batch, 2 * S5_TILE_STATE), lambda d, j, c: (d, j, 0, 0))],
        out_shape=[jax.ShapeDtypeStruct((2, rows, S5_WIDTH), jnp.float32),
                   jax.ShapeDtypeStruct((2, S5_TILES, batch, 2 * S5_TILE_STATE), jnp.float32)],
        scratch_shapes=[pltpu.VMEM((r, 2 * S5_TILE_STATE), jnp.float32),
                        pltpu.VMEM((batch, 2 * S5_TILE_STATE), jnp.float32)],
        compiler_params=pltpu.CompilerParams(dimension_semantics=("parallel", "parallel", "arbitrary"),
                                             vmem_limit_bytes=VMEM_LIMIT_BYTES),
        name="s5_scan",
    )(u_tm, win, wout, a, h0)


def _gelu_tanh(x):
    return 0.5 * x * (1.0 + jnp.tanh(math.sqrt(2.0 / math.pi) * (x + 0.044715 * (x * x * x))))


def _s5_glu_kernel(u_ref, y_ref, d_ref, w_ref, b_ref, o_ref):
    y = d_ref[...] * u_ref[...] + y_ref[0] + y_ref[1]
    y = _gelu_tanh(y)
    z = jnp.dot(y.astype(jnp.bfloat16), w_ref[...], preferred_element_type=jnp.float32) + b_ref[...]
    o_ref[...] = y * (1.0 / (1.0 + jnp.exp(-z)))


def s5_glu(u_tm, y, d_skip, glu_w_bf16, glu_b, tr=1024):
    rows, w = u_tm.shape
    tr = min(tr, rows)
    assert rows % tr == 0
    return pl.pallas_call(
        _s5_glu_kernel,
        grid=(rows // tr,),
        in_specs=[pl.BlockSpec((tr, w), lambda i: (i, 0)),
                  pl.BlockSpec((2, tr, w), lambda i: (0, i, 0)),
                  pl.BlockSpec((1, w), lambda i: (0, 0)),
                  pl.BlockSpec((w, w), lambda i: (0, 0)),
                  pl.BlockSpec((1, w), lambda i: (0, 0))],
        out_specs=pl.BlockSpec((tr, w), lambda i: (i, 0)),
        out_shape=jax.ShapeDtypeStruct((rows, w), jnp.float32),
        compiler_params=pltpu.CompilerParams(dimension_semantics=("parallel",),
                                             vmem_limit_bytes=VMEM_LIMIT_BYTES),
        name="s5_glu",
    )(u_tm, y, d_skip.reshape(1, w), glu_w_bf16, glu_b.reshape(1, w))


def s5_mixer(u, h0_re, h0_im, prep, d_skip, glu_w, glu_b):
    win, wout, a = prep
    b, l, w = u.shape
    u_tm = jnp.swapaxes(u, 0, 1).reshape(l * b, w)
    if h0_re is None:
        h0 = jnp.zeros((2, S5_TILES, b, 2 * S5_TILE_STATE), jnp.float32)
    else:
        def tiles(h):
            return jnp.transpose(h.astype(jnp.float32).reshape(b, 2, S5_TILES, S5_TILE_STATE), (1, 2, 0, 3))
        h0 = jnp.concatenate([tiles(h0_re), tiles(h0_im)], axis=-1)
    y, fin = s5_scan(u_tm, win, wout, a, h0, b)
    out_tm = s5_glu(u_tm, y, d_skip, glu_w.astype(jnp.bfloat16), glu_b)
    out = jnp.swapaxes(out_tm.reshape(l, b, w), 0, 1)

    def untile(f):
        return jnp.transpose(f, (2, 0, 1, 3)).reshape(b, 2, S5_GROUPS, S5_STATE)
    return out, untile(fin[..., :S5_TILE_STATE]), untile(fin[..., S5_TILE_STATE:])


def _half_mean_matrix():
    i = lax.broadcasted_iota(jnp.int32, (LANES, LANES), 0) // DIFF_QK_DIM
    j = lax.broadcasted_iota(jnp.int32, (LANES, LANES), 1) // DIFF_QK_DIM
    return jnp.where(i == j, 1.0 / DIFF_QK_DIM, 0.0).astype(jnp.bfloat16)


def _rms_groups(x, gain, avg):
    xx = x * x
    hi = xx.astype(jnp.bfloat16)
    lo = (xx - hi.astype(jnp.float32)).astype(jnp.bfloat16)
    ms = (jnp.dot(hi, avg, preferred_element_type=jnp.float32)
          + jnp.dot(lo, avg, preferred_element_type=jnp.float32))
    return x * lax.rsqrt(ms + NORM_EPS) * gain


def _rope_lanes(y, cos, sin_signed, first_half):
    rot = jnp.where(first_half, pltpu.roll(y, LANES - 16, 1), pltpu.roll(y, 16, 1))
    return y * cos + rot * sin_signed


def _diff_prep_kernel(q_ref, k_ref, qg_ref, kg_ref, cos_ref, sin_ref, qo_ref, ko_ref, kn_ref, *, use_rope):
    avg = _half_mean_matrix()
    lane = lax.broadcasted_iota(jnp.int32, (1, LANES), 1)
    first_half = (lane % 32) < 16
    scale = DIFF_QK_DIM ** -0.5
    for h in range(DIFF_HEADS):
        cols = slice(h * LANES, (h + 1) * LANES)
        qn = _rms_groups(q_ref[:, cols], qg_ref[...], avg)
        kn = _rms_groups(k_ref[:, cols], kg_ref[...], avg)
        kn_ref[:, cols] = kn
        if use_rope:
            qn = _rope_lanes(qn, cos_ref[...], sin_ref[...], first_half)
            kn = _rope_lanes(kn, cos_ref[...], sin_ref[...], first_half)
        qo_ref[:, cols] = (qn * scale).astype(jnp.bfloat16)
        ko_ref[:, cols] = kn.astype(jnp.bfloat16)


def diff_prep(p, q_g, k_g, cos, sin_signed, seq_len, use_rope, tr=512):
    rows = p.shape[0]
    tr = min(tr, seq_len)
    assert seq_len % tr == 0
    w = DIFF_HEADS * LANES
    per_seq = seq_len // tr
    g2 = lambda g: jnp.tile(g.astype(jnp.float32), 2).reshape(1, LANES)
    return pl.pallas_call(
        functools.partial(_diff_prep_kernel, use_rope=use_rope),
        grid=(rows // tr,),
        in_specs=[pl.BlockSpec((tr, w), lambda i: (i, 1)),
                  pl.BlockSpec((tr, w), lambda i: (i, 2)),
                  pl.BlockSpec((1, LANES), lambda i: (0, 0)),
                  pl.BlockSpec((1, LANES), lambda i: (0, 0)),
                  pl.BlockSpec((tr, LANES), lambda i: (i % per_seq, 0)),
                  pl.BlockSpec((tr, LANES), lambda i: (i % per_seq, 0))],
        out_specs=[pl.BlockSpec((tr, w), lambda i: (i, 0))] * 3,
        out_shape=[jax.ShapeDtypeStruct((rows, w), jnp.bfloat16), jax.ShapeDtypeStruct((rows, w), jnp.bfloat16),
                   jax.ShapeDtypeStruct((rows, w), jnp.float32)],
        compiler_params=pltpu.CompilerParams(dimension_semantics=("parallel",),
                                             vmem_limit_bytes=VMEM_LIMIT_BYTES),
        name="diff_prep",
    )(p, p, g2(q_g), g2(k_g), cos, sin_signed)


def rope_tables(l):
    t = jnp.arange(l)
    row = (t // GRID_W).astype(jnp.float32)
    col = (t % GRID_W).astype(jnp.float32)
    nf = DIFF_QK_DIM // 4
    inv = ROPE_BASE ** (-jnp.arange(nf, dtype=jnp.float32) / nf)
    ang = jnp.stack([row[:, None] * inv, col[:, None] * inv], axis=1)
    ang = jnp.stack([ang, ang], axis=2).reshape(l, DIFF_QK_DIM)
    sign = jnp.where((jnp.arange(DIFF_QK_DIM) % 32) < 16, -1.0, 1.0)
    return jnp.tile(jnp.cos(ang), (1, 2)), jnp.tile(jnp.sin(ang) * sign, (1, 2))


def _diff_attn_kernel(*refs, has_ctx):
    if has_ctx:
        q_ref, k_ref, v_ref, ck_ref, cv_ref, lam_ref, g_ref, o_ref = refs
    else:
        q_ref, k_ref, v_ref, lam_ref, g_ref, o_ref = refs
    q = q_ref[...]
    lane = lax.broadcasted_iota(jnp.int32, (1, LANES), 1)
    zero = jnp.zeros_like(q)
    qs = (jnp.where(lane < DIFF_QK_DIM, q, zero), jnp.where(lane >= DIFF_QK_DIM, q, zero))
    nt = (((1,), (1,)), ((), ()))
    k = k_ref[...]
    ck = ck_ref[...].astype(jnp.bfloat16) if has_ctx else None
    lam = lam_ref[0:1, 0:1]
    w_self, w_ctx = None, None
    for i in range(2):
        s = lax.dot_general(qs[i], k, nt, preferred_element_type=jnp.float32)
        m = jnp.max(s, axis=-1, keepdims=True)
        if has_ctx:
            sc = lax.dot_general(qs[i], ck, nt, preferred_element_type=jnp.float32)
            m = jnp.maximum(m, jnp.max(sc, axis=-1, keepdims=True))
        p = jnp.exp(s - m)
        l = jnp.sum(p, axis=-1, keepdims=True)
        if has_ctx:
            pc = jnp.exp(sc - m)
            l = l + jnp.sum(pc, axis=-1, keepdims=True)
        coef = 1.0 / l if i == 0 else -lam / l
        w_self = p * coef if i == 0 else w_self + p * coef
        if has_ctx:
            w_ctx = pc * coef if i == 0 else w_ctx + pc * coef
    o = jnp.dot(w_self.astype(jnp.bfloat16), v_ref[...].astype(jnp.bfloat16), preferred_element_type=jnp.float32)
    if has_ctx:
        o = o + jnp.dot(w_ctx.astype(jnp.bfloat16), cv_ref[...].astype(jnp.bfloat16),
                        preferred_element_type=jnp.float32)
    ms = jnp.mean(o * o, axis=-1, keepdims=True)
    o_ref[...] = o * lax.rsqrt(ms + NORM_EPS) * g_ref[...]


def diff_attention(q_bf16, k_bf16, p, ctx_k, ctx_v, lam, gain, batch, seq_len, tq=256):
    tq = min(tq, seq_len)
    nq = seq_len // tq
    has_ctx = ctx_k is not None
    v_col0 = 3 * DIFF_HEADS
    in_specs = [pl.BlockSpec((tq, LANES), lambda b, h, i: (b * nq + i, h)),
                pl.BlockSpec((seq_len, LANES), lambda b, h, i: (b, h)),
                pl.BlockSpec((seq_len, LANES), lambda b, h, i: (b, v_col0 + h))]
    args = [q_bf16, k_bf16, p]
    if has_ctx:
        lc = ctx_k.shape[0] // batch
        in_specs += [pl.BlockSpec((lc, LANES), lambda b, h, i: (b, h))] * 2
        args += [ctx_k, ctx_v]
    in_specs += [pl.BlockSpec((1, LANES), lambda b, h, i: (0, 0))] * 2
    args += [jnp.broadcast_to(lam.astype(jnp.float32), (1, LANES)), gain.astype(jnp.float32).reshape(1, LANES)]
    return pl.pallas_call(
        functools.partial(_diff_attn_kernel, has_ctx=has_ctx),
        grid=(batch, DIFF_HEADS, nq),
        in_specs=in_specs,
        out_specs=pl.BlockSpec((tq, LANES), lambda b, h, i: (b * nq + i, h)),
        out_shape=jax.ShapeDtypeStruct((batch * seq_len, DIFF_HEADS * LANES), jnp.float32),
        compiler_params=pltpu.CompilerParams(dimension_semantics=("parallel", "parallel", "arbitrary"),
                                             vmem_limit_bytes=VMEM_LIMIT_BYTES),
        name="diff_attention",
    )(*args)


NA_Q_ROWS = 8
NA_WIN_ROWS = 16


def _na_prep_kernel(q_ref, k_ref, qg_ref, kg_ref, qo_ref, ko_ref, kn_ref):
    avg = jnp.full((LANES, LANES), 1.0 / HEAD_DIM, jnp.bfloat16)
    for h in range(NA_HEADS):
        cols = slice(h * LANES, (h + 1) * LANES)
        qo_ref[:, cols] = _rms_groups(q_ref[:, cols], qg_ref[...], avg).astype(jnp.bfloat16)
        kn = _rms_groups(k_ref[:, cols], kg_ref[...], avg)
        kn_ref[:, cols] = kn
        ko_ref[:, cols] = kn.astype(jnp.bfloat16)


def na_prep(p, q_g, k_g, tr=512):
    rows = p.shape[0]
    tr = min(tr, rows)
    w = NA_HEADS * LANES
    g1 = lambda g: g.astype(jnp.float32).reshape(1, LANES)
    return pl.pallas_call(
        _na_prep_kernel,
        grid=(rows // tr,),
        in_specs=[pl.BlockSpec((tr, w), lambda i: (i, 0)),
                  pl.BlockSpec((tr, w), lambda i: (i, 1)),
                  pl.BlockSpec((1, LANES), lambda i: (0, 0)),
                  pl.BlockSpec((1, LANES), lambda i: (0, 0))],
        out_specs=[pl.BlockSpec((tr, w), lambda i: (i, 0))] * 3,
        out_shape=[jax.ShapeDtypeStruct((rows, w), jnp.bfloat16), jax.ShapeDtypeStruct((rows, w), jnp.bfloat16),
                   jax.ShapeDtypeStruct((rows, w), jnp.float32)],
        compiler_params=pltpu.CompilerParams(dimension_semantics=("parallel",),
                                             vmem_limit_bytes=VMEM_LIMIT_BYTES),
        name="na_prep",
    )(p, p, g1(q_g), g1(k_g))


def na_bias_tables(rpb, rows):
    nblk = rows // NA_Q_ROWS
    cq = jnp.arange(GRID_W)[:, None]
    ck = jnp.arange(GRID_W)[None, :]
    col_start = jnp.clip(cq - NA_KW // 2, 0, GRID_W - NA_KW)
    col_ok = (ck >= col_start) & (ck < col_start + NA_KW)
    cidx = jnp.clip(ck - cq + NA_KW - 1, 0, 2 * NA_KW - 2)
    tabs = []
    for i in (0, 1, nblk - 1):
        base = min(max(NA_Q_ROWS * i - NA_KH // 2, 0), rows - NA_WIN_ROWS)
        r = NA_Q_ROWS * i + jnp.arange(NA_Q_ROWS)[:, None]
        rk = base + jnp.arange(NA_WIN_ROWS)[None, :]
        r0 = jnp.clip(r - NA_KH // 2, 0, rows - NA_KH)
        row_ok = (rk >= r0) & (rk < r0 + NA_KH)
        ridx = jnp.clip(rk - r + NA_KH - 1, 0, 2 * NA_KH - 2)
        b = rpb[:, ridx][:, :, :, cidx]
        ok = row_ok[:, :, None, None] & col_ok[None, None, :, :]
        b = jnp.where(ok[None], b.astype(jnp.float32), NEG_INF)
        tabs.append(jnp.transpose(b, (0, 1, 3, 2, 4)).reshape(NA_HEADS, NA_Q_ROWS * GRID_W, NA_WIN_ROWS * GRID_W))
    return jnp.stack(tabs, axis=0)


def _na_attn_kernel(*refs, windowed, rows):
    nt = (((1,), (1,)), ((), ()))
    scale = HEAD_DIM ** -0.5
    if windowed:
        q_ref, k_ref, v_ref, b_ref, ck_ref, cv_ref, o_ref = refs
        i = pl.program_id(2)
        base = jnp.clip(NA_Q_ROWS * i - NA_KH // 2, 0, rows - NA_WIN_ROWS)
        k0 = pl.multiple_of(base * GRID_W, GRID_W)
        nk = NA_WIN_ROWS * GRID_W
        q = q_ref[...]
        s = lax.dot_general(q, k_ref[pl.ds(k0, nk), :], nt, preferred_element_type=jnp.float32) * scale + b_ref[0, 0]
        sc = lax.dot_general(q, ck_ref[...].astype(jnp.bfloat16), nt, preferred_element_type=jnp.float32) * scale
        m = jnp.maximum(jnp.max(s, axis=-1, keepdims=True), jnp.max(sc, axis=-1, keepdims=True))
        p = jnp.exp(s - m)
        pc = jnp.exp(sc - m)
        l = jnp.sum(p, axis=-1, keepdims=True) + jnp.sum(pc, axis=-1, keepdims=True)
        o = (jnp.dot(p.astype(jnp.bfloat16), v_ref[pl.ds(k0, nk), :].astype(jnp.bfloat16),
                     preferred_element_type=jnp.float32)
             + jnp.dot(pc.astype(jnp.bfloat16), cv_ref[...].astype(jnp.bfloat16), preferred_element_type=jnp.float32))
    else:
        q_ref, k_ref, v_ref, o_ref = refs
        s = lax.dot_general(q_ref[...], k_ref[...], nt, preferred_element_type=jnp.float32) * scale
        m = jnp.max(s, axis=-1, keepdims=True)
        p = jnp.exp(s - m)
        l = jnp.sum(p, axis=-1, keepdims=True)
        o = jnp.dot(p.astype(jnp.bfloat16), v_ref[...].astype(jnp.bfloat16), preferred_element_type=jnp.float32)
    o_ref[...] = o / l


def na_attention(q_bf16, k_bf16, p, bias_tabs, ctx_k, ctx_v, batch, seq_len):
    windowed = bias_tabs is not None
    v_col0 = 2 * NA_HEADS
    rows = seq_len // GRID_W
    if windowed:
        tq = NA_Q_ROWS * GRID_W
        nq = seq_len // tq
        lc = ctx_k.shape[0] // batch
        nk = NA_WIN_ROWS * GRID_W
        in_specs = [pl.BlockSpec((tq, LANES), lambda b, h, i: (b * nq + i, h)),
                    pl.BlockSpec((seq_len, LANES), lambda b, h, i: (b, h)),
                    pl.BlockSpec((seq_len, LANES), lambda b, h, i: (b, v_col0 + h)),
                    pl.BlockSpec((1, 1, tq, nk),
                                 lambda b, h, i: (jnp.where(i == 0, 0, jnp.where(i == nq - 1, 2, 1)), h, 0, 0)),
                    pl.BlockSpec((lc, LANES), lambda b, h, i: (b, h)),
                    pl.BlockSpec((lc, LANES), lambda b, h, i: (b, h))]
        args = [q_bf16, k_bf16, p, bias_tabs, ctx_k, ctx_v]
    else:
        tq = seq_len
        nq = 1
        in_specs = [pl.BlockSpec((tq, LANES), lambda b, h, i: (b, h)),
                    pl.BlockSpec((seq_len, LANES), lambda b, h, i: (b, h)),
                    pl.BlockSpec((seq_len, LANES), lambda b, h, i: (b, v_col0 + h))]
        args = [q_bf16, k_bf16, p]
    return pl.pallas_call(
        functools.partial(_na_attn_kernel, windowed=windowed, rows=rows),
        grid=(batch, NA_HEADS, nq),
        in_specs=in_specs,
        out_specs=pl.BlockSpec((tq, LANES), lambda b, h, i: (b * nq + i, h)),
        out_shape=jax.ShapeDtypeStruct((batch * seq_len, NA_HEADS * LANES), jnp.float32),
        compiler_params=pltpu.CompilerParams(dimension_semantics=("parallel", "parallel", "arbitrary"),
                                             vmem_limit_bytes=VMEM_LIMIT_BYTES),
        name="na_attention",
    )(*args)


def even_mixer(x, mods, norm_g, b, l, ev, ctx_kv, bias_tabs):
    (w_in, w_out, q_g, k_g, conv_w, conv_b, w1, b1, f1, w2, b2, f2, w3, decay, bias) = ev
    p = mm_norm(x, norm_g, mods[0], mods[1], w_in)
    q, k, kn = na_prep(p, q_g, k_g)
    if ctx_kv is None:
        oa = na_attention(q, k, p, None, None, None, b, l)
    else:
        lc = ctx_kv[0].shape[1]
        oa = na_attention(q, k, p, bias_tabs, ctx_kv[0].reshape(b * lc, MIX_WIDTH),
                          ctx_kv[1].reshape(b * lc, MIX_WIDTH), b, l)
    hb = p[:, 3 * MIX_WIDTH:].reshape(b, l, 3 * HY_WIDTH)
    ob = hyena(hb, conv_w, conv_b, (w1, b1, f1, w2, b2, f2, w3, decay), bias)
    out = mm_pair_res(oa, ob.reshape(b * l, HY_WIDTH), w_out, x, mods[2])
    shp = (b, l, NA_HEADS, HEAD_DIM)
    return out, kn.reshape(shp), p[:, 2 * MIX_WIDTH:3 * MIX_WIDTH].reshape(shp)


def odd_mixer(x, mods, norm_g, b, l, od, lam_init, h0_re, h0_im, ctx_kv, rope):
    (w_in, w_out, s5_prep, d_skip, glu_w, glu_b, q_g, k_g, lq1, lk1, lq2, lk2, subln_g) = od
    p = mm_norm(x, norm_g, mods[0], mods[1], w_in)
    oc, fr, fi = s5_mixer(p[:, :S5_WIDTH].reshape(b, l, S5_WIDTH), h0_re, h0_im, s5_prep, d_skip, glu_w, glu_b)
    lam = (jnp.exp(jnp.sum(lq1 * lk1).astype(jnp.float32))
           - jnp.exp(jnp.sum(lq2 * lk2).astype(jnp.float32)) + lam_init)
    cos, sin_signed = rope
    q, k, kn = diff_prep(p, q_g, k_g, cos, sin_signed, l, use_rope=ctx_kv is not None)
    gain = subln_g.astype(jnp.float32) * (1.0 - lam_init)
    if ctx_kv is None:
        o = diff_attention(q, k, p, None, None, lam, gain, b, l)
    else:
        lc = ctx_kv[0].shape[1]
        o = diff_attention(q, k, p, ctx_kv[0].reshape(b * lc, MIX_WIDTH), ctx_kv[1].reshape(b * lc, MIX_WIDTH),
                           lam, gain, b, l)
    out = mm_pair_res(oc.reshape(b * l, S5_WIDTH), o, w_out, x, mods[2])
    kn = kn.reshape(b, l, DIFF_HEADS, 2, DIFF_QK_DIM)
    v = p[:, S5_WIDTH + 2 * MIX_WIDTH:].reshape(b, l, DIFF_HEADS, DIFF_V_DIM)
    return out, kn, v, fr, fi


PEER_TOKEN_TILE = 512
PEER_EXPERT_TILE = 1024
PEER_SUB_EXPERTS = 256


def _top16_rows(cur, iota):
    n = cur.shape[0]
    tops, idxs = [], []
    for _ in range(PEER_TOPK):
        m = jnp.max(cur, axis=0, keepdims=True)
        first = jnp.min(jnp.where(cur == m, iota, n), axis=0, keepdims=True)
        tops.append(m)
        idxs.append(first)
        cur = jnp.where(iota == first, -jnp.inf, cur)
    return tops, idxs


def _router_kernel(q_ref, keys_ref, m_ref, r1_ref, e0_ref, e1_ref, st_ref):
    tt = q_ref.shape[0]
    half = PEER_QUERY_DIM // 2
    for h in range(PEER_HEADS):
        for s in range(2):
            qs = q_ref[:, (2 * h + s) * half:(2 * h + s + 1) * half].astype(jnp.bfloat16)
            st = lax.dot_general(keys_ref[h, s], qs, (((1,), (1,)), ((), ())),
                                 preferred_element_type=jnp.float32)
            st_ref[2 * h + s] = st

    iota128 = lax.broadcasted_iota(jnp.int32, (PEER_N_KEYS, LANES), 0)
    iota8 = lax.broadcasted_iota(jnp.int32, (SUBLANES, LANES), 0)
    iota_cand = lax.broadcasted_iota(jnp.int32, (PEER_TOPK + SUBLANES * SUBLANES, LANES), 0)
    n_chunks = tt // LANES

    def route(h, lane0):
        s0 = st_ref[2 * h, :, pl.ds(lane0, LANES)]
        s1 = st_ref[2 * h + 1, :, pl.ds(lane0, LANES)]
        a, a_idx = _top16_rows(s0, iota128)
        b, b_idx = _top16_rows(s1, iota128)
        bmat = jnp.concatenate(b, axis=0)
        sums = [a[k] + bmat for k in range(PEER_TOPK)]
        cand = [sums[0], sums[1][:SUBLANES]]
        for k in range(2, SUBLANES):
            cand.append(jnp.where(iota8 < PEER_TOPK // (k + 1), sums[k][:SUBLANES], -jnp.inf))
        cand.append(jnp.concatenate(a[SUBLANES:], axis=0) + b[0])
        f, _ = _top16_rows(jnp.concatenate(cand, axis=0), iota_cand)
        tau = f[PEER_TOPK - 1]
        z = jnp.ones_like(f[0])
        for k in range(1, PEER_TOPK):
            z = z + jnp.exp(f[k] - f[0])
        m = jnp.zeros((PEER_N_KEYS, LANES), jnp.float32)
        r1 = jnp.full((PEER_N_KEYS, LANES), float(PEER_TOPK), jnp.float32)
        for k in range(PEER_TOPK):
            m_k = jnp.sum(jnp.where(sums[k] >= tau, 1.0, 0.0), axis=0, keepdims=True)
            m = jnp.where(iota128 == a_idx[k], m_k, m)
            r1 = jnp.where(iota128 == b_idx[k], float(k), r1)
        m_ref[h, :, pl.ds(lane0, LANES)] = m
        r1_ref[h, :, pl.ds(lane0, LANES)] = r1.astype(jnp.bfloat16)
        e0_ref[h, :, pl.ds(lane0, LANES)] = jnp.exp(s0 - a[0]) / z
        e1_ref[h, :, pl.ds(lane0, LANES)] = jnp.exp(s1 - b[0]).astype(jnp.bfloat16)

    def body(i, carry):
        h = i // (n_chunks // 2)
        c = i % (n_chunks // 2)
        route(h, pl.multiple_of(2 * c * LANES, LANES))
        route(h, pl.multiple_of((2 * c + 1) * LANES, LANES))
        return carry

    lax.fori_loop(0, PEER_HEADS * n_chunks // 2, body, 0)


def peer_router(q, keys_bf16, tt):
    t = q.shape[0]
    assert t % tt == 0 and tt % (2 * LANES) == 0
    f32 = jnp.float32
    big = jax.ShapeDtypeStruct((PEER_HEADS, PEER_N_KEYS, t), f32)
    half = jax.ShapeDtypeStruct((PEER_HEADS, PEER_N_KEYS, t), jnp.bfloat16)
    blk = pl.BlockSpec((PEER_HEADS, PEER_N_KEYS, tt), lambda i: (0, 0, i))
    return pl.pallas_call(
        _router_kernel,
        grid=(t // tt,),
        in_specs=[pl.BlockSpec((tt, q.shape[1]), lambda i: (i, 0)),
                  pl.BlockSpec(keys_bf16.shape, lambda i: (0, 0, 0, 0))],
        out_specs=[blk, blk, blk, blk],
        out_shape=[big, half, big, half],
        scratch_shapes=[pltpu.VMEM((2 * PEER_HEADS, PEER_N_KEYS, tt), f32)],
        compiler_params=pltpu.CompilerParams(dimension_semantics=("parallel",),
                                             vmem_limit_bytes=VMEM_LIMIT_BYTES),
        name="peer_router",
    )(q, keys_bf16)


def _dense_kernel(xt_ref, u_ref, vt_ref, m_ref, r1_ref, e0_ref, e1_ref, res_ref, gate_ref,
                  o_ref, at_ref, wt_ref, acc_ref):
    e = pl.program_id(1)
    n_e = pl.num_programs(1)
    eb = u_ref.shape[0]
    tt = xt_ref.shape[1]
    n_chunks = tt // LANES
    n_sub = eb // PEER_SUB_EXPERTS
    i1_per_sub = PEER_SUB_EXPERTS // PEER_N_KEYS

    @pl.when(e == 0)
    def _():
        acc_ref[...] = jnp.zeros_like(acc_ref)

    at_ref[...] = jnp.dot(u_ref[...], xt_ref[...], preferred_element_type=jnp.float32)
    for sb in range(n_sub):
        for c in range(n_chunks):
            lanes = slice(c * LANES, (c + 1) * LANES)
            for k in range(i1_per_sub):
                i1 = sb * i1_per_sub + k
                zero = jnp.zeros((PEER_N_KEYS, LANES), jnp.bfloat16)
                g = zero
                for h in range(PEER_HEADS):
                    mrow = m_ref[h, i1:i1 + 1, lanes].astype(jnp.bfloat16)
                    e0row = e0_ref[h, i1:i1 + 1, lanes].astype(jnp.bfloat16)
                    g = g + jnp.where(r1_ref[h, :, lanes] < mrow, e0row * e1_ref[h, :, lanes], zero)
                r = slice(i1 * PEER_N_KEYS, (i1 + 1) * PEER_N_KEYS)
                wt_ref[r, lanes] = _gelu_tanh(at_ref[r, lanes]).astype(jnp.bfloat16) * g
    acc_ref[...] += jnp.dot(vt_ref[0], wt_ref[...], preferred_element_type=jnp.float32)

    @pl.when(e == n_e - 1)
    def _():
        o_ref[...] = res_ref[...] + gate_ref[0] * acc_ref[...].T


def peer_dense(xt_bf16, u_bf16, vt_bf16, m, r1, e0, e1, resid, gate, tokens_per_gate, tt, eb):
    d, t = xt_bf16.shape
    n_exp = u_bf16.shape[0]
    assert t % tt == 0 and n_exp % eb == 0 and tokens_per_gate % tt == 0
    assert eb // PEER_N_KEYS == SUBLANES
    tiles_per_gate = tokens_per_gate // tt
    rblk = pl.BlockSpec((PEER_HEADS, PEER_N_KEYS, tt), lambda i, j: (0, 0, i))
    gblk = pl.BlockSpec((PEER_HEADS, SUBLANES, tt), lambda i, j: (0, j, i))
    return pl.pallas_call(
        _dense_kernel,
        grid=(t // tt, n_exp // eb),
        in_specs=[pl.BlockSpec((d, tt), lambda i, j: (0, i)),
                  pl.BlockSpec((eb, d), lambda i, j: (j, 0)),
                  pl.BlockSpec((1, d, eb), lambda i, j: (j, 0, 0)),
                  gblk, rblk, gblk, rblk,
                  pl.BlockSpec((tt, d), lambda i, j: (i, 0)),
                  pl.BlockSpec((1, 1, d), lambda i, j: (i // tiles_per_gate, 0, 0))],
        out_specs=pl.BlockSpec((tt, d), lambda i, j: (i, 0)),
        out_shape=jax.ShapeDtypeStruct((t, d), jnp.float32),
        scratch_shapes=[pltpu.VMEM((eb, tt), jnp.float32),
                        pltpu.VMEM((eb, tt), jnp.bfloat16),
                        pltpu.VMEM((d, tt), jnp.float32)],
        compiler_params=pltpu.CompilerParams(dimension_semantics=("parallel", "arbitrary"),
                                             vmem_limit_bytes=VMEM_LIMIT_BYTES),
        name="peer_dense",
    )(xt_bf16, u_bf16, vt_bf16, m, r1, e0, e1, resid, gate)


def peer_layer(x, norm_g, mods, w_q, keys, u_bf16, vt_bf16):
    t, d = x.shape
    q, h = mm_norm(x, norm_g, mods[3], mods[4], w_q, emit_h=True)
    gate = mods[5]
    m, r1, e0, e1 = peer_router(q, keys.astype(jnp.bfloat16), PEER_TOKEN_TILE)
    return peer_dense(h.T, u_bf16, vt_bf16, m, r1, e0, e1, x, gate, t // gate.shape[0],
                      PEER_TOKEN_TILE, PEER_EXPERT_TILE)


def kernel(x_prompt, x_sample, c, cache_na_k, cache_na_v, cache_diff_k, cache_diff_v, state_s5_re, state_s5_im, c_ctx, mod_w, mod_b, norm_mix_g, norm_ffn_g, ev_w_in, ev_w_out, na_q_g, na_k_g, na_rpb, hy_conv_w, hy_conv_b, hy_w1, hy_b1, hy_f1, hy_w2, hy_b2, hy_f2, hy_w3, hy_decay, hy_bias, od_w_in, od_w_out, s5_lam_re, s5_lam_im, s5_log_dt, s5_b_re, s5_b_im, s5_c_re, s5_c_im, s5_d, s5_glu_w, s5_glu_b, diff_q_g, diff_k_g, diff_lq1, diff_lk1, diff_lq2, diff_lk2, diff_subln_g, peer_w_q, peer_keys, peer_u, peer_v):
    bp, lp, d = x_prompt.shape
    bs, ls, _ = x_sample.shape
    rope_p, rope_s = rope_tables(lp), rope_tables(ls)
    xp, xs = x_prompt.reshape(bp * lp, d), x_sample.reshape(bs * ls, d)
    new_na_k, new_na_v, new_dk, new_dv, new_sr, new_si = [], [], [], [], [], []
    for i in range(DEPTH):
        mods = adaln(jnp.concatenate([c_ctx[None, :], c], axis=0), mod_w[i], mod_b[i])
        mp = [v[:1] for v in mods]
        ms = [v[1:] for v in mods]
        j = i // 2
        if i % 2 == 0:
            ev = (ev_w_in[j].astype(jnp.bfloat16), ev_w_out[j].astype(jnp.bfloat16), na_q_g[j], na_k_g[j],
                  hy_conv_w[j], hy_conv_b[j],
                  hy_w1[j], hy_b1[j], hy_f1[j], hy_w2[j], hy_b2[j], hy_f2[j], hy_w3[j], hy_decay[j], hy_bias[j])
            bias_tabs = na_bias_tables(na_rpb[j], ls // GRID_W)
            xp, kp, vp = even_mixer(xp, mp, norm_mix_g[i], bp, lp, ev, None, None)
            xs, _, _ = even_mixer(xs, ms, norm_mix_g[i], bs, ls, ev, (cache_na_k[:, j], cache_na_v[:, j]), bias_tabs)
            new_na_k.append(kp)
            new_na_v.append(vp)
        else:
            lam_init = 0.8 - 0.6 * math.exp(-0.3 * i)
            s5_prep = s5_prepare(s5_lam_re[j], s5_lam_im[j], s5_log_dt[j], s5_b_re[j], s5_b_im[j],
                                 s5_c_re[j], s5_c_im[j])
            od = (od_w_in[j].astype(jnp.bfloat16), od_w_out[j].astype(jnp.bfloat16), s5_prep, s5_d[j],
                  s5_glu_w[j], s5_glu_b[j], diff_q_g[j], diff_k_g[j],
                  diff_lq1[j], diff_lk1[j], diff_lq2[j], diff_lk2[j], diff_subln_g[j])
            xp, kp, vp, sr, si = odd_mixer(xp, mp, norm_mix_g[i], bp, lp, od, lam_init, None, None, None, rope_p)
            xs, _, _, _, _ = odd_mixer(xs, ms, norm_mix_g[i], bs, ls, od, lam_init, state_s5_re[:, j],
                                       state_s5_im[:, j], (cache_diff_k[:, j], cache_diff_v[:, j]), rope_s)
            new_dk.append(kp)
            new_dv.append(vp)
            new_sr.append(sr)
            new_si.append(si)
        u_bf16 = peer_u[i].astype(jnp.bfloat16)
        vt_bf16 = jnp.swapaxes(peer_v[i].reshape(-1, PEER_EXPERT_TILE, d), 1, 2).astype(jnp.bfloat16)
        w_q_bf16 = peer_w_q[i].astype(jnp.bfloat16)
        xp = peer_layer(xp, norm_ffn_g[i], mp, w_q_bf16, peer_keys[i], u_bf16, vt_bf16)
        xs = peer_layer(xs, norm_ffn_g[i], ms, w_q_bf16, peer_keys[i], u_bf16, vt_bf16)
    return (xp.reshape(bp, lp, d), xs.reshape(bs, ls, d), jnp.stack(new_na_k, axis=1), jnp.stack(new_na_v, axis=1),
            jnp.stack(new_dk, axis=1), jnp.stack(new_dv, axis=1),
            jnp.stack(new_sr, axis=1), jnp.stack(new_si, axis=1))
```

```python
import functools
import math

import jax
import jax.numpy as jnp
from jax import lax
from jax.experimental import pallas as pl
from jax.experimental.pallas import tpu as pltpu

D_MODEL = 2048
DEPTH = 2
GRID_W = 64
MIX_WIDTH = D_MODEL // 2
HEAD_DIM = 128
NA_HEADS = MIX_WIDTH // HEAD_DIM
NA_KH = 8
NA_KW = 16
HY_WIDTH = MIX_WIDTH
HY_ORDER = 2
HY_POS_EMB = 33
S5_WIDTH = MIX_WIDTH
S5_GROUP = 16
S5_GROUPS = S5_WIDTH // S5_GROUP
S5_STATE = 64
DIFF_HEADS = MIX_WIDTH // HEAD_DIM
DIFF_QK_DIM = HEAD_DIM // 2
DIFF_V_DIM = HEAD_DIM
ROPE_BASE = 10000.0
PEER_HEADS = 8
PEER_N_KEYS = 128
PEER_QUERY_DIM = 256
PEER_TOPK = 16
NORM_EPS = 1e-6
NEG_INF = -1e30

LANES = 128
SUBLANES = 8
VMEM_LIMIT_BYTES = 56 * 1024 * 1024


MM_ROW_TILE = 1024
MM_COL_TILE = 512


def _mm_norm_kernel(x_ref, g_ref, sc_ref, sh_ref, w_ref, *out_refs, emit_h):
    if emit_h:
        o_ref, h_ref, hs_ref = out_refs
    else:
        o_ref, hs_ref = out_refs

    @pl.when(pl.program_id(1) == 0)
    def _():
        x = x_ref[...]
        y = x * lax.rsqrt(jnp.mean(x * x, axis=-1, keepdims=True) + NORM_EPS) * g_ref[...]
        hs_ref[...] = (y * (1.0 + sc_ref[0]) + sh_ref[0]).astype(jnp.bfloat16)
        if emit_h:
            h_ref[...] = hs_ref[...]

    o_ref[...] = jnp.dot(hs_ref[...], w_ref[...].astype(jnp.bfloat16), preferred_element_type=jnp.float32)


def mm_norm(x, norm_g, shift, scale, w, emit_h=False):
    m, k = x.shape
    n = w.shape[1]
    tm, tn = min(MM_ROW_TILE, m), min(MM_COL_TILE, n)
    nb = scale.shape[0]
    assert m % tm == 0 and n % tn == 0 and (m // nb) % tm == 0
    tiles_per_mod = (m // nb) // tm
    mod = pl.BlockSpec((1, 1, k), lambda i, j: (i // tiles_per_mod, 0, 0))
    out_specs = [pl.BlockSpec((tm, tn), lambda i, j: (i, j))]
    out_shape = [jax.ShapeDtypeStruct((m, n), jnp.float32)]
    if emit_h:
        out_specs.append(pl.BlockSpec((tm, k), lambda i, j: (i, 0)))
        out_shape.append(jax.ShapeDtypeStruct((m, k), jnp.bfloat16))
    res = pl.pallas_call(
        functools.partial(_mm_norm_kernel, emit_h=emit_h),
        grid=(m // tm, n // tn),
        in_specs=[pl.BlockSpec((tm, k), lambda i, j: (i, 0)),
                  pl.BlockSpec((1, k), lambda i, j: (0, 0)),
                  mod, mod,
                  pl.BlockSpec((k, tn), lambda i, j: (0, j))],
        out_specs=out_specs,
        out_shape=out_shape,
        scratch_shapes=[pltpu.VMEM((tm, k), jnp.bfloat16)],
        compiler_params=pltpu.CompilerParams(dimension_semantics=("parallel", "arbitrary"),
                                             vmem_limit_bytes=VMEM_LIMIT_BYTES),
        name="mm_norm",
    )(x, norm_g.astype(jnp.float32).reshape(1, k), scale, shift, w)
    return res if emit_h else res[0]


def _mm_pair_res_kernel(a_ref, b_ref, wa_ref, wb_ref, r_ref, g_ref, o_ref):
    y = (jnp.dot(a_ref[...].astype(jnp.bfloat16), wa_ref[...].astype(jnp.bfloat16), preferred_element_type=jnp.float32)
         + jnp.dot(b_ref[...].astype(jnp.bfloat16), wb_ref[...].astype(jnp.bfloat16), preferred_element_type=jnp.float32))
    o_ref[...] = r_ref[...] + g_ref[0] * y


def mm_pair_res(a, b, w, resid, gate):
    m, kh = a.shape
    n = w.shape[1]
    tm, tn = min(MM_ROW_TILE, m), min(MM_COL_TILE, n)
    nb = gate.shape[0]
    assert m % tm == 0 and n % tn == 0 and (m // nb) % tm == 0 and w.shape[0] == 2 * kh
    tiles_per_mod = (m // nb) // tm
    return pl.pallas_call(
        _mm_pair_res_kernel,
        grid=(m // tm, n // tn),
        in_specs=[pl.BlockSpec((tm, kh), lambda i, j: (i, 0)),
                  pl.BlockSpec((tm, kh), lambda i, j: (i, 0)),
                  pl.BlockSpec((kh, tn), lambda i, j: (0, j)),
                  pl.BlockSpec((kh, tn), lambda i, j: (1, j)),
                  pl.BlockSpec((tm, tn), lambda i, j: (i, j)),
                  pl.BlockSpec((1, 1, tn), lambda i, j: (i // tiles_per_mod, 0, j))],
        out_specs=pl.BlockSpec((tm, tn), lambda i, j: (i, j)),
        out_shape=jax.ShapeDtypeStruct((m, n), jnp.float32),
        compiler_params=pltpu.CompilerParams(dimension_semantics=("parallel", "arbitrary"),
                                             vmem_limit_bytes=VMEM_LIMIT_BYTES),
        name="mm_pair_res",
    )(a, b, w, w, resid, gate)


def adaln(cond, w, b):
    m = jax.nn.silu(cond) @ w + b
    return jnp.split(m[:, None, :], 6, axis=-1)


def hyena_filters(l, w1, b1, f1, w2, b2, f2, w3, decay):
    t = jnp.linspace(0.0, 1.0, l, dtype=jnp.float32)[:, None]
    bands = (HY_POS_EMB - 1) // 2
    w_ang = 2.0 * math.pi * jnp.arange(l, dtype=jnp.float32)[:, None] / l
    freqs = jnp.linspace(1e-4, bands - 1, bands, dtype=jnp.float32)[None, :]
    z = jnp.concatenate([t, jnp.cos(freqs * w_ang), -jnp.sin(freqs * w_ang)], axis=-1)
    h = jnp.sin(f1 * (z @ w1 + b1))
    h = jnp.sin(f2 * (h @ w2 + b2))
    h = (h @ w3).reshape(l, 2, HY_ORDER, HY_WIDTH).astype(jnp.float32)
    h = h * jnp.exp(-t.reshape(l, 1, 1, 1) * jnp.abs(decay.astype(jnp.float32)))
    h_f, h_b = h[:, 0], h[:, 1]
    zero = jnp.zeros((1, HY_ORDER, HY_WIDTH), jnp.float32)
    return jnp.concatenate([h_f, zero, h_b[1:][::-1]], axis=0)


def _split_bf16(x):
    hi = x.astype(jnp.bfloat16)
    lo = (x - hi.astype(jnp.float32)).astype(jnp.bfloat16)
    return hi, lo


def _dot3(a_hi, a_lo, x):
    m = a_hi.shape[0]
    xh, xl = _split_bf16(x)
    r = jnp.dot(jnp.concatenate([a_hi, a_lo], axis=0), xh, preferred_element_type=jnp.float32)
    return r[:m] + r[m:] + jnp.dot(a_hi, xl, preferred_element_type=jnp.float32)


HY_STEP_ROWS = 256


def hyena_factors(l):
    n = 2 * l
    n2 = 64 if n >= 8192 else 16
    return n // n2, n2


def hyena_tables(l):
    n = 2 * l
    n1, n2 = hyena_factors(l)
    t = (n2 * jnp.arange(n1 // 2)[None, None, :] + jnp.arange(n2)[:, None, None])
    k1 = jnp.arange(n1)[None, :, None]
    ang = (2.0 * math.pi / n) * ((t * k1) % n).astype(jnp.float32)
    fa = jnp.concatenate([jnp.cos(ang), -jnp.sin(ang)], axis=1)
    fc = jnp.swapaxes(fa, 1, 2) / n
    a2 = (2.0 * math.pi / n2) * ((jnp.arange(n2)[:, None] * jnp.arange(n2)[None, :]) % n2).astype(jnp.float32)
    c, s = jnp.cos(a2), jnp.sin(a2)
    fb = jnp.concatenate([jnp.concatenate([c, s], axis=1), jnp.concatenate([-s, c], axis=1)], axis=0)
    fbi = jnp.concatenate([jnp.concatenate([c, -s], axis=1), jnp.concatenate([s, c], axis=1)], axis=0)
    return tuple(_split_bf16(x) for x in (fa, fc, fb, fbi))


def hyena_spectrum(filt, l):
    n1, n2 = hyena_factors(l)
    kf = jnp.fft.fft(filt, axis=0)
    kf = jnp.stack([jnp.real(kf), jnp.imag(kf)], axis=0).astype(jnp.float32)
    kf = kf.reshape(2, n2, n1, HY_ORDER, HY_WIDTH)
    return jnp.transpose(kf, (3, 0, 2, 1, 4))


def _hy_stage_a_kernel(z_ref, fh_ref, fl_ref, o_ref):
    for i in range(z_ref.shape[0]):
        o_ref[i] = _dot3(fh_ref[0], fl_ref[0], z_ref[i])


def _hy_stage_b_kernel(s_ref, k_ref, fh_ref, fl_ref, gh_ref, gl_ref, o_ref):
    n2 = s_ref.shape[3]
    kr, ki = k_ref[0, 0], k_ref[1, 0]
    for i in range(s_ref.shape[0]):
        y = jnp.concatenate([s_ref[i, 0, 0], s_ref[i, 1, 0]], axis=0)
        z = _dot3(fh_ref[...], fl_ref[...], y)
        zr, zi = z[:n2], z[n2:]
        p = jnp.concatenate([zr * kr - zi * ki, zr * ki + zi * kr], axis=0)
        q = _dot3(gh_ref[...], gl_ref[...], p)
        o_ref[i, 0, 0] = q[:n2]
        o_ref[i, 1, 0] = q[n2:]


def _hy_stage_c_kernel(q_ref, fh_ref, fl_ref, z_ref, g_ref, b_ref, o_ref):
    for i in range(q_ref.shape[0]):
        y = _dot3(fh_ref[0], fl_ref[0], q_ref[i])
        o_ref[i] = g_ref[i] * (y + z_ref[i] * b_ref[...])


def hyena_long_conv(z, z_col, gate, gate_col, bias_o, spec_o, tables, l):
    (fah, fal), (fch, fcl), (fbh, fbl), (fgh, fgl) = tables
    b = z.shape[0]
    c = HY_WIDTH
    n1, n2 = hyena_factors(l)
    h1 = n1 // 2
    bb = max(1, min(b, HY_STEP_ROWS // n1))
    assert b % bb == 0
    cp = pltpu.CompilerParams(dimension_semantics=("parallel", "parallel"), vmem_limit_bytes=VMEM_LIMIT_BYTES)
    zblocks = z.shape[-1] // c
    gblocks = gate.shape[-1] // c
    zv = z.reshape(b, h1, n2 * z.shape[-1])
    gv = gate.reshape(b, h1, n2 * gate.shape[-1])
    s1 = pl.pallas_call(
        _hy_stage_a_kernel,
        grid=(b // bb, n2),
        in_specs=[pl.BlockSpec((bb, h1, c), lambda i, j: (i, 0, j * zblocks + z_col)),
                  pl.BlockSpec((1, 2 * n1, h1), lambda i, j: (j, 0, 0)),
                  pl.BlockSpec((1, 2 * n1, h1), lambda i, j: (j, 0, 0))],
        out_specs=pl.BlockSpec((bb, 2 * n1, c), lambda i, j: (i, 0, j)),
        out_shape=jax.ShapeDtypeStruct((b, 2 * n1, n2 * c), jnp.float32),
        compiler_params=cp, name="hyena_stage_a",
    )(zv, fah, fal)
    s1 = s1.reshape(b, 2, n1, n2, c)
    mat = pl.BlockSpec((2 * n2, 2 * n2), lambda k, i: (0, 0))
    q = pl.pallas_call(
        _hy_stage_b_kernel,
        grid=(n1, b // bb),
        in_specs=[pl.BlockSpec((bb, 2, 1, n2, c), lambda k, i: (i, 0, k, 0, 0)),
                  pl.BlockSpec((2, 1, n2, c), lambda k, i: (0, k, 0, 0)),
                  mat, mat, mat, mat],
        out_specs=pl.BlockSpec((bb, 2, 1, n2, c), lambda k, i: (i, 0, k, 0, 0)),
        out_shape=jax.ShapeDtypeStruct((b, 2, n1, n2, c), jnp.float32),
        compiler_params=cp, name="hyena_stage_b",
    )(s1, spec_o, fbh, fbl, fgh, fgl)
    q = q.reshape(b, 2 * n1, n2 * c)
    out = pl.pallas_call(
        _hy_stage_c_kernel,
        grid=(b // bb, n2),
        in_specs=[pl.BlockSpec((bb, 2 * n1, c), lambda i, j: (i, 0, j)),
                  pl.BlockSpec((1, h1, 2 * n1), lambda i, j: (j, 0, 0)),
                  pl.BlockSpec((1, h1, 2 * n1), lambda i, j: (j, 0, 0)),
                  pl.BlockSpec((bb, h1, c), lambda i, j: (i, 0, j * zblocks + z_col)),
                  pl.BlockSpec((bb, h1, c), lambda i, j: (i, 0, j * gblocks + gate_col)),
                  pl.BlockSpec((1, c), lambda i, j: (0, 0))],
        out_specs=pl.BlockSpec((bb, h1, c), lambda i, j: (i, 0, j)),
        out_shape=jax.ShapeDtypeStruct((b, h1, n2 * c), jnp.float32),
        compiler_params=cp, name="hyena_stage_c",
    )(q, fch, fcl, zv, gv, bias_o.astype(jnp.float32).reshape(1, c))
    return out.reshape(b, l, c)


def hyena(u, conv_w, conv_b, filt, bias):
    b, l, _ = u.shape
    up = jnp.pad(u, ((0, 0), (1, 1), (0, 0)))
    u = up[:, :-2] * conv_w[0] + up[:, 1:-1] * conv_w[1] + up[:, 2:] * conv_w[2] + conv_b
    tables = hyena_tables(l)
    spec = hyena_spectrum(hyena_filters(l, *filt), l)
    z, z_col = u, 0
    for o in range(HY_ORDER):
        z = hyena_long_conv(z, z_col, u, 1 + o, bias[o], spec[o], tables, l)
        z_col = 0
    return z


S5_TILE_GROUPS = 8
S5_TILES = S5_GROUPS // S5_TILE_GROUPS
S5_TILE_IN = S5_TILE_GROUPS * S5_GROUP
S5_TILE_STATE = S5_TILE_GROUPS * S5_STATE
S5_ROWS_PER_STEP = 2048


def s5_prepare(lam_re, lam_im, log_dt, b_re, b_im, c_re, c_im):
    f32 = jnp.float32
    lam_re, lam_im = lam_re.astype(f32), lam_im.astype(f32)
    dt = jnp.exp(log_dt.astype(f32))[..., None]
    mag = jnp.exp(lam_re * dt)
    ar, ai = mag * jnp.cos(lam_im * dt), mag * jnp.sin(lam_im * dt)
    den = lam_re * lam_re + lam_im * lam_im
    cr = ((ar - 1.0) * lam_re + ai * lam_im) / den
    ci = (ai * lam_re - (ar - 1.0) * lam_im) / den
    bbr = cr[..., None] * b_re - ci[..., None] * b_im
    bbi = cr[..., None] * b_im + ci[..., None] * b_re
    eye = jnp.eye(S5_TILE_GROUPS, dtype=f32)

    def tile_in(bb):
        x = bb.reshape(2, S5_TILES, S5_TILE_GROUPS, S5_STATE, S5_GROUP)
        x = jnp.einsum('dtgnp,gh->dtgphn', x, eye)
        return x.reshape(2, S5_TILES, S5_TILE_IN, S5_TILE_STATE)

    def tile_out(cc):
        x = cc.astype(f32).reshape(2, S5_TILES, S5_TILE_GROUPS, S5_GROUP, S5_STATE)
        x = jnp.einsum('dtgpn,gh->dtgnhp', x, eye)
        return x.reshape(2, S5_TILES, S5_TILE_STATE, S5_TILE_IN)

    win = jnp.concatenate([tile_in(bbr), tile_in(bbi)], axis=-1).astype(jnp.bfloat16)
    wout = jnp.concatenate([tile_out(c_re), -tile_out(c_im)], axis=-2).astype(jnp.bfloat16)
    a = jnp.stack([ar.reshape(2, S5_TILES, S5_TILE_STATE), ai.reshape(2, S5_TILES, S5_TILE_STATE)], axis=2)
    return win, wout, a


def _s5_scan_kernel(u_ref, win_ref, wout_ref, a_ref, h0_ref, y_ref, fin_ref, bu_ref, st_ref, *, batch):
    d = pl.program_id(0)
    c = pl.program_id(2)
    n_c = pl.num_programs(2)
    ns = S5_TILE_STATE
    steps = u_ref.shape[0] // batch

    @pl.when(c == 0)
    def _():
        st_ref[...] = h0_ref[0, 0]

    bu_ref[...] = jnp.dot(u_ref[...].astype(jnp.bfloat16), win_ref[0, 0], preferred_element_type=jnp.float32)
    ar = jnp.broadcast_to(a_ref[0, 0, 0:1, :], (batch, ns))
    ai = jnp.broadcast_to(a_ref[0, 0, 1:2, :], (batch, ns))

    def step(t, carry):
        hr, hi = carry
        te = jnp.where(d == 0, t, steps - 1 - t)
        r0 = pl.multiple_of(te * batch, batch)
        nr = ar * hr - ai * hi + bu_ref[pl.ds(r0, batch), 0:ns]
        ni = ar * hi + ai * hr + bu_ref[pl.ds(r0, batch), ns:2 * ns]
        bu_ref[pl.ds(r0, batch), 0:ns] = nr
        bu_ref[pl.ds(r0, batch), ns:2 * ns] = ni
        return nr, ni

    hr, hi = lax.fori_loop(0, steps, step, (st_ref[:, 0:ns], st_ref[:, ns:2 * ns]), unroll=4)
    st_ref[:, 0:ns] = hr
    st_ref[:, ns:2 * ns] = hi
    y_ref[0] = jnp.dot(bu_ref[...].astype(jnp.bfloat16), wout_ref[0, 0], preferred_element_type=jnp.float32)

    @pl.when(c == n_c - 1)
    def _():
        fin_ref[0, 0] = st_ref[...]


def s5_scan(u_tm, win, wout, a, h0, batch):
    rows = u_tm.shape[0]
    r = min(S5_ROWS_PER_STEP, rows)
    assert rows % r == 0 and r % batch == 0
    n_c = rows // r

    def chunk(d, c):
        return c + d * (n_c - 1 - 2 * c)

    return pl.pallas_call(
        functools.partial(_s5_scan_kernel, batch=batch),
        grid=(2, S5_TILES, n_c),
        in_specs=[pl.BlockSpec((r, S5_TILE_IN), lambda d, j, c: (chunk(d, c), j)),
                  pl.BlockSpec((1, 1, S5_TILE_IN, 2 * S5_TILE_STATE), lambda d, j, c: (d, j, 0, 0)),
                  pl.BlockSpec((1, 1, 2 * S5_TILE_STATE, S5_TILE_IN), lambda d, j, c: (d, j, 0, 0)),
                  pl.BlockSpec((1, 1, 2, S5_TILE_STATE), lambda d, j, c: (d, j, 0, 0)),
                  pl.BlockSpec((1, 1, batch, 2 * S5_TILE_STATE), lambda d, j, c: (d, j, 0, 0))],
        out_specs=[pl.BlockSpec((1, r, S5_TILE_IN), lambda d, j, c: (d, chunk(d, c), j)),
                   pl.BlockSpec((1, 1, batch, 2 * S5_TILE_STATE), lambda d, j, c: (d, j, 0, 0))],
        out_shape=[jax.ShapeDtypeStruct((2, rows, S5_WIDTH), jnp.float32),
                   jax.ShapeDtypeStruct((2, S5_TILES, batch, 2 * S5_TILE_STATE), jnp.float32)],
        scratch_shapes=[pltpu.VMEM((r, 2 * S5_TILE_STATE), jnp.float32),
                        pltpu.VMEM((batch, 2 * S5_TILE_STATE), jnp.float32)],
        compiler_params=pltpu.CompilerParams(dimension_semantics=("parallel", "parallel", "arbitrary"),
                                             vmem_limit_bytes=VMEM_LIMIT_BYTES),
        name="s5_scan",
    )(u_tm, win, wout, a, h0)


def _gelu_tanh(x):
    return 0.5 * x * (1.0 + jnp.tanh(math.sqrt(2.0 / math.pi) * (x + 0.044715 * (x * x * x))))


def _s5_glu_kernel(u_ref, y_ref, d_ref, w_ref, b_ref, o_ref):
    y = d_ref[...] * u_ref[...] + y_ref[0] + y_ref[1]
    y = _gelu_tanh(y)
    z = jnp.dot(y.astype(jnp.bfloat16), w_ref[...], preferred_element_type=jnp.float32) + b_ref[...]
    o_ref[...] = y * (1.0 / (1.0 + jnp.exp(-z)))


def s5_glu(u_tm, y, d_skip, glu_w_bf16, glu_b, tr=1024):
    rows, w = u_tm.shape
    tr = min(tr, rows)
    assert rows % tr == 0
    return pl.pallas_call(
        _s5_glu_kernel,
        grid=(rows // tr,),
        in_specs=[pl.BlockSpec((tr, w), lambda i: (i, 0)),
                  pl.BlockSpec((2, tr, w), lambda i: (0, i, 0)),
                  pl.BlockSpec((1, w), lambda i: (0, 0)),
                  pl.BlockSpec((w, w), lambda i: (0, 0)),
                  pl.BlockSpec((1, w), lambda i: (0, 0))],
        out_specs=pl.BlockSpec((tr, w), lambda i: (i, 0)),
        out_shape=jax.ShapeDtypeStruct((rows, w), jnp.float32),
        compiler_params=pltpu.CompilerParams(dimension_semantics=("parallel",),
                                             vmem_limit_bytes=VMEM_LIMIT_BYTES),
        name="s5_glu",
    )(u_tm, y, d_skip.reshape(1, w), glu_w_bf16, glu_b.reshape(1, w))


def s5_mixer(u, h0_re, h0_im, prep, d_skip, glu_w, glu_b):
    win, wout, a = prep
    b, l, w = u.shape
    u_tm = jnp.swapaxes(u, 0, 1).reshape(l * b, w)
    if h0_re is None:
        h0 = jnp.zeros((2, S5_TILES, b, 2 * S5_TILE_STATE), jnp.float32)
    else:
        def tiles(h):
            return jnp.transpose(h.astype(jnp.float32).reshape(b, 2, S5_TILES, S5_TILE_STATE), (1, 2, 0, 3))
        h0 = jnp.concatenate([tiles(h0_re), tiles(h0_im)], axis=-1)
    y, fin = s5_scan(u_tm, win, wout, a, h0, b)
    out_tm = s5_glu(u_tm, y, d_skip, glu_w.astype(jnp.bfloat16), glu_b)
    out = jnp.swapaxes(out_tm.reshape(l, b, w), 0, 1)

    def untile(f):
        return jnp.transpose(f, (2, 0, 1, 3)).reshape(b, 2, S5_GROUPS, S5_STATE)
    return out, untile(fin[..., :S5_TILE_STATE]), untile(fin[..., S5_TILE_STATE:])


def _half_mean_matrix():
    i = lax.broadcasted_iota(jnp.int32, (LANES, LANES), 0) // DIFF_QK_DIM
    j = lax.broadcasted_iota(jnp.int32, (LANES, LANES), 1) // DIFF_QK_DIM
    return jnp.where(i == j, 1.0 / DIFF_QK_DIM, 0.0).astype(jnp.bfloat16)


def _rms_groups(x, gain, avg):
    xx = x * x
    hi = xx.astype(jnp.bfloat16)
    lo = (xx - hi.astype(jnp.float32)).astype(jnp.bfloat16)
    ms = (jnp.dot(hi, avg, preferred_element_type=jnp.float32)
          + jnp.dot(lo, avg, preferred_element_type=jnp.float32))
    return x * lax.rsqrt(ms + NORM_EPS) * gain


def _rope_lanes(y, cos, sin_signed, first_half):
    rot = jnp.where(first_half, pltpu.roll(y, LANES - 16, 1), pltpu.roll(y, 16, 1))
    return y * cos + rot * sin_signed


def _diff_prep_kernel(q_ref, k_ref, qg_ref, kg_ref, cos_ref, sin_ref, qo_ref, ko_ref, kn_ref, *, use_rope):
    avg = _half_mean_matrix()
    lane = lax.broadcasted_iota(jnp.int32, (1, LANES), 1)
    first_half = (lane % 32) < 16
    scale = DIFF_QK_DIM ** -0.5 * math.log2(math.e)
    for h in range(DIFF_HEADS):
        cols = slice(h * LANES, (h + 1) * LANES)
        qn = _rms_groups(q_ref[:, cols], qg_ref[...], avg)
        kn = _rms_groups(k_ref[:, cols], kg_ref[...], avg)
        kn_ref[:, cols] = kn
        if use_rope:
            qn = _rope_lanes(qn, cos_ref[...], sin_ref[...], first_half)
            kn = _rope_lanes(kn, cos_ref[...], sin_ref[...], first_half)
        qo_ref[:, cols] = (qn * scale).astype(jnp.bfloat16)
        ko_ref[:, cols] = kn.astype(jnp.bfloat16)


def diff_prep(p, q_g, k_g, cos, sin_signed, seq_len, use_rope, tr=512):
    rows = p.shape[0]
    tr = min(tr, seq_len)
    assert seq_len % tr == 0
    w = DIFF_HEADS * LANES
    per_seq = seq_len // tr
    g2 = lambda g: jnp.tile(g.astype(jnp.float32), 2).reshape(1, LANES)
    return pl.pallas_call(
        functools.partial(_diff_prep_kernel, use_rope=use_rope),
        grid=(rows // tr,),
        in_specs=[pl.BlockSpec((tr, w), lambda i: (i, 1)),
                  pl.BlockSpec((tr, w), lambda i: (i, 2)),
                  pl.BlockSpec((1, LANES), lambda i: (0, 0)),
                  pl.BlockSpec((1, LANES), lambda i: (0, 0)),
                  pl.BlockSpec((tr, LANES), lambda i: (i % per_seq, 0)),
                  pl.BlockSpec((tr, LANES), lambda i: (i % per_seq, 0))],
        out_specs=[pl.BlockSpec((tr, w), lambda i: (i, 0))] * 3,
        out_shape=[jax.ShapeDtypeStruct((rows, w), jnp.bfloat16), jax.ShapeDtypeStruct((rows, w), jnp.bfloat16),
                   jax.ShapeDtypeStruct((rows, w), jnp.float32)],
        compiler_params=pltpu.CompilerParams(dimension_semantics=("parallel",),
                                             vmem_limit_bytes=VMEM_LIMIT_BYTES),
        name="diff_prep",
    )(p, p, g2(q_g), g2(k_g), cos, sin_signed)


def rope_tables(l):
    t = jnp.arange(l)
    row = (t // GRID_W).astype(jnp.float32)
    col = (t % GRID_W).astype(jnp.float32)
    nf = DIFF_QK_DIM // 4
    inv = ROPE_BASE ** (-jnp.arange(nf, dtype=jnp.float32) / nf)
    ang = jnp.stack([row[:, None] * inv, col[:, None] * inv], axis=1)
    ang = jnp.stack([ang, ang], axis=2).reshape(l, DIFF_QK_DIM)
    sign = jnp.where((jnp.arange(DIFF_QK_DIM) % 32) < 16, -1.0, 1.0)
    return jnp.tile(jnp.cos(ang), (1, 2)), jnp.tile(jnp.sin(ang) * sign, (1, 2))


def _diff_attn_kernel(*refs, has_ctx):
    if has_ctx:
        q_ref, k_ref, v_ref, ck_ref, cv_ref, lam_ref, g_ref, o_ref = refs
    else:
        q_ref, k_ref, v_ref, lam_ref, g_ref, o_ref = refs
    q = q_ref[...]
    lane = lax.broadcasted_iota(jnp.int32, (1, LANES), 1)
    zero = jnp.zeros_like(q)
    qs = (jnp.where(lane < DIFF_QK_DIM, q, zero), jnp.where(lane >= DIFF_QK_DIM, q, zero))
    nt = (((1,), (1,)), ((), ()))
    k = k_ref[...]
    ck = ck_ref[...].astype(jnp.bfloat16) if has_ctx else None
    lam = lam_ref[0:1, 0:1]
    w_self, w_ctx = None, None
    for i in range(2):
        s = lax.dot_general(qs[i], k, nt, preferred_element_type=jnp.float32)
        m = jnp.max(s, axis=-1, keepdims=True)
        if has_ctx:
            sc = lax.dot_general(qs[i], ck, nt, preferred_element_type=jnp.float32)
            m = jnp.maximum(m, jnp.max(sc, axis=-1, keepdims=True))
        p = jnp.exp2(s - m)
        l = jnp.sum(p, axis=-1, keepdims=True)
        if has_ctx:
            pc = jnp.exp2(sc - m)
            l = l + jnp.sum(pc, axis=-1, keepdims=True)
        coef = 1.0 / l if i == 0 else -lam / l
        w_self = p * coef if i == 0 else w_self + p * coef
        if has_ctx:
            w_ctx = pc * coef if i == 0 else w_ctx + pc * coef
    o = jnp.dot(w_self.astype(jnp.bfloat16), v_ref[...].astype(jnp.bfloat16), preferred_element_type=jnp.float32)
    if has_ctx:
        o = o + jnp.dot(w_ctx.astype(jnp.bfloat16), cv_ref[...].astype(jnp.bfloat16),
                        preferred_element_type=jnp.float32)
    ms = jnp.mean(o * o, axis=-1, keepdims=True)
    o_ref[...] = o * lax.rsqrt(ms + NORM_EPS) * g_ref[...]


def diff_attention(q_bf16, k_bf16, p, ctx_k, ctx_v, lam, gain, batch, seq_len, tq=256):
    tq = min(tq, seq_len)
    nq = seq_len // tq
    has_ctx = ctx_k is not None
    v_col0 = 3 * DIFF_HEADS
    in_specs = [pl.BlockSpec((tq, LANES), lambda b, h, i: (b * nq + i, h)),
                pl.BlockSpec((seq_len, LANES), lambda b, h, i: (b, h)),
                pl.BlockSpec((seq_len, LANES), lambda b, h, i: (b, v_col0 + h))]
    args = [q_bf16, k_bf16, p]
    if has_ctx:
        lc = ctx_k.shape[0] // batch
        in_specs += [pl.BlockSpec((lc, LANES), lambda b, h, i: (b, h))] * 2
        args += [ctx_k, ctx_v]
    in_specs += [pl.BlockSpec((1, LANES), lambda b, h, i: (0, 0))] * 2
    args += [jnp.broadcast_to(lam.astype(jnp.float32), (1, LANES)), gain.astype(jnp.float32).reshape(1, LANES)]
    return pl.pallas_call(
        functools.partial(_diff_attn_kernel, has_ctx=has_ctx),
        grid=(batch, DIFF_HEADS, nq),
        in_specs=in_specs,
        out_specs=pl.BlockSpec((tq, LANES), lambda b, h, i: (b * nq + i, h)),
        out_shape=jax.ShapeDtypeStruct((batch * seq_len, DIFF_HEADS * LANES), jnp.float32),
        compiler_params=pltpu.CompilerParams(dimension_semantics=("parallel", "parallel", "arbitrary"),
                                             vmem_limit_bytes=VMEM_LIMIT_BYTES),
        name="diff_attention",
    )(*args)


NA_Q_ROWS = 8
NA_WIN_ROWS = 16


def _na_prep_kernel(q_ref, k_ref, qg_ref, kg_ref, qo_ref, ko_ref, kn_ref):
    avg = jnp.full((LANES, LANES), 1.0 / HEAD_DIM, jnp.bfloat16)
    for h in range(NA_HEADS):
        cols = slice(h * LANES, (h + 1) * LANES)
        qo_ref[:, cols] = _rms_groups(q_ref[:, cols], qg_ref[...], avg).astype(jnp.bfloat16)
        kn = _rms_groups(k_ref[:, cols], kg_ref[...], avg)
        kn_ref[:, cols] = kn
        ko_ref[:, cols] = kn.astype(jnp.bfloat16)


def na_prep(p, q_g, k_g, tr=512):
    rows = p.shape[0]
    tr = min(tr, rows)
    w = NA_HEADS * LANES
    g1 = lambda g: g.astype(jnp.float32).reshape(1, LANES)
    return pl.pallas_call(
        _na_prep_kernel,
        grid=(rows // tr,),
        in_specs=[pl.BlockSpec((tr, w), lambda i: (i, 0)),
                  pl.BlockSpec((tr, w), lambda i: (i, 1)),
                  pl.BlockSpec((1, LANES), lambda i: (0, 0)),
                  pl.BlockSpec((1, LANES), lambda i: (0, 0))],
        out_specs=[pl.BlockSpec((tr, w), lambda i: (i, 0))] * 3,
        out_shape=[jax.ShapeDtypeStruct((rows, w), jnp.bfloat16), jax.ShapeDtypeStruct((rows, w), jnp.bfloat16),
                   jax.ShapeDtypeStruct((rows, w), jnp.float32)],
        compiler_params=pltpu.CompilerParams(dimension_semantics=("parallel",),
                                             vmem_limit_bytes=VMEM_LIMIT_BYTES),
        name="na_prep",
    )(p, p, g1(q_g), g1(k_g))


def na_bias_tables(rpb, rows):
    nblk = rows // NA_Q_ROWS
    cq = jnp.arange(GRID_W)[:, None]
    ck = jnp.arange(GRID_W)[None, :]
    col_start = jnp.clip(cq - NA_KW // 2, 0, GRID_W - NA_KW)
    col_ok = (ck >= col_start) & (ck < col_start + NA_KW)
    cidx = jnp.clip(ck - cq + NA_KW - 1, 0, 2 * NA_KW - 2)
    tabs = []
    for i in (0, 1, nblk - 1):
        base = min(max(NA_Q_ROWS * i - NA_KH // 2, 0), rows - NA_WIN_ROWS)
        r = NA_Q_ROWS * i + jnp.arange(NA_Q_ROWS)[:, None]
        rk = base + jnp.arange(NA_WIN_ROWS)[None, :]
        r0 = jnp.clip(r - NA_KH // 2, 0, rows - NA_KH)
        row_ok = (rk >= r0) & (rk < r0 + NA_KH)
        ridx = jnp.clip(rk - r + NA_KH - 1, 0, 2 * NA_KH - 2)
        b = rpb[:, ridx][:, :, :, cidx]
        ok = row_ok[:, :, None, None] & col_ok[None, None, :, :]
        b = jnp.where(ok[None], b.astype(jnp.float32), NEG_INF)
        tabs.append(jnp.transpose(b, (0, 1, 3, 2, 4)).reshape(NA_HEADS, NA_Q_ROWS * GRID_W, NA_WIN_ROWS * GRID_W))
    return jnp.stack(tabs, axis=0)


def _na_attn_kernel(*refs, windowed, rows):
    nt = (((1,), (1,)), ((), ()))
    scale = HEAD_DIM ** -0.5
    if windowed:
        q_ref, k_ref, v_ref, b_ref, ck_ref, cv_ref, o_ref = refs
        i = pl.program_id(2)
        base = jnp.clip(NA_Q_ROWS * i - NA_KH // 2, 0, rows - NA_WIN_ROWS)
        k0 = pl.multiple_of(base * GRID_W, GRID_W)
        nk = NA_WIN_ROWS * GRID_W
        q = q_ref[...]
        s = lax.dot_general(q, k_ref[pl.ds(k0, nk), :], nt, preferred_element_type=jnp.float32) * scale + b_ref[0, 0]
        sc = lax.dot_general(q, ck_ref[...].astype(jnp.bfloat16), nt, preferred_element_type=jnp.float32) * scale
        m = jnp.maximum(jnp.max(s, axis=-1, keepdims=True), jnp.max(sc, axis=-1, keepdims=True))
        p = jnp.exp(s - m)
        pc = jnp.exp(sc - m)
        l = jnp.sum(p, axis=-1, keepdims=True) + jnp.sum(pc, axis=-1, keepdims=True)
        o = (jnp.dot(p.astype(jnp.bfloat16), v_ref[pl.ds(k0, nk), :].astype(jnp.bfloat16),
                     preferred_element_type=jnp.float32)
             + jnp.dot(pc.astype(jnp.bfloat16), cv_ref[...].astype(jnp.bfloat16), preferred_element_type=jnp.float32))
    else:
        q_ref, k_ref, v_ref, o_ref = refs
        s = lax.dot_general(q_ref[...], k_ref[...], nt, preferred_element_type=jnp.float32) * scale
        m = jnp.max(s, axis=-1, keepdims=True)
        p = jnp.exp(s - m)
        l = jnp.sum(p, axis=-1, keepdims=True)
        o = jnp.dot(p.astype(jnp.bfloat16), v_ref[...].astype(jnp.bfloat16), preferred_element_type=jnp.float32)
    o_ref[...] = o / l


def na_attention(q_bf16, k_bf16, p, bias_tabs, ctx_k, ctx_v, batch, seq_len):
    windowed = bias_tabs is not None
    v_col0 = 2 * NA_HEADS
    rows = seq_len // GRID_W
    if windowed:
        tq = NA_Q_ROWS * GRID_W
        nq = seq_len // tq
        lc = ctx_k.shape[0] // batch
        nk = NA_WIN_ROWS * GRID_W
        in_specs = [pl.BlockSpec((tq, LANES), lambda b, h, i: (b * nq + i, h)),
                    pl.BlockSpec((seq_len, LANES), lambda b, h, i: (b, h)),
                    pl.BlockSpec((seq_len, LANES), lambda b, h, i: (b, v_col0 + h)),
                    pl.BlockSpec((1, 1, tq, nk),
                                 lambda b, h, i: (jnp.where(i == 0, 0, jnp.where(i == nq - 1, 2, 1)), h, 0, 0)),
                    pl.BlockSpec((lc, LANES), lambda b, h, i: (b, h)),
                    pl.BlockSpec((lc, LANES), lambda b, h, i: (b, h))]
        args = [q_bf16, k_bf16, p, bias_tabs, ctx_k, ctx_v]
    else:
        tq = seq_len
        nq = 1
        in_specs = [pl.BlockSpec((tq, LANES), lambda b, h, i: (b, h)),
                    pl.BlockSpec((seq_len, LANES), lambda b, h, i: (b, h)),
                    pl.BlockSpec((seq_len, LANES), lambda b, h, i: (b, v_col0 + h))]
        args = [q_bf16, k_bf16, p]
    return pl.pallas_call(
        functools.partial(_na_attn_kernel, windowed=windowed, rows=rows),
        grid=(batch, NA_HEADS, nq),
        in_specs=in_specs,
        out_specs=pl.BlockSpec((tq, LANES), lambda b, h, i: (b * nq + i, h)),
        out_shape=jax.ShapeDtypeStruct((batch * seq_len, NA_HEADS * LANES), jnp.float32),
        compiler_params=pltpu.CompilerParams(dimension_semantics=("parallel", "parallel", "arbitrary"),
                                             vmem_limit_bytes=VMEM_LIMIT_BYTES),
        name="na_attention",
    )(*args)


def even_mixer(x, mods, norm_g, b, l, ev, ctx_kv, bias_tabs):
    (w_in, w_out, q_g, k_g, conv_w, conv_b, w1, b1, f1, w2, b2, f2, w3, decay, bias) = ev
    p = mm_norm(x, norm_g, mods[0], mods[1], w_in)
    q, k, kn = na_prep(p, q_g, k_g)
    if ctx_kv is None:
        oa = na_attention(q, k, p, None, None, None, b, l)
    else:
        lc = ctx_kv[0].shape[1]
        oa = na_attention(q, k, p, bias_tabs, ctx_kv[0].reshape(b * lc, MIX_WIDTH),
                          ctx_kv[1].reshape(b * lc, MIX_WIDTH), b, l)
    hb = p[:, 3 * MIX_WIDTH:].reshape(b, l, 3 * HY_WIDTH)
    ob = hyena(hb, conv_w, conv_b, (w1, b1, f1, w2, b2, f2, w3, decay), bias)
    out = mm_pair_res(oa, ob.reshape(b * l, HY_WIDTH), w_out, x, mods[2])
    shp = (b, l, NA_HEADS, HEAD_DIM)
    return out, kn.reshape(shp), p[:, 2 * MIX_WIDTH:3 * MIX_WIDTH].reshape(shp)


def odd_mixer(x, mods, norm_g, b, l, od, lam_init, h0_re, h0_im, ctx_kv, rope):
    (w_in, w_out, s5_prep, d_skip, glu_w, glu_b, q_g, k_g, lq1, lk1, lq2, lk2, subln_g) = od
    p = mm_norm(x, norm_g, mods[0], mods[1], w_in)
    oc, fr, fi = s5_mixer(p[:, :S5_WIDTH].reshape(b, l, S5_WIDTH), h0_re, h0_im, s5_prep, d_skip, glu_w, glu_b)
    lam = (jnp.exp(jnp.sum(lq1 * lk1).astype(jnp.float32))
           - jnp.exp(jnp.sum(lq2 * lk2).astype(jnp.float32)) + lam_init)
    cos, sin_signed = rope
    q, k, kn = diff_prep(p, q_g, k_g, cos, sin_signed, l, use_rope=ctx_kv is not None)
    gain = subln_g.astype(jnp.float32) * (1.0 - lam_init)
    if ctx_kv is None:
        o = diff_attention(q, k, p, None, None, lam, gain, b, l)
    else:
        lc = ctx_kv[0].shape[1]
        o = diff_attention(q, k, p, ctx_kv[0].reshape(b * lc, MIX_WIDTH), ctx_kv[1].reshape(b * lc, MIX_WIDTH),
                           lam, gain, b, l)
    out = mm_pair_res(oc.reshape(b * l, S5_WIDTH), o, w_out, x, mods[2])
    kn = kn.reshape(b, l, DIFF_HEADS, 2, DIFF_QK_DIM)
    v = p[:, S5_WIDTH + 2 * MIX_WIDTH:].reshape(b, l, DIFF_HEADS, DIFF_V_DIM)
    return out, kn, v, fr, fi


PEER_TOKEN_TILE = 512
PEER_EXPERT_TILE = 1024
PEER_SUB_EXPERTS = 256


def _top16_rows(cur, iota):
    n = cur.shape[0]
    tops, idxs = [], []
    for _ in range(PEER_TOPK):
        m = jnp.max(cur, axis=0, keepdims=True)
        first = jnp.min(jnp.where(cur == m, iota, n), axis=0, keepdims=True)
        tops.append(m)
        idxs.append(first)
        cur = jnp.where(iota == first, -jnp.inf, cur)
    return tops, idxs


def _router_kernel(q_ref, keys_ref, m_ref, r1_ref, e0_ref, e1_ref, st_ref):
    tt = q_ref.shape[0]
    half = PEER_QUERY_DIM // 2
    for h in range(PEER_HEADS):
        for s in range(2):
            qs = q_ref[:, (2 * h + s) * half:(2 * h + s + 1) * half].astype(jnp.bfloat16)
            st = lax.dot_general(keys_ref[h, s], qs, (((1,), (1,)), ((), ())),
                                 preferred_element_type=jnp.float32)
            st_ref[2 * h + s] = st

    iota128 = lax.broadcasted_iota(jnp.int32, (PEER_N_KEYS, LANES), 0)
    iota8 = lax.broadcasted_iota(jnp.int32, (SUBLANES, LANES), 0)
    iota_cand = lax.broadcasted_iota(jnp.int32, (PEER_TOPK + SUBLANES * SUBLANES, LANES), 0)
    n_chunks = tt // LANES

    def route(h, lane0):
        s0 = st_ref[2 * h, :, pl.ds(lane0, LANES)]
        s1 = st_ref[2 * h + 1, :, pl.ds(lane0, LANES)]
        a, a_idx = _top16_rows(s0, iota128)
        b, b_idx = _top16_rows(s1, iota128)
        bmat = jnp.concatenate(b, axis=0)
        sums = [a[k] + bmat for k in range(PEER_TOPK)]
        cand = [sums[0], sums[1][:SUBLANES]]
        for k in range(2, SUBLANES):
            cand.append(jnp.where(iota8 < PEER_TOPK // (k + 1), sums[k][:SUBLANES], -jnp.inf))
        cand.append(jnp.concatenate(a[SUBLANES:], axis=0) + b[0])
        f, _ = _top16_rows(jnp.concatenate(cand, axis=0), iota_cand)
        tau = f[PEER_TOPK - 1]
        z = jnp.ones_like(f[0])
        for k in range(1, PEER_TOPK):
            z = z + jnp.exp(f[k] - f[0])
        m = jnp.zeros((PEER_N_KEYS, LANES), jnp.float32)
        r1 = jnp.full((PEER_N_KEYS, LANES), float(PEER_TOPK), jnp.float32)
        for k in range(PEER_TOPK):
            m_k = jnp.sum(jnp.where(sums[k] >= tau, 1.0, 0.0), axis=0, keepdims=True)
            m = jnp.where(iota128 == a_idx[k], m_k, m)
            r1 = jnp.where(iota128 == b_idx[k], float(k), r1)
        m_ref[h, :, pl.ds(lane0, LANES)] = m
        r1_ref[h, :, pl.ds(lane0, LANES)] = r1.astype(jnp.bfloat16)
        e0_ref[h, :, pl.ds(lane0, LANES)] = jnp.exp(s0 - a[0]) / z
        e1_ref[h, :, pl.ds(lane0, LANES)] = jnp.exp(s1 - b[0]).astype(jnp.bfloat16)

    def body(h, carry):
        for c in range(n_chunks):
            route(h, c * LANES)
        return carry

    lax.fori_loop(0, PEER_HEADS, body, 0)


def peer_router(q, keys_bf16, tt):
    t = q.shape[0]
    assert t % tt == 0 and tt % LANES == 0
    f32 = jnp.float32
    big = jax.ShapeDtypeStruct((PEER_HEADS, PEER_N_KEYS, t), f32)
    half = jax.ShapeDtypeStruct((PEER_HEADS, PEER_N_KEYS, t), jnp.bfloat16)
    blk = pl.BlockSpec((PEER_HEADS, PEER_N_KEYS, tt), lambda i: (0, 0, i))
    return pl.pallas_call(
        _router_kernel,
        grid=(t // tt,),
        in_specs=[pl.BlockSpec((tt, q.shape[1]), lambda i: (i, 0)),
                  pl.BlockSpec(keys_bf16.shape, lambda i: (0, 0, 0, 0))],
        out_specs=[blk, blk, blk, blk],
        out_shape=[big, half, big, half],
        scratch_shapes=[pltpu.VMEM((2 * PEER_HEADS, PEER_N_KEYS, tt), f32)],
        compiler_params=pltpu.CompilerParams(dimension_semantics=("parallel",),
                                             vmem_limit_bytes=VMEM_LIMIT_BYTES),
        name="peer_router",
    )(q, keys_bf16)


def _dense_kernel(xt_ref, u_ref, vt_ref, m_ref, r1_ref, e0_ref, e1_ref, res_ref, gate_ref,
                  o_ref, at_ref, wt_ref, acc_ref):
    e = pl.program_id(1)
    n_e = pl.num_programs(1)
    eb = u_ref.shape[0]
    tt = xt_ref.shape[1]
    n_chunks = tt // LANES
    n_sub = eb // PEER_SUB_EXPERTS
    i1_per_sub = PEER_SUB_EXPERTS // PEER_N_KEYS

    @pl.when(e == 0)
    def _():
        acc_ref[...] = jnp.zeros_like(acc_ref)

    at_ref[...] = jnp.dot(u_ref[...], xt_ref[...], preferred_element_type=jnp.float32)
    for sb in range(n_sub):
        for c in range(n_chunks):
            lanes = slice(c * LANES, (c + 1) * LANES)
            for k in range(i1_per_sub):
                i1 = sb * i1_per_sub + k
                zero = jnp.zeros((PEER_N_KEYS, LANES), jnp.bfloat16)
                g = zero
                for h in range(PEER_HEADS):
                    mrow = m_ref[h, i1:i1 + 1, lanes].astype(jnp.bfloat16)
                    e0row = e0_ref[h, i1:i1 + 1, lanes].astype(jnp.bfloat16)
                    g = g + jnp.where(r1_ref[h, :, lanes] < mrow, e0row * e1_ref[h, :, lanes], zero)
                r = slice(i1 * PEER_N_KEYS, (i1 + 1) * PEER_N_KEYS)
                wt_ref[r, lanes] = _gelu_tanh(at_ref[r, lanes]).astype(jnp.bfloat16) * g
    acc_ref[...] += jnp.dot(vt_ref[0], wt_ref[...], preferred_element_type=jnp.float32)

    @pl.when(e == n_e - 1)
    def _():
        o_ref[...] = res_ref[...] + gate_ref[0] * acc_ref[...].T


def peer_dense(xt_bf16, u_bf16, vt_bf16, m, r1, e0, e1, resid, gate, tokens_per_gate, tt, eb):
    d, t = xt_bf16.shape
    n_exp = u_bf16.shape[0]
    assert t % tt == 0 and n_exp % eb == 0 and tokens_per_gate % tt == 0
    assert eb // PEER_N_KEYS == SUBLANES
    tiles_per_gate = tokens_per_gate // tt
    rblk = pl.BlockSpec((PEER_HEADS, PEER_N_KEYS, tt), lambda i, j: (0, 0, i))
    gblk = pl.BlockSpec((PEER_HEADS, SUBLANES, tt), lambda i, j: (0, j, i))
    return pl.pallas_call(
        _dense_kernel,
        grid=(t // tt, n_exp // eb),
        in_specs=[pl.BlockSpec((d, tt), lambda i, j: (0, i)),
                  pl.BlockSpec((eb, d), lambda i, j: (j, 0)),
                  pl.BlockSpec((1, d, eb), lambda i, j: (j, 0, 0)),
                  gblk, rblk, gblk, rblk,
                  pl.BlockSpec((tt, d), lambda i, j: (i, 0)),
                  pl.BlockSpec((1, 1, d), lambda i, j: (i // tiles_per_gate, 0, 0))],
        out_specs=pl.BlockSpec((tt, d), lambda i, j: (i, 0)),
        out_shape=jax.ShapeDtypeStruct((t, d), jnp.float32),
        scratch_shapes=[pltpu.VMEM((eb, tt), jnp.float32),
                        pltpu.VMEM((eb, tt), jnp.bfloat16),
                        pltpu.VMEM((d, tt), jnp.float32)],
        compiler_params=pltpu.CompilerParams(dimension_semantics=("parallel", "arbitrary"),
                                             vmem_limit_bytes=VMEM_LIMIT_BYTES),
        name="peer_dense",
    )(xt_bf16, u_bf16, vt_bf16, m, r1, e0, e1, resid, gate)


def peer_layer(x, norm_g, mods, w_q, keys, u_bf16, vt_bf16):
    t, d = x.shape
    q, h = mm_norm(x, norm_g, mods[3], mods[4], w_q, emit_h=True)
    gate = mods[5]
    m, r1, e0, e1 = peer_router(q, keys.astype(jnp.bfloat16), PEER_TOKEN_TILE)
    return peer_dense(h.T, u_bf16, vt_bf16, m, r1, e0, e1, x, gate, t // gate.shape[0],
                      PEER_TOKEN_TILE, PEER_EXPERT_TILE)


def kernel(x_prompt, x_sample, c, cache_na_k, cache_na_v, cache_diff_k, cache_diff_v, state_s5_re, state_s5_im, c_ctx, mod_w, mod_b, norm_mix_g, norm_ffn_g, ev_w_in, ev_w_out, na_q_g, na_k_g, na_rpb, hy_conv_w, hy_conv_b, hy_w1, hy_b1, hy_f1, hy_w2, hy_b2, hy_f2, hy_w3, hy_decay, hy_bias, od_w_in, od_w_out, s5_lam_re, s5_lam_im, s5_log_dt, s5_b_re, s5_b_im, s5_c_re, s5_c_im, s5_d, s5_glu_w, s5_glu_b, diff_q_g, diff_k_g, diff_lq1, diff_lk1, diff_lq2, diff_lk2, diff_subln_g, peer_w_q, peer_keys, peer_u, peer_v):
    bp, lp, d = x_prompt.shape
    bs, ls, _ = x_sample.shape
    rope_p, rope_s = rope_tables(lp), rope_tables(ls)
    xp, xs = x_prompt.reshape(bp * lp, d), x_sample.reshape(bs * ls, d)
    new_na_k, new_na_v, new_dk, new_dv, new_sr, new_si = [], [], [], [], [], []
    for i in range(DEPTH):
        mods = adaln(jnp.concatenate([c_ctx[None, :], c], axis=0), mod_w[i], mod_b[i])
        mp = [v[:1] for v in mods]
        ms = [v[1:] for v in mods]
        j = i // 2
        if i % 2 == 0:
            ev = (ev_w_in[j].astype(jnp.bfloat16), ev_w_out[j].astype(jnp.bfloat16), na_q_g[j], na_k_g[j],
                  hy_conv_w[j], hy_conv_b[j],
                  hy_w1[j], hy_b1[j], hy_f1[j], hy_w2[j], hy_b2[j], hy_f2[j], hy_w3[j], hy_decay[j], hy_bias[j])
            bias_tabs = na_bias_tables(na_rpb[j], ls // GRID_W)
            xp, kp, vp = even_mixer(xp, mp, norm_mix_g[i], bp, lp, ev, None, None)
            xs, _, _ = even_mixer(xs, ms, norm_mix_g[i], bs, ls, ev, (cache_na_k[:, j], cache_na_v[:, j]), bias_tabs)
            new_na_k.append(kp)
            new_na_v.append(vp)
        else:
            lam_init = 0.8 - 0.6 * math.exp(-0.3 * i)
            s5_prep = s5_prepare(s5_lam_re[j], s5_lam_im[j], s5_log_dt[j], s5_b_re[j], s5_b_im[j],
                                 s5_c_re[j], s5_c_im[j])
            od = (od_w_in[j].astype(jnp.bfloat16), od_w_out[j].astype(jnp.bfloat16), s5_prep, s5_d[j],
                  s5_glu_w[j], s5_glu_b[j], diff_q_g[j], diff_k_g[j],
                  diff_lq1[j], diff_lk1[j], diff_lq2[j], diff_lk2[j], diff_subln_g[j])
            xp, kp, vp, sr, si = odd_mixer(xp, mp, norm_mix_g[i], bp, lp, od, lam_init, None, None, None, rope_p)
            xs, _, _, _, _ = odd_mixer(xs, ms, norm_mix_g[i], bs, ls, od, lam_init, state_s5_re[:, j],
                                       state_s5_im[:, j], (cache_diff_k[:, j], cache_diff_v[:, j]), rope_s)
            new_dk.append(kp)
            new_dv.append(vp)
            new_sr.append(sr)
            new_si.append(si)
        u_bf16 = peer_u[i].astype(jnp.bfloat16)
        vt_bf16 = jnp.swapaxes(peer_v[i].reshape(-1, PEER_EXPERT_TILE, d), 1, 2).astype(jnp.bfloat16)
        w_q_bf16 = peer_w_q[i].astype(jnp.bfloat16)
        xp = peer_layer(xp, norm_ffn_g[i], mp, w_q_bf16, peer_keys[i], u_bf16, vt_bf16)
        xs = peer_layer(xs, norm_ffn_g[i], ms, w_q_bf16, peer_keys[i], u_bf16, vt_bf16)
    return (xp.reshape(bp, lp, d), xs.reshape(bs, ls, d), jnp.stack(new_na_k, axis=1), jnp.stack(new_na_v, axis=1),
            jnp.stack(new_dk, axis=1), jnp.stack(new_dv, axis=1),
            jnp.stack(new_sr, axis=1), jnp.stack(new_si, axis=1))
```

```python
import functools
import math

import jax
import jax.numpy as jnp
from jax import lax
from jax.experimental import pallas as pl
from jax.experimental.pallas import tpu as pltpu

D_MODEL = 2048
DEPTH = 2
GRID_W = 64
MIX_WIDTH = D_MODEL // 2
HEAD_DIM = 128
NA_HEADS = MIX_WIDTH // HEAD_DIM
NA_KH = 8
NA_KW = 16
HY_WIDTH = MIX_WIDTH
HY_ORDER = 2
HY_POS_EMB = 33
S5_WIDTH = MIX_WIDTH
S5_GROUP = 16
S5_GROUPS = S5_WIDTH // S5_GROUP
S5_STATE = 64
DIFF_HEADS = MIX_WIDTH // HEAD_DIM
DIFF_QK_DIM = HEAD_DIM // 2
DIFF_V_DIM = HEAD_DIM
ROPE_BASE = 10000.0
PEER_HEADS = 8
PEER_N_KEYS = 128
PEER_QUERY_DIM = 256
PEER_TOPK = 16
NORM_EPS = 1e-6
NEG_INF = -1e30

LANES = 128
SUBLANES = 8
VMEM_LIMIT_BYTES = 56 * 1024 * 1024


MM_ROW_TILE = 1024
MM_COL_TILE = 512


def _mm_norm_kernel(x_ref, g_ref, sc_ref, sh_ref, w_ref, *out_refs, emit_h):
    if emit_h:
        o_ref, h_ref, hs_ref = out_refs
    else:
        o_ref, hs_ref = out_refs

    @pl.when(pl.program_id(1) == 0)
    def _():
        x = x_ref[...]
        y = x * lax.rsqrt(jnp.mean(x * x, axis=-1, keepdims=True) + NORM_EPS) * g_ref[...]
        hs_ref[...] = (y * (1.0 + sc_ref[0]) + sh_ref[0]).astype(jnp.bfloat16)
        if emit_h:
            h_ref[...] = hs_ref[...]

    o_ref[...] = jnp.dot(hs_ref[...], w_ref[...].astype(jnp.bfloat16), preferred_element_type=jnp.float32)


def mm_norm(x, norm_g, shift, scale, w, emit_h=False):
    m, k = x.shape
    n = w.shape[1]
    tm, tn = min(MM_ROW_TILE, m), min(MM_COL_TILE, n)
    nb = scale.shape[0]
    assert m % tm == 0 and n % tn == 0 and (m // nb) % tm == 0
    tiles_per_mod = (m // nb) // tm
    mod = pl.BlockSpec((1, 1, k), lambda i, j: (i // tiles_per_mod, 0, 0))
    out_specs = [pl.BlockSpec((tm, tn), lambda i, j: (i, j))]
    out_shape = [jax.ShapeDtypeStruct((m, n), jnp.float32)]
    if emit_h:
        out_specs.append(pl.BlockSpec((tm, k), lambda i, j: (i, 0)))
        out_shape.append(jax.ShapeDtypeStruct((m, k), jnp.bfloat16))
    res = pl.pallas_call(
        functools.partial(_mm_norm_kernel, emit_h=emit_h),
        grid=(m // tm, n // tn),
        in_specs=[pl.BlockSpec((tm, k), lambda i, j: (i, 0)),
                  pl.BlockSpec((1, k), lambda i, j: (0, 0)),
                  mod, mod,
                  pl.BlockSpec((k, tn), lambda i, j: (0, j))],
        out_specs=out_specs,
        out_shape=out_shape,
        scratch_shapes=[pltpu.VMEM((tm, k), jnp.bfloat16)],
        compiler_params=pltpu.CompilerParams(dimension_semantics=("parallel", "arbitrary"),
                                             vmem_limit_bytes=VMEM_LIMIT_BYTES),
        name="mm_norm",
    )(x, norm_g.astype(jnp.float32).reshape(1, k), scale, shift, w)
    return res if emit_h else res[0]


def _mm_pair_res_kernel(a_ref, b_ref, wa_ref, wb_ref, r_ref, g_ref, o_ref):
    y = (jnp.dot(a_ref[...].astype(jnp.bfloat16), wa_ref[...].astype(jnp.bfloat16), preferred_element_type=jnp.float32)
         + jnp.dot(b_ref[...].astype(jnp.bfloat16), wb_ref[...].astype(jnp.bfloat16), preferred_element_type=jnp.float32))
    o_ref[...] = r_ref[...] + g_ref[0] * y


def mm_pair_res(a, b, w, resid, gate):
    m, kh = a.shape
    n = w.shape[1]
    tm, tn = min(MM_ROW_TILE, m), min(MM_COL_TILE, n)
    nb = gate.shape[0]
    assert m % tm == 0 and n % tn == 0 and (m // nb) % tm == 0 and w.shape[0] == 2 * kh
    tiles_per_mod = (m // nb) // tm
    return pl.pallas_call(
        _mm_pair_res_kernel,
        grid=(m // tm, n // tn),
        in_specs=[pl.BlockSpec((tm, kh), lambda i, j: (i, 0)),
                  pl.BlockSpec((tm, kh), lambda i, j: (i, 0)),
                  pl.BlockSpec((kh, tn), lambda i, j: (0, j)),
                  pl.BlockSpec((kh, tn), lambda i, j: (1, j)),
                  pl.BlockSpec((tm, tn), lambda i, j: (i, j)),
                  pl.BlockSpec((1, 1, tn), lambda i, j: (i // tiles_per_mod, 0, j))],
        out_specs=pl.BlockSpec((tm, tn), lambda i, j: (i, j)),
        out_shape=jax.ShapeDtypeStruct((m, n), jnp.float32),
        compiler_params=pltpu.CompilerParams(dimension_semantics=("parallel", "arbitrary"),
                                             vmem_limit_bytes=VMEM_LIMIT_BYTES),
        name="mm_pair_res",
    )(a, b, w, w, resid, gate)


def adaln(cond, w, b):
    m = jax.nn.silu(cond) @ w + b
    return jnp.split(m[:, None, :], 6, axis=-1)


def hyena_filters(l, w1, b1, f1, w2, b2, f2, w3, decay):
    t = jnp.linspace(0.0, 1.0, l, dtype=jnp.float32)[:, None]
    bands = (HY_POS_EMB - 1) // 2
    w_ang = 2.0 * math.pi * jnp.arange(l, dtype=jnp.float32)[:, None] / l
    freqs = jnp.linspace(1e-4, bands - 1, bands, dtype=jnp.float32)[None, :]
    z = jnp.concatenate([t, jnp.cos(freqs * w_ang), -jnp.sin(freqs * w_ang)], axis=-1)
    h = jnp.sin(f1 * (z @ w1 + b1))
    h = jnp.sin(f2 * (h @ w2 + b2))
    h = (h @ w3).reshape(l, 2, HY_ORDER, HY_WIDTH).astype(jnp.float32)
    h = h * jnp.exp(-t.reshape(l, 1, 1, 1) * jnp.abs(decay.astype(jnp.float32)))
    h_f, h_b = h[:, 0], h[:, 1]
    zero = jnp.zeros((1, HY_ORDER, HY_WIDTH), jnp.float32)
    return jnp.concatenate([h_f, zero, h_b[1:][::-1]], axis=0)


def _split_bf16(x):
    hi = x.astype(jnp.bfloat16)
    lo = (x - hi.astype(jnp.float32)).astype(jnp.bfloat16)
    return hi, lo


def _dot3(a_hi, a_lo, x):
    m = a_hi.shape[0]
    xh, xl = _split_bf16(x)
    r = jnp.dot(jnp.concatenate([a_hi, a_lo], axis=0), xh, preferred_element_type=jnp.float32)
    return r[:m] + r[m:] + jnp.dot(a_hi, xl, preferred_element_type=jnp.float32)


HY_STEP_ROWS = 512


def hyena_factors(l):
    n = 2 * l
    n2 = 64 if n >= 8192 else 16
    return n // n2, n2


def hyena_tables(l):
    n = 2 * l
    n1, n2 = hyena_factors(l)
    t = (n2 * jnp.arange(n1 // 2)[None, None, :] + jnp.arange(n2)[:, None, None])
    k1 = jnp.arange(n1)[None, :, None]
    ang = (2.0 * math.pi / n) * ((t * k1) % n).astype(jnp.float32)
    fa = jnp.concatenate([jnp.cos(ang), -jnp.sin(ang)], axis=1)
    fc = jnp.swapaxes(fa, 1, 2) / n
    a2 = (2.0 * math.pi / n2) * ((jnp.arange(n2)[:, None] * jnp.arange(n2)[None, :]) % n2).astype(jnp.float32)
    c, s = jnp.cos(a2), jnp.sin(a2)
    fb = jnp.concatenate([jnp.concatenate([c, s], axis=1), jnp.concatenate([-s, c], axis=1)], axis=0)
    fbi = jnp.concatenate([jnp.concatenate([c, -s], axis=1), jnp.concatenate([s, c], axis=1)], axis=0)
    return tuple(_split_bf16(x) for x in (fa, fc, fb, fbi))


def hyena_spectrum(filt, l):
    n1, n2 = hyena_factors(l)
    kf = jnp.fft.fft(filt, axis=0)
    kf = jnp.stack([jnp.real(kf), jnp.imag(kf)], axis=0).astype(jnp.float32)
    kf = kf.reshape(2, n2, n1, HY_ORDER, HY_WIDTH)
    return jnp.transpose(kf, (3, 0, 2, 1, 4))


def _hy_stage_a_kernel(z_ref, fh_ref, fl_ref, o_ref):
    for i in range(z_ref.shape[0]):
        o_ref[i] = _dot3(fh_ref[0], fl_ref[0], z_ref[i])


def _hy_stage_b_kernel(s_ref, k_ref, fh_ref, fl_ref, gh_ref, gl_ref, o_ref):
    n2 = s_ref.shape[3]
    kr, ki = k_ref[0, 0], k_ref[1, 0]
    for i in range(s_ref.shape[0]):
        y = jnp.concatenate([s_ref[i, 0, 0], s_ref[i, 1, 0]], axis=0)
        z = _dot3(fh_ref[...], fl_ref[...], y)
        zr, zi = z[:n2], z[n2:]
        p = jnp.concatenate([zr * kr - zi * ki, zr * ki + zi * kr], axis=0)
        q = _dot3(gh_ref[...], gl_ref[...], p)
        o_ref[i, 0, 0] = q[:n2]
        o_ref[i, 1, 0] = q[n2:]


def _hy_stage_c_kernel(q_ref, fh_ref, fl_ref, z_ref, g_ref, b_ref, o_ref):
    for i in range(q_ref.shape[0]):
        y = _dot3(fh_ref[0], fl_ref[0], q_ref[i])
        o_ref[i] = g_ref[i] * (y + z_ref[i] * b_ref[...])


def hyena_long_conv(z, z_col, gate, gate_col, bias_o, spec_o, tables, l):
    (fah, fal), (fch, fcl), (fbh, fbl), (fgh, fgl) = tables
    b = z.shape[0]
    c = HY_WIDTH
    n1, n2 = hyena_factors(l)
    h1 = n1 // 2
    bb = max(1, min(b, HY_STEP_ROWS // n1))
    assert b % bb == 0
    cp = pltpu.CompilerParams(dimension_semantics=("parallel", "parallel"), vmem_limit_bytes=VMEM_LIMIT_BYTES)
    zblocks = z.shape[-1] // c
    gblocks = gate.shape[-1] // c
    zv = z.reshape(b, h1, n2 * z.shape[-1])
    gv = gate.reshape(b, h1, n2 * gate.shape[-1])
    s1 = pl.pallas_call(
        _hy_stage_a_kernel,
        grid=(b // bb, n2),
        in_specs=[pl.BlockSpec((bb, h1, c), lambda i, j: (i, 0, j * zblocks + z_col)),
                  pl.BlockSpec((1, 2 * n1, h1), lambda i, j: (j, 0, 0)),
                  pl.BlockSpec((1, 2 * n1, h1), lambda i, j: (j, 0, 0))],
        out_specs=pl.BlockSpec((bb, 2 * n1, c), lambda i, j: (i, 0, j)),
        out_shape=jax.ShapeDtypeStruct((b, 2 * n1, n2 * c), jnp.float32),
        compiler_params=cp, name="hyena_stage_a",
    )(zv, fah, fal)
    s1 = s1.reshape(b, 2, n1, n2, c)
    mat = pl.BlockSpec((2 * n2, 2 * n2), lambda k, i: (0, 0))
    q = pl.pallas_call(
        _hy_stage_b_kernel,
        grid=(n1, b // bb),
        in_specs=[pl.BlockSpec((bb, 2, 1, n2, c), lambda k, i: (i, 0, k, 0, 0)),
                  pl.BlockSpec((2, 1, n2, c), lambda k, i: (0, k, 0, 0)),
                  mat, mat, mat, mat],
        out_specs=pl.BlockSpec((bb, 2, 1, n2, c), lambda k, i: (i, 0, k, 0, 0)),
        out_shape=jax.ShapeDtypeStruct((b, 2, n1, n2, c), jnp.float32),
        compiler_params=cp, name="hyena_stage_b",
    )(s1, spec_o, fbh, fbl, fgh, fgl)
    q = q.reshape(b, 2 * n1, n2 * c)
    out = pl.pallas_call(
        _hy_stage_c_kernel,
        grid=(b // bb, n2),
        in_specs=[pl.BlockSpec((bb, 2 * n1, c), lambda i, j: (i, 0, j)),
                  pl.BlockSpec((1, h1, 2 * n1), lambda i, j: (j, 0, 0)),
                  pl.BlockSpec((1, h1, 2 * n1), lambda i, j: (j, 0, 0)),
                  pl.BlockSpec((bb, h1, c), lambda i, j: (i, 0, j * zblocks + z_col)),
                  pl.BlockSpec((bb, h1, c), lambda i, j: (i, 0, j * gblocks + gate_col)),
                  pl.BlockSpec((1, c), lambda i, j: (0, 0))],
        out_specs=pl.BlockSpec((bb, h1, c), lambda i, j: (i, 0, j)),
        out_shape=jax.ShapeDtypeStruct((b, h1, n2 * c), jnp.float32),
        compiler_params=cp, name="hyena_stage_c",
    )(q, fch, fcl, zv, gv, bias_o.astype(jnp.float32).reshape(1, c))
    return out.reshape(b, l, c)


def hyena(u, conv_w, conv_b, filt, bias):
    b, l, _ = u.shape
    up = jnp.pad(u, ((0, 0), (1, 1), (0, 0)))
    u = up[:, :-2] * conv_w[0] + up[:, 1:-1] * conv_w[1] + up[:, 2:] * conv_w[2] + conv_b
    tables = hyena_tables(l)
    spec = hyena_spectrum(hyena_filters(l, *filt), l)
    z, z_col = u, 0
    for o in range(HY_ORDER):
        z = hyena_long_conv(z, z_col, u, 1 + o, bias[o], spec[o], tables, l)
        z_col = 0
    return z


S5_TILE_GROUPS = 8
S5_TILES = S5_GROUPS // S5_TILE_GROUPS
S5_TILE_IN = S5_TILE_GROUPS * S5_GROUP
S5_TILE_STATE = S5_TILE_GROUPS * S5_STATE
S5_ROWS_PER_STEP = 2048


def s5_prepare(lam_re, lam_im, log_dt, b_re, b_im, c_re, c_im):
    f32 = jnp.float32
    lam_re, lam_im = lam_re.astype(f32), lam_im.astype(f32)
    dt = jnp.exp(log_dt.astype(f32))[..., None]
    mag = jnp.exp(lam_re * dt)
    ar, ai = mag * jnp.cos(lam_im * dt), mag * jnp.sin(lam_im * dt)
    den = lam_re * lam_re + lam_im * lam_im
    cr = ((ar - 1.0) * lam_re + ai * lam_im) / den
    ci = (ai * lam_re - (ar - 1.0) * lam_im) / den
    bbr = cr[..., None] * b_re - ci[..., None] * b_im
    bbi = cr[..., None] * b_im + ci[..., None] * b_re
    eye = jnp.eye(S5_TILE_GROUPS, dtype=f32)

    def tile_in(bb):
        x = bb.reshape(2, S5_TILES, S5_TILE_GROUPS, S5_STATE, S5_GROUP)
        x = jnp.einsum('dtgnp,gh->dtgphn', x, eye)
        return x.reshape(2, S5_TILES, S5_TILE_IN, S5_TILE_STATE)

    def tile_out(cc):
        x = cc.astype(f32).reshape(2, S5_TILES, S5_TILE_GROUPS, S5_GROUP, S5_STATE)
        x = jnp.einsum('dtgpn,gh->dtgnhp', x, eye)
        return x.reshape(2, S5_TILES, S5_TILE_STATE, S5_TILE_IN)

    win = jnp.concatenate([tile_in(bbr), tile_in(bbi)], axis=-1).astype(jnp.bfloat16)
    wout = jnp.concatenate([tile_out(c_re), -tile_out(c_im)], axis=-2).astype(jnp.bfloat16)
    a = jnp.stack([ar.reshape(2, S5_TILES, S5_TILE_STATE), ai.reshape(2, S5_TILES, S5_TILE_STATE)], axis=2)
    return win, wout, a


def _s5_scan_kernel(u_ref, win_ref, wout_ref, a_ref, h0_ref, y_ref, fin_ref, bu_ref, st_ref, *, batch):
    d = pl.program_id(0)
    c = pl.program_id(2)
    n_c = pl.num_programs(2)
    ns = S5_TILE_STATE
    steps = u_ref.shape[0] // batch

    @pl.when(c == 0)
    def _():
        st_ref[...] = h0_ref[0, 0]

    bu_ref[...] = jnp.dot(u_ref[...].astype(jnp.bfloat16), win_ref[0, 0], preferred_element_type=jnp.float32)
    ar = jnp.broadcast_to(a_ref[0, 0, 0:1, :], (batch, ns))
    ai = jnp.broadcast_to(a_ref[0, 0, 1:2, :], (batch, ns))

    def step(t, carry):
        hr, hi = carry
        te = jnp.where(d == 0, t, steps - 1 - t)
        r0 = pl.multiple_of(te * batch, batch)
        nr = ar * hr - ai * hi + bu_ref[pl.ds(r0, batch), 0:ns]
        ni = ar * hi + ai * hr + bu_ref[pl.ds(r0, batch), ns:2 * ns]
        bu_ref[pl.ds(r0, batch), 0:ns] = nr
        bu_ref[pl.ds(r0, batch), ns:2 * ns] = ni
        return nr, ni

    hr, hi = lax.fori_loop(0, steps, step, (st_ref[:, 0:ns], st_ref[:, ns:2 * ns]), unroll=4)
    st_ref[:, 0:ns] = hr
    st_ref[:, ns:2 * ns] = hi
    y_ref[0] = jnp.dot(bu_ref[...].astype(jnp.bfloat16), wout_ref[0, 0], preferred_element_type=jnp.float32)

    @pl.when(c == n_c - 1)
    def _():
        fin_ref[0, 0] = st_ref[...]


def s5_scan(u_tm, win, wout, a, h0, batch):
    rows = u_tm.shape[0]
    r = min(S5_ROWS_PER_STEP, rows)
    assert rows % r == 0 and r % batch == 0
    n_c = rows // r

    def chunk(d, c):
        return c + d * (n_c - 1 - 2 * c)

    return pl.pallas_call(
        functools.partial(_s5_scan_kernel, batch=batch),
        grid=(2, S5_TILES, n_c),
        in_specs=[pl.BlockSpec((r, S5_TILE_IN), lambda d, j, c: (chunk(d, c), j)),
                  pl.BlockSpec((1, 1, S5_TILE_IN, 2 * S5_TILE_STATE), lambda d, j, c: (d, j, 0, 0)),
                  pl.BlockSpec((1, 1, 2 * S5_TILE_STATE, S5_TILE_IN), lambda d, j, c: (d, j, 0, 0)),
                  pl.BlockSpec((1, 1, 2, S5_TILE_STATE), lambda d, j, c: (d, j, 0, 0)),
                  pl.BlockSpec((1, 1, batch, 2 * S5_TILE_STATE), lambda d, j, c: (d, j, 0, 0))],
        out_specs=[pl.BlockSpec((1, r, S5_TILE_IN), lambda d, j, c: (d, chunk(d, c), j)),
                   pl.BlockSpec((1, 1, batch, 2 * S5_TILE_STATE), lambda d, j, c: (d, j, 0, 0))],
        out_shape=[jax.ShapeDtypeStruct((2, rows, S5_WIDTH), jnp.float32),
                   jax.ShapeDtypeStruct((2, S5_TILES, batch, 2 * S5_TILE_STATE), jnp.float32)],
        scratch_shapes=[pltpu.VMEM((r, 2 * S5_TILE_STATE), jnp.float32),
                        pltpu.VMEM((batch, 2 * S5_TILE_STATE), jnp.float32)],
        compiler_params=pltpu.CompilerParams(dimension_semantics=("parallel", "parallel", "arbitrary"),
                                             vmem_limit_bytes=VMEM_LIMIT_BYTES),
        name="s5_scan",
    )(u_tm, win, wout, a, h0)


def _gelu_tanh(x):
    return 0.5 * x * (1.0 + jnp.tanh(math.sqrt(2.0 / math.pi) * (x + 0.044715 * (x * x * x))))


def _s5_glu_kernel(u_ref, y_ref, d_ref, w_ref, b_ref, o_ref):
    y = d_ref[...] * u_ref[...] + y_ref[0] + y_ref[1]
    y = _gelu_tanh(y)
    z = jnp.dot(y.astype(jnp.bfloat16), w_ref[...], preferred_element_type=jnp.float32) + b_ref[...]
    o_ref[...] = y * (1.0 / (1.0 + jnp.exp(-z)))


def s5_glu(u_tm, y, d_skip, glu_w_bf16, glu_b, tr=1024):
    rows, w = u_tm.shape
    tr = min(tr, rows)
    assert rows % tr == 0
    return pl.pallas_call(
        _s5_glu_kernel,
        grid=(rows // tr,),
        in_specs=[pl.BlockSpec((tr, w), lambda i: (i, 0)),
                  pl.BlockSpec((2, tr, w), lambda i: (0, i, 0)),
                  pl.BlockSpec((1, w), lambda i: (0, 0)),
                  pl.BlockSpec((w, w), lambda i: (0, 0)),
                  pl.BlockSpec((1, w), lambda i: (0, 0))],
        out_specs=pl.BlockSpec((tr, w), lambda i: (i, 0)),
        out_shape=jax.ShapeDtypeStruct((rows, w), jnp.float32),
        compiler_params=pltpu.CompilerParams(dimension_semantics=("parallel",),
                                             vmem_limit_bytes=VMEM_LIMIT_BYTES),
        name="s5_glu",
    )(u_tm, y, d_skip.reshape(1, w), glu_w_bf16, glu_b.reshape(1, w))


def s5_mixer(u, h0_re, h0_im, prep, d_skip, glu_w, glu_b):
    win, wout, a = prep
    b, l, w = u.shape
    u_tm = jnp.swapaxes(u, 0, 1).reshape(l * b, w)
    if h0_re is None:
        h0 = jnp.zeros((2, S5_TILES, b, 2 * S5_TILE_STATE), jnp.float32)
    else:
        def tiles(h):
            return jnp.transpose(h.astype(jnp.float32).reshape(b, 2, S5_TILES, S5_TILE_STATE), (1, 2, 0, 3))
        h0 = jnp.concatenate([tiles(h0_re), tiles(h0_im)], axis=-1)
    y, fin = s5_scan(u_tm, win, wout, a, h0, b)
    out_tm = s5_glu(u_tm, y, d_skip, glu_w.astype(jnp.bfloat16), glu_b)
    out = jnp.swapaxes(out_tm.reshape(l, b, w), 0, 1)

    def untile(f):
        return jnp.transpose(f, (2, 0, 1, 3)).reshape(b, 2, S5_GROUPS, S5_STATE)
    return out, untile(fin[..., :S5_TILE_STATE]), untile(fin[..., S5_TILE_STATE:])


def _half_mean_matrix():
    i = lax.broadcasted_iota(jnp.int32, (LANES, LANES), 0) // DIFF_QK_DIM
    j = lax.broadcasted_iota(jnp.int32, (LANES, LANES), 1) // DIFF_QK_DIM
    return jnp.where(i == j, 1.0 / DIFF_QK_DIM, 0.0).astype(jnp.bfloat16)


def _rms_groups(x, gain, avg):
    xx = x * x
    hi = xx.astype(jnp.bfloat16)
    lo = (xx - hi.astype(jnp.float32)).astype(jnp.bfloat16)
    ms = (jnp.dot(hi, avg, preferred_element_type=jnp.float32)
          + jnp.dot(lo, avg, preferred_element_type=jnp.float32))
    return x * lax.rsqrt(ms + NORM_EPS) * gain


def _rope_lanes(y, cos, sin_signed, first_half):
    rot = jnp.where(first_half, pltpu.roll(y, LANES - 16, 1), pltpu.roll(y, 16, 1))
    return y * cos + rot * sin_signed


def _diff_prep_kernel(q_ref, k_ref, qg_ref, kg_ref, cos_ref, sin_ref, qo_ref, ko_ref, kn_ref, *, use_rope):
    avg = _half_mean_matrix()
    lane = lax.broadcasted_iota(jnp.int32, (1, LANES), 1)
    first_half = (lane % 32) < 16
    scale = DIFF_QK_DIM ** -0.5 * math.log2(math.e)
    for h in range(DIFF_HEADS):
        cols = slice(h * LANES, (h + 1) * LANES)
        qn = _rms_groups(q_ref[:, cols], qg_ref[...], avg)
        kn = _rms_groups(k_ref[:, cols], kg_ref[...], avg)
        kn_ref[:, cols] = kn
        if use_rope:
            qn = _rope_lanes(qn, cos_ref[...], sin_ref[...], first_half)
            kn = _rope_lanes(kn, cos_ref[...], sin_ref[...], first_half)
        qo_ref[:, cols] = (qn * scale).astype(jnp.bfloat16)
        ko_ref[:, cols] = kn.astype(jnp.bfloat16)


def diff_prep(p, q_g, k_g, cos, sin_signed, seq_len, use_rope, tr=512):
    rows = p.shape[0]
    tr = min(tr, seq_len)
    assert seq_len % tr == 0
    w = DIFF_HEADS * LANES
    per_seq = seq_len // tr
    g2 = lambda g: jnp.tile(g.astype(jnp.float32), 2).reshape(1, LANES)
    return pl.pallas_call(
        functools.partial(_diff_prep_kernel, use_rope=use_rope),
        grid=(rows // tr,),
        in_specs=[pl.BlockSpec((tr, w), lambda i: (i, 1)),
                  pl.BlockSpec((tr, w), lambda i: (i, 2)),
                  pl.BlockSpec((1, LANES), lambda i: (0, 0)),
                  pl.BlockSpec((1, LANES), lambda i: (0, 0)),
                  pl.BlockSpec((tr, LANES), lambda i: (i % per_seq, 0)),
                  pl.BlockSpec((tr, LANES), lambda i: (i % per_seq, 0))],
        out_specs=[pl.BlockSpec((tr, w), lambda i: (i, 0))] * 3,
        out_shape=[jax.ShapeDtypeStruct((rows, w), jnp.bfloat16), jax.ShapeDtypeStruct((rows, w), jnp.bfloat16),
                   jax.ShapeDtypeStruct((rows, w), jnp.float32)],
        compiler_params=pltpu.CompilerParams(dimension_semantics=("parallel",),
                                             vmem_limit_bytes=VMEM_LIMIT_BYTES),
        name="diff_prep",
    )(p, p, g2(q_g), g2(k_g), cos, sin_signed)


def rope_tables(l):
    t = jnp.arange(l)
    row = (t // GRID_W).astype(jnp.float32)
    col = (t % GRID_W).astype(jnp.float32)
    nf = DIFF_QK_DIM // 4
    inv = ROPE_BASE ** (-jnp.arange(nf, dtype=jnp.float32) / nf)
    ang = jnp.stack([row[:, None] * inv, col[:, None] * inv], axis=1)
    ang = jnp.stack([ang, ang], axis=2).reshape(l, DIFF_QK_DIM)
    sign = jnp.where((jnp.arange(DIFF_QK_DIM) % 32) < 16, -1.0, 1.0)
    return jnp.tile(jnp.cos(ang), (1, 2)), jnp.tile(jnp.sin(ang) * sign, (1, 2))


def _diff_attn_kernel(*refs, has_ctx):
    if has_ctx:
        q_ref, k_ref, v_ref, ck_ref, cv_ref, lam_ref, g_ref, o_ref = refs
    else:
        q_ref, k_ref, v_ref, lam_ref, g_ref, o_ref = refs
    q = q_ref[...]
    lane = lax.broadcasted_iota(jnp.int32, (1, LANES), 1)
    zero = jnp.zeros_like(q)
    qs = (jnp.where(lane < DIFF_QK_DIM, q, zero), jnp.where(lane >= DIFF_QK_DIM, q, zero))
    nt = (((1,), (1,)), ((), ()))
    k = k_ref[...]
    ck = ck_ref[...].astype(jnp.bfloat16) if has_ctx else None
    lam = lam_ref[0:1, 0:1]
    w_self, w_ctx = None, None
    for i in range(2):
        s = lax.dot_general(qs[i], k, nt, preferred_element_type=jnp.float32)
        m = jnp.max(s, axis=-1, keepdims=True)
        if has_ctx:
            sc = lax.dot_general(qs[i], ck, nt, preferred_element_type=jnp.float32)
            m = jnp.maximum(m, jnp.max(sc, axis=-1, keepdims=True))
        p = jnp.exp2(s - m)
        l = jnp.sum(p, axis=-1, keepdims=True)
        if has_ctx:
            pc = jnp.exp2(sc - m)
            l = l + jnp.sum(pc, axis=-1, keepdims=True)
        coef = 1.0 / l if i == 0 else -lam / l
        w_self = p * coef if i == 0 else w_self + p * coef
        if has_ctx:
            w_ctx = pc * coef if i == 0 else w_ctx + pc * coef
    o = jnp.dot(w_self.astype(jnp.bfloat16), v_ref[...].astype(jnp.bfloat16), preferred_element_type=jnp.float32)
    if has_ctx:
        o = o + jnp.dot(w_ctx.astype(jnp.bfloat16), cv_ref[...].astype(jnp.bfloat16),
                        preferred_element_type=jnp.float32)
    ms = jnp.mean(o * o, axis=-1, keepdims=True)
    o_ref[...] = o * lax.rsqrt(ms + NORM_EPS) * g_ref[...]


def diff_attention(q_bf16, k_bf16, p, ctx_k, ctx_v, lam, gain, batch, seq_len, tq=256):
    tq = min(tq, seq_len)
    nq = seq_len // tq
    has_ctx = ctx_k is not None
    v_col0 = 3 * DIFF_HEADS
    in_specs = [pl.BlockSpec((tq, LANES), lambda b, h, i: (b * nq + i, h)),
                pl.BlockSpec((seq_len, LANES), lambda b, h, i: (b, h)),
                pl.BlockSpec((seq_len, LANES), lambda b, h, i: (b, v_col0 + h))]
    args = [q_bf16, k_bf16, p]
    if has_ctx:
        lc = ctx_k.shape[0] // batch
        in_specs += [pl.BlockSpec((lc, LANES), lambda b, h, i: (b, h))] * 2
        args += [ctx_k, ctx_v]
    in_specs += [pl.BlockSpec((1, LANES), lambda b, h, i: (0, 0))] * 2
    args += [jnp.broadcast_to(lam.astype(jnp.float32), (1, LANES)), gain.astype(jnp.float32).reshape(1, LANES)]
    return pl.pallas_call(
        functools.partial(_diff_attn_kernel, has_ctx=has_ctx),
        grid=(batch, DIFF_HEADS, nq),
        in_specs=in_specs,
        out_specs=pl.BlockSpec((tq, LANES), lambda b, h, i: (b * nq + i, h)),
        out_shape=jax.ShapeDtypeStruct((batch * seq_len, DIFF_HEADS * LANES), jnp.float32),
        compiler_params=pltpu.CompilerParams(dimension_semantics=("parallel", "parallel", "arbitrary"),
                                             vmem_limit_bytes=VMEM_LIMIT_BYTES),
        name="diff_attention",
    )(*args)


NA_Q_ROWS = 8
NA_WIN_ROWS = 16


def _na_prep_kernel(q_ref, k_ref, qg_ref, kg_ref, qo_ref, ko_ref, kn_ref):
    avg = jnp.full((LANES, LANES), 1.0 / HEAD_DIM, jnp.bfloat16)
    for h in range(NA_HEADS):
        cols = slice(h * LANES, (h + 1) * LANES)
        qo_ref[:, cols] = _rms_groups(q_ref[:, cols], qg_ref[...], avg).astype(jnp.bfloat16)
        kn = _rms_groups(k_ref[:, cols], kg_ref[...], avg)
        kn_ref[:, cols] = kn
        ko_ref[:, cols] = kn.astype(jnp.bfloat16)


def na_prep(p, q_g, k_g, tr=512):
    rows = p.shape[0]
    tr = min(tr, rows)
    w = NA_HEADS * LANES
    g1 = lambda g: g.astype(jnp.float32).reshape(1, LANES)
    return pl.pallas_call(
        _na_prep_kernel,
        grid=(rows // tr,),
        in_specs=[pl.BlockSpec((tr, w), lambda i: (i, 0)),
                  pl.BlockSpec((tr, w), lambda i: (i, 1)),
                  pl.BlockSpec((1, LANES), lambda i: (0, 0)),
                  pl.BlockSpec((1, LANES), lambda i: (0, 0))],
        out_specs=[pl.BlockSpec((tr, w), lambda i: (i, 0))] * 3,
        out_shape=[jax.ShapeDtypeStruct((rows, w), jnp.bfloat16), jax.ShapeDtypeStruct((rows, w), jnp.bfloat16),
                   jax.ShapeDtypeStruct((rows, w), jnp.float32)],
        compiler_params=pltpu.CompilerParams(dimension_semantics=("parallel",),
                                             vmem_limit_bytes=VMEM_LIMIT_BYTES),
        name="na_prep",
    )(p, p, g1(q_g), g1(k_g))


def na_bias_tables(rpb, rows):
    nblk = rows // NA_Q_ROWS
    cq = jnp.arange(GRID_W)[:, None]
    ck = jnp.arange(GRID_W)[None, :]
    col_start = jnp.clip(cq - NA_KW // 2, 0, GRID_W - NA_KW)
    col_ok = (ck >= col_start) & (ck < col_start + NA_KW)
    cidx = jnp.clip(ck - cq + NA_KW - 1, 0, 2 * NA_KW - 2)
    tabs = []
    for i in (0, 1, nblk - 1):
        base = min(max(NA_Q_ROWS * i - NA_KH // 2, 0), rows - NA_WIN_ROWS)
        r = NA_Q_ROWS * i + jnp.arange(NA_Q_ROWS)[:, None]
        rk = base + jnp.arange(NA_WIN_ROWS)[None, :]
        r0 = jnp.clip(r - NA_KH // 2, 0, rows - NA_KH)
        row_ok = (rk >= r0) & (rk < r0 + NA_KH)
        ridx = jnp.clip(rk - r + NA_KH - 1, 0, 2 * NA_KH - 2)
        b = rpb[:, ridx][:, :, :, cidx]
        ok = row_ok[:, :, None, None] & col_ok[None, None, :, :]
        b = jnp.where(ok[None], b.astype(jnp.float32), NEG_INF)
        tabs.append(jnp.transpose(b, (0, 1, 3, 2, 4)).reshape(NA_HEADS, NA_Q_ROWS * GRID_W, NA_WIN_ROWS * GRID_W))
    return jnp.stack(tabs, axis=0)


def _na_attn_kernel(*refs, windowed, rows):
    nt = (((1,), (1,)), ((), ()))
    scale = HEAD_DIM ** -0.5
    if windowed:
        q_ref, k_ref, v_ref, b_ref, ck_ref, cv_ref, o_ref = refs
        i = pl.program_id(2)
        base = jnp.clip(NA_Q_ROWS * i - NA_KH // 2, 0, rows - NA_WIN_ROWS)
        k0 = pl.multiple_of(base * GRID_W, GRID_W)
        nk = NA_WIN_ROWS * GRID_W
        q = q_ref[...]
        s = lax.dot_general(q, k_ref[pl.ds(k0, nk), :], nt, preferred_element_type=jnp.float32) * scale + b_ref[0, 0]
        sc = lax.dot_general(q, ck_ref[...].astype(jnp.bfloat16), nt, preferred_element_type=jnp.float32) * scale
        m = jnp.maximum(jnp.max(s, axis=-1, keepdims=True), jnp.max(sc, axis=-1, keepdims=True))
        p = jnp.exp(s - m)
        pc = jnp.exp(sc - m)
        l = jnp.sum(p, axis=-1, keepdims=True) + jnp.sum(pc, axis=-1, keepdims=True)
        o = (jnp.dot(p.astype(jnp.bfloat16), v_ref[pl.ds(k0, nk), :].astype(jnp.bfloat16),
                     preferred_element_type=jnp.float32)
             + jnp.dot(pc.astype(jnp.bfloat16), cv_ref[...].astype(jnp.bfloat16), preferred_element_type=jnp.float32))
    else:
        q_ref, k_ref, v_ref, o_ref = refs
        s = lax.dot_general(q_ref[...], k_ref[...], nt, preferred_element_type=jnp.float32) * scale
        m = jnp.max(s, axis=-1, keepdims=True)
        p = jnp.exp(s - m)
        l = jnp.sum(p, axis=-1, keepdims=True)
        o = jnp.dot(p.astype(jnp.bfloat16), v_ref[...].astype(jnp.bfloat16), preferred_element_type=jnp.float32)
    o_ref[...] = o / l


def na_attention(q_bf16, k_bf16, p, bias_tabs, ctx_k, ctx_v, batch, seq_len):
    windowed = bias_tabs is not None
    v_col0 = 2 * NA_HEADS
    rows = seq_len // GRID_W
    if windowed:
        tq = NA_Q_ROWS * GRID_W
        nq = seq_len // tq
        lc = ctx_k.shape[0] // batch
        nk = NA_WIN_ROWS * GRID_W
        in_specs = [pl.BlockSpec((tq, LANES), lambda b, h, i: (b * nq + i, h)),
                    pl.BlockSpec((seq_len, LANES), lambda b, h, i: (b, h)),
                    pl.BlockSpec((seq_len, LANES), lambda b, h, i: (b, v_col0 + h)),
                    pl.BlockSpec((1, 1, tq, nk),
                                 lambda b, h, i: (jnp.where(i == 0, 0, jnp.where(i == nq - 1, 2, 1)), h, 0, 0)),
                    pl.BlockSpec((lc, LANES), lambda b, h, i: (b, h)),
                    pl.BlockSpec((lc, LANES), lambda b, h, i: (b, h))]
        args = [q_bf16, k_bf16, p, bias_tabs, ctx_k, ctx_v]
    else:
        tq = seq_len
        nq = 1
        in_specs = [pl.BlockSpec((tq, LANES), lambda b, h, i: (b, h)),
                    pl.BlockSpec((seq_len, LANES), lambda b, h, i: (b, h)),
                    pl.BlockSpec((seq_len, LANES), lambda b, h, i: (b, v_col0 + h))]
        args = [q_bf16, k_bf16, p]
    return pl.pallas_call(
        functools.partial(_na_attn_kernel, windowed=windowed, rows=rows),
        grid=(batch, NA_HEADS, nq),
        in_specs=in_specs,
        out_specs=pl.BlockSpec((tq, LANES), lambda b, h, i: (b * nq + i, h)),
        out_shape=jax.ShapeDtypeStruct((batch * seq_len, NA_HEADS * LANES), jnp.float32),
        compiler_params=pltpu.CompilerParams(dimension_semantics=("parallel", "parallel", "arbitrary"),
                                             vmem_limit_bytes=VMEM_LIMIT_BYTES),
        name="na_attention",
    )(*args)


def even_mixer(x, mods, norm_g, b, l, ev, ctx_kv, bias_tabs):
    (w_in, w_out, q_g, k_g, conv_w, conv_b, w1, b1, f1, w2, b2, f2, w3, decay, bias) = ev
    p = mm_norm(x, norm_g, mods[0], mods[1], w_in)
    q, k, kn = na_prep(p, q_g, k_g)
    if ctx_kv is None:
        oa = na_attention(q, k, p, None, None, None, b, l)
    else:
        lc = ctx_kv[0].shape[1]
        oa = na_attention(q, k, p, bias_tabs, ctx_kv[0].reshape(b * lc, MIX_WIDTH),
                          ctx_kv[1].reshape(b * lc, MIX_WIDTH), b, l)
    hb = p[:, 3 * MIX_WIDTH:].reshape(b, l, 3 * HY_WIDTH)
    ob = hyena(hb, conv_w, conv_b, (w1, b1, f1, w2, b2, f2, w3, decay), bias)
    out = mm_pair_res(oa, ob.reshape(b * l, HY_WIDTH), w_out, x, mods[2])
    shp = (b, l, NA_HEADS, HEAD_DIM)
    return out, kn.reshape(shp), p[:, 2 * MIX_WIDTH:3 * MIX_WIDTH].reshape(shp)


def odd_mixer(x, mods, norm_g, b, l, od, lam_init, h0_re, h0_im, ctx_kv, rope):
    (w_in, w_out, s5_prep, d_skip, glu_w, glu_b, q_g, k_g, lq1, lk1, lq2, lk2, subln_g) = od
    p = mm_norm(x, norm_g, mods[0], mods[1], w_in)
    oc, fr, fi = s5_mixer(p[:, :S5_WIDTH].reshape(b, l, S5_WIDTH), h0_re, h0_im, s5_prep, d_skip, glu_w, glu_b)
    lam = (jnp.exp(jnp.sum(lq1 * lk1).astype(jnp.float32))
           - jnp.exp(jnp.sum(lq2 * lk2).astype(jnp.float32)) + lam_init)
    cos, sin_signed = rope
    q, k, kn = diff_prep(p, q_g, k_g, cos, sin_signed, l, use_rope=ctx_kv is not None)
    gain = subln_g.astype(jnp.float32) * (1.0 - lam_init)
    if ctx_kv is None:
        o = diff_attention(q, k, p, None, None, lam, gain, b, l)
    else:
        lc = ctx_kv[0].shape[1]
        o = diff_attention(q, k, p, ctx_kv[0].reshape(b * lc, MIX_WIDTH), ctx_kv[1].reshape(b * lc, MIX_WIDTH),
                           lam, gain, b, l)
    out = mm_pair_res(oc.reshape(b * l, S5_WIDTH), o, w_out, x, mods[2])
    kn = kn.reshape(b, l, DIFF_HEADS, 2, DIFF_QK_DIM)
    v = p[:, S5_WIDTH + 2 * MIX_WIDTH:].reshape(b, l, DIFF_HEADS, DIFF_V_DIM)
    return out, kn, v, fr, fi


PEER_TOKEN_TILE = 512
PEER_EXPERT_TILE = 1024
PEER_SUB_EXPERTS = 256


def _top16_rows(cur, iota):
    n = cur.shape[0]
    tops, idxs = [], []
    for _ in range(PEER_TOPK):
        m = jnp.max(cur, axis=0, keepdims=True)
        first = jnp.min(jnp.where(cur == m, iota, n), axis=0, keepdims=True)
        tops.append(m)
        idxs.append(first)
        cur = jnp.where(iota == first, -jnp.inf, cur)
    return tops, idxs


def _router_kernel(q_ref, keys_ref, m_ref, r1_ref, e0_ref, e1_ref, st_ref):
    tt = q_ref.shape[0]
    half = PEER_QUERY_DIM // 2
    for h in range(PEER_HEADS):
        for s in range(2):
            qs = q_ref[:, (2 * h + s) * half:(2 * h + s + 1) * half].astype(jnp.bfloat16)
            st = lax.dot_general(keys_ref[h, s], qs, (((1,), (1,)), ((), ())),
                                 preferred_element_type=jnp.float32)
            st_ref[2 * h + s] = st

    iota128 = lax.broadcasted_iota(jnp.int32, (PEER_N_KEYS, LANES), 0)
    iota8 = lax.broadcasted_iota(jnp.int32, (SUBLANES, LANES), 0)
    iota_cand = lax.broadcasted_iota(jnp.int32, (PEER_TOPK + SUBLANES * SUBLANES, LANES), 0)
    n_chunks = tt // LANES

    def route(h, lane0):
        s0 = st_ref[2 * h, :, pl.ds(lane0, LANES)]
        s1 = st_ref[2 * h + 1, :, pl.ds(lane0, LANES)]
        a, a_idx = _top16_rows(s0, iota128)
        b, b_idx = _top16_rows(s1, iota128)
        bmat = jnp.concatenate(b, axis=0)
        sums = [a[k] + bmat for k in range(PEER_TOPK)]
        cand = [sums[0], sums[1][:SUBLANES]]
        for k in range(2, SUBLANES):
            cand.append(jnp.where(iota8 < PEER_TOPK // (k + 1), sums[k][:SUBLANES], -jnp.inf))
        cand.append(jnp.concatenate(a[SUBLANES:], axis=0) + b[0])
        f, _ = _top16_rows(jnp.concatenate(cand, axis=0), iota_cand)
        tau = f[PEER_TOPK - 1]
        z = jnp.ones_like(f[0])
        for k in range(1, PEER_TOPK):
            z = z + jnp.exp(f[k] - f[0])
        m = jnp.zeros((PEER_N_KEYS, LANES), jnp.float32)
        r1 = jnp.full((PEER_N_KEYS, LANES), float(PEER_TOPK), jnp.float32)
        for k in range(PEER_TOPK):
            m_k = jnp.sum(jnp.where(sums[k] >= tau, 1.0, 0.0), axis=0, keepdims=True)
            m = jnp.where(iota128 == a_idx[k], m_k, m)
            r1 = jnp.where(iota128 == b_idx[k], float(k), r1)
        m_ref[h, :, pl.ds(lane0, LANES)] = m
        r1_ref[h, :, pl.ds(lane0, LANES)] = r1.astype(jnp.bfloat16)
        e0_ref[h, :, pl.ds(lane0, LANES)] = jnp.exp(s0 - a[0]) / z
        e1_ref[h, :, pl.ds(lane0, LANES)] = jnp.exp(s1 - b[0]).astype(jnp.bfloat16)

    def body(h, carry):
        for c in range(n_chunks):
            route(h, c * LANES)
        return carry

    lax.fori_loop(0, PEER_HEADS, body, 0)


def peer_router(q, keys_bf16, tt):
    t = q.shape[0]
    assert t % tt == 0 and tt % LANES == 0
    f32 = jnp.float32
    big = jax.ShapeDtypeStruct((PEER_HEADS, PEER_N_KEYS, t), f32)
    half = jax.ShapeDtypeStruct((PEER_HEADS, PEER_N_KEYS, t), jnp.bfloat16)
    blk = pl.BlockSpec((PEER_HEADS, PEER_N_KEYS, tt), lambda i: (0, 0, i))
    return pl.pallas_call(
        _router_kernel,
        grid=(t // tt,),
        in_specs=[pl.BlockSpec((tt, q.shape[1]), lambda i: (i, 0)),
                  pl.BlockSpec(keys_bf16.shape, lambda i: (0, 0, 0, 0))],
        out_specs=[blk, blk, blk, blk],
        out_shape=[big, half, big, half],
        scratch_shapes=[pltpu.VMEM((2 * PEER_HEADS, PEER_N_KEYS, tt), f32)],
        compiler_params=pltpu.CompilerParams(dimension_semantics=("parallel",),
                                             vmem_limit_bytes=VMEM_LIMIT_BYTES),
        name="peer_router",
    )(q, keys_bf16)


def _dense_kernel(xt_ref, u_ref, vt_ref, m_ref, r1_ref, e0_ref, e1_ref, res_ref, gate_ref,
                  o_ref, at_ref, wt_ref, acc_ref):
    e = pl.program_id(1)
    n_e = pl.num_programs(1)
    eb = u_ref.shape[0]
    tt = xt_ref.shape[1]
    n_chunks = tt // LANES
    n_sub = eb // PEER_SUB_EXPERTS
    i1_per_sub = PEER_SUB_EXPERTS // PEER_N_KEYS

    @pl.when(e == 0)
    def _():
        acc_ref[...] = jnp.zeros_like(acc_ref)

    at_ref[...] = jnp.dot(u_ref[...], xt_ref[...], preferred_element_type=jnp.float32)
    for sb in range(n_sub):
        for c in range(n_chunks):
            lanes = slice(c * LANES, (c + 1) * LANES)
            for k in range(i1_per_sub):
                i1 = sb * i1_per_sub + k
                zero = jnp.zeros((PEER_N_KEYS, LANES), jnp.bfloat16)
                g = zero
                for h in range(PEER_HEADS):
                    mrow = m_ref[h, i1:i1 + 1, lanes].astype(jnp.bfloat16)
                    e0row = e0_ref[h, i1:i1 + 1, lanes].astype(jnp.bfloat16)
                    g = g + jnp.where(r1_ref[h, :, lanes] < mrow, e0row * e1_ref[h, :, lanes], zero)
                r = slice(i1 * PEER_N_KEYS, (i1 + 1) * PEER_N_KEYS)
                wt_ref[r, lanes] = _gelu_tanh(at_ref[r, lanes]).astype(jnp.bfloat16) * g
    acc_ref[...] += jnp.dot(vt_ref[0], wt_ref[...], preferred_element_type=jnp.float32)

    @pl.when(e == n_e - 1)
    def _():
        o_ref[...] = res_ref[...] + gate_ref[0] * acc_ref[...].T


def peer_dense(xt_bf16, u_bf16, vt_bf16, m, r1, e0, e1, resid, gate, tokens_per_gate, tt, eb):
    d, t = xt_bf16.shape
    n_exp = u_bf16.shape[0]
    assert t % tt == 0 and n_exp % eb == 0 and tokens_per_gate % tt == 0
    assert eb // PEER_N_KEYS == SUBLANES
    tiles_per_gate = tokens_per_gate // tt
    rblk = pl.BlockSpec((PEER_HEADS, PEER_N_KEYS, tt), lambda i, j: (0, 0, i))
    gblk = pl.BlockSpec((PEER_HEADS, SUBLANES, tt), lambda i, j: (0, j, i))
    return pl.pallas_call(
        _dense_kernel,
        grid=(t // tt, n_exp // eb),
        in_specs=[pl.BlockSpec((d, tt), lambda i, j: (0, i)),
                  pl.BlockSpec((eb, d), lambda i, j: (j, 0)),
                  pl.BlockSpec((1, d, eb), lambda i, j: (j, 0, 0)),
                  gblk, rblk, gblk, rblk,
                  pl.BlockSpec((tt, d), lambda i, j: (i, 0)),
                  pl.BlockSpec((1, 1, d), lambda i, j: (i // tiles_per_gate, 0, 0))],
        out_specs=pl.BlockSpec((tt, d), lambda i, j: (i, 0)),
        out_shape=jax.ShapeDtypeStruct((t, d), jnp.float32),
        scratch_shapes=[pltpu.VMEM((eb, tt), jnp.float32),
                        pltpu.VMEM((eb, tt), jnp.bfloat16),
                        pltpu.VMEM((d, tt), jnp.float32)],
        compiler_params=pltpu.CompilerParams(dimension_semantics=("parallel", "arbitrary"),
                                             vmem_limit_bytes=VMEM_LIMIT_BYTES),
        name="peer_dense",
    )(xt_bf16, u_bf16, vt_bf16, m, r1, e0, e1, resid, gate)


def peer_layer(x, norm_g, mods, w_q, keys, u_bf16, vt_bf16):
    t, d = x.shape
    q, h = mm_norm(x, norm_g, mods[3], mods[4], w_q, emit_h=True)
    gate = mods[5]
    m, r1, e0, e1 = peer_router(q, keys.astype(jnp.bfloat16), PEER_TOKEN_TILE)
    return peer_dense(h.T, u_bf16, vt_bf16, m, r1, e0, e1, x, gate, t // gate.shape[0],
                      PEER_TOKEN_TILE, PEER_EXPERT_TILE)


def kernel(x_prompt, x_sample, c, cache_na_k, cache_na_v, cache_diff_k, cache_diff_v, state_s5_re, state_s5_im, c_ctx, mod_w, mod_b, norm_mix_g, norm_ffn_g, ev_w_in, ev_w_out, na_q_g, na_k_g, na_rpb, hy_conv_w, hy_conv_b, hy_w1, hy_b1, hy_f1, hy_w2, hy_b2, hy_f2, hy_w3, hy_decay, hy_bias, od_w_in, od_w_out, s5_lam_re, s5_lam_im, s5_log_dt, s5_b_re, s5_b_im, s5_c_re, s5_c_im, s5_d, s5_glu_w, s5_glu_b, diff_q_g, diff_k_g, diff_lq1, diff_lk1, diff_lq2, diff_lk2, diff_subln_g, peer_w_q, peer_keys, peer_u, peer_v):
    bp, lp, d = x_prompt.shape
    bs, ls, _ = x_sample.shape
    rope_p, rope_s = rope_tables(lp), rope_tables(ls)
    xp, xs = x_prompt.reshape(bp * lp, d), x_sample.reshape(bs * ls, d)
    new_na_k, new_na_v, new_dk, new_dv, new_sr, new_si = [], [], [], [], [], []
    for i in range(DEPTH):
        mods = adaln(jnp.concatenate([c_ctx[None, :], c], axis=0), mod_w[i], mod_b[i])
        mp = [v[:1] for v in mods]
        ms = [v[1:] for v in mods]
        j = i // 2
        if i % 2 == 0:
            ev = (ev_w_in[j].astype(jnp.bfloat16), ev_w_out[j].astype(jnp.bfloat16), na_q_g[j], na_k_g[j],
                  hy_conv_w[j], hy_conv_b[j],
                  hy_w1[j], hy_b1[j], hy_f1[j], hy_w2[j], hy_b2[j], hy_f2[j], hy_w3[j], hy_decay[j], hy_bias[j])
            bias_tabs = na_bias_tables(na_rpb[j], ls // GRID_W)
            xp, kp, vp = even_mixer(xp, mp, norm_mix_g[i], bp, lp, ev, None, None)
            xs, _, _ = even_mixer(xs, ms, norm_mix_g[i], bs, ls, ev, (cache_na_k[:, j], cache_na_v[:, j]), bias_tabs)
            new_na_k.append(kp)
            new_na_v.append(vp)
        else:
            lam_init = 0.8 - 0.6 * math.exp(-0.3 * i)
            s5_prep = s5_prepare(s5_lam_re[j], s5_lam_im[j], s5_log_dt[j], s5_b_re[j], s5_b_im[j],
                                 s5_c_re[j], s5_c_im[j])
            od = (od_w_in[j].astype(jnp.bfloat16), od_w_out[j].astype(jnp.bfloat16), s5_prep, s5_d[j],
                  s5_glu_w[j], s5_glu_b[j], diff_q_g[j], diff_k_g[j],
                  diff_lq1[j], diff_lk1[j], diff_lq2[j], diff_lk2[j], diff_subln_g[j])
            xp, kp, vp, sr, si = odd_mixer(xp, mp, norm_mix_g[i], bp, lp, od, lam_init, None, None, None, rope_p)
            xs, _, _, _, _ = odd_mixer(xs, ms, norm_mix_g[i], bs, ls, od, lam_init, state_s5_re[:, j],
                                       state_s5_im[:, j], (cache_diff_k[:, j], cache_diff_v[:, j]), rope_s)
            new_dk.append(kp)
            new_dv.append(vp)
            new_sr.append(sr)
            new_si.append(si)
        u_bf16 = peer_u[i].astype(jnp.bfloat16)
        vt_bf16 = jnp.swapaxes(peer_v[i].reshape(-1, PEER_EXPERT_TILE, d), 1, 2).astype(jnp.bfloat16)
        w_q_bf16 = peer_w_q[i].astype(jnp.bfloat16)
        xp = peer_layer(xp, norm_ffn_g[i], mp, w_q_bf16, peer_keys[i], u_bf16, vt_bf16)
        xs = peer_layer(xs, norm_ffn_g[i], ms, w_q_bf16, peer_keys[i], u_bf16, vt_bf16)
    return (xp.reshape(bp, lp, d), xs.reshape(bs, ls, d), jnp.stack(new_na_k, axis=1), jnp.stack(new_na_v, axis=1),
            jnp.stack(new_dk, axis=1), jnp.stack(new_dv, axis=1),
            jnp.stack(new_sr, axis=1), jnp.stack(new_si, axis=1))
```

```python
import functools
import math

import jax
import jax.numpy as jnp
from jax import lax
from jax.experimental import pallas as pl
from jax.experimental.pallas import tpu as pltpu

D_MODEL = 2048
DEPTH = 2
GRID_W = 64
MIX_WIDTH = D_MODEL // 2
HEAD_DIM = 128
NA_HEADS = MIX_WIDTH // HEAD_DIM
NA_KH = 8
NA_KW = 16
HY_WIDTH = MIX_WIDTH
HY_ORDER = 2
HY_POS_EMB = 33
S5_WIDTH = MIX_WIDTH
S5_GROUP = 16
S5_GROUPS = S5_WIDTH // S5_GROUP
S5_STATE = 64
DIFF_HEADS = MIX_WIDTH // HEAD_DIM
DIFF_QK_DIM = HEAD_DIM // 2
DIFF_V_DIM = HEAD_DIM
ROPE_BASE = 10000.0
PEER_HEADS = 8
PEER_N_KEYS = 128
PEER_QUERY_DIM = 256
PEER_TOPK = 16
NORM_EPS = 1e-6
NEG_INF = -1e30

LANES = 128
SUBLANES = 8
VMEM_LIMIT_BYTES = 56 * 1024 * 1024


MM_ROW_TILE = 1024
MM_COL_TILE = 512


def _mm_norm_kernel(x_ref, g_ref, sc_ref, sh_ref, w_ref, *out_refs, emit_h):
    if emit_h:
        o_ref, h_ref, hs_ref = out_refs
    else:
        o_ref, hs_ref = out_refs

    @pl.when(pl.program_id(1) == 0)
    def _():
        x = x_ref[...]
        y = x * lax.rsqrt(jnp.mean(x * x, axis=-1, keepdims=True) + NORM_EPS) * g_ref[...]
        hs_ref[...] = (y * (1.0 + sc_ref[0]) + sh_ref[0]).astype(jnp.bfloat16)
        if emit_h:
            h_ref[...] = hs_ref[...]

    o_ref[...] = jnp.dot(hs_ref[...], w_ref[...].astype(jnp.bfloat16), preferred_element_type=jnp.float32)


def mm_norm(x, norm_g, shift, scale, w, emit_h=False):
    m, k = x.shape
    n = w.shape[1]
    tm, tn = min(MM_ROW_TILE, m), min(MM_COL_TILE, n)
    nb = scale.shape[0]
    assert m % tm == 0 and n % tn == 0 and (m // nb) % tm == 0
    tiles_per_mod = (m // nb) // tm
    mod = pl.BlockSpec((1, 1, k), lambda i, j: (i // tiles_per_mod, 0, 0))
    out_specs = [pl.BlockSpec((tm, tn), lambda i, j: (i, j))]
    out_shape = [jax.ShapeDtypeStruct((m, n), jnp.float32)]
    if emit_h:
        out_specs.append(pl.BlockSpec((tm, k), lambda i, j: (i, 0)))
        out_shape.append(jax.ShapeDtypeStruct((m, k), jnp.bfloat16))
    res = pl.pallas_call(
        functools.partial(_mm_norm_kernel, emit_h=emit_h),
        grid=(m // tm, n // tn),
        in_specs=[pl.BlockSpec((tm, k), lambda i, j: (i, 0)),
                  pl.BlockSpec((1, k), lambda i, j: (0, 0)),
                  mod, mod,
                  pl.BlockSpec((k, tn), lambda i, j: (0, j))],
        out_specs=out_specs,
        out_shape=out_shape,
        scratch_shapes=[pltpu.VMEM((tm, k), jnp.bfloat16)],
        compiler_params=pltpu.CompilerParams(dimension_semantics=("parallel", "arbitrary"),
                                             vmem_limit_bytes=VMEM_LIMIT_BYTES),
        name="mm_norm",
    )(x, norm_g.astype(jnp.float32).reshape(1, k), scale, shift, w)
    return res if emit_h else res[0]


def _mm_pair_res_kernel(a_ref, b_ref, wa_ref, wb_ref, r_ref, g_ref, o_ref):
    y = (jnp.dot(a_ref[...].astype(jnp.bfloat16), wa_ref[...].astype(jnp.bfloat16), preferred_element_type=jnp.float32)
         + jnp.dot(b_ref[...].astype(jnp.bfloat16), wb_ref[...].astype(jnp.bfloat16), preferred_element_type=jnp.float32))
    o_ref[...] = r_ref[...] + g_ref[0] * y


def mm_pair_res(a, b, w, resid, gate):
    m, kh = a.shape
    n = w.shape[1]
    tm, tn = min(MM_ROW_TILE, m), min(MM_COL_TILE, n)
    nb = gate.shape[0]
    assert m % tm == 0 and n % tn == 0 and (m // nb) % tm == 0 and w.shape[0] == 2 * kh
    tiles_per_mod = (m // nb) // tm
    return pl.pallas_call(
        _mm_pair_res_kernel,
        grid=(m // tm, n // tn),
        in_specs=[pl.BlockSpec((tm, kh), lambda i, j: (i, 0)),
                  pl.BlockSpec((tm, kh), lambda i, j: (i, 0)),
                  pl.BlockSpec((kh, tn), lambda i, j: (0, j)),
                  pl.BlockSpec((kh, tn), lambda i, j: (1, j)),
                  pl.BlockSpec((tm, tn), lambda i, j: (i, j)),
                  pl.BlockSpec((1, 1, tn), lambda i, j: (i // tiles_per_mod, 0, j))],
        out_specs=pl.BlockSpec((tm, tn), lambda i, j: (i, j)),
        out_shape=jax.ShapeDtypeStruct((m, n), jnp.float32),
        compiler_params=pltpu.CompilerParams(dimension_semantics=("parallel", "arbitrary"),
                                             vmem_limit_bytes=VMEM_LIMIT_BYTES),
        name="mm_pair_res",
    )(a, b, w, w, resid, gate)


def adaln(cond, w, b):
    m = jax.nn.silu(cond) @ w + b
    return jnp.split(m[:, None, :], 6, axis=-1)


def hyena_filters(l, w1, b1, f1, w2, b2, f2, w3, decay):
    t = jnp.linspace(0.0, 1.0, l, dtype=jnp.float32)[:, None]
    bands = (HY_POS_EMB - 1) // 2
    w_ang = 2.0 * math.pi * jnp.arange(l, dtype=jnp.float32)[:, None] / l
    freqs = jnp.linspace(1e-4, bands - 1, bands, dtype=jnp.float32)[None, :]
    z = jnp.concatenate([t, jnp.cos(freqs * w_ang), -jnp.sin(freqs * w_ang)], axis=-1)
    h = jnp.sin(f1 * (z @ w1 + b1))
    h = jnp.sin(f2 * (h @ w2 + b2))
    h = (h @ w3).reshape(l, 2, HY_ORDER, HY_WIDTH).astype(jnp.float32)
    h = h * jnp.exp(-t.reshape(l, 1, 1, 1) * jnp.abs(decay.astype(jnp.float32)))
    h_f, h_b = h[:, 0], h[:, 1]
    zero = jnp.zeros((1, HY_ORDER, HY_WIDTH), jnp.float32)
    return jnp.concatenate([h_f, zero, h_b[1:][::-1]], axis=0)


def _split_bf16(x):
    hi = x.astype(jnp.bfloat16)
    lo = (x - hi.astype(jnp.float32)).astype(jnp.bfloat16)
    return hi, lo


def _dot3(a_hi, a_lo, x):
    m = a_hi.shape[0]
    xh, xl = _split_bf16(x)
    r = jnp.dot(jnp.concatenate([a_hi, a_lo], axis=0), xh, preferred_element_type=jnp.float32)
    return r[:m] + r[m:] + jnp.dot(a_hi, xl, preferred_element_type=jnp.float32)


HY_STEP_ROWS = 1024


def hyena_factors(l):
    n = 2 * l
    n2 = 64 if n >= 8192 else 16
    return n // n2, n2


def hyena_tables(l):
    n = 2 * l
    n1, n2 = hyena_factors(l)
    t = (n2 * jnp.arange(n1 // 2)[None, None, :] + jnp.arange(n2)[:, None, None])
    k1 = jnp.arange(n1)[None, :, None]
    ang = (2.0 * math.pi / n) * ((t * k1) % n).astype(jnp.float32)
    fa = jnp.concatenate([jnp.cos(ang), -jnp.sin(ang)], axis=1)
    fc = jnp.swapaxes(fa, 1, 2) / n
    a2 = (2.0 * math.pi / n2) * ((jnp.arange(n2)[:, None] * jnp.arange(n2)[None, :]) % n2).astype(jnp.float32)
    c, s = jnp.cos(a2), jnp.sin(a2)
    fb = jnp.concatenate([jnp.concatenate([c, s], axis=1), jnp.concatenate([-s, c], axis=1)], axis=0)
    fbi = jnp.concatenate([jnp.concatenate([c, -s], axis=1), jnp.concatenate([s, c], axis=1)], axis=0)
    return tuple(_split_bf16(x) for x in (fa, fc, fb, fbi))


def hyena_spectrum(filt, l):
    n1, n2 = hyena_factors(l)
    kf = jnp.fft.fft(filt, axis=0)
    kf = jnp.stack([jnp.real(kf), jnp.imag(kf)], axis=0).astype(jnp.float32)
    kf = kf.reshape(2, n2, n1, HY_ORDER, HY_WIDTH)
    return jnp.transpose(kf, (3, 0, 2, 1, 4))


def _hy_stage_a_kernel(z_ref, fh_ref, fl_ref, o_ref):
    for i in range(z_ref.shape[0]):
        o_ref[i] = _dot3(fh_ref[0], fl_ref[0], z_ref[i])


def _hy_stage_b_kernel(s_ref, k_ref, fh_ref, fl_ref, gh_ref, gl_ref, o_ref):
    n2 = s_ref.shape[3]
    kr, ki = k_ref[0, 0], k_ref[1, 0]
    for i in range(s_ref.shape[0]):
        y = jnp.concatenate([s_ref[i, 0, 0], s_ref[i, 1, 0]], axis=0)
        z = _dot3(fh_ref[...], fl_ref[...], y)
        zr, zi = z[:n2], z[n2:]
        p = jnp.concatenate([zr * kr - zi * ki, zr * ki + zi * kr], axis=0)
        q = _dot3(gh_ref[...], gl_ref[...], p)
        o_ref[i, 0, 0] = q[:n2]
        o_ref[i, 1, 0] = q[n2:]


def _hy_stage_c_kernel(q_ref, fh_ref, fl_ref, z_ref, g_ref, b_ref, o_ref):
    for i in range(q_ref.shape[0]):
        y = _dot3(fh_ref[0], fl_ref[0], q_ref[i])
        o_ref[i] = g_ref[i] * (y + z_ref[i] * b_ref[...])


def hyena_long_conv(z, z_col, gate, gate_col, bias_o, spec_o, tables, l):
    (fah, fal), (fch, fcl), (fbh, fbl), (fgh, fgl) = tables
    b = z.shape[0]
    c = HY_WIDTH
    n1, n2 = hyena_factors(l)
    h1 = n1 // 2
    bb = max(1, min(b, HY_STEP_ROWS // n1))
    assert b % bb == 0
    cp = pltpu.CompilerParams(dimension_semantics=("parallel", "parallel"), vmem_limit_bytes=VMEM_LIMIT_BYTES)
    zblocks = z.shape[-1] // c
    gblocks = gate.shape[-1] // c
    zv = z.reshape(b, h1, n2 * z.shape[-1])
    gv = gate.reshape(b, h1, n2 * gate.shape[-1])
    s1 = pl.pallas_call(
        _hy_stage_a_kernel,
        grid=(b // bb, n2),
        in_specs=[pl.BlockSpec((bb, h1, c), lambda i, j: (i, 0, j * zblocks + z_col)),
                  pl.BlockSpec((1, 2 * n1, h1), lambda i, j: (j, 0, 0)),
                  pl.BlockSpec((1, 2 * n1, h1), lambda i, j: (j, 0, 0))],
        out_specs=pl.BlockSpec((bb, 2 * n1, c), lambda i, j: (i, 0, j)),
        out_shape=jax.ShapeDtypeStruct((b, 2 * n1, n2 * c), jnp.float32),
        compiler_params=cp, name="hyena_stage_a",
    )(zv, fah, fal)
    s1 = s1.reshape(b, 2, n1, n2, c)
    mat = pl.BlockSpec((2 * n2, 2 * n2), lambda k, i: (0, 0))
    q = pl.pallas_call(
        _hy_stage_b_kernel,
        grid=(n1, b // bb),
        in_specs=[pl.BlockSpec((bb, 2, 1, n2, c), lambda k, i: (i, 0, k, 0, 0)),
                  pl.BlockSpec((2, 1, n2, c), lambda k, i: (0, k, 0, 0)),
                  mat, mat, mat, mat],
        out_specs=pl.BlockSpec((bb, 2, 1, n2, c), lambda k, i: (i, 0, k, 0, 0)),
        out_shape=jax.ShapeDtypeStruct((b, 2, n1, n2, c), jnp.float32),
        compiler_params=cp, name="hyena_stage_b",
    )(s1, spec_o, fbh, fbl, fgh, fgl)
    q = q.reshape(b, 2 * n1, n2 * c)
    out = pl.pallas_call(
        _hy_stage_c_kernel,
        grid=(b // bb, n2),
        in_specs=[pl.BlockSpec((bb, 2 * n1, c), lambda i, j: (i, 0, j)),
                  pl.BlockSpec((1, h1, 2 * n1), lambda i, j: (j, 0, 0)),
                  pl.BlockSpec((1, h1, 2 * n1), lambda i, j: (j, 0, 0)),
                  pl.BlockSpec((bb, h1, c), lambda i, j: (i, 0, j * zblocks + z_col)),
                  pl.BlockSpec((bb, h1, c), lambda i, j: (i, 0, j * gblocks + gate_col)),
                  pl.BlockSpec((1, c), lambda i, j: (0, 0))],
        out_specs=pl.BlockSpec((bb, h1, c), lambda i, j: (i, 0, j)),
        out_shape=jax.ShapeDtypeStruct((b, h1, n2 * c), jnp.float32),
        compiler_params=cp, name="hyena_stage_c",
    )(q, fch, fcl, zv, gv, bias_o.astype(jnp.float32).reshape(1, c))
    return out.reshape(b, l, c)


def hyena(u, conv_w, conv_b, filt, bias):
    b, l, _ = u.shape
    up = jnp.pad(u, ((0, 0), (1, 1), (0, 0)))
    u = up[:, :-2] * conv_w[0] + up[:, 1:-1] * conv_w[1] + up[:, 2:] * conv_w[2] + conv_b
    tables = hyena_tables(l)
    spec = hyena_spectrum(hyena_filters(l, *filt), l)
    z, z_col = u, 0
    for o in range(HY_ORDER):
        z = hyena_long_conv(z, z_col, u, 1 + o, bias[o], spec[o], tables, l)
        z_col = 0
    return z


S5_TILE_GROUPS = 8
S5_TILES = S5_GROUPS // S5_TILE_GROUPS
S5_TILE_IN = S5_TILE_GROUPS * S5_GROUP
S5_TILE_STATE = S5_TILE_GROUPS * S5_STATE
S5_ROWS_PER_STEP = 2048


def s5_prepare(lam_re, lam_im, log_dt, b_re, b_im, c_re, c_im):
    f32 = jnp.float32
    lam_re, lam_im = lam_re.astype(f32), lam_im.astype(f32)
    dt = jnp.exp(log_dt.astype(f32))[..., None]
    mag = jnp.exp(lam_re * dt)
    ar, ai = mag * jnp.cos(lam_im * dt), mag * jnp.sin(lam_im * dt)
    den = lam_re * lam_re + lam_im * lam_im
    cr = ((ar - 1.0) * lam_re + ai * lam_im) / den
    ci = (ai * lam_re - (ar - 1.0) * lam_im) / den
    bbr = cr[..., None] * b_re - ci[..., None] * b_im
    bbi = cr[..., None] * b_im + ci[..., None] * b_re
    eye = jnp.eye(S5_TILE_GROUPS, dtype=f32)

    def tile_in(bb):
        x = bb.reshape(2, S5_TILES, S5_TILE_GROUPS, S5_STATE, S5_GROUP)
        x = jnp.einsum('dtgnp,gh->dtgphn', x, eye)
        return x.reshape(2, S5_TILES, S5_TILE_IN, S5_TILE_STATE)

    def tile_out(cc):
        x = cc.astype(f32).reshape(2, S5_TILES, S5_TILE_GROUPS, S5_GROUP, S5_STATE)
        x = jnp.einsum('dtgpn,gh->dtgnhp', x, eye)
        return x.reshape(2, S5_TILES, S5_TILE_STATE, S5_TILE_IN)

    win = jnp.concatenate([tile_in(bbr), tile_in(bbi)], axis=-1).astype(jnp.bfloat16)
    wout = jnp.concatenate([tile_out(c_re), -tile_out(c_im)], axis=-2).astype(jnp.bfloat16)
    a = jnp.stack([ar.reshape(2, S5_TILES, S5_TILE_STATE), ai.reshape(2, S5_TILES, S5_TILE_STATE)], axis=2)
    return win, wout, a


def _s5_scan_kernel(u_ref, win_ref, wout_ref, a_ref, h0_ref, y_ref, fin_ref, bu_ref, st_ref, *, batch):
    d = pl.program_id(0)
    c = pl.program_id(2)
    n_c = pl.num_programs(2)
    ns = S5_TILE_STATE
    steps = u_ref.shape[0] // batch

    @pl.when(c == 0)
    def _():
        st_ref[...] = h0_ref[0, 0]

    bu_ref[...] = jnp.dot(u_ref[...].astype(jnp.bfloat16), win_ref[0, 0], preferred_element_type=jnp.float32)
    ar = jnp.broadcast_to(a_ref[0, 0, 0:1, :], (batch, ns))
    ai = jnp.broadcast_to(a_ref[0, 0, 1:2, :], (batch, ns))

    def step(t, carry):
        hr, hi = carry
        te = jnp.where(d == 0, t, steps - 1 - t)
        r0 = pl.multiple_of(te * batch, batch)
        nr = ar * hr - ai * hi + bu_ref[pl.ds(r0, batch), 0:ns]
        ni = ar * hi + ai * hr + bu_ref[pl.ds(r0, batch), ns:2 * ns]
        bu_ref[pl.ds(r0, batch), 0:ns] = nr
        bu_ref[pl.ds(r0, batch), ns:2 * ns] = ni
        return nr, ni

    hr, hi = lax.fori_loop(0, steps, step, (st_ref[:, 0:ns], st_ref[:, ns:2 * ns]), unroll=4)
    st_ref[:, 0:ns] = hr
    st_ref[:, ns:2 * ns] = hi
    y_ref[0] = jnp.dot(bu_ref[...].astype(jnp.bfloat16), wout_ref[0, 0], preferred_element_type=jnp.float32)

    @pl.when(c == n_c - 1)
    def _():
        fin_ref[0, 0] = st_ref[...]


def s5_scan(u_tm, win, wout, a, h0, batch):
    rows = u_tm.shape[0]
    r = min(S5_ROWS_PER_STEP, rows)
    assert rows % r == 0 and r % batch == 0
    n_c = rows // r

    def chunk(d, c):
        return c + d * (n_c - 1 - 2 * c)

    return pl.pallas_call(
        functools.partial(_s5_scan_kernel, batch=batch),
        grid=(2, S5_TILES, n_c),
        in_specs=[pl.BlockSpec((r, S5_TILE_IN), lambda d, j, c: (chunk(d, c), j)),
                  pl.BlockSpec((1, 1, S5_TILE_IN, 2 * S5_TILE_STATE), lambda d, j, c: (d, j, 0, 0)),
                  pl.BlockSpec((1, 1, 2 * S5_TILE_STATE, S5_TILE_IN), lambda d, j, c: (d, j, 0, 0)),
                  pl.BlockSpec((1, 1, 2, S5_TILE_STATE), lambda d, j, c: (d, j, 0, 0)),
                  pl.BlockSpec((1, 1, batch, 2 * S5_TILE_STATE), lambda d, j, c: (d, j, 0, 0))],
        out_specs=[pl.BlockSpec((1, r, S5_TILE_IN), lambda d, j, c: (d, chunk(d, c), j)),
                   pl.BlockSpec((1, 1, batch, 2 * S5_TILE_STATE), lambda d, j, c: (d, j, 0, 0))],
        out_shape=[jax.ShapeDtypeStruct((2, rows, S5_WIDTH), jnp.float32),
                   jax.ShapeDtypeStruct((2, S5_TILES, batch, 2 * S5_TILE_STATE), jnp.float32)],
        scratch_shapes=[pltpu.VMEM((r, 2 * S5_TILE_STATE), jnp.float32),
                        pltpu.VMEM((batch, 2 * S5_TILE_STATE), jnp.float32)],
        compiler_params=pltpu.CompilerParams(dimension_semantics=("parallel", "parallel", "arbitrary"),
                                             vmem_limit_bytes=VMEM_LIMIT_BYTES),
        name="s5_scan",
    )(u_tm, win, wout, a, h0)


def _gelu_tanh(x):
    return 0.5 * x * (1.0 + jnp.tanh(math.sqrt(2.0 / math.pi) * (x + 0.044715 * (x * x * x))))


def _s5_glu_kernel(u_ref, y_ref, d_ref, w_ref, b_ref, o_ref):
    y = d_ref[...] * u_ref[...] + y_ref[0] + y_ref[1]
    y = _gelu_tanh(y)
    z = jnp.dot(y.astype(jnp.bfloat16), w_ref[...], preferred_element_type=jnp.float32) + b_ref[...]
    o_ref[...] = y * (1.0 / (1.0 + jnp.exp(-z)))


def s5_glu(u_tm, y, d_skip, glu_w_bf16, glu_b, tr=1024):
    rows, w = u_tm.shape
    tr = min(tr, rows)
    assert rows % tr == 0
    return pl.pallas_call(
        _s5_glu_kernel,
        grid=(rows // tr,),
        in_specs=[pl.BlockSpec((tr, w), lambda i: (i, 0)),
                  pl.BlockSpec((2, tr, w), lambda i: (0, i, 0)),
                  pl.BlockSpec((1, w), lambda i: (0, 0)),
                  pl.BlockSpec((w, w), lambda i: (0, 0)),
                  pl.BlockSpec((1, w), lambda i: (0, 0))],
        out_specs=pl.BlockSpec((tr, w), lambda i: (i, 0)),
        out_shape=jax.ShapeDtypeStruct((rows, w), jnp.float32),
        compiler_params=pltpu.CompilerParams(dimension_semantics=("parallel",),
                                             vmem_limit_bytes=VMEM_LIMIT_BYTES),
        name="s5_glu",
    )(u_tm, y, d_skip.reshape(1, w), glu_w_bf16, glu_b.reshape(1, w))


def s5_mixer(u, h0_re, h0_im, prep, d_skip, glu_w, glu_b):
    win, wout, a = prep
    b, l, w = u.shape
    u_tm = jnp.swapaxes(u, 0, 1).reshape(l * b, w)
    if h0_re is None:
        h0 = jnp.zeros((2, S5_TILES, b, 2 * S5_TILE_STATE), jnp.float32)
    else:
        def tiles(h):
            return jnp.transpose(h.astype(jnp.float32).reshape(b, 2, S5_TILES, S5_TILE_STATE), (1, 2, 0, 3))
        h0 = jnp.concatenate([tiles(h0_re), tiles(h0_im)], axis=-1)
    y, fin = s5_scan(u_tm, win, wout, a, h0, b)
    out_tm = s5_glu(u_tm, y, d_skip, glu_w.astype(jnp.bfloat16), glu_b)
    out = jnp.swapaxes(out_tm.reshape(l, b, w), 0, 1)

    def untile(f):
        return jnp.transpose(f, (2, 0, 1, 3)).reshape(b, 2, S5_GROUPS, S5_STATE)
    return out, untile(fin[..., :S5_TILE_STATE]), untile(fin[..., S5_TILE_STATE:])


def _half_mean_matrix():
    i = lax.broadcasted_iota(jnp.int32, (LANES, LANES), 0) // DIFF_QK_DIM
    j = lax.broadcasted_iota(jnp.int32, (LANES, LANES), 1) // DIFF_QK_DIM
    return jnp.where(i == j, 1.0 / DIFF_QK_DIM, 0.0).astype(jnp.bfloat16)


def _rms_groups(x, gain, avg):
    xx = x * x
    hi = xx.astype(jnp.bfloat16)
    lo = (xx - hi.astype(jnp.float32)).astype(jnp.bfloat16)
    ms = (jnp.dot(hi, avg, preferred_element_type=jnp.float32)
          + jnp.dot(lo, avg, preferred_element_type=jnp.float32))
    return x * lax.rsqrt(ms + NORM_EPS) * gain


def _rope_lanes(y, cos, sin_signed, first_half):
    rot = jnp.where(first_half, pltpu.roll(y, LANES - 16, 1), pltpu.roll(y, 16, 1))
    return y * cos + rot * sin_signed


def _diff_prep_kernel(q_ref, k_ref, qg_ref, kg_ref, cos_ref, sin_ref, qo_ref, ko_ref, kn_ref, *, use_rope):
    avg = _half_mean_matrix()
    lane = lax.broadcasted_iota(jnp.int32, (1, LANES), 1)
    first_half = (lane % 32) < 16
    scale = DIFF_QK_DIM ** -0.5 * math.log2(math.e)
    for h in range(DIFF_HEADS):
        cols = slice(h * LANES, (h + 1) * LANES)
        qn = _rms_groups(q_ref[:, cols], qg_ref[...], avg)
        kn = _rms_groups(k_ref[:, cols], kg_ref[...], avg)
        kn_ref[:, cols] = kn
        if use_rope:
            qn = _rope_lanes(qn, cos_ref[...], sin_ref[...], first_half)
            kn = _rope_lanes(kn, cos_ref[...], sin_ref[...], first_half)
        qo_ref[:, cols] = (qn * scale).astype(jnp.bfloat16)
        ko_ref[:, cols] = kn.astype(jnp.bfloat16)


def diff_prep(p, q_g, k_g, cos, sin_signed, seq_len, use_rope, tr=512):
    rows = p.shape[0]
    tr = min(tr, seq_len)
    assert seq_len % tr == 0
    w = DIFF_HEADS * LANES
    per_seq = seq_len // tr
    g2 = lambda g: jnp.tile(g.astype(jnp.float32), 2).reshape(1, LANES)
    return pl.pallas_call(
        functools.partial(_diff_prep_kernel, use_rope=use_rope),
        grid=(rows // tr,),
        in_specs=[pl.BlockSpec((tr, w), lambda i: (i, 1)),
                  pl.BlockSpec((tr, w), lambda i: (i, 2)),
                  pl.BlockSpec((1, LANES), lambda i: (0, 0)),
                  pl.BlockSpec((1, LANES), lambda i: (0, 0)),
                  pl.BlockSpec((tr, LANES), lambda i: (i % per_seq, 0)),
                  pl.BlockSpec((tr, LANES), lambda i: (i % per_seq, 0))],
        out_specs=[pl.BlockSpec((tr, w), lambda i: (i, 0))] * 3,
        out_shape=[jax.ShapeDtypeStruct((rows, w), jnp.bfloat16), jax.ShapeDtypeStruct((rows, w), jnp.bfloat16),
                   jax.ShapeDtypeStruct((rows, w), jnp.float32)],
        compiler_params=pltpu.CompilerParams(dimension_semantics=("parallel",),
                                             vmem_limit_bytes=VMEM_LIMIT_BYTES),
        name="diff_prep",
    )(p, p, g2(q_g), g2(k_g), cos, sin_signed)


def rope_tables(l):
    t = jnp.arange(l)
    row = (t // GRID_W).astype(jnp.float32)
    col = (t % GRID_W).astype(jnp.float32)
    nf = DIFF_QK_DIM // 4
    inv = ROPE_BASE ** (-jnp.arange(nf, dtype=jnp.float32) / nf)
    ang = jnp.stack([row[:, None] * inv, col[:, None] * inv], axis=1)
    ang = jnp.stack([ang, ang], axis=2).reshape(l, DIFF_QK_DIM)
    sign = jnp.where((jnp.arange(DIFF_QK_DIM) % 32) < 16, -1.0, 1.0)
    return jnp.tile(jnp.cos(ang), (1, 2)), jnp.tile(jnp.sin(ang) * sign, (1, 2))


def _diff_attn_kernel(*refs, has_ctx):
    if has_ctx:
        q_ref, k_ref, v_ref, ck_ref, cv_ref, lam_ref, g_ref, o_ref = refs
    else:
        q_ref, k_ref, v_ref, lam_ref, g_ref, o_ref = refs
    q = q_ref[...]
    lane = lax.broadcasted_iota(jnp.int32, (1, LANES), 1)
    zero = jnp.zeros_like(q)
    qs = (jnp.where(lane < DIFF_QK_DIM, q, zero), jnp.where(lane >= DIFF_QK_DIM, q, zero))
    nt = (((1,), (1,)), ((), ()))
    k = k_ref[...]
    ck = ck_ref[...].astype(jnp.bfloat16) if has_ctx else None
    lam = lam_ref[0:1, 0:1]
    w_self, w_ctx = None, None
    for i in range(2):
        s = lax.dot_general(qs[i], k, nt, preferred_element_type=jnp.float32)
        m = jnp.max(s, axis=-1, keepdims=True)
        if has_ctx:
            sc = lax.dot_general(qs[i], ck, nt, preferred_element_type=jnp.float32)
            m = jnp.maximum(m, jnp.max(sc, axis=-1, keepdims=True))
        p = jnp.exp2(s - m)
        l = jnp.sum(p, axis=-1, keepdims=True)
        if has_ctx:
            pc = jnp.exp2(sc - m)
            l = l + jnp.sum(pc, axis=-1, keepdims=True)
        coef = 1.0 / l if i == 0 else -lam / l
        w_self = p * coef if i == 0 else w_self + p * coef
        if has_ctx:
            w_ctx = pc * coef if i == 0 else w_ctx + pc * coef
    o = jnp.dot(w_self.astype(jnp.bfloat16), v_ref[...].astype(jnp.bfloat16), preferred_element_type=jnp.float32)
    if has_ctx:
        o = o + jnp.dot(w_ctx.astype(jnp.bfloat16), cv_ref[...].astype(jnp.bfloat16),
                        preferred_element_type=jnp.float32)
    ms = jnp.mean(o * o, axis=-1, keepdims=True)
    o_ref[...] = o * lax.rsqrt(ms + NORM_EPS) * g_ref[...]


def diff_attention(q_bf16, k_bf16, p, ctx_k, ctx_v, lam, gain, batch, seq_len, tq=256):
    tq = min(tq, seq_len)
    nq = seq_len // tq
    has_ctx = ctx_k is not None
    v_col0 = 3 * DIFF_HEADS
    in_specs = [pl.BlockSpec((tq, LANES), lambda b, h, i: (b * nq + i, h)),
                pl.BlockSpec((seq_len, LANES), lambda b, h, i: (b, h)),
                pl.BlockSpec((seq_len, LANES), lambda b, h, i: (b, v_col0 + h))]
    args = [q_bf16, k_bf16, p]
    if has_ctx:
        lc = ctx_k.shape[0] // batch
        in_specs += [pl.BlockSpec((lc, LANES), lambda b, h, i: (b, h))] * 2
        args += [ctx_k, ctx_v]
    in_specs += [pl.BlockSpec((1, LANES), lambda b, h, i: (0, 0))] * 2
    args += [jnp.broadcast_to(lam.astype(jnp.float32), (1, LANES)), gain.astype(jnp.float32).reshape(1, LANES)]
    return pl.pallas_call(
        functools.partial(_diff_attn_kernel, has_ctx=has_ctx),
        grid=(batch, DIFF_HEADS, nq),
        in_specs=in_specs,
        out_specs=pl.BlockSpec((tq, LANES), lambda b, h, i: (b * nq + i, h)),
        out_shape=jax.ShapeDtypeStruct((batch * seq_len, DIFF_HEADS * LANES), jnp.float32),
        compiler_params=pltpu.CompilerParams(dimension_semantics=("parallel", "parallel", "arbitrary"),
                                             vmem_limit_bytes=VMEM_LIMIT_BYTES),
        name="diff_attention",
    )(*args)


NA_Q_ROWS = 8
NA_WIN_ROWS = 16


def _na_prep_kernel(q_ref, k_ref, qg_ref, kg_ref, qo_ref, ko_ref, kn_ref):
    avg = jnp.full((LANES, LANES), 1.0 / HEAD_DIM, jnp.bfloat16)
    for h in range(NA_HEADS):
        cols = slice(h * LANES, (h + 1) * LANES)
        qo_ref[:, cols] = _rms_groups(q_ref[:, cols], qg_ref[...], avg).astype(jnp.bfloat16)
        kn = _rms_groups(k_ref[:, cols], kg_ref[...], avg)
        kn_ref[:, cols] = kn
        ko_ref[:, cols] = kn.astype(jnp.bfloat16)


def na_prep(p, q_g, k_g, tr=512):
    rows = p.shape[0]
    tr = min(tr, rows)
    w = NA_HEADS * LANES
    g1 = lambda g: g.astype(jnp.float32).reshape(1, LANES)
    return pl.pallas_call(
        _na_prep_kernel,
        grid=(rows // tr,),
        in_specs=[pl.BlockSpec((tr, w), lambda i: (i, 0)),
                  pl.BlockSpec((tr, w), lambda i: (i, 1)),
                  pl.BlockSpec((1, LANES), lambda i: (0, 0)),
                  pl.BlockSpec((1, LANES), lambda i: (0, 0))],
        out_specs=[pl.BlockSpec((tr, w), lambda i: (i, 0))] * 3,
        out_shape=[jax.ShapeDtypeStruct((rows, w), jnp.bfloat16), jax.ShapeDtypeStruct((rows, w), jnp.bfloat16),
                   jax.ShapeDtypeStruct((rows, w), jnp.float32)],
        compiler_params=pltpu.CompilerParams(dimension_semantics=("parallel",),
                                             vmem_limit_bytes=VMEM_LIMIT_BYTES),
        name="na_prep",
    )(p, p, g1(q_g), g1(k_g))


def na_bias_tables(rpb, rows):
    nblk = rows // NA_Q_ROWS
    cq = jnp.arange(GRID_W)[:, None]
    ck = jnp.arange(GRID_W)[None, :]
    col_start = jnp.clip(cq - NA_KW // 2, 0, GRID_W - NA_KW)
    col_ok = (ck >= col_start) & (ck < col_start + NA_KW)
    cidx = jnp.clip(ck - cq + NA_KW - 1, 0, 2 * NA_KW - 2)
    tabs = []
    for i in (0, 1, nblk - 1):
        base = min(max(NA_Q_ROWS * i - NA_KH // 2, 0), rows - NA_WIN_ROWS)
        r = NA_Q_ROWS * i + jnp.arange(NA_Q_ROWS)[:, None]
        rk = base + jnp.arange(NA_WIN_ROWS)[None, :]
        r0 = jnp.clip(r - NA_KH // 2, 0, rows - NA_KH)
        row_ok = (rk >= r0) & (rk < r0 + NA_KH)
        ridx = jnp.clip(rk - r + NA_KH - 1, 0, 2 * NA_KH - 2)
        b = rpb[:, ridx][:, :, :, cidx]
        ok = row_ok[:, :, None, None] & col_ok[None, None, :, :]
        b = jnp.where(ok[None], b.astype(jnp.float32), NEG_INF)
        tabs.append(jnp.transpose(b, (0, 1, 3, 2, 4)).reshape(NA_HEADS, NA_Q_ROWS * GRID_W, NA_WIN_ROWS * GRID_W))
    return jnp.stack(tabs, axis=0)


def _na_attn_kernel(*refs, windowed, rows):
    nt = (((1,), (1,)), ((), ()))
    scale = HEAD_DIM ** -0.5
    if windowed:
        q_ref, k_ref, v_ref, b_ref, ck_ref, cv_ref, o_ref = refs
        i = pl.program_id(2)
        base = jnp.clip(NA_Q_ROWS * i - NA_KH // 2, 0, rows - NA_WIN_ROWS)
        k0 = pl.multiple_of(base * GRID_W, GRID_W)
        nk = NA_WIN_ROWS * GRID_W
        q = q_ref[...]
        s = lax.dot_general(q, k_ref[pl.ds(k0, nk), :], nt, preferred_element_type=jnp.float32) * scale + b_ref[0, 0]
        sc = lax.dot_general(q, ck_ref[...].astype(jnp.bfloat16), nt, preferred_element_type=jnp.float32) * scale
        m = jnp.maximum(jnp.max(s, axis=-1, keepdims=True), jnp.max(sc, axis=-1, keepdims=True))
        p = jnp.exp(s - m)
        pc = jnp.exp(sc - m)
        l = jnp.sum(p, axis=-1, keepdims=True) + jnp.sum(pc, axis=-1, keepdims=True)
        o = (jnp.dot(p.astype(jnp.bfloat16), v_ref[pl.ds(k0, nk), :].astype(jnp.bfloat16),
                     preferred_element_type=jnp.float32)
             + jnp.dot(pc.astype(jnp.bfloat16), cv_ref[...].astype(jnp.bfloat16), preferred_element_type=jnp.float32))
    else:
        q_ref, k_ref, v_ref, o_ref = refs
        s = lax.dot_general(q_ref[...], k_ref[...], nt, preferred_element_type=jnp.float32) * scale
        m = jnp.max(s, axis=-1, keepdims=True)
        p = jnp.exp(s - m)
        l = jnp.sum(p, axis=-1, keepdims=True)
        o = jnp.dot(p.astype(jnp.bfloat16), v_ref[...].astype(jnp.bfloat16), preferred_element_type=jnp.float32)
    o_ref[...] = o / l


def na_attention(q_bf16, k_bf16, p, bias_tabs, ctx_k, ctx_v, batch, seq_len):
    windowed = bias_tabs is not None
    v_col0 = 2 * NA_HEADS
    rows = seq_len // GRID_W
    if windowed:
        tq = NA_Q_ROWS * GRID_W
        nq = seq_len // tq
        lc = ctx_k.shape[0] // batch
        nk = NA_WIN_ROWS * GRID_W
        in_specs = [pl.BlockSpec((tq, LANES), lambda b, h, i: (b * nq + i, h)),
                    pl.BlockSpec((seq_len, LANES), lambda b, h, i: (b, h)),
                    pl.BlockSpec((seq_len, LANES), lambda b, h, i: (b, v_col0 + h)),
                    pl.BlockSpec((1, 1, tq, nk),
                                 lambda b, h, i: (jnp.where(i == 0, 0, jnp.where(i == nq - 1, 2, 1)), h, 0, 0)),
                    pl.BlockSpec((lc, LANES), lambda b, h, i: (b, h)),
                    pl.BlockSpec((lc, LANES), lambda b, h, i: (b, h))]
        args = [q_bf16, k_bf16, p, bias_tabs, ctx_k, ctx_v]
    else:
        tq = seq_len
        nq = 1
        in_specs = [pl.BlockSpec((tq, LANES), lambda b, h, i: (b, h)),
                    pl.BlockSpec((seq_len, LANES), lambda b, h, i: (b, h)),
                    pl.BlockSpec((seq_len, LANES), lambda b, h, i: (b, v_col0 + h))]
        args = [q_bf16, k_bf16, p]
    return pl.pallas_call(
        functools.partial(_na_attn_kernel, windowed=windowed, rows=rows),
        grid=(batch, NA_HEADS, nq),
        in_specs=in_specs,
        out_specs=pl.BlockSpec((tq, LANES), lambda b, h, i: (b * nq + i, h)),
        out_shape=jax.ShapeDtypeStruct((batch * seq_len, NA_HEADS * LANES), jnp.float32),
        compiler_params=pltpu.CompilerParams(dimension_semantics=("parallel", "parallel", "arbitrary"),
                                             vmem_limit_bytes=VMEM_LIMIT_BYTES),
        name="na_attention",
    )(*args)


def even_mixer(x, mods, norm_g, b, l, ev, ctx_kv, bias_tabs):
    (w_in, w_out, q_g, k_g, conv_w, conv_b, w1, b1, f1, w2, b2, f2, w3, decay, bias) = ev
    p = mm_norm(x, norm_g, mods[0], mods[1], w_in)
    q, k, kn = na_prep(p, q_g, k_g)
    if ctx_kv is None:
        oa = na_attention(q, k, p, None, None, None, b, l)
    else:
        lc = ctx_kv[0].shape[1]
        oa = na_attention(q, k, p, bias_tabs, ctx_kv[0].reshape(b * lc, MIX_WIDTH),
                          ctx_kv[1].reshape(b * lc, MIX_WIDTH), b, l)
    hb = p[:, 3 * MIX_WIDTH:].reshape(b, l, 3 * HY_WIDTH)
    ob = hyena(hb, conv_w, conv_b, (w1, b1, f1, w2, b2, f2, w3, decay), bias)
    out = mm_pair_res(oa, ob.reshape(b * l, HY_WIDTH), w_out, x, mods[2])
    shp = (b, l, NA_HEADS, HEAD_DIM)
    return out, kn.reshape(shp), p[:, 2 * MIX_WIDTH:3 * MIX_WIDTH].reshape(shp)


def odd_mixer(x, mods, norm_g, b, l, od, lam_init, h0_re, h0_im, ctx_kv, rope):
    (w_in, w_out, s5_prep, d_skip, glu_w, glu_b, q_g, k_g, lq1, lk1, lq2, lk2, subln_g) = od
    p = mm_norm(x, norm_g, mods[0], mods[1], w_in)
    oc, fr, fi = s5_mixer(p[:, :S5_WIDTH].reshape(b, l, S5_WIDTH), h0_re, h0_im, s5_prep, d_skip, glu_w, glu_b)
    lam = (jnp.exp(jnp.sum(lq1 * lk1).astype(jnp.float32))
           - jnp.exp(jnp.sum(lq2 * lk2).astype(jnp.float32)) + lam_init)
    cos, sin_signed = rope
    q, k, kn = diff_prep(p, q_g, k_g, cos, sin_signed, l, use_rope=ctx_kv is not None)
    gain = subln_g.astype(jnp.float32) * (1.0 - lam_init)
    if ctx_kv is None:
        o = diff_attention(q, k, p, None, None, lam, gain, b, l)
    else:
        lc = ctx_kv[0].shape[1]
        o = diff_attention(q, k, p, ctx_kv[0].reshape(b * lc, MIX_WIDTH), ctx_kv[1].reshape(b * lc, MIX_WIDTH),
                           lam, gain, b, l)
    out = mm_pair_res(oc.reshape(b * l, S5_WIDTH), o, w_out, x, mods[2])
    kn = kn.reshape(b, l, DIFF_HEADS, 2, DIFF_QK_DIM)
    v = p[:, S5_WIDTH + 2 * MIX_WIDTH:].reshape(b, l, DIFF_HEADS, DIFF_V_DIM)
    return out, kn, v, fr, fi


PEER_TOKEN_TILE = 512
PEER_EXPERT_TILE = 1024
PEER_SUB_EXPERTS = 256


def _top16_rows(cur, iota):
    n = cur.shape[0]
    tops, idxs = [], []
    for _ in range(PEER_TOPK):
        m = jnp.max(cur, axis=0, keepdims=True)
        first = jnp.min(jnp.where(cur == m, iota, n), axis=0, keepdims=True)
        tops.append(m)
        idxs.append(first)
        cur = jnp.where(iota == first, -jnp.inf, cur)
    return tops, idxs


def _router_kernel(q_ref, keys_ref, m_ref, r1_ref, e0_ref, e1_ref, st_ref):
    tt = q_ref.shape[0]
    half = PEER_QUERY_DIM // 2
    for h in range(PEER_HEADS):
        for s in range(2):
            qs = q_ref[:, (2 * h + s) * half:(2 * h + s + 1) * half].astype(jnp.bfloat16)
            st = lax.dot_general(keys_ref[h, s], qs, (((1,), (1,)), ((), ())),
                                 preferred_element_type=jnp.float32)
            st_ref[2 * h + s] = st

    iota128 = lax.broadcasted_iota(jnp.int32, (PEER_N_KEYS, LANES), 0)
    iota8 = lax.broadcasted_iota(jnp.int32, (SUBLANES, LANES), 0)
    iota_cand = lax.broadcasted_iota(jnp.int32, (PEER_TOPK + SUBLANES * SUBLANES, LANES), 0)
    n_chunks = tt // LANES

    def route(h, lane0):
        s0 = st_ref[2 * h, :, pl.ds(lane0, LANES)]
        s1 = st_ref[2 * h + 1, :, pl.ds(lane0, LANES)]
        a, a_idx = _top16_rows(s0, iota128)
        b, b_idx = _top16_rows(s1, iota128)
        bmat = jnp.concatenate(b, axis=0)
        sums = [a[k] + bmat for k in range(PEER_TOPK)]
        cand = [sums[0], sums[1][:SUBLANES]]
        for k in range(2, SUBLANES):
            cand.append(jnp.where(iota8 < PEER_TOPK // (k + 1), sums[k][:SUBLANES], -jnp.inf))
        cand.append(jnp.concatenate(a[SUBLANES:], axis=0) + b[0])
        f, _ = _top16_rows(jnp.concatenate(cand, axis=0), iota_cand)
        tau = f[PEER_TOPK - 1]
        z = jnp.ones_like(f[0])
        for k in range(1, PEER_TOPK):
            z = z + jnp.exp(f[k] - f[0])
        m = jnp.zeros((PEER_N_KEYS, LANES), jnp.float32)
        r1 = jnp.full((PEER_N_KEYS, LANES), float(PEER_TOPK), jnp.float32)
        for k in range(PEER_TOPK):
            m_k = jnp.sum(jnp.where(sums[k] >= tau, 1.0, 0.0), axis=0, keepdims=True)
            m = jnp.where(iota128 == a_idx[k], m_k, m)
            r1 = jnp.where(iota128 == b_idx[k], float(k), r1)
        m_ref[h, :, pl.ds(lane0, LANES)] = m
        r1_ref[h, :, pl.ds(lane0, LANES)] = r1.astype(jnp.bfloat16)
        e0_ref[h, :, pl.ds(lane0, LANES)] = jnp.exp(s0 - a[0]) / z
        e1_ref[h, :, pl.ds(lane0, LANES)] = jnp.exp(s1 - b[0]).astype(jnp.bfloat16)

    def body(h, carry):
        for c in range(n_chunks):
            route(h, c * LANES)
        return carry

    lax.fori_loop(0, PEER_HEADS, body, 0)


def peer_router(q, keys_bf16, tt):
    t = q.shape[0]
    assert t % tt == 0 and tt % LANES == 0
    f32 = jnp.float32
    big = jax.ShapeDtypeStruct((PEER_HEADS, PEER_N_KEYS, t), f32)
    half = jax.ShapeDtypeStruct((PEER_HEADS, PEER_N_KEYS, t), jnp.bfloat16)
    blk = pl.BlockSpec((PEER_HEADS, PEER_N_KEYS, tt), lambda i: (0, 0, i))
    return pl.pallas_call(
        _router_kernel,
        grid=(t // tt,),
        in_specs=[pl.BlockSpec((tt, q.shape[1]), lambda i: (i, 0)),
                  pl.BlockSpec(keys_bf16.shape, lambda i: (0, 0, 0, 0))],
        out_specs=[blk, blk, blk, blk],
        out_shape=[big, half, big, half],
        scratch_shapes=[pltpu.VMEM((2 * PEER_HEADS, PEER_N_KEYS, tt), f32)],
        compiler_params=pltpu.CompilerParams(dimension_semantics=("parallel",),
                                             vmem_limit_bytes=VMEM_LIMIT_BYTES),
        name="peer_router",
    )(q, keys_bf16)


def _dense_kernel(xt_ref, u_ref, vt_ref, m_ref, r1_ref, e0_ref, e1_ref, res_ref, gate_ref,
                  o_ref, at_ref, wt_ref, acc_ref):
    e = pl.program_id(1)
    n_e = pl.num_programs(1)
    eb = u_ref.shape[0]
    tt = xt_ref.shape[1]
    n_chunks = tt // LANES
    n_sub = eb // PEER_SUB_EXPERTS
    i1_per_sub = PEER_SUB_EXPERTS // PEER_N_KEYS

    @pl.when(e == 0)
    def _():
        acc_ref[...] = jnp.zeros_like(acc_ref)

    at_ref[...] = jnp.dot(u_ref[...], xt_ref[...], preferred_element_type=jnp.float32)
    for sb in range(n_sub):
        for c in range(n_chunks):
            lanes = slice(c * LANES, (c + 1) * LANES)
            for k in range(i1_per_sub):
                i1 = sb * i1_per_sub + k
                zero = jnp.zeros((PEER_N_KEYS, LANES), jnp.bfloat16)
                g = zero
                for h in range(PEER_HEADS):
                    mrow = m_ref[h, i1:i1 + 1, lanes].astype(jnp.bfloat16)
                    e0row = e0_ref[h, i1:i1 + 1, lanes].astype(jnp.bfloat16)
                    g = g + jnp.where(r1_ref[h, :, lanes] < mrow, e0row * e1_ref[h, :, lanes], zero)
                r = slice(i1 * PEER_N_KEYS, (i1 + 1) * PEER_N_KEYS)
                wt_ref[r, lanes] = _gelu_tanh(at_ref[r, lanes]).astype(jnp.bfloat16) * g
    acc_ref[...] += jnp.dot(vt_ref[0], wt_ref[...], preferred_element_type=jnp.float32)

    @pl.when(e == n_e - 1)
    def _():
        o_ref[...] = res_ref[...] + gate_ref[0] * acc_ref[...].T


def peer_dense(xt_bf16, u_bf16, vt_bf16, m, r1, e0, e1, resid, gate, tokens_per_gate, tt, eb):
    d, t = xt_bf16.shape
    n_exp = u_bf16.shape[0]
    assert t % tt == 0 and n_exp % eb == 0 and tokens_per_gate % tt == 0
    assert eb // PEER_N_KEYS == SUBLANES
    tiles_per_gate = tokens_per_gate // tt
    rblk = pl.BlockSpec((PEER_HEADS, PEER_N_KEYS, tt), lambda i, j: (0, 0, i))
    gblk = pl.BlockSpec((PEER_HEADS, SUBLANES, tt), lambda i, j: (0, j, i))
    return pl.pallas_call(
        _dense_kernel,
        grid=(t // tt, n_exp // eb),
        in_specs=[pl.BlockSpec((d, tt), lambda i, j: (0, i)),
                  pl.BlockSpec((eb, d), lambda i, j: (j, 0)),
                  pl.BlockSpec((1, d, eb), lambda i, j: (j, 0, 0)),
                  gblk, rblk, gblk, rblk,
                  pl.BlockSpec((tt, d), lambda i, j: (i, 0)),
                  pl.BlockSpec((1, 1, d), lambda i, j: (i // tiles_per_gate, 0, 0))],
        out_specs=pl.BlockSpec((tt, d), lambda i, j: (i, 0)),
        out_shape=jax.ShapeDtypeStruct((t, d), jnp.float32),
        scratch_shapes=[pltpu.VMEM((eb, tt), jnp.float32),
                        pltpu.VMEM((eb, tt), jnp.bfloat16),
                        pltpu.VMEM((d, tt), jnp.float32)],
        compiler_params=pltpu.CompilerParams(dimension_semantics=("parallel", "arbitrary"),
                                             vmem_limit_bytes=VMEM_LIMIT_BYTES),
        name="peer_dense",
    )(xt_bf16, u_bf16, vt_bf16, m, r1, e0, e1, resid, gate)


def peer_layer(x, norm_g, mods, w_q, keys, u_bf16, vt_bf16):
    t, d = x.shape
    q, h = mm_norm(x, norm_g, mods[3], mods[4], w_q, emit_h=True)
    gate = mods[5]
    m, r1, e0, e1 = peer_router(q, keys.astype(jnp.bfloat16), PEER_TOKEN_TILE)
    return peer_dense(h.T, u_bf16, vt_bf16, m, r1, e0, e1, x, gate, t // gate.shape[0],
                      PEER_TOKEN_TILE, PEER_EXPERT_TILE)


def kernel(x_prompt, x_sample, c, cache_na_k, cache_na_v, cache_diff_k, cache_diff_v, state_s5_re, state_s5_im, c_ctx, mod_w, mod_b, norm_mix_g, norm_ffn_g, ev_w_in, ev_w_out, na_q_g, na_k_g, na_rpb, hy_conv_w, hy_conv_b, hy_w1, hy_b1, hy_f1, hy_w2, hy_b2, hy_f2, hy_w3, hy_decay, hy_bias, od_w_in, od_w_out, s5_lam_re, s5_lam_im, s5_log_dt, s5_b_re, s5_b_im, s5_c_re, s5_c_im, s5_d, s5_glu_w, s5_glu_b, diff_q_g, diff_k_g, diff_lq1, diff_lk1, diff_lq2, diff_lk2, diff_subln_g, peer_w_q, peer_keys, peer_u, peer_v):
    bp, lp, d = x_prompt.shape
    bs, ls, _ = x_sample.shape
    rope_p, rope_s = rope_tables(lp), rope_tables(ls)
    xp, xs = x_prompt.reshape(bp * lp, d), x_sample.reshape(bs * ls, d)
    new_na_k, new_na_v, new_dk, new_dv, new_sr, new_si = [], [], [], [], [], []
    for i in range(DEPTH):
        mods = adaln(jnp.concatenate([c_ctx[None, :], c], axis=0), mod_w[i], mod_b[i])
        mp = [v[:1] for v in mods]
        ms = [v[1:] for v in mods]
        j = i // 2
        if i % 2 == 0:
            ev = (ev_w_in[j].astype(jnp.bfloat16), ev_w_out[j].astype(jnp.bfloat16), na_q_g[j], na_k_g[j],
                  hy_conv_w[j], hy_conv_b[j],
                  hy_w1[j], hy_b1[j], hy_f1[j], hy_w2[j], hy_b2[j], hy_f2[j], hy_w3[j], hy_decay[j], hy_bias[j])
            bias_tabs = na_bias_tables(na_rpb[j], ls // GRID_W)
            xp, kp, vp = even_mixer(xp, mp, norm_mix_g[i], bp, lp, ev, None, None)
            xs, _, _ = even_mixer(xs, ms, norm_mix_g[i], bs, ls, ev, (cache_na_k[:, j], cache_na_v[:, j]), bias_tabs)
            new_na_k.append(kp)
            new_na_v.append(vp)
        else:
            lam_init = 0.8 - 0.6 * math.exp(-0.3 * i)
            s5_prep = s5_prepare(s5_lam_re[j], s5_lam_im[j], s5_log_dt[j], s5_b_re[j], s5_b_im[j],
                                 s5_c_re[j], s5_c_im[j])
            od = (od_w_in[j].astype(jnp.bfloat16), od_w_out[j].astype(jnp.bfloat16), s5_prep, s5_d[j],
                  s5_glu_w[j], s5_glu_b[j], diff_q_g[j], diff_k_g[j],
                  diff_lq1[j], diff_lk1[j], diff_lq2[j], diff_lk2[j], diff_subln_g[j])
            xp, kp, vp, sr, si = odd_mixer(xp, mp, norm_mix_g[i], bp, lp, od, lam_init, None, None, None, rope_p)
            xs, _, _, _, _ = odd_mixer(xs, ms, norm_mix_g[i], bs, ls, od, lam_init, state_s5_re[:, j],
                                       state_s5_im[:, j], (cache_diff_k[:, j], cache_diff_v[:, j]), rope_s)
            new_dk.append(kp)
            new_dv.append(vp)
            new_sr.append(sr)
            new_si.append(si)
        u_bf16 = peer_u[i].astype(jnp.bfloat16)
        vt_bf16 = jnp.swapaxes(peer_v[i].reshape(-1, PEER_EXPERT_TILE, d), 1, 2).astype(jnp.bfloat16)
        w_q_bf16 = peer_w_q[i].astype(jnp.bfloat16)
        xp = peer_layer(xp, norm_ffn_g[i], mp, w_q_bf16, peer_keys[i], u_bf16, vt_bf16)
        xs = peer_layer(xs, norm_ffn_g[i], ms, w_q_bf16, peer_keys[i], u_bf16, vt_bf16)
    return (xp.reshape(bp, lp, d), xs.reshape(bs, ls, d), jnp.stack(new_na_k, axis=1), jnp.stack(new_na_v, axis=1),
            jnp.stack(new_dk, axis=1), jnp.stack(new_dv, axis=1),
            jnp.stack(new_sr, axis=1), jnp.stack(new_si, axis=1))
```

```python
import functools
import math

import jax
import jax.numpy as jnp
from jax import lax
from jax.experimental import pallas as pl
from jax.experimental.pallas import tpu as pltpu

D_MODEL = 2048
DEPTH = 2
GRID_W = 64
MIX_WIDTH = D_MODEL // 2
HEAD_DIM = 128
NA_HEADS = MIX_WIDTH // HEAD_DIM
NA_KH = 8
NA_KW = 16
HY_WIDTH = MIX_WIDTH
HY_ORDER = 2
HY_POS_EMB = 33
S5_WIDTH = MIX_WIDTH
S5_GROUP = 16
S5_GROUPS = S5_WIDTH // S5_GROUP
S5_STATE = 64
DIFF_HEADS = MIX_WIDTH // HEAD_DIM
DIFF_QK_DIM = HEAD_DIM // 2
DIFF_V_DIM = HEAD_DIM
ROPE_BASE = 10000.0
PEER_HEADS = 8
PEER_N_KEYS = 128
PEER_QUERY_DIM = 256
PEER_TOPK = 16
NORM_EPS = 1e-6
NEG_INF = -1e30

LANES = 128
SUBLANES = 8
VMEM_LIMIT_BYTES = 56 * 1024 * 1024


MM_ROW_TILE = 1024
MM_COL_TILE = 512


def _mm_norm_kernel(x_ref, g_ref, sc_ref, sh_ref, w_ref, *out_refs, emit_h):
    if emit_h:
        o_ref, h_ref, hs_ref = out_refs
    else:
        o_ref, hs_ref = out_refs

    @pl.when(pl.program_id(1) == 0)
    def _():
        x = x_ref[...]
        gain = g_ref[...] * (1.0 + sc_ref[0])
        r = lax.rsqrt(jnp.mean(x * x, axis=-1, keepdims=True) + NORM_EPS)
        hs_ref[...] = ((x * r) * gain + sh_ref[0]).astype(jnp.bfloat16)
        if emit_h:
            h_ref[...] = hs_ref[...]

    o_ref[...] = jnp.dot(hs_ref[...], w_ref[...].astype(jnp.bfloat16), preferred_element_type=jnp.float32)


def mm_norm(x, norm_g, shift, scale, w, emit_h=False):
    m, k = x.shape
    n = w.shape[1]
    tm, tn = min(MM_ROW_TILE, m), min(MM_COL_TILE, n)
    nb = scale.shape[0]
    assert m % tm == 0 and n % tn == 0 and (m // nb) % tm == 0
    tiles_per_mod = (m // nb) // tm
    mod = pl.BlockSpec((1, 1, k), lambda i, j: (i // tiles_per_mod, 0, 0))
    out_specs = [pl.BlockSpec((tm, tn), lambda i, j: (i, j))]
    out_shape = [jax.ShapeDtypeStruct((m, n), jnp.float32)]
    if emit_h:
        out_specs.append(pl.BlockSpec((tm, k), lambda i, j: (i, 0)))
        out_shape.append(jax.ShapeDtypeStruct((m, k), jnp.bfloat16))
    res = pl.pallas_call(
        functools.partial(_mm_norm_kernel, emit_h=emit_h),
        grid=(m // tm, n // tn),
        in_specs=[pl.BlockSpec((tm, k), lambda i, j: (i, 0)),
                  pl.BlockSpec((1, k), lambda i, j: (0, 0)),
                  mod, mod,
                  pl.BlockSpec((k, tn), lambda i, j: (0, j))],
        out_specs=out_specs,
        out_shape=out_shape,
        scratch_shapes=[pltpu.VMEM((tm, k), jnp.bfloat16)],
        compiler_params=pltpu.CompilerParams(dimension_semantics=("parallel", "arbitrary"),
                                             vmem_limit_bytes=VMEM_LIMIT_BYTES),
        name="mm_norm",
    )(x, norm_g.astype(jnp.float32).reshape(1, k), scale, shift, w)
    return res if emit_h else res[0]


def _mm_pair_res_kernel(a_ref, b_ref, wa_ref, wb_ref, r_ref, g_ref, o_ref):
    y = (jnp.dot(a_ref[...].astype(jnp.bfloat16), wa_ref[...].astype(jnp.bfloat16), preferred_element_type=jnp.float32)
         + jnp.dot(b_ref[...].astype(jnp.bfloat16), wb_ref[...].astype(jnp.bfloat16), preferred_element_type=jnp.float32))
    o_ref[...] = r_ref[...] + g_ref[0] * y


def mm_pair_res(a, b, w, resid, gate):
    m, kh = a.shape
    n = w.shape[1]
    tm, tn = min(MM_ROW_TILE, m), min(MM_COL_TILE, n)
    nb = gate.shape[0]
    assert m % tm == 0 and n % tn == 0 and (m // nb) % tm == 0 and w.shape[0] == 2 * kh
    tiles_per_mod = (m // nb) // tm
    return pl.pallas_call(
        _mm_pair_res_kernel,
        grid=(m // tm, n // tn),
        in_specs=[pl.BlockSpec((tm, kh), lambda i, j: (i, 0)),
                  pl.BlockSpec((tm, kh), lambda i, j: (i, 0)),
                  pl.BlockSpec((kh, tn), lambda i, j: (0, j)),
                  pl.BlockSpec((kh, tn), lambda i, j: (1, j)),
                  pl.BlockSpec((tm, tn), lambda i, j: (i, j)),
                  pl.BlockSpec((1, 1, tn), lambda i, j: (i // tiles_per_mod, 0, j))],
        out_specs=pl.BlockSpec((tm, tn), lambda i, j: (i, j)),
        out_shape=jax.ShapeDtypeStruct((m, n), jnp.float32),
        compiler_params=pltpu.CompilerParams(dimension_semantics=("parallel", "arbitrary"),
                                             vmem_limit_bytes=VMEM_LIMIT_BYTES),
        name="mm_pair_res",
    )(a, b, w, w, resid, gate)


def adaln(cond, w, b):
    m = jax.nn.silu(cond) @ w + b
    return jnp.split(m[:, None, :], 6, axis=-1)


def hyena_filters(l, w1, b1, f1, w2, b2, f2, w3, decay):
    t = jnp.linspace(0.0, 1.0, l, dtype=jnp.float32)[:, None]
    bands = (HY_POS_EMB - 1) // 2
    w_ang = 2.0 * math.pi * jnp.arange(l, dtype=jnp.float32)[:, None] / l
    freqs = jnp.linspace(1e-4, bands - 1, bands, dtype=jnp.float32)[None, :]
    z = jnp.concatenate([t, jnp.cos(freqs * w_ang), -jnp.sin(freqs * w_ang)], axis=-1)
    h = jnp.sin(f1 * (z @ w1 + b1))
    h = jnp.sin(f2 * (h @ w2 + b2))
    h = (h @ w3).reshape(l, 2, HY_ORDER, HY_WIDTH).astype(jnp.float32)
    h = h * jnp.exp(-t.reshape(l, 1, 1, 1) * jnp.abs(decay.astype(jnp.float32)))
    h_f, h_b = h[:, 0], h[:, 1]
    zero = jnp.zeros((1, HY_ORDER, HY_WIDTH), jnp.float32)
    return jnp.concatenate([h_f, zero, h_b[1:][::-1]], axis=0)


def _split_bf16(x):
    hi = x.astype(jnp.bfloat16)
    lo = (x - hi.astype(jnp.float32)).astype(jnp.bfloat16)
    return hi, lo


def _dot3(a_hi, a_lo, x):
    m = a_hi.shape[0]
    xh, xl = _split_bf16(x)
    r = jnp.dot(jnp.concatenate([a_hi, a_lo], axis=0), xh, preferred_element_type=jnp.float32)
    return r[:m] + r[m:] + jnp.dot(a_hi, xl, preferred_element_type=jnp.float32)


HY_STEP_ROWS = 1024


def hyena_factors(l):
    n = 2 * l
    n2 = 64 if n >= 8192 else 16
    return n // n2, n2


def hyena_tables(l):
    n = 2 * l
    n1, n2 = hyena_factors(l)
    t = (n2 * jnp.arange(n1 // 2)[None, None, :] + jnp.arange(n2)[:, None, None])
    k1 = jnp.arange(n1)[None, :, None]
    ang = (2.0 * math.pi / n) * ((t * k1) % n).astype(jnp.float32)
    fa = jnp.concatenate([jnp.cos(ang), -jnp.sin(ang)], axis=1)
    fc = jnp.swapaxes(fa, 1, 2) / n
    a2 = (2.0 * math.pi / n2) * ((jnp.arange(n2)[:, None] * jnp.arange(n2)[None, :]) % n2).astype(jnp.float32)
    c, s = jnp.cos(a2), jnp.sin(a2)
    fb = jnp.concatenate([jnp.concatenate([c, s], axis=1), jnp.concatenate([-s, c], axis=1)], axis=0)
    fbi = jnp.concatenate([jnp.concatenate([c, -s], axis=1), jnp.concatenate([s, c], axis=1)], axis=0)
    return tuple(_split_bf16(x) for x in (fa, fc, fb, fbi))


def hyena_spectrum(filt, l):
    n1, n2 = hyena_factors(l)
    kf = jnp.fft.fft(filt, axis=0)
    kf = jnp.stack([jnp.real(kf), jnp.imag(kf)], axis=0).astype(jnp.float32)
    kf = kf.reshape(2, n2, n1, HY_ORDER, HY_WIDTH)
    return jnp.transpose(kf, (3, 0, 2, 1, 4))


def _hy_stage_a_kernel(z_ref, fh_ref, fl_ref, o_ref):
    for i in range(z_ref.shape[0]):
        o_ref[i] = _dot3(fh_ref[0], fl_ref[0], z_ref[i])


def _hy_stage_b_kernel(s_ref, k_ref, fh_ref, fl_ref, gh_ref, gl_ref, o_ref):
    n2 = s_ref.shape[3]
    kr, ki = k_ref[0, 0], k_ref[1, 0]
    for i in range(s_ref.shape[0]):
        y = jnp.concatenate([s_ref[i, 0, 0], s_ref[i, 1, 0]], axis=0)
        z = _dot3(fh_ref[...], fl_ref[...], y)
        zr, zi = z[:n2], z[n2:]
        p = jnp.concatenate([zr * kr - zi * ki, zr * ki + zi * kr], axis=0)
        q = _dot3(gh_ref[...], gl_ref[...], p)
        o_ref[i, 0, 0] = q[:n2]
        o_ref[i, 1, 0] = q[n2:]


def _hy_stage_c_kernel(q_ref, fh_ref, fl_ref, z_ref, g_ref, b_ref, o_ref):
    for i in range(q_ref.shape[0]):
        y = _dot3(fh_ref[0], fl_ref[0], q_ref[i])
        o_ref[i] = g_ref[i] * (y + z_ref[i] * b_ref[...])


def hyena_long_conv(z, z_col, gate, gate_col, bias_o, spec_o, tables, l):
    (fah, fal), (fch, fcl), (fbh, fbl), (fgh, fgl) = tables
    b = z.shape[0]
    c = HY_WIDTH
    n1, n2 = hyena_factors(l)
    h1 = n1 // 2
    bb = max(1, min(b, HY_STEP_ROWS // n1))
    assert b % bb == 0
    cp = pltpu.CompilerParams(dimension_semantics=("parallel", "parallel"), vmem_limit_bytes=VMEM_LIMIT_BYTES)
    zblocks = z.shape[-1] // c
    gblocks = gate.shape[-1] // c
    zv = z.reshape(b, h1, n2 * z.shape[-1])
    gv = gate.reshape(b, h1, n2 * gate.shape[-1])
    s1 = pl.pallas_call(
        _hy_stage_a_kernel,
        grid=(b // bb, n2),
        in_specs=[pl.BlockSpec((bb, h1, c), lambda i, j: (i, 0, j * zblocks + z_col)),
                  pl.BlockSpec((1, 2 * n1, h1), lambda i, j: (j, 0, 0)),
                  pl.BlockSpec((1, 2 * n1, h1), lambda i, j: (j, 0, 0))],
        out_specs=pl.BlockSpec((bb, 2 * n1, c), lambda i, j: (i, 0, j)),
        out_shape=jax.ShapeDtypeStruct((b, 2 * n1, n2 * c), jnp.float32),
        compiler_params=cp, name="hyena_stage_a",
    )(zv, fah, fal)
    s1 = s1.reshape(b, 2, n1, n2, c)
    mat = pl.BlockSpec((2 * n2, 2 * n2), lambda k, i: (0, 0))
    q = pl.pallas_call(
        _hy_stage_b_kernel,
        grid=(n1, b // bb),
        in_specs=[pl.BlockSpec((bb, 2, 1, n2, c), lambda k, i: (i, 0, k, 0, 0)),
                  pl.BlockSpec((2, 1, n2, c), lambda k, i: (0, k, 0, 0)),
                  mat, mat, mat, mat],
        out_specs=pl.BlockSpec((bb, 2, 1, n2, c), lambda k, i: (i, 0, k, 0, 0)),
        out_shape=jax.ShapeDtypeStruct((b, 2, n1, n2, c), jnp.float32),
        compiler_params=cp, name="hyena_stage_b",
    )(s1, spec_o, fbh, fbl, fgh, fgl)
    q = q.reshape(b, 2 * n1, n2 * c)
    out = pl.pallas_call(
        _hy_stage_c_kernel,
        grid=(b // bb, n2),
        in_specs=[pl.BlockSpec((bb, 2 * n1, c), lambda i, j: (i, 0, j)),
                  pl.BlockSpec((1, h1, 2 * n1), lambda i, j: (j, 0, 0)),
                  pl.BlockSpec((1, h1, 2 * n1), lambda i, j: (j, 0, 0)),
                  pl.BlockSpec((bb, h1, c), lambda i, j: (i, 0, j * zblocks + z_col)),
                  pl.BlockSpec((bb, h1, c), lambda i, j: (i, 0, j * gblocks + gate_col)),
                  pl.BlockSpec((1, c), lambda i, j: (0, 0))],
        out_specs=pl.BlockSpec((bb, h1, c), lambda i, j: (i, 0, j)),
        out_shape=jax.ShapeDtypeStruct((b, h1, n2 * c), jnp.float32),
        compiler_params=cp, name="hyena_stage_c",
    )(q, fch, fcl, zv, gv, bias_o.astype(jnp.float32).reshape(1, c))
    return out.reshape(b, l, c)


def hyena(u, conv_w, conv_b, filt, bias):
    b, l, _ = u.shape
    up = jnp.pad(u, ((0, 0), (1, 1), (0, 0)))
    u = up[:, :-2] * conv_w[0] + up[:, 1:-1] * conv_w[1] + up[:, 2:] * conv_w[2] + conv_b
    tables = hyena_tables(l)
    spec = hyena_spectrum(hyena_filters(l, *filt), l)
    z, z_col = u, 0
    for o in range(HY_ORDER):
        z = hyena_long_conv(z, z_col, u, 1 + o, bias[o], spec[o], tables, l)
        z_col = 0
    return z


S5_TILE_GROUPS = 8
S5_TILES = S5_GROUPS // S5_TILE_GROUPS
S5_TILE_IN = S5_TILE_GROUPS * S5_GROUP
S5_TILE_STATE = S5_TILE_GROUPS * S5_STATE
S5_ROWS_PER_STEP = 2048


def s5_prepare(lam_re, lam_im, log_dt, b_re, b_im, c_re, c_im):
    f32 = jnp.float32
    lam_re, lam_im = lam_re.astype(f32), lam_im.astype(f32)
    dt = jnp.exp(log_dt.astype(f32))[..., None]
    mag = jnp.exp(lam_re * dt)
    ar, ai = mag * jnp.cos(lam_im * dt), mag * jnp.sin(lam_im * dt)
    den = lam_re * lam_re + lam_im * lam_im
    cr = ((ar - 1.0) * lam_re + ai * lam_im) / den
    ci = (ai * lam_re - (ar - 1.0) * lam_im) / den
    bbr = cr[..., None] * b_re - ci[..., None] * b_im
    bbi = cr[..., None] * b_im + ci[..., None] * b_re
    eye = jnp.eye(S5_TILE_GROUPS, dtype=f32)

    def tile_in(bb):
        x = bb.reshape(2, S5_TILES, S5_TILE_GROUPS, S5_STATE, S5_GROUP)
        x = jnp.einsum('dtgnp,gh->dtgphn', x, eye)
        return x.reshape(2, S5_TILES, S5_TILE_IN, S5_TILE_STATE)

    def tile_out(cc):
        x = cc.astype(f32).reshape(2, S5_TILES, S5_TILE_GROUPS, S5_GROUP, S5_STATE)
        x = jnp.einsum('dtgpn,gh->dtgnhp', x, eye)
        return x.reshape(2, S5_TILES, S5_TILE_STATE, S5_TILE_IN)

    win = jnp.concatenate([tile_in(bbr), tile_in(bbi)], axis=-1).astype(jnp.bfloat16)
    wout = jnp.concatenate([tile_out(c_re), -tile_out(c_im)], axis=-2).astype(jnp.bfloat16)
    a = jnp.stack([ar.reshape(2, S5_TILES, S5_TILE_STATE), ai.reshape(2, S5_TILES, S5_TILE_STATE)], axis=2)
    return win, wout, a


def _s5_scan_kernel(u_ref, win_ref, wout_ref, a_ref, h0_ref, y_ref, fin_ref, bu_ref, st_ref, *, batch):
    d = pl.program_id(0)
    c = pl.program_id(2)
    n_c = pl.num_programs(2)
    ns = S5_TILE_STATE
    steps = u_ref.shape[0] // batch

    @pl.when(c == 0)
    def _():
        st_ref[...] = h0_ref[0, 0]

    bu_ref[...] = jnp.dot(u_ref[...].astype(jnp.bfloat16), win_ref[0, 0], preferred_element_type=jnp.float32)
    ar = jnp.broadcast_to(a_ref[0, 0, 0:1, :], (batch, ns))
    ai = jnp.broadcast_to(a_ref[0, 0, 1:2, :], (batch, ns))

    def step(t, carry):
        hr, hi = carry
        te = jnp.where(d == 0, t, steps - 1 - t)
        r0 = pl.multiple_of(te * batch, batch)
        nr = ar * hr - ai * hi + bu_ref[pl.ds(r0, batch), 0:ns]
        ni = ar * hi + ai * hr + bu_ref[pl.ds(r0, batch), ns:2 * ns]
        bu_ref[pl.ds(r0, batch), 0:ns] = nr
        bu_ref[pl.ds(r0, batch), ns:2 * ns] = ni
        return nr, ni

    hr, hi = lax.fori_loop(0, steps, step, (st_ref[:, 0:ns], st_ref[:, ns:2 * ns]), unroll=4)
    st_ref[:, 0:ns] = hr
    st_ref[:, ns:2 * ns] = hi
    y_ref[0] = jnp.dot(bu_ref[...].astype(jnp.bfloat16), wout_ref[0, 0], preferred_element_type=jnp.float32)

    @pl.when(c == n_c - 1)
    def _():
        fin_ref[0, 0] = st_ref[...]


def s5_scan(u_tm, win, wout, a, h0, batch):
    rows = u_tm.shape[0]
    r = min(S5_ROWS_PER_STEP, rows)
    assert rows % r == 0 and r % batch == 0
    n_c = rows // r

    def chunk(d, c):
        return c + d * (n_c - 1 - 2 * c)

    return pl.pallas_call(
        functools.partial(_s5_scan_kernel, batch=batch),
        grid=(2, S5_TILES, n_c),
        in_specs=[pl.BlockSpec((r, S5_TILE_IN), lambda d, j, c: (chunk(d, c), j)),
                  pl.BlockSpec((1, 1, S5_TILE_IN, 2 * S5_TILE_STATE), lambda d, j, c: (d, j, 0, 0)),
                  pl.BlockSpec((1, 1, 2 * S5_TILE_STATE, S5_TILE_IN), lambda d, j, c: (d, j, 0, 0)),
                  pl.BlockSpec((1, 1, 2, S5_TILE_STATE), lambda d, j, c: (d, j, 0, 0)),
                  pl.BlockSpec((1, 1, batch, 2 * S5_TILE_STATE), lambda d, j, c: (d, j, 0, 0))],
        out_specs=[pl.BlockSpec((1, r, S5_TILE_IN), lambda d, j, c: (d, chunk(d, c), j)),
                   pl.BlockSpec((1, 1, batch, 2 * S5_TILE_STATE), lambda d, j, c: (d, j, 0, 0))],
        out_shape=[jax.ShapeDtypeStruct((2, rows, S5_WIDTH), jnp.float32),
                   jax.ShapeDtypeStruct((2, S5_TILES, batch, 2 * S5_TILE_STATE), jnp.float32)],
        scratch_shapes=[pltpu.VMEM((r, 2 * S5_TILE_STATE), jnp.float32),
                        pltpu.VMEM((batch, 2 * S5_TILE_STATE), jnp.float32)],
        compiler_params=pltpu.CompilerParams(dimension_semantics=("parallel", "parallel", "arbitrary"),
                                             vmem_limit_bytes=VMEM_LIMIT_BYTES),
        name="s5_scan",
    )(u_tm, win, wout, a, h0)


def _gelu_tanh(x):
    return 0.5 * x * (1.0 + jnp.tanh(math.sqrt(2.0 / math.pi) * (x + 0.044715 * (x * x * x))))


def _s5_glu_kernel(u_ref, y_ref, d_ref, w_ref, b_ref, o_ref):
    y = d_ref[...] * u_ref[...] + y_ref[0] + y_ref[1]
    y = _gelu_tanh(y)
    z = jnp.dot(y.astype(jnp.bfloat16), w_ref[...], preferred_element_type=jnp.float32) + b_ref[...]
    o_ref[...] = y * (1.0 / (1.0 + jnp.exp(-z)))


def s5_glu(u_tm, y, d_skip, glu_w_bf16, glu_b, tr=1024):
    rows, w = u_tm.shape
    tr = min(tr, rows)
    assert rows % tr == 0
    return pl.pallas_call(
        _s5_glu_kernel,
        grid=(rows // tr,),
        in_specs=[pl.BlockSpec((tr, w), lambda i: (i, 0)),
                  pl.BlockSpec((2, tr, w), lambda i: (0, i, 0)),
                  pl.BlockSpec((1, w), lambda i: (0, 0)),
                  pl.BlockSpec((w, w), lambda i: (0, 0)),
                  pl.BlockSpec((1, w), lambda i: (0, 0))],
        out_specs=pl.BlockSpec((tr, w), lambda i: (i, 0)),
        out_shape=jax.ShapeDtypeStruct((rows, w), jnp.float32),
        compiler_params=pltpu.CompilerParams(dimension_semantics=("parallel",),
                                             vmem_limit_bytes=VMEM_LIMIT_BYTES),
        name="s5_glu",
    )(u_tm, y, d_skip.reshape(1, w), glu_w_bf16, glu_b.reshape(1, w))


def s5_mixer(u, h0_re, h0_im, prep, d_skip, glu_w, glu_b):
    win, wout, a = prep
    b, l, w = u.shape
    u_tm = jnp.swapaxes(u, 0, 1).reshape(l * b, w)
    if h0_re is None:
        h0 = jnp.zeros((2, S5_TILES, b, 2 * S5_TILE_STATE), jnp.float32)
    else:
        def tiles(h):
            return jnp.transpose(h.astype(jnp.float32).reshape(b, 2, S5_TILES, S5_TILE_STATE), (1, 2, 0, 3))
        h0 = jnp.concatenate([tiles(h0_re), tiles(h0_im)], axis=-1)
    y, fin = s5_scan(u_tm, win, wout, a, h0, b)
    out_tm = s5_glu(u_tm, y, d_skip, glu_w.astype(jnp.bfloat16), glu_b)
    out = jnp.swapaxes(out_tm.reshape(l, b, w), 0, 1)

    def untile(f):
        return jnp.transpose(f, (2, 0, 1, 3)).reshape(b, 2, S5_GROUPS, S5_STATE)
    return out, untile(fin[..., :S5_TILE_STATE]), untile(fin[..., S5_TILE_STATE:])


def _half_mean_matrix():
    i = lax.broadcasted_iota(jnp.int32, (LANES, LANES), 0) // DIFF_QK_DIM
    j = lax.broadcasted_iota(jnp.int32, (LANES, LANES), 1) // DIFF_QK_DIM
    return jnp.where(i == j, 1.0 / DIFF_QK_DIM, 0.0).astype(jnp.bfloat16)


def _rms_groups(x, gain, avg):
    xx = x * x
    hi = xx.astype(jnp.bfloat16)
    lo = (xx - hi.astype(jnp.float32)).astype(jnp.bfloat16)
    ms = (jnp.dot(hi, avg, preferred_element_type=jnp.float32)
          + jnp.dot(lo, avg, preferred_element_type=jnp.float32))
    return x * lax.rsqrt(ms + NORM_EPS) * gain


def _rope_lanes(y, cos, sin_signed, first_half):
    rot = jnp.where(first_half, pltpu.roll(y, LANES - 16, 1), pltpu.roll(y, 16, 1))
    return y * cos + rot * sin_signed


def _diff_prep_kernel(q_ref, k_ref, qg_ref, kg_ref, cos_ref, sin_ref, qo_ref, ko_ref, kn_ref, *, use_rope):
    avg = _half_mean_matrix()
    lane = lax.broadcasted_iota(jnp.int32, (1, LANES), 1)
    first_half = (lane % 32) < 16
    scale = DIFF_QK_DIM ** -0.5 * math.log2(math.e)
    for h in range(DIFF_HEADS):
        cols = slice(h * LANES, (h + 1) * LANES)
        qn = _rms_groups(q_ref[:, cols], qg_ref[...], avg)
        kn = _rms_groups(k_ref[:, cols], kg_ref[...], avg)
        kn_ref[:, cols] = kn
        if use_rope:
            qn = _rope_lanes(qn, cos_ref[...], sin_ref[...], first_half)
            kn = _rope_lanes(kn, cos_ref[...], sin_ref[...], first_half)
        qo_ref[:, cols] = (qn * scale).astype(jnp.bfloat16)
        ko_ref[:, cols] = kn.astype(jnp.bfloat16)


def diff_prep(p, q_g, k_g, cos, sin_signed, seq_len, use_rope, tr=512):
    rows = p.shape[0]
    tr = min(tr, seq_len)
    assert seq_len % tr == 0
    w = DIFF_HEADS * LANES
    per_seq = seq_len // tr
    g2 = lambda g: jnp.tile(g.astype(jnp.float32), 2).reshape(1, LANES)
    return pl.pallas_call(
        functools.partial(_diff_prep_kernel, use_rope=use_rope),
        grid=(rows // tr,),
        in_specs=[pl.BlockSpec((tr, w), lambda i: (i, 1)),
                  pl.BlockSpec((tr, w), lambda i: (i, 2)),
                  pl.BlockSpec((1, LANES), lambda i: (0, 0)),
                  pl.BlockSpec((1, LANES), lambda i: (0, 0)),
                  pl.BlockSpec((tr, LANES), lambda i: (i % per_seq, 0)),
                  pl.BlockSpec((tr, LANES), lambda i: (i % per_seq, 0))],
        out_specs=[pl.BlockSpec((tr, w), lambda i: (i, 0))] * 3,
        out_shape=[jax.ShapeDtypeStruct((rows, w), jnp.bfloat16), jax.ShapeDtypeStruct((rows, w), jnp.bfloat16),
                   jax.ShapeDtypeStruct((rows, w), jnp.float32)],
        compiler_params=pltpu.CompilerParams(dimension_semantics=("parallel",),
                                             vmem_limit_bytes=VMEM_LIMIT_BYTES),
        name="diff_prep",
    )(p, p, g2(q_g), g2(k_g), cos, sin_signed)


def rope_tables(l):
    t = jnp.arange(l)
    row = (t // GRID_W).astype(jnp.float32)
    col = (t % GRID_W).astype(jnp.float32)
    nf = DIFF_QK_DIM // 4
    inv = ROPE_BASE ** (-jnp.arange(nf, dtype=jnp.float32) / nf)
    ang = jnp.stack([row[:, None] * inv, col[:, None] * inv], axis=1)
    ang = jnp.stack([ang, ang], axis=2).reshape(l, DIFF_QK_DIM)
    sign = jnp.where((jnp.arange(DIFF_QK_DIM) % 32) < 16, -1.0, 1.0)
    return jnp.tile(jnp.cos(ang), (1, 2)), jnp.tile(jnp.sin(ang) * sign, (1, 2))


def _diff_attn_kernel(*refs, has_ctx):
    if has_ctx:
        q_ref, k_ref, v_ref, ck_ref, cv_ref, lam_ref, g_ref, o_ref = refs
    else:
        q_ref, k_ref, v_ref, lam_ref, g_ref, o_ref = refs
    q = q_ref[...]
    lane = lax.broadcasted_iota(jnp.int32, (1, LANES), 1)
    zero = jnp.zeros_like(q)
    qs = (jnp.where(lane < DIFF_QK_DIM, q, zero), jnp.where(lane >= DIFF_QK_DIM, q, zero))
    nt = (((1,), (1,)), ((), ()))
    k = k_ref[...]
    ck = ck_ref[...].astype(jnp.bfloat16) if has_ctx else None
    lam = lam_ref[0:1, 0:1]
    w_self, w_ctx = None, None
    for i in range(2):
        s = lax.dot_general(qs[i], k, nt, preferred_element_type=jnp.float32)
        m = jnp.max(s, axis=-1, keepdims=True)
        if has_ctx:
            sc = lax.dot_general(qs[i], ck, nt, preferred_element_type=jnp.float32)
            m = jnp.maximum(m, jnp.max(sc, axis=-1, keepdims=True))
        p = jnp.exp2(s - m)
        l = jnp.sum(p, axis=-1, keepdims=True)
        if has_ctx:
            pc = jnp.exp2(sc - m)
            l = l + jnp.sum(pc, axis=-1, keepdims=True)
        coef = 1.0 / l if i == 0 else -lam / l
        w_self = p * coef if i == 0 else w_self + p * coef
        if has_ctx:
            w_ctx = pc * coef if i == 0 else w_ctx + pc * coef
    o = jnp.dot(w_self.astype(jnp.bfloat16), v_ref[...].astype(jnp.bfloat16), preferred_element_type=jnp.float32)
    if has_ctx:
        o = o + jnp.dot(w_ctx.astype(jnp.bfloat16), cv_ref[...].astype(jnp.bfloat16),
                        preferred_element_type=jnp.float32)
    ms = jnp.mean(o * o, axis=-1, keepdims=True)
    o_ref[...] = o * lax.rsqrt(ms + NORM_EPS) * g_ref[...]


def diff_attention(q_bf16, k_bf16, p, ctx_k, ctx_v, lam, gain, batch, seq_len, tq=256):
    tq = min(tq, seq_len)
    nq = seq_len // tq
    has_ctx = ctx_k is not None
    v_col0 = 3 * DIFF_HEADS
    in_specs = [pl.BlockSpec((tq, LANES), lambda b, h, i: (b * nq + i, h)),
                pl.BlockSpec((seq_len, LANES), lambda b, h, i: (b, h)),
                pl.BlockSpec((seq_len, LANES), lambda b, h, i: (b, v_col0 + h))]
    args = [q_bf16, k_bf16, p]
    if has_ctx:
        lc = ctx_k.shape[0] // batch
        in_specs += [pl.BlockSpec((lc, LANES), lambda b, h, i: (b, h))] * 2
        args += [ctx_k, ctx_v]
    in_specs += [pl.BlockSpec((1, LANES), lambda b, h, i: (0, 0))] * 2
    args += [jnp.broadcast_to(lam.astype(jnp.float32), (1, LANES)), gain.astype(jnp.float32).reshape(1, LANES)]
    return pl.pallas_call(
        functools.partial(_diff_attn_kernel, has_ctx=has_ctx),
        grid=(batch, DIFF_HEADS, nq),
        in_specs=in_specs,
        out_specs=pl.BlockSpec((tq, LANES), lambda b, h, i: (b * nq + i, h)),
        out_shape=jax.ShapeDtypeStruct((batch * seq_len, DIFF_HEADS * LANES), jnp.float32),
        compiler_params=pltpu.CompilerParams(dimension_semantics=("parallel", "parallel", "arbitrary"),
                                             vmem_limit_bytes=VMEM_LIMIT_BYTES),
        name="diff_attention",
    )(*args)


NA_Q_ROWS = 8
NA_WIN_ROWS = 16


def _na_prep_kernel(q_ref, k_ref, qg_ref, kg_ref, qo_ref, ko_ref, kn_ref):
    avg = jnp.full((LANES, LANES), 1.0 / HEAD_DIM, jnp.bfloat16)
    for h in range(NA_HEADS):
        cols = slice(h * LANES, (h + 1) * LANES)
        qo_ref[:, cols] = _rms_groups(q_ref[:, cols], qg_ref[...], avg).astype(jnp.bfloat16)
        kn = _rms_groups(k_ref[:, cols], kg_ref[...], avg)
        kn_ref[:, cols] = kn
        ko_ref[:, cols] = kn.astype(jnp.bfloat16)


def na_prep(p, q_g, k_g, tr=512):
    rows = p.shape[0]
    tr = min(tr, rows)
    w = NA_HEADS * LANES
    g1 = lambda g: g.astype(jnp.float32).reshape(1, LANES)
    return pl.pallas_call(
        _na_prep_kernel,
        grid=(rows // tr,),
        in_specs=[pl.BlockSpec((tr, w), lambda i: (i, 0)),
                  pl.BlockSpec((tr, w), lambda i: (i, 1)),
                  pl.BlockSpec((1, LANES), lambda i: (0, 0)),
                  pl.BlockSpec((1, LANES), lambda i: (0, 0))],
        out_specs=[pl.BlockSpec((tr, w), lambda i: (i, 0))] * 3,
        out_shape=[jax.ShapeDtypeStruct((rows, w), jnp.bfloat16), jax.ShapeDtypeStruct((rows, w), jnp.bfloat16),
                   jax.ShapeDtypeStruct((rows, w), jnp.float32)],
        compiler_params=pltpu.CompilerParams(dimension_semantics=("parallel",),
                                             vmem_limit_bytes=VMEM_LIMIT_BYTES),
        name="na_prep",
    )(p, p, g1(q_g), g1(k_g))


def na_bias_tables(rpb, rows):
    nblk = rows // NA_Q_ROWS
    cq = jnp.arange(GRID_W)[:, None]
    ck = jnp.arange(GRID_W)[None, :]
    col_start = jnp.clip(cq - NA_KW // 2, 0, GRID_W - NA_KW)
    col_ok = (ck >= col_start) & (ck < col_start + NA_KW)
    cidx = jnp.clip(ck - cq + NA_KW - 1, 0, 2 * NA_KW - 2)
    tabs = []
    for i in (0, 1, nblk - 1):
        base = min(max(NA_Q_ROWS * i - NA_KH // 2, 0), rows - NA_WIN_ROWS)
        r = NA_Q_ROWS * i + jnp.arange(NA_Q_ROWS)[:, None]
        rk = base + jnp.arange(NA_WIN_ROWS)[None, :]
        r0 = jnp.clip(r - NA_KH // 2, 0, rows - NA_KH)
        row_ok = (rk >= r0) & (rk < r0 + NA_KH)
        ridx = jnp.clip(rk - r + NA_KH - 1, 0, 2 * NA_KH - 2)
        b = rpb[:, ridx][:, :, :, cidx]
        ok = row_ok[:, :, None, None] & col_ok[None, None, :, :]
        b = jnp.where(ok[None], b.astype(jnp.float32), NEG_INF)
        tabs.append(jnp.transpose(b, (0, 1, 3, 2, 4)).reshape(NA_HEADS, NA_Q_ROWS * GRID_W, NA_WIN_ROWS * GRID_W))
    return jnp.stack(tabs, axis=0)


def _na_attn_kernel(*refs, windowed, rows):
    nt = (((1,), (1,)), ((), ()))
    scale = HEAD_DIM ** -0.5
    if windowed:
        q_ref, k_ref, v_ref, b_ref, ck_ref, cv_ref, o_ref = refs
        i = pl.program_id(2)
        base = jnp.clip(NA_Q_ROWS * i - NA_KH // 2, 0, rows - NA_WIN_ROWS)
        k0 = pl.multiple_of(base * GRID_W, GRID_W)
        nk = NA_WIN_ROWS * GRID_W
        q = q_ref[...]
        s = lax.dot_general(q, k_ref[pl.ds(k0, nk), :], nt, preferred_element_type=jnp.float32) * scale + b_ref[0, 0]
        sc = lax.dot_general(q, ck_ref[...].astype(jnp.bfloat16), nt, preferred_element_type=jnp.float32) * scale
        m = jnp.maximum(jnp.max(s, axis=-1, keepdims=True), jnp.max(sc, axis=-1, keepdims=True))
        p = jnp.exp(s - m)
        pc = jnp.exp(sc - m)
        l = jnp.sum(p, axis=-1, keepdims=True) + jnp.sum(pc, axis=-1, keepdims=True)
        o = (jnp.dot(p.astype(jnp.bfloat16), v_ref[pl.ds(k0, nk), :].astype(jnp.bfloat16),
                     preferred_element_type=jnp.float32)
             + jnp.dot(pc.astype(jnp.bfloat16), cv_ref[...].astype(jnp.bfloat16), preferred_element_type=jnp.float32))
    else:
        q_ref, k_ref, v_ref, o_ref = refs
        s = lax.dot_general(q_ref[...], k_ref[...], nt, preferred_element_type=jnp.float32) * scale
        m = jnp.max(s, axis=-1, keepdims=True)
        p = jnp.exp(s - m)
        l = jnp.sum(p, axis=-1, keepdims=True)
        o = jnp.dot(p.astype(jnp.bfloat16), v_ref[...].astype(jnp.bfloat16), preferred_element_type=jnp.float32)
    o_ref[...] = o / l


def na_attention(q_bf16, k_bf16, p, bias_tabs, ctx_k, ctx_v, batch, seq_len):
    windowed = bias_tabs is not None
    v_col0 = 2 * NA_HEADS
    rows = seq_len // GRID_W
    if windowed:
        tq = NA_Q_ROWS * GRID_W
        nq = seq_len // tq
        lc = ctx_k.shape[0] // batch
        nk = NA_WIN_ROWS * GRID_W
        in_specs = [pl.BlockSpec((tq, LANES), lambda b, h, i: (b * nq + i, h)),
                    pl.BlockSpec((seq_len, LANES), lambda b, h, i: (b, h)),
                    pl.BlockSpec((seq_len, LANES), lambda b, h, i: (b, v_col0 + h)),
                    pl.BlockSpec((1, 1, tq, nk),
                                 lambda b, h, i: (jnp.where(i == 0, 0, jnp.where(i == nq - 1, 2, 1)), h, 0, 0)),
                    pl.BlockSpec((lc, LANES), lambda b, h, i: (b, h)),
                    pl.BlockSpec((lc, LANES), lambda b, h, i: (b, h))]
        args = [q_bf16, k_bf16, p, bias_tabs, ctx_k, ctx_v]
    else:
        tq = seq_len
        nq = 1
        in_specs = [pl.BlockSpec((tq, LANES), lambda b, h, i: (b, h)),
                    pl.BlockSpec((seq_len, LANES), lambda b, h, i: (b, h)),
                    pl.BlockSpec((seq_len, LANES), lambda b, h, i: (b, v_col0 + h))]
        args = [q_bf16, k_bf16, p]
    return pl.pallas_call(
        functools.partial(_na_attn_kernel, windowed=windowed, rows=rows),
        grid=(batch, NA_HEADS, nq),
        in_specs=in_specs,
        out_specs=pl.BlockSpec((tq, LANES), lambda b, h, i: (b * nq + i, h)),
        out_shape=jax.ShapeDtypeStruct((batch * seq_len, NA_HEADS * LANES), jnp.float32),
        compiler_params=pltpu.CompilerParams(dimension_semantics=("parallel", "parallel", "arbitrary"),
                                             vmem_limit_bytes=VMEM_LIMIT_BYTES),
        name="na_attention",
    )(*args)


def even_mixer(x, mods, norm_g, b, l, ev, ctx_kv, bias_tabs):
    (w_in, w_out, q_g, k_g, conv_w, conv_b, w1, b1, f1, w2, b2, f2, w3, decay, bias) = ev
    p = mm_norm(x, norm_g, mods[0], mods[1], w_in)
    q, k, kn = na_prep(p, q_g, k_g)
    if ctx_kv is None:
        oa = na_attention(q, k, p, None, None, None, b, l)
    else:
        lc = ctx_kv[0].shape[1]
        oa = na_attention(q, k, p, bias_tabs, ctx_kv[0].reshape(b * lc, MIX_WIDTH),
                          ctx_kv[1].reshape(b * lc, MIX_WIDTH), b, l)
    hb = p[:, 3 * MIX_WIDTH:].reshape(b, l, 3 * HY_WIDTH)
    ob = hyena(hb, conv_w, conv_b, (w1, b1, f1, w2, b2, f2, w3, decay), bias)
    out = mm_pair_res(oa, ob.reshape(b * l, HY_WIDTH), w_out, x, mods[2])
    shp = (b, l, NA_HEADS, HEAD_DIM)
    return out, kn.reshape(shp), p[:, 2 * MIX_WIDTH:3 * MIX_WIDTH].reshape(shp)


def odd_mixer(x, mods, norm_g, b, l, od, lam_init, h0_re, h0_im, ctx_kv, rope):
    (w_in, w_out, s5_prep, d_skip, glu_w, glu_b, q_g, k_g, lq1, lk1, lq2, lk2, subln_g) = od
    p = mm_norm(x, norm_g, mods[0], mods[1], w_in)
    oc, fr, fi = s5_mixer(p[:, :S5_WIDTH].reshape(b, l, S5_WIDTH), h0_re, h0_im, s5_prep, d_skip, glu_w, glu_b)
    lam = (jnp.exp(jnp.sum(lq1 * lk1).astype(jnp.float32))
           - jnp.exp(jnp.sum(lq2 * lk2).astype(jnp.float32)) + lam_init)
    cos, sin_signed = rope
    q, k, kn = diff_prep(p, q_g, k_g, cos, sin_signed, l, use_rope=ctx_kv is not None)
    gain = subln_g.astype(jnp.float32) * (1.0 - lam_init)
    if ctx_kv is None:
        o = diff_attention(q, k, p, None, None, lam, gain, b, l)
    else:
        lc = ctx_kv[0].shape[1]
        o = diff_attention(q, k, p, ctx_kv[0].reshape(b * lc, MIX_WIDTH), ctx_kv[1].reshape(b * lc, MIX_WIDTH),
                           lam, gain, b, l)
    out = mm_pair_res(oc.reshape(b * l, S5_WIDTH), o, w_out, x, mods[2])
    kn = kn.reshape(b, l, DIFF_HEADS, 2, DIFF_QK_DIM)
    v = p[:, S5_WIDTH + 2 * MIX_WIDTH:].reshape(b, l, DIFF_HEADS, DIFF_V_DIM)
    return out, kn, v, fr, fi


PEER_TOKEN_TILE = 512
PEER_EXPERT_TILE = 1024
PEER_SUB_EXPERTS = 256


def _top16_rows(cur, iota):
    n = cur.shape[0]
    tops, idxs = [], []
    for _ in range(PEER_TOPK):
        m = jnp.max(cur, axis=0, keepdims=True)
        first = jnp.min(jnp.where(cur == m, iota, n), axis=0, keepdims=True)
        tops.append(m)
        idxs.append(first)
        cur = jnp.where(iota == first, -jnp.inf, cur)
    return tops, idxs


def _router_kernel(q_ref, keys_ref, m_ref, r1_ref, e0_ref, e1_ref, st_ref):
    tt = q_ref.shape[0]
    half = PEER_QUERY_DIM // 2
    for h in range(PEER_HEADS):
        for s in range(2):
            qs = q_ref[:, (2 * h + s) * half:(2 * h + s + 1) * half].astype(jnp.bfloat16)
            st = lax.dot_general(keys_ref[h, s], qs, (((1,), (1,)), ((), ())),
                                 preferred_element_type=jnp.float32)
            st_ref[2 * h + s] = st

    iota128 = lax.broadcasted_iota(jnp.int32, (PEER_N_KEYS, LANES), 0)
    iota8 = lax.broadcasted_iota(jnp.int32, (SUBLANES, LANES), 0)
    iota_cand = lax.broadcasted_iota(jnp.int32, (PEER_TOPK + SUBLANES * SUBLANES, LANES), 0)
    n_chunks = tt // LANES

    def route(h, lane0):
        s0 = st_ref[2 * h, :, pl.ds(lane0, LANES)]
        s1 = st_ref[2 * h + 1, :, pl.ds(lane0, LANES)]
        a, a_idx = _top16_rows(s0, iota128)
        b, b_idx = _top16_rows(s1, iota128)
        bmat = jnp.concatenate(b, axis=0)
        sums = [a[k] + bmat for k in range(PEER_TOPK)]
        cand = [sums[0], sums[1][:SUBLANES]]
        for k in range(2, SUBLANES):
            cand.append(jnp.where(iota8 < PEER_TOPK // (k + 1), sums[k][:SUBLANES], -jnp.inf))
        cand.append(jnp.concatenate(a[SUBLANES:], axis=0) + b[0])
        f, _ = _top16_rows(jnp.concatenate(cand, axis=0), iota_cand)
        tau = f[PEER_TOPK - 1]
        z = jnp.ones_like(f[0])
        for k in range(1, PEER_TOPK):
            z = z + jnp.exp(f[k] - f[0])
        m = jnp.zeros((PEER_N_KEYS, LANES), jnp.float32)
        r1 = jnp.full((PEER_N_KEYS, LANES), float(PEER_TOPK), jnp.float32)
        for k in range(PEER_TOPK):
            m_k = jnp.sum(jnp.where(sums[k] >= tau, 1.0, 0.0), axis=0, keepdims=True)
            m = jnp.where(iota128 == a_idx[k], m_k, m)
            r1 = jnp.where(iota128 == b_idx[k], float(k), r1)
        m_ref[h, :, pl.ds(lane0, LANES)] = m
        r1_ref[h, :, pl.ds(lane0, LANES)] = r1.astype(jnp.bfloat16)
        e0_ref[h, :, pl.ds(lane0, LANES)] = jnp.exp(s0 - a[0]) / z
        e1_ref[h, :, pl.ds(lane0, LANES)] = jnp.exp(s1 - b[0]).astype(jnp.bfloat16)

    def body(h, carry):
        for c in range(n_chunks):
            route(h, c * LANES)
        return carry

    lax.fori_loop(0, PEER_HEADS, body, 0)


def peer_router(q, keys_bf16, tt):
    t = q.shape[0]
    assert t % tt == 0 and tt % LANES == 0
    f32 = jnp.float32
    big = jax.ShapeDtypeStruct((PEER_HEADS, PEER_N_KEYS, t), f32)
    half = jax.ShapeDtypeStruct((PEER_HEADS, PEER_N_KEYS, t), jnp.bfloat16)
    blk = pl.BlockSpec((PEER_HEADS, PEER_N_KEYS, tt), lambda i: (0, 0, i))
    return pl.pallas_call(
        _router_kernel,
        grid=(t // tt,),
        in_specs=[pl.BlockSpec((tt, q.shape[1]), lambda i: (i, 0)),
                  pl.BlockSpec(keys_bf16.shape, lambda i: (0, 0, 0, 0))],
        out_specs=[blk, blk, blk, blk],
        out_shape=[big, half, big, half],
        scratch_shapes=[pltpu.VMEM((2 * PEER_HEADS, PEER_N_KEYS, tt), f32)],
        compiler_params=pltpu.CompilerParams(dimension_semantics=("parallel",),
                                             vmem_limit_bytes=VMEM_LIMIT_BYTES),
        name="peer_router",
    )(q, keys_bf16)


def _dense_kernel(xt_ref, u_ref, vt_ref, m_ref, r1_ref, e0_ref, e1_ref, res_ref, gate_ref,
                  o_ref, at_ref, wt_ref, acc_ref):
    e = pl.program_id(1)
    n_e = pl.num_programs(1)
    eb = u_ref.shape[0]
    tt = xt_ref.shape[1]
    n_chunks = tt // LANES
    n_sub = eb // PEER_SUB_EXPERTS
    i1_per_sub = PEER_SUB_EXPERTS // PEER_N_KEYS

    @pl.when(e == 0)
    def _():
        acc_ref[...] = jnp.zeros_like(acc_ref)

    at_ref[...] = jnp.dot(u_ref[...], xt_ref[...], preferred_element_type=jnp.float32)
    for sb in range(n_sub):
        for c in range(n_chunks):
            lanes = slice(c * LANES, (c + 1) * LANES)
            for k in range(i1_per_sub):
                i1 = sb * i1_per_sub + k
                zero = jnp.zeros((PEER_N_KEYS, LANES), jnp.bfloat16)
                g = zero
                for h in range(PEER_HEADS):
                    mrow = m_ref[h, i1:i1 + 1, lanes].astype(jnp.bfloat16)
                    e0row = e0_ref[h, i1:i1 + 1, lanes].astype(jnp.bfloat16)
                    g = g + jnp.where(r1_ref[h, :, lanes] < mrow, e0row * e1_ref[h, :, lanes], zero)
                r = slice(i1 * PEER_N_KEYS, (i1 + 1) * PEER_N_KEYS)
                wt_ref[r, lanes] = _gelu_tanh(at_ref[r, lanes]).astype(jnp.bfloat16) * g
    acc_ref[...] += jnp.dot(vt_ref[0], wt_ref[...], preferred_element_type=jnp.float32)

    @pl.when(e == n_e - 1)
    def _():
        o_ref[...] = res_ref[...] + gate_ref[0] * acc_ref[...].T


def peer_dense(xt_bf16, u_bf16, vt_bf16, m, r1, e0, e1, resid, gate, tokens_per_gate, tt, eb):
    d, t = xt_bf16.shape
    n_exp = u_bf16.shape[0]
    assert t % tt == 0 and n_exp % eb == 0 and tokens_per_gate % tt == 0
    assert eb // PEER_N_KEYS == SUBLANES
    tiles_per_gate = tokens_per_gate // tt
    rblk = pl.BlockSpec((PEER_HEADS, PEER_N_KEYS, tt), lambda i, j: (0, 0, i))
    gblk = pl.BlockSpec((PEER_HEADS, SUBLANES, tt), lambda i, j: (0, j, i))
    return pl.pallas_call(
        _dense_kernel,
        grid=(t // tt, n_exp // eb),
        in_specs=[pl.BlockSpec((d, tt), lambda i, j: (0, i)),
                  pl.BlockSpec((eb, d), lambda i, j: (j, 0)),
                  pl.BlockSpec((1, d, eb), lambda i, j: (j, 0, 0)),
                  gblk, rblk, gblk, rblk,
                  pl.BlockSpec((tt, d), lambda i, j: (i, 0)),
                  pl.BlockSpec((1, 1, d), lambda i, j: (i // tiles_per_gate, 0, 0))],
        out_specs=pl.BlockSpec((tt, d), lambda i, j: (i, 0)),
        out_shape=jax.ShapeDtypeStruct((t, d), jnp.float32),
        scratch_shapes=[pltpu.VMEM((eb, tt), jnp.float32),
                        pltpu.VMEM((eb, tt), jnp.bfloat16),
                        pltpu.VMEM((d, tt), jnp.float32)],
        compiler_params=pltpu.CompilerParams(dimension_semantics=("parallel", "arbitrary"),
                                             vmem_limit_bytes=VMEM_LIMIT_BYTES),
        name="peer_dense",
    )(xt_bf16, u_bf16, vt_bf16, m, r1, e0, e1, resid, gate)


def peer_layer(x, norm_g, mods, w_q, keys, u_bf16, vt_bf16):
    t, d = x.shape
    q, h = mm_norm(x, norm_g, mods[3], mods[4], w_q, emit_h=True)
    gate = mods[5]
    m, r1, e0, e1 = peer_router(q, keys.astype(jnp.bfloat16), PEER_TOKEN_TILE)
    return peer_dense(h.T, u_bf16, vt_bf16, m, r1, e0, e1, x, gate, t // gate.shape[0],
                      PEER_TOKEN_TILE, PEER_EXPERT_TILE)


def kernel(x_prompt, x_sample, c, cache_na_k, cache_na_v, cache_diff_k, cache_diff_v, state_s5_re, state_s5_im, c_ctx, mod_w, mod_b, norm_mix_g, norm_ffn_g, ev_w_in, ev_w_out, na_q_g, na_k_g, na_rpb, hy_conv_w, hy_conv_b, hy_w1, hy_b1, hy_f1, hy_w2, hy_b2, hy_f2, hy_w3, hy_decay, hy_bias, od_w_in, od_w_out, s5_lam_re, s5_lam_im, s5_log_dt, s5_b_re, s5_b_im, s5_c_re, s5_c_im, s5_d, s5_glu_w, s5_glu_b, diff_q_g, diff_k_g, diff_lq1, diff_lk1, diff_lq2, diff_lk2, diff_subln_g, peer_w_q, peer_keys, peer_u, peer_v):
    bp, lp, d = x_prompt.shape
    bs, ls, _ = x_sample.shape
    rope_p, rope_s = rope_tables(lp), rope_tables(ls)
    xp, xs = x_prompt.reshape(bp * lp, d), x_sample.reshape(bs * ls, d)
    new_na_k, new_na_v, new_dk, new_dv, new_sr, new_si = [], [], [], [], [], []
    for i in range(DEPTH):
        mods = adaln(jnp.concatenate([c_ctx[None, :], c], axis=0), mod_w[i], mod_b[i])
        mp = [v[:1] for v in mods]
        ms = [v[1:] for v in mods]
        j = i // 2
        if i % 2 == 0:
            ev = (ev_w_in[j].astype(jnp.bfloat16), ev_w_out[j].astype(jnp.bfloat16), na_q_g[j], na_k_g[j],
                  hy_conv_w[j], hy_conv_b[j],
                  hy_w1[j], hy_b1[j], hy_f1[j], hy_w2[j], hy_b2[j], hy_f2[j], hy_w3[j], hy_decay[j], hy_bias[j])
            bias_tabs = na_bias_tables(na_rpb[j], ls // GRID_W)
            xp, kp, vp = even_mixer(xp, mp, norm_mix_g[i], bp, lp, ev, None, None)
            xs, _, _ = even_mixer(xs, ms, norm_mix_g[i], bs, ls, ev, (cache_na_k[:, j], cache_na_v[:, j]), bias_tabs)
            new_na_k.append(kp)
            new_na_v.append(vp)
        else:
            lam_init = 0.8 - 0.6 * math.exp(-0.3 * i)
            s5_prep = s5_prepare(s5_lam_re[j], s5_lam_im[j], s5_log_dt[j], s5_b_re[j], s5_b_im[j],
                                 s5_c_re[j], s5_c_im[j])
            od = (od_w_in[j].astype(jnp.bfloat16), od_w_out[j].astype(jnp.bfloat16), s5_prep, s5_d[j],
                  s5_glu_w[j], s5_glu_b[j], diff_q_g[j], diff_k_g[j],
                  diff_lq1[j], diff_lk1[j], diff_lq2[j], diff_lk2[j], diff_subln_g[j])
            xp, kp, vp, sr, si = odd_mixer(xp, mp, norm_mix_g[i], bp, lp, od, lam_init, None, None, None, rope_p)
            xs, _, _, _, _ = odd_mixer(xs, ms, norm_mix_g[i], bs, ls, od, lam_init, state_s5_re[:, j],
                                       state_s5_im[:, j], (cache_diff_k[:, j], cache_diff_v[:, j]), rope_s)
            new_dk.append(kp)
            new_dv.append(vp)
            new_sr.append(sr)
            new_si.append(si)
        u_bf16 = peer_u[i].astype(jnp.bfloat16)
        vt_bf16 = jnp.swapaxes(peer_v[i].reshape(-1, PEER_EXPERT_TILE, d), 1, 2).astype(jnp.bfloat16)
        w_q_bf16 = peer_w_q[i].astype(jnp.bfloat16)
        xp = peer_layer(xp, norm_ffn_g[i], mp, w_q_bf16, peer_keys[i], u_bf16, vt_bf16)
        xs = peer_layer(xs, norm_ffn_g[i], ms, w_q_bf16, peer_keys[i], u_bf16, vt_bf16)
    return (xp.reshape(bp, lp, d), xs.reshape(bs, ls, d), jnp.stack(new_na_k, axis=1), jnp.stack(new_na_v, axis=1),
            jnp.stack(new_dk, axis=1), jnp.stack(new_dv, axis=1),
            jnp.stack(new_sr, axis=1), jnp.stack(new_si, axis=1))
```

```python
import functools
import math

import jax
import jax.numpy as jnp
from jax import lax
from jax.experimental import pallas as pl
from jax.experimental.pallas import tpu as pltpu

D_MODEL = 2048
DEPTH = 2
GRID_W = 64
MIX_WIDTH = D_MODEL // 2
HEAD_DIM = 128
NA_HEADS = MIX_WIDTH // HEAD_DIM
NA_KH = 8
NA_KW = 16
HY_WIDTH = MIX_WIDTH
HY_ORDER = 2
HY_POS_EMB = 33
S5_WIDTH = MIX_WIDTH
S5_GROUP = 16
S5_GROUPS = S5_WIDTH // S5_GROUP
S5_STATE = 64
DIFF_HEADS = MIX_WIDTH // HEAD_DIM
DIFF_QK_DIM = HEAD_DIM // 2
DIFF_V_DIM = HEAD_DIM
ROPE_BASE = 10000.0
PEER_HEADS = 8
PEER_N_KEYS = 128
PEER_QUERY_DIM = 256
PEER_TOPK = 16
NORM_EPS = 1e-6
NEG_INF = -1e30

LANES = 128
SUBLANES = 8
VMEM_LIMIT_BYTES = 56 * 1024 * 1024


MM_ROW_TILE = 1024
MM_COL_TILE = 512


def _mm_norm_kernel(x_ref, g_ref, sc_ref, sh_ref, w_ref, *out_refs, emit_h):
    if emit_h:
        o_ref, h_ref, hs_ref = out_refs
    else:
        o_ref, hs_ref = out_refs

    @pl.when(pl.program_id(1) == 0)
    def _():
        x = x_ref[...]
        gain = g_ref[...] * (1.0 + sc_ref[0])
        r = lax.rsqrt(jnp.mean(x * x, axis=-1, keepdims=True) + NORM_EPS)
        hs_ref[...] = ((x * r) * gain + sh_ref[0]).astype(jnp.bfloat16)
        if emit_h:
            h_ref[...] = hs_ref[...]

    o_ref[...] = jnp.dot(hs_ref[...], w_ref[...].astype(jnp.bfloat16), preferred_element_type=jnp.float32)


def mm_norm(x, norm_g, shift, scale, w, emit_h=False):
    m, k = x.shape
    n = w.shape[1]
    tm, tn = min(MM_ROW_TILE, m), min(MM_COL_TILE, n)
    nb = scale.shape[0]
    assert m % tm == 0 and n % tn == 0 and (m // nb) % tm == 0
    tiles_per_mod = (m // nb) // tm
    mod = pl.BlockSpec((1, 1, k), lambda i, j: (i // tiles_per_mod, 0, 0))
    out_specs = [pl.BlockSpec((tm, tn), lambda i, j: (i, j))]
    out_shape = [jax.ShapeDtypeStruct((m, n), jnp.float32)]
    if emit_h:
        out_specs.append(pl.BlockSpec((tm, k), lambda i, j: (i, 0)))
        out_shape.append(jax.ShapeDtypeStruct((m, k), jnp.bfloat16))
    res = pl.pallas_call(
        functools.partial(_mm_norm_kernel, emit_h=emit_h),
        grid=(m // tm, n // tn),
        in_specs=[pl.BlockSpec((tm, k), lambda i, j: (i, 0)),
                  pl.BlockSpec((1, k), lambda i, j: (0, 0)),
                  mod, mod,
                  pl.BlockSpec((k, tn), lambda i, j: (0, j))],
        out_specs=out_specs,
        out_shape=out_shape,
        scratch_shapes=[pltpu.VMEM((tm, k), jnp.bfloat16)],
        compiler_params=pltpu.CompilerParams(dimension_semantics=("parallel", "arbitrary"),
                                             vmem_limit_bytes=VMEM_LIMIT_BYTES),
        name="mm_norm",
    )(x, norm_g.astype(jnp.float32).reshape(1, k), scale, shift, w)
    return res if emit_h else res[0]


def _mm_pair_res_kernel(a_ref, b_ref, wa_ref, wb_ref, r_ref, g_ref, o_ref):
    y = (jnp.dot(a_ref[...].astype(jnp.bfloat16), wa_ref[...].astype(jnp.bfloat16), preferred_element_type=jnp.float32)
         + jnp.dot(b_ref[...].astype(jnp.bfloat16), wb_ref[...].astype(jnp.bfloat16), preferred_element_type=jnp.float32))
    o_ref[...] = r_ref[...] + g_ref[0] * y


def mm_pair_res(a, b, w, resid, gate):
    m, kh = a.shape
    n = w.shape[1]
    tm, tn = min(MM_ROW_TILE, m), min(MM_COL_TILE, n)
    nb = gate.shape[0]
    assert m % tm == 0 and n % tn == 0 and (m // nb) % tm == 0 and w.shape[0] == 2 * kh
    tiles_per_mod = (m // nb) // tm
    return pl.pallas_call(
        _mm_pair_res_kernel,
        grid=(m // tm, n // tn),
        in_specs=[pl.BlockSpec((tm, kh), lambda i, j: (i, 0)),
                  pl.BlockSpec((tm, kh), lambda i, j: (i, 0)),
                  pl.BlockSpec((kh, tn), lambda i, j: (0, j)),
                  pl.BlockSpec((kh, tn), lambda i, j: (1, j)),
                  pl.BlockSpec((tm, tn), lambda i, j: (i, j)),
                  pl.BlockSpec((1, 1, tn), lambda i, j: (i // tiles_per_mod, 0, j))],
        out_specs=pl.BlockSpec((tm, tn), lambda i, j: (i, j)),
        out_shape=jax.ShapeDtypeStruct((m, n), jnp.float32),
        compiler_params=pltpu.CompilerParams(dimension_semantics=("parallel", "arbitrary"),
                                             vmem_limit_bytes=VMEM_LIMIT_BYTES),
        name="mm_pair_res",
    )(a, b, w, w, resid, gate)


def adaln(cond, w, b):
    m = jax.nn.silu(cond) @ w + b
    return jnp.split(m[:, None, :], 6, axis=-1)


def hyena_filters(l, w1, b1, f1, w2, b2, f2, w3, decay):
    t = jnp.linspace(0.0, 1.0, l, dtype=jnp.float32)[:, None]
    bands = (HY_POS_EMB - 1) // 2
    w_ang = 2.0 * math.pi * jnp.arange(l, dtype=jnp.float32)[:, None] / l
    freqs = jnp.linspace(1e-4, bands - 1, bands, dtype=jnp.float32)[None, :]
    z = jnp.concatenate([t, jnp.cos(freqs * w_ang), -jnp.sin(freqs * w_ang)], axis=-1)
    h = jnp.sin(f1 * (z @ w1 + b1))
    h = jnp.sin(f2 * (h @ w2 + b2))
    h = (h @ w3).reshape(l, 2, HY_ORDER, HY_WIDTH).astype(jnp.float32)
    h = h * jnp.exp(-t.reshape(l, 1, 1, 1) * jnp.abs(decay.astype(jnp.float32)))
    h_f, h_b = h[:, 0], h[:, 1]
    zero = jnp.zeros((1, HY_ORDER, HY_WIDTH), jnp.float32)
    return jnp.concatenate([h_f, zero, h_b[1:][::-1]], axis=0)


def _split_bf16(x):
    hi = x.astype(jnp.bfloat16)
    lo = (x - hi.astype(jnp.float32)).astype(jnp.bfloat16)
    return hi, lo


def _dot3(a_hi, a_lo, x):
    m = a_hi.shape[0]
    xh, xl = _split_bf16(x)
    r = jnp.dot(jnp.concatenate([a_hi, a_lo], axis=0), xh, preferred_element_type=jnp.float32)
    return r[:m] + r[m:] + jnp.dot(a_hi, xl, preferred_element_type=jnp.float32)


HY_STEP_ROWS = 1024


def hyena_factors(l):
    n = 2 * l
    n2 = 64 if n >= 8192 else 16
    return n // n2, n2


def hyena_tables(l):
    n = 2 * l
    n1, n2 = hyena_factors(l)
    t = (n2 * jnp.arange(n1 // 2)[None, None, :] + jnp.arange(n2)[:, None, None])
    k1 = jnp.arange(n1)[None, :, None]
    ang = (2.0 * math.pi / n) * ((t * k1) % n).astype(jnp.float32)
    fa = jnp.concatenate([jnp.cos(ang), -jnp.sin(ang)], axis=1)
    fc = jnp.swapaxes(fa, 1, 2) / n
    a2 = (2.0 * math.pi / n2) * ((jnp.arange(n2)[:, None] * jnp.arange(n2)[None, :]) % n2).astype(jnp.float32)
    c, s = jnp.cos(a2), jnp.sin(a2)
    fb = jnp.concatenate([jnp.concatenate([c, s], axis=1), jnp.concatenate([-s, c], axis=1)], axis=0)
    fbi = jnp.concatenate([jnp.concatenate([c, -s], axis=1), jnp.concatenate([s, c], axis=1)], axis=0)
    return tuple(_split_bf16(x) for x in (fa, fc, fb, fbi))


def hyena_spectrum(filt, l):
    n1, n2 = hyena_factors(l)
    kf = jnp.fft.fft(filt, axis=0)
    kf = jnp.stack([jnp.real(kf), jnp.imag(kf)], axis=0).astype(jnp.float32)
    kf = kf.reshape(2, n2, n1, HY_ORDER, HY_WIDTH)
    return jnp.transpose(kf, (3, 0, 2, 1, 4))


def _hy_stage_a_kernel(z_ref, fh_ref, fl_ref, o_ref):
    for i in range(z_ref.shape[0]):
        o_ref[i] = _dot3(fh_ref[0], fl_ref[0], z_ref[i])


def _hy_stage_b_kernel(s_ref, k_ref, fh_ref, fl_ref, gh_ref, gl_ref, o_ref):
    n2 = s_ref.shape[3]
    kr, ki = k_ref[0, 0], k_ref[1, 0]
    for i in range(s_ref.shape[0]):
        y = jnp.concatenate([s_ref[i, 0, 0], s_ref[i, 1, 0]], axis=0)
        z = _dot3(fh_ref[...], fl_ref[...], y)
        zr, zi = z[:n2], z[n2:]
        p = jnp.concatenate([zr * kr - zi * ki, zr * ki + zi * kr], axis=0)
        q = _dot3(gh_ref[...], gl_ref[...], p)
        o_ref[i, 0, 0] = q[:n2]
        o_ref[i, 1, 0] = q[n2:]


def _hy_stage_c_kernel(q_ref, fh_ref, fl_ref, z_ref, g_ref, b_ref, o_ref):
    for i in range(q_ref.shape[0]):
        y = _dot3(fh_ref[0], fl_ref[0], q_ref[i])
        o_ref[i] = g_ref[i] * (y + z_ref[i] * b_ref[...])


def hyena_long_conv(z, z_col, gate, gate_col, bias_o, spec_o, tables, l):
    (fah, fal), (fch, fcl), (fbh, fbl), (fgh, fgl) = tables
    b = z.shape[0]
    c = HY_WIDTH
    n1, n2 = hyena_factors(l)
    h1 = n1 // 2
    bb = max(1, min(b, HY_STEP_ROWS // n1))
    assert b % bb == 0
    cp = pltpu.CompilerParams(dimension_semantics=("parallel", "parallel"), vmem_limit_bytes=VMEM_LIMIT_BYTES)
    zblocks = z.shape[-1] // c
    gblocks = gate.shape[-1] // c
    zv = z.reshape(b, h1, n2 * z.shape[-1])
    gv = gate.reshape(b, h1, n2 * gate.shape[-1])
    s1 = pl.pallas_call(
        _hy_stage_a_kernel,
        grid=(b // bb, n2),
        in_specs=[pl.BlockSpec((bb, h1, c), lambda i, j: (i, 0, j * zblocks + z_col)),
                  pl.BlockSpec((1, 2 * n1, h1), lambda i, j: (j, 0, 0)),
                  pl.BlockSpec((1, 2 * n1, h1), lambda i, j: (j, 0, 0))],
        out_specs=pl.BlockSpec((bb, 2 * n1, c), lambda i, j: (i, 0, j)),
        out_shape=jax.ShapeDtypeStruct((b, 2 * n1, n2 * c), jnp.float32),
        compiler_params=cp, name="hyena_stage_a",
    )(zv, fah, fal)
    s1 = s1.reshape(b, 2, n1, n2, c)
    mat = pl.BlockSpec((2 * n2, 2 * n2), lambda k, i: (0, 0))
    q = pl.pallas_call(
        _hy_stage_b_kernel,
        grid=(n1, b // bb),
        in_specs=[pl.BlockSpec((bb, 2, 1, n2, c), lambda k, i: (i, 0, k, 0, 0)),
                  pl.BlockSpec((2, 1, n2, c), lambda k, i: (0, k, 0, 0)),
                  mat, mat, mat, mat],
        out_specs=pl.BlockSpec((bb, 2, 1, n2, c), lambda k, i: (i, 0, k, 0, 0)),
        out_shape=jax.ShapeDtypeStruct((b, 2, n1, n2, c), jnp.float32),
        compiler_params=cp, name="hyena_stage_b",
    )(s1, spec_o, fbh, fbl, fgh, fgl)
    q = q.reshape(b, 2 * n1, n2 * c)
    out = pl.pallas_call(
        _hy_stage_c_kernel,
        grid=(b // bb, n2),
        in_specs=[pl.BlockSpec((bb, 2 * n1, c), lambda i, j: (i, 0, j)),
                  pl.BlockSpec((1, h1, 2 * n1), lambda i, j: (j, 0, 0)),
                  pl.BlockSpec((1, h1, 2 * n1), lambda i, j: (j, 0, 0)),
                  pl.BlockSpec((bb, h1, c), lambda i, j: (i, 0, j * zblocks + z_col)),
                  pl.BlockSpec((bb, h1, c), lambda i, j: (i, 0, j * gblocks + gate_col)),
                  pl.BlockSpec((1, c), lambda i, j: (0, 0))],
        out_specs=pl.BlockSpec((bb, h1, c), lambda i, j: (i, 0, j)),
        out_shape=jax.ShapeDtypeStruct((b, h1, n2 * c), jnp.float32),
        compiler_params=cp, name="hyena_stage_c",
    )(q, fch, fcl, zv, gv, bias_o.astype(jnp.float32).reshape(1, c))
    return out.reshape(b, l, c)


def hyena(u, conv_w, conv_b, filt, bias):
    b, l, _ = u.shape
    up = jnp.pad(u, ((0, 0), (1, 1), (0, 0)))
    u = up[:, :-2] * conv_w[0] + up[:, 1:-1] * conv_w[1] + up[:, 2:] * conv_w[2] + conv_b
    tables = hyena_tables(l)
    spec = hyena_spectrum(hyena_filters(l, *filt), l)
    z, z_col = u, 0
    for o in range(HY_ORDER):
        z = hyena_long_conv(z, z_col, u, 1 + o, bias[o], spec[o], tables, l)
        z_col = 0
    return z


S5_TILE_GROUPS = 8
S5_TILES = S5_GROUPS // S5_TILE_GROUPS
S5_TILE_IN = S5_TILE_GROUPS * S5_GROUP
S5_TILE_STATE = S5_TILE_GROUPS * S5_STATE
S5_ROWS_PER_STEP = 2048


def s5_prepare(lam_re, lam_im, log_dt, b_re, b_im, c_re, c_im):
    f32 = jnp.float32
    lam_re, lam_im = lam_re.astype(f32), lam_im.astype(f32)
    dt = jnp.exp(log_dt.astype(f32))[..., None]
    mag = jnp.exp(lam_re * dt)
    ar, ai = mag * jnp.cos(lam_im * dt), mag * jnp.sin(lam_im * dt)
    den = lam_re * lam_re + lam_im * lam_im
    cr = ((ar - 1.0) * lam_re + ai * lam_im) / den
    ci = (ai * lam_re - (ar - 1.0) * lam_im) / den
    bbr = cr[..., None] * b_re - ci[..., None] * b_im
    bbi = cr[..., None] * b_im + ci[..., None] * b_re
    eye = jnp.eye(S5_TILE_GROUPS, dtype=f32)

    def tile_in(bb):
        x = bb.reshape(2, S5_TILES, S5_TILE_GROUPS, S5_STATE, S5_GROUP)
        x = jnp.einsum('dtgnp,gh->dtgphn', x, eye)
        return x.reshape(2, S5_TILES, S5_TILE_IN, S5_TILE_STATE)

    def tile_out(cc):
        x = cc.astype(f32).reshape(2, S5_TILES, S5_TILE_GROUPS, S5_GROUP, S5_STATE)
        x = jnp.einsum('dtgpn,gh->dtgnhp', x, eye)
        return x.reshape(2, S5_TILES, S5_TILE_STATE, S5_TILE_IN)

    win = jnp.concatenate([tile_in(bbr), tile_in(bbi)], axis=-1).astype(jnp.bfloat16)
    wout = jnp.concatenate([tile_out(c_re), -tile_out(c_im)], axis=-2).astype(jnp.bfloat16)
    a = jnp.stack([ar.reshape(2, S5_TILES, S5_TILE_STATE), ai.reshape(2, S5_TILES, S5_TILE_STATE)], axis=2)
    return win, wout, a


def _s5_scan_kernel(u_ref, win_ref, wout_ref, a_ref, h0_ref, y_ref, fin_ref, bu_ref, st_ref, *, batch):
    d = pl.program_id(0)
    c = pl.program_id(2)
    n_c = pl.num_programs(2)
    ns = S5_TILE_STATE
    steps = u_ref.shape[0] // batch

    @pl.when(c == 0)
    def _():
        st_ref[...] = h0_ref[0, 0]

    bu_ref[...] = jnp.dot(u_ref[...].astype(jnp.bfloat16), win_ref[0, 0], preferred_element_type=jnp.float32)
    ar = jnp.broadcast_to(a_ref[0, 0, 0:1, :], (batch, ns))
    ai = jnp.broadcast_to(a_ref[0, 0, 1:2, :], (batch, ns))

    def step(t, carry):
        hr, hi = carry
        te = jnp.where(d == 0, t, steps - 1 - t)
        r0 = pl.multiple_of(te * batch, batch)
        nr = ar * hr - ai * hi + bu_ref[pl.ds(r0, batch), 0:ns]
        ni = ar * hi + ai * hr + bu_ref[pl.ds(r0, batch), ns:2 * ns]
        bu_ref[pl.ds(r0, batch), 0:ns] = nr
        bu_ref[pl.ds(r0, batch), ns:2 * ns] = ni
        return nr, ni

    hr, hi = lax.fori_loop(0, steps, step, (st_ref[:, 0:ns], st_ref[:, ns:2 * ns]), unroll=4)
    st_ref[:, 0:ns] = hr
    st_ref[:, ns:2 * ns] = hi
    y_ref[0] = jnp.dot(bu_ref[...].astype(jnp.bfloat16), wout_ref[0, 0], preferred_element_type=jnp.float32)

    @pl.when(c == n_c - 1)
    def _():
        fin_ref[0, 0] = st_ref[...]


def s5_scan(u_tm, win, wout, a, h0, batch):
    rows = u_tm.shape[0]
    r = min(S5_ROWS_PER_STEP, rows)
    assert rows % r == 0 and r % batch == 0
    n_c = rows // r

    def chunk(d, c):
        return c + d * (n_c - 1 - 2 * c)

    return pl.pallas_call(
        functools.partial(_s5_scan_kernel, batch=batch),
        grid=(2, S5_TILES, n_c),
        in_specs=[pl.BlockSpec((r, S5_TILE_IN), lambda d, j, c: (chunk(d, c), j)),
                  pl.BlockSpec((1, 1, S5_TILE_IN, 2 * S5_TILE_STATE), lambda d, j, c: (d, j, 0, 0)),
                  pl.BlockSpec((1, 1, 2 * S5_TILE_STATE, S5_TILE_IN), lambda d, j, c: (d, j, 0, 0)),
                  pl.BlockSpec((1, 1, 2, S5_TILE_STATE), lambda d, j, c: (d, j, 0, 0)),
                  pl.BlockSpec((1, 1, batch, 2 * S5_TILE_STATE), lambda d, j, c: (d, j, 0, 0))],
        out_specs=[pl.BlockSpec((1, r, S5_TILE_IN), lambda d, j, c: (d, chunk(d, c), j)),
                   pl.BlockSpec((1, 1, batch, 2 * S5_TILE_STATE), lambda d, j, c: (d, j, 0, 0))],
        out_shape=[jax.ShapeDtypeStruct((2, rows, S5_WIDTH), jnp.float32),
                   jax.ShapeDtypeStruct((2, S5_TILES, batch, 2 * S5_TILE_STATE), jnp.float32)],
        scratch_shapes=[pltpu.VMEM((r, 2 * S5_TILE_STATE), jnp.float32),
                        pltpu.VMEM((batch, 2 * S5_TILE_STATE), jnp.float32)],
        compiler_params=pltpu.CompilerParams(dimension_semantics=("parallel", "parallel", "arbitrary"),
                                             vmem_limit_bytes=VMEM_LIMIT_BYTES),
        name="s5_scan",
    )(u_tm, win, wout, a, h0)


def _gelu_tanh(x):
    c = math.sqrt(2.0 / math.pi)
    return x * (0.5 + 0.5 * jnp.tanh(x * (c + (c * 0.044715) * (x * x))))


def _s5_glu_kernel(u_ref, y_ref, d_ref, w_ref, b_ref, o_ref):
    y = d_ref[...] * u_ref[...] + y_ref[0] + y_ref[1]
    y = _gelu_tanh(y)
    z = jnp.dot(y.astype(jnp.bfloat16), w_ref[...], preferred_element_type=jnp.float32) + b_ref[...]
    o_ref[...] = y * (1.0 / (1.0 + jnp.exp(-z)))


def s5_glu(u_tm, y, d_skip, glu_w_bf16, glu_b, tr=1024):
    rows, w = u_tm.shape
    tr = min(tr, rows)
    assert rows % tr == 0
    return pl.pallas_call(
        _s5_glu_kernel,
        grid=(rows // tr,),
        in_specs=[pl.BlockSpec((tr, w), lambda i: (i, 0)),
                  pl.BlockSpec((2, tr, w), lambda i: (0, i, 0)),
                  pl.BlockSpec((1, w), lambda i: (0, 0)),
                  pl.BlockSpec((w, w), lambda i: (0, 0)),
                  pl.BlockSpec((1, w), lambda i: (0, 0))],
        out_specs=pl.BlockSpec((tr, w), lambda i: (i, 0)),
        out_shape=jax.ShapeDtypeStruct((rows, w), jnp.float32),
        compiler_params=pltpu.CompilerParams(dimension_semantics=("parallel",),
                                             vmem_limit_bytes=VMEM_LIMIT_BYTES),
        name="s5_glu",
    )(u_tm, y, d_skip.reshape(1, w), glu_w_bf16, glu_b.reshape(1, w))


def s5_mixer(u, h0_re, h0_im, prep, d_skip, glu_w, glu_b):
    win, wout, a = prep
    b, l, w = u.shape
    u_tm = jnp.swapaxes(u, 0, 1).reshape(l * b, w)
    if h0_re is None:
        h0 = jnp.zeros((2, S5_TILES, b, 2 * S5_TILE_STATE), jnp.float32)
    else:
        def tiles(h):
            return jnp.transpose(h.astype(jnp.float32).reshape(b, 2, S5_TILES, S5_TILE_STATE), (1, 2, 0, 3))
        h0 = jnp.concatenate([tiles(h0_re), tiles(h0_im)], axis=-1)
    y, fin = s5_scan(u_tm, win, wout, a, h0, b)
    out_tm = s5_glu(u_tm, y, d_skip, glu_w.astype(jnp.bfloat16), glu_b)
    out = jnp.swapaxes(out_tm.reshape(l, b, w), 0, 1)

    def untile(f):
        return jnp.transpose(f, (2, 0, 1, 3)).reshape(b, 2, S5_GROUPS, S5_STATE)
    return out, untile(fin[..., :S5_TILE_STATE]), untile(fin[..., S5_TILE_STATE:])


def _half_mean_matrix():
    i = lax.broadcasted_iota(jnp.int32, (LANES, LANES), 0) // DIFF_QK_DIM
    j = lax.broadcasted_iota(jnp.int32, (LANES, LANES), 1) // DIFF_QK_DIM
    return jnp.where(i == j, 1.0 / DIFF_QK_DIM, 0.0).astype(jnp.bfloat16)


def _rms_groups(x, gain, avg):
    xx = x * x
    hi = xx.astype(jnp.bfloat16)
    lo = (xx - hi.astype(jnp.float32)).astype(jnp.bfloat16)
    ms = (jnp.dot(hi, avg, preferred_element_type=jnp.float32)
          + jnp.dot(lo, avg, preferred_element_type=jnp.float32))
    return x * lax.rsqrt(ms + NORM_EPS) * gain


def _rope_lanes(y, cos, sin_signed, first_half):
    rot = jnp.where(first_half, pltpu.roll(y, LANES - 16, 1), pltpu.roll(y, 16, 1))
    return y * cos + rot * sin_signed


def _diff_prep_kernel(q_ref, k_ref, qg_ref, kg_ref, cos_ref, sin_ref, qo_ref, ko_ref, kn_ref, *, use_rope):
    avg = _half_mean_matrix()
    lane = lax.broadcasted_iota(jnp.int32, (1, LANES), 1)
    first_half = (lane % 32) < 16
    scale = DIFF_QK_DIM ** -0.5 * math.log2(math.e)
    for h in range(DIFF_HEADS):
        cols = slice(h * LANES, (h + 1) * LANES)
        qn = _rms_groups(q_ref[:, cols], qg_ref[...], avg)
        kn = _rms_groups(k_ref[:, cols], kg_ref[...], avg)
        kn_ref[:, cols] = kn
        if use_rope:
            qn = _rope_lanes(qn, cos_ref[...], sin_ref[...], first_half)
            kn = _rope_lanes(kn, cos_ref[...], sin_ref[...], first_half)
        qo_ref[:, cols] = (qn * scale).astype(jnp.bfloat16)
        ko_ref[:, cols] = kn.astype(jnp.bfloat16)


def diff_prep(p, q_g, k_g, cos, sin_signed, seq_len, use_rope, tr=512):
    rows = p.shape[0]
    tr = min(tr, seq_len)
    assert seq_len % tr == 0
    w = DIFF_HEADS * LANES
    per_seq = seq_len // tr
    g2 = lambda g: jnp.tile(g.astype(jnp.float32), 2).reshape(1, LANES)
    return pl.pallas_call(
        functools.partial(_diff_prep_kernel, use_rope=use_rope),
        grid=(rows // tr,),
        in_specs=[pl.BlockSpec((tr, w), lambda i: (i, 1)),
                  pl.BlockSpec((tr, w), lambda i: (i, 2)),
                  pl.BlockSpec((1, LANES), lambda i: (0, 0)),
                  pl.BlockSpec((1, LANES), lambda i: (0, 0)),
                  pl.BlockSpec((tr, LANES), lambda i: (i % per_seq, 0)),
                  pl.BlockSpec((tr, LANES), lambda i: (i % per_seq, 0))],
        out_specs=[pl.BlockSpec((tr, w), lambda i: (i, 0))] * 3,
        out_shape=[jax.ShapeDtypeStruct((rows, w), jnp.bfloat16), jax.ShapeDtypeStruct((rows, w), jnp.bfloat16),
                   jax.ShapeDtypeStruct((rows, w), jnp.float32)],
        compiler_params=pltpu.CompilerParams(dimension_semantics=("parallel",),
                                             vmem_limit_bytes=VMEM_LIMIT_BYTES),
        name="diff_prep",
    )(p, p, g2(q_g), g2(k_g), cos, sin_signed)


def rope_tables(l):
    t = jnp.arange(l)
    row = (t // GRID_W).astype(jnp.float32)
    col = (t % GRID_W).astype(jnp.float32)
    nf = DIFF_QK_DIM // 4
    inv = ROPE_BASE ** (-jnp.arange(nf, dtype=jnp.float32) / nf)
    ang = jnp.stack([row[:, None] * inv, col[:, None] * inv], axis=1)
    ang = jnp.stack([ang, ang], axis=2).reshape(l, DIFF_QK_DIM)
    sign = jnp.where((jnp.arange(DIFF_QK_DIM) % 32) < 16, -1.0, 1.0)
    return jnp.tile(jnp.cos(ang), (1, 2)), jnp.tile(jnp.sin(ang) * sign, (1, 2))


def _diff_attn_kernel(*refs, has_ctx):
    if has_ctx:
        q_ref, k_ref, v_ref, ck_ref, cv_ref, lam_ref, g_ref, o_ref = refs
    else:
        q_ref, k_ref, v_ref, lam_ref, g_ref, o_ref = refs
    q = q_ref[...]
    lane = lax.broadcasted_iota(jnp.int32, (1, LANES), 1)
    zero = jnp.zeros_like(q)
    qs = (jnp.where(lane < DIFF_QK_DIM, q, zero), jnp.where(lane >= DIFF_QK_DIM, q, zero))
    nt = (((1,), (1,)), ((), ()))
    k = k_ref[...]
    ck = ck_ref[...].astype(jnp.bfloat16) if has_ctx else None
    lam = lam_ref[0:1, 0:1]
    w_self, w_ctx = None, None
    for i in range(2):
        s = lax.dot_general(qs[i], k, nt, preferred_element_type=jnp.float32)
        m = jnp.max(s, axis=-1, keepdims=True)
        if has_ctx:
            sc = lax.dot_general(qs[i], ck, nt, preferred_element_type=jnp.float32)
            m = jnp.maximum(m, jnp.max(sc, axis=-1, keepdims=True))
        p = jnp.exp2(s - m)
        l = jnp.sum(p, axis=-1, keepdims=True)
        if has_ctx:
            pc = jnp.exp2(sc - m)
            l = l + jnp.sum(pc, axis=-1, keepdims=True)
        coef = 1.0 / l if i == 0 else -lam / l
        w_self = p * coef if i == 0 else w_self + p * coef
        if has_ctx:
            w_ctx = pc * coef if i == 0 else w_ctx + pc * coef
    o = jnp.dot(w_self.astype(jnp.bfloat16), v_ref[...].astype(jnp.bfloat16), preferred_element_type=jnp.float32)
    if has_ctx:
        o = o + jnp.dot(w_ctx.astype(jnp.bfloat16), cv_ref[...].astype(jnp.bfloat16),
                        preferred_element_type=jnp.float32)
    ms = jnp.mean(o * o, axis=-1, keepdims=True)
    o_ref[...] = o * lax.rsqrt(ms + NORM_EPS) * g_ref[...]


def diff_attention(q_bf16, k_bf16, p, ctx_k, ctx_v, lam, gain, batch, seq_len, tq=256):
    tq = min(tq, seq_len)
    nq = seq_len // tq
    has_ctx = ctx_k is not None
    v_col0 = 3 * DIFF_HEADS
    in_specs = [pl.BlockSpec((tq, LANES), lambda b, h, i: (b * nq + i, h)),
                pl.BlockSpec((seq_len, LANES), lambda b, h, i: (b, h)),
                pl.BlockSpec((seq_len, LANES), lambda b, h, i: (b, v_col0 + h))]
    args = [q_bf16, k_bf16, p]
    if has_ctx:
        lc = ctx_k.shape[0] // batch
        in_specs += [pl.BlockSpec((lc, LANES), lambda b, h, i: (b, h))] * 2
        args += [ctx_k, ctx_v]
    in_specs += [pl.BlockSpec((1, LANES), lambda b, h, i: (0, 0))] * 2
    args += [jnp.broadcast_to(lam.astype(jnp.float32), (1, LANES)), gain.astype(jnp.float32).reshape(1, LANES)]
    return pl.pallas_call(
        functools.partial(_diff_attn_kernel, has_ctx=has_ctx),
        grid=(batch, DIFF_HEADS, nq),
        in_specs=in_specs,
        out_specs=pl.BlockSpec((tq, LANES), lambda b, h, i: (b * nq + i, h)),
        out_shape=jax.ShapeDtypeStruct((batch * seq_len, DIFF_HEADS * LANES), jnp.float32),
        compiler_params=pltpu.CompilerParams(dimension_semantics=("parallel", "parallel", "arbitrary"),
                                             vmem_limit_bytes=VMEM_LIMIT_BYTES),
        name="diff_attention",
    )(*args)


NA_Q_ROWS = 8
NA_WIN_ROWS = 16


def _na_prep_kernel(q_ref, k_ref, qg_ref, kg_ref, qo_ref, ko_ref, kn_ref):
    avg = jnp.full((LANES, LANES), 1.0 / HEAD_DIM, jnp.bfloat16)
    for h in range(NA_HEADS):
        cols = slice(h * LANES, (h + 1) * LANES)
        qo_ref[:, cols] = _rms_groups(q_ref[:, cols], qg_ref[...], avg).astype(jnp.bfloat16)
        kn = _rms_groups(k_ref[:, cols], kg_ref[...], avg)
        kn_ref[:, cols] = kn
        ko_ref[:, cols] = kn.astype(jnp.bfloat16)


def na_prep(p, q_g, k_g, tr=512):
    rows = p.shape[0]
    tr = min(tr, rows)
    w = NA_HEADS * LANES
    g1 = lambda g: g.astype(jnp.float32).reshape(1, LANES)
    return pl.pallas_call(
        _na_prep_kernel,
        grid=(rows // tr,),
        in_specs=[pl.BlockSpec((tr, w), lambda i: (i, 0)),
                  pl.BlockSpec((tr, w), lambda i: (i, 1)),
                  pl.BlockSpec((1, LANES), lambda i: (0, 0)),
                  pl.BlockSpec((1, LANES), lambda i: (0, 0))],
        out_specs=[pl.BlockSpec((tr, w), lambda i: (i, 0))] * 3,
        out_shape=[jax.ShapeDtypeStruct((rows, w), jnp.bfloat16), jax.ShapeDtypeStruct((rows, w), jnp.bfloat16),
                   jax.ShapeDtypeStruct((rows, w), jnp.float32)],
        compiler_params=pltpu.CompilerParams(dimension_semantics=("parallel",),
                                             vmem_limit_bytes=VMEM_LIMIT_BYTES),
        name="na_prep",
    )(p, p, g1(q_g), g1(k_g))


def na_bias_tables(rpb, rows):
    nblk = rows // NA_Q_ROWS
    cq = jnp.arange(GRID_W)[:, None]
    ck = jnp.arange(GRID_W)[None, :]
    col_start = jnp.clip(cq - NA_KW // 2, 0, GRID_W - NA_KW)
    col_ok = (ck >= col_start) & (ck < col_start + NA_KW)
    cidx = jnp.clip(ck - cq + NA_KW - 1, 0, 2 * NA_KW - 2)
    tabs = []
    for i in (0, 1, nblk - 1):
        base = min(max(NA_Q_ROWS * i - NA_KH // 2, 0), rows - NA_WIN_ROWS)
        r = NA_Q_ROWS * i + jnp.arange(NA_Q_ROWS)[:, None]
        rk = base + jnp.arange(NA_WIN_ROWS)[None, :]
        r0 = jnp.clip(r - NA_KH // 2, 0, rows - NA_KH)
        row_ok = (rk >= r0) & (rk < r0 + NA_KH)
        ridx = jnp.clip(rk - r + NA_KH - 1, 0, 2 * NA_KH - 2)
        b = rpb[:, ridx][:, :, :, cidx]
        ok = row_ok[:, :, None, None] & col_ok[None, None, :, :]
        b = jnp.where(ok[None], b.astype(jnp.float32), NEG_INF)
        tabs.append(jnp.transpose(b, (0, 1, 3, 2, 4)).reshape(NA_HEADS, NA_Q_ROWS * GRID_W, NA_WIN_ROWS * GRID_W))
    return jnp.stack(tabs, axis=0)


def _na_attn_kernel(*refs, windowed, rows):
    nt = (((1,), (1,)), ((), ()))
    scale = HEAD_DIM ** -0.5
    if windowed:
        q_ref, k_ref, v_ref, b_ref, ck_ref, cv_ref, o_ref = refs
        i = pl.program_id(2)
        base = jnp.clip(NA_Q_ROWS * i - NA_KH // 2, 0, rows - NA_WIN_ROWS)
        k0 = pl.multiple_of(base * GRID_W, GRID_W)
        nk = NA_WIN_ROWS * GRID_W
        q = q_ref[...]
        s = lax.dot_general(q, k_ref[pl.ds(k0, nk), :], nt, preferred_element_type=jnp.float32) * scale + b_ref[0, 0]
        sc = lax.dot_general(q, ck_ref[...].astype(jnp.bfloat16), nt, preferred_element_type=jnp.float32) * scale
        m = jnp.maximum(jnp.max(s, axis=-1, keepdims=True), jnp.max(sc, axis=-1, keepdims=True))
        p = jnp.exp(s - m)
        pc = jnp.exp(sc - m)
        l = jnp.sum(p, axis=-1, keepdims=True) + jnp.sum(pc, axis=-1, keepdims=True)
        o = (jnp.dot(p.astype(jnp.bfloat16), v_ref[pl.ds(k0, nk), :].astype(jnp.bfloat16),
                     preferred_element_type=jnp.float32)
             + jnp.dot(pc.astype(jnp.bfloat16), cv_ref[...].astype(jnp.bfloat16), preferred_element_type=jnp.float32))
    else:
        q_ref, k_ref, v_ref, o_ref = refs
        s = lax.dot_general(q_ref[...], k_ref[...], nt, preferred_element_type=jnp.float32) * scale
        m = jnp.max(s, axis=-1, keepdims=True)
        p = jnp.exp(s - m)
        l = jnp.sum(p, axis=-1, keepdims=True)
        o = jnp.dot(p.astype(jnp.bfloat16), v_ref[...].astype(jnp.bfloat16), preferred_element_type=jnp.float32)
    o_ref[...] = o / l


def na_attention(q_bf16, k_bf16, p, bias_tabs, ctx_k, ctx_v, batch, seq_len):
    windowed = bias_tabs is not None
    v_col0 = 2 * NA_HEADS
    rows = seq_len // GRID_W
    if windowed:
        tq = NA_Q_ROWS * GRID_W
        nq = seq_len // tq
        lc = ctx_k.shape[0] // batch
        nk = NA_WIN_ROWS * GRID_W
        in_specs = [pl.BlockSpec((tq, LANES), lambda b, h, i: (b * nq + i, h)),
                    pl.BlockSpec((seq_len, LANES), lambda b, h, i: (b, h)),
                    pl.BlockSpec((seq_len, LANES), lambda b, h, i: (b, v_col0 + h)),
                    pl.BlockSpec((1, 1, tq, nk),
                                 lambda b, h, i: (jnp.where(i == 0, 0, jnp.where(i == nq - 1, 2, 1)), h, 0, 0)),
                    pl.BlockSpec((lc, LANES), lambda b, h, i: (b, h)),
                    pl.BlockSpec((lc, LANES), lambda b, h, i: (b, h))]
        args = [q_bf16, k_bf16, p, bias_tabs, ctx_k, ctx_v]
    else:
        tq = seq_len
        nq = 1
        in_specs = [pl.BlockSpec((tq, LANES), lambda b, h, i: (b, h)),
                    pl.BlockSpec((seq_len, LANES), lambda b, h, i: (b, h)),
                    pl.BlockSpec((seq_len, LANES), lambda b, h, i: (b, v_col0 + h))]
        args = [q_bf16, k_bf16, p]
    return pl.pallas_call(
        functools.partial(_na_attn_kernel, windowed=windowed, rows=rows),
        grid=(batch, NA_HEADS, nq),
        in_specs=in_specs,
        out_specs=pl.BlockSpec((tq, LANES), lambda b, h, i: (b * nq + i, h)),
        out_shape=jax.ShapeDtypeStruct((batch * seq_len, NA_HEADS * LANES), jnp.float32),
        compiler_params=pltpu.CompilerParams(dimension_semantics=("parallel", "parallel", "arbitrary"),
                                             vmem_limit_bytes=VMEM_LIMIT_BYTES),
        name="na_attention",
    )(*args)


def even_mixer(x, mods, norm_g, b, l, ev, ctx_kv, bias_tabs):
    (w_in, w_out, q_g, k_g, conv_w, conv_b, w1, b1, f1, w2, b2, f2, w3, decay, bias) = ev
    p = mm_norm(x, norm_g, mods[0], mods[1], w_in)
    q, k, kn = na_prep(p, q_g, k_g)
    if ctx_kv is None:
        oa = na_attention(q, k, p, None, None, None, b, l)
    else:
        lc = ctx_kv[0].shape[1]
        oa = na_attention(q, k, p, bias_tabs, ctx_kv[0].reshape(b * lc, MIX_WIDTH),
                          ctx_kv[1].reshape(b * lc, MIX_WIDTH), b, l)
    hb = p[:, 3 * MIX_WIDTH:].reshape(b, l, 3 * HY_WIDTH)
    ob = hyena(hb, conv_w, conv_b, (w1, b1, f1, w2, b2, f2, w3, decay), bias)
    out = mm_pair_res(oa, ob.reshape(b * l, HY_WIDTH), w_out, x, mods[2])
    shp = (b, l, NA_HEADS, HEAD_DIM)
    return out, kn.reshape(shp), p[:, 2 * MIX_WIDTH:3 * MIX_WIDTH].reshape(shp)


def odd_mixer(x, mods, norm_g, b, l, od, lam_init, h0_re, h0_im, ctx_kv, rope):
    (w_in, w_out, s5_prep, d_skip, glu_w, glu_b, q_g, k_g, lq1, lk1, lq2, lk2, subln_g) = od
    p = mm_norm(x, norm_g, mods[0], mods[1], w_in)
    oc, fr, fi = s5_mixer(p[:, :S5_WIDTH].reshape(b, l, S5_WIDTH), h0_re, h0_im, s5_prep, d_skip, glu_w, glu_b)
    lam = (jnp.exp(jnp.sum(lq1 * lk1).astype(jnp.float32))
           - jnp.exp(jnp.sum(lq2 * lk2).astype(jnp.float32)) + lam_init)
    cos, sin_signed = rope
    q, k, kn = diff_prep(p, q_g, k_g, cos, sin_signed, l, use_rope=ctx_kv is not None)
    gain = subln_g.astype(jnp.float32) * (1.0 - lam_init)
    if ctx_kv is None:
        o = diff_attention(q, k, p, None, None, lam, gain, b, l)
    else:
        lc = ctx_kv[0].shape[1]
        o = diff_attention(q, k, p, ctx_kv[0].reshape(b * lc, MIX_WIDTH), ctx_kv[1].reshape(b * lc, MIX_WIDTH),
                           lam, gain, b, l)
    out = mm_pair_res(oc.reshape(b * l, S5_WIDTH), o, w_out, x, mods[2])
    kn = kn.reshape(b, l, DIFF_HEADS, 2, DIFF_QK_DIM)
    v = p[:, S5_WIDTH + 2 * MIX_WIDTH:].reshape(b, l, DIFF_HEADS, DIFF_V_DIM)
    return out, kn, v, fr, fi


PEER_TOKEN_TILE = 512
PEER_EXPERT_TILE = 1024
PEER_SUB_EXPERTS = 256


def _top16_rows(cur, iota):
    n = cur.shape[0]
    tops, idxs = [], []
    for _ in range(PEER_TOPK):
        m = jnp.max(cur, axis=0, keepdims=True)
        first = jnp.min(jnp.where(cur == m, iota, n), axis=0, keepdims=True)
        tops.append(m)
        idxs.append(first)
        cur = jnp.where(iota == first, -jnp.inf, cur)
    return tops, idxs


def _router_kernel(q_ref, keys_ref, m_ref, r1_ref, e0_ref, e1_ref, st_ref):
    tt = q_ref.shape[0]
    half = PEER_QUERY_DIM // 2
    for h in range(PEER_HEADS):
        for s in range(2):
            qs = q_ref[:, (2 * h + s) * half:(2 * h + s + 1) * half].astype(jnp.bfloat16)
            st = lax.dot_general(keys_ref[h, s], qs, (((1,), (1,)), ((), ())),
                                 preferred_element_type=jnp.float32)
            st_ref[2 * h + s] = st

    iota128 = lax.broadcasted_iota(jnp.int32, (PEER_N_KEYS, LANES), 0)
    iota8 = lax.broadcasted_iota(jnp.int32, (SUBLANES, LANES), 0)
    iota_cand = lax.broadcasted_iota(jnp.int32, (PEER_TOPK + SUBLANES * SUBLANES, LANES), 0)
    n_chunks = tt // LANES

    def route(h, lane0):
        s0 = st_ref[2 * h, :, pl.ds(lane0, LANES)]
        s1 = st_ref[2 * h + 1, :, pl.ds(lane0, LANES)]
        a, a_idx = _top16_rows(s0, iota128)
        b, b_idx = _top16_rows(s1, iota128)
        bmat = jnp.concatenate(b, axis=0)
        sums = [a[k] + bmat for k in range(PEER_TOPK)]
        cand = [sums[0], sums[1][:SUBLANES]]
        for k in range(2, SUBLANES):
            cand.append(jnp.where(iota8 < PEER_TOPK // (k + 1), sums[k][:SUBLANES], -jnp.inf))
        cand.append(jnp.concatenate(a[SUBLANES:], axis=0) + b[0])
        f, _ = _top16_rows(jnp.concatenate(cand, axis=0), iota_cand)
        tau = f[PEER_TOPK - 1]
        z = jnp.ones_like(f[0])
        for k in range(1, PEER_TOPK):
            z = z + jnp.exp(f[k] - f[0])
        m = jnp.zeros((PEER_N_KEYS, LANES), jnp.float32)
        r1 = jnp.full((PEER_N_KEYS, LANES), float(PEER_TOPK), jnp.float32)
        for k in range(PEER_TOPK):
            m_k = jnp.sum(jnp.where(sums[k] >= tau, 1.0, 0.0), axis=0, keepdims=True)
            m = jnp.where(iota128 == a_idx[k], m_k, m)
            r1 = jnp.where(iota128 == b_idx[k], float(k), r1)
        m_ref[h, :, pl.ds(lane0, LANES)] = m
        r1_ref[h, :, pl.ds(lane0, LANES)] = r1.astype(jnp.bfloat16)
        e0_ref[h, :, pl.ds(lane0, LANES)] = jnp.exp(s0 - a[0]) / z
        e1_ref[h, :, pl.ds(lane0, LANES)] = jnp.exp(s1 - b[0]).astype(jnp.bfloat16)

    def body(h, carry):
        for c in range(n_chunks):
            route(h, c * LANES)
        return carry

    lax.fori_loop(0, PEER_HEADS, body, 0)


def peer_router(q, keys_bf16, tt):
    t = q.shape[0]
    assert t % tt == 0 and tt % LANES == 0
    f32 = jnp.float32
    big = jax.ShapeDtypeStruct((PEER_HEADS, PEER_N_KEYS, t), f32)
    half = jax.ShapeDtypeStruct((PEER_HEADS, PEER_N_KEYS, t), jnp.bfloat16)
    blk = pl.BlockSpec((PEER_HEADS, PEER_N_KEYS, tt), lambda i: (0, 0, i))
    return pl.pallas_call(
        _router_kernel,
        grid=(t // tt,),
        in_specs=[pl.BlockSpec((tt, q.shape[1]), lambda i: (i, 0)),
                  pl.BlockSpec(keys_bf16.shape, lambda i: (0, 0, 0, 0))],
        out_specs=[blk, blk, blk, blk],
        out_shape=[big, half, big, half],
        scratch_shapes=[pltpu.VMEM((2 * PEER_HEADS, PEER_N_KEYS, tt), f32)],
        compiler_params=pltpu.CompilerParams(dimension_semantics=("parallel",),
                                             vmem_limit_bytes=VMEM_LIMIT_BYTES),
        name="peer_router",
    )(q, keys_bf16)


def _dense_kernel(xt_ref, u_ref, vt_ref, m_ref, r1_ref, e0_ref, e1_ref, res_ref, gate_ref,
                  o_ref, at_ref, wt_ref, acc_ref):
    e = pl.program_id(1)
    n_e = pl.num_programs(1)
    eb = u_ref.shape[0]
    tt = xt_ref.shape[1]
    n_chunks = tt // LANES
    n_sub = eb // PEER_SUB_EXPERTS
    i1_per_sub = PEER_SUB_EXPERTS // PEER_N_KEYS

    @pl.when(e == 0)
    def _():
        acc_ref[...] = jnp.zeros_like(acc_ref)

    at_ref[...] = jnp.dot(u_ref[...], xt_ref[...], preferred_element_type=jnp.float32)
    for sb in range(n_sub):
        for c in range(n_chunks):
            lanes = slice(c * LANES, (c + 1) * LANES)
            for k in range(i1_per_sub):
                i1 = sb * i1_per_sub + k
                zero = jnp.zeros((PEER_N_KEYS, LANES), jnp.bfloat16)
                g = zero
                for h in range(PEER_HEADS):
                    mrow = m_ref[h, i1:i1 + 1, lanes].astype(jnp.bfloat16)
                    e0row = e0_ref[h, i1:i1 + 1, lanes].astype(jnp.bfloat16)
                    g = g + jnp.where(r1_ref[h, :, lanes] < mrow, e0row * e1_ref[h, :, lanes], zero)
                r = slice(i1 * PEER_N_KEYS, (i1 + 1) * PEER_N_KEYS)
                wt_ref[r, lanes] = _gelu_tanh(at_ref[r, lanes]).astype(jnp.bfloat16) * g
    acc_ref[...] += jnp.dot(vt_ref[0], wt_ref[...], preferred_element_type=jnp.float32)

    @pl.when(e == n_e - 1)
    def _():
        o_ref[...] = res_ref[...] + gate_ref[0] * acc_ref[...].T


def peer_dense(xt_bf16, u_bf16, vt_bf16, m, r1, e0, e1, resid, gate, tokens_per_gate, tt, eb):
    d, t = xt_bf16.shape
    n_exp = u_bf16.shape[0]
    assert t % tt == 0 and n_exp % eb == 0 and tokens_per_gate % tt == 0
    assert eb // PEER_N_KEYS == SUBLANES
    tiles_per_gate = tokens_per_gate // tt
    rblk = pl.BlockSpec((PEER_HEADS, PEER_N_KEYS, tt), lambda i, j: (0, 0, i))
    gblk = pl.BlockSpec((PEER_HEADS, SUBLANES, tt), lambda i, j: (0, j, i))
    return pl.pallas_call(
        _dense_kernel,
        grid=(t // tt, n_exp // eb),
        in_specs=[pl.BlockSpec((d, tt), lambda i, j: (0, i)),
                  pl.BlockSpec((eb, d), lambda i, j: (j, 0)),
                  pl.BlockSpec((1, d, eb), lambda i, j: (j, 0, 0)),
                  gblk, rblk, gblk, rblk,
                  pl.BlockSpec((tt, d), lambda i, j: (i, 0)),
                  pl.BlockSpec((1, 1, d), lambda i, j: (i // tiles_per_gate, 0, 0))],
        out_specs=pl.BlockSpec((tt, d), lambda i, j: (i, 0)),
        out_shape=jax.ShapeDtypeStruct((t, d), jnp.float32),
        scratch_shapes=[pltpu.VMEM((eb, tt), jnp.float32),
                        pltpu.VMEM((eb, tt), jnp.bfloat16),
                        pltpu.VMEM((d, tt), jnp.float32)],
        compiler_params=pltpu.CompilerParams(dimension_semantics=("parallel", "arbitrary"),
                                             vmem_limit_bytes=VMEM_LIMIT_BYTES),
        name="peer_dense",
    )(xt_bf16, u_bf16, vt_bf16, m, r1, e0, e1, resid, gate)


def peer_layer(x, norm_g, mods, w_q, keys, u_bf16, vt_bf16):
    t, d = x.shape
    q, h = mm_norm(x, norm_g, mods[3], mods[4], w_q, emit_h=True)
    gate = mods[5]
    m, r1, e0, e1 = peer_router(q, keys.astype(jnp.bfloat16), PEER_TOKEN_TILE)
    return peer_dense(h.T, u_bf16, vt_bf16, m, r1, e0, e1, x, gate, t // gate.shape[0],
                      PEER_TOKEN_TILE, PEER_EXPERT_TILE)


def kernel(x_prompt, x_sample, c, cache_na_k, cache_na_v, cache_diff_k, cache_diff_v, state_s5_re, state_s5_im, c_ctx, mod_w, mod_b, norm_mix_g, norm_ffn_g, ev_w_in, ev_w_out, na_q_g, na_k_g, na_rpb, hy_conv_w, hy_conv_b, hy_w1, hy_b1, hy_f1, hy_w2, hy_b2, hy_f2, hy_w3, hy_decay, hy_bias, od_w_in, od_w_out, s5_lam_re, s5_lam_im, s5_log_dt, s5_b_re, s5_b_im, s5_c_re, s5_c_im, s5_d, s5_glu_w, s5_glu_b, diff_q_g, diff_k_g, diff_lq1, diff_lk1, diff_lq2, diff_lk2, diff_subln_g, peer_w_q, peer_keys, peer_u, peer_v):
    bp, lp, d = x_prompt.shape
    bs, ls, _ = x_sample.shape
    rope_p, rope_s = rope_tables(lp), rope_tables(ls)
    xp, xs = x_prompt.reshape(bp * lp, d), x_sample.reshape(bs * ls, d)
    new_na_k, new_na_v, new_dk, new_dv, new_sr, new_si = [], [], [], [], [], []
    for i in range(DEPTH):
        mods = adaln(jnp.concatenate([c_ctx[None, :], c], axis=0), mod_w[i], mod_b[i])
        mp = [v[:1] for v in mods]
        ms = [v[1:] for v in mods]
        j = i // 2
        if i % 2 == 0:
            ev = (ev_w_in[j].astype(jnp.bfloat16), ev_w_out[j].astype(jnp.bfloat16), na_q_g[j], na_k_g[j],
                  hy_conv_w[j], hy_conv_b[j],
                  hy_w1[j], hy_b1[j], hy_f1[j], hy_w2[j], hy_b2[j], hy_f2[j], hy_w3[j], hy_decay[j], hy_bias[j])
            bias_tabs = na_bias_tables(na_rpb[j], ls // GRID_W)
            xp, kp, vp = even_mixer(xp, mp, norm_mix_g[i], bp, lp, ev, None, None)
            xs, _, _ = even_mixer(xs, ms, norm_mix_g[i], bs, ls, ev, (cache_na_k[:, j], cache_na_v[:, j]), bias_tabs)
            new_na_k.append(kp)
            new_na_v.append(vp)
        else:
            lam_init = 0.8 - 0.6 * math.exp(-0.3 * i)
            s5_prep = s5_prepare(s5_lam_re[j], s5_lam_im[j], s5_log_dt[j], s5_b_re[j], s5_b_im[j],
                                 s5_c_re[j], s5_c_im[j])
            od = (od_w_in[j].astype(jnp.bfloat16), od_w_out[j].astype(jnp.bfloat16), s5_prep, s5_d[j],
                  s5_glu_w[j], s5_glu_b[j], diff_q_g[j], diff_k_g[j],
                  diff_lq1[j], diff_lk1[j], diff_lq2[j], diff_lk2[j], diff_subln_g[j])
            xp, kp, vp, sr, si = odd_mixer(xp, mp, norm_mix_g[i], bp, lp, od, lam_init, None, None, None, rope_p)
            xs, _, _, _, _ = odd_mixer(xs, ms, norm_mix_g[i], bs, ls, od, lam_init, state_s5_re[:, j],
                                       state_s5_im[:, j], (cache_diff_k[:, j], cache_diff_v[:, j]), rope_s)
            new_dk.append(kp)
            new_dv.append(vp)
            new_sr.append(sr)
            new_si.append(si)
        u_bf16 = peer_u[i].astype(jnp.bfloat16)
        vt_bf16 = jnp.swapaxes(peer_v[i].reshape(-1, PEER_EXPERT_TILE, d), 1, 2).astype(jnp.bfloat16)
        w_q_bf16 = peer_w_q[i].astype(jnp.bfloat16)
        xp = peer_layer(xp, norm_ffn_g[i], mp, w_q_bf16, peer_keys[i], u_bf16, vt_bf16)
        xs = peer_layer(xs, norm_ffn_g[i], ms, w_q_bf16, peer_keys[i], u_bf16, vt_bf16)
    return (xp.reshape(bp, lp, d), xs.reshape(bs, ls, d), jnp.stack(new_na_k, axis=1), jnp.stack(new_na_v, axis=1),
            jnp.stack(new_dk, axis=1), jnp.stack(new_dv, axis=1),
            jnp.stack(new_sr, axis=1), jnp.stack(new_si, axis=1))
```
